```python
import math
import jax, jax.numpy as jnp
from jax import lax
import numpy as np

D_MODEL = 2048
BATCH = 8
SEQ = 4096
DEPTH = 4

N_META = 16
ATTN_HEADS = 16
ATTN_HEAD_DIM = 64
ATTN_WIDTH = ATTN_HEADS * ATTN_HEAD_DIM
Q_BLOCK = 128
CONV_WIDTH = D_MODEL - ATTN_WIDTH
CONV_K = 3
PROJ_WIDTH = 3 * ATTN_WIDTH + ATTN_HEADS + 3 * CONV_WIDTH
S5_GROUP = 16
S5_GROUPS = D_MODEL // S5_GROUP
S5_STATE = 64
S5_MIN_DECAY = 1e-4
S5_DT_MIN = 1e-3
S5_DT_MAX = 1e-1
FFN_HIDDEN = int(math.ceil(8 * D_MODEL / 3 / 256) * 256)
NORM_EPS = 1e-6
N_EVEN = (DEPTH + 1) // 2
N_ODD = DEPTH // 2

kernel_name = "fox_shortconv_s5_hybrid_trunk"


def rms_norm(x, g):
    xf = x.astype(jnp.float32)
    y = xf * lax.rsqrt(jnp.mean(xf * xf, axis=-1, keepdims=True) + NORM_EPS)
    return (y * g.astype(jnp.float32)).astype(x.dtype)


def fox_attention(q, k, v, log_f):
    L = q.shape[1]
    seq_real = L - N_META
    c = jnp.transpose(jnp.cumsum(log_f, axis=1), (0, 2, 1))
    scale = ATTN_HEAD_DIM ** -0.5
    blocks = [(0, N_META)] + [(N_META + i * Q_BLOCK, Q_BLOCK) for i in range(seq_real // Q_BLOCK)]
    outs = []
    for start, size in blocks:
        end = start + size
        qb = q[:, start:end]
        kb = k[:, :end]
        vb = v[:, :end]
        s = jnp.einsum('bqhd,bkhd->bhqk', qb, kb).astype(jnp.float32) * scale
        s = s + c[:, :, start:end][..., None] - c[:, :, None, :end]
        qpos = jnp.arange(start, end)[:, None]
        kpos = jnp.arange(end)[None, :]
        s = jnp.where(kpos <= qpos, s, -jnp.inf)
        p = jax.nn.softmax(s, axis=-1).astype(v.dtype)
        outs.append(jnp.einsum('bhqk,bkhd->bqhd', p, vb))
    return jnp.concatenate(outs, axis=1)


def attn_conv_mixer(h, w_in, b_f, conv_w, conv_b, w_o):
    Bsz, L, _ = h.shape
    proj = h @ w_in
    cuts = [ATTN_WIDTH, 2 * ATTN_WIDTH, 3 * ATTN_WIDTH, 3 * ATTN_WIDTH + ATTN_HEADS,
            3 * ATTN_WIDTH + ATTN_HEADS + CONV_WIDTH, 3 * ATTN_WIDTH + ATTN_HEADS + 2 * CONV_WIDTH]
    q, k, v, fg, gate_b, gate_c, xc = jnp.split(proj, cuts, axis=-1)
    log_f = jax.nn.log_sigmoid((fg + b_f).astype(jnp.float32))
    hshape = (Bsz, L, ATTN_HEADS, ATTN_HEAD_DIM)
    attn = fox_attention(q.reshape(hshape), k.reshape(hshape), v.reshape(hshape), log_f)
    attn = attn.reshape(Bsz, L, ATTN_WIDTH)
    z = gate_c * xc
    zp = jnp.pad(z, ((0, 0), (CONV_K - 1, 0), (0, 0)))
    conv = sum(conv_w[j] * zp[:, j:j + L] for j in range(CONV_K)) + conv_b
    conv_out = gate_b * conv
    return jnp.concatenate([attn, conv_out], axis=-1) @ w_o


def _complex_affine_combine(e1, e2):
    a1r, a1i, b1r, b1i = e1
    a2r, a2i, b2r, b2i = e2
    ar = a2r * a1r - a2i * a1i
    ai = a2r * a1i + a2i * a1r
    br = a2r * b1r - a2i * b1i + b2r
    bi = a2r * b1i + a2i * b1r + b2i
    return (ar, ai, br, bi)


def s5_mixer(u, a_re, a_im, log_step, b_re, b_im, c_re, c_im, d_skip, w_glu1, w_glu2):
    Bsz, L, _ = u.shape
    f32 = jnp.float32
    lam_re = jnp.minimum(a_re.astype(f32), -S5_MIN_DECAY)
    lam_im = a_im.astype(f32)
    delta = jnp.exp(log_step.astype(f32))[:, None]
    mag = jnp.exp(lam_re * delta)
    ang = lam_im * delta
    lb_re = mag * jnp.cos(ang)
    lb_im = mag * jnp.sin(ang)
    den = lam_re * lam_re + lam_im * lam_im
    nr = lb_re - 1.0
    ni = lb_im
    coef_re = (nr * lam_re + ni * lam_im) / den
    coef_im = (ni * lam_re - nr * lam_im) / den
    br_ = b_re.astype(f32)
    bi_ = b_im.astype(f32)
    bb_re = coef_re[..., None] * br_ - coef_im[..., None] * bi_
    bb_im = coef_re[..., None] * bi_ + coef_im[..., None] * br_
    uf = u.astype(f32)
    ug = uf.reshape(Bsz, L, S5_GROUPS, S5_GROUP)
    bu_re = jnp.einsum('blgh,gph->blgp', ug, bb_re)
    bu_im = jnp.einsum('blgh,gph->blgp', ug, bb_im)
    a_r = jnp.broadcast_to(lb_re, bu_re.shape)
    a_i = jnp.broadcast_to(lb_im, bu_re.shape)
    _, _, x_re, x_im = lax.associative_scan(_complex_affine_combine, (a_r, a_i, bu_re, bu_im), axis=1)
    y = (jnp.einsum('blgp,ghp->blgh', x_re, c_re.astype(f32))
         - jnp.einsum('blgp,ghp->blgh', x_im, c_im.astype(f32)))
    y = y.reshape(Bsz, L, D_MODEL) + d_skip.astype(f32) * uf
    g = jax.nn.gelu(y).astype(u.dtype)
    return (g @ w_glu1) * jax.nn.sigmoid(g @ w_glu2)


def swiglu_ffn(h, w_gate, w_up, w_down):
    return (jax.nn.silu(h @ w_gate) * (h @ w_up)) @ w_down


def _fwd_setup_inputs(seed: int = 0) -> dict:
    key = jax.random.key(seed)
    ks = jax.random.split(key, 24)
    f32 = jnp.float32
    D = D_MODEL

    def nrm(k, shape, scale):
        return jax.random.normal(k, shape, f32) * scale

    x = nrm(ks[0], (BATCH, SEQ, D), 1.0)
    meta_tokens = nrm(ks[1], (N_META, D), 1.0)
    norm_g = 1.0 + nrm(ks[2], (DEPTH, 4, D), 0.02)
    ab_w_in = nrm(ks[3], (N_EVEN, D, PROJ_WIDTH), D ** -0.5)
    ab_b_f = 3.0 + nrm(ks[4], (N_EVEN, ATTN_HEADS), 0.5)
    ab_conv_w = nrm(ks[5], (N_EVEN, CONV_K, CONV_WIDTH), CONV_K ** -0.5)
    ab_conv_b = nrm(ks[6], (N_EVEN, CONV_WIDTH), 0.02)
    ab_w_o = nrm(ks[7], (N_EVEN, D, D), D ** -0.5)
    s5_a_re = -0.5 + nrm(ks[8], (N_ODD, S5_GROUPS, S5_STATE), 0.01)
    s5_a_im = (jnp.pi * jnp.arange(S5_STATE, dtype=f32))[None, None, :] + nrm(ks[9], (N_ODD, S5_GROUPS, S5_STATE), 0.01)
    s5_log_step = jax.random.uniform(ks[10], (N_ODD, S5_GROUPS), f32, math.log(S5_DT_MIN), math.log(S5_DT_MAX))
    s5_b_re = nrm(ks[11], (N_ODD, S5_GROUPS, S5_STATE, S5_GROUP), (2 * S5_GROUP) ** -0.5)
    s5_b_im = nrm(ks[12], (N_ODD, S5_GROUPS, S5_STATE, S5_GROUP), (2 * S5_GROUP) ** -0.5)
    s5_c_re = nrm(ks[13], (N_ODD, S5_GROUPS, S5_GROUP, S5_STATE), (2 * S5_STATE) ** -0.5)
    s5_c_im = nrm(ks[14], (N_ODD, S5_GROUPS, S5_GROUP, S5_STATE), (2 * S5_STATE) ** -0.5)
    s5_d = nrm(ks[15], (N_ODD, D), 1.0)
    s5_w_glu1 = nrm(ks[16], (N_ODD, D, D), D ** -0.5)
    s5_w_glu2 = nrm(ks[17], (N_ODD, D, D), D ** -0.5)
    ffn_w_gate = nrm(ks[18], (DEPTH, D, FFN_HIDDEN), D ** -0.5)
    ffn_w_up = nrm(ks[19], (DEPTH, D, FFN_HIDDEN), D ** -0.5)
    ffn_w_down = nrm(ks[20], (DEPTH, FFN_HIDDEN, D), FFN_HIDDEN ** -0.5)
    return {"x": x, "meta_tokens": meta_tokens, "norm_g": norm_g,
            "ab_w_in": ab_w_in, "ab_b_f": ab_b_f, "ab_conv_w": ab_conv_w, "ab_conv_b": ab_conv_b, "ab_w_o": ab_w_o,
            "s5_a_re": s5_a_re, "s5_a_im": s5_a_im, "s5_log_step": s5_log_step,
            "s5_b_re": s5_b_re, "s5_b_im": s5_b_im, "s5_c_re": s5_c_re, "s5_c_im": s5_c_im,
            "s5_d": s5_d, "s5_w_glu1": s5_w_glu1, "s5_w_glu2": s5_w_glu2,
            "ffn_w_gate": ffn_w_gate, "ffn_w_up": ffn_w_up, "ffn_w_down": ffn_w_down}


def _fwd_reference(x, meta_tokens, norm_g, ab_w_in, ab_b_f, ab_conv_w, ab_conv_b, ab_w_o,
              s5_a_re, s5_a_im, s5_log_step, s5_b_re, s5_b_im, s5_c_re, s5_c_im,
              s5_d, s5_w_glu1, s5_w_glu2, ffn_w_gate, ffn_w_up, ffn_w_down):
    Bsz = x.shape[0]
    meta = jnp.broadcast_to(meta_tokens.astype(x.dtype)[None], (Bsz, N_META, D_MODEL))
    h = jnp.concatenate([meta, x], axis=1)
    for i in range(DEPTH):
        u = rms_norm(h, norm_g[i, 0])
        if i % 2 == 0:
            j = i // 2
            m = attn_conv_mixer(u, ab_w_in[j], ab_b_f[j], ab_conv_w[j], ab_conv_b[j], ab_w_o[j])
        else:
            j = i // 2
            m = s5_mixer(u, s5_a_re[j], s5_a_im[j], s5_log_step[j], s5_b_re[j], s5_b_im[j],
                         s5_c_re[j], s5_c_im[j], s5_d[j], s5_w_glu1[j], s5_w_glu2[j])
        h = h + rms_norm(m, norm_g[i, 1])
        f = swiglu_ffn(rms_norm(h, norm_g[i, 2]), ffn_w_gate[i], ffn_w_up[i], ffn_w_down[i])
        h = h + rms_norm(f, norm_g[i, 3])
    return h[:, N_META:]


import jax as _jax
import jax.numpy as _jnp

TWIN_FORMAT = 'train_step'
FWD_PARAMS = ['x', 'meta_tokens', 'norm_g', 'ab_w_in', 'ab_b_f', 'ab_conv_w', 'ab_conv_b', 'ab_w_o', 's5_a_re', 's5_a_im', 's5_log_step', 's5_b_re', 's5_b_im', 's5_c_re', 's5_c_im', 's5_d', 's5_w_glu1', 's5_w_glu2', 'ffn_w_gate', 'ffn_w_up', 'ffn_w_down']
TWIN_WEIGHTS = ['meta_tokens', 'norm_g', 'ab_w_in', 'ab_b_f', 'ab_conv_w', 'ab_conv_b', 'ab_w_o', 's5_a_re', 's5_a_im', 's5_log_step', 's5_b_re', 's5_b_im', 's5_c_re', 's5_c_im', 's5_d', 's5_w_glu1', 's5_w_glu2', 'ffn_w_gate', 'ffn_w_up', 'ffn_w_down']
TWIN_DIFF_INPUT = 'x'
TWIN_INPUTS = ['x', 'meta_tokens', 'norm_g', 'ab_w_in', 'ab_b_f', 'ab_conv_w', 'ab_conv_b', 'ab_w_o', 's5_a_re', 's5_a_im', 's5_log_step', 's5_b_re', 's5_b_im', 's5_c_re', 's5_c_im', 's5_d', 's5_w_glu1', 's5_w_glu2', 'ffn_w_gate', 'ffn_w_up', 'ffn_w_down', 'loss_target', 'm_meta_tokens', 'm_norm_g', 'm_ab_w_in', 'm_ab_b_f', 'm_ab_conv_w', 'm_ab_conv_b', 'm_ab_w_o', 'm_s5_a_re', 'm_s5_a_im', 'm_s5_log_step', 'm_s5_b_re', 'm_s5_b_im', 'm_s5_c_re', 'm_s5_c_im', 'm_s5_d', 'm_s5_w_glu1', 'm_s5_w_glu2', 'm_ffn_w_gate', 'm_ffn_w_up', 'm_ffn_w_down', 'v_meta_tokens', 'v_norm_g', 'v_ab_w_in', 'v_ab_b_f', 'v_ab_conv_w', 'v_ab_conv_b', 'v_ab_w_o', 'v_s5_a_re', 'v_s5_a_im', 'v_s5_log_step', 'v_s5_b_re', 'v_s5_b_im', 'v_s5_c_re', 'v_s5_c_im', 'v_s5_d', 'v_s5_w_glu1', 'v_s5_w_glu2', 'v_ffn_w_gate', 'v_ffn_w_up', 'v_ffn_w_down']
TWIN_OUTPUTS = ['loss', 'grad_x', 'grad_meta_tokens', 'grad_norm_g', 'grad_ab_w_in', 'grad_ab_b_f', 'grad_ab_conv_w', 'grad_ab_conv_b', 'grad_ab_w_o', 'grad_s5_a_re', 'grad_s5_a_im', 'grad_s5_log_step', 'grad_s5_b_re', 'grad_s5_b_im', 'grad_s5_c_re', 'grad_s5_c_im', 'grad_s5_d', 'grad_s5_w_glu1', 'grad_s5_w_glu2', 'grad_ffn_w_gate', 'grad_ffn_w_up', 'grad_ffn_w_down', 'delta_meta_tokens', 'delta_norm_g', 'delta_ab_w_in', 'delta_ab_b_f', 'delta_ab_conv_w', 'delta_ab_conv_b', 'delta_ab_w_o', 'delta_s5_a_re', 'delta_s5_a_im', 'delta_s5_log_step', 'delta_s5_b_re', 'delta_s5_b_im', 'delta_s5_c_re', 'delta_s5_c_im', 'delta_s5_d', 'delta_s5_w_glu1', 'delta_s5_w_glu2', 'delta_ffn_w_gate', 'delta_ffn_w_up', 'delta_ffn_w_down', 'new_m_meta_tokens', 'new_m_norm_g', 'new_m_ab_w_in', 'new_m_ab_b_f', 'new_m_ab_conv_w', 'new_m_ab_conv_b', 'new_m_ab_w_o', 'new_m_s5_a_re', 'new_m_s5_a_im', 'new_m_s5_log_step', 'new_m_s5_b_re', 'new_m_s5_b_im', 'new_m_s5_c_re', 'new_m_s5_c_im', 'new_m_s5_d', 'new_m_s5_w_glu1', 'new_m_s5_w_glu2', 'new_m_ffn_w_gate', 'new_m_ffn_w_up', 'new_m_ffn_w_down', 'new_v_meta_tokens', 'new_v_norm_g', 'new_v_ab_w_in', 'new_v_ab_b_f', 'new_v_ab_conv_w', 'new_v_ab_conv_b', 'new_v_ab_w_o', 'new_v_s5_a_re', 'new_v_s5_a_im', 'new_v_s5_log_step', 'new_v_s5_b_re', 'new_v_s5_b_im', 'new_v_s5_c_re', 'new_v_s5_c_im', 'new_v_s5_d', 'new_v_s5_w_glu1', 'new_v_s5_w_glu2', 'new_v_ffn_w_gate', 'new_v_ffn_w_up', 'new_v_ffn_w_down']
TWIN_LEAF_KINDS = {'loss': 'loss', 'grad_x': 'grad_x', 'grad_meta_tokens': 'grad_w', 'grad_norm_g': 'grad_w', 'grad_ab_w_in': 'grad_w', 'grad_ab_b_f': 'grad_w', 'grad_ab_conv_w': 'grad_w', 'grad_ab_conv_b': 'grad_w', 'grad_ab_w_o': 'grad_w', 'grad_s5_a_re': 'grad_w', 'grad_s5_a_im': 'grad_w', 'grad_s5_log_step': 'grad_w', 'grad_s5_b_re': 'grad_w', 'grad_s5_b_im': 'grad_w', 'grad_s5_c_re': 'grad_w', 'grad_s5_c_im': 'grad_w', 'grad_s5_d': 'grad_w', 'grad_s5_w_glu1': 'grad_w', 'grad_s5_w_glu2': 'grad_w', 'grad_ffn_w_gate': 'grad_w', 'grad_ffn_w_up': 'grad_w', 'grad_ffn_w_down': 'grad_w', 'delta_meta_tokens': 'delta_w', 'delta_norm_g': 'delta_w', 'delta_ab_w_in': 'delta_w', 'delta_ab_b_f': 'delta_w', 'delta_ab_conv_w': 'delta_w', 'delta_ab_conv_b': 'delta_w', 'delta_ab_w_o': 'delta_w', 'delta_s5_a_re': 'delta_w', 'delta_s5_a_im': 'delta_w', 'delta_s5_log_step': 'delta_w', 'delta_s5_b_re': 'delta_w', 'delta_s5_b_im': 'delta_w', 'delta_s5_c_re': 'delta_w', 'delta_s5_c_im': 'delta_w', 'delta_s5_d': 'delta_w', 'delta_s5_w_glu1': 'delta_w', 'delta_s5_w_glu2': 'delta_w', 'delta_ffn_w_gate': 'delta_w', 'delta_ffn_w_up': 'delta_w', 'delta_ffn_w_down': 'delta_w', 'new_m_meta_tokens': 'new_m', 'new_m_norm_g': 'new_m', 'new_m_ab_w_in': 'new_m', 'new_m_ab_b_f': 'new_m', 'new_m_ab_conv_w': 'new_m', 'new_m_ab_conv_b': 'new_m', 'new_m_ab_w_o': 'new_m', 'new_m_s5_a_re': 'new_m', 'new_m_s5_a_im': 'new_m', 'new_m_s5_log_step': 'new_m', 'new_m_s5_b_re': 'new_m', 'new_m_s5_b_im': 'new_m', 'new_m_s5_c_re': 'new_m', 'new_m_s5_c_im': 'new_m', 'new_m_s5_d': 'new_m', 'new_m_s5_w_glu1': 'new_m', 'new_m_s5_w_glu2': 'new_m', 'new_m_ffn_w_gate': 'new_m', 'new_m_ffn_w_up': 'new_m', 'new_m_ffn_w_down': 'new_m', 'new_v_meta_tokens': 'new_v', 'new_v_norm_g': 'new_v', 'new_v_ab_w_in': 'new_v', 'new_v_ab_b_f': 'new_v', 'new_v_ab_conv_w': 'new_v', 'new_v_ab_conv_b': 'new_v', 'new_v_ab_w_o': 'new_v', 'new_v_s5_a_re': 'new_v', 'new_v_s5_a_im': 'new_v', 'new_v_s5_log_step': 'new_v', 'new_v_s5_b_re': 'new_v', 'new_v_s5_b_im': 'new_v', 'new_v_s5_c_re': 'new_v', 'new_v_s5_c_im': 'new_v', 'new_v_s5_d': 'new_v', 'new_v_s5_w_glu1': 'new_v', 'new_v_s5_w_glu2': 'new_v', 'new_v_ffn_w_gate': 'new_v', 'new_v_ffn_w_up': 'new_v', 'new_v_ffn_w_down': 'new_v'}


def _forward(args):
    return _fwd_reference(*[args[k] for k in FWD_PARAMS])


def _output_shape():
    def fwd():
        inp = _fwd_setup_inputs(0)
        return _fwd_reference(*[inp[k] for k in FWD_PARAMS])
    out = _jax.eval_shape(fwd)
    return out.shape, out.dtype

N_MICROBATCH = 1
ADAM_LR = 0.001
ADAM_B1 = 0.9
ADAM_B2 = 0.999
ADAM_EPS = 1e-08
ADAM_WD = 0.01
ADAM_STEP = 10
PER_EXAMPLE_BATCH_AXIS = {'x': 0, 'loss_target': 0}
SHARED_INPUTS = []
_WEIGHT_DTYPES = {'meta_tokens': _jnp.float32, 'norm_g': _jnp.float32, 'ab_w_in': _jnp.float32, 'ab_b_f': _jnp.float32, 'ab_conv_w': _jnp.float32, 'ab_conv_b': _jnp.float32, 'ab_w_o': _jnp.float32, 's5_a_re': _jnp.float32, 's5_a_im': _jnp.float32, 's5_log_step': _jnp.float32, 's5_b_re': _jnp.float32, 's5_b_im': _jnp.float32, 's5_c_re': _jnp.float32, 's5_c_im': _jnp.float32, 's5_d': _jnp.float32, 's5_w_glu1': _jnp.float32, 's5_w_glu2': _jnp.float32, 'ffn_w_gate': _jnp.float32, 'ffn_w_up': _jnp.float32, 'ffn_w_down': _jnp.float32}
MOMENT_SCALE = {'meta_tokens': 3.949505e-02, 'norm_g': 1.181918e+01, 'ab_w_in': 9.876318e-01, 'ab_b_f': 1.824932e+00, 'ab_conv_w': 1.073247e+00, 'ab_conv_b': 2.205987e+00, 'ab_w_o': 1.479740e+00, 's5_a_re': 9.586114e-02, 's5_a_im': 8.240109e-02, 's5_log_step': 1.369483e+01, 's5_b_re': 7.324021e-02, 's5_b_im': 7.131106e-02, 's5_c_re': 1.464071e-01, 's5_c_im': 1.442758e-01, 's5_d': 5.000998e+00, 's5_w_glu1': 4.651894e+00, 's5_w_glu2': 6.813802e-01, 'ffn_w_gate': 3.865095e-01, 'ffn_w_up': 5.647953e-01, 'ffn_w_down': 9.552156e-01}


def _to_microbatches(a, axis):
    t = _jnp.moveaxis(a, axis, 0)
    t = t.reshape((N_MICROBATCH, t.shape[0] // N_MICROBATCH) + t.shape[1:])
    return _jnp.moveaxis(t, 1, axis + 1)


def setup_inputs(seed: int = 0) -> dict:
    inp = _fwd_setup_inputs(seed)
    key = _jax.random.fold_in(_jax.random.key(seed), 7919)
    shape, _ = _output_shape()
    out = dict(inp)
    out["loss_target"] = _jax.random.normal(_jax.random.fold_in(key, 0), shape, _jnp.float32)
    for i, name in enumerate(TWIN_WEIGHTS):
        w = inp[name].astype(_jnp.float32)
        if MOMENT_SCALE is None:
            s = _jnp.sqrt(_jnp.mean(_jnp.square(w)) + 1e-30)
        else:
            s = MOMENT_SCALE[name]
        km, kv = _jax.random.split(_jax.random.fold_in(key, i + 1))
        out[name] = w
        out["m_" + name] = s * _jax.random.normal(km, w.shape, _jnp.float32)
        out["v_" + name] = (s * s) * _jax.random.uniform(kv, w.shape, _jnp.float32, 0.5, 1.5)
    if N_MICROBATCH > 1:
        for name, axis in PER_EXAMPLE_BATCH_AXIS.items():
            out[name] = _to_microbatches(out[name], axis)
    return {'x': out['x'], 'meta_tokens': out['meta_tokens'], 'norm_g': out['norm_g'], 'ab_w_in': out['ab_w_in'], 'ab_b_f': out['ab_b_f'], 'ab_conv_w': out['ab_conv_w'], 'ab_conv_b': out['ab_conv_b'], 'ab_w_o': out['ab_w_o'], 's5_a_re': out['s5_a_re'], 's5_a_im': out['s5_a_im'], 's5_log_step': out['s5_log_step'], 's5_b_re': out['s5_b_re'], 's5_b_im': out['s5_b_im'], 's5_c_re': out['s5_c_re'], 's5_c_im': out['s5_c_im'], 's5_d': out['s5_d'], 's5_w_glu1': out['s5_w_glu1'], 's5_w_glu2': out['s5_w_glu2'], 'ffn_w_gate': out['ffn_w_gate'], 'ffn_w_up': out['ffn_w_up'], 'ffn_w_down': out['ffn_w_down'], 'loss_target': out['loss_target'], 'm_meta_tokens': out['m_meta_tokens'], 'm_norm_g': out['m_norm_g'], 'm_ab_w_in': out['m_ab_w_in'], 'm_ab_b_f': out['m_ab_b_f'], 'm_ab_conv_w': out['m_ab_conv_w'], 'm_ab_conv_b': out['m_ab_conv_b'], 'm_ab_w_o': out['m_ab_w_o'], 'm_s5_a_re': out['m_s5_a_re'], 'm_s5_a_im': out['m_s5_a_im'], 'm_s5_log_step': out['m_s5_log_step'], 'm_s5_b_re': out['m_s5_b_re'], 'm_s5_b_im': out['m_s5_b_im'], 'm_s5_c_re': out['m_s5_c_re'], 'm_s5_c_im': out['m_s5_c_im'], 'm_s5_d': out['m_s5_d'], 'm_s5_w_glu1': out['m_s5_w_glu1'], 'm_s5_w_glu2': out['m_s5_w_glu2'], 'm_ffn_w_gate': out['m_ffn_w_gate'], 'm_ffn_w_up': out['m_ffn_w_up'], 'm_ffn_w_down': out['m_ffn_w_down'], 'v_meta_tokens': out['v_meta_tokens'], 'v_norm_g': out['v_norm_g'], 'v_ab_w_in': out['v_ab_w_in'], 'v_ab_b_f': out['v_ab_b_f'], 'v_ab_conv_w': out['v_ab_conv_w'], 'v_ab_conv_b': out['v_ab_conv_b'], 'v_ab_w_o': out['v_ab_w_o'], 'v_s5_a_re': out['v_s5_a_re'], 'v_s5_a_im': out['v_s5_a_im'], 'v_s5_log_step': out['v_s5_log_step'], 'v_s5_b_re': out['v_s5_b_re'], 'v_s5_b_im': out['v_s5_b_im'], 'v_s5_c_re': out['v_s5_c_re'], 'v_s5_c_im': out['v_s5_c_im'], 'v_s5_d': out['v_s5_d'], 'v_s5_w_glu1': out['v_s5_w_glu1'], 'v_s5_w_glu2': out['v_s5_w_glu2'], 'v_ffn_w_gate': out['v_ffn_w_gate'], 'v_ffn_w_up': out['v_ffn_w_up'], 'v_ffn_w_down': out['v_ffn_w_down']}


def _loss(weights, diff, rest, loss_target):
    with _jax.named_scope("forward"):
        args = {**rest, TWIN_DIFF_INPUT: diff, **{k: w.astype(_WEIGHT_DTYPES[k]) for k, w in weights.items()}}
        y = _forward(args)
    with _jax.named_scope("loss_head"):
        err = _jnp.square(y.astype(_jnp.float32) - loss_target)
        return 0.5 * _jnp.sum(_jnp.mean(err, axis=-1)) if err.ndim else 0.5 * err


def _adamw(w, g, m, v):
    m = ADAM_B1 * m + (1.0 - ADAM_B1) * g
    v = ADAM_B2 * v + (1.0 - ADAM_B2) * _jnp.square(g)
    m_hat = m / (1.0 - ADAM_B1 ** ADAM_STEP)
    v_hat = v / (1.0 - ADAM_B2 ** ADAM_STEP)
    delta = -ADAM_LR * (m_hat / (_jnp.sqrt(v_hat) + ADAM_EPS) + ADAM_WD * w)
    return delta, m, v


def reference(x, meta_tokens, norm_g, ab_w_in, ab_b_f, ab_conv_w, ab_conv_b, ab_w_o, s5_a_re, s5_a_im, s5_log_step, s5_b_re, s5_b_im, s5_c_re, s5_c_im, s5_d, s5_w_glu1, s5_w_glu2, ffn_w_gate, ffn_w_up, ffn_w_down, loss_target, m_meta_tokens, m_norm_g, m_ab_w_in, m_ab_b_f, m_ab_conv_w, m_ab_conv_b, m_ab_w_o, m_s5_a_re, m_s5_a_im, m_s5_log_step, m_s5_b_re, m_s5_b_im, m_s5_c_re, m_s5_c_im, m_s5_d, m_s5_w_glu1, m_s5_w_glu2, m_ffn_w_gate, m_ffn_w_up, m_ffn_w_down, v_meta_tokens, v_norm_g, v_ab_w_in, v_ab_b_f, v_ab_conv_w, v_ab_conv_b, v_ab_w_o, v_s5_a_re, v_s5_a_im, v_s5_log_step, v_s5_b_re, v_s5_b_im, v_s5_c_re, v_s5_c_im, v_s5_d, v_s5_w_glu1, v_s5_w_glu2, v_ffn_w_gate, v_ffn_w_up, v_ffn_w_down):
    given = dict(x=x, meta_tokens=meta_tokens, norm_g=norm_g, ab_w_in=ab_w_in, ab_b_f=ab_b_f, ab_conv_w=ab_conv_w, ab_conv_b=ab_conv_b, ab_w_o=ab_w_o, s5_a_re=s5_a_re, s5_a_im=s5_a_im, s5_log_step=s5_log_step, s5_b_re=s5_b_re, s5_b_im=s5_b_im, s5_c_re=s5_c_re, s5_c_im=s5_c_im, s5_d=s5_d, s5_w_glu1=s5_w_glu1, s5_w_glu2=s5_w_glu2, ffn_w_gate=ffn_w_gate, ffn_w_up=ffn_w_up, ffn_w_down=ffn_w_down, loss_target=loss_target, m_meta_tokens=m_meta_tokens, m_norm_g=m_norm_g, m_ab_w_in=m_ab_w_in, m_ab_b_f=m_ab_b_f, m_ab_conv_w=m_ab_conv_w, m_ab_conv_b=m_ab_conv_b, m_ab_w_o=m_ab_w_o, m_s5_a_re=m_s5_a_re, m_s5_a_im=m_s5_a_im, m_s5_log_step=m_s5_log_step, m_s5_b_re=m_s5_b_re, m_s5_b_im=m_s5_b_im, m_s5_c_re=m_s5_c_re, m_s5_c_im=m_s5_c_im, m_s5_d=m_s5_d, m_s5_w_glu1=m_s5_w_glu1, m_s5_w_glu2=m_s5_w_glu2, m_ffn_w_gate=m_ffn_w_gate, m_ffn_w_up=m_ffn_w_up, m_ffn_w_down=m_ffn_w_down, v_meta_tokens=v_meta_tokens, v_norm_g=v_norm_g, v_ab_w_in=v_ab_w_in, v_ab_b_f=v_ab_b_f, v_ab_conv_w=v_ab_conv_w, v_ab_conv_b=v_ab_conv_b, v_ab_w_o=v_ab_w_o, v_s5_a_re=v_s5_a_re, v_s5_a_im=v_s5_a_im, v_s5_log_step=v_s5_log_step, v_s5_b_re=v_s5_b_re, v_s5_b_im=v_s5_b_im, v_s5_c_re=v_s5_c_re, v_s5_c_im=v_s5_c_im, v_s5_d=v_s5_d, v_s5_w_glu1=v_s5_w_glu1, v_s5_w_glu2=v_s5_w_glu2, v_ffn_w_gate=v_ffn_w_gate, v_ffn_w_up=v_ffn_w_up, v_ffn_w_down=v_ffn_w_down)
    weights = {n: given[n] for n in TWIN_WEIGHTS}
    shared = {n: given[n] for n in SHARED_INPUTS}
    per_example = {n: given[n] for n in ['x']}
    grad_fn = _jax.value_and_grad(_loss, argnums=(0, 1))

    def one_microbatch(ex, loss_target):
        ex = dict(ex)
        diff = ex.pop(TWIN_DIFF_INPUT)
        return grad_fn(weights, diff, {**shared, **ex}, loss_target)

    if N_MICROBATCH == 1:
        loss, (grad_w, grad_x) = one_microbatch(per_example, given["loss_target"])
    else:
        def body(carry, xs):
            loss_sum, grad_sum = carry
            l_k, (gw_k, gx_k) = one_microbatch(xs[0], xs[1])
            with _jax.named_scope("update"):
                return (loss_sum + l_k, _jax.tree.map(_jnp.add, grad_sum, gw_k)), gx_k

        init = (_jnp.zeros((), _jnp.float32), _jax.tree.map(_jnp.zeros_like, weights))
        (loss, grad_w), grad_x = _jax.lax.scan(body, init, (per_example, given["loss_target"]))
    with _jax.named_scope("update"):
        delta_w, new_m, new_v = {}, {}, {}
        for n in TWIN_WEIGHTS:
            delta_w[n], new_m[n], new_v[n] = _adamw(weights[n], grad_w[n], given["m_" + n], given["v_" + n])
    return (loss, grad_x, *[grad_w[n] for n in TWIN_WEIGHTS], *[delta_w[n] for n in TWIN_WEIGHTS],
            *[new_m[n] for n in TWIN_WEIGHTS], *[new_v[n] for n in TWIN_WEIGHTS])
```

```python
import functools
import math

import jax
import jax.numpy as jnp
from jax import lax
from jax.experimental import pallas as pl
from jax.experimental.pallas import tpu as pltpu

F32 = jnp.float32
BF16 = jnp.bfloat16

N_META = 16
HEADS = 16
HEAD_DIM = 64
ATTN_W = HEADS * HEAD_DIM
CONV_K = 3
S5_GROUP = 16
S5_STATE = 64
S5_MIN_DECAY = 1e-4
NORM_EPS = 1e-6
ADAM_LR = 0.001
ADAM_B1 = 0.9
ADAM_B2 = 0.999
ADAM_EPS = 1e-08
ADAM_WD = 0.01
ADAM_STEP = 10

LANES = 128
SUBLANES = 8
VMEM_LIMIT = 56 * 1024 * 1024
VMEM_TILE_BUDGET = 34 * 1024 * 1024
ROW_TILE = 384
S5_BLOCK_GROUPS = LANES // S5_GROUP
S5_BLOCK_STATES = S5_BLOCK_GROUPS * S5_STATE
NEG_BIG = -1e30

MESH = pl.DeviceIdType.MESH
ANY = pl.BlockSpec(memory_space=pl.ANY)
VMEM_SPEC = pl.BlockSpec(memory_space=pltpu.VMEM)


def _params(sem=None):
    return pltpu.CompilerParams(dimension_semantics=sem, vmem_limit_bytes=VMEM_LIMIT)


def _div_tile(n, prefs):
    for p in prefs:
        if n % p == 0:
            return p
    return n


def _row_tile(rows, cols, itemsize=4, limit=2 * 1024 * 1024):
    for p in (512, 256, 128, 64, 32, 16):
        if rows % p == 0 and p * cols * itemsize <= limit:
            return p
    return 16 if rows % 16 == 0 else rows


def _tile_cands(n):
    c = [d for d in range(LANES, min(n, 2048) + 1, LANES) if n % d == 0]
    if not c or n <= 2048 and n not in c:
        c.append(n)
    return sorted(set(c), reverse=True)


def _mm_tiles(M, N, K, a_bytes, b_bytes, o_bytes, npairs):
    best = None
    for tk in _tile_cands(K):
        for tm in _tile_cands(M):
            for tn in _tile_cands(N):
                mem = npairs * 2 * (tm * tk * a_bytes + tk * tn * b_bytes) + 2 * tm * tn * o_bytes + tm * tn * 4
                mem += npairs * ((tm * tk * 2 if a_bytes == 4 else 0) + (tk * tn * 2 if b_bytes == 4 else 0))
                if mem > VMEM_TILE_BUDGET:
                    continue
                key = (tm * tn * tk, tk, tn)
                if best is None or key > best[0]:
                    best = (key, (tm, tn, tk))
    assert best is not None, (M, N, K)
    return best[1]


_DIMS = {"nn": (((1,), (0,)), ((), ())), "nt": (((1,), (1,)), ((), ())), "tn": (((0,), (0,)), ((), ()))}


def matmul(pairs, kind, out_dtype, name):
    a0, b0 = pairs[0]
    if kind == "nn":
        (M, K), N = a0.shape, b0.shape[1]
    elif kind == "nt":
        (M, K), N = a0.shape, b0.shape[0]
    else:
        (K, M), N = a0.shape, b0.shape[1]
    tm, tn, tk = _mm_tiles(M, N, K, a0.dtype.itemsize, b0.dtype.itemsize, jnp.dtype(out_dtype).itemsize, len(pairs))
    nk = K // tk
    dims = _DIMS[kind]
    npairs = len(pairs)

    def body(*refs):
        ins, o_ref = refs[:2 * npairs], refs[2 * npairs]
        part = None
        for p in range(npairs):
            d = lax.dot_general(ins[2 * p][...].astype(BF16), ins[2 * p + 1][...].astype(BF16), dims,
                                preferred_element_type=F32)
            part = d if part is None else part + d
        if nk == 1:
            o_ref[...] = part.astype(o_ref.dtype)
        else:
            acc_ref = refs[2 * npairs + 1]
            k = pl.program_id(2)

            @pl.when(k == 0)
            def _():
                acc_ref[...] = part

            @pl.when(k > 0)
            def _():
                acc_ref[...] += part

            @pl.when(k == nk - 1)
            def _():
                o_ref[...] = acc_ref[...].astype(o_ref.dtype)

    if kind == "nn":
        a_spec = pl.BlockSpec((tm, tk), lambda j, i, k: (i, k))
        b_spec = pl.BlockSpec((tk, tn), lambda j, i, k: (k, j))
    elif kind == "nt":
        a_spec = pl.BlockSpec((tm, tk), lambda j, i, k: (i, k))
        b_spec = pl.BlockSpec((tn, tk), lambda j, i, k: (j, k))
    else:
        a_spec = pl.BlockSpec((tk, tm), lambda j, i, k: (k, i))
        b_spec = pl.BlockSpec((tk, tn), lambda j, i, k: (k, j))
    flat = [t for ab in pairs for t in ab]
    return pl.pallas_call(
        body, name=name,
        grid=(N // tn, M // tm, nk),
        in_specs=[a_spec, b_spec] * npairs,
        out_specs=pl.BlockSpec((tm, tn), lambda j, i, k: (i, j)),
        out_shape=jax.ShapeDtypeStruct((M, N), out_dtype),
        scratch_shapes=[] if nk == 1 else [pltpu.VMEM((tm, tn), F32)],
        compiler_params=_params(("parallel", "parallel", "arbitrary")),
    )(*flat)


def _sigmoid(x):
    return 1.0 / (1.0 + jnp.exp(-x))


def dual_matmul_act(x, w1, w2, act, out_dtype, name):
    M, K = x.shape
    N = w1.shape[1]
    tm = _div_tile(M, (ROW_TILE,))
    tn = _div_tile(N, (1408, 1024, 512, 256, 128))

    def body(x_ref, w1_ref, w2_ref, o1_ref, o2_ref, out_ref):
        xv = x_ref[...]
        o1 = jnp.dot(xv, w1_ref[...], preferred_element_type=F32)
        o2 = jnp.dot(xv, w2_ref[...], preferred_element_type=F32)
        o1_ref[...] = o1.astype(BF16)
        o2_ref[...] = o2.astype(BF16)
        if act == "swiglu":
            out = o1 * _sigmoid(o1) * o2
        else:
            out = o1 * _sigmoid(o2)
        out_ref[...] = out.astype(out_ref.dtype)

    w_spec = pl.BlockSpec((K, tn), lambda j, i: (0, j))
    o_spec = pl.BlockSpec((tm, tn), lambda j, i: (i, j))
    return pl.pallas_call(
        body, name=name, grid=(N // tn, M // tm),
        in_specs=[pl.BlockSpec((tm, K), lambda j, i: (i, 0)), w_spec, w_spec],
        out_specs=[o_spec, o_spec, o_spec],
        out_shape=[jax.ShapeDtypeStruct((M, N), BF16), jax.ShapeDtypeStruct((M, N), BF16),
                   jax.ShapeDtypeStruct((M, N), out_dtype)],
        compiler_params=_params(("parallel", "parallel")),
    )(x, w1, w2)


def ffn_bwd_act(df, wd, a, b, name):
    M, K = df.shape
    N = wd.shape[0]
    tm = _div_tile(M, (ROW_TILE,))
    tn = _div_tile(N, (1408, 1024, 512, 256, 128))

    def body(df_ref, wd_ref, a_ref, b_ref, da_ref, db_ref):
        dh = lax.dot_general(df_ref[...], wd_ref[...], _DIMS["nt"], preferred_element_type=F32)
        av = a_ref[...].astype(F32)
        bv = b_ref[...].astype(F32)
        sig = _sigmoid(av)
        silu = av * sig
        da_ref[...] = (dh * bv * (sig + silu * (1.0 - sig))).astype(BF16)
        db_ref[...] = (dh * silu).astype(BF16)

    t_spec = pl.BlockSpec((tm, tn), lambda j, i: (i, j))
    return pl.pallas_call(
        body, name=name, grid=(N // tn, M // tm),
        in_specs=[pl.BlockSpec((tm, K), lambda j, i: (i, 0)), pl.BlockSpec((tn, K), lambda j, i: (j, 0)),
                  t_spec, t_spec],
        out_specs=[t_spec, t_spec],
        out_shape=[jax.ShapeDtypeStruct((M, N), BF16)] * 2,
        compiler_params=_params(("parallel", "parallel")),
    )(df, wd, a, b)


def glu_bwd_act(dout, o1, o2, name):
    M, N = dout.shape
    tm = _div_tile(M, (ROW_TILE,))

    def body(d_ref, o1_ref, o2_ref, d1_ref, d2_ref):
        d = d_ref[...].astype(F32)
        sig = _sigmoid(o2_ref[...].astype(F32))
        d1_ref[...] = (d * sig).astype(BF16)
        d2_ref[...] = (d * o1_ref[...].astype(F32) * sig * (1.0 - sig)).astype(BF16)

    spec = pl.BlockSpec((tm, N), lambda i: (i, 0))
    return pl.pallas_call(
        body, name=name, grid=(M // tm,), in_specs=[spec] * 3, out_specs=[spec] * 2,
        out_shape=[jax.ShapeDtypeStruct((M, N), BF16)] * 2,
        compiler_params=_params(("parallel",)),
    )(dout, o1, o2)


def rmsnorm_fwd(x, g, out_dtype, name, residual=None):
    L, D = x.shape
    tr = _div_tile(L, (ROW_TILE,))
    has_res = residual is not None

    def body(*refs):
        x_ref, g_ref = refs[0], refs[1]
        o_ref = refs[-1]
        xv = x_ref[...]
        r = lax.rsqrt(jnp.mean(xv * xv, axis=-1, keepdims=True) + NORM_EPS)
        y = xv * r * g_ref[...]
        if has_res:
            y = refs[2][...] + y
        o_ref[...] = y.astype(o_ref.dtype)

    row = pl.BlockSpec((tr, D), lambda i: (i, 0))
    gsp = pl.BlockSpec((1, D), lambda i: (0, 0))
    args = (x, g) + ((residual,) if has_res else ())
    return pl.pallas_call(
        body, name=name, grid=(L // tr,), in_specs=[row, gsp] + ([row] if has_res else []), out_specs=row,
        out_shape=jax.ShapeDtypeStruct((L, D), out_dtype), compiler_params=_params(("parallel",)),
    )(*args)


def rmsnorm_bwd(x, g, dy, out_dtype, name, add=None):
    L, D = x.shape
    tr = _div_tile(L, (ROW_TILE,))
    has_add = add is not None

    def body(*refs):
        x_ref, g_ref, dy_ref = refs[0], refs[1], refs[2]
        dx_ref, dg_ref = refs[-2], refs[-1]
        xv = x_ref[...]
        dyv = dy_ref[...].astype(F32)
        r = lax.rsqrt(jnp.mean(xv * xv, axis=-1, keepdims=True) + NORM_EPS)
        t = dyv * g_ref[...]
        dx = r * t - xv * (r * r * r) * jnp.mean(xv * t, axis=-1, keepdims=True)
        if has_add:
            dx = refs[3][...] + dx
        dx_ref[...] = dx.astype(dx_ref.dtype)
        dgp = jnp.sum(dyv * xv * r, axis=0, keepdims=True)

        @pl.when(pl.program_id(0) == 0)
        def _():
            dg_ref[...] = dgp

        @pl.when(pl.program_id(0) > 0)
        def _():
            dg_ref[...] += dgp

    row = pl.BlockSpec((tr, D), lambda i: (i, 0))
    gsp = pl.BlockSpec((1, D), lambda i: (0, 0))
    args = (x, g, dy) + ((add,) if has_add else ())
    return pl.pallas_call(
        body, name=name, grid=(L // tr,), in_specs=[row, gsp, row] + ([row] if has_add else []),
        out_specs=[row, gsp],
        out_shape=[jax.ShapeDtypeStruct((L, D), out_dtype), jax.ShapeDtypeStruct((1, D), F32)],
        compiler_params=_params(("arbitrary",)),
    )(*args)


def _gate_z(fg_ref, b_ref):
    return fg_ref[...] + b_ref[...]


def gate_fwd(fg_src, col_block, b, name):
    L = fg_src.shape[0]
    T = _div_tile(L, (ROW_TILE,))

    def body(fg_ref, b_ref, c_ref, carry):
        @pl.when(pl.program_id(0) == 0)
        def _():
            carry[...] = jnp.zeros_like(carry)

        z = _gate_z(fg_ref, b_ref)
        logf = jnp.minimum(z, 0.0) - jnp.log(1.0 + jnp.exp(-jnp.abs(z)))
        tri = (lax.broadcasted_iota(jnp.int32, (T, T), 1) <= lax.broadcasted_iota(jnp.int32, (T, T), 0)).astype(F32)
        c = jnp.dot(tri, logf, precision=lax.Precision.HIGHEST, preferred_element_type=F32) + carry[...]
        c_ref[...] = c
        carry[...] = c[T - 1:T, :]

    return pl.pallas_call(
        body, name=name, grid=(L // T,),
        in_specs=[pl.BlockSpec((T, LANES), lambda i: (i, col_block)), pl.BlockSpec((1, LANES), lambda i: (0, 0))],
        out_specs=pl.BlockSpec((T, LANES), lambda i: (i, 0)),
        out_shape=jax.ShapeDtypeStruct((L, LANES), F32),
        scratch_shapes=[pltpu.VMEM((1, LANES), F32)],
        compiler_params=_params(("arbitrary",)),
    )(fg_src, b)


def gate_bwd(fg_src, col_block, b, dc, name):
    L = fg_src.shape[0]
    T = _div_tile(L, (ROW_TILE,))
    nb = L // T

    def body(fg_ref, b_ref, dc_ref, dfg_ref, db_ref, carry):
        @pl.when(pl.program_id(0) == 0)
        def _():
            carry[...] = jnp.zeros_like(carry)
            db_ref[...] = jnp.zeros_like(db_ref)

        z = _gate_z(fg_ref, b_ref)
        dcv = dc_ref[...]
        tri = (lax.broadcasted_iota(jnp.int32, (T, T), 1) >= lax.broadcasted_iota(jnp.int32, (T, T), 0)).astype(F32)
        dlogf = jnp.dot(tri, dcv, precision=lax.Precision.HIGHEST, preferred_element_type=F32) + carry[...]
        dfg = dlogf * _sigmoid(-z)
        dfg_ref[...] = dfg
        db_ref[...] += jnp.sum(dfg, axis=0, keepdims=True)
        carry[...] = dlogf[0:1, :]

    return pl.pallas_call(
        body, name=name, grid=(nb,),
        in_specs=[pl.BlockSpec((T, LANES), lambda i: (nb - 1 - i, col_block)),
                  pl.BlockSpec((1, LANES), lambda i: (0, 0)),
                  pl.BlockSpec((T, LANES), lambda i: (nb - 1 - i, 0))],
        out_specs=[pl.BlockSpec((T, LANES), lambda i: (nb - 1 - i, 0)), pl.BlockSpec((1, LANES), lambda i: (0, 0))],
        out_shape=[jax.ShapeDtypeStruct((L, LANES), F32), jax.ShapeDtypeStruct((1, LANES), F32)],
        scratch_shapes=[pltpu.VMEM((1, LANES), F32)],
        compiler_params=_params(("arbitrary",)),
    )(fg_src, b, dc)


def attn_fwd(proj, cq_col, ck_row, name):
    L = proj.shape[0]
    T = _div_tile(L, (ROW_TILE,))
    nq = L // T
    npair = HEADS // 2
    scale = HEAD_DIM ** -0.5

    def body(q_ref, k_ref, v_ref, cq_ref, ck_ref, o_ref, lse_ref):
        qb = pl.program_id(1)
        outs, lses = [], []
        for h in range(2):
            sl = slice(h * HEAD_DIM, (h + 1) * HEAD_DIM)
            q = (q_ref[:, sl] * scale).astype(BF16)
            cq = cq_ref[0, :, h:h + 1]

            def step(kb, carry, masked, sl=sl, q=q, cq=cq, h=h):
                m, l, acc = carry
                ks = pl.multiple_of(kb * T, T)
                k = k_ref[pl.ds(ks, T), sl].astype(BF16)
                v = v_ref[pl.ds(ks, T), sl].astype(BF16)
                ck = ck_ref[0, h:h + 1, pl.ds(ks, T)]
                s = lax.dot_general(q, k, _DIMS["nt"], preferred_element_type=F32) + cq - ck
                if masked:
                    keep = lax.broadcasted_iota(jnp.int32, (T, T), 1) <= lax.broadcasted_iota(jnp.int32, (T, T), 0)
                    s = jnp.where(keep, s, NEG_BIG)
                m_new = jnp.maximum(m, jnp.max(s, axis=1, keepdims=True))
                alpha = jnp.exp(m - m_new)
                p = jnp.exp(s - m_new)
                l = alpha * l + jnp.sum(p, axis=1, keepdims=True)
                acc = alpha * acc + jnp.dot(p.astype(BF16), v, preferred_element_type=F32)
                return m_new, l, acc

            init = (jnp.full((T, 1), NEG_BIG, F32), jnp.zeros((T, 1), F32), jnp.zeros((T, HEAD_DIM), F32))
            carry = lax.fori_loop(0, qb, functools.partial(step, masked=False), init)
            m, l, acc = step(qb, carry, True)
            outs.append(acc / l)
            lses.append(m + jnp.log(l))
        o_ref[...] = jnp.concatenate(outs, axis=1).astype(o_ref.dtype)
        lse_ref[0] = jnp.concatenate(lses, axis=1)

    return pl.pallas_call(
        body, name=name, grid=(npair, nq),
        in_specs=[pl.BlockSpec((T, LANES), lambda p, i: (i, p)),
                  pl.BlockSpec((L, LANES), lambda p, i: (0, npair + p)),
                  pl.BlockSpec((L, LANES), lambda p, i: (0, 2 * npair + p)),
                  pl.BlockSpec((1, T, 2), lambda p, i: (p, i, 0)),
                  pl.BlockSpec((1, 2, L), lambda p, i: (p, 0, 0))],
        out_specs=[pl.BlockSpec((T, LANES), lambda p, i: (i, p)), pl.BlockSpec((1, T, 2), lambda p, i: (p, i, 0))],
        out_shape=[jax.ShapeDtypeStruct((L, ATTN_W), BF16), jax.ShapeDtypeStruct((npair, L, 2), F32)],
        compiler_params=_params(("parallel", "parallel")),
    )(proj, proj, proj, cq_col, ck_row)


def attn_delta(dcat, cat, name):
    L = dcat.shape[0]
    T = _div_tile(L, (ROW_TILE,))

    def body(do_ref, o_ref, d_ref):
        prod = do_ref[...] * o_ref[...].astype(F32)
        sel = (lax.broadcasted_iota(jnp.int32, (ATTN_W, LANES), 0) // HEAD_DIM
               == lax.broadcasted_iota(jnp.int32, (ATTN_W, LANES), 1)).astype(F32)
        d_ref[...] = jnp.dot(prod, sel, precision=lax.Precision.HIGHEST, preferred_element_type=F32)

    return pl.pallas_call(
        body, name=name, grid=(L // T,),
        in_specs=[pl.BlockSpec((T, ATTN_W), lambda i: (i, 0)), pl.BlockSpec((T, ATTN_W), lambda i: (i, 0))],
        out_specs=pl.BlockSpec((T, LANES), lambda i: (i, 0)),
        out_shape=jax.ShapeDtypeStruct((L, LANES), F32),
        compiler_params=_params(("parallel",)),
    )(dcat, cat)


def attn_bwd(proj, dcat, lse_row, delta_row, cq_row, ck_col, name):
    L = proj.shape[0]
    T = _div_tile(L, (ROW_TILE,))
    nb = L // T
    npair = HEADS // 2
    scale = HEAD_DIM ** -0.5

    def body(q_ref, k_ref, v_ref, do_ref, lse_ref, dl_ref, cq_ref, ck_ref,
             dq_ref, dk_ref, dv_ref, dcq_ref, dck_ref, dq_acc, dcq_acc):
        kb = pl.program_id(1)

        @pl.when(kb == 0)
        def _():
            dq_acc[...] = jnp.zeros_like(dq_acc)
            dcq_acc[...] = jnp.zeros_like(dcq_acc)

        dks, dvs, dcks = [], [], []
        for h in range(2):
            sl = slice(h * HEAD_DIM, (h + 1) * HEAD_DIM)
            k = k_ref[:, sl].astype(BF16)
            v = v_ref[:, sl].astype(BF16)
            ck = ck_ref[0, :, h:h + 1]

            def step(qb, carry, masked, sl=sl, k=k, v=v, ck=ck, h=h):
                dk, dv, dck = carry
                qs = pl.multiple_of(qb * T, T)
                q = q_ref[pl.ds(qs, T), sl].astype(BF16)
                do = do_ref[pl.ds(qs, T), sl].astype(BF16)
                lse = lse_ref[0, h:h + 1, pl.ds(qs, T)]
                dl = dl_ref[0, h:h + 1, pl.ds(qs, T)]
                cq = cq_ref[0, h:h + 1, pl.ds(qs, T)]
                st = lax.dot_general(k, q, _DIMS["nt"], preferred_element_type=F32) * scale + cq - ck
                if masked:
                    keep = lax.broadcasted_iota(jnp.int32, (T, T), 0) <= lax.broadcasted_iota(jnp.int32, (T, T), 1)
                    st = jnp.where(keep, st, NEG_BIG)
                pt = jnp.exp(st - lse)
                dv = dv + jnp.dot(pt.astype(BF16), do, preferred_element_type=F32)
                dpt = lax.dot_general(v, do, _DIMS["nt"], preferred_element_type=F32)
                dst = pt * (dpt - dl)
                dsb = dst.astype(BF16)
                dk = dk + jnp.dot(dsb, q, preferred_element_type=F32)
                dq_acc[pl.ds(qs, T), sl] += lax.dot_general(dsb, k, _DIMS["tn"], preferred_element_type=F32)
                dcq_acc[h:h + 1, pl.ds(qs, T)] += jnp.sum(dst, axis=0, keepdims=True)
                dck = dck + jnp.sum(dst, axis=1, keepdims=True)
                return dk, dv, dck

            init = (jnp.zeros((T, HEAD_DIM), F32), jnp.zeros((T, HEAD_DIM), F32), jnp.zeros((T, 1), F32))
            carry = step(kb, init, True)
            dk, dv, dck = lax.fori_loop(kb + 1, nb, functools.partial(step, masked=False), carry)
            dks.append(dk * scale)
            dvs.append(dv)
            dcks.append(-dck)
        dk_ref[...] = jnp.concatenate(dks, axis=1).astype(dk_ref.dtype)
        dv_ref[...] = jnp.concatenate(dvs, axis=1).astype(dv_ref.dtype)
        dck_ref[0] = jnp.concatenate(dcks, axis=1)

        @pl.when(kb == nb - 1)
        def _():
            dq_ref[...] = (dq_acc[...] * scale).astype(dq_ref.dtype)
            dcq_ref[0] = dcq_acc[...]

    full = lambda col: pl.BlockSpec((L, LANES), col)
    row_stat = pl.BlockSpec((1, 2, L), lambda p, i: (p, 0, 0))
    return pl.pallas_call(
        body, name=name, grid=(npair, nb),
        in_specs=[full(lambda p, i: (0, p)),
                  pl.BlockSpec((T, LANES), lambda p, i: (i, npair + p)),
                  pl.BlockSpec((T, LANES), lambda p, i: (i, 2 * npair + p)),
                  full(lambda p, i: (0, p)),
                  row_stat, row_stat, row_stat,
                  pl.BlockSpec((1, T, 2), lambda p, i: (p, i, 0))],
        out_specs=[full(lambda p, i: (0, p)),
                   pl.BlockSpec((T, LANES), lambda p, i: (i, p)),
                   pl.BlockSpec((T, LANES), lambda p, i: (i, p)),
                   row_stat,
                   pl.BlockSpec((1, T, 2), lambda p, i: (p, i, 0))],
        out_shape=[jax.ShapeDtypeStruct((L, ATTN_W), BF16)] * 3
        + [jax.ShapeDtypeStruct((npair, 2, L), F32), jax.ShapeDtypeStruct((npair, L, 2), F32)],
        scratch_shapes=[pltpu.VMEM((L, LANES), F32), pltpu.VMEM((2, L), F32)],
        compiler_params=_params(("parallel", "arbitrary")),
    )(proj, proj, proj, dcat, lse_row, delta_row, cq_row, ck_col)


def _shift_down(x, k):
    rolled = pltpu.roll(x, k, 0)
    return jnp.where(lax.broadcasted_iota(jnp.int32, x.shape, 0) >= k, rolled, 0.0)


def _shift_up(x, k):
    n = x.shape[0]
    rolled = pltpu.roll(x, n - k, 0)
    return jnp.where(lax.broadcasted_iota(jnp.int32, x.shape, 0) < n - k, rolled, 0.0)


def conv_fwd(proj, col0, conv_w, conv_b, name):
    L = proj.shape[0]
    C = conv_w.shape[1]
    nc = C // LANES

    def body(gb_ref, gc_ref, xc_ref, w_ref, b_ref, o_ref):
        z = gc_ref[...] * xc_ref[...]
        conv = (w_ref[0:1, :] * _shift_down(z, 2) + w_ref[1:2, :] * _shift_down(z, 1) + w_ref[2:3, :] * z
                + b_ref[...])
        o_ref[...] = (gb_ref[...] * conv).astype(o_ref.dtype)

    col = lambda off: pl.BlockSpec((L, LANES), lambda j, off=off: (0, col0 + off + j))
    return pl.pallas_call(
        body, name=name, grid=(nc,),
        in_specs=[col(0), col(nc), col(2 * nc), pl.BlockSpec((CONV_K, LANES), lambda j: (0, j)),
                  pl.BlockSpec((1, LANES), lambda j: (0, j))],
        out_specs=pl.BlockSpec((L, LANES), lambda j: (0, j)),
        out_shape=jax.ShapeDtypeStruct((L, C), BF16),
        compiler_params=_params(("parallel",)),
    )(proj, proj, proj, conv_w, conv_b)


def conv_bwd(proj, col0, conv_w, conv_b, dcat, dcol0, name):
    L = proj.shape[0]
    C = conv_w.shape[1]
    nc = C // LANES

    def body(gb_ref, gc_ref, xc_ref, w_ref, b_ref, do_ref, dgb_ref, dgc_ref, dxc_ref, dw_ref, db_ref):
        gc, xc = gc_ref[...], xc_ref[...]
        z = gc * xc
        z1, z2 = _shift_down(z, 1), _shift_down(z, 2)
        w0, w1, w2 = w_ref[0:1, :], w_ref[1:2, :], w_ref[2:3, :]
        conv = w0 * z2 + w1 * z1 + w2 * z + b_ref[...]
        dout = do_ref[...]
        dgb_ref[...] = (dout * conv).astype(dgb_ref.dtype)
        dconv = dout * gb_ref[...]
        dw_ref[...] = jnp.concatenate([jnp.sum(dconv * z2, axis=0, keepdims=True),
                                       jnp.sum(dconv * z1, axis=0, keepdims=True),
                                       jnp.sum(dconv * z, axis=0, keepdims=True)], axis=0)
        db_ref[...] = jnp.sum(dconv, axis=0, keepdims=True)
        dz = w2 * dconv + w1 * _shift_up(dconv, 1) + w0 * _shift_up(dconv, 2)
        dgc_ref[...] = (dz * xc).astype(dgc_ref.dtype)
        dxc_ref[...] = (dz * gc).astype(dxc_ref.dtype)

    col = lambda off: pl.BlockSpec((L, LANES), lambda j, off=off: (0, col0 + off + j))
    out_col = pl.BlockSpec((L, LANES), lambda j: (0, j))
    return pl.pallas_call(
        body, name=name, grid=(nc,),
        in_specs=[col(0), col(nc), col(2 * nc), pl.BlockSpec((CONV_K, LANES), lambda j: (0, j)),
                  pl.BlockSpec((1, LANES), lambda j: (0, j)),
                  pl.BlockSpec((L, LANES), lambda j: (0, dcol0 + j))],
        out_specs=[out_col, out_col, out_col, pl.BlockSpec((CONV_K, LANES), lambda j: (0, j)),
                   pl.BlockSpec((1, LANES), lambda j: (0, j))],
        out_shape=[jax.ShapeDtypeStruct((L, C), BF16)] * 3
        + [jax.ShapeDtypeStruct((CONV_K, C), F32), jax.ShapeDtypeStruct((1, C), F32)],
        compiler_params=_params(("parallel",)),
    )(proj, proj, proj, conv_w, conv_b, dcat)


_GELU_C = math.sqrt(2.0 / math.pi)
_GELU_A = 0.044715


def _gelu(y):
    return 0.5 * y * (1.0 + jnp.tanh(_GELU_C * (y + _GELU_A * y * y * y)))


def _gelu_grad(y):
    t = jnp.tanh(_GELU_C * (y + _GELU_A * y * y * y))
    return 0.5 * (1.0 + t) + 0.5 * y * (1.0 - t * t) * _GELU_C * (1.0 + 3.0 * _GELU_A * y * y)


def _cmul_add(xr, xi, pr, pi, sr, si):
    return xr + pr * sr - pi * si, xi + pr * si + pi * sr


def _scan_tile(br, bi, cr, ci, tab_ref, reverse):
    n = S5_BLOCK_STATES
    xr, xi = br, bi
    for s, k in enumerate((1, 2, 4)):
        shift = SUBLANES - k if reverse else k
        xr, xi = _cmul_add(xr, xi, tab_ref[0, s, :, :n], tab_ref[0, s, :, n:],
                           pltpu.roll(xr, shift, 0), pltpu.roll(xi, shift, 0))
    return _cmul_add(xr, xi, tab_ref[0, 3, :, :n], tab_ref[0, 3, :, n:], cr, ci)


def s5_fwd(u, bmat, cmat, tab, dvec, name):
    L, D = u.shape
    nblk = D // LANES
    T = _div_tile(L, (ROW_TILE,))
    ns = 2 * S5_BLOCK_STATES
    n = S5_BLOCK_STATES

    def body(u_ref, b_ref, c_ref, tab_ref, d_ref, y_ref, g_ref, xs_ref, buf, car):
        @pl.when(pl.program_id(1) == 0)
        def _():
            car[...] = jnp.zeros_like(car)

        uv = u_ref[...]
        buf[...] = jnp.dot(uv.astype(BF16), b_ref[0], preferred_element_type=F32)

        def tile(i, carry):
            cr, ci = carry
            r0 = pl.multiple_of(i * SUBLANES, SUBLANES)
            xr, xi = _scan_tile(buf[pl.ds(r0, SUBLANES), :n], buf[pl.ds(r0, SUBLANES), n:], cr, ci, tab_ref, False)
            buf[pl.ds(r0, SUBLANES), :n] = xr
            buf[pl.ds(r0, SUBLANES), n:] = xi
            return xr[SUBLANES - 1:, :], xi[SUBLANES - 1:, :]

        cr, ci = lax.fori_loop(0, T // SUBLANES, tile, (car[:, :n], car[:, n:]))
        car[:, :n] = cr
        car[:, n:] = ci
        xs = buf[...]
        xs_ref[...] = xs
        y = jnp.dot(xs.astype(BF16), c_ref[0], preferred_element_type=F32) + d_ref[...] * uv
        y_ref[...] = y
        g_ref[...] = _gelu(y).astype(g_ref.dtype)

    blk = pl.BlockSpec((T, LANES), lambda j, i: (i, j))
    return pl.pallas_call(
        body, name=name, grid=(nblk, L // T),
        in_specs=[blk, pl.BlockSpec((1, LANES, ns), lambda j, i: (j, 0, 0)),
                  pl.BlockSpec((1, ns, LANES), lambda j, i: (j, 0, 0)),
                  pl.BlockSpec((1, 4, SUBLANES, ns), lambda j, i: (j, 0, 0, 0)),
                  pl.BlockSpec((1, LANES), lambda j, i: (0, j))],
        out_specs=[blk, blk, pl.BlockSpec((T, ns), lambda j, i: (i, j))],
        out_shape=[jax.ShapeDtypeStruct((L, D), F32), jax.ShapeDtypeStruct((L, D), BF16),
                   jax.ShapeDtypeStruct((L, nblk * ns), F32)],
        scratch_shapes=[pltpu.VMEM((T, ns), F32), pltpu.VMEM((1, ns), F32)],
        compiler_params=_params(("parallel", "arbitrary")),
    )(u, bmat, cmat, tab, dvec)


def s5_bwd(dg, y, u, xs, cmat_t, bmat_t, rtab, dvec, name):
    L, D = u.shape
    nblk = D // LANES
    T = _div_tile(L, (ROW_TILE,))
    nch = L // T
    ns = 2 * S5_BLOCK_STATES
    n = S5_BLOCK_STATES
    ntile = T // SUBLANES

    def body(dg_ref, y_ref, u_ref, xs_ref, xp_ref, ct_ref, bt_ref, tab_ref, d_ref,
             du_ref, dc_ref, db_ref, dlam_ref, dd_ref, buf, xbuf, car):
        step = pl.program_id(1)
        first_chunk = step == nch - 1

        @pl.when(step == 0)
        def _():
            car[...] = jnp.zeros_like(car)
            dc_ref[...] = jnp.zeros_like(dc_ref)
            db_ref[...] = jnp.zeros_like(db_ref)
            dlam_ref[...] = jnp.zeros_like(dlam_ref)
            dd_ref[...] = jnp.zeros_like(dd_ref)

        uv = u_ref[...]
        dy = dg_ref[...].astype(F32) * _gelu_grad(y_ref[...])
        dd_ref[...] += jnp.sum(dy * uv, axis=0, keepdims=True)
        dyb = dy.astype(BF16)
        buf[...] = jnp.dot(dyb, ct_ref[0], preferred_element_type=F32)
        xs = xs_ref[...]
        xbuf[pl.ds(SUBLANES, T), :] = xs
        xbuf[pl.ds(0, SUBLANES), :] = jnp.where(first_chunk, 0.0, xp_ref[...])
        row0 = lax.broadcasted_iota(jnp.int32, (SUBLANES, n), 0) == 0

        def tile(ii, carry):
            cr, ci, ar, ai = carry
            r0 = pl.multiple_of((ntile - 1 - ii) * SUBLANES, SUBLANES)
            xr, xi = _scan_tile(buf[pl.ds(r0, SUBLANES), :n], buf[pl.ds(r0, SUBLANES), n:], cr, ci, tab_ref, True)
            buf[pl.ds(r0, SUBLANES), :n] = xr
            buf[pl.ds(r0, SUBLANES), n:] = xi
            r1 = pl.multiple_of(r0 + SUBLANES, SUBLANES)
            pr = jnp.where(row0, xbuf[pl.ds(r0, SUBLANES), :n][SUBLANES - 1:, :],
                           pltpu.roll(xbuf[pl.ds(r1, SUBLANES), :n], 1, 0))
            pi = jnp.where(row0, xbuf[pl.ds(r0, SUBLANES), n:][SUBLANES - 1:, :],
                           pltpu.roll(xbuf[pl.ds(r1, SUBLANES), n:], 1, 0))
            ar = ar + xr * pr + xi * pi
            ai = ai + xi * pr - xr * pi
            return xr[0:1, :], xi[0:1, :], ar, ai

        zero = jnp.zeros((SUBLANES, n), F32)
        cr, ci, ar, ai = lax.fori_loop(0, ntile, tile, (car[:, :n], car[:, n:], zero, zero))
        car[:, :n] = cr
        car[:, n:] = ci
        dlam_ref[0, :, :n] += ar
        dlam_ref[0, :, n:] += ai
        dxa = buf[...]
        dc_ref[0] += lax.dot_general(dyb, xs.astype(BF16), _DIMS["tn"], preferred_element_type=F32)
        dxb = dxa.astype(BF16)
        db_ref[0] += lax.dot_general(uv.astype(BF16), dxb, _DIMS["tn"], preferred_element_type=F32)
        du_ref[...] = jnp.dot(dxb, bt_ref[0], preferred_element_type=F32) + d_ref[...] * dy

    rev = lambda j, i: (nch - 1 - i, j)
    blk = pl.BlockSpec((T, LANES), rev)
    tpb = T // SUBLANES
    acc = pl.BlockSpec((1, LANES, ns), lambda j, i: (j, 0, 0))
    return pl.pallas_call(
        body, name=name, grid=(nblk, nch),
        in_specs=[blk, blk, blk, pl.BlockSpec((T, ns), rev),
                  pl.BlockSpec((SUBLANES, ns), lambda j, i: (jnp.maximum((nch - 1 - i) * tpb - 1, 0), j)),
                  pl.BlockSpec((1, LANES, ns), lambda j, i: (j, 0, 0)),
                  pl.BlockSpec((1, ns, LANES), lambda j, i: (j, 0, 0)),
                  pl.BlockSpec((1, 4, SUBLANES, ns), lambda j, i: (j, 0, 0, 0)),
                  pl.BlockSpec((1, LANES), lambda j, i: (0, j))],
        out_specs=[blk, acc, acc, pl.BlockSpec((1, SUBLANES, ns), lambda j, i: (j, 0, 0)),
                   pl.BlockSpec((1, LANES), lambda j, i: (0, j))],
        out_shape=[jax.ShapeDtypeStruct((L, D), F32), jax.ShapeDtypeStruct((nblk, LANES, ns), F32),
                   jax.ShapeDtypeStruct((nblk, LANES, ns), F32), jax.ShapeDtypeStruct((nblk, SUBLANES, ns), F32),
                   jax.ShapeDtypeStruct((1, D), F32)],
        scratch_shapes=[pltpu.VMEM((T, ns), F32), pltpu.VMEM((T + SUBLANES, ns), F32), pltpu.VMEM((1, ns), F32)],
        compiler_params=_params(("parallel", "arbitrary")),
    )(dg, y, u, xs, xs, cmat_t, bmat_t, rtab, dvec)


def _s5_discretize(a_re, a_im, log_step, b_re, b_im):
    lam_re = jnp.minimum(a_re, -S5_MIN_DECAY)
    lam_im = a_im
    delta = jnp.exp(log_step)[:, None]
    mag = jnp.exp(lam_re * delta)
    ang = lam_im * delta
    lb_re = mag * jnp.cos(ang)
    lb_im = mag * jnp.sin(ang)
    den = lam_re * lam_re + lam_im * lam_im
    nr = lb_re - 1.0
    ni = lb_im
    coef_re = (nr * lam_re + ni * lam_im) / den
    coef_im = (ni * lam_re - nr * lam_im) / den
    bb_re = coef_re[..., None] * b_re - coef_im[..., None] * b_im
    bb_im = coef_re[..., None] * b_im + coef_im[..., None] * b_re
    return lb_re, lb_im, bb_re, bb_im


def _s5_tables(lb_re, lb_im):
    nblk = lb_re.shape[0] // S5_BLOCK_GROUPS
    lr = lb_re.reshape(nblk, S5_BLOCK_STATES)
    li = lb_im.reshape(nblk, S5_BLOCK_STATES)
    pows = [(jnp.ones_like(lr), jnp.zeros_like(li))]
    for _ in range(SUBLANES):
        pr, pi = pows[-1]
        pows.append((pr * lr - pi * li, pr * li + pi * lr))
    rows = jnp.arange(SUBLANES)[None, :, None]

    def table(conj, reverse):
        sgn = -1.0 if conj else 1.0
        out = []
        for k in (1, 2, 4):
            mask = (rows <= SUBLANES - 1 - k) if reverse else (rows >= k)
            out.append(jnp.concatenate([jnp.where(mask, pows[k][0][:, None, :], 0.0),
                                        jnp.where(mask, sgn * pows[k][1][:, None, :], 0.0)], axis=-1))
        order = range(SUBLANES, 0, -1) if reverse else range(1, SUBLANES + 1)
        cre = jnp.stack([pows[k][0] for k in order], axis=1)
        cim = jnp.stack([sgn * pows[k][1] for k in order], axis=1)
        out.append(jnp.concatenate([cre, cim], axis=-1))
        return jnp.stack(out, axis=1)

    return table(False, False), table(True, True)


def _s5_block_mats(bb_re, bb_im, c_re, c_im):
    G = bb_re.shape[0]
    nblk = G // S5_BLOCK_GROUPS
    eye = jnp.eye(S5_BLOCK_GROUPS, dtype=F32)
    bb = jnp.stack([bb_re, bb_im]).reshape(2, nblk, S5_BLOCK_GROUPS, S5_STATE, S5_GROUP)
    bmat = jnp.einsum("ab,rjaph->jahrbp", eye, bb).reshape(nblk, LANES, 2 * S5_BLOCK_STATES)
    cc = jnp.stack([c_re, -c_im]).reshape(2, nblk, S5_BLOCK_GROUPS, S5_GROUP, S5_STATE)
    cmat = jnp.einsum("ab,rjahp->jrbpah", eye, cc).reshape(nblk, 2 * S5_BLOCK_STATES, LANES)
    return bmat, cmat


def _s5_unblock(dB, dC, dlam):
    nblk = dB.shape[0]
    G = nblk * S5_BLOCK_GROUPS
    d6 = dB.reshape(nblk, S5_BLOCK_GROUPS, S5_GROUP, 2, S5_BLOCK_GROUPS, S5_STATE)
    dbb = jnp.einsum("jahrap->rjaph", d6).reshape(2, G, S5_STATE, S5_GROUP)
    c6 = dC.reshape(nblk, S5_BLOCK_GROUPS, S5_GROUP, 2, S5_BLOCK_GROUPS, S5_STATE)
    dcc = jnp.einsum("jahrap->rjahp", c6).reshape(2, G, S5_GROUP, S5_STATE)
    dl = jnp.sum(dlam, axis=1).reshape(nblk, 2, S5_BLOCK_GROUPS, S5_STATE)
    dl = jnp.transpose(dl, (1, 0, 2, 3)).reshape(2, G, S5_STATE)
    return dbb[0], dbb[1], dcc[0], -dcc[1], dl[0], dl[1]


def loss_and_grad(y, target, name):
    L, D = y.shape
    tr = _div_tile(L, (512, 256, 128))

    def body(y_ref, t_ref, dy_ref, loss_ref):
        err = y_ref[...] - t_ref[...]
        dy_ref[...] = err * (1.0 / D)
        part = 0.5 * jnp.sum(jnp.mean(err * err, axis=-1, keepdims=True), axis=0, keepdims=True)

        @pl.when(pl.program_id(0) == 0)
        def _():
            loss_ref[...] = part

        @pl.when(pl.program_id(0) > 0)
        def _():
            loss_ref[...] += part

    row = pl.BlockSpec((tr, D), lambda i: (i, 0))
    return pl.pallas_call(
        body, name=name, grid=(L // tr,), in_specs=[row, row],
        out_specs=[row, pl.BlockSpec((1, 1), lambda i: (0, 0))],
        out_shape=[jax.ShapeDtypeStruct((L, D), F32), jax.ShapeDtypeStruct((1, 1), F32)],
        compiler_params=_params(("arbitrary",)),
    )(y, target)


def _adam_math(w, g, m, v):
    m = ADAM_B1 * m + (1.0 - ADAM_B1) * g
    v = ADAM_B2 * v + (1.0 - ADAM_B2) * (g * g)
    m_hat = m / (1.0 - ADAM_B1 ** ADAM_STEP)
    v_hat = v / (1.0 - ADAM_B2 ** ADAM_STEP)
    delta = -ADAM_LR * (m_hat / (jnp.sqrt(v_hat) + ADAM_EPS) + ADAM_WD * w)
    return delta, m, v


def _as3d(a):
    return a.reshape((-1,) + a.shape[-2:])


def adamw(w, g, m, v, name):
    shape = w.shape
    w3, g3, m3, v3 = _as3d(w), _as3d(g), _as3d(m), _as3d(v)
    A, R, C = w3.shape
    tr = _row_tile(R, C)

    def body(w_ref, g_ref, m_ref, v_ref, d_ref, mo_ref, vo_ref):
        d, mn, vn = _adam_math(w_ref[...], g_ref[...], m_ref[...], v_ref[...])
        d_ref[...] = d
        mo_ref[...] = mn
        vo_ref[...] = vn

    spec = pl.BlockSpec((1, tr, C), lambda a, i: (a, i, 0))
    outs = pl.pallas_call(
        body, name=name, grid=(A, R // tr), in_specs=[spec] * 4, out_specs=[spec] * 3,
        out_shape=[jax.ShapeDtypeStruct((A, R, C), F32)] * 3,
        compiler_params=_params(("parallel", "parallel")),
    )(w3, g3, m3, v3)
    return [o.reshape(shape) for o in outs]


def cast_bf16(a, name):
    a3 = _as3d(a)
    A, R, C = a3.shape
    tr = _row_tile(R, C)

    def body(a_ref, o_ref):
        o_ref[...] = a_ref[...].astype(BF16)

    spec = pl.BlockSpec((1, tr, C), lambda b, i: (b, i, 0))
    out = pl.pallas_call(
        body, name=name, grid=(A, R // tr), in_specs=[spec], out_specs=spec,
        out_shape=jax.ShapeDtypeStruct((A, R, C), BF16), compiler_params=_params(("parallel", "parallel")),
    )(a3)
    return out.reshape(a.shape)


def _place():
    x, y, c = lax.axis_index("x"), lax.axis_index("y"), lax.axis_index("c")
    other_chips = [(1 - x, y), (x, 1 - y), (1 - x, 1 - y)]
    return x, y, c, other_chips


def _chip_id(chip):
    return 2 * chip[0] + chip[1]


def _shard_view(ref, kind, shard, lo, cnt):
    if kind == "stack":
        return ref.at[shard, pl.ds(lo, cnt)]
    if kind == "row":
        R = ref.shape[1] // 4
        return ref.at[pl.ds(lo, cnt), pl.ds(pl.multiple_of(shard * R, 16), R), :]
    C = ref.shape[2] // 4
    return ref.at[pl.ds(lo, cnt), :, pl.ds(pl.multiple_of(shard * C, LANES), C)]


def _layer_view(ref, kind, lo, cnt):
    if kind == "stack":
        return ref.at[:, pl.ds(lo, cnt)]
    return ref.at[pl.ds(lo, cnt)]


def _gathered_shape(shape, kind):
    n, R, C = shape
    return {"stack": (4, n, R, C), "row": (n, 4 * R, C), "col": (n, R, 4 * C)}[kind]


def _remote(src, dst, send_sem, recv_sem, dev):
    return pltpu.make_async_remote_copy(src_ref=src, dst_ref=dst, send_sem=send_sem, recv_sem=recv_sem,
                                        device_id=dev, device_id_type=MESH)


def allgather_weights(shards, kinds):
    T = len(shards)

    def body(*refs):
        ins, outs = refs[:T], refs[T:2 * T]
        send_sems, recv_sems, local_sems = refs[2 * T:]
        x, y, c, chips = _place()
        me, sibling = _chip_id((x, y)), (x, y, 1 - c)
        local, sends = [], []
        for t in range(T):
            n = ins[t].shape[0]
            hn = n // 2
            cp = pltpu.make_async_copy(ins[t], _shard_view(outs[t], kinds[t], me, 0, n), local_sems.at[t])
            cp.start()
            local.append(cp)
            for j, chip in enumerate(chips):
                cp = _remote(ins[t].at[pl.ds(c * hn, hn)], _shard_view(outs[t], kinds[t], me, c * hn, hn),
                             send_sems.at[6 * t + j], recv_sems.at[6 * t + j], (*chip, c))
                cp.start()
                sends.append(cp)
        for t in range(T):
            hn = ins[t].shape[0] // 2
            for j, chip in enumerate(chips):
                piece = _shard_view(outs[t], kinds[t], _chip_id(chip), c * hn, hn)
                cp = _remote(piece, piece, send_sems.at[6 * t + 3 + j], recv_sems.at[6 * t + j], sibling)
                cp.wait_recv()
                fwd = _remote(piece, piece, send_sems.at[6 * t + 3 + j], recv_sems.at[6 * t + 3 + j], sibling)
                fwd.start()
                sends.append(fwd)
        for t in range(T):
            hn = ins[t].shape[0] // 2
            for j, chip in enumerate(chips):
                piece = _shard_view(outs[t], kinds[t], _chip_id(chip), (1 - c) * hn, hn)
                _remote(piece, piece, send_sems.at[6 * t + 3 + j], recv_sems.at[6 * t + 3 + j], sibling).wait_recv()
        for cp in sends:
            cp.wait_send()
        for cp in local:
            cp.wait()

    return pl.pallas_call(
        body, name="allgather_weights", in_specs=[ANY] * T, out_specs=[ANY] * T,
        out_shape=[jax.ShapeDtypeStruct(_gathered_shape(s.shape, k), s.dtype) for s, k in zip(shards, kinds)],
        scratch_shapes=[pltpu.SemaphoreType.DMA((6 * T,)), pltpu.SemaphoreType.DMA((6 * T,)),
                        pltpu.SemaphoreType.DMA((T,))],
        compiler_params=pltpu.CompilerParams(has_side_effects=True),
    )(*shards)


def allgather_small(arrs):
    T = len(arrs)

    def body(*refs):
        ins, outs = refs[:T], refs[T:2 * T]
        send_sems, recv_sems = refs[2 * T:]
        x, y, c, chips = _place()
        me = _chip_id((x, y))
        sends = []
        for t in range(T):
            outs[t][me] = ins[t][...]
            for j, chip in enumerate(chips):
                cp = _remote(ins[t], outs[t].at[me], send_sems.at[3 * t + j], recv_sems.at[3 * t + j], (*chip, c))
                cp.start()
                sends.append(cp)
        for t in range(T):
            for j, chip in enumerate(chips):
                slot = outs[t].at[_chip_id(chip)]
                _remote(slot, slot, send_sems.at[3 * t + j], recv_sems.at[3 * t + j], (*chip, c)).wait_recv()
        for cp in sends:
            cp.wait_send()

    return pl.pallas_call(
        body, name="allgather_small", in_specs=[VMEM_SPEC] * T, out_specs=[VMEM_SPEC] * T,
        out_shape=[jax.ShapeDtypeStruct((4,) + a.shape, a.dtype) for a in arrs],
        scratch_shapes=[pltpu.SemaphoreType.DMA((3 * T,)), pltpu.SemaphoreType.DMA((3 * T,))],
        compiler_params=pltpu.CompilerParams(vmem_limit_bytes=VMEM_LIMIT, has_side_effects=True),
    )(*arrs)


def allreduce_small(buf):
    R, C = buf.shape

    def body(in_ref, out_ref, pair_ref, all_ref, send_sems, recv_sems):
        x, y, c, chips = _place()
        me, sibling = _chip_id((x, y)), (x, y, 1 - c)
        swap = _remote(in_ref, pair_ref, send_sems.at[0], recv_sems.at[0], sibling)
        swap.start()
        swap.wait()
        all_ref[me] = in_ref[...] + pair_ref[...]
        sends = []
        for j, chip in enumerate(chips):
            cp = _remote(all_ref.at[me], all_ref.at[me], send_sems.at[1 + j], recv_sems.at[1 + j], (*chip, c))
            cp.start()
            sends.append(cp)
        for j, chip in enumerate(chips):
            slot = all_ref.at[_chip_id(chip)]
            _remote(slot, slot, send_sems.at[1 + j], recv_sems.at[1 + j], (*chip, c)).wait_recv()
        for cp in sends:
            cp.wait_send()
        out_ref[...] = ((all_ref[0] + all_ref[1]) + all_ref[2]) + all_ref[3]

    return pl.pallas_call(
        body, name="allreduce_small", in_specs=[VMEM_SPEC], out_specs=VMEM_SPEC,
        out_shape=jax.ShapeDtypeStruct((R, C), F32),
        scratch_shapes=[pltpu.VMEM((R, C), F32), pltpu.VMEM((4, R, C), F32),
                        pltpu.SemaphoreType.DMA((4,)), pltpu.SemaphoreType.DMA((4,))],
        compiler_params=pltpu.CompilerParams(vmem_limit_bytes=VMEM_LIMIT, has_side_effects=True),
    )(buf)


def _half_shape(shape, kind):
    s = list(shape)
    s[1 if kind == "stack" else 0] //= 2
    return tuple(s)


def rs_swap_halves(grads, kinds):
    T = len(grads)

    def body(*refs):
        ins, outs = refs[:T], refs[T:2 * T]
        send_sems, recv_sems = refs[2 * T:]
        x, y, c, _ = _place()
        copies = []
        for t in range(T):
            n = ins[t].shape[1 if kinds[t] == "stack" else 0]
            hn = n // 2
            cp = _remote(_layer_view(ins[t], kinds[t], (1 - c) * hn, hn), outs[t], send_sems.at[t], recv_sems.at[t],
                         (x, y, 1 - c))
            cp.start()
            copies.append(cp)
        for cp in copies:
            cp.wait()

    return pl.pallas_call(
        body, name="rs_swap_halves", in_specs=[ANY] * T, out_specs=[ANY] * T,
        out_shape=[jax.ShapeDtypeStruct(_half_shape(g.shape, k), g.dtype) for g, k in zip(grads, kinds)],
        scratch_shapes=[pltpu.SemaphoreType.DMA((T,)), pltpu.SemaphoreType.DMA((T,))],
        compiler_params=pltpu.CompilerParams(has_side_effects=True),
    )(*grads)


def rs_pair_sum(grad, recv, kind, core, name):
    g3, r3 = _as3d(grad), _as3d(recv)
    A, R, C = r3.shape
    n = grad.shape[1 if kind == "stack" else 0]
    hn = n // 2
    tr = _row_tile(R, C)

    def body(core_ref, g_ref, r_ref, f_ref, b_ref):
        s = g_ref[...] + r_ref[...]
        f_ref[...] = s
        b_ref[...] = s.astype(BF16)

    def g_map(a, i, core_ref):
        return ((a // hn) * n + core_ref[0] * hn + a % hn, i, 0)

    spec = pl.BlockSpec((1, tr, C), lambda a, i, core_ref: (a, i, 0))
    f, b = pl.pallas_call(
        body, name=name,
        grid_spec=pltpu.PrefetchScalarGridSpec(
            num_scalar_prefetch=1, grid=(A, R // tr),
            in_specs=[pl.BlockSpec((1, tr, C), g_map), spec], out_specs=[spec, spec]),
        out_shape=[jax.ShapeDtypeStruct((A, R, C), F32), jax.ShapeDtypeStruct((A, R, C), BF16)],
        compiler_params=_params(("parallel", "parallel")),
    )(core, g3, r3)
    return f.reshape(recv.shape), b.reshape(recv.shape)


def rs_exchange(pair_f32, pair_bf16, kinds):
    T = len(pair_f32)

    def shard_shape(t):
        full = pair_f32[t].shape
        if kinds[t] == "stack":
            return full[1:]
        if kinds[t] == "row":
            return (full[0], full[1] // 4, full[2])
        return (full[0], full[1], full[2] // 4)

    def body(*refs):
        pf, pb = refs[:T], refs[T:2 * T]
        own, got = refs[2 * T:3 * T], refs[3 * T:4 * T]
        send_sems, recv_sems, local_sems = refs[4 * T:]
        x, y, c, chips = _place()
        me = _chip_id((x, y))
        local, sends = [], []
        for t in range(T):
            hn = own[t].shape[0]
            cp = pltpu.make_async_copy(_shard_view(pf[t], kinds[t], me, 0, hn), own[t], local_sems.at[t])
            cp.start()
            local.append(cp)
            for j, chip in enumerate(chips):
                cp = _remote(_shard_view(pb[t], kinds[t], _chip_id(chip), 0, hn), got[t].at[j],
                             send_sems.at[3 * t + j], recv_sems.at[3 * t + j], (*chip, c))
                cp.start()
                sends.append(cp)
        for cp in sends:
            cp.wait()
        for cp in local:
            cp.wait()

    return pl.pallas_call(
        body, name="rs_exchange", in_specs=[ANY] * (2 * T), out_specs=[ANY] * (2 * T),
        out_shape=[jax.ShapeDtypeStruct(shard_shape(t), F32) for t in range(T)]
        + [jax.ShapeDtypeStruct((3,) + shard_shape(t), BF16) for t in range(T)],
        scratch_shapes=[pltpu.SemaphoreType.DMA((3 * T,)), pltpu.SemaphoreType.DMA((3 * T,)),
                        pltpu.SemaphoreType.DMA((T,))],
        compiler_params=pltpu.CompilerParams(has_side_effects=True),
    )(*pair_f32, *pair_bf16)


def rs_total(own, got, name):
    A, R, C = own.shape
    tr = _row_tile(R, C)

    def body(o_ref, g_ref, t_ref):
        t_ref[...] = ((o_ref[...] + g_ref[0].astype(F32)) + g_ref[1].astype(F32)) + g_ref[2].astype(F32)

    spec = pl.BlockSpec((1, tr, C), lambda a, i: (a, i, 0))
    return pl.pallas_call(
        body, name=name, grid=(A, R // tr),
        in_specs=[spec, pl.BlockSpec((3, 1, tr, C), lambda a, i: (0, a, i, 0))], out_specs=spec,
        out_shape=jax.ShapeDtypeStruct((A, R, C), F32), compiler_params=_params(("parallel", "parallel")),
    )(own, got)


def rs_share_halves(totals):
    T = len(totals)

    def body(*refs):
        ins, outs = refs[:T], refs[T:2 * T]
        send_sems, recv_sems, local_sems = refs[2 * T:]
        x, y, c, _ = _place()
        copies, local = [], []
        for t in range(T):
            hn = ins[t].shape[0]
            mine = outs[t].at[pl.ds(c * hn, hn)]
            cp = pltpu.make_async_copy(ins[t], mine, local_sems.at[t])
            cp.start()
            local.append(cp)
            cp = _remote(ins[t], mine, send_sems.at[t], recv_sems.at[t], (x, y, 1 - c))
            cp.start()
            copies.append(cp)
        for cp in copies:
            cp.wait()
        for cp in local:
            cp.wait()

    return pl.pallas_call(
        body, name="rs_share_halves", in_specs=[ANY] * T, out_specs=[ANY] * T,
        out_shape=[jax.ShapeDtypeStruct((2 * a.shape[0],) + a.shape[1:], a.dtype) for a in totals],
        scratch_shapes=[pltpu.SemaphoreType.DMA((T,)), pltpu.SemaphoreType.DMA((T,)), pltpu.SemaphoreType.DMA((T,))],
        compiler_params=pltpu.CompilerParams(has_side_effects=True),
    )(*totals)


def reduce_scatter_grads(grads, kinds, core):
    recv = rs_swap_halves(grads, kinds)
    pf, pb = [], []
    for t, (g, r, k) in enumerate(zip(grads, recv, kinds)):
        f, b = rs_pair_sum(g, r, k, core, "rs_pair_sum_%d" % t)
        pf.append(f)
        pb.append(b)
    res = rs_exchange(pf, pb, kinds)
    own, got = res[:len(grads)], res[len(grads):]
    totals = [rs_total(o, g, "rs_total_%d" % t) for t, (o, g) in enumerate(zip(own, got))]
    return rs_share_halves(totals)


def _round_up(n, m):
    return (n + m - 1) // m * m


def _heads_col(a16):
    L = a16.shape[0]
    return jnp.transpose(a16.reshape(L, HEADS // 2, 2), (1, 0, 2))


def _heads_row(a16):
    L = a16.shape[0]
    return jnp.transpose(a16.reshape(L, HEADS // 2, 2), (1, 2, 0))


def local_step(x, target, meta, norm_g, w_proj, b_f, conv_w, conv_b, w_o, s5, s5_d, w_glu1, w_glu2,
               w_gate, w_up, w_down):
    S, D = x.shape
    depth = norm_g.shape[0]
    L = N_META + S
    Lp = _round_up(L, ROW_TILE)
    h = jnp.concatenate([meta, x, jnp.zeros((Lp - L, D), F32)], axis=0)
    qkv_blocks = 3 * ATTN_W // LANES
    conv_c = conv_w.shape[2]
    fg_block = qkv_blocks + 3 * conv_c // LANES
    saved = []

    for i in range(depth):
        g = norm_g[i]
        j = i // 2
        tag = "l%d_" % i
        st = {"h0": h}
        if i % 2 == 0:
            u = rmsnorm_fwd(h, g[0:1], BF16, tag + "norm0")
            proj = matmul([(u, w_proj[j])], "nn", F32, tag + "proj")
            cgate = gate_fwd(proj, fg_block, b_f[j], tag + "gate")
            c16 = cgate[:, :HEADS]
            attn, lse = attn_fwd(proj, _heads_col(c16), _heads_row(c16), tag + "attn")
            convo = conv_fwd(proj, qkv_blocks, conv_w[j], conv_b[j], tag + "conv")
            cat = jnp.concatenate([attn, convo], axis=1)
            m = matmul([(cat, w_o[j])], "nn", F32, tag + "wo")
            st.update(u=u, proj=proj, c16=c16, lse=lse, cat=cat)
        else:
            p = s5[j]
            u = rmsnorm_fwd(h, g[0:1], F32, tag + "norm0")
            y, gact, xs = s5_fwd(u, p["bmat"], p["cmat"], p["tab"], s5_d[j], tag + "s5")
            o1, o2, m = dual_matmul_act(gact, w_glu1[j], w_glu2[j], "glu", F32, tag + "glu")
            st.update(u=u, y=y, gact=gact, xs=xs, o1=o1, o2=o2)
        h1 = rmsnorm_fwd(m, g[1:2], F32, tag + "norm1", residual=h)
        u2 = rmsnorm_fwd(h1, g[2:3], BF16, tag + "norm2")
        a, b, hact = dual_matmul_act(u2, w_gate[i], w_up[i], "swiglu", BF16, tag + "ffn_in")
        f = matmul([(hact, w_down[i])], "nn", F32, tag + "ffn_out")
        h = rmsnorm_fwd(f, g[3:4], F32, tag + "norm3", residual=h1)
        st.update(m=m, h1=h1, u2=u2, a=a, b=b, hact=hact, f=f)
        saved.append(st)

    dy, loss = loss_and_grad(h[N_META:L], target, "loss")
    dh = jnp.concatenate([jnp.zeros((N_META, D), F32), dy, jnp.zeros((Lp - L, D), F32)], axis=0)

    grads = {k: [None] * (depth // 2) for k in
             ("w_proj", "b_f", "conv_w", "conv_b", "w_o", "s5_d", "w_glu1", "w_glu2", "s5_dB", "s5_dC", "s5_dlam")}
    grads.update({k: [None] * depth for k in ("w_gate", "w_up", "w_down", "norm_g")})

    for i in reversed(range(depth)):
        g = norm_g[i]
        j = i // 2
        tag = "l%d_b_" % i
        st = saved[i]
        df, dg3 = rmsnorm_bwd(st["f"], g[3:4], dh, BF16, tag + "norm3")
        grads["w_down"][i] = matmul([(st["hact"], df)], "tn", F32, tag + "dw_down")
        da, db = ffn_bwd_act(df, w_down[i], st["a"], st["b"], tag + "ffn_act")
        grads["w_gate"][i] = matmul([(st["u2"], da)], "tn", F32, tag + "dw_gate")
        grads["w_up"][i] = matmul([(st["u2"], db)], "tn", F32, tag + "dw_up")
        du2 = matmul([(da, w_gate[i]), (db, w_up[i])], "nt", F32, tag + "du2")
        dh1, dg2 = rmsnorm_bwd(st["h1"], g[2:3], du2, F32, tag + "norm2", add=dh)
        if i % 2 == 0:
            dm, dg1 = rmsnorm_bwd(st["m"], g[1:2], dh1, BF16, tag + "norm1")
            grads["w_o"][j] = matmul([(st["cat"], dm)], "tn", F32, tag + "dw_o")
            dcat = matmul([(dm, w_o[j])], "nt", F32, tag + "dcat")
            delta = attn_delta(dcat, st["cat"], tag + "delta")
            c16 = st["c16"]
            lse16 = jnp.transpose(st["lse"], (1, 0, 2)).reshape(Lp, HEADS)
            dq, dk, dv, dcq, dck = attn_bwd(st["proj"], dcat, _heads_row(lse16), _heads_row(delta[:, :HEADS]),
                                            _heads_row(c16), _heads_col(c16), tag + "attn")
            dc16 = (jnp.transpose(dcq, (2, 0, 1)).reshape(Lp, HEADS)
                    + jnp.transpose(dck, (1, 0, 2)).reshape(Lp, HEADS))
            dc = jnp.pad(dc16, ((0, 0), (0, LANES - HEADS)))
            dfg, dbf = gate_bwd(st["proj"], fg_block, b_f[j], dc, tag + "gate")
            dgb, dgc, dxc, dcw, dcb = conv_bwd(st["proj"], qkv_blocks, conv_w[j], conv_b[j], dcat,
                                               ATTN_W // LANES, tag + "conv")
            dproj = jnp.concatenate([dq, dk, dv, dgb, dgc, dxc, dfg.astype(BF16)], axis=1)
            grads["w_proj"][j] = matmul([(st["u"], dproj)], "tn", F32, tag + "dw_proj")
            du = matmul([(dproj, w_proj[j])], "nt", F32, tag + "du")
            grads["b_f"][j], grads["conv_w"][j], grads["conv_b"][j] = dbf, dcw, dcb
        else:
            p = s5[j]
            dmix, dg1 = rmsnorm_bwd(st["m"], g[1:2], dh1, F32, tag + "norm1")
            do1, do2 = glu_bwd_act(dmix, st["o1"], st["o2"], tag + "glu_act")
            grads["w_glu1"][j] = matmul([(st["gact"], do1)], "tn", F32, tag + "dw_glu1")
            grads["w_glu2"][j] = matmul([(st["gact"], do2)], "tn", F32, tag + "dw_glu2")
            dgact = matmul([(do1, w_glu1[j]), (do2, w_glu2[j])], "nt", F32, tag + "dgact")
            du, dC, dB, dlam, dd = s5_bwd(dgact, st["y"], st["u"], st["xs"], p["cmat_t"], p["bmat_t"], p["rtab"],
                                          s5_d[j], tag + "s5")
            grads["s5_dB"][j], grads["s5_dC"][j], grads["s5_dlam"][j], grads["s5_d"][j] = dB, dC, dlam, dd
        dh, dg0 = rmsnorm_bwd(st["h0"], g[0:1], du, F32, tag + "norm0", add=dh1)
        grads["norm_g"][i] = jnp.concatenate([dg0, dg1, dg2, dg3], axis=0)

    grads["meta"] = dh[:N_META]
    return loss, dh[N_META:L], grads


def _pack(arrs):
    flat = jnp.concatenate([a.reshape(-1).astype(F32) for a in arrs])
    n = flat.shape[0]
    rows = _round_up(_round_up(n, LANES) // LANES, SUBLANES)
    return jnp.pad(flat, (0, rows * LANES - n)).reshape(rows, LANES)


def _unpack(buf, shapes):
    flat = buf.reshape(-1)
    out, off = [], 0
    for s in shapes:
        size = math.prod(s)
        out.append(flat[off:off + size].reshape(s))
        off += size
    return out


def device_step(x, target, P):
    D = x.shape[-1]
    n_even, n_odd = P["ab_w_in"].shape[0], P["s5_w_glu1"].shape[0]
    width = P["ab_w_in"].shape[2]
    conv_c = P["ab_conv_b"].shape[1]
    width_p = _round_up(width, LANES)
    fg0 = 3 * ATTN_W
    nqc = width - fg0 - HEADS
    w_in = P["ab_w_in"]
    w_proj = jnp.concatenate([w_in[:, :, :fg0], w_in[:, :, fg0 + HEADS:], w_in[:, :, fg0:fg0 + HEADS],
                              jnp.zeros((n_even, D, width_p - width), BF16)], axis=2)
    b_f_pad = jnp.pad(P["ab_b_f"], ((0, 0), (0, LANES - HEADS))).reshape(n_even, 1, LANES)

    s5, s5_vjps = [], []
    for j in range(n_odd):
        disc, vjp = jax.vjp(_s5_discretize, P["s5_a_re"][j], P["s5_a_im"][j], P["s5_log_step"][j],
                            P["s5_b_re"][j], P["s5_b_im"][j])
        lb_re, lb_im, bb_re, bb_im = disc
        tab, rtab = _s5_tables(lb_re, lb_im)
        bmat, cmat = _s5_block_mats(bb_re, bb_im, P["s5_c_re"][j], P["s5_c_im"][j])
        s5.append(dict(tab=tab, rtab=rtab, bmat=bmat.astype(BF16), cmat=cmat.astype(BF16),
                       bmat_t=jnp.transpose(bmat, (0, 2, 1)).astype(BF16),
                       cmat_t=jnp.transpose(cmat, (0, 2, 1)).astype(BF16)))
        s5_vjps.append(vjp)

    loss, grad_x, G = local_step(
        x, target, P["meta_tokens"], P["norm_g"], w_proj, b_f_pad, P["ab_conv_w"],
        P["ab_conv_b"].reshape(n_even, 1, conv_c), P["ab_w_o"], s5, P["s5_d"].reshape(n_odd, 1, D),
        P["s5_w_glu1"], P["s5_w_glu2"], P["ffn_w_gate"], P["ffn_w_up"], P["ffn_w_down"])

    dproj = jnp.stack(G["w_proj"])
    out = {
        "ab_w_in": jnp.concatenate([dproj[:, :, :fg0], dproj[:, :, fg0 + nqc:fg0 + nqc + HEADS],
                                    dproj[:, :, fg0:fg0 + nqc]], axis=2),
        "ab_w_o": jnp.stack(G["w_o"]), "s5_w_glu1": jnp.stack(G["w_glu1"]), "s5_w_glu2": jnp.stack(G["w_glu2"]),
        "ffn_w_gate": jnp.stack(G["w_gate"]), "ffn_w_up": jnp.stack(G["w_up"]), "ffn_w_down": jnp.stack(G["w_down"]),
        "meta_tokens": G["meta"],
        "norm_g": jnp.stack(G["norm_g"]),
        "ab_b_f": jnp.stack([b[0, :HEADS] for b in G["b_f"]]),
        "ab_conv_w": jnp.stack(G["conv_w"]),
        "ab_conv_b": jnp.stack([b[0] for b in G["conv_b"]]),
        "s5_d": jnp.stack([d[0] for d in G["s5_d"]]),
    }
    s5g = {k: [] for k in ("s5_a_re", "s5_a_im", "s5_log_step", "s5_b_re", "s5_b_im", "s5_c_re", "s5_c_im")}
    for j in range(n_odd):
        dbb_re, dbb_im, dc_re, dc_im, dl_re, dl_im = _s5_unblock(G["s5_dB"][j], G["s5_dC"][j], G["s5_dlam"][j])
        da_re, da_im, dls, db_re, db_im = s5_vjps[j]((dl_re, dl_im, dbb_re, dbb_im))
        for k, val in zip(s5g, (da_re, da_im, dls, db_re, db_im, dc_re, dc_im)):
            s5g[k].append(val)
    out.update({k: jnp.stack(v) for k, v in s5g.items()})
    return loss, grad_x, out


def kernel(x, meta_tokens, norm_g, ab_w_in, ab_b_f, ab_conv_w, ab_conv_b, ab_w_o, s5_a_re, s5_a_im, s5_log_step, s5_b_re, s5_b_im, s5_c_re, s5_c_im, s5_d, s5_w_glu1, s5_w_glu2, ffn_w_gate, ffn_w_up, ffn_w_down, loss_target, m_meta_tokens, m_norm_g, m_ab_w_in, m_ab_b_f, m_ab_conv_w, m_ab_conv_b, m_ab_w_o, m_s5_a_re, m_s5_a_im, m_s5_log_step, m_s5_b_re, m_s5_b_im, m_s5_c_re, m_s5_c_im, m_s5_d, m_s5_w_glu1, m_s5_w_glu2, m_ffn_w_gate, m_ffn_w_up, m_ffn_w_down, v_meta_tokens, v_norm_g, v_ab_w_in, v_ab_b_f, v_ab_conv_w, v_ab_conv_b, v_ab_w_o, v_s5_a_re, v_s5_a_im, v_s5_log_step, v_s5_b_re, v_s5_b_im, v_s5_c_re, v_s5_c_im, v_s5_d, v_s5_w_glu1, v_s5_w_glu2, v_ffn_w_gate, v_ffn_w_up, v_ffn_w_down):
    names = ["meta_tokens", "norm_g", "ab_w_in", "ab_b_f", "ab_conv_w", "ab_conv_b", "ab_w_o", "s5_a_re", "s5_a_im",
             "s5_log_step", "s5_b_re", "s5_b_im", "s5_c_re", "s5_c_im", "s5_d", "s5_w_glu1", "s5_w_glu2",
             "ffn_w_gate", "ffn_w_up", "ffn_w_down"]
    W = dict(zip(names, [meta_tokens, norm_g, ab_w_in, ab_b_f, ab_conv_w, ab_conv_b, ab_w_o, s5_a_re, s5_a_im,
                         s5_log_step, s5_b_re, s5_b_im, s5_c_re, s5_c_im, s5_d, s5_w_glu1, s5_w_glu2,
                         ffn_w_gate, ffn_w_up, ffn_w_down]))
    Mo = dict(zip(names, [m_meta_tokens, m_norm_g, m_ab_w_in, m_ab_b_f, m_ab_conv_w, m_ab_conv_b, m_ab_w_o, m_s5_a_re,
                          m_s5_a_im, m_s5_log_step, m_s5_b_re, m_s5_b_im, m_s5_c_re, m_s5_c_im, m_s5_d, m_s5_w_glu1,
                          m_s5_w_glu2, m_ffn_w_gate, m_ffn_w_up, m_ffn_w_down]))
    Vo = dict(zip(names, [v_meta_tokens, v_norm_g, v_ab_w_in, v_ab_b_f, v_ab_conv_w, v_ab_conv_b, v_ab_w_o, v_s5_a_re,
                          v_s5_a_im, v_s5_log_step, v_s5_b_re, v_s5_b_im, v_s5_c_re, v_s5_c_im, v_s5_d, v_s5_w_glu1,
                          v_s5_w_glu2, v_ffn_w_gate, v_ffn_w_up, v_ffn_w_down]))
    D = x.shape[-1]
    n_even, n_odd, depth = ab_w_in.shape[0], s5_w_glu1.shape[0], ffn_w_gate.shape[0]
    chip = 2 * lax.axis_index("x") + lax.axis_index("y")
    core = lax.axis_index("c").astype(jnp.int32).reshape(1)

    big = ["ab_w_in", "ab_w_o", "s5_w_glu1", "s5_w_glu2", "ffn_w_gate", "ffn_w_up", "ffn_w_down"]
    kinds = ["stack", "row", "row", "row", "col", "col", "row"]
    shards_bf16 = [cast_bf16(W[k], "cast_" + k) for k in big]
    full = dict(zip(big, allgather_weights(shards_bf16, kinds)))
    width = 4 * ab_w_in.shape[2]
    full["ab_w_in"] = jnp.transpose(full["ab_w_in"], (1, 2, 0, 3)).reshape(n_even, D, width)
    g_meta, g_norm, g_convw, g_s5d = allgather_small([meta_tokens, norm_g, ab_conv_w, s5_d])
    full["meta_tokens"] = jnp.transpose(g_meta, (1, 0, 2)).reshape(N_META, D)
    full["norm_g"] = jnp.transpose(g_norm, (1, 2, 0, 3)).reshape(depth, 4, D)
    full["ab_conv_w"] = jnp.transpose(g_convw, (1, 2, 0, 3)).reshape(n_even, CONV_K, -1)
    full["s5_d"] = jnp.transpose(g_s5d, (1, 0, 2)).reshape(n_odd, D)
    for k in names:
        full.setdefault(k, W[k])

    loss, grad_x, G = device_step(x[0], loss_target[0], full)

    G["ab_w_in"] = jnp.transpose(G["ab_w_in"].reshape(n_even, D, 4, width // 4), (2, 0, 1, 3))
    reduced = dict(zip(big, reduce_scatter_grads([G[k] for k in big], kinds, core)))

    small_w = [k for k in names if k not in big]
    small_names = ["loss"] + small_w
    G["loss"] = loss
    summed = dict(zip(small_names, _unpack(allreduce_small(_pack([G[k] for k in small_names])),
                                           [G[k].shape for k in small_names])))
    loss_out = summed["loss"].reshape(())
    for k in ("meta_tokens", "norm_g", "ab_conv_w", "s5_d"):
        n_last = W[k].shape[-1]
        summed[k] = lax.dynamic_slice_in_dim(summed[k], chip * n_last, n_last, axis=summed[k].ndim - 1)
    shapes = [W[k].shape for k in small_w]
    d_s, m_s, v_s = adamw(_pack([W[k] for k in small_w])[None], _pack([summed[k] for k in small_w])[None],
                          _pack([Mo[k] for k in small_w])[None], _pack([Vo[k] for k in small_w])[None], "adamw_small")
    delta = dict(zip(small_w, _unpack(d_s, shapes)))
    new_m = dict(zip(small_w, _unpack(m_s, shapes)))
    new_v = dict(zip(small_w, _unpack(v_s, shapes)))
    grad = {k: summed[k] for k in small_w}
    for k in big:
        grad[k] = reduced[k]
        delta[k], new_m[k], new_v[k] = adamw(W[k], reduced[k], Mo[k], Vo[k], "adamw_" + k)

    return (loss_out, grad_x[None], *[grad[k] for k in names], *[delta[k] for k in names],
            *[new_m[k] for k in names], *[new_v[k] for k in names])
```

```python
import functools
import math

import jax
import jax.numpy as jnp
from jax import lax
from jax.experimental import pallas as pl
from jax.experimental.pallas import tpu as pltpu

F32 = jnp.float32
BF16 = jnp.bfloat16

N_META = 16
HEADS = 16
HEAD_DIM = 64
ATTN_W = HEADS * HEAD_DIM
CONV_K = 3
S5_GROUP = 16
S5_STATE = 64
S5_MIN_DECAY = 1e-4
NORM_EPS = 1e-6
ADAM_LR = 0.001
ADAM_B1 = 0.9
ADAM_B2 = 0.999
ADAM_EPS = 1e-08
ADAM_WD = 0.01
ADAM_STEP = 10

LANES = 128
SUBLANES = 8
VMEM_LIMIT = 56 * 1024 * 1024
VMEM_TILE_BUDGET = 34 * 1024 * 1024
ROW_TILE = 384
ATTN_ROWS = 128
S5_BLOCK_GROUPS = LANES // S5_GROUP
S5_BLOCK_STATES = S5_BLOCK_GROUPS * S5_STATE
NEG_BIG = -1e30

MESH = pl.DeviceIdType.MESH
ANY = pl.BlockSpec(memory_space=pl.ANY)
VMEM_SPEC = pl.BlockSpec(memory_space=pltpu.VMEM)


def _params(sem=None):
    return pltpu.CompilerParams(dimension_semantics=sem, vmem_limit_bytes=VMEM_LIMIT)


def _div_tile(n, prefs):
    for p in prefs:
        if n % p == 0:
            return p
    return n


def _row_tile(rows, cols, itemsize=4, limit=2 * 1024 * 1024):
    for p in (512, 256, 128, 64, 32, 16):
        if rows % p == 0 and p * cols * itemsize <= limit:
            return p
    return 16 if rows % 16 == 0 else rows


def _tile_cands(n):
    c = [d for d in range(LANES, min(n, 2048) + 1, LANES) if n % d == 0]
    if not c or n <= 2048 and n not in c:
        c.append(n)
    return sorted(set(c), reverse=True)


def _mm_tiles(M, N, K, a_bytes, b_bytes, o_bytes, npairs):
    best = None
    for tk in _tile_cands(K):
        for tm in _tile_cands(M):
            for tn in _tile_cands(N):
                mem = npairs * 2 * (tm * tk * a_bytes + tk * tn * b_bytes) + 2 * tm * tn * o_bytes + tm * tn * 4
                mem += npairs * ((tm * tk * 2 if a_bytes == 4 else 0) + (tk * tn * 2 if b_bytes == 4 else 0))
                if mem > VMEM_TILE_BUDGET:
                    continue
                key = (tm * tn * tk, tk, tn)
                if best is None or key > best[0]:
                    best = (key, (tm, tn, tk))
    assert best is not None, (M, N, K)
    return best[1]


_DIMS = {"nn": (((1,), (0,)), ((), ())), "nt": (((1,), (1,)), ((), ())), "tn": (((0,), (0,)), ((), ()))}


def _layered(op):
    return op if isinstance(op, tuple) else (op, None)


def _layer_spec(block, index_map, layer):
    if layer is None:
        return pl.BlockSpec(block, index_map)
    return pl.BlockSpec((None,) + block, lambda *g: (layer,) + index_map(*g))


def matmul(pairs, kind, out_dtype, name, out_stack=None):
    ops = [(_layered(a), _layered(b)) for a, b in pairs]
    a0, b0 = ops[0][0][0], ops[0][1][0]
    ash, bsh = a0.shape[-2:], b0.shape[-2:]
    if kind == "nn":
        (M, K), N = ash, bsh[1]
    elif kind == "nt":
        (M, K), N = ash, bsh[0]
    else:
        (K, M), N = ash, bsh[1]
    tm, tn, tk = _mm_tiles(M, N, K, a0.dtype.itemsize, b0.dtype.itemsize, jnp.dtype(out_dtype).itemsize, len(pairs))
    nk = K // tk
    dims = _DIMS[kind]
    npairs = len(pairs)
    previous = out_stack[2] if out_stack is not None else None
    n_in = 2 * npairs + (1 if previous is not None else 0)

    def body(*refs):
        ins, o_ref = refs[:2 * npairs], refs[n_in]
        part = None
        for p in range(npairs):
            d = lax.dot_general(ins[2 * p][...].astype(BF16), ins[2 * p + 1][...].astype(BF16), dims,
                                preferred_element_type=F32)
            part = d if part is None else part + d
        if nk == 1:
            o_ref[...] = part.astype(o_ref.dtype)
        else:
            acc_ref = refs[n_in + 1]
            k = pl.program_id(2)

            @pl.when(k == 0)
            def _():
                acc_ref[...] = part

            @pl.when(k > 0)
            def _():
                acc_ref[...] += part

            @pl.when(k == nk - 1)
            def _():
                o_ref[...] = acc_ref[...].astype(o_ref.dtype)

    if kind == "nn":
        a_blk, a_map = (tm, tk), lambda j, i, k: (i, k)
        b_blk, b_map = (tk, tn), lambda j, i, k: (k, j)
    elif kind == "nt":
        a_blk, a_map = (tm, tk), lambda j, i, k: (i, k)
        b_blk, b_map = (tn, tk), lambda j, i, k: (j, k)
    else:
        a_blk, a_map = (tk, tm), lambda j, i, k: (k, i)
        b_blk, b_map = (tk, tn), lambda j, i, k: (k, j)
    in_specs, flat = [], []
    for (a, la), (b, lb) in ops:
        in_specs += [_layer_spec(a_blk, a_map, la), _layer_spec(b_blk, b_map, lb)]
        flat += [a, b]
    out_map = lambda j, i, k: (i, j)
    aliases = {}
    if out_stack is None:
        out_spec, out_shape = pl.BlockSpec((tm, tn), out_map), (M, N)
    else:
        out_spec, out_shape = _layer_spec((tm, tn), out_map, out_stack[1]), (out_stack[0], M, N)
        if previous is not None:
            in_specs.append(ANY)
            flat.append(previous)
            aliases = {2 * npairs: 0}
    return pl.pallas_call(
        body, name=name,
        grid=(N // tn, M // tm, nk),
        in_specs=in_specs,
        out_specs=out_spec,
        out_shape=jax.ShapeDtypeStruct(out_shape, out_dtype),
        scratch_shapes=[] if nk == 1 else [pltpu.VMEM((tm, tn), F32)],
        input_output_aliases=aliases,
        compiler_params=_params(("parallel", "parallel", "arbitrary")),
    )(*flat)


def _sigmoid(x):
    return 1.0 / (1.0 + jnp.exp(-x))


def dual_matmul_act(x, w1, w2, act, out_dtype, name):
    M, K = x.shape
    (w1, l1), (w2, l2) = _layered(w1), _layered(w2)
    N = w1.shape[-1]
    tm = _div_tile(M, (ROW_TILE,))
    tn = _div_tile(N, (1408, 1024, 512, 256, 128))

    def body(x_ref, w1_ref, w2_ref, o1_ref, o2_ref, out_ref):
        xv = x_ref[...]
        o1 = jnp.dot(xv, w1_ref[...], preferred_element_type=F32)
        o2 = jnp.dot(xv, w2_ref[...], preferred_element_type=F32)
        o1_ref[...] = o1.astype(BF16)
        o2_ref[...] = o2.astype(BF16)
        if act == "swiglu":
            out = o1 * _sigmoid(o1) * o2
        else:
            out = o1 * _sigmoid(o2)
        out_ref[...] = out.astype(out_ref.dtype)

    w_map = lambda j, i: (0, j)
    o_spec = pl.BlockSpec((tm, tn), lambda j, i: (i, j))
    return pl.pallas_call(
        body, name=name, grid=(N // tn, M // tm),
        in_specs=[pl.BlockSpec((tm, K), lambda j, i: (i, 0)), _layer_spec((K, tn), w_map, l1),
                  _layer_spec((K, tn), w_map, l2)],
        out_specs=[o_spec, o_spec, o_spec],
        out_shape=[jax.ShapeDtypeStruct((M, N), BF16), jax.ShapeDtypeStruct((M, N), BF16),
                   jax.ShapeDtypeStruct((M, N), out_dtype)],
        compiler_params=_params(("parallel", "parallel")),
    )(x, w1, w2)


def ffn_bwd_act(df, wd, a, b, name):
    M, K = df.shape
    wd, layer = _layered(wd)
    N = wd.shape[-2]
    tm = _div_tile(M, (ROW_TILE,))
    tn = _div_tile(N, (1408, 1024, 512, 256, 128))

    def body(df_ref, wd_ref, a_ref, b_ref, da_ref, db_ref):
        dh = lax.dot_general(df_ref[...], wd_ref[...], _DIMS["nt"], preferred_element_type=F32)
        av = a_ref[...].astype(F32)
        bv = b_ref[...].astype(F32)
        sig = _sigmoid(av)
        silu = av * sig
        da_ref[...] = (dh * bv * (sig + silu * (1.0 - sig))).astype(BF16)
        db_ref[...] = (dh * silu).astype(BF16)

    t_spec = pl.BlockSpec((tm, tn), lambda j, i: (i, j))
    return pl.pallas_call(
        body, name=name, grid=(N // tn, M // tm),
        in_specs=[pl.BlockSpec((tm, K), lambda j, i: (i, 0)), _layer_spec((tn, K), lambda j, i: (j, 0), layer),
                  t_spec, t_spec],
        out_specs=[t_spec, t_spec],
        out_shape=[jax.ShapeDtypeStruct((M, N), BF16)] * 2,
        compiler_params=_params(("parallel", "parallel")),
    )(df, wd, a, b)


def glu_bwd_act(dout, o1, o2, name):
    M, N = dout.shape
    tm = _div_tile(M, (ROW_TILE,))

    def body(d_ref, o1_ref, o2_ref, d1_ref, d2_ref):
        d = d_ref[...].astype(F32)
        sig = _sigmoid(o2_ref[...].astype(F32))
        d1_ref[...] = (d * sig).astype(BF16)
        d2_ref[...] = (d * o1_ref[...].astype(F32) * sig * (1.0 - sig)).astype(BF16)

    spec = pl.BlockSpec((tm, N), lambda i: (i, 0))
    return pl.pallas_call(
        body, name=name, grid=(M // tm,), in_specs=[spec] * 3, out_specs=[spec] * 2,
        out_shape=[jax.ShapeDtypeStruct((M, N), BF16)] * 2,
        compiler_params=_params(("parallel",)),
    )(dout, o1, o2)


def rmsnorm_fwd(x, g, out_dtype, name, residual=None):
    L, D = x.shape
    tr = _div_tile(L, (ROW_TILE,))
    has_res = residual is not None

    def body(*refs):
        x_ref, g_ref = refs[0], refs[1]
        o_ref = refs[-1]
        xv = x_ref[...]
        r = lax.rsqrt(jnp.mean(xv * xv, axis=-1, keepdims=True) + NORM_EPS)
        y = xv * r * g_ref[...]
        if has_res:
            y = refs[2][...] + y
        o_ref[...] = y.astype(o_ref.dtype)

    row = pl.BlockSpec((tr, D), lambda i: (i, 0))
    gsp = pl.BlockSpec((1, D), lambda i: (0, 0))
    args = (x, g) + ((residual,) if has_res else ())
    return pl.pallas_call(
        body, name=name, grid=(L // tr,), in_specs=[row, gsp] + ([row] if has_res else []), out_specs=row,
        out_shape=jax.ShapeDtypeStruct((L, D), out_dtype), compiler_params=_params(("parallel",)),
    )(*args)


def rmsnorm_bwd(x, g, dy, out_dtype, name, add=None, dy2=None):
    L, D = x.shape
    tr = _div_tile(L, (ROW_TILE,))
    has_add = add is not None
    has_dy2 = dy2 is not None

    def body(*refs):
        x_ref, g_ref, dy_ref = refs[0], refs[1], refs[2]
        dx_ref, dg_ref = refs[-2], refs[-1]
        xv = x_ref[...]
        dyv = dy_ref[...].astype(F32)
        if has_dy2:
            dyv = dyv + refs[3][...].astype(F32)
        r = lax.rsqrt(jnp.mean(xv * xv, axis=-1, keepdims=True) + NORM_EPS)
        t = dyv * g_ref[...]
        dx = r * t - xv * (r * r * r) * jnp.mean(xv * t, axis=-1, keepdims=True)
        if has_add:
            dx = refs[3 + has_dy2][...] + dx
        dx_ref[...] = dx.astype(dx_ref.dtype)
        dgp = jnp.sum(dyv * xv * r, axis=0, keepdims=True)

        @pl.when(pl.program_id(0) == 0)
        def _():
            dg_ref[...] = dgp

        @pl.when(pl.program_id(0) > 0)
        def _():
            dg_ref[...] += dgp

    row = pl.BlockSpec((tr, D), lambda i: (i, 0))
    gsp = pl.BlockSpec((1, D), lambda i: (0, 0))
    args = (x, g, dy) + ((dy2,) if has_dy2 else ()) + ((add,) if has_add else ())
    return pl.pallas_call(
        body, name=name, grid=(L // tr,), in_specs=[row, gsp] + [row] * (len(args) - 2),
        out_specs=[row, gsp],
        out_shape=[jax.ShapeDtypeStruct((L, D), out_dtype), jax.ShapeDtypeStruct((1, D), F32)],
        compiler_params=_params(("arbitrary",)),
    )(*args)


def _gate_z(fg_ref, b_ref):
    return fg_ref[...] + b_ref[...]


def gate_fwd(fg_src, col_block, b, name):
    L = fg_src.shape[0]
    T = _div_tile(L, (ROW_TILE,))

    def body(fg_ref, b_ref, c_ref, carry):
        @pl.when(pl.program_id(0) == 0)
        def _():
            carry[...] = jnp.zeros_like(carry)

        z = _gate_z(fg_ref, b_ref)
        logf = jnp.minimum(z, 0.0) - jnp.log(1.0 + jnp.exp(-jnp.abs(z)))
        tri = (lax.broadcasted_iota(jnp.int32, (T, T), 1) <= lax.broadcasted_iota(jnp.int32, (T, T), 0)).astype(F32)
        c = jnp.dot(tri, logf, precision=lax.Precision.HIGHEST, preferred_element_type=F32) + carry[...]
        c_ref[...] = c
        carry[...] = c[T - 1:T, :]

    return pl.pallas_call(
        body, name=name, grid=(L // T,),
        in_specs=[pl.BlockSpec((T, LANES), lambda i: (i, col_block)), pl.BlockSpec((1, LANES), lambda i: (0, 0))],
        out_specs=pl.BlockSpec((T, LANES), lambda i: (i, 0)),
        out_shape=jax.ShapeDtypeStruct((L, LANES), F32),
        scratch_shapes=[pltpu.VMEM((1, LANES), F32)],
        compiler_params=_params(("arbitrary",)),
    )(fg_src, b)


def gate_bwd(fg_src, col_block, b, dc, name):
    L = fg_src.shape[0]
    T = _div_tile(L, (ROW_TILE,))
    nb = L // T

    def body(fg_ref, b_ref, dc_ref, dfg_ref, db_ref, carry):
        @pl.when(pl.program_id(0) == 0)
        def _():
            carry[...] = jnp.zeros_like(carry)
            db_ref[...] = jnp.zeros_like(db_ref)

        z = _gate_z(fg_ref, b_ref)
        dcv = dc_ref[...]
        tri = (lax.broadcasted_iota(jnp.int32, (T, T), 1) >= lax.broadcasted_iota(jnp.int32, (T, T), 0)).astype(F32)
        dlogf = jnp.dot(tri, dcv, precision=lax.Precision.HIGHEST, preferred_element_type=F32) + carry[...]
        dfg = dlogf * _sigmoid(-z)
        dfg_ref[...] = dfg
        db_ref[...] += jnp.sum(dfg, axis=0, keepdims=True)
        carry[...] = dlogf[0:1, :]

    return pl.pallas_call(
        body, name=name, grid=(nb,),
        in_specs=[pl.BlockSpec((T, LANES), lambda i: (nb - 1 - i, col_block)),
                  pl.BlockSpec((1, LANES), lambda i: (0, 0)),
                  pl.BlockSpec((T, LANES), lambda i: (nb - 1 - i, 0))],
        out_specs=[pl.BlockSpec((T, LANES), lambda i: (nb - 1 - i, 0)), pl.BlockSpec((1, LANES), lambda i: (0, 0))],
        out_shape=[jax.ShapeDtypeStruct((L, LANES), F32), jax.ShapeDtypeStruct((1, LANES), F32)],
        scratch_shapes=[pltpu.VMEM((1, LANES), F32)],
        compiler_params=_params(("arbitrary",)),
    )(fg_src, b, dc)


def attn_fwd(proj, cq_col, ck_row, name):
    L = proj.shape[0]
    T = _div_tile(L, (ROW_TILE,))
    nq = L // T
    npair = HEADS // 2
    scale = HEAD_DIM ** -0.5
    SUB = ATTN_ROWS
    nsub = T // SUB

    def body(q_ref, k_ref, v_ref, cq_ref, ck_ref, o_ref, lse_ref):
        qb = pl.program_id(1)
        rows = [slice(r * SUB, (r + 1) * SUB) for r in range(nsub)]
        head1 = lax.broadcasted_iota(jnp.int32, (SUB, LANES), 1) >= HEAD_DIM
        qs = [[jnp.where(head1 == (h == 1), q_ref[rs, :] * scale, 0.0).astype(BF16) for rs in rows] for h in range(2)]
        cqs = [[cq_ref[0, rs, h:h + 1] for rs in rows] for h in range(2)]

        def step(kb, carry, masked):
            ks = pl.multiple_of(kb * T, T)
            k = k_ref[pl.ds(ks, T), :]
            v = v_ref[pl.ds(ks, T), :]
            lane = lax.broadcasted_iota(jnp.int32, (T, LANES), 1)
            new = []
            for h in range(2):
                ck = ck_ref[0, h:h + 1, pl.ds(ks, T)]
                vh = jnp.where(lane == spare[h], 1.0, v).astype(BF16)
                for r in range(nsub):
                    m, acc = carry[h * nsub + r]
                    s = lax.dot_general(qs[h][r], k, _DIMS["nt"], preferred_element_type=F32) + cqs[h][r] - ck
                    if masked:
                        keep = (lax.broadcasted_iota(jnp.int32, (SUB, T), 1)
                                <= lax.broadcasted_iota(jnp.int32, (SUB, T), 0) + r * SUB)
                        s = jnp.where(keep, s, NEG_BIG)
                    m_new = jnp.maximum(m, jnp.max(s, axis=1, keepdims=True))
                    p = jnp.exp(s - m_new)
                    acc = jnp.exp(m - m_new) * acc + jnp.dot(p.astype(BF16), vh, preferred_element_type=F32)
                    new.append((m_new, acc))
            return tuple(new)

        spare = (HEAD_DIM, 0)
        one = (jnp.full((SUB, 1), NEG_BIG, F32), jnp.zeros((SUB, LANES), F32))
        carry = lax.fori_loop(0, qb, functools.partial(step, masked=False), (one,) * (2 * nsub))
        carry = step(qb, carry, True)
        out, lse = [], []
        for h in range(2):
            chains = carry[h * nsub:(h + 1) * nsub]
            ls = [acc[:, spare[h]:spare[h] + 1] for _, acc in chains]
            out.append(jnp.concatenate([acc / l for (_, acc), l in zip(chains, ls)], axis=0))
            lse.append(jnp.concatenate([m + jnp.log(l) for (m, _), l in zip(chains, ls)], axis=0))
        o_ref[...] = jnp.where(lax.broadcasted_iota(jnp.int32, (T, LANES), 1) >= HEAD_DIM, out[1], out[0]
                               ).astype(o_ref.dtype)
        lse_ref[0] = jnp.concatenate(lse, axis=1)

    return pl.pallas_call(
        body, name=name, grid=(npair, nq),
        in_specs=[pl.BlockSpec((T, LANES), lambda p, i: (i, p)),
                  pl.BlockSpec((L, LANES), lambda p, i: (0, npair + p)),
                  pl.BlockSpec((L, LANES), lambda p, i: (0, 2 * npair + p)),
                  pl.BlockSpec((1, T, 2), lambda p, i: (p, i, 0)),
                  pl.BlockSpec((1, 2, L), lambda p, i: (p, 0, 0))],
        out_specs=[pl.BlockSpec((T, LANES), lambda p, i: (i, p)), pl.BlockSpec((1, T, 2), lambda p, i: (p, i, 0))],
        out_shape=[jax.ShapeDtypeStruct((L, ATTN_W), BF16), jax.ShapeDtypeStruct((npair, L, 2), F32)],
        compiler_params=_params(("parallel", "parallel")),
    )(proj, proj, proj, cq_col, ck_row)


def attn_delta(dcat, cat, name):
    L = dcat.shape[0]
    T = _div_tile(L, (ROW_TILE,))

    def body(do_ref, o_ref, d_ref):
        prod = do_ref[...] * o_ref[...].astype(F32)
        sel = (lax.broadcasted_iota(jnp.int32, (ATTN_W, LANES), 0) // HEAD_DIM
               == lax.broadcasted_iota(jnp.int32, (ATTN_W, LANES), 1)).astype(F32)
        d_ref[...] = jnp.dot(prod, sel, precision=lax.Precision.HIGHEST, preferred_element_type=F32)

    return pl.pallas_call(
        body, name=name, grid=(L // T,),
        in_specs=[pl.BlockSpec((T, ATTN_W), lambda i: (i, 0)), pl.BlockSpec((T, ATTN_W), lambda i: (i, 0))],
        out_specs=pl.BlockSpec((T, LANES), lambda i: (i, 0)),
        out_shape=jax.ShapeDtypeStruct((L, LANES), F32),
        compiler_params=_params(("parallel",)),
    )(dcat, cat)


def attn_bwd(proj, dcat, lse_row, delta_row, cq_row, ck_col, name):
    L = proj.shape[0]
    T = _div_tile(L, (ROW_TILE,))
    nb = L // T
    npair = HEADS // 2
    scale = HEAD_DIM ** -0.5

    def body(q_ref, k_ref, v_ref, do_ref, lse_ref, dl_ref, cq_ref, ck_ref,
             dq_ref, dk_ref, dv_ref, dcq_ref, dck_ref, dq_acc, dcq_acc):
        kb = pl.program_id(1)

        @pl.when(kb == 0)
        def _():
            dq_acc[...] = jnp.zeros_like(dq_acc)
            dcq_acc[...] = jnp.zeros_like(dcq_acc)

        head1 = lax.broadcasted_iota(jnp.int32, (T, LANES), 1) >= HEAD_DIM
        ks = [jnp.where(head1 == (h == 1), k_ref[...] * scale, 0.0).astype(BF16) for h in range(2)]
        vs = [jnp.where(head1 == (h == 1), v_ref[...], 0.0).astype(BF16) for h in range(2)]
        cks = [ck_ref[0, :, h:h + 1] for h in range(2)]

        def step(qb, carry, masked):
            qs = pl.multiple_of(qb * T, T)
            q = q_ref[pl.ds(qs, T), :]
            do = do_ref[pl.ds(qs, T), :].astype(BF16)
            new, dq = [], None
            for h in range(2):
                dk, dv, dck = carry[h]
                lse = lse_ref[0, h:h + 1, pl.ds(qs, T)]
                dl = dl_ref[0, h:h + 1, pl.ds(qs, T)]
                cq = cq_ref[0, h:h + 1, pl.ds(qs, T)]
                st = lax.dot_general(ks[h], q, _DIMS["nt"], preferred_element_type=F32) + cq - cks[h]
                if masked:
                    keep = lax.broadcasted_iota(jnp.int32, (T, T), 0) <= lax.broadcasted_iota(jnp.int32, (T, T), 1)
                    st = jnp.where(keep, st, NEG_BIG)
                pt = jnp.exp(st - lse)
                dv = dv + jnp.dot(pt.astype(BF16), do, preferred_element_type=F32)
                dpt = lax.dot_general(vs[h], do, _DIMS["nt"], preferred_element_type=F32)
                dst = pt * (dpt - dl)
                dsb = dst.astype(BF16)
                dk = dk + jnp.dot(dsb, q, preferred_element_type=F32)
                part = lax.dot_general(dsb, ks[h], _DIMS["tn"], preferred_element_type=F32)
                dq = part if dq is None else dq + part
                dcq_acc[h:h + 1, pl.ds(qs, T)] += jnp.sum(dst, axis=0, keepdims=True)
                dck = dck + jnp.sum(dst, axis=1, keepdims=True)
                new.append((dk, dv, dck))
            dq_acc[pl.ds(qs, T), :] += dq
            return tuple(new)

        one = (jnp.zeros((T, LANES), F32), jnp.zeros((T, LANES), F32), jnp.zeros((T, 1), F32))
        carry = step(kb, (one, one), True)
        carry = lax.fori_loop(kb + 1, nb, functools.partial(step, masked=False), carry)
        (dk0, dv0, dck0), (dk1, dv1, dck1) = carry
        dk_ref[...] = (jnp.where(head1, dk1, dk0) * scale).astype(dk_ref.dtype)
        dv_ref[...] = jnp.where(head1, dv1, dv0).astype(dv_ref.dtype)
        dck_ref[0] = jnp.concatenate([-dck0, -dck1], axis=1)

        @pl.when(kb == nb - 1)
        def _():
            dq_ref[...] = dq_acc[...].astype(dq_ref.dtype)
            dcq_ref[0] = dcq_acc[...]

    full = lambda col: pl.BlockSpec((L, LANES), col)
    row_stat = pl.BlockSpec((1, 2, L), lambda p, i: (p, 0, 0))
    return pl.pallas_call(
        body, name=name, grid=(npair, nb),
        in_specs=[full(lambda p, i: (0, p)),
                  pl.BlockSpec((T, LANES), lambda p, i: (i, npair + p)),
                  pl.BlockSpec((T, LANES), lambda p, i: (i, 2 * npair + p)),
                  full(lambda p, i: (0, p)),
                  row_stat, row_stat, row_stat,
                  pl.BlockSpec((1, T, 2), lambda p, i: (p, i, 0))],
        out_specs=[full(lambda p, i: (0, p)),
                   pl.BlockSpec((T, LANES), lambda p, i: (i, p)),
                   pl.BlockSpec((T, LANES), lambda p, i: (i, p)),
                   row_stat,
                   pl.BlockSpec((1, T, 2), lambda p, i: (p, i, 0))],
        out_shape=[jax.ShapeDtypeStruct((L, ATTN_W), BF16)] * 3
        + [jax.ShapeDtypeStruct((npair, 2, L), F32), jax.ShapeDtypeStruct((npair, L, 2), F32)],
        scratch_shapes=[pltpu.VMEM((L, LANES), F32), pltpu.VMEM((2, L), F32)],
        compiler_params=_params(("parallel", "arbitrary")),
    )(proj, proj, proj, dcat, lse_row, delta_row, cq_row, ck_col)


def _shift_down(x, k):
    rolled = pltpu.roll(x, k, 0)
    return jnp.where(lax.broadcasted_iota(jnp.int32, x.shape, 0) >= k, rolled, 0.0)


def _shift_up(x, k):
    n = x.shape[0]
    rolled = pltpu.roll(x, n - k, 0)
    return jnp.where(lax.broadcasted_iota(jnp.int32, x.shape, 0) < n - k, rolled, 0.0)


def conv_fwd(proj, col0, conv_w, conv_b, name):
    L = proj.shape[0]
    C = conv_w.shape[1]
    nc = C // LANES

    def body(gb_ref, gc_ref, xc_ref, w_ref, b_ref, o_ref):
        z = gc_ref[...] * xc_ref[...]
        conv = (w_ref[0:1, :] * _shift_down(z, 2) + w_ref[1:2, :] * _shift_down(z, 1) + w_ref[2:3, :] * z
                + b_ref[...])
        o_ref[...] = (gb_ref[...] * conv).astype(o_ref.dtype)

    col = lambda off: pl.BlockSpec((L, LANES), lambda j, off=off: (0, col0 + off + j))
    return pl.pallas_call(
        body, name=name, grid=(nc,),
        in_specs=[col(0), col(nc), col(2 * nc), pl.BlockSpec((CONV_K, LANES), lambda j: (0, j)),
                  pl.BlockSpec((1, LANES), lambda j: (0, j))],
        out_specs=pl.BlockSpec((L, LANES), lambda j: (0, j)),
        out_shape=jax.ShapeDtypeStruct((L, C), BF16),
        compiler_params=_params(("parallel",)),
    )(proj, proj, proj, conv_w, conv_b)


def conv_bwd(proj, col0, conv_w, conv_b, dcat, dcol0, name):
    L = proj.shape[0]
    C = conv_w.shape[1]
    nc = C // LANES

    def body(gb_ref, gc_ref, xc_ref, w_ref, b_ref, do_ref, dgb_ref, dgc_ref, dxc_ref, dw_ref, db_ref):
        gc, xc = gc_ref[...], xc_ref[...]
        z = gc * xc
        z1, z2 = _shift_down(z, 1), _shift_down(z, 2)
        w0, w1, w2 = w_ref[0:1, :], w_ref[1:2, :], w_ref[2:3, :]
        conv = w0 * z2 + w1 * z1 + w2 * z + b_ref[...]
        dout = do_ref[...]
        dgb_ref[...] = (dout * conv).astype(dgb_ref.dtype)
        dconv = dout * gb_ref[...]
        dw_ref[...] = jnp.concatenate([jnp.sum(dconv * z2, axis=0, keepdims=True),
                                       jnp.sum(dconv * z1, axis=0, keepdims=True),
                                       jnp.sum(dconv * z, axis=0, keepdims=True)], axis=0)
        db_ref[...] = jnp.sum(dconv, axis=0, keepdims=True)
        dz = w2 * dconv + w1 * _shift_up(dconv, 1) + w0 * _shift_up(dconv, 2)
        dgc_ref[...] = (dz * xc).astype(dgc_ref.dtype)
        dxc_ref[...] = (dz * gc).astype(dxc_ref.dtype)

    col = lambda off: pl.BlockSpec((L, LANES), lambda j, off=off: (0, col0 + off + j))
    out_col = pl.BlockSpec((L, LANES), lambda j: (0, j))
    return pl.pallas_call(
        body, name=name, grid=(nc,),
        in_specs=[col(0), col(nc), col(2 * nc), pl.BlockSpec((CONV_K, LANES), lambda j: (0, j)),
                  pl.BlockSpec((1, LANES), lambda j: (0, j)),
                  pl.BlockSpec((L, LANES), lambda j: (0, dcol0 + j))],
        out_specs=[out_col, out_col, out_col, pl.BlockSpec((CONV_K, LANES), lambda j: (0, j)),
                   pl.BlockSpec((1, LANES), lambda j: (0, j))],
        out_shape=[jax.ShapeDtypeStruct((L, C), BF16)] * 3
        + [jax.ShapeDtypeStruct((CONV_K, C), F32), jax.ShapeDtypeStruct((1, C), F32)],
        compiler_params=_params(("parallel",)),
    )(proj, proj, proj, conv_w, conv_b, dcat)


_GELU_C = math.sqrt(2.0 / math.pi)
_GELU_A = 0.044715


def _gelu(y):
    return 0.5 * y * (1.0 + jnp.tanh(_GELU_C * (y + _GELU_A * y * y * y)))


def _gelu_grad(y):
    t = jnp.tanh(_GELU_C * (y + _GELU_A * y * y * y))
    return 0.5 * (1.0 + t) + 0.5 * y * (1.0 - t * t) * _GELU_C * (1.0 + 3.0 * _GELU_A * y * y)


def _cmul_add(xr, xi, pr, pi, sr, si):
    return xr + pr * sr - pi * si, xi + pr * si + pi * sr


def _scan_tile(br, bi, cr, ci, tab_ref, reverse):
    n = S5_BLOCK_STATES
    xr, xi = br, bi
    for s, k in enumerate((1, 2, 4)):
        shift = SUBLANES - k if reverse else k
        xr, xi = _cmul_add(xr, xi, tab_ref[0, s, :, :n], tab_ref[0, s, :, n:],
                           pltpu.roll(xr, shift, 0), pltpu.roll(xi, shift, 0))
    return _cmul_add(xr, xi, tab_ref[0, 3, :, :n], tab_ref[0, 3, :, n:], cr, ci)


def s5_fwd(u, bmat, cmat, tab, dvec, name):
    L, D = u.shape
    nblk = D // LANES
    T = _div_tile(L, (ROW_TILE,))
    ns = 2 * S5_BLOCK_STATES
    n = S5_BLOCK_STATES

    def body(u_ref, b_ref, c_ref, tab_ref, d_ref, y_ref, g_ref, xs_ref, buf, car):
        @pl.when(pl.program_id(1) == 0)
        def _():
            car[...] = jnp.zeros_like(car)

        uv = u_ref[...]
        buf[...] = jnp.dot(uv.astype(BF16), b_ref[0], preferred_element_type=F32)

        def tile(i, carry):
            cr, ci = carry
            r0 = pl.multiple_of(i * SUBLANES, SUBLANES)
            xr, xi = _scan_tile(buf[pl.ds(r0, SUBLANES), :n], buf[pl.ds(r0, SUBLANES), n:], cr, ci, tab_ref, False)
            buf[pl.ds(r0, SUBLANES), :n] = xr
            buf[pl.ds(r0, SUBLANES), n:] = xi
            return xr[SUBLANES - 1:, :], xi[SUBLANES - 1:, :]

        cr, ci = lax.fori_loop(0, T // SUBLANES, tile, (car[:, :n], car[:, n:]))
        car[:, :n] = cr
        car[:, n:] = ci
        xs = buf[...]
        xs_ref[...] = xs
        y = jnp.dot(xs.astype(BF16), c_ref[0], preferred_element_type=F32) + d_ref[...] * uv
        y_ref[...] = y
        g_ref[...] = _gelu(y).astype(g_ref.dtype)

    blk = pl.BlockSpec((T, LANES), lambda j, i: (i, j))
    return pl.pallas_call(
        body, name=name, grid=(nblk, L // T),
        in_specs=[blk, pl.BlockSpec((1, LANES, ns), lambda j, i: (j, 0, 0)),
                  pl.BlockSpec((1, ns, LANES), lambda j, i: (j, 0, 0)),
                  pl.BlockSpec((1, 4, SUBLANES, ns), lambda j, i: (j, 0, 0, 0)),
                  pl.BlockSpec((1, LANES), lambda j, i: (0, j))],
        out_specs=[blk, blk, pl.BlockSpec((T, ns), lambda j, i: (i, j))],
        out_shape=[jax.ShapeDtypeStruct((L, D), F32), jax.ShapeDtypeStruct((L, D), BF16),
                   jax.ShapeDtypeStruct((L, nblk * ns), F32)],
        scratch_shapes=[pltpu.VMEM((T, ns), F32), pltpu.VMEM((1, ns), F32)],
        compiler_params=_params(("parallel", "arbitrary")),
    )(u, bmat, cmat, tab, dvec)


def s5_bwd(dg, y, u, xs, cmat_t, bmat_t, rtab, dvec, name):
    L, D = u.shape
    nblk = D // LANES
    T = _div_tile(L, (ROW_TILE,))
    nch = L // T
    ns = 2 * S5_BLOCK_STATES
    n = S5_BLOCK_STATES
    ntile = T // SUBLANES

    def body(dg_ref, y_ref, u_ref, xs_ref, xp_ref, ct_ref, bt_ref, tab_ref, d_ref,
             du_ref, dc_ref, db_ref, dlam_ref, dd_ref, buf, xbuf, car):
        step = pl.program_id(1)
        first_chunk = step == nch - 1

        @pl.when(step == 0)
        def _():
            car[...] = jnp.zeros_like(car)
            dc_ref[...] = jnp.zeros_like(dc_ref)
            db_ref[...] = jnp.zeros_like(db_ref)
            dlam_ref[...] = jnp.zeros_like(dlam_ref)
            dd_ref[...] = jnp.zeros_like(dd_ref)

        uv = u_ref[...]
        dy = dg_ref[...].astype(F32) * _gelu_grad(y_ref[...])
        dd_ref[...] += jnp.sum(dy * uv, axis=0, keepdims=True)
        dyb = dy.astype(BF16)
        buf[...] = jnp.dot(dyb, ct_ref[0], preferred_element_type=F32)
        xs = xs_ref[...]
        xbuf[pl.ds(SUBLANES, T), :] = xs
        xbuf[pl.ds(0, SUBLANES), :] = jnp.where(first_chunk, 0.0, xp_ref[...])
        row0 = lax.broadcasted_iota(jnp.int32, (SUBLANES, n), 0) == 0

        def tile(ii, carry):
            cr, ci, ar, ai = carry
            r0 = pl.multiple_of((ntile - 1 - ii) * SUBLANES, SUBLANES)
            xr, xi = _scan_tile(buf[pl.ds(r0, SUBLANES), :n], buf[pl.ds(r0, SUBLANES), n:], cr, ci, tab_ref, True)
            buf[pl.ds(r0, SUBLANES), :n] = xr
            buf[pl.ds(r0, SUBLANES), n:] = xi
            r1 = pl.multiple_of(r0 + SUBLANES, SUBLANES)
            pr = jnp.where(row0, xbuf[pl.ds(r0, SUBLANES), :n][SUBLANES - 1:, :],
                           pltpu.roll(xbuf[pl.ds(r1, SUBLANES), :n], 1, 0))
            pi = jnp.where(row0, xbuf[pl.ds(r0, SUBLANES), n:][SUBLANES - 1:, :],
                           pltpu.roll(xbuf[pl.ds(r1, SUBLANES), n:], 1, 0))
            ar = ar + xr * pr + xi * pi
            ai = ai + xi * pr - xr * pi
            return xr[0:1, :], xi[0:1, :], ar, ai

        zero = jnp.zeros((SUBLANES, n), F32)
        cr, ci, ar, ai = lax.fori_loop(0, ntile, tile, (car[:, :n], car[:, n:], zero, zero))
        car[:, :n] = cr
        car[:, n:] = ci
        dlam_ref[0, :, :n] += ar
        dlam_ref[0, :, n:] += ai
        dxa = buf[...]
        dc_ref[0] += lax.dot_general(dyb, xs.astype(BF16), _DIMS["tn"], preferred_element_type=F32)
        dxb = dxa.astype(BF16)
        db_ref[0] += lax.dot_general(uv.astype(BF16), dxb, _DIMS["tn"], preferred_element_type=F32)
        du_ref[...] = jnp.dot(dxb, bt_ref[0], preferred_element_type=F32) + d_ref[...] * dy

    rev = lambda j, i: (nch - 1 - i, j)
    blk = pl.BlockSpec((T, LANES), rev)
    tpb = T // SUBLANES
    acc = pl.BlockSpec((1, LANES, ns), lambda j, i: (j, 0, 0))
    return pl.pallas_call(
        body, name=name, grid=(nblk, nch),
        in_specs=[blk, blk, blk, pl.BlockSpec((T, ns), rev),
                  pl.BlockSpec((SUBLANES, ns), lambda j, i: (jnp.maximum((nch - 1 - i) * tpb - 1, 0), j)),
                  pl.BlockSpec((1, LANES, ns), lambda j, i: (j, 0, 0)),
                  pl.BlockSpec((1, ns, LANES), lambda j, i: (j, 0, 0)),
                  pl.BlockSpec((1, 4, SUBLANES, ns), lambda j, i: (j, 0, 0, 0)),
                  pl.BlockSpec((1, LANES), lambda j, i: (0, j))],
        out_specs=[blk, acc, acc, pl.BlockSpec((1, SUBLANES, ns), lambda j, i: (j, 0, 0)),
                   pl.BlockSpec((1, LANES), lambda j, i: (0, j))],
        out_shape=[jax.ShapeDtypeStruct((L, D), F32), jax.ShapeDtypeStruct((nblk, LANES, ns), F32),
                   jax.ShapeDtypeStruct((nblk, LANES, ns), F32), jax.ShapeDtypeStruct((nblk, SUBLANES, ns), F32),
                   jax.ShapeDtypeStruct((1, D), F32)],
        scratch_shapes=[pltpu.VMEM((T, ns), F32), pltpu.VMEM((T + SUBLANES, ns), F32), pltpu.VMEM((1, ns), F32)],
        compiler_params=_params(("parallel", "arbitrary")),
    )(dg, y, u, xs, xs, cmat_t, bmat_t, rtab, dvec)


def _s5_discretize(a_re, a_im, log_step, b_re, b_im):
    lam_re = jnp.minimum(a_re, -S5_MIN_DECAY)
    lam_im = a_im
    delta = jnp.exp(log_step)[:, None]
    mag = jnp.exp(lam_re * delta)
    ang = lam_im * delta
    lb_re = mag * jnp.cos(ang)
    lb_im = mag * jnp.sin(ang)
    den = lam_re * lam_re + lam_im * lam_im
    nr = lb_re - 1.0
    ni = lb_im
    coef_re = (nr * lam_re + ni * lam_im) / den
    coef_im = (ni * lam_re - nr * lam_im) / den
    bb_re = coef_re[..., None] * b_re - coef_im[..., None] * b_im
    bb_im = coef_re[..., None] * b_im + coef_im[..., None] * b_re
    return lb_re, lb_im, bb_re, bb_im


def _s5_tables(lb_re, lb_im):
    nblk = lb_re.shape[0] // S5_BLOCK_GROUPS
    lr = lb_re.reshape(nblk, S5_BLOCK_STATES)
    li = lb_im.reshape(nblk, S5_BLOCK_STATES)
    pows = [(jnp.ones_like(lr), jnp.zeros_like(li))]
    for _ in range(SUBLANES):
        pr, pi = pows[-1]
        pows.append((pr * lr - pi * li, pr * li + pi * lr))
    rows = jnp.arange(SUBLANES)[None, :, None]

    def table(conj, reverse):
        sgn = -1.0 if conj else 1.0
        out = []
        for k in (1, 2, 4):
            mask = (rows <= SUBLANES - 1 - k) if reverse else (rows >= k)
            out.append(jnp.concatenate([jnp.where(mask, pows[k][0][:, None, :], 0.0),
                                        jnp.where(mask, sgn * pows[k][1][:, None, :], 0.0)], axis=-1))
        order = range(SUBLANES, 0, -1) if reverse else range(1, SUBLANES + 1)
        cre = jnp.stack([pows[k][0] for k in order], axis=1)
        cim = jnp.stack([sgn * pows[k][1] for k in order], axis=1)
        out.append(jnp.concatenate([cre, cim], axis=-1))
        return jnp.stack(out, axis=1)

    return table(False, False), table(True, True)


def _s5_block_mats(bb_re, bb_im, c_re, c_im):
    G = bb_re.shape[0]
    nblk = G // S5_BLOCK_GROUPS
    eye = jnp.eye(S5_BLOCK_GROUPS, dtype=F32)
    bb = jnp.stack([bb_re, bb_im]).reshape(2, nblk, S5_BLOCK_GROUPS, S5_STATE, S5_GROUP)
    bmat = jnp.einsum("ab,rjaph->jahrbp", eye, bb).reshape(nblk, LANES, 2 * S5_BLOCK_STATES)
    cc = jnp.stack([c_re, -c_im]).reshape(2, nblk, S5_BLOCK_GROUPS, S5_GROUP, S5_STATE)
    cmat = jnp.einsum("ab,rjahp->jrbpah", eye, cc).reshape(nblk, 2 * S5_BLOCK_STATES, LANES)
    return bmat, cmat


def _s5_unblock(dB, dC, dlam):
    nblk = dB.shape[0]
    G = nblk * S5_BLOCK_GROUPS
    d6 = dB.reshape(nblk, S5_BLOCK_GROUPS, S5_GROUP, 2, S5_BLOCK_GROUPS, S5_STATE)
    dbb = jnp.einsum("jahrap->rjaph", d6).reshape(2, G, S5_STATE, S5_GROUP)
    c6 = dC.reshape(nblk, S5_BLOCK_GROUPS, S5_GROUP, 2, S5_BLOCK_GROUPS, S5_STATE)
    dcc = jnp.einsum("jahrap->rjahp", c6).reshape(2, G, S5_GROUP, S5_STATE)
    dl = jnp.sum(dlam, axis=1).reshape(nblk, 2, S5_BLOCK_GROUPS, S5_STATE)
    dl = jnp.transpose(dl, (1, 0, 2, 3)).reshape(2, G, S5_STATE)
    return dbb[0], dbb[1], dcc[0], -dcc[1], dl[0], dl[1]


def loss_and_grad(y, target, name):
    L, D = y.shape
    tr = _div_tile(L, (512, 256, 128))

    def body(y_ref, t_ref, dy_ref, loss_ref):
        err = y_ref[...] - t_ref[...]
        dy_ref[...] = err * (1.0 / D)
        part = 0.5 * jnp.sum(jnp.mean(err * err, axis=-1, keepdims=True), axis=0, keepdims=True)

        @pl.when(pl.program_id(0) == 0)
        def _():
            loss_ref[...] = part

        @pl.when(pl.program_id(0) > 0)
        def _():
            loss_ref[...] += part

    row = pl.BlockSpec((tr, D), lambda i: (i, 0))
    return pl.pallas_call(
        body, name=name, grid=(L // tr,), in_specs=[row, row],
        out_specs=[row, pl.BlockSpec((1, 1), lambda i: (0, 0))],
        out_shape=[jax.ShapeDtypeStruct((L, D), F32), jax.ShapeDtypeStruct((1, 1), F32)],
        compiler_params=_params(("arbitrary",)),
    )(y, target)


def _adam_math(w, g, m, v):
    m = ADAM_B1 * m + (1.0 - ADAM_B1) * g
    v = ADAM_B2 * v + (1.0 - ADAM_B2) * (g * g)
    m_hat = m / (1.0 - ADAM_B1 ** ADAM_STEP)
    v_hat = v / (1.0 - ADAM_B2 ** ADAM_STEP)
    delta = -ADAM_LR * (m_hat / (jnp.sqrt(v_hat) + ADAM_EPS) + ADAM_WD * w)
    return delta, m, v


def _as3d(a):
    return a.reshape((-1,) + a.shape[-2:])


def adamw(w, g, m, v, name):
    shape = w.shape
    w3, g3, m3, v3 = _as3d(w), _as3d(g), _as3d(m), _as3d(v)
    A, R, C = w3.shape
    tr = _row_tile(R, C)

    def body(w_ref, g_ref, m_ref, v_ref, d_ref, mo_ref, vo_ref):
        d, mn, vn = _adam_math(w_ref[...], g_ref[...], m_ref[...], v_ref[...])
        d_ref[...] = d
        mo_ref[...] = mn
        vo_ref[...] = vn

    spec = pl.BlockSpec((1, tr, C), lambda a, i: (a, i, 0))
    outs = pl.pallas_call(
        body, name=name, grid=(A, R // tr), in_specs=[spec] * 4, out_specs=[spec] * 3,
        out_shape=[jax.ShapeDtypeStruct((A, R, C), F32)] * 3,
        compiler_params=_params(("parallel", "parallel")),
    )(w3, g3, m3, v3)
    return [o.reshape(shape) for o in outs]


def _gathered_shape(shape, kind):
    n, R, C = shape
    return {"stack": (4, n, R, C), "row": (n, 4 * R, C), "col": (n, R, 4 * C)}[kind]


def _own_shard_spec(kind, tr, R, C):
    if kind == "stack":
        return pl.BlockSpec((None, None, tr, C), lambda a, i: (_my_chip(), a, i, 0))
    if kind == "row":
        return pl.BlockSpec((None, tr, C), lambda a, i: (a, _my_chip() * (R // tr) + i, 0))
    return pl.BlockSpec((None, tr, C), lambda a, i: (a, i, _my_chip()))


def cast_into_gathered(shard, kind, name):
    n, R, C = shard.shape
    tr = _row_tile(R, C)

    def body(a_ref, o_ref):
        o_ref[...] = a_ref[...].astype(BF16)

    return pl.pallas_call(
        body, name=name, grid=(n, R // tr),
        in_specs=[pl.BlockSpec((None, tr, C), lambda a, i: (a, i, 0))],
        out_specs=_own_shard_spec(kind, tr, R, C),
        out_shape=jax.ShapeDtypeStruct(_gathered_shape(shard.shape, kind), BF16),
        compiler_params=_params(("parallel", "parallel")),
    )(shard)


def _place():
    x, y, c = lax.axis_index("x"), lax.axis_index("y"), lax.axis_index("c")
    other_chips = [(1 - x, y), (x, 1 - y), (1 - x, 1 - y)]
    return x, y, c, other_chips


def _chip_id(chip):
    return 2 * chip[0] + chip[1]


def _my_chip():
    return 2 * lax.axis_index("x") + lax.axis_index("y")


def _shard_view(ref, kind, shard, lo, cnt):
    if kind == "stack":
        return ref.at[shard, pl.ds(lo, cnt)]
    if kind == "row":
        R = ref.shape[1] // 4
        return ref.at[pl.ds(lo, cnt), pl.ds(pl.multiple_of(shard * R, 16), R), :]
    C = ref.shape[2] // 4
    return ref.at[pl.ds(lo, cnt), :, pl.ds(pl.multiple_of(shard * C, LANES), C)]


def _layer_view(ref, kind, lo, cnt):
    if kind == "stack":
        return ref.at[:, pl.ds(lo, cnt)]
    return ref.at[pl.ds(lo, cnt)]


def _n_layers(ref, kind):
    return ref.shape[1 if kind == "stack" else 0]


def _remote(src, dst, send_sem, recv_sem, dev):
    return pltpu.make_async_remote_copy(src_ref=src, dst_ref=dst, send_sem=send_sem, recv_sem=recv_sem,
                                        device_id=dev, device_id_type=MESH)


def allgather_weights(gathered, kinds):
    T = len(gathered)

    def body(*refs):
        outs = refs[T:2 * T]
        send_sems, recv_sems = refs[2 * T:]
        x, y, c, chips = _place()
        me, sibling = _chip_id((x, y)), (x, y, 1 - c)
        sends = []
        for t in range(T):
            hn = _n_layers(outs[t], kinds[t]) // 2
            mine = _shard_view(outs[t], kinds[t], me, c * hn, hn)
            for j, chip in enumerate(chips):
                cp = _remote(mine, mine, send_sems.at[6 * t + j], recv_sems.at[6 * t + j], (*chip, c))
                cp.start()
                sends.append(cp)
        for t in range(T):
            hn = _n_layers(outs[t], kinds[t]) // 2
            for j, chip in enumerate(chips):
                piece = _shard_view(outs[t], kinds[t], _chip_id(chip), c * hn, hn)
                cp = _remote(piece, piece, send_sems.at[6 * t + 3 + j], recv_sems.at[6 * t + j], sibling)
                cp.wait_recv()
                fwd = _remote(piece, piece, send_sems.at[6 * t + 3 + j], recv_sems.at[6 * t + 3 + j], sibling)
                fwd.start()
                sends.append(fwd)
        for t in range(T):
            hn = _n_layers(outs[t], kinds[t]) // 2
            for j, chip in enumerate(chips):
                piece = _shard_view(outs[t], kinds[t], _chip_id(chip), (1 - c) * hn, hn)
                _remote(piece, piece, send_sems.at[6 * t + 3 + j], recv_sems.at[6 * t + 3 + j], sibling).wait_recv()
        for cp in sends:
            cp.wait_send()

    return pl.pallas_call(
        body, name="allgather_weights", in_specs=[ANY] * T, out_specs=[ANY] * T,
        out_shape=[jax.ShapeDtypeStruct(g.shape, g.dtype) for g in gathered],
        input_output_aliases={t: t for t in range(T)},
        scratch_shapes=[pltpu.SemaphoreType.DMA((6 * T,)), pltpu.SemaphoreType.DMA((6 * T,))],
        compiler_params=pltpu.CompilerParams(has_side_effects=True),
    )(*gathered)


def allgather_small(arrs):
    T = len(arrs)

    def body(*refs):
        ins, outs = refs[:T], refs[T:2 * T]
        send_sems, recv_sems = refs[2 * T:]
        x, y, c, chips = _place()
        me = _chip_id((x, y))
        sends = []
        for t in range(T):
            outs[t][me] = ins[t][...]
            for j, chip in enumerate(chips):
                cp = _remote(ins[t], outs[t].at[me], send_sems.at[3 * t + j], recv_sems.at[3 * t + j], (*chip, c))
                cp.start()
                sends.append(cp)
        for t in range(T):
            for j, chip in enumerate(chips):
                slot = outs[t].at[_chip_id(chip)]
                _remote(slot, slot, send_sems.at[3 * t + j], recv_sems.at[3 * t + j], (*chip, c)).wait_recv()
        for cp in sends:
            cp.wait_send()

    return pl.pallas_call(
        body, name="allgather_small", in_specs=[VMEM_SPEC] * T, out_specs=[VMEM_SPEC] * T,
        out_shape=[jax.ShapeDtypeStruct((4,) + a.shape, a.dtype) for a in arrs],
        scratch_shapes=[pltpu.SemaphoreType.DMA((3 * T,)), pltpu.SemaphoreType.DMA((3 * T,))],
        compiler_params=pltpu.CompilerParams(vmem_limit_bytes=VMEM_LIMIT, has_side_effects=True),
    )(*arrs)


def allreduce_small(buf):
    R, C = buf.shape

    def body(in_ref, out_ref, pair_ref, all_ref, send_sems, recv_sems):
        x, y, c, chips = _place()
        me, sibling = _chip_id((x, y)), (x, y, 1 - c)
        swap = _remote(in_ref, pair_ref, send_sems.at[0], recv_sems.at[0], sibling)
        swap.start()
        swap.wait()
        all_ref[me] = in_ref[...] + pair_ref[...]
        sends = []
        for j, chip in enumerate(chips):
            cp = _remote(all_ref.at[me], all_ref.at[me], send_sems.at[1 + j], recv_sems.at[1 + j], (*chip, c))
            cp.start()
            sends.append(cp)
        for j, chip in enumerate(chips):
            slot = all_ref.at[_chip_id(chip)]
            _remote(slot, slot, send_sems.at[1 + j], recv_sems.at[1 + j], (*chip, c)).wait_recv()
        for cp in sends:
            cp.wait_send()
        out_ref[...] = ((all_ref[0] + all_ref[1]) + all_ref[2]) + all_ref[3]

    return pl.pallas_call(
        body, name="allreduce_small", in_specs=[VMEM_SPEC], out_specs=VMEM_SPEC,
        out_shape=jax.ShapeDtypeStruct((R, C), F32),
        scratch_shapes=[pltpu.VMEM((R, C), F32), pltpu.VMEM((4, R, C), F32),
                        pltpu.SemaphoreType.DMA((4,)), pltpu.SemaphoreType.DMA((4,))],
        compiler_params=pltpu.CompilerParams(vmem_limit_bytes=VMEM_LIMIT, has_side_effects=True),
    )(buf)


def _half_shape(shape, kind):
    s = list(shape)
    s[1 if kind == "stack" else 0] //= 2
    return tuple(s)


def rs_swap_halves(grads, kinds):
    T = len(grads)

    def body(*refs):
        ins, outs = refs[:T], refs[T:2 * T]
        send_sems, recv_sems = refs[2 * T:]
        x, y, c, _ = _place()
        copies = []
        for t in range(T):
            n = ins[t].shape[1 if kinds[t] == "stack" else 0]
            hn = n // 2
            cp = _remote(_layer_view(ins[t], kinds[t], (1 - c) * hn, hn), outs[t], send_sems.at[t], recv_sems.at[t],
                         (x, y, 1 - c))
            cp.start()
            copies.append(cp)
        for cp in copies:
            cp.wait()

    return pl.pallas_call(
        body, name="rs_swap_halves", in_specs=[ANY] * T, out_specs=[ANY] * T,
        out_shape=[jax.ShapeDtypeStruct(_half_shape(g.shape, k), g.dtype) for g, k in zip(grads, kinds)],
        scratch_shapes=[pltpu.SemaphoreType.DMA((T,)), pltpu.SemaphoreType.DMA((T,))],
        compiler_params=pltpu.CompilerParams(has_side_effects=True),
    )(*grads)


def rs_pair_sum(grad, recv, kind, name):
    g3, r3 = _as3d(grad), _as3d(recv)
    A, R, C = r3.shape
    n = grad.shape[1 if kind == "stack" else 0]
    hn = n // 2
    tr = _row_tile(R, C)

    def body(g_ref, r_ref, f_ref, b_ref):
        s = g_ref[...] + r_ref[...]
        f_ref[...] = s
        b_ref[...] = s.astype(BF16)

    def g_map(a, i):
        return ((a // hn) * n + lax.axis_index("c") * hn + a % hn, i, 0)

    spec = pl.BlockSpec((1, tr, C), lambda a, i: (a, i, 0))
    f, b = pl.pallas_call(
        body, name=name, grid=(A, R // tr),
        in_specs=[pl.BlockSpec((1, tr, C), g_map), spec], out_specs=[spec, spec],
        out_shape=[jax.ShapeDtypeStruct((A, R, C), F32), jax.ShapeDtypeStruct((A, R, C), BF16)],
        compiler_params=_params(("parallel", "parallel")),
    )(g3, r3)
    return f.reshape(recv.shape), b.reshape(recv.shape)


def _shard_shape(half_gathered_shape, kind):
    full = half_gathered_shape
    if kind == "stack":
        return tuple(full[1:])
    if kind == "row":
        return (full[0], full[1] // 4, full[2])
    return (full[0], full[1], full[2] // 4)


def rs_exchange(pair_bf16, kinds):
    T = len(pair_bf16)

    def body(*refs):
        pb, got = refs[:T], refs[T:2 * T]
        send_sems, recv_sems = refs[2 * T:]
        x, y, c, chips = _place()
        sends = []
        for t in range(T):
            hn = got[t].shape[1]
            for j, chip in enumerate(chips):
                cp = _remote(_shard_view(pb[t], kinds[t], _chip_id(chip), 0, hn), got[t].at[j],
                             send_sems.at[3 * t + j], recv_sems.at[3 * t + j], (*chip, c))
                cp.start()
                sends.append(cp)
        for cp in sends:
            cp.wait()

    return pl.pallas_call(
        body, name="rs_exchange", in_specs=[ANY] * T, out_specs=[ANY] * T,
        out_shape=[jax.ShapeDtypeStruct((3,) + _shard_shape(p.shape, k), BF16) for p, k in zip(pair_bf16, kinds)],
        scratch_shapes=[pltpu.SemaphoreType.DMA((3 * T,)), pltpu.SemaphoreType.DMA((3 * T,))],
        compiler_params=pltpu.CompilerParams(has_side_effects=True),
    )(*pair_bf16)


def rs_total(pair_f32, got, kind, name):
    _, hn, R, C = got.shape
    tr = _row_tile(R, C)

    def body(p_ref, g_ref, t_ref):
        t_ref[...] = ((p_ref[...] + g_ref[0].astype(F32)) + g_ref[1].astype(F32)) + g_ref[2].astype(F32)

    return pl.pallas_call(
        body, name=name, grid=(hn, R // tr),
        in_specs=[_own_shard_spec(kind, tr, R, C), pl.BlockSpec((3, None, tr, C), lambda a, i: (0, a, i, 0))],
        out_specs=pl.BlockSpec((None, tr, C), lambda a, i: (lax.axis_index("c") * hn + a, i, 0)),
        out_shape=jax.ShapeDtypeStruct((2 * hn, R, C), F32), compiler_params=_params(("parallel", "parallel")),
    )(pair_f32, got)


def rs_share_halves(reduced):
    T = len(reduced)

    def body(*refs):
        outs = refs[T:2 * T]
        send_sems, recv_sems = refs[2 * T:]
        x, y, c, _ = _place()
        copies = []
        for t in range(T):
            hn = outs[t].shape[0] // 2
            mine = outs[t].at[pl.ds(c * hn, hn)]
            cp = _remote(mine, mine, send_sems.at[t], recv_sems.at[t], (x, y, 1 - c))
            cp.start()
            copies.append(cp)
        for t, cp in enumerate(copies):
            hn = outs[t].shape[0] // 2
            theirs = outs[t].at[pl.ds((1 - c) * hn, hn)]
            cp.wait_send()
            _remote(theirs, theirs, send_sems.at[t], recv_sems.at[t], (x, y, 1 - c)).wait_recv()

    return pl.pallas_call(
        body, name="rs_share_halves", in_specs=[ANY] * T, out_specs=[ANY] * T,
        out_shape=[jax.ShapeDtypeStruct(a.shape, a.dtype) for a in reduced],
        input_output_aliases={t: t for t in range(T)},
        scratch_shapes=[pltpu.SemaphoreType.DMA((T,)), pltpu.SemaphoreType.DMA((T,))],
        compiler_params=pltpu.CompilerParams(has_side_effects=True),
    )(*reduced)


def reduce_scatter_grads(grads, kinds):
    recv = rs_swap_halves(grads, kinds)
    pf, pb = [], []
    for t, (g, r, k) in enumerate(zip(grads, recv, kinds)):
        f, b = rs_pair_sum(g, r, k, "rs_pair_sum_%d" % t)
        pf.append(f)
        pb.append(b)
    got = rs_exchange(pb, kinds)
    halves = [rs_total(f, g, k, "rs_total_%d" % t) for t, (f, g, k) in enumerate(zip(pf, got, kinds))]
    return rs_share_halves(halves)


def _round_up(n, m):
    return (n + m - 1) // m * m


def _heads_col(a16):
    L = a16.shape[0]
    return jnp.transpose(a16.reshape(L, HEADS // 2, 2), (1, 0, 2))


def _heads_row(a16):
    L = a16.shape[0]
    return jnp.transpose(a16.reshape(L, HEADS // 2, 2), (1, 2, 0))


def local_step(x, target, meta, norm_g, w_qkv, w_rest, b_f, conv_w, conv_b, w_o, s5, s5_d, w_glu1, w_glu2,
               w_gate, w_up, w_down):
    S, D = x.shape
    depth = norm_g.shape[0]
    n_even, n_odd = w_qkv.shape[0], w_glu1.shape[0]
    L = N_META + S
    Lp = _round_up(L, ROW_TILE)
    h = jnp.concatenate([meta, x, jnp.zeros((Lp - L, D), F32)], axis=0)
    conv_c = conv_w.shape[2]
    fg_block = 3 * conv_c // LANES
    saved = []

    for i in range(depth):
        g = norm_g[i]
        j = i // 2
        tag = "l%d_" % i
        st = {"h0": h}
        if i % 2 == 0:
            u = rmsnorm_fwd(h, g[0:1], BF16, tag + "norm0")
            qkv = matmul([(u, (w_qkv, j))], "nn", BF16, tag + "qkv")
            rest = matmul([(u, (w_rest, j))], "nn", F32, tag + "rest")
            cgate = gate_fwd(rest, fg_block, b_f[j], tag + "gate")
            c16 = cgate[:, :HEADS]
            attn, lse = attn_fwd(qkv, _heads_col(c16), _heads_row(c16), tag + "attn")
            convo = conv_fwd(rest, 0, conv_w[j], conv_b[j], tag + "conv")
            cat = jnp.concatenate([attn, convo], axis=1)
            m = matmul([(cat, (w_o, j))], "nn", F32, tag + "wo")
            st.update(u=u, qkv=qkv, rest=rest, c16=c16, lse=lse, cat=cat)
        else:
            p = s5[j]
            u = rmsnorm_fwd(h, g[0:1], F32, tag + "norm0")
            y, gact, xs = s5_fwd(u, p["bmat"], p["cmat"], p["tab"], s5_d[j], tag + "s5")
            o1, o2, m = dual_matmul_act(gact, (w_glu1, j), (w_glu2, j), "glu", F32, tag + "glu")
            st.update(u=u, y=y, gact=gact, xs=xs, o1=o1, o2=o2)
        h1 = rmsnorm_fwd(m, g[1:2], F32, tag + "norm1", residual=h)
        u2 = rmsnorm_fwd(h1, g[2:3], BF16, tag + "norm2")
        a, b, hact = dual_matmul_act(u2, (w_gate, i), (w_up, i), "swiglu", BF16, tag + "ffn_in")
        f = matmul([(hact, (w_down, i))], "nn", F32, tag + "ffn_out")
        h = rmsnorm_fwd(f, g[3:4], F32, tag + "norm3", residual=h1)
        st.update(m=m, h1=h1, u2=u2, a=a, b=b, hact=hact, f=f)
        saved.append(st)

    dy, loss = loss_and_grad(h[N_META:L], target, "loss")
    dh = jnp.concatenate([jnp.zeros((N_META, D), F32), dy, jnp.zeros((Lp - L, D), F32)], axis=0)

    grads = {k: [None] * n_even for k in ("b_f", "conv_w", "conv_b")}
    grads.update({k: [None] * n_odd for k in ("s5_d", "s5_dB", "s5_dC", "s5_dlam")})
    grads["norm_g"] = [None] * depth
    stacks = {}

    def weight_grad(key, n, layer, a_op, b_op, tag):
        stacks[key] = matmul([(a_op, b_op)], "tn", F32, tag, out_stack=(n, layer, stacks.get(key)))

    for i in reversed(range(depth)):
        g = norm_g[i]
        j = i // 2
        tag = "l%d_b_" % i
        st = saved[i]
        df, dg3 = rmsnorm_bwd(st["f"], g[3:4], dh, BF16, tag + "norm3")
        weight_grad("w_down", depth, i, st["hact"], df, tag + "dw_down")
        da, db = ffn_bwd_act(df, (w_down, i), st["a"], st["b"], tag + "ffn_act")
        weight_grad("w_gate", depth, i, st["u2"], da, tag + "dw_gate")
        weight_grad("w_up", depth, i, st["u2"], db, tag + "dw_up")
        du2 = matmul([(da, (w_gate, i)), (db, (w_up, i))], "nt", F32, tag + "du2")
        dh1, dg2 = rmsnorm_bwd(st["h1"], g[2:3], du2, F32, tag + "norm2", add=dh)
        if i % 2 == 0:
            dm, dg1 = rmsnorm_bwd(st["m"], g[1:2], dh1, BF16, tag + "norm1")
            weight_grad("w_o", n_even, j, st["cat"], dm, tag + "dw_o")
            dcat = matmul([(dm, (w_o, j))], "nt", F32, tag + "dcat")
            delta = attn_delta(dcat, st["cat"], tag + "delta")
            c16 = st["c16"]
            lse16 = jnp.transpose(st["lse"], (1, 0, 2)).reshape(Lp, HEADS)
            dq, dk, dv, dcq, dck = attn_bwd(st["qkv"], dcat, _heads_row(lse16), _heads_row(delta[:, :HEADS]),
                                            _heads_row(c16), _heads_col(c16), tag + "attn")
            dc16 = (jnp.transpose(dcq, (2, 0, 1)).reshape(Lp, HEADS)
                    + jnp.transpose(dck, (1, 0, 2)).reshape(Lp, HEADS))
            dc = jnp.pad(dc16, ((0, 0), (0, LANES - HEADS)))
            dfg, dbf = gate_bwd(st["rest"], fg_block, b_f[j], dc, tag + "gate")
            dgb, dgc, dxc, dcw, dcb = conv_bwd(st["rest"], 0, conv_w[j], conv_b[j], dcat, ATTN_W // LANES,
                                               tag + "conv")
            dqkv = jnp.concatenate([dq, dk, dv], axis=1)
            drest = jnp.concatenate([dgb, dgc, dxc, dfg.astype(BF16)], axis=1)
            weight_grad("w_qkv", n_even, j, st["u"], dqkv, tag + "dw_qkv")
            weight_grad("w_rest", n_even, j, st["u"], drest, tag + "dw_rest")
            du = matmul([(dqkv, (w_qkv, j))], "nt", F32, tag + "du_qkv")
            du_b = matmul([(drest, (w_rest, j))], "nt", F32, tag + "du_rest")
            grads["b_f"][j], grads["conv_w"][j], grads["conv_b"][j] = dbf, dcw, dcb
        else:
            p = s5[j]
            dmix, dg1 = rmsnorm_bwd(st["m"], g[1:2], dh1, F32, tag + "norm1")
            do1, do2 = glu_bwd_act(dmix, st["o1"], st["o2"], tag + "glu_act")
            weight_grad("w_glu1", n_odd, j, st["gact"], do1, tag + "dw_glu1")
            weight_grad("w_glu2", n_odd, j, st["gact"], do2, tag + "dw_glu2")
            dgact = matmul([(do1, (w_glu1, j)), (do2, (w_glu2, j))], "nt", F32, tag + "dgact")
            du, dC, dB, dlam, dd = s5_bwd(dgact, st["y"], st["u"], st["xs"], p["cmat_t"], p["bmat_t"], p["rtab"],
                                          s5_d[j], tag + "s5")
            du_b = None
            grads["s5_dB"][j], grads["s5_dC"][j], grads["s5_dlam"][j], grads["s5_d"][j] = dB, dC, dlam, dd
        dh, dg0 = rmsnorm_bwd(st["h0"], g[0:1], du, F32, tag + "norm0", add=dh1, dy2=du_b)
        grads["norm_g"][i] = jnp.concatenate([dg0, dg1, dg2, dg3], axis=0)

    grads.update(stacks)
    grads["meta"] = dh[:N_META]
    return loss, dh[N_META:L], grads


def _pack(arrs):
    flat = jnp.concatenate([a.reshape(-1).astype(F32) for a in arrs])
    n = flat.shape[0]
    rows = _round_up(_round_up(n, LANES) // LANES, SUBLANES)
    return jnp.pad(flat, (0, rows * LANES - n)).reshape(rows, LANES)


def _unpack(buf, shapes):
    flat = buf.reshape(-1)
    out, off = [], 0
    for s in shapes:
        size = math.prod(s)
        out.append(flat[off:off + size].reshape(s))
        off += size
    return out


def device_step(x, target, P):
    D = x.shape[-1]
    n_even, n_odd = P["ab_w_in"].shape[0], P["s5_w_glu1"].shape[0]
    width = P["ab_w_in"].shape[2]
    conv_c = P["ab_conv_b"].shape[1]
    fg0 = 3 * ATTN_W
    nqc = width - fg0 - HEADS
    w_in = P["ab_w_in"]
    w_qkv = w_in[:, :, :fg0]
    w_rest = jnp.concatenate([w_in[:, :, fg0 + HEADS:], w_in[:, :, fg0:fg0 + HEADS],
                              jnp.zeros((n_even, D, LANES - HEADS), BF16)], axis=2)
    b_f_pad = jnp.pad(P["ab_b_f"], ((0, 0), (0, LANES - HEADS))).reshape(n_even, 1, LANES)

    s5, s5_vjps = [], []
    for j in range(n_odd):
        disc, vjp = jax.vjp(_s5_discretize, P["s5_a_re"][j], P["s5_a_im"][j], P["s5_log_step"][j],
                            P["s5_b_re"][j], P["s5_b_im"][j])
        lb_re, lb_im, bb_re, bb_im = disc
        tab, rtab = _s5_tables(lb_re, lb_im)
        bmat, cmat = _s5_block_mats(bb_re, bb_im, P["s5_c_re"][j], P["s5_c_im"][j])
        s5.append(dict(tab=tab, rtab=rtab, bmat=bmat.astype(BF16), cmat=cmat.astype(BF16),
                       bmat_t=jnp.transpose(bmat, (0, 2, 1)).astype(BF16),
                       cmat_t=jnp.transpose(cmat, (0, 2, 1)).astype(BF16)))
        s5_vjps.append(vjp)

    loss, grad_x, G = local_step(
        x, target, P["meta_tokens"], P["norm_g"], w_qkv, w_rest, b_f_pad, P["ab_conv_w"],
        P["ab_conv_b"].reshape(n_even, 1, conv_c), P["ab_w_o"], s5, P["s5_d"].reshape(n_odd, 1, D),
        P["s5_w_glu1"], P["s5_w_glu2"], P["ffn_w_gate"], P["ffn_w_up"], P["ffn_w_down"])

    out = {
        "ab_w_in": jnp.concatenate([G["w_qkv"], G["w_rest"][:, :, nqc:nqc + HEADS], G["w_rest"][:, :, :nqc]], axis=2),
        "ab_w_o": G["w_o"], "s5_w_glu1": G["w_glu1"], "s5_w_glu2": G["w_glu2"],
        "ffn_w_gate": G["w_gate"], "ffn_w_up": G["w_up"], "ffn_w_down": G["w_down"],
        "meta_tokens": G["meta"],
        "norm_g": jnp.stack(G["norm_g"]),
        "ab_b_f": jnp.stack([b[0, :HEADS] for b in G["b_f"]]),
        "ab_conv_w": jnp.stack(G["conv_w"]),
        "ab_conv_b": jnp.stack([b[0] for b in G["conv_b"]]),
        "s5_d": jnp.stack([d[0] for d in G["s5_d"]]),
    }
    s5g = {k: [] for k in ("s5_a_re", "s5_a_im", "s5_log_step", "s5_b_re", "s5_b_im", "s5_c_re", "s5_c_im")}
    for j in range(n_odd):
        dbb_re, dbb_im, dc_re, dc_im, dl_re, dl_im = _s5_unblock(G["s5_dB"][j], G["s5_dC"][j], G["s5_dlam"][j])
        da_re, da_im, dls, db_re, db_im = s5_vjps[j]((dl_re, dl_im, dbb_re, dbb_im))
        for k, val in zip(s5g, (da_re, da_im, dls, db_re, db_im, dc_re, dc_im)):
            s5g[k].append(val)
    out.update({k: jnp.stack(v) for k, v in s5g.items()})
    return loss, grad_x, out


def kernel(x, meta_tokens, norm_g, ab_w_in, ab_b_f, ab_conv_w, ab_conv_b, ab_w_o, s5_a_re, s5_a_im, s5_log_step, s5_b_re, s5_b_im, s5_c_re, s5_c_im, s5_d, s5_w_glu1, s5_w_glu2, ffn_w_gate, ffn_w_up, ffn_w_down, loss_target, m_meta_tokens, m_norm_g, m_ab_w_in, m_ab_b_f, m_ab_conv_w, m_ab_conv_b, m_ab_w_o, m_s5_a_re, m_s5_a_im, m_s5_log_step, m_s5_b_re, m_s5_b_im, m_s5_c_re, m_s5_c_im, m_s5_d, m_s5_w_glu1, m_s5_w_glu2, m_ffn_w_gate, m_ffn_w_up, m_ffn_w_down, v_meta_tokens, v_norm_g, v_ab_w_in, v_ab_b_f, v_ab_conv_w, v_ab_conv_b, v_ab_w_o, v_s5_a_re, v_s5_a_im, v_s5_log_step, v_s5_b_re, v_s5_b_im, v_s5_c_re, v_s5_c_im, v_s5_d, v_s5_w_glu1, v_s5_w_glu2, v_ffn_w_gate, v_ffn_w_up, v_ffn_w_down):
    names = ["meta_tokens", "norm_g", "ab_w_in", "ab_b_f", "ab_conv_w", "ab_conv_b", "ab_w_o", "s5_a_re", "s5_a_im",
             "s5_log_step", "s5_b_re", "s5_b_im", "s5_c_re", "s5_c_im", "s5_d", "s5_w_glu1", "s5_w_glu2",
             "ffn_w_gate", "ffn_w_up", "ffn_w_down"]
    W = dict(zip(names, [meta_tokens, norm_g, ab_w_in, ab_b_f, ab_conv_w, ab_conv_b, ab_w_o, s5_a_re, s5_a_im,
                         s5_log_step, s5_b_re, s5_b_im, s5_c_re, s5_c_im, s5_d, s5_w_glu1, s5_w_glu2,
                         ffn_w_gate, ffn_w_up, ffn_w_down]))
    Mo = dict(zip(names, [m_meta_tokens, m_norm_g, m_ab_w_in, m_ab_b_f, m_ab_conv_w, m_ab_conv_b, m_ab_w_o, m_s5_a_re,
                          m_s5_a_im, m_s5_log_step, m_s5_b_re, m_s5_b_im, m_s5_c_re, m_s5_c_im, m_s5_d, m_s5_w_glu1,
                          m_s5_w_glu2, m_ffn_w_gate, m_ffn_w_up, m_ffn_w_down]))
    Vo = dict(zip(names, [v_meta_tokens, v_norm_g, v_ab_w_in, v_ab_b_f, v_ab_conv_w, v_ab_conv_b, v_ab_w_o, v_s5_a_re,
                          v_s5_a_im, v_s5_log_step, v_s5_b_re, v_s5_b_im, v_s5_c_re, v_s5_c_im, v_s5_d, v_s5_w_glu1,
                          v_s5_w_glu2, v_ffn_w_gate, v_ffn_w_up, v_ffn_w_down]))
    D = x.shape[-1]
    n_even, n_odd, depth = ab_w_in.shape[0], s5_w_glu1.shape[0], ffn_w_gate.shape[0]
    chip = 2 * lax.axis_index("x") + lax.axis_index("y")

    big = ["ab_w_in", "ab_w_o", "s5_w_glu1", "s5_w_glu2", "ffn_w_gate", "ffn_w_up", "ffn_w_down"]
    kinds = ["stack", "row", "row", "row", "col", "col", "row"]
    own_bf16 = [cast_into_gathered(W[k], kind, "cast_" + k) for k, kind in zip(big, kinds)]
    full = dict(zip(big, allgather_weights(own_bf16, kinds)))
    width = 4 * ab_w_in.shape[2]
    full["ab_w_in"] = jnp.transpose(full["ab_w_in"], (1, 2, 0, 3)).reshape(n_even, D, width)
    g_meta, g_norm, g_convw, g_s5d = allgather_small([meta_tokens, norm_g, ab_conv_w, s5_d])
    full["meta_tokens"] = jnp.transpose(g_meta, (1, 0, 2)).reshape(N_META, D)
    full["norm_g"] = jnp.transpose(g_norm, (1, 2, 0, 3)).reshape(depth, 4, D)
    full["ab_conv_w"] = jnp.transpose(g_convw, (1, 2, 0, 3)).reshape(n_even, CONV_K, -1)
    full["s5_d"] = jnp.transpose(g_s5d, (1, 0, 2)).reshape(n_odd, D)
    for k in names:
        full.setdefault(k, W[k])

    loss, grad_x, G = device_step(x[0], loss_target[0], full)

    G["ab_w_in"] = jnp.transpose(G["ab_w_in"].reshape(n_even, D, 4, width // 4), (2, 0, 1, 3))
    reduced = dict(zip(big, reduce_scatter_grads([G[k] for k in big], kinds)))

    small_w = [k for k in names if k not in big]
    small_names = ["loss"] + small_w
    G["loss"] = loss
    summed = dict(zip(small_names, _unpack(allreduce_small(_pack([G[k] for k in small_names])),
                                           [G[k].shape for k in small_names])))
    loss_out = summed["loss"].reshape(())
    for k in ("meta_tokens", "norm_g", "ab_conv_w", "s5_d"):
        n_last = W[k].shape[-1]
        summed[k] = lax.dynamic_slice_in_dim(summed[k], chip * n_last, n_last, axis=summed[k].ndim - 1)
    shapes = [W[k].shape for k in small_w]
    d_s, m_s, v_s = adamw(_pack([W[k] for k in small_w])[None], _pack([summed[k] for k in small_w])[None],
                          _pack([Mo[k] for k in small_w])[None], _pack([Vo[k] for k in small_w])[None], "adamw_small")
    delta = dict(zip(small_w, _unpack(d_s, shapes)))
    new_m = dict(zip(small_w, _unpack(m_s, shapes)))
    new_v = dict(zip(small_w, _unpack(v_s, shapes)))
    grad = {k: summed[k] for k in small_w}
    for k in big:
        grad[k] = reduced[k]
        delta[k], new_m[k], new_v[k] = adamw(W[k], reduced[k], Mo[k], Vo[k], "adamw_" + k)

    return (loss_out, grad_x[None], *[grad[k] for k in names], *[delta[k] for k in names],
            *[new_m[k] for k in names], *[new_v[k] for k in names])
```

```python
import functools
import math

import jax
import jax.numpy as jnp
from jax import lax
from jax.experimental import pallas as pl
from jax.experimental.pallas import tpu as pltpu

F32 = jnp.float32
BF16 = jnp.bfloat16

N_META = 16
HEADS = 16
HEAD_DIM = 64
ATTN_W = HEADS * HEAD_DIM
CONV_K = 3
S5_GROUP = 16
S5_STATE = 64
S5_MIN_DECAY = 1e-4
NORM_EPS = 1e-6
ADAM_LR = 0.001
ADAM_B1 = 0.9
ADAM_B2 = 0.999
ADAM_EPS = 1e-08
ADAM_WD = 0.01
ADAM_STEP = 10

LANES = 128
SUBLANES = 8
VMEM_LIMIT = 56 * 1024 * 1024
VMEM_TILE_BUDGET = 34 * 1024 * 1024
ROW_TILE = 384
ATTN_ROWS = 128
S5_BLOCK_GROUPS = LANES // S5_GROUP
S5_BLOCK_STATES = S5_BLOCK_GROUPS * S5_STATE
NEG_BIG = -1e30

MESH = pl.DeviceIdType.MESH
ANY = pl.BlockSpec(memory_space=pl.ANY)
VMEM_SPEC = pl.BlockSpec(memory_space=pltpu.VMEM)


def _params(sem=None):
    return pltpu.CompilerParams(dimension_semantics=sem, vmem_limit_bytes=VMEM_LIMIT)


def _div_tile(n, prefs):
    for p in prefs:
        if n % p == 0:
            return p
    return n


def _row_tile(rows, cols, itemsize=4, limit=2 * 1024 * 1024):
    for p in (512, 256, 128, 64, 32, 16):
        if rows % p == 0 and p * cols * itemsize <= limit:
            return p
    return 16 if rows % 16 == 0 else rows


def _tile_cands(n):
    c = [d for d in range(LANES, min(n, 2048) + 1, LANES) if n % d == 0]
    if not c or n <= 2048 and n not in c:
        c.append(n)
    return sorted(set(c), reverse=True)


def _mm_tiles(M, N, K, a_bytes, b_bytes, o_bytes, npairs):
    best = None
    for tk in _tile_cands(K):
        for tm in _tile_cands(M):
            for tn in _tile_cands(N):
                mem = npairs * 2 * (tm * tk * a_bytes + tk * tn * b_bytes) + 2 * tm * tn * o_bytes + tm * tn * 4
                mem += npairs * ((tm * tk * 2 if a_bytes == 4 else 0) + (tk * tn * 2 if b_bytes == 4 else 0))
                if mem > VMEM_TILE_BUDGET:
                    continue
                key = (tm * tn * tk, tk, tn)
                if best is None or key > best[0]:
                    best = (key, (tm, tn, tk))
    assert best is not None, (M, N, K)
    return best[1]


class Comm:
    def __init__(self, operands, out_shapes, aliases, n_sems, begin, middle=None, finish=None, middle_frac=0.5):
        self.operands, self.out_shapes, self.aliases, self.n_sems = list(operands), list(out_shapes), aliases, n_sems
        self.begin, self.middle, self.finish, self.middle_frac = begin, middle, finish, middle_frac


def carrier_call(body, name, grid, in_specs, out_specs, out_shape, scratch_shapes, args, comm, semantics):
    n_in, n_out = len(args), len(out_shape)
    if comm is None:
        outs = pl.pallas_call(body, name=name, grid=grid, in_specs=in_specs, out_specs=out_specs, out_shape=out_shape,
                              scratch_shapes=scratch_shapes, compiler_params=_params(semantics))(*args)
        return list(outs), []
    ci, co = len(comm.operands), len(comm.out_shapes)
    total = math.prod(grid)
    middle_at = min(total - 1, max(0, int(total * comm.middle_frac)))

    def carried(*refs):
        ins, cins = refs[:n_in], refs[n_in:n_in + ci]
        outs = refs[n_in + ci:n_in + ci + n_out]
        couts = refs[n_in + ci + n_out:n_in + ci + n_out + co]
        scratch, (send_sems, recv_sems) = refs[n_in + ci + n_out + co:-2], refs[-2:]
        step = 0
        for d, size in enumerate(grid):
            step = step * size + pl.program_id(d)

        @pl.when(step == 0)
        def _():
            comm.begin(cins, couts, send_sems, recv_sems)

        if comm.middle is not None:
            @pl.when(step == middle_at)
            def _():
                comm.middle(cins, couts, send_sems, recv_sems)

        body(*ins, *outs, *scratch)

        @pl.when(step == total - 1)
        def _():
            comm.finish(cins, couts, send_sems, recv_sems)

    outs = pl.pallas_call(
        carried, name=name, grid=grid,
        in_specs=list(in_specs) + [ANY] * ci, out_specs=list(out_specs) + [ANY] * co,
        out_shape=list(out_shape) + comm.out_shapes,
        scratch_shapes=list(scratch_shapes) + [pltpu.SemaphoreType.DMA((comm.n_sems,)),
                                                pltpu.SemaphoreType.DMA((comm.n_sems,))],
        input_output_aliases={n_in + i: n_out + o for i, o in comm.aliases.items()},
        compiler_params=pltpu.CompilerParams(dimension_semantics=("arbitrary",) * len(grid),
                                             vmem_limit_bytes=VMEM_LIMIT, has_side_effects=True),
    )(*args, *comm.operands)
    return list(outs[:n_out]), list(outs[n_out:])


def comm_call(name, comm):
    ci = len(comm.operands)

    def body(*refs):
        cins, couts = refs[:ci], refs[ci:ci + len(comm.out_shapes)]
        send_sems, recv_sems = refs[-2:]
        comm.begin(cins, couts, send_sems, recv_sems)
        if comm.middle is not None:
            comm.middle(cins, couts, send_sems, recv_sems)
        comm.finish(cins, couts, send_sems, recv_sems)

    return pl.pallas_call(
        body, name=name, in_specs=[ANY] * ci, out_specs=[ANY] * len(comm.out_shapes), out_shape=comm.out_shapes,
        input_output_aliases=dict(comm.aliases),
        scratch_shapes=[pltpu.SemaphoreType.DMA((comm.n_sems,)), pltpu.SemaphoreType.DMA((comm.n_sems,))],
        compiler_params=pltpu.CompilerParams(has_side_effects=True),
    )(*comm.operands)


_DIMS ={"nn": (((1,), (0,)), ((), ())), "nt": (((1,), (1,)), ((), ())), "tn": (((0,), (0,)), ((), ()))}


def _layered(op):
    return op if isinstance(op, tuple) else (op, None)


def _layer_spec(block, index_map, layer):
    if layer is None:
        return pl.BlockSpec(block, index_map)
    return pl.BlockSpec((None,) + block, lambda *g: (layer,) + index_map(*g))


def matmul(pairs, kind, out_dtype, name, out_stack=None):
    ops = [(_layered(a), _layered(b)) for a, b in pairs]
    a0, b0 = ops[0][0][0], ops[0][1][0]
    ash, bsh = a0.shape[-2:], b0.shape[-2:]
    if kind == "nn":
        (M, K), N = ash, bsh[1]
    elif kind == "nt":
        (M, K), N = ash, bsh[0]
    else:
        (K, M), N = ash, bsh[1]
    tm, tn, tk = _mm_tiles(M, N, K, a0.dtype.itemsize, b0.dtype.itemsize, jnp.dtype(out_dtype).itemsize, len(pairs))
    nk = K // tk
    dims = _DIMS[kind]
    npairs = len(pairs)
    previous = out_stack[2] if out_stack is not None else None
    n_in = 2 * npairs + (1 if previous is not None else 0)

    def body(*refs):
        ins, o_ref = refs[:2 * npairs], refs[n_in]
        part = None
        for p in range(npairs):
            d = lax.dot_general(ins[2 * p][...].astype(BF16), ins[2 * p + 1][...].astype(BF16), dims,
                                preferred_element_type=F32)
            part = d if part is None else part + d
        if nk == 1:
            o_ref[...] = part.astype(o_ref.dtype)
        else:
            acc_ref = refs[n_in + 1]
            k = pl.program_id(2)

            @pl.when(k == 0)
            def _():
                acc_ref[...] = part

            @pl.when(k > 0)
            def _():
                acc_ref[...] += part

            @pl.when(k == nk - 1)
            def _():
                o_ref[...] = acc_ref[...].astype(o_ref.dtype)

    if kind == "nn":
        a_blk, a_map = (tm, tk), lambda j, i, k: (i, k)
        b_blk, b_map = (tk, tn), lambda j, i, k: (k, j)
    elif kind == "nt":
        a_blk, a_map = (tm, tk), lambda j, i, k: (i, k)
        b_blk, b_map = (tn, tk), lambda j, i, k: (j, k)
    else:
        a_blk, a_map = (tk, tm), lambda j, i, k: (k, i)
        b_blk, b_map = (tk, tn), lambda j, i, k: (k, j)
    in_specs, flat = [], []
    for (a, la), (b, lb) in ops:
        in_specs += [_layer_spec(a_blk, a_map, la), _layer_spec(b_blk, b_map, lb)]
        flat += [a, b]
    out_map = lambda j, i, k: (i, j)
    aliases = {}
    if out_stack is None:
        out_spec, out_shape = pl.BlockSpec((tm, tn), out_map), (M, N)
    else:
        out_spec, out_shape = _layer_spec((tm, tn), out_map, out_stack[1]), (out_stack[0], M, N)
        if previous is not None:
            in_specs.append(ANY)
            flat.append(previous)
            aliases = {2 * npairs: 0}
    return pl.pallas_call(
        body, name=name,
        grid=(N // tn, M // tm, nk),
        in_specs=in_specs,
        out_specs=out_spec,
        out_shape=jax.ShapeDtypeStruct(out_shape, out_dtype),
        scratch_shapes=[] if nk == 1 else [pltpu.VMEM((tm, tn), F32)],
        input_output_aliases=aliases,
        compiler_params=_params(("parallel", "parallel", "arbitrary")),
    )(*flat)


def _sigmoid(x):
    return 1.0 / (1.0 + jnp.exp(-x))


def dual_matmul_act(x, w1, w2, act, out_dtype, name):
    M, K = x.shape
    (w1, l1), (w2, l2) = _layered(w1), _layered(w2)
    N = w1.shape[-1]
    tm = _div_tile(M, (ROW_TILE,))
    tn = _div_tile(N, (1408, 1024, 512, 256, 128))

    def body(x_ref, w1_ref, w2_ref, o1_ref, o2_ref, out_ref):
        xv = x_ref[...]
        o1 = jnp.dot(xv, w1_ref[...], preferred_element_type=F32)
        o2 = jnp.dot(xv, w2_ref[...], preferred_element_type=F32)
        o1_ref[...] = o1.astype(BF16)
        o2_ref[...] = o2.astype(BF16)
        if act == "swiglu":
            out = o1 * _sigmoid(o1) * o2
        else:
            out = o1 * _sigmoid(o2)
        out_ref[...] = out.astype(out_ref.dtype)

    w_map = lambda j, i: (0, j)
    o_spec = pl.BlockSpec((tm, tn), lambda j, i: (i, j))
    return pl.pallas_call(
        body, name=name, grid=(N // tn, M // tm),
        in_specs=[pl.BlockSpec((tm, K), lambda j, i: (i, 0)), _layer_spec((K, tn), w_map, l1),
                  _layer_spec((K, tn), w_map, l2)],
        out_specs=[o_spec, o_spec, o_spec],
        out_shape=[jax.ShapeDtypeStruct((M, N), BF16), jax.ShapeDtypeStruct((M, N), BF16),
                   jax.ShapeDtypeStruct((M, N), out_dtype)],
        compiler_params=_params(("parallel", "parallel")),
    )(x, w1, w2)


def ffn_bwd_act(df, wd, a, b, name, comm=None):
    M, K = df.shape
    wd, layer = _layered(wd)
    N = wd.shape[-2]
    tm = _div_tile(M, (ROW_TILE,))
    tn = _div_tile(N, (1408, 1024, 512, 256, 128))

    def body(df_ref, wd_ref, a_ref, b_ref, da_ref, db_ref):
        dh = lax.dot_general(df_ref[...], wd_ref[...], _DIMS["nt"], preferred_element_type=F32)
        av = a_ref[...].astype(F32)
        bv = b_ref[...].astype(F32)
        sig = _sigmoid(av)
        silu = av * sig
        da_ref[...] = (dh * bv * (sig + silu * (1.0 - sig))).astype(BF16)
        db_ref[...] = (dh * silu).astype(BF16)

    t_spec = pl.BlockSpec((tm, tn), lambda j, i: (i, j))
    return carrier_call(
        body, name, (N // tn, M // tm),
        [pl.BlockSpec((tm, K), lambda j, i: (i, 0)), _layer_spec((tn, K), lambda j, i: (j, 0), layer), t_spec, t_spec],
        [t_spec, t_spec], [jax.ShapeDtypeStruct((M, N), BF16)] * 2, [], (df, wd, a, b), comm,
        ("parallel", "parallel"))


def glu_bwd_act(dout, o1, o2, name):
    M, N = dout.shape
    tm = _div_tile(M, (ROW_TILE,))

    def body(d_ref, o1_ref, o2_ref, d1_ref, d2_ref):
        d = d_ref[...].astype(F32)
        sig = _sigmoid(o2_ref[...].astype(F32))
        d1_ref[...] = (d * sig).astype(BF16)
        d2_ref[...] = (d * o1_ref[...].astype(F32) * sig * (1.0 - sig)).astype(BF16)

    spec = pl.BlockSpec((tm, N), lambda i: (i, 0))
    return pl.pallas_call(
        body, name=name, grid=(M // tm,), in_specs=[spec] * 3, out_specs=[spec] * 2,
        out_shape=[jax.ShapeDtypeStruct((M, N), BF16)] * 2,
        compiler_params=_params(("parallel",)),
    )(dout, o1, o2)


def rmsnorm_fwd(x, g, out_dtype, name, residual=None):
    L, D = x.shape
    tr = _div_tile(L, (ROW_TILE,))
    has_res = residual is not None

    def body(*refs):
        x_ref, g_ref = refs[0], refs[1]
        o_ref = refs[-1]
        xv = x_ref[...]
        r = lax.rsqrt(jnp.mean(xv * xv, axis=-1, keepdims=True) + NORM_EPS)
        y = xv * r * g_ref[...]
        if has_res:
            y = refs[2][...] + y
        o_ref[...] = y.astype(o_ref.dtype)

    row = pl.BlockSpec((tr, D), lambda i: (i, 0))
    gsp = pl.BlockSpec((1, D), lambda i: (0, 0))
    args = (x, g) + ((residual,) if has_res else ())
    return pl.pallas_call(
        body, name=name, grid=(L // tr,), in_specs=[row, gsp] + ([row] if has_res else []), out_specs=row,
        out_shape=jax.ShapeDtypeStruct((L, D), out_dtype), compiler_params=_params(("parallel",)),
    )(*args)


def rmsnorm_bwd(x, g, dy, out_dtype, name, add=None, dy2=None):
    L, D = x.shape
    tr = _div_tile(L, (ROW_TILE,))
    has_add = add is not None
    has_dy2 = dy2 is not None

    def body(*refs):
        x_ref, g_ref, dy_ref = refs[0], refs[1], refs[2]
        dx_ref, dg_ref = refs[-2], refs[-1]
        xv = x_ref[...]
        dyv = dy_ref[...].astype(F32)
        if has_dy2:
            dyv = dyv + refs[3][...].astype(F32)
        r = lax.rsqrt(jnp.mean(xv * xv, axis=-1, keepdims=True) + NORM_EPS)
        t = dyv * g_ref[...]
        dx = r * t - xv * (r * r * r) * jnp.mean(xv * t, axis=-1, keepdims=True)
        if has_add:
            dx = refs[3 + has_dy2][...] + dx
        dx_ref[...] = dx.astype(dx_ref.dtype)
        dgp = jnp.sum(dyv * xv * r, axis=0, keepdims=True)

        @pl.when(pl.program_id(0) == 0)
        def _():
            dg_ref[...] = dgp

        @pl.when(pl.program_id(0) > 0)
        def _():
            dg_ref[...] += dgp

    row = pl.BlockSpec((tr, D), lambda i: (i, 0))
    gsp = pl.BlockSpec((1, D), lambda i: (0, 0))
    args = (x, g, dy) + ((dy2,) if has_dy2 else ()) + ((add,) if has_add else ())
    return pl.pallas_call(
        body, name=name, grid=(L // tr,), in_specs=[row, gsp] + [row] * (len(args) - 2),
        out_specs=[row, gsp],
        out_shape=[jax.ShapeDtypeStruct((L, D), out_dtype), jax.ShapeDtypeStruct((1, D), F32)],
        compiler_params=_params(("arbitrary",)),
    )(*args)


def _gate_z(fg_ref, b_ref):
    return fg_ref[...] + b_ref[...]


def gate_fwd(fg_src, col_block, b, name):
    L = fg_src.shape[0]
    T = _div_tile(L, (ROW_TILE,))

    def body(fg_ref, b_ref, c_ref, carry):
        @pl.when(pl.program_id(0) == 0)
        def _():
            carry[...] = jnp.zeros_like(carry)

        z = _gate_z(fg_ref, b_ref)
        logf = jnp.minimum(z, 0.0) - jnp.log(1.0 + jnp.exp(-jnp.abs(z)))
        tri = (lax.broadcasted_iota(jnp.int32, (T, T), 1) <= lax.broadcasted_iota(jnp.int32, (T, T), 0)).astype(F32)
        c = jnp.dot(tri, logf, precision=lax.Precision.HIGHEST, preferred_element_type=F32) + carry[...]
        c_ref[...] = c
        carry[...] = c[T - 1:T, :]

    return pl.pallas_call(
        body, name=name, grid=(L // T,),
        in_specs=[pl.BlockSpec((T, LANES), lambda i: (i, col_block)), pl.BlockSpec((1, LANES), lambda i: (0, 0))],
        out_specs=pl.BlockSpec((T, LANES), lambda i: (i, 0)),
        out_shape=jax.ShapeDtypeStruct((L, LANES), F32),
        scratch_shapes=[pltpu.VMEM((1, LANES), F32)],
        compiler_params=_params(("arbitrary",)),
    )(fg_src, b)


def gate_bwd(fg_src, col_block, b, dc, name):
    L = fg_src.shape[0]
    T = _div_tile(L, (ROW_TILE,))
    nb = L // T

    def body(fg_ref, b_ref, dc_ref, dfg_ref, db_ref, carry):
        @pl.when(pl.program_id(0) == 0)
        def _():
            carry[...] = jnp.zeros_like(carry)
            db_ref[...] = jnp.zeros_like(db_ref)

        z = _gate_z(fg_ref, b_ref)
        dcv = dc_ref[...]
        tri = (lax.broadcasted_iota(jnp.int32, (T, T), 1) >= lax.broadcasted_iota(jnp.int32, (T, T), 0)).astype(F32)
        dlogf = jnp.dot(tri, dcv, precision=lax.Precision.HIGHEST, preferred_element_type=F32) + carry[...]
        dfg = dlogf * _sigmoid(-z)
        dfg_ref[...] = dfg
        db_ref[...] += jnp.sum(dfg, axis=0, keepdims=True)
        carry[...] = dlogf[0:1, :]

    return pl.pallas_call(
        body, name=name, grid=(nb,),
        in_specs=[pl.BlockSpec((T, LANES), lambda i: (nb - 1 - i, col_block)),
                  pl.BlockSpec((1, LANES), lambda i: (0, 0)),
                  pl.BlockSpec((T, LANES), lambda i: (nb - 1 - i, 0))],
        out_specs=[pl.BlockSpec((T, LANES), lambda i: (nb - 1 - i, 0)), pl.BlockSpec((1, LANES), lambda i: (0, 0))],
        out_shape=[jax.ShapeDtypeStruct((L, LANES), F32), jax.ShapeDtypeStruct((1, LANES), F32)],
        scratch_shapes=[pltpu.VMEM((1, LANES), F32)],
        compiler_params=_params(("arbitrary",)),
    )(fg_src, b, dc)


def attn_fwd(proj, cq_col, ck_row, name, comm=None):
    L = proj.shape[0]
    T = _div_tile(L, (ROW_TILE,))
    nq = L // T
    npair = HEADS // 2
    scale = HEAD_DIM ** -0.5
    SUB = ATTN_ROWS
    nsub = T // SUB

    def body(q_ref, k_ref, v_ref, cq_ref, ck_ref, o_ref, lse_ref):
        qb = pl.program_id(1)
        rows = [slice(r * SUB, (r + 1) * SUB) for r in range(nsub)]
        head1 = lax.broadcasted_iota(jnp.int32, (SUB, LANES), 1) >= HEAD_DIM
        qs = [[jnp.where(head1 == (h == 1), q_ref[rs, :] * scale, 0.0).astype(BF16) for rs in rows] for h in range(2)]
        cqs = [[cq_ref[0, rs, h:h + 1] for rs in rows] for h in range(2)]

        def step(kb, carry, masked):
            ks = pl.multiple_of(kb * T, T)
            k = k_ref[pl.ds(ks, T), :]
            v = v_ref[pl.ds(ks, T), :]
            lane = lax.broadcasted_iota(jnp.int32, (T, LANES), 1)
            new = []
            for h in range(2):
                ck = ck_ref[0, h:h + 1, pl.ds(ks, T)]
                vh = jnp.where(lane == spare[h], 1.0, v).astype(BF16)
                for r in range(nsub):
                    m, acc = carry[h * nsub + r]
                    s = lax.dot_general(qs[h][r], k, _DIMS["nt"], preferred_element_type=F32) + cqs[h][r] - ck
                    if masked:
                        keep = (lax.broadcasted_iota(jnp.int32, (SUB, T), 1)
                                <= lax.broadcasted_iota(jnp.int32, (SUB, T), 0) + r * SUB)
                        s = jnp.where(keep, s, NEG_BIG)
                    m_new = jnp.maximum(m, jnp.max(s, axis=1, keepdims=True))
                    p = jnp.exp(s - m_new)
                    acc = jnp.exp(m - m_new) * acc + jnp.dot(p.astype(BF16), vh, preferred_element_type=F32)
                    new.append((m_new, acc))
            return tuple(new)

        spare = (HEAD_DIM, 0)
        one = (jnp.full((SUB, 1), NEG_BIG, F32), jnp.zeros((SUB, LANES), F32))
        carry = lax.fori_loop(0, qb, functools.partial(step, masked=False), (one,) * (2 * nsub))
        carry = step(qb, carry, True)
        out, lse = [], []
        for h in range(2):
            chains = carry[h * nsub:(h + 1) * nsub]
            ls = [acc[:, spare[h]:spare[h] + 1] for _, acc in chains]
            out.append(jnp.concatenate([acc / l for (_, acc), l in zip(chains, ls)], axis=0))
            lse.append(jnp.concatenate([m + jnp.log(l) for (m, _), l in zip(chains, ls)], axis=0))
        o_ref[...] = jnp.where(lax.broadcasted_iota(jnp.int32, (T, LANES), 1) >= HEAD_DIM, out[1], out[0]
                               ).astype(o_ref.dtype)
        lse_ref[0] = jnp.concatenate(lse, axis=1)

    return carrier_call(
        body, name, (npair, nq),
        [pl.BlockSpec((T, LANES), lambda p, i: (i, p)),
         pl.BlockSpec((L, LANES), lambda p, i: (0, npair + p)),
         pl.BlockSpec((L, LANES), lambda p, i: (0, 2 * npair + p)),
         pl.BlockSpec((1, T, 2), lambda p, i: (p, i, 0)),
         pl.BlockSpec((1, 2, L), lambda p, i: (p, 0, 0))],
        [pl.BlockSpec((T, LANES), lambda p, i: (i, p)), pl.BlockSpec((1, T, 2), lambda p, i: (p, i, 0))],
        [jax.ShapeDtypeStruct((L, ATTN_W), BF16), jax.ShapeDtypeStruct((npair, L, 2), F32)],
        [], (proj, proj, proj, cq_col, ck_row), comm, ("parallel", "parallel"))


def attn_delta(dcat, cat, name):
    L = dcat.shape[0]
    T = _div_tile(L, (ROW_TILE,))

    def body(do_ref, o_ref, d_ref):
        prod = do_ref[...] * o_ref[...].astype(F32)
        sel = (lax.broadcasted_iota(jnp.int32, (ATTN_W, LANES), 0) // HEAD_DIM
               == lax.broadcasted_iota(jnp.int32, (ATTN_W, LANES), 1)).astype(F32)
        d_ref[...] = jnp.dot(prod, sel, precision=lax.Precision.HIGHEST, preferred_element_type=F32)

    return pl.pallas_call(
        body, name=name, grid=(L // T,),
        in_specs=[pl.BlockSpec((T, ATTN_W), lambda i: (i, 0)), pl.BlockSpec((T, ATTN_W), lambda i: (i, 0))],
        out_specs=pl.BlockSpec((T, LANES), lambda i: (i, 0)),
        out_shape=jax.ShapeDtypeStruct((L, LANES), F32),
        compiler_params=_params(("parallel",)),
    )(dcat, cat)


def attn_bwd(proj, dcat, lse_row, delta_row, cq_row, ck_col, name, comm=None):
    L = proj.shape[0]
    T = _div_tile(L, (ROW_TILE,))
    nb = L // T
    npair = HEADS // 2
    scale = HEAD_DIM ** -0.5

    def body(q_ref, k_ref, v_ref, do_ref, lse_ref, dl_ref, cq_ref, ck_ref,
             dq_ref, dk_ref, dv_ref, dcq_ref, dck_ref, dq_acc, dcq_acc):
        kb = pl.program_id(1)

        @pl.when(kb == 0)
        def _():
            dq_acc[...] = jnp.zeros_like(dq_acc)
            dcq_acc[...] = jnp.zeros_like(dcq_acc)

        head1 = lax.broadcasted_iota(jnp.int32, (T, LANES), 1) >= HEAD_DIM
        ks = [jnp.where(head1 == (h == 1), k_ref[...] * scale, 0.0).astype(BF16) for h in range(2)]
        vs = [jnp.where(head1 == (h == 1), v_ref[...], 0.0).astype(BF16) for h in range(2)]
        cks = [ck_ref[0, :, h:h + 1] for h in range(2)]

        def step(qb, carry, masked):
            qs = pl.multiple_of(qb * T, T)
            q = q_ref[pl.ds(qs, T), :]
            do = do_ref[pl.ds(qs, T), :].astype(BF16)
            new, dq = [], None
            for h in range(2):
                dk, dv, dck = carry[h]
                lse = lse_ref[0, h:h + 1, pl.ds(qs, T)]
                dl = dl_ref[0, h:h + 1, pl.ds(qs, T)]
                cq = cq_ref[0, h:h + 1, pl.ds(qs, T)]
                st = lax.dot_general(ks[h], q, _DIMS["nt"], preferred_element_type=F32) + cq - cks[h]
                if masked:
                    keep = lax.broadcasted_iota(jnp.int32, (T, T), 0) <= lax.broadcasted_iota(jnp.int32, (T, T), 1)
                    st = jnp.where(keep, st, NEG_BIG)
                pt = jnp.exp(st - lse)
                dv = dv + jnp.dot(pt.astype(BF16), do, preferred_element_type=F32)
                dpt = lax.dot_general(vs[h], do, _DIMS["nt"], preferred_element_type=F32)
                dst = pt * (dpt - dl)
                dsb = dst.astype(BF16)
                dk = dk + jnp.dot(dsb, q, preferred_element_type=F32)
                part = lax.dot_general(dsb, ks[h], _DIMS["tn"], preferred_element_type=F32)
                dq = part if dq is None else dq + part
                dcq_acc[h:h + 1, pl.ds(qs, T)] += jnp.sum(dst, axis=0, keepdims=True)
                dck = dck + jnp.sum(dst, axis=1, keepdims=True)
                new.append((dk, dv, dck))
            dq_acc[pl.ds(qs, T), :] += dq
            return tuple(new)

        one = (jnp.zeros((T, LANES), F32), jnp.zeros((T, LANES), F32), jnp.zeros((T, 1), F32))
        carry = step(kb, (one, one), True)
        carry = lax.fori_loop(kb + 1, nb, functools.partial(step, masked=False), carry)
        (dk0, dv0, dck0), (dk1, dv1, dck1) = carry
        dk_ref[...] = (jnp.where(head1, dk1, dk0) * scale).astype(dk_ref.dtype)
        dv_ref[...] = jnp.where(head1, dv1, dv0).astype(dv_ref.dtype)
        dck_ref[0] = jnp.concatenate([-dck0, -dck1], axis=1)

        @pl.when(kb == nb - 1)
        def _():
            dq_ref[...] = dq_acc[...].astype(dq_ref.dtype)
            dcq_ref[0] = dcq_acc[...]

    full = lambda col: pl.BlockSpec((L, LANES), col)
    row_stat = pl.BlockSpec((1, 2, L), lambda p, i: (p, 0, 0))
    return carrier_call(
        body, name, (npair, nb),
        [full(lambda p, i: (0, p)),
         pl.BlockSpec((T, LANES), lambda p, i: (i, npair + p)),
         pl.BlockSpec((T, LANES), lambda p, i: (i, 2 * npair + p)),
         full(lambda p, i: (0, p)),
         row_stat, row_stat, row_stat,
         pl.BlockSpec((1, T, 2), lambda p, i: (p, i, 0))],
        [full(lambda p, i: (0, p)),
         pl.BlockSpec((T, LANES), lambda p, i: (i, p)),
         pl.BlockSpec((T, LANES), lambda p, i: (i, p)),
         row_stat,
         pl.BlockSpec((1, T, 2), lambda p, i: (p, i, 0))],
        [jax.ShapeDtypeStruct((L, ATTN_W), BF16)] * 3
        + [jax.ShapeDtypeStruct((npair, 2, L), F32), jax.ShapeDtypeStruct((npair, L, 2), F32)],
        [pltpu.VMEM((L, LANES), F32), pltpu.VMEM((2, L), F32)],
        (proj, proj, proj, dcat, lse_row, delta_row, cq_row, ck_col), comm, ("parallel", "arbitrary"))


def _shift_down(x, k):
    rolled = pltpu.roll(x, k, 0)
    return jnp.where(lax.broadcasted_iota(jnp.int32, x.shape, 0) >= k, rolled, 0.0)


def _shift_up(x, k):
    n = x.shape[0]
    rolled = pltpu.roll(x, n - k, 0)
    return jnp.where(lax.broadcasted_iota(jnp.int32, x.shape, 0) < n - k, rolled, 0.0)


def conv_fwd(proj, col0, conv_w, conv_b, name):
    L = proj.shape[0]
    C = conv_w.shape[1]
    nc = C // LANES

    def body(gb_ref, gc_ref, xc_ref, w_ref, b_ref, o_ref):
        z = gc_ref[...] * xc_ref[...]
        conv = (w_ref[0:1, :] * _shift_down(z, 2) + w_ref[1:2, :] * _shift_down(z, 1) + w_ref[2:3, :] * z
                + b_ref[...])
        o_ref[...] = (gb_ref[...] * conv).astype(o_ref.dtype)

    col = lambda off: pl.BlockSpec((L, LANES), lambda j, off=off: (0, col0 + off + j))
    return pl.pallas_call(
        body, name=name, grid=(nc,),
        in_specs=[col(0), col(nc), col(2 * nc), pl.BlockSpec((CONV_K, LANES), lambda j: (0, j)),
                  pl.BlockSpec((1, LANES), lambda j: (0, j))],
        out_specs=pl.BlockSpec((L, LANES), lambda j: (0, j)),
        out_shape=jax.ShapeDtypeStruct((L, C), BF16),
        compiler_params=_params(("parallel",)),
    )(proj, proj, proj, conv_w, conv_b)


def conv_bwd(proj, col0, conv_w, conv_b, dcat, dcol0, name):
    L = proj.shape[0]
    C = conv_w.shape[1]
    nc = C // LANES

    def body(gb_ref, gc_ref, xc_ref, w_ref, b_ref, do_ref, dgb_ref, dgc_ref, dxc_ref, dw_ref, db_ref):
        gc, xc = gc_ref[...], xc_ref[...]
        z = gc * xc
        z1, z2 = _shift_down(z, 1), _shift_down(z, 2)
        w0, w1, w2 = w_ref[0:1, :], w_ref[1:2, :], w_ref[2:3, :]
        conv = w0 * z2 + w1 * z1 + w2 * z + b_ref[...]
        dout = do_ref[...]
        dgb_ref[...] = (dout * conv).astype(dgb_ref.dtype)
        dconv = dout * gb_ref[...]
        dw_ref[...] = jnp.concatenate([jnp.sum(dconv * z2, axis=0, keepdims=True),
                                       jnp.sum(dconv * z1, axis=0, keepdims=True),
                                       jnp.sum(dconv * z, axis=0, keepdims=True)], axis=0)
        db_ref[...] = jnp.sum(dconv, axis=0, keepdims=True)
        dz = w2 * dconv + w1 * _shift_up(dconv, 1) + w0 * _shift_up(dconv, 2)
        dgc_ref[...] = (dz * xc).astype(dgc_ref.dtype)
        dxc_ref[...] = (dz * gc).astype(dxc_ref.dtype)

    col = lambda off: pl.BlockSpec((L, LANES), lambda j, off=off: (0, col0 + off + j))
    out_col = pl.BlockSpec((L, LANES), lambda j: (0, j))
    return pl.pallas_call(
        body, name=name, grid=(nc,),
        in_specs=[col(0), col(nc), col(2 * nc), pl.BlockSpec((CONV_K, LANES), lambda j: (0, j)),
                  pl.BlockSpec((1, LANES), lambda j: (0, j)),
                  pl.BlockSpec((L, LANES), lambda j: (0, dcol0 + j))],
        out_specs=[out_col, out_col, out_col, pl.BlockSpec((CONV_K, LANES), lambda j: (0, j)),
                   pl.BlockSpec((1, LANES), lambda j: (0, j))],
        out_shape=[jax.ShapeDtypeStruct((L, C), BF16)] * 3
        + [jax.ShapeDtypeStruct((CONV_K, C), F32), jax.ShapeDtypeStruct((1, C), F32)],
        compiler_params=_params(("parallel",)),
    )(proj, proj, proj, conv_w, conv_b, dcat)


_GELU_C = math.sqrt(2.0 / math.pi)
_GELU_A = 0.044715


def _gelu(y):
    return 0.5 * y * (1.0 + jnp.tanh(_GELU_C * (y + _GELU_A * y * y * y)))


def _gelu_grad(y):
    t = jnp.tanh(_GELU_C * (y + _GELU_A * y * y * y))
    return 0.5 * (1.0 + t) + 0.5 * y * (1.0 - t * t) * _GELU_C * (1.0 + 3.0 * _GELU_A * y * y)


def _cmul_add(xr, xi, pr, pi, sr, si):
    return xr + pr * sr - pi * si, xi + pr * si + pi * sr


def _scan_tile(br, bi, cr, ci, tab_ref, reverse):
    n = S5_BLOCK_STATES
    xr, xi = br, bi
    for s, k in enumerate((1, 2, 4)):
        shift = SUBLANES - k if reverse else k
        xr, xi = _cmul_add(xr, xi, tab_ref[0, s, :, :n], tab_ref[0, s, :, n:],
                           pltpu.roll(xr, shift, 0), pltpu.roll(xi, shift, 0))
    return _cmul_add(xr, xi, tab_ref[0, 3, :, :n], tab_ref[0, 3, :, n:], cr, ci)


def s5_fwd(u, bmat, cmat, tab, dvec, name, comm=None):
    L, D = u.shape
    nblk = D // LANES
    T = _div_tile(L, (ROW_TILE,))
    ns = 2 * S5_BLOCK_STATES
    n = S5_BLOCK_STATES

    def body(u_ref, b_ref, c_ref, tab_ref, d_ref, y_ref, g_ref, xs_ref, buf, car):
        @pl.when(pl.program_id(1) == 0)
        def _():
            car[...] = jnp.zeros_like(car)

        uv = u_ref[...]
        buf[...] = jnp.dot(uv.astype(BF16), b_ref[0], preferred_element_type=F32)

        def tile(i, carry):
            cr, ci = carry
            r0 = pl.multiple_of(i * SUBLANES, SUBLANES)
            xr, xi = _scan_tile(buf[pl.ds(r0, SUBLANES), :n], buf[pl.ds(r0, SUBLANES), n:], cr, ci, tab_ref, False)
            buf[pl.ds(r0, SUBLANES), :n] = xr
            buf[pl.ds(r0, SUBLANES), n:] = xi
            return xr[SUBLANES - 1:, :], xi[SUBLANES - 1:, :]

        cr, ci = lax.fori_loop(0, T // SUBLANES, tile, (car[:, :n], car[:, n:]))
        car[:, :n] = cr
        car[:, n:] = ci
        xs = buf[...]
        xs_ref[...] = xs
        y = jnp.dot(xs.astype(BF16), c_ref[0], preferred_element_type=F32) + d_ref[...] * uv
        y_ref[...] = y
        g_ref[...] = _gelu(y).astype(g_ref.dtype)

    blk = pl.BlockSpec((T, LANES), lambda j, i: (i, j))
    return carrier_call(
        body, name, (nblk, L // T),
        [blk, pl.BlockSpec((1, LANES, ns), lambda j, i: (j, 0, 0)),
         pl.BlockSpec((1, ns, LANES), lambda j, i: (j, 0, 0)),
         pl.BlockSpec((1, 4, SUBLANES, ns), lambda j, i: (j, 0, 0, 0)),
         pl.BlockSpec((1, LANES), lambda j, i: (0, j))],
        [blk, blk, pl.BlockSpec((T, ns), lambda j, i: (i, j))],
        [jax.ShapeDtypeStruct((L, D), F32), jax.ShapeDtypeStruct((L, D), BF16),
         jax.ShapeDtypeStruct((L, nblk * ns), F32)],
        [pltpu.VMEM((T, ns), F32), pltpu.VMEM((1, ns), F32)],
        (u, bmat, cmat, tab, dvec), comm, ("parallel", "arbitrary"))


def s5_bwd(dg, y, u, xs, cmat_t, bmat_t, rtab, dvec, name, comm=None):
    L, D = u.shape
    nblk = D // LANES
    T = _div_tile(L, (ROW_TILE,))
    nch = L // T
    ns = 2 * S5_BLOCK_STATES
    n = S5_BLOCK_STATES
    ntile = T // SUBLANES

    def body(dg_ref, y_ref, u_ref, xs_ref, xp_ref, ct_ref, bt_ref, tab_ref, d_ref,
             du_ref, dc_ref, db_ref, dlam_ref, dd_ref, buf, xbuf, car):
        step = pl.program_id(1)
        first_chunk = step == nch - 1

        @pl.when(step == 0)
        def _():
            car[...] = jnp.zeros_like(car)
            dc_ref[...] = jnp.zeros_like(dc_ref)
            db_ref[...] = jnp.zeros_like(db_ref)
            dlam_ref[...] = jnp.zeros_like(dlam_ref)
            dd_ref[...] = jnp.zeros_like(dd_ref)

        uv = u_ref[...]
        dy = dg_ref[...].astype(F32) * _gelu_grad(y_ref[...])
        dd_ref[...] += jnp.sum(dy * uv, axis=0, keepdims=True)
        dyb = dy.astype(BF16)
        buf[...] = jnp.dot(dyb, ct_ref[0], preferred_element_type=F32)
        xs = xs_ref[...]
        xbuf[pl.ds(SUBLANES, T), :] = xs
        xbuf[pl.ds(0, SUBLANES), :] = jnp.where(first_chunk, 0.0, xp_ref[...])
        row0 = lax.broadcasted_iota(jnp.int32, (SUBLANES, n), 0) == 0

        def tile(ii, carry):
            cr, ci, ar, ai = carry
            r0 = pl.multiple_of((ntile - 1 - ii) * SUBLANES, SUBLANES)
            xr, xi = _scan_tile(buf[pl.ds(r0, SUBLANES), :n], buf[pl.ds(r0, SUBLANES), n:], cr, ci, tab_ref, True)
            buf[pl.ds(r0, SUBLANES), :n] = xr
            buf[pl.ds(r0, SUBLANES), n:] = xi
            r1 = pl.multiple_of(r0 + SUBLANES, SUBLANES)
            pr = jnp.where(row0, xbuf[pl.ds(r0, SUBLANES), :n][SUBLANES - 1:, :],
                           pltpu.roll(xbuf[pl.ds(r1, SUBLANES), :n], 1, 0))
            pi = jnp.where(row0, xbuf[pl.ds(r0, SUBLANES), n:][SUBLANES - 1:, :],
                           pltpu.roll(xbuf[pl.ds(r1, SUBLANES), n:], 1, 0))
            ar = ar + xr * pr + xi * pi
            ai = ai + xi * pr - xr * pi
            return xr[0:1, :], xi[0:1, :], ar, ai

        zero = jnp.zeros((SUBLANES, n), F32)
        cr, ci, ar, ai = lax.fori_loop(0, ntile, tile, (car[:, :n], car[:, n:], zero, zero))
        car[:, :n] = cr
        car[:, n:] = ci
        dlam_ref[0, :, :n] += ar
        dlam_ref[0, :, n:] += ai
        dxa = buf[...]
        dc_ref[0] += lax.dot_general(dyb, xs.astype(BF16), _DIMS["tn"], preferred_element_type=F32)
        dxb = dxa.astype(BF16)
        db_ref[0] += lax.dot_general(uv.astype(BF16), dxb, _DIMS["tn"], preferred_element_type=F32)
        du_ref[...] = jnp.dot(dxb, bt_ref[0], preferred_element_type=F32) + d_ref[...] * dy

    rev = lambda j, i: (nch - 1 - i, j)
    blk = pl.BlockSpec((T, LANES), rev)
    tpb = T // SUBLANES
    acc = pl.BlockSpec((1, LANES, ns), lambda j, i: (j, 0, 0))
    return carrier_call(
        body, name, (nblk, nch),
        [blk, blk, blk, pl.BlockSpec((T, ns), rev),
         pl.BlockSpec((SUBLANES, ns), lambda j, i: (jnp.maximum((nch - 1 - i) * tpb - 1, 0), j)),
         pl.BlockSpec((1, LANES, ns), lambda j, i: (j, 0, 0)),
         pl.BlockSpec((1, ns, LANES), lambda j, i: (j, 0, 0)),
         pl.BlockSpec((1, 4, SUBLANES, ns), lambda j, i: (j, 0, 0, 0)),
         pl.BlockSpec((1, LANES), lambda j, i: (0, j))],
        [blk, acc, acc, pl.BlockSpec((1, SUBLANES, ns), lambda j, i: (j, 0, 0)),
         pl.BlockSpec((1, LANES), lambda j, i: (0, j))],
        [jax.ShapeDtypeStruct((L, D), F32), jax.ShapeDtypeStruct((nblk, LANES, ns), F32),
         jax.ShapeDtypeStruct((nblk, LANES, ns), F32), jax.ShapeDtypeStruct((nblk, SUBLANES, ns), F32),
         jax.ShapeDtypeStruct((1, D), F32)],
        [pltpu.VMEM((T, ns), F32), pltpu.VMEM((T + SUBLANES, ns), F32), pltpu.VMEM((1, ns), F32)],
        (dg, y, u, xs, xs, cmat_t, bmat_t, rtab, dvec), comm, ("parallel", "arbitrary"))


def _s5_discretize(a_re, a_im, log_step, b_re, b_im):
    lam_re = jnp.minimum(a_re, -S5_MIN_DECAY)
    lam_im = a_im
    delta = jnp.exp(log_step)[:, None]
    mag = jnp.exp(lam_re * delta)
    ang = lam_im * delta
    lb_re = mag * jnp.cos(ang)
    lb_im = mag * jnp.sin(ang)
    den = lam_re * lam_re + lam_im * lam_im
    nr = lb_re - 1.0
    ni = lb_im
    coef_re = (nr * lam_re + ni * lam_im) / den
    coef_im = (ni * lam_re - nr * lam_im) / den
    bb_re = coef_re[..., None] * b_re - coef_im[..., None] * b_im
    bb_im = coef_re[..., None] * b_im + coef_im[..., None] * b_re
    return lb_re, lb_im, bb_re, bb_im


def _s5_tables(lb_re, lb_im):
    nblk = lb_re.shape[0] // S5_BLOCK_GROUPS
    lr = lb_re.reshape(nblk, S5_BLOCK_STATES)
    li = lb_im.reshape(nblk, S5_BLOCK_STATES)
    pows = [(jnp.ones_like(lr), jnp.zeros_like(li))]
    for _ in range(SUBLANES):
        pr, pi = pows[-1]
        pows.append((pr * lr - pi * li, pr * li + pi * lr))
    rows = jnp.arange(SUBLANES)[None, :, None]

    def table(conj, reverse):
        sgn = -1.0 if conj else 1.0
        out = []
        for k in (1, 2, 4):
            mask = (rows <= SUBLANES - 1 - k) if reverse else (rows >= k)
            out.append(jnp.concatenate([jnp.where(mask, pows[k][0][:, None, :], 0.0),
                                        jnp.where(mask, sgn * pows[k][1][:, None, :], 0.0)], axis=-1))
        order = range(SUBLANES, 0, -1) if reverse else range(1, SUBLANES + 1)
        cre = jnp.stack([pows[k][0] for k in order], axis=1)
        cim = jnp.stack([sgn * pows[k][1] for k in order], axis=1)
        out.append(jnp.concatenate([cre, cim], axis=-1))
        return jnp.stack(out, axis=1)

    return table(False, False), table(True, True)


def _s5_block_mats(bb_re, bb_im, c_re, c_im):
    G = bb_re.shape[0]
    nblk = G // S5_BLOCK_GROUPS
    eye = jnp.eye(S5_BLOCK_GROUPS, dtype=F32)
    bb = jnp.stack([bb_re, bb_im]).reshape(2, nblk, S5_BLOCK_GROUPS, S5_STATE, S5_GROUP)
    bmat = jnp.einsum("ab,rjaph->jahrbp", eye, bb).reshape(nblk, LANES, 2 * S5_BLOCK_STATES)
    cc = jnp.stack([c_re, -c_im]).reshape(2, nblk, S5_BLOCK_GROUPS, S5_GROUP, S5_STATE)
    cmat = jnp.einsum("ab,rjahp->jrbpah", eye, cc).reshape(nblk, 2 * S5_BLOCK_STATES, LANES)
    return bmat, cmat


def _s5_unblock(dB, dC, dlam):
    nblk = dB.shape[0]
    G = nblk * S5_BLOCK_GROUPS
    d6 = dB.reshape(nblk, S5_BLOCK_GROUPS, S5_GROUP, 2, S5_BLOCK_GROUPS, S5_STATE)
    dbb = jnp.einsum("jahrap->rjaph", d6).reshape(2, G, S5_STATE, S5_GROUP)
    c6 = dC.reshape(nblk, S5_BLOCK_GROUPS, S5_GROUP, 2, S5_BLOCK_GROUPS, S5_STATE)
    dcc = jnp.einsum("jahrap->rjahp", c6).reshape(2, G, S5_GROUP, S5_STATE)
    dl = jnp.sum(dlam, axis=1).reshape(nblk, 2, S5_BLOCK_GROUPS, S5_STATE)
    dl = jnp.transpose(dl, (1, 0, 2, 3)).reshape(2, G, S5_STATE)
    return dbb[0], dbb[1], dcc[0], -dcc[1], dl[0], dl[1]


def loss_and_grad(y, target, name):
    L, D = y.shape
    tr = _div_tile(L, (512, 256, 128))

    def body(y_ref, t_ref, dy_ref, loss_ref):
        err = y_ref[...] - t_ref[...]
        dy_ref[...] = err * (1.0 / D)
        part = 0.5 * jnp.sum(jnp.mean(err * err, axis=-1, keepdims=True), axis=0, keepdims=True)

        @pl.when(pl.program_id(0) == 0)
        def _():
            loss_ref[...] = part

        @pl.when(pl.program_id(0) > 0)
        def _():
            loss_ref[...] += part

    row = pl.BlockSpec((tr, D), lambda i: (i, 0))
    return pl.pallas_call(
        body, name=name, grid=(L // tr,), in_specs=[row, row],
        out_specs=[row, pl.BlockSpec((1, 1), lambda i: (0, 0))],
        out_shape=[jax.ShapeDtypeStruct((L, D), F32), jax.ShapeDtypeStruct((1, 1), F32)],
        compiler_params=_params(("arbitrary",)),
    )(y, target)


def _adam_math(w, g, m, v):
    m = ADAM_B1 * m + (1.0 - ADAM_B1) * g
    v = ADAM_B2 * v + (1.0 - ADAM_B2) * (g * g)
    m_hat = m / (1.0 - ADAM_B1 ** ADAM_STEP)
    v_hat = v / (1.0 - ADAM_B2 ** ADAM_STEP)
    delta = -ADAM_LR * (m_hat / (jnp.sqrt(v_hat) + ADAM_EPS) + ADAM_WD * w)
    return delta, m, v


def _as3d(a):
    return a.reshape((-1,) + a.shape[-2:])


def adamw(w, g, m, v, name):
    shape = w.shape
    w3, g3, m3, v3 = _as3d(w), _as3d(g), _as3d(m), _as3d(v)
    A, R, C = w3.shape
    tr = _row_tile(R, C)

    def body(w_ref, g_ref, m_ref, v_ref, d_ref, mo_ref, vo_ref):
        d, mn, vn = _adam_math(w_ref[...], g_ref[...], m_ref[...], v_ref[...])
        d_ref[...] = d
        mo_ref[...] = mn
        vo_ref[...] = vn

    spec = pl.BlockSpec((1, tr, C), lambda a, i: (a, i, 0))
    outs = pl.pallas_call(
        body, name=name, grid=(A, R // tr), in_specs=[spec] * 4, out_specs=[spec] * 3,
        out_shape=[jax.ShapeDtypeStruct((A, R, C), F32)] * 3,
        compiler_params=_params(("parallel", "parallel")),
    )(w3, g3, m3, v3)
    return [o.reshape(shape) for o in outs]


def _gathered_shape(shape, kind):
    n, R, C = shape
    return {"stack": (4, n, R, C), "row": (n, 4 * R, C), "col": (n, R, 4 * C)}[kind]


def _own_shard_spec(kind, tr, R, C):
    if kind == "stack":
        return pl.BlockSpec((None, None, tr, C), lambda a, i: (_my_chip(), a, i, 0))
    if kind == "row":
        return pl.BlockSpec((None, tr, C), lambda a, i: (a, _my_chip() * (R // tr) + i, 0))
    return pl.BlockSpec((None, tr, C), lambda a, i: (a, i, _my_chip()))


def cast_into_gathered(shard, kind, name):
    n, R, C = shard.shape
    tr = _row_tile(R, C)

    def body(a_ref, o_ref):
        o_ref[...] = a_ref[...].astype(BF16)

    return pl.pallas_call(
        body, name=name, grid=(n, R // tr),
        in_specs=[pl.BlockSpec((None, tr, C), lambda a, i: (a, i, 0))],
        out_specs=_own_shard_spec(kind, tr, R, C),
        out_shape=jax.ShapeDtypeStruct(_gathered_shape(shard.shape, kind), BF16),
        compiler_params=_params(("parallel", "parallel")),
    )(shard)


def _place():
    x, y, c = lax.axis_index("x"), lax.axis_index("y"), lax.axis_index("c")
    other_chips = [(1 - x, y), (x, 1 - y), (1 - x, 1 - y)]
    return x, y, c, other_chips


def _chip_id(chip):
    return 2 * chip[0] + chip[1]


def _my_chip():
    return 2 * lax.axis_index("x") + lax.axis_index("y")


def _shard_view(ref, kind, shard, lo, cnt):
    if kind == "stack":
        return ref.at[shard, pl.ds(lo, cnt)]
    if kind == "row":
        R = ref.shape[1] // 4
        return ref.at[pl.ds(lo, cnt), pl.ds(pl.multiple_of(shard * R, 16), R), :]
    C = ref.shape[2] // 4
    return ref.at[pl.ds(lo, cnt), :, pl.ds(pl.multiple_of(shard * C, LANES), C)]


def _layer_view(ref, kind, lo, cnt):
    if kind == "stack":
        return ref.at[:, pl.ds(lo, cnt)]
    return ref.at[pl.ds(lo, cnt)]


def _n_layers(ref, kind):
    return ref.shape[1 if kind == "stack" else 0]


def _remote(src, dst, send_sem, recv_sem, dev):
    return pltpu.make_async_remote_copy(src_ref=src, dst_ref=dst, send_sem=send_sem, recv_sem=recv_sem,
                                        device_id=dev, device_id_type=MESH)


def allgather_weights(gathered, kinds):
    T = len(gathered)

    def body(*refs):
        outs = refs[T:2 * T]
        send_sems, recv_sems = refs[2 * T:]
        x, y, c, chips = _place()
        me, sibling = _chip_id((x, y)), (x, y, 1 - c)
        sends = []
        for t in range(T):
            hn = _n_layers(outs[t], kinds[t]) // 2
            mine = _shard_view(outs[t], kinds[t], me, c * hn, hn)
            for j, chip in enumerate(chips):
                cp = _remote(mine, mine, send_sems.at[6 * t + j], recv_sems.at[6 * t + j], (*chip, c))
                cp.start()
                sends.append(cp)
        for t in range(T):
            hn = _n_layers(outs[t], kinds[t]) // 2
            for j, chip in enumerate(chips):
                piece = _shard_view(outs[t], kinds[t], _chip_id(chip), c * hn, hn)
                cp = _remote(piece, piece, send_sems.at[6 * t + 3 + j], recv_sems.at[6 * t + j], sibling)
                cp.wait_recv()
                fwd = _remote(piece, piece, send_sems.at[6 * t + 3 + j], recv_sems.at[6 * t + 3 + j], sibling)
                fwd.start()
                sends.append(fwd)
        for t in range(T):
            hn = _n_layers(outs[t], kinds[t]) // 2
            for j, chip in enumerate(chips):
                piece = _shard_view(outs[t], kinds[t], _chip_id(chip), (1 - c) * hn, hn)
                _remote(piece, piece, send_sems.at[6 * t + 3 + j], recv_sems.at[6 * t + 3 + j], sibling).wait_recv()
        for cp in sends:
            cp.wait_send()

    return pl.pallas_call(
        body, name="allgather_weights", in_specs=[ANY] * T, out_specs=[ANY] * T,
        out_shape=[jax.ShapeDtypeStruct(g.shape, g.dtype) for g in gathered],
        input_output_aliases={t: t for t in range(T)},
        scratch_shapes=[pltpu.SemaphoreType.DMA((6 * T,)), pltpu.SemaphoreType.DMA((6 * T,))],
        compiler_params=pltpu.CompilerParams(has_side_effects=True),
    )(*gathered)


def allgather_small(arrs):
    T = len(arrs)

    def body(*refs):
        ins, outs = refs[:T], refs[T:2 * T]
        send_sems, recv_sems = refs[2 * T:]
        x, y, c, chips = _place()
        me = _chip_id((x, y))
        sends = []
        for t in range(T):
            outs[t][me] = ins[t][...]
            for j, chip in enumerate(chips):
                cp = _remote(ins[t], outs[t].at[me], send_sems.at[3 * t + j], recv_sems.at[3 * t + j], (*chip, c))
                cp.start()
                sends.append(cp)
        for t in range(T):
            for j, chip in enumerate(chips):
                slot = outs[t].at[_chip_id(chip)]
                _remote(slot, slot, send_sems.at[3 * t + j], recv_sems.at[3 * t + j], (*chip, c)).wait_recv()
        for cp in sends:
            cp.wait_send()

    return pl.pallas_call(
        body, name="allgather_small", in_specs=[VMEM_SPEC] * T, out_specs=[VMEM_SPEC] * T,
        out_shape=[jax.ShapeDtypeStruct((4,) + a.shape, a.dtype) for a in arrs],
        scratch_shapes=[pltpu.SemaphoreType.DMA((3 * T,)), pltpu.SemaphoreType.DMA((3 * T,))],
        compiler_params=pltpu.CompilerParams(vmem_limit_bytes=VMEM_LIMIT, has_side_effects=True),
    )(*arrs)


def allreduce_small(buf):
    R, C = buf.shape

    def body(in_ref, out_ref, pair_ref, all_ref, send_sems, recv_sems):
        x, y, c, chips = _place()
        me, sibling = _chip_id((x, y)), (x, y, 1 - c)
        swap = _remote(in_ref, pair_ref, send_sems.at[0], recv_sems.at[0], sibling)
        swap.start()
        swap.wait()
        all_ref[me] = in_ref[...] + pair_ref[...]
        sends = []
        for j, chip in enumerate(chips):
            cp = _remote(all_ref.at[me], all_ref.at[me], send_sems.at[1 + j], recv_sems.at[1 + j], (*chip, c))
            cp.start()
            sends.append(cp)
        for j, chip in enumerate(chips):
            slot = all_ref.at[_chip_id(chip)]
            _remote(slot, slot, send_sems.at[1 + j], recv_sems.at[1 + j], (*chip, c)).wait_recv()
        for cp in sends:
            cp.wait_send()
        out_ref[...] = ((all_ref[0] + all_ref[1]) + all_ref[2]) + all_ref[3]

    return pl.pallas_call(
        body, name="allreduce_small", in_specs=[VMEM_SPEC], out_specs=VMEM_SPEC,
        out_shape=jax.ShapeDtypeStruct((R, C), F32),
        scratch_shapes=[pltpu.VMEM((R, C), F32), pltpu.VMEM((4, R, C), F32),
                        pltpu.SemaphoreType.DMA((4,)), pltpu.SemaphoreType.DMA((4,))],
        compiler_params=pltpu.CompilerParams(vmem_limit_bytes=VMEM_LIMIT, has_side_effects=True),
    )(buf)


def _half_shape(shape, kind):
    s = list(shape)
    s[1 if kind == "stack" else 0] //= 2
    return tuple(s)


def rs_swap_halves(grads, kinds):
    T = len(grads)

    def body(*refs):
        ins, outs = refs[:T], refs[T:2 * T]
        send_sems, recv_sems = refs[2 * T:]
        x, y, c, _ = _place()
        copies = []
        for t in range(T):
            n = ins[t].shape[1 if kinds[t] == "stack" else 0]
            hn = n // 2
            cp = _remote(_layer_view(ins[t], kinds[t], (1 - c) * hn, hn), outs[t], send_sems.at[t], recv_sems.at[t],
                         (x, y, 1 - c))
            cp.start()
            copies.append(cp)
        for cp in copies:
            cp.wait()

    return pl.pallas_call(
        body, name="rs_swap_halves", in_specs=[ANY] * T, out_specs=[ANY] * T,
        out_shape=[jax.ShapeDtypeStruct(_half_shape(g.shape, k), g.dtype) for g, k in zip(grads, kinds)],
        scratch_shapes=[pltpu.SemaphoreType.DMA((T,)), pltpu.SemaphoreType.DMA((T,))],
        compiler_params=pltpu.CompilerParams(has_side_effects=True),
    )(*grads)


def rs_pair_sum(grad, recv, kind, name):
    g3, r3 = _as3d(grad), _as3d(recv)
    A, R, C = r3.shape
    n = grad.shape[1 if kind == "stack" else 0]
    hn = n // 2
    tr = _row_tile(R, C)

    def body(g_ref, r_ref, f_ref, b_ref):
        s = g_ref[...] + r_ref[...]
        f_ref[...] = s
        b_ref[...] = s.astype(BF16)

    def g_map(a, i):
        return ((a // hn) * n + lax.axis_index("c") * hn + a % hn, i, 0)

    spec = pl.BlockSpec((1, tr, C), lambda a, i: (a, i, 0))
    f, b = pl.pallas_call(
        body, name=name, grid=(A, R // tr),
        in_specs=[pl.BlockSpec((1, tr, C), g_map), spec], out_specs=[spec, spec],
        out_shape=[jax.ShapeDtypeStruct((A, R, C), F32), jax.ShapeDtypeStruct((A, R, C), BF16)],
        compiler_params=_params(("parallel", "parallel")),
    )(g3, r3)
    return f.reshape(recv.shape), b.reshape(recv.shape)


def _shard_shape(half_gathered_shape, kind):
    full = half_gathered_shape
    if kind == "stack":
        return tuple(full[1:])
    if kind == "row":
        return (full[0], full[1] // 4, full[2])
    return (full[0], full[1], full[2] // 4)


def rs_exchange(pair_bf16, kinds):
    T = len(pair_bf16)

    def body(*refs):
        pb, got = refs[:T], refs[T:2 * T]
        send_sems, recv_sems = refs[2 * T:]
        x, y, c, chips = _place()
        sends = []
        for t in range(T):
            hn = got[t].shape[1]
            for j, chip in enumerate(chips):
                cp = _remote(_shard_view(pb[t], kinds[t], _chip_id(chip), 0, hn), got[t].at[j],
                             send_sems.at[3 * t + j], recv_sems.at[3 * t + j], (*chip, c))
                cp.start()
                sends.append(cp)
        for cp in sends:
            cp.wait()

    return pl.pallas_call(
        body, name="rs_exchange", in_specs=[ANY] * T, out_specs=[ANY] * T,
        out_shape=[jax.ShapeDtypeStruct((3,) + _shard_shape(p.shape, k), BF16) for p, k in zip(pair_bf16, kinds)],
        scratch_shapes=[pltpu.SemaphoreType.DMA((3 * T,)), pltpu.SemaphoreType.DMA((3 * T,))],
        compiler_params=pltpu.CompilerParams(has_side_effects=True),
    )(*pair_bf16)


def rs_total(pair_f32, got, kind, name):
    _, hn, R, C = got.shape
    tr = _row_tile(R, C)

    def body(p_ref, g_ref, t_ref):
        t_ref[...] = ((p_ref[...] + g_ref[0].astype(F32)) + g_ref[1].astype(F32)) + g_ref[2].astype(F32)

    return pl.pallas_call(
        body, name=name, grid=(hn, R // tr),
        in_specs=[_own_shard_spec(kind, tr, R, C), pl.BlockSpec((3, None, tr, C), lambda a, i: (0, a, i, 0))],
        out_specs=pl.BlockSpec((None, tr, C), lambda a, i: (lax.axis_index("c") * hn + a, i, 0)),
        out_shape=jax.ShapeDtypeStruct((2 * hn, R, C), F32), compiler_params=_params(("parallel", "parallel")),
    )(pair_f32, got)


def rs_share_halves(reduced):
    T = len(reduced)

    def body(*refs):
        outs = refs[T:2 * T]
        send_sems, recv_sems = refs[2 * T:]
        x, y, c, _ = _place()
        copies = []
        for t in range(T):
            hn = outs[t].shape[0] // 2
            mine = outs[t].at[pl.ds(c * hn, hn)]
            cp = _remote(mine, mine, send_sems.at[t], recv_sems.at[t], (x, y, 1 - c))
            cp.start()
            copies.append(cp)
        for t, cp in enumerate(copies):
            hn = outs[t].shape[0] // 2
            theirs = outs[t].at[pl.ds((1 - c) * hn, hn)]
            cp.wait_send()
            _remote(theirs, theirs, send_sems.at[t], recv_sems.at[t], (x, y, 1 - c)).wait_recv()

    return pl.pallas_call(
        body, name="rs_share_halves", in_specs=[ANY] * T, out_specs=[ANY] * T,
        out_shape=[jax.ShapeDtypeStruct(a.shape, a.dtype) for a in reduced],
        input_output_aliases={t: t for t in range(T)},
        scratch_shapes=[pltpu.SemaphoreType.DMA((T,)), pltpu.SemaphoreType.DMA((T,))],
        compiler_params=pltpu.CompilerParams(has_side_effects=True),
    )(*reduced)


def reduce_scatter_grads(grads, kinds):
    recv = rs_swap_halves(grads, kinds)
    pf, pb = [], []
    for t, (g, r, k) in enumerate(zip(grads, recv, kinds)):
        f, b = rs_pair_sum(g, r, k, "rs_pair_sum_%d" % t)
        pf.append(f)
        pb.append(b)
    got = rs_exchange(pb, kinds)
    halves = [rs_total(f, g, k, "rs_total_%d" % t) for t, (f, g, k) in enumerate(zip(pf, got, kinds))]
    return rs_share_halves(halves)


def _half_rows(ref, layout, shard, half):
    if layout == "S":
        hr = ref.shape[1] // 2
        return ref.at[shard, pl.ds(pl.multiple_of(half * hr, 16), hr), :]
    hr, C = ref.shape[0] // 2, ref.shape[1] // 4
    return ref.at[pl.ds(pl.multiple_of(half * hr, 16), hr), pl.ds(pl.multiple_of(shard * C, LANES), C)]


def _half_rows_all(ref, layout, half):
    if layout == "S":
        hr = ref.shape[1] // 2
        return ref.at[:, pl.ds(pl.multiple_of(half * hr, 16), hr), :]
    hr = ref.shape[0] // 2
    return ref.at[pl.ds(pl.multiple_of(half * hr, 16), hr), :]


def _shard_of_half(ref, layout, shard):
    if layout == "S":
        return ref.at[shard]
    C = ref.shape[1] // 4
    return ref.at[:, pl.ds(pl.multiple_of(shard * C, LANES), C)]


def _own_block_spec(layout, tr, C):
    if layout == "S":
        return pl.BlockSpec((None, tr, C), lambda i: (_my_chip(), i, 0))
    return pl.BlockSpec((tr, C), lambda i: (i, _my_chip()))


def cast_into_gathered(shards, layer, layout, name):
    _, R, C = shards.shape
    tr = _row_tile(R, C)

    def body(a_ref, o_ref):
        o_ref[...] = a_ref[...].astype(BF16)

    return pl.pallas_call(
        body, name=name, grid=(R // tr,),
        in_specs=[pl.BlockSpec((None, tr, C), lambda i: (layer, i, 0))],
        out_specs=_own_block_spec(layout, tr, C),
        out_shape=jax.ShapeDtypeStruct((4, R, C) if layout == "S" else (R, 4 * C), BF16),
        compiler_params=_params(("parallel",)),
    )(shards)


def gather_comm(bufs, layouts):
    T = len(bufs)

    def begin(_, outs, send_sems, recv_sems):
        x, y, c, chips = _place()
        for t in range(T):
            mine = _half_rows(outs[t], layouts[t], _chip_id((x, y)), c)
            for j, chip in enumerate(chips):
                _remote(mine, mine, send_sems.at[6 * t + j], recv_sems.at[6 * t + j], (*chip, c)).start()

    def middle(_, outs, send_sems, recv_sems):
        x, y, c, chips = _place()
        for t in range(T):
            for j, chip in enumerate(chips):
                piece = _half_rows(outs[t], layouts[t], _chip_id(chip), c)
                _remote(piece, piece, send_sems.at[6 * t + j], recv_sems.at[6 * t + j], (*chip, c)).wait_recv()
                _remote(piece, piece, send_sems.at[6 * t + 3 + j], recv_sems.at[6 * t + 3 + j], (x, y, 1 - c)).start()

    def finish(_, outs, send_sems, recv_sems):
        x, y, c, chips = _place()
        for t in range(T):
            mine = _half_rows(outs[t], layouts[t], _chip_id((x, y)), c)
            for j, chip in enumerate(chips):
                theirs = _half_rows(outs[t], layouts[t], _chip_id(chip), 1 - c)
                _remote(theirs, theirs, send_sems.at[6 * t + 3 + j], recv_sems.at[6 * t + 3 + j],
                        (x, y, 1 - c)).wait_recv()
                _remote(mine, mine, send_sems.at[6 * t + j], recv_sems.at[6 * t + j], (*chip, c)).wait_send()
                piece = _half_rows(outs[t], layouts[t], _chip_id(chip), c)
                _remote(piece, piece, send_sems.at[6 * t + 3 + j], recv_sems.at[6 * t + 3 + j],
                        (x, y, 1 - c)).wait_send()

    return Comm(bufs, [jax.ShapeDtypeStruct(b.shape, b.dtype) for b in bufs], {t: t for t in range(T)}, 6 * T,
                begin, middle, finish, middle_frac=0.75)


def swap_comm(grads, layouts):
    T = len(grads)

    def out_shape(g, layout):
        return (4, g.shape[1] // 2, g.shape[2]) if layout == "S" else (g.shape[0] // 2, g.shape[1])

    def copies(ins, outs, send_sems, recv_sems):
        x, y, c, _ = _place()
        return [_remote(_half_rows_all(ins[t], layouts[t], 1 - c), outs[t], send_sems.at[t], recv_sems.at[t],
                        (x, y, 1 - c)) for t in range(T)]

    def begin(*refs):
        for cp in copies(*refs):
            cp.start()

    def finish(*refs):
        for cp in copies(*refs):
            cp.wait()

    return Comm(grads, [jax.ShapeDtypeStruct(out_shape(g, k), F32) for g, k in zip(grads, layouts)], {}, T,
                begin, None, finish)


def pair_sum(grad, recv, layout, name):
    if layout == "S":
        _, hr, C = recv.shape
        tr = _row_tile(hr, C)
        nb = hr // tr
        grid = (4, nb)
        g_spec = pl.BlockSpec((None, tr, C), lambda a, i: (a, lax.axis_index("c") * nb + i, 0))
        spec = pl.BlockSpec((None, tr, C), lambda a, i: (a, i, 0))
    else:
        hr, C = recv.shape
        tr = _row_tile(hr, C)
        nb = hr // tr
        grid = (nb,)
        g_spec = pl.BlockSpec((tr, C), lambda i: (lax.axis_index("c") * nb + i, 0))
        spec = pl.BlockSpec((tr, C), lambda i: (i, 0))

    def body(g_ref, r_ref, f_ref, b_ref):
        s = g_ref[...] + r_ref[...]
        f_ref[...] = s
        b_ref[...] = s.astype(BF16)

    return pl.pallas_call(
        body, name=name, grid=grid, in_specs=[g_spec, spec], out_specs=[spec, spec],
        out_shape=[jax.ShapeDtypeStruct(recv.shape, F32), jax.ShapeDtypeStruct(recv.shape, BF16)],
        compiler_params=_params(("parallel",) * len(grid)),
    )(grad, recv)


def exchange_comm(pair_bf16, layouts):
    T = len(pair_bf16)

    def out_shape(p, layout):
        return (3,) + ((p.shape[1], p.shape[2]) if layout == "S" else (p.shape[0], p.shape[1] // 4))

    def copies(ins, outs, send_sems, recv_sems):
        x, y, c, chips = _place()
        return [_remote(_shard_of_half(ins[t], layouts[t], _chip_id(chip)), outs[t].at[j],
                        send_sems.at[3 * t + j], recv_sems.at[3 * t + j], (*chip, c))
                for t in range(T) for j, chip in enumerate(chips)]

    def begin(*refs):
        for cp in copies(*refs):
            cp.start()

    def finish(*refs):
        for cp in copies(*refs):
            cp.wait()

    return Comm(pair_bf16, [jax.ShapeDtypeStruct(out_shape(p, k), BF16) for p, k in zip(pair_bf16, layouts)], {},
                3 * T, begin, None, finish)


def reduce_total(pair_f32, got, layout, layer, n_layers, previous, name):
    _, hr, C = got.shape
    tr = _row_tile(hr, C)
    nb = hr // tr

    def body(*refs):
        p_ref, g_ref, t_ref = refs[0], refs[1], refs[-1]
        t_ref[...] = ((p_ref[...] + g_ref[0].astype(F32)) + g_ref[1].astype(F32)) + g_ref[2].astype(F32)

    args = [pair_f32, got] + ([previous] if previous is not None else [])
    return pl.pallas_call(
        body, name=name, grid=(nb,),
        in_specs=[_own_block_spec(layout, tr, C), pl.BlockSpec((3, tr, C), lambda i: (0, i, 0))]
        + ([ANY] if previous is not None else []),
        out_specs=pl.BlockSpec((None, tr, C), lambda i: (layer, lax.axis_index("c") * nb + i, 0)),
        out_shape=jax.ShapeDtypeStruct((n_layers, 2 * hr, C), F32),
        input_output_aliases={2: 0} if previous is not None else {},
        compiler_params=_params(("parallel",)),
    )(*args)


def share_comm(reduced):
    T = len(reduced)

    def halves(outs, half):
        return [o.at[:, pl.ds(pl.multiple_of(half * (o.shape[1] // 2), 8), o.shape[1] // 2), :] for o in outs]

    def begin(_, outs, send_sems, recv_sems):
        x, y, c, _p = _place()
        for t, mine in enumerate(halves(outs, c)):
            _remote(mine, mine, send_sems.at[t], recv_sems.at[t], (x, y, 1 - c)).start()

    def finish(_, outs, send_sems, recv_sems):
        x, y, c, _p = _place()
        for t, (mine, theirs) in enumerate(zip(halves(outs, c), halves(outs, 1 - c))):
            _remote(mine, mine, send_sems.at[t], recv_sems.at[t], (x, y, 1 - c)).wait_send()
            _remote(theirs, theirs, send_sems.at[t], recv_sems.at[t], (x, y, 1 - c)).wait_recv()

    return Comm(reduced, [jax.ShapeDtypeStruct(r.shape, r.dtype) for r in reduced], {t: t for t in range(T)}, T,
                begin, None, finish)


def _round_up(n, m):
    return (n + m - 1) // m * m


def _heads_col(a16):
    L = a16.shape[0]
    return jnp.transpose(a16.reshape(L, HEADS // 2, 2), (1, 0, 2))


def _heads_row(a16):
    L = a16.shape[0]
    return jnp.transpose(a16.reshape(L, HEADS // 2, 2), (1, 2, 0))


def local_step(x, target, meta, norm_g, b_f, conv_w, conv_b, s5, s5_d, stager):
    S, D = x.shape
    depth = norm_g.shape[0]
    n_even, n_odd = b_f.shape[0], s5_d.shape[0]
    L = N_META + S
    Lp = _round_up(L, ROW_TILE)
    h = jnp.concatenate([meta, x, jnp.zeros((Lp - L, D), F32)], axis=0)
    conv_c = conv_w.shape[2]
    fg_block = 3 * conv_c // LANES
    saved = []

    for i in range(depth):
        g = norm_g[i]
        j = i // 2
        tag = "l%d_" % i
        w = stager.weights(i)
        st = {"h0": h, "w": w}
        if i % 2 == 0:
            u = rmsnorm_fwd(h, g[0:1], BF16, tag + "norm0")
            qkv = matmul([(u, w["w_qkv"])], "nn", BF16, tag + "qkv")
            rest = matmul([(u, w["w_rest"])], "nn", F32, tag + "rest")
            cgate = gate_fwd(rest, fg_block, b_f[j], tag + "gate")
            c16 = cgate[:, :HEADS]
            (attn, lse), arrived = attn_fwd(qkv, _heads_col(c16), _heads_row(c16), tag + "attn",
                                            comm=stager.gather_comm(i))
            stager.gathered(i, arrived)
            convo = conv_fwd(rest, 0, conv_w[j], conv_b[j], tag + "conv")
            cat = jnp.concatenate([attn, convo], axis=1)
            m = matmul([(cat, w["w_o"])], "nn", F32, tag + "wo")
            st.update(u=u, qkv=qkv, rest=rest, c16=c16, lse=lse, cat=cat)
        else:
            p = s5[j]
            u = rmsnorm_fwd(h, g[0:1], F32, tag + "norm0")
            (y, gact, xs), arrived = s5_fwd(u, p["bmat"], p["cmat"], p["tab"], s5_d[j], tag + "s5",
                                            comm=stager.gather_comm(i))
            stager.gathered(i, arrived)
            o1, o2, m = dual_matmul_act(gact, w["w_glu1"], w["w_glu2"], "glu", F32, tag + "glu")
            st.update(u=u, y=y, gact=gact, xs=xs, o1=o1, o2=o2)
        h1 = rmsnorm_fwd(m, g[1:2], F32, tag + "norm1", residual=h)
        u2 = rmsnorm_fwd(h1, g[2:3], BF16, tag + "norm2")
        a, b, hact = dual_matmul_act(u2, w["w_gate"], w["w_up"], "swiglu", BF16, tag + "ffn_in")
        f = matmul([(hact, w["w_down"])], "nn", F32, tag + "ffn_out")
        h = rmsnorm_fwd(f, g[3:4], F32, tag + "norm3", residual=h1)
        st.update(m=m, h1=h1, u2=u2, a=a, b=b, hact=hact, f=f)
        saved.append(st)

    dy, loss = loss_and_grad(h[N_META:L], target, "loss")
    dh = jnp.concatenate([jnp.zeros((N_META, D), F32), dy, jnp.zeros((Lp - L, D), F32)], axis=0)

    grads = {k: [None] * n_even for k in ("b_f", "conv_w", "conv_b")}
    grads.update({k: [None] * n_odd for k in ("s5_d", "s5_dB", "s5_dC", "s5_dlam")})
    grads["norm_g"] = [None] * depth

    for i in reversed(range(depth)):
        g = norm_g[i]
        j = i // 2
        tag = "l%d_b_" % i
        st = saved[i]
        w = st["w"]
        wg = {}
        df, dg3 = rmsnorm_bwd(st["f"], g[3:4], dh, BF16, tag + "norm3")
        wg["w_down"] = matmul([(st["hact"], df)], "tn", F32, tag + "dw_down")
        (da, db), swapped = ffn_bwd_act(df, w["w_down"], st["a"], st["b"], tag + "ffn_act", comm=stager.swap_comm(i))
        stager.swapped(i, swapped)
        wg["w_gate"] = matmul([(st["u2"], da)], "tn", F32, tag + "dw_gate")
        wg["w_up"] = matmul([(st["u2"], db)], "tn", F32, tag + "dw_up")
        du2 = matmul([(da, w["w_gate"]), (db, w["w_up"])], "nt", F32, tag + "du2")
        dh1, dg2 = rmsnorm_bwd(st["h1"], g[2:3], du2, F32, tag + "norm2", add=dh)
        if i % 2 == 0:
            dm, dg1 = rmsnorm_bwd(st["m"], g[1:2], dh1, BF16, tag + "norm1")
            wg["w_o"] = matmul([(st["cat"], dm)], "tn", F32, tag + "dw_o")
            dcat = matmul([(dm, w["w_o"])], "nt", F32, tag + "dcat")
            delta = attn_delta(dcat, st["cat"], tag + "delta")
            c16 = st["c16"]
            lse16 = jnp.transpose(st["lse"], (1, 0, 2)).reshape(Lp, HEADS)
            (dq, dk, dv, dcq, dck), exchanged = attn_bwd(
                st["qkv"], dcat, _heads_row(lse16), _heads_row(delta[:, :HEADS]), _heads_row(c16), _heads_col(c16),
                tag + "attn", comm=stager.exchange_comm(i))
            stager.exchanged(i, exchanged)
            dc16 = (jnp.transpose(dcq, (2, 0, 1)).reshape(Lp, HEADS)
                    + jnp.transpose(dck, (1, 0, 2)).reshape(Lp, HEADS))
            dc = jnp.pad(dc16, ((0, 0), (0, LANES - HEADS)))
            dfg, dbf = gate_bwd(st["rest"], fg_block, b_f[j], dc, tag + "gate")
            dgb, dgc, dxc, dcw, dcb = conv_bwd(st["rest"], 0, conv_w[j], conv_b[j], dcat, ATTN_W // LANES,
                                               tag + "conv")
            dqkv = jnp.concatenate([dq, dk, dv], axis=1)
            drest = jnp.concatenate([dgb, dgc, dxc, dfg.astype(BF16)], axis=1)
            wg["w_qkv"] = matmul([(st["u"], dqkv)], "tn", F32, tag + "dw_qkv")
            wg["w_rest"] = matmul([(st["u"], drest)], "tn", F32, tag + "dw_rest")
            du = matmul([(dqkv, w["w_qkv"])], "nt", F32, tag + "du_qkv")
            du_b = matmul([(drest, w["w_rest"])], "nt", F32, tag + "du_rest")
            grads["b_f"][j], grads["conv_w"][j], grads["conv_b"][j] = dbf, dcw, dcb
        else:
            p = s5[j]
            dmix, dg1 = rmsnorm_bwd(st["m"], g[1:2], dh1, F32, tag + "norm1")
            do1, do2 = glu_bwd_act(dmix, st["o1"], st["o2"], tag + "glu_act")
            wg["w_glu1"] = matmul([(st["gact"], do1)], "tn", F32, tag + "dw_glu1")
            wg["w_glu2"] = matmul([(st["gact"], do2)], "tn", F32, tag + "dw_glu2")
            dgact = matmul([(do1, w["w_glu1"]), (do2, w["w_glu2"])], "nt", F32, tag + "dgact")
            (du, dC, dB, dlam, dd), exchanged = s5_bwd(dgact, st["y"], st["u"], st["xs"], p["cmat_t"], p["bmat_t"],
                                                       p["rtab"], s5_d[j], tag + "s5", comm=stager.exchange_comm(i))
            stager.exchanged(i, exchanged)
            du_b = None
            grads["s5_dB"][j], grads["s5_dC"][j], grads["s5_dlam"][j], grads["s5_d"][j] = dB, dC, dlam, dd
        dh, dg0 = rmsnorm_bwd(st["h0"], g[0:1], du, F32, tag + "norm0", add=dh1, dy2=du_b)
        grads["norm_g"][i] = jnp.concatenate([dg0, dg1, dg2, dg3], axis=0)
        stager.put_grads(i, wg)

    grads["meta"] = dh[:N_META]
    return loss, dh[N_META:L], grads


def _pack(arrs):
    flat = jnp.concatenate([a.reshape(-1).astype(F32) for a in arrs])
    n = flat.shape[0]
    rows = _round_up(_round_up(n, LANES) // LANES, SUBLANES)
    return jnp.pad(flat, (0, rows * LANES - n)).reshape(rows, LANES)


def _unpack(buf, shapes):
    flat = buf.reshape(-1)
    out, off = [], 0
    for s in shapes:
        size = math.prod(s)
        out.append(flat[off:off + size].reshape(s))
        off += size
    return out


class MeshStager:
    LAYOUT = {"ab_w_in": "S", "ab_w_o": "S", "s5_w_glu1": "S", "s5_w_glu2": "S",
              "ffn_w_gate": "C", "ffn_w_up": "C", "ffn_w_down": "S"}
    EVEN = ("ab_w_in", "ab_w_o", "ffn_w_gate", "ffn_w_up", "ffn_w_down")
    ODD = ("s5_w_glu1", "s5_w_glu2", "ffn_w_gate", "ffn_w_up", "ffn_w_down")

    def __init__(self, shards):
        self.shards = shards
        self.depth = shards["ffn_w_gate"].shape[0]
        self.bufs = {}
        for i in range(self.depth):
            for k in self.keys(i):
                self.bufs[k, i] = cast_into_gathered(shards[k], self.index(k, i), self.LAYOUT[k],
                                                     "cast_%s_%d" % (k, i))
        self.grads, self.pairs, self.reduced = {}, {}, {}
        self.gathered(-1, comm_call("gather_0", self.gather_comm(-1)))

    def keys(self, i):
        return self.EVEN if i % 2 == 0 else self.ODD

    @staticmethod
    def index(key, i):
        return i if key.startswith("ffn") else i // 2

    def layouts(self, i):
        return [self.LAYOUT[k] for k in self.keys(i)]

    def gather_comm(self, i):
        if i + 1 >= self.depth:
            return None
        return gather_comm([self.bufs[k, i + 1] for k in self.keys(i + 1)], self.layouts(i + 1))

    def gathered(self, i, bufs):
        for k, b in zip(self.keys(i + 1), bufs):
            self.bufs[k, i + 1] = b

    def weights(self, i):
        out = {}
        for k in self.keys(i):
            b = self.bufs[k, i]
            if k == "ab_w_in":
                w_in = jnp.transpose(b, (1, 0, 2)).reshape(b.shape[1], 4 * b.shape[2])
                out["w_qkv"], out["w_rest"] = split_w_in(w_in)
            else:
                name = {"ab_w_o": "w_o", "s5_w_glu1": "w_glu1", "s5_w_glu2": "w_glu2"}.get(k, k[4:])
                out[name] = b.reshape(4 * b.shape[1], b.shape[2]) if self.LAYOUT[k] == "S" else b
        return out

    def put_grads(self, i, wg):
        g = {}
        for k in self.keys(i):
            _, R, C = self.shards[k].shape
            if k == "ab_w_in":
                dw = merge_dw_in(wg["w_qkv"], wg["w_rest"])
                g[k] = jnp.transpose(dw.reshape(R, 4, C), (1, 0, 2))
            else:
                name = {"ab_w_o": "w_o", "s5_w_glu1": "w_glu1", "s5_w_glu2": "w_glu2"}.get(k, k[4:])
                g[k] = wg[name].reshape(4, R, C) if self.LAYOUT[k] == "S" else wg[name]
        self.grads[i] = g

    def swap_comm(self, i):
        if i + 1 not in self.grads:
            return None
        return swap_comm([self.grads[i + 1][k] for k in self.keys(i + 1)], self.layouts(i + 1))

    def swapped(self, i, received):
        if not received:
            return
        s = i + 1
        self.pairs[s] = [pair_sum(self.grads[s][k], r, self.LAYOUT[k], "pair_sum_%s_%d" % (k, s))
                         for k, r in zip(self.keys(s), received)]

    def exchange_comm(self, i):
        if i + 1 not in self.pairs:
            return None
        return exchange_comm([b for _, b in self.pairs[i + 1]], self.layouts(i + 1))

    def exchanged(self, i, got):
        if not got:
            return
        s = i + 1
        for k, (f, _), g in zip(self.keys(s), self.pairs[s], got):
            self.reduced[k] = reduce_total(f, g, self.LAYOUT[k], self.index(k, s), self.shards[k].shape[0],
                                           self.reduced.get(k), "reduce_total_%s_%d" % (k, s))

    def finish(self):
        self.swapped(-1, comm_call("swap_0", self.swap_comm(-1)))
        self.exchanged(-1, comm_call("exchange_0", self.exchange_comm(-1)))
        names = list(self.LAYOUT)
        return dict(zip(names, comm_call("share_reduced", share_comm([self.reduced[k] for k in names]))))


def split_w_in(w_in):
    fg0 = 3 * ATTN_W
    w_rest = jnp.concatenate([w_in[:, fg0 + HEADS:], w_in[:, fg0:fg0 + HEADS],
                              jnp.zeros((w_in.shape[0], LANES - HEADS), w_in.dtype)], axis=1)
    return w_in[:, :fg0], w_rest


def merge_dw_in(dw_qkv, dw_rest):
    nqc = dw_rest.shape[1] - LANES
    return jnp.concatenate([dw_qkv, dw_rest[:, nqc:nqc + HEADS], dw_rest[:, :nqc]], axis=1)


def device_step(x, target, P, stager):
    D = x.shape[-1]
    n_even, n_odd = P["ab_b_f"].shape[0], P["s5_d"].shape[0]
    conv_c = P["ab_conv_b"].shape[1]
    b_f_pad = jnp.pad(P["ab_b_f"], ((0, 0), (0, LANES - HEADS))).reshape(n_even, 1, LANES)

    s5, s5_vjps = [], []
    for j in range(n_odd):
        disc, vjp = jax.vjp(_s5_discretize, P["s5_a_re"][j], P["s5_a_im"][j], P["s5_log_step"][j],
                            P["s5_b_re"][j], P["s5_b_im"][j])
        lb_re, lb_im, bb_re, bb_im = disc
        tab, rtab = _s5_tables(lb_re, lb_im)
        bmat, cmat = _s5_block_mats(bb_re, bb_im, P["s5_c_re"][j], P["s5_c_im"][j])
        s5.append(dict(tab=tab, rtab=rtab, bmat=bmat.astype(BF16), cmat=cmat.astype(BF16),
                       bmat_t=jnp.transpose(bmat, (0, 2, 1)).astype(BF16),
                       cmat_t=jnp.transpose(cmat, (0, 2, 1)).astype(BF16)))
        s5_vjps.append(vjp)

    loss, grad_x, G = local_step(
        x, target, P["meta_tokens"], P["norm_g"], b_f_pad, P["ab_conv_w"],
        P["ab_conv_b"].reshape(n_even, 1, conv_c), s5, P["s5_d"].reshape(n_odd, 1, D), stager)

    out = {
        "meta_tokens": G["meta"],
        "norm_g": jnp.stack(G["norm_g"]),
        "ab_b_f": jnp.stack([b[0, :HEADS] for b in G["b_f"]]),
        "ab_conv_w": jnp.stack(G["conv_w"]),
        "ab_conv_b": jnp.stack([b[0] for b in G["conv_b"]]),
        "s5_d": jnp.stack([d[0] for d in G["s5_d"]]),
    }
    s5g = {k: [] for k in ("s5_a_re", "s5_a_im", "s5_log_step", "s5_b_re", "s5_b_im", "s5_c_re", "s5_c_im")}
    for j in range(n_odd):
        dbb_re, dbb_im, dc_re, dc_im, dl_re, dl_im = _s5_unblock(G["s5_dB"][j], G["s5_dC"][j], G["s5_dlam"][j])
        da_re, da_im, dls, db_re, db_im = s5_vjps[j]((dl_re, dl_im, dbb_re, dbb_im))
        for k, val in zip(s5g, (da_re, da_im, dls, db_re, db_im, dc_re, dc_im)):
            s5g[k].append(val)
    out.update({k: jnp.stack(v) for k, v in s5g.items()})
    return loss, grad_x, out


def kernel(x, meta_tokens, norm_g, ab_w_in, ab_b_f, ab_conv_w, ab_conv_b, ab_w_o, s5_a_re, s5_a_im, s5_log_step, s5_b_re, s5_b_im, s5_c_re, s5_c_im, s5_d, s5_w_glu1, s5_w_glu2, ffn_w_gate, ffn_w_up, ffn_w_down, loss_target, m_meta_tokens, m_norm_g, m_ab_w_in, m_ab_b_f, m_ab_conv_w, m_ab_conv_b, m_ab_w_o, m_s5_a_re, m_s5_a_im, m_s5_log_step, m_s5_b_re, m_s5_b_im, m_s5_c_re, m_s5_c_im, m_s5_d, m_s5_w_glu1, m_s5_w_glu2, m_ffn_w_gate, m_ffn_w_up, m_ffn_w_down, v_meta_tokens, v_norm_g, v_ab_w_in, v_ab_b_f, v_ab_conv_w, v_ab_conv_b, v_ab_w_o, v_s5_a_re, v_s5_a_im, v_s5_log_step, v_s5_b_re, v_s5_b_im, v_s5_c_re, v_s5_c_im, v_s5_d, v_s5_w_glu1, v_s5_w_glu2, v_ffn_w_gate, v_ffn_w_up, v_ffn_w_down):
    names = ["meta_tokens", "norm_g", "ab_w_in", "ab_b_f", "ab_conv_w", "ab_conv_b", "ab_w_o", "s5_a_re", "s5_a_im",
             "s5_log_step", "s5_b_re", "s5_b_im", "s5_c_re", "s5_c_im", "s5_d", "s5_w_glu1", "s5_w_glu2",
             "ffn_w_gate", "ffn_w_up", "ffn_w_down"]
    W = dict(zip(names, [meta_tokens, norm_g, ab_w_in, ab_b_f, ab_conv_w, ab_conv_b, ab_w_o, s5_a_re, s5_a_im,
                         s5_log_step, s5_b_re, s5_b_im, s5_c_re, s5_c_im, s5_d, s5_w_glu1, s5_w_glu2,
                         ffn_w_gate, ffn_w_up, ffn_w_down]))
    Mo = dict(zip(names, [m_meta_tokens, m_norm_g, m_ab_w_in, m_ab_b_f, m_ab_conv_w, m_ab_conv_b, m_ab_w_o, m_s5_a_re,
                          m_s5_a_im, m_s5_log_step, m_s5_b_re, m_s5_b_im, m_s5_c_re, m_s5_c_im, m_s5_d, m_s5_w_glu1,
                          m_s5_w_glu2, m_ffn_w_gate, m_ffn_w_up, m_ffn_w_down]))
    Vo = dict(zip(names, [v_meta_tokens, v_norm_g, v_ab_w_in, v_ab_b_f, v_ab_conv_w, v_ab_conv_b, v_ab_w_o, v_s5_a_re,
                          v_s5_a_im, v_s5_log_step, v_s5_b_re, v_s5_b_im, v_s5_c_re, v_s5_c_im, v_s5_d, v_s5_w_glu1,
                          v_s5_w_glu2, v_ffn_w_gate, v_ffn_w_up, v_ffn_w_down]))
    D = x.shape[-1]
    n_even, n_odd, depth = ab_w_in.shape[0], s5_w_glu1.shape[0], ffn_w_gate.shape[0]
    chip = 2 * lax.axis_index("x") + lax.axis_index("y")

    big = list(MeshStager.LAYOUT)
    stager = MeshStager({k: W[k] for k in big})
    g_meta, g_norm, g_convw, g_s5d = allgather_small([meta_tokens, norm_g, ab_conv_w, s5_d])
    full = {k: W[k] for k in names if k not in big}
    full["meta_tokens"] = jnp.transpose(g_meta, (1, 0, 2)).reshape(N_META, D)
    full["norm_g"] = jnp.transpose(g_norm, (1, 2, 0, 3)).reshape(depth, 4, D)
    full["ab_conv_w"] = jnp.transpose(g_convw, (1, 2, 0, 3)).reshape(n_even, CONV_K, -1)
    full["s5_d"] = jnp.transpose(g_s5d, (1, 0, 2)).reshape(n_odd, D)

    loss, grad_x, G = device_step(x[0], loss_target[0], full, stager)
    reduced = stager.finish()

    small_w = [k for k in names if k not in big]
    small_names = ["loss"] + small_w
    G["loss"] = loss
    summed = dict(zip(small_names, _unpack(allreduce_small(_pack([G[k] for k in small_names])),
                                           [G[k].shape for k in small_names])))
    loss_out = summed["loss"].reshape(())
    for k in ("meta_tokens", "norm_g", "ab_conv_w", "s5_d"):
        n_last = W[k].shape[-1]
        summed[k] = lax.dynamic_slice_in_dim(summed[k], chip * n_last, n_last, axis=summed[k].ndim - 1)
    shapes = [W[k].shape for k in small_w]
    d_s, m_s, v_s = adamw(_pack([W[k] for k in small_w])[None], _pack([summed[k] for k in small_w])[None],
                          _pack([Mo[k] for k in small_w])[None], _pack([Vo[k] for k in small_w])[None], "adamw_small")
    delta = dict(zip(small_w, _unpack(d_s, shapes)))
    new_m = dict(zip(small_w, _unpack(m_s, shapes)))
    new_v = dict(zip(small_w, _unpack(v_s, shapes)))
    grad = {k: summed[k] for k in small_w}
    for k in big:
        grad[k] = reduced[k]
        delta[k], new_m[k], new_v[k] = adamw(W[k], reduced[k], Mo[k], Vo[k], "adamw_" + k)

    return (loss_out, grad_x[None], *[grad[k] for k in names], *[delta[k] for k in names],
            *[new_m[k] for k in names], *[new_v[k] for k in names])
```

```python
import functools
import math

import jax
import jax.numpy as jnp
from jax import lax
from jax.experimental import pallas as pl
from jax.experimental.pallas import tpu as pltpu

F32 = jnp.float32
BF16 = jnp.bfloat16

N_META = 16
HEADS = 16
HEAD_DIM = 64
ATTN_W = HEADS * HEAD_DIM
CONV_K = 3
S5_GROUP = 16
S5_STATE = 64
S5_MIN_DECAY = 1e-4
NORM_EPS = 1e-6
ADAM_LR = 0.001
ADAM_B1 = 0.9
ADAM_B2 = 0.999
ADAM_EPS = 1e-08
ADAM_WD = 0.01
ADAM_STEP = 10

LANES = 128
SUBLANES = 8
VMEM_LIMIT = 56 * 1024 * 1024
VMEM_TILE_BUDGET = 34 * 1024 * 1024
ROW_TILE = 384
ATTN_ROWS = 128
S5_BLOCK_GROUPS = LANES // S5_GROUP
S5_BLOCK_STATES = S5_BLOCK_GROUPS * S5_STATE
NEG_BIG = -1e30

MESH = pl.DeviceIdType.MESH
ANY = pl.BlockSpec(memory_space=pl.ANY)
VMEM_SPEC = pl.BlockSpec(memory_space=pltpu.VMEM)


def _params(sem=None):
    return pltpu.CompilerParams(dimension_semantics=sem, vmem_limit_bytes=VMEM_LIMIT)


def _div_tile(n, prefs):
    for p in prefs:
        if n % p == 0:
            return p
    return n


def _row_tile(rows, cols, itemsize=4, limit=2 * 1024 * 1024):
    for p in (512, 256, 128, 64, 32, 16):
        if rows % p == 0 and p * cols * itemsize <= limit:
            return p
    return 16 if rows % 16 == 0 else rows


def _tile_cands(n):
    c = [d for d in range(LANES, min(n, 2048) + 1, LANES) if n % d == 0]
    if not c or n <= 2048 and n not in c:
        c.append(n)
    return sorted(set(c), reverse=True)


def _mm_tiles(M, N, K, a_bytes, b_bytes, o_bytes, npairs):
    best = None
    for tk in _tile_cands(K):
        for tm in _tile_cands(M):
            for tn in _tile_cands(N):
                mem = npairs * 2 * (tm * tk * a_bytes + tk * tn * b_bytes) + 2 * tm * tn * o_bytes + tm * tn * 4
                mem += npairs * ((tm * tk * 2 if a_bytes == 4 else 0) + (tk * tn * 2 if b_bytes == 4 else 0))
                if mem > VMEM_TILE_BUDGET:
                    continue
                key = (tm * tn * tk, tk, tn)
                if best is None or key > best[0]:
                    best = (key, (tm, tn, tk))
    assert best is not None, (M, N, K)
    return best[1]


class Comm:
    def __init__(self, operands, out_shapes, aliases, n_sems, begin, middle=None, finish=None, middle_frac=0.5):
        self.operands, self.out_shapes, self.aliases, self.n_sems = list(operands), list(out_shapes), aliases, n_sems
        self.begin, self.middle, self.finish, self.middle_frac = begin, middle, finish, middle_frac


def carrier_call(body, name, grid, in_specs, out_specs, out_shape, scratch_shapes, args, comm, semantics):
    n_in, n_out = len(args), len(out_shape)
    if comm is None:
        outs = pl.pallas_call(body, name=name, grid=grid, in_specs=in_specs, out_specs=out_specs, out_shape=out_shape,
                              scratch_shapes=scratch_shapes, compiler_params=_params(semantics))(*args)
        return list(outs), []
    ci, co = len(comm.operands), len(comm.out_shapes)
    total = math.prod(grid)
    middle_at = min(total - 1, max(0, int(total * comm.middle_frac)))

    def carried(*refs):
        ins, cins = refs[:n_in], refs[n_in:n_in + ci]
        outs = refs[n_in + ci:n_in + ci + n_out]
        couts = refs[n_in + ci + n_out:n_in + ci + n_out + co]
        scratch, (send_sems, recv_sems) = refs[n_in + ci + n_out + co:-2], refs[-2:]
        step = 0
        for d, size in enumerate(grid):
            step = step * size + pl.program_id(d)

        @pl.when(step == 0)
        def _():
            comm.begin(cins, couts, send_sems, recv_sems)

        if comm.middle is not None:
            @pl.when(step == middle_at)
            def _():
                comm.middle(cins, couts, send_sems, recv_sems)

        body(*ins, *outs, *scratch)

        @pl.when(step == total - 1)
        def _():
            comm.finish(cins, couts, send_sems, recv_sems)

    outs = pl.pallas_call(
        carried, name=name, grid=grid,
        in_specs=list(in_specs) + [ANY] * ci, out_specs=list(out_specs) + [ANY] * co,
        out_shape=list(out_shape) + comm.out_shapes,
        scratch_shapes=list(scratch_shapes) + [pltpu.SemaphoreType.DMA((comm.n_sems,)),
                                                pltpu.SemaphoreType.DMA((comm.n_sems,))],
        input_output_aliases={n_in + i: n_out + o for i, o in comm.aliases.items()},
        compiler_params=pltpu.CompilerParams(dimension_semantics=("arbitrary",) * len(grid),
                                             vmem_limit_bytes=VMEM_LIMIT, has_side_effects=True),
    )(*args, *comm.operands)
    return list(outs[:n_out]), list(outs[n_out:])


def comm_call(name, comm):
    ci = len(comm.operands)

    def body(*refs):
        cins, couts = refs[:ci], refs[ci:ci + len(comm.out_shapes)]
        send_sems, recv_sems = refs[-2:]
        comm.begin(cins, couts, send_sems, recv_sems)
        if comm.middle is not None:
            comm.middle(cins, couts, send_sems, recv_sems)
        comm.finish(cins, couts, send_sems, recv_sems)

    return pl.pallas_call(
        body, name=name, in_specs=[ANY] * ci, out_specs=[ANY] * len(comm.out_shapes), out_shape=comm.out_shapes,
        input_output_aliases=dict(comm.aliases),
        scratch_shapes=[pltpu.SemaphoreType.DMA((comm.n_sems,)), pltpu.SemaphoreType.DMA((comm.n_sems,))],
        compiler_params=pltpu.CompilerParams(has_side_effects=True),
    )(*comm.operands)


_DIMS ={"nn": (((1,), (0,)), ((), ())), "nt": (((1,), (1,)), ((), ())), "tn": (((0,), (0,)), ((), ()))}


def matmul(pairs, kind, out_dtype, name, comm=None):
    a0, b0 = pairs[0]
    if kind == "nn":
        (M, K), N = a0.shape, b0.shape[1]
    elif kind == "nt":
        (M, K), N = a0.shape, b0.shape[0]
    else:
        (K, M), N = a0.shape, b0.shape[1]
    tm, tn, tk = _mm_tiles(M, N, K, a0.dtype.itemsize, b0.dtype.itemsize, jnp.dtype(out_dtype).itemsize, len(pairs))
    nk = K // tk
    dims = _DIMS[kind]
    npairs = len(pairs)
    n_in = 2 * npairs

    def body(*refs):
        ins, o_ref = refs[:2 * npairs], refs[n_in]
        part = None
        for p in range(npairs):
            d = lax.dot_general(ins[2 * p][...].astype(BF16), ins[2 * p + 1][...].astype(BF16), dims,
                                preferred_element_type=F32)
            part = d if part is None else part + d
        if nk == 1:
            o_ref[...] = part.astype(o_ref.dtype)
        else:
            acc_ref = refs[n_in + 1]
            k = pl.program_id(2)

            @pl.when(k == 0)
            def _():
                acc_ref[...] = part

            @pl.when(k > 0)
            def _():
                acc_ref[...] += part

            @pl.when(k == nk - 1)
            def _():
                o_ref[...] = acc_ref[...].astype(o_ref.dtype)

    if kind == "nn":
        a_blk, a_map = (tm, tk), lambda j, i, k: (i, k)
        b_blk, b_map = (tk, tn), lambda j, i, k: (k, j)
    elif kind == "nt":
        a_blk, a_map = (tm, tk), lambda j, i, k: (i, k)
        b_blk, b_map = (tn, tk), lambda j, i, k: (j, k)
    else:
        a_blk, a_map = (tk, tm), lambda j, i, k: (k, i)
        b_blk, b_map = (tk, tn), lambda j, i, k: (k, j)
    (out,), arrived = carrier_call(
        body, name, (N // tn, M // tm, nk),
        [pl.BlockSpec(a_blk, a_map), pl.BlockSpec(b_blk, b_map)] * npairs,
        [pl.BlockSpec((tm, tn), lambda j, i, k: (i, j))], [jax.ShapeDtypeStruct((M, N), out_dtype)],
        [] if nk == 1 else [pltpu.VMEM((tm, tn), F32)], [t for ab in pairs for t in ab], comm,
        ("parallel", "parallel", "arbitrary"))
    return out if comm is None else ([out], arrived)


def _sigmoid(x):
    return 1.0 / (1.0 + jnp.exp(-x))


def dual_matmul_act(x, w1, w2, act, out_dtype, name, comm=None):
    M, K = x.shape
    N = w1.shape[-1]
    tm = _div_tile(M, (ROW_TILE,))
    tn = _div_tile(N, (1408, 1024, 512, 256, 128))

    def body(x_ref, w1_ref, w2_ref, o1_ref, o2_ref, out_ref):
        xv = x_ref[...]
        o1 = jnp.dot(xv, w1_ref[...], preferred_element_type=F32)
        o2 = jnp.dot(xv, w2_ref[...], preferred_element_type=F32)
        o1_ref[...] = o1.astype(BF16)
        o2_ref[...] = o2.astype(BF16)
        if act == "swiglu":
            out = o1 * _sigmoid(o1) * o2
        else:
            out = o1 * _sigmoid(o2)
        out_ref[...] = out.astype(out_ref.dtype)

    w_spec = pl.BlockSpec((K, tn), lambda j, i: (0, j))
    o_spec = pl.BlockSpec((tm, tn), lambda j, i: (i, j))
    return carrier_call(
        body, name, (N // tn, M // tm), [pl.BlockSpec((tm, K), lambda j, i: (i, 0)), w_spec, w_spec],
        [o_spec, o_spec, o_spec],
        [jax.ShapeDtypeStruct((M, N), BF16), jax.ShapeDtypeStruct((M, N), BF16),
         jax.ShapeDtypeStruct((M, N), out_dtype)], [], (x, w1, w2), comm, ("parallel", "parallel"))


def ffn_bwd_act(df, wd, a, b, name, comm=None):
    M, K = df.shape
    N = wd.shape[0]
    tm = _div_tile(M, (ROW_TILE,))
    tn = _div_tile(N, (1408, 1024, 512, 256, 128))

    def body(df_ref, wd_ref, a_ref, b_ref, da_ref, db_ref):
        dh = lax.dot_general(df_ref[...], wd_ref[...], _DIMS["nt"], preferred_element_type=F32)
        av = a_ref[...].astype(F32)
        bv = b_ref[...].astype(F32)
        sig = _sigmoid(av)
        silu = av * sig
        da_ref[...] = (dh * bv * (sig + silu * (1.0 - sig))).astype(BF16)
        db_ref[...] = (dh * silu).astype(BF16)

    t_spec = pl.BlockSpec((tm, tn), lambda j, i: (i, j))
    return carrier_call(
        body, name, (N // tn, M // tm),
        [pl.BlockSpec((tm, K), lambda j, i: (i, 0)), pl.BlockSpec((tn, K), lambda j, i: (j, 0)), t_spec, t_spec],
        [t_spec, t_spec], [jax.ShapeDtypeStruct((M, N), BF16)] * 2, [], (df, wd, a, b), comm,
        ("parallel", "parallel"))


def glu_bwd_act(dout, o1, o2, name):
    M, N = dout.shape
    tm = _div_tile(M, (ROW_TILE,))

    def body(d_ref, o1_ref, o2_ref, d1_ref, d2_ref):
        d = d_ref[...].astype(F32)
        sig = _sigmoid(o2_ref[...].astype(F32))
        d1_ref[...] = (d * sig).astype(BF16)
        d2_ref[...] = (d * o1_ref[...].astype(F32) * sig * (1.0 - sig)).astype(BF16)

    spec = pl.BlockSpec((tm, N), lambda i: (i, 0))
    return pl.pallas_call(
        body, name=name, grid=(M // tm,), in_specs=[spec] * 3, out_specs=[spec] * 2,
        out_shape=[jax.ShapeDtypeStruct((M, N), BF16)] * 2,
        compiler_params=_params(("parallel",)),
    )(dout, o1, o2)


def rmsnorm_fwd(x, g, out_dtype, name, residual=None):
    L, D = x.shape
    tr = _div_tile(L, (ROW_TILE,))
    has_res = residual is not None

    def body(*refs):
        x_ref, g_ref = refs[0], refs[1]
        o_ref = refs[-1]
        xv = x_ref[...]
        r = lax.rsqrt(jnp.mean(xv * xv, axis=-1, keepdims=True) + NORM_EPS)
        y = xv * r * g_ref[...]
        if has_res:
            y = refs[2][...] + y
        o_ref[...] = y.astype(o_ref.dtype)

    row = pl.BlockSpec((tr, D), lambda i: (i, 0))
    gsp = pl.BlockSpec((1, D), lambda i: (0, 0))
    args = (x, g) + ((residual,) if has_res else ())
    return pl.pallas_call(
        body, name=name, grid=(L // tr,), in_specs=[row, gsp] + ([row] if has_res else []), out_specs=row,
        out_shape=jax.ShapeDtypeStruct((L, D), out_dtype), compiler_params=_params(("parallel",)),
    )(*args)


def rmsnorm_bwd(x, g, dy, out_dtype, name, add=None, dy2=None):
    L, D = x.shape
    tr = _div_tile(L, (ROW_TILE,))
    has_add = add is not None
    has_dy2 = dy2 is not None

    def body(*refs):
        x_ref, g_ref, dy_ref = refs[0], refs[1], refs[2]
        dx_ref, dg_ref = refs[-2], refs[-1]
        xv = x_ref[...]
        dyv = dy_ref[...].astype(F32)
        if has_dy2:
            dyv = dyv + refs[3][...].astype(F32)
        r = lax.rsqrt(jnp.mean(xv * xv, axis=-1, keepdims=True) + NORM_EPS)
        t = dyv * g_ref[...]
        dx = r * t - xv * (r * r * r) * jnp.mean(xv * t, axis=-1, keepdims=True)
        if has_add:
            dx = refs[3 + has_dy2][...] + dx
        dx_ref[...] = dx.astype(dx_ref.dtype)
        dgp = jnp.sum(dyv * xv * r, axis=0, keepdims=True)

        @pl.when(pl.program_id(0) == 0)
        def _():
            dg_ref[...] = dgp

        @pl.when(pl.program_id(0) > 0)
        def _():
            dg_ref[...] += dgp

    row = pl.BlockSpec((tr, D), lambda i: (i, 0))
    gsp = pl.BlockSpec((1, D), lambda i: (0, 0))
    args = (x, g, dy) + ((dy2,) if has_dy2 else ()) + ((add,) if has_add else ())
    return pl.pallas_call(
        body, name=name, grid=(L // tr,), in_specs=[row, gsp] + [row] * (len(args) - 2),
        out_specs=[row, gsp],
        out_shape=[jax.ShapeDtypeStruct((L, D), out_dtype), jax.ShapeDtypeStruct((1, D), F32)],
        compiler_params=_params(("arbitrary",)),
    )(*args)


def _gate_z(fg_ref, b_ref):
    return fg_ref[...] + b_ref[...]


def gate_fwd(fg_src, col_block, b, name):
    L = fg_src.shape[0]
    T = _div_tile(L, (ROW_TILE,))

    def body(fg_ref, b_ref, c_ref, carry):
        @pl.when(pl.program_id(0) == 0)
        def _():
            carry[...] = jnp.zeros_like(carry)

        z = _gate_z(fg_ref, b_ref)
        logf = jnp.minimum(z, 0.0) - jnp.log(1.0 + jnp.exp(-jnp.abs(z)))
        tri = (lax.broadcasted_iota(jnp.int32, (T, T), 1) <= lax.broadcasted_iota(jnp.int32, (T, T), 0)).astype(F32)
        c = jnp.dot(tri, logf, precision=lax.Precision.HIGHEST, preferred_element_type=F32) + carry[...]
        c_ref[...] = c
        carry[...] = c[T - 1:T, :]

    return pl.pallas_call(
        body, name=name, grid=(L // T,),
        in_specs=[pl.BlockSpec((T, LANES), lambda i: (i, col_block)), pl.BlockSpec((1, LANES), lambda i: (0, 0))],
        out_specs=pl.BlockSpec((T, LANES), lambda i: (i, 0)),
        out_shape=jax.ShapeDtypeStruct((L, LANES), F32),
        scratch_shapes=[pltpu.VMEM((1, LANES), F32)],
        compiler_params=_params(("arbitrary",)),
    )(fg_src, b)


def gate_bwd(fg_src, col_block, b, dc, name):
    L = fg_src.shape[0]
    T = _div_tile(L, (ROW_TILE,))
    nb = L // T

    def body(fg_ref, b_ref, dc_ref, dfg_ref, db_ref, carry):
        @pl.when(pl.program_id(0) == 0)
        def _():
            carry[...] = jnp.zeros_like(carry)
            db_ref[...] = jnp.zeros_like(db_ref)

        z = _gate_z(fg_ref, b_ref)
        dcv = dc_ref[...]
        tri = (lax.broadcasted_iota(jnp.int32, (T, T), 1) >= lax.broadcasted_iota(jnp.int32, (T, T), 0)).astype(F32)
        dlogf = jnp.dot(tri, dcv, precision=lax.Precision.HIGHEST, preferred_element_type=F32) + carry[...]
        dfg = dlogf * _sigmoid(-z)
        dfg_ref[...] = dfg
        db_ref[...] += jnp.sum(dfg, axis=0, keepdims=True)
        carry[...] = dlogf[0:1, :]

    return pl.pallas_call(
        body, name=name, grid=(nb,),
        in_specs=[pl.BlockSpec((T, LANES), lambda i: (nb - 1 - i, col_block)),
                  pl.BlockSpec((1, LANES), lambda i: (0, 0)),
                  pl.BlockSpec((T, LANES), lambda i: (nb - 1 - i, 0))],
        out_specs=[pl.BlockSpec((T, LANES), lambda i: (nb - 1 - i, 0)), pl.BlockSpec((1, LANES), lambda i: (0, 0))],
        out_shape=[jax.ShapeDtypeStruct((L, LANES), F32), jax.ShapeDtypeStruct((1, LANES), F32)],
        scratch_shapes=[pltpu.VMEM((1, LANES), F32)],
        compiler_params=_params(("arbitrary",)),
    )(fg_src, b, dc)


def attn_fwd(proj, cq_col, ck_row, name, comm=None):
    L = proj.shape[0]
    T = _div_tile(L, (ROW_TILE,))
    nq = L // T
    npair = HEADS // 2
    scale = HEAD_DIM ** -0.5
    SUB = ATTN_ROWS
    nsub = T // SUB

    def body(q_ref, k_ref, v_ref, cq_ref, ck_ref, o_ref, lse_ref):
        qb = pl.program_id(1)
        rows = [slice(r * SUB, (r + 1) * SUB) for r in range(nsub)]
        head1 = lax.broadcasted_iota(jnp.int32, (SUB, LANES), 1) >= HEAD_DIM
        qs = [[jnp.where(head1 == (h == 1), q_ref[rs, :] * scale, 0.0).astype(BF16) for rs in rows] for h in range(2)]
        cqs = [[cq_ref[0, rs, h:h + 1] for rs in rows] for h in range(2)]

        def logits(kb):
            ks = pl.multiple_of(kb * T, T)
            k = k_ref[pl.ds(ks, T), :]
            return tuple(lax.dot_general(qs[h][r], k, _DIMS["nt"], preferred_element_type=F32) + cqs[h][r]
                         - ck_ref[0, h:h + 1, pl.ds(ks, T)] for h in range(2) for r in range(nsub))

        def softmax_step(kb, s_all, carry, masked):
            ks = pl.multiple_of(kb * T, T)
            v = v_ref[pl.ds(ks, T), :]
            lane = lax.broadcasted_iota(jnp.int32, (T, LANES), 1)
            new = []
            for h in range(2):
                vh = jnp.where(lane == spare[h], 1.0, v).astype(BF16)
                for r in range(nsub):
                    m, acc = carry[h * nsub + r]
                    s = s_all[h * nsub + r]
                    if masked:
                        keep = (lax.broadcasted_iota(jnp.int32, (SUB, T), 1)
                                <= lax.broadcasted_iota(jnp.int32, (SUB, T), 0) + r * SUB)
                        s = jnp.where(keep, s, NEG_BIG)
                    m_new = jnp.maximum(m, jnp.max(s, axis=1, keepdims=True))
                    p = jnp.exp(s - m_new)
                    acc = jnp.exp(m - m_new) * acc + jnp.dot(p.astype(BF16), vh, preferred_element_type=F32)
                    new.append((m_new, acc))
            return tuple(new)

        def step(kb, state):
            s_all, carry = state
            s_next = logits(kb + 1)
            return s_next, softmax_step(kb, s_all, carry, False)

        spare = (HEAD_DIM, 0)
        one = (jnp.full((SUB, 1), NEG_BIG, F32), jnp.zeros((SUB, LANES), F32))
        s_all, carry = lax.fori_loop(0, qb, step, (logits(0), (one,) * (2 * nsub)))
        carry = softmax_step(qb, s_all, carry, True)
        out, lse = [], []
        for h in range(2):
            chains = carry[h * nsub:(h + 1) * nsub]
            ls = [acc[:, spare[h]:spare[h] + 1] for _, acc in chains]
            out.append(jnp.concatenate([acc / l for (_, acc), l in zip(chains, ls)], axis=0))
            lse.append(jnp.concatenate([m + jnp.log(l) for (m, _), l in zip(chains, ls)], axis=0))
        o_ref[...] = jnp.where(lax.broadcasted_iota(jnp.int32, (T, LANES), 1) >= HEAD_DIM, out[1], out[0]
                               ).astype(o_ref.dtype)
        lse_ref[0] = jnp.concatenate(lse, axis=1)

    return carrier_call(
        body, name, (npair, nq),
        [pl.BlockSpec((T, LANES), lambda p, i: (i, p)),
         pl.BlockSpec((L, LANES), lambda p, i: (0, npair + p)),
         pl.BlockSpec((L, LANES), lambda p, i: (0, 2 * npair + p)),
         pl.BlockSpec((1, T, 2), lambda p, i: (p, i, 0)),
         pl.BlockSpec((1, 2, L), lambda p, i: (p, 0, 0))],
        [pl.BlockSpec((T, LANES), lambda p, i: (i, p)), pl.BlockSpec((1, T, 2), lambda p, i: (p, i, 0))],
        [jax.ShapeDtypeStruct((L, ATTN_W), BF16), jax.ShapeDtypeStruct((npair, L, 2), F32)],
        [], (proj, proj, proj, cq_col, ck_row), comm, ("parallel", "parallel"))


def attn_delta(dcat, cat, name):
    L = dcat.shape[0]
    T = _div_tile(L, (ROW_TILE,))

    def body(do_ref, o_ref, d_ref):
        prod = do_ref[...] * o_ref[...].astype(F32)
        sel = (lax.broadcasted_iota(jnp.int32, (ATTN_W, LANES), 0) // HEAD_DIM
               == lax.broadcasted_iota(jnp.int32, (ATTN_W, LANES), 1)).astype(F32)
        d_ref[...] = jnp.dot(prod, sel, precision=lax.Precision.HIGHEST, preferred_element_type=F32)

    return pl.pallas_call(
        body, name=name, grid=(L // T,),
        in_specs=[pl.BlockSpec((T, ATTN_W), lambda i: (i, 0)), pl.BlockSpec((T, ATTN_W), lambda i: (i, 0))],
        out_specs=pl.BlockSpec((T, LANES), lambda i: (i, 0)),
        out_shape=jax.ShapeDtypeStruct((L, LANES), F32),
        compiler_params=_params(("parallel",)),
    )(dcat, cat)


def attn_bwd(proj, dcat, lse_row, delta_row, cq_row, ck_col, name, comm=None):
    L = proj.shape[0]
    T = _div_tile(L, (ROW_TILE,))
    nb = L // T
    npair = HEADS // 2
    scale = HEAD_DIM ** -0.5

    def body(q_ref, k_ref, v_ref, do_ref, lse_ref, dl_ref, cq_ref, ck_ref,
             dq_ref, dk_ref, dv_ref, dcq_ref, dck_ref, dq_acc, dcq_acc):
        kb = pl.program_id(1)

        @pl.when(kb == 0)
        def _():
            dq_acc[...] = jnp.zeros_like(dq_acc)
            dcq_acc[...] = jnp.zeros_like(dcq_acc)

        head1 = lax.broadcasted_iota(jnp.int32, (T, LANES), 1) >= HEAD_DIM
        ks = [jnp.where(head1 == (h == 1), k_ref[...] * scale, 0.0).astype(BF16) for h in range(2)]
        vs = [jnp.where(head1 == (h == 1), v_ref[...], 0.0).astype(BF16) for h in range(2)]
        cks = [ck_ref[0, :, h:h + 1] for h in range(2)]

        def step(qb, carry, masked):
            qs = pl.multiple_of(qb * T, T)
            q = q_ref[pl.ds(qs, T), :]
            do = do_ref[pl.ds(qs, T), :].astype(BF16)
            new, dq = [], None
            for h in range(2):
                dk, dv, dck = carry[h]
                lse = lse_ref[0, h:h + 1, pl.ds(qs, T)]
                dl = dl_ref[0, h:h + 1, pl.ds(qs, T)]
                cq = cq_ref[0, h:h + 1, pl.ds(qs, T)]
                st = lax.dot_general(ks[h], q, _DIMS["nt"], preferred_element_type=F32) + cq - cks[h]
                if masked:
                    keep = lax.broadcasted_iota(jnp.int32, (T, T), 0) <= lax.broadcasted_iota(jnp.int32, (T, T), 1)
                    st = jnp.where(keep, st, NEG_BIG)
                pt = jnp.exp(st - lse)
                dv = dv + jnp.dot(pt.astype(BF16), do, preferred_element_type=F32)
                dpt = lax.dot_general(vs[h], do, _DIMS["nt"], preferred_element_type=F32)
                dst = pt * (dpt - dl)
                dsb = dst.astype(BF16)
                dk = dk + jnp.dot(dsb, q, preferred_element_type=F32)
                part = lax.dot_general(dsb, ks[h], _DIMS["tn"], preferred_element_type=F32)
                dq = part if dq is None else dq + part
                dcq_acc[h:h + 1, pl.ds(qs, T)] += jnp.sum(dst, axis=0, keepdims=True)
                dck = dck + jnp.sum(dst, axis=1, keepdims=True)
                new.append((dk, dv, dck))
            dq_acc[pl.ds(qs, T), :] += dq
            return tuple(new)

        one = (jnp.zeros((T, LANES), F32), jnp.zeros((T, LANES), F32), jnp.zeros((T, 1), F32))
        carry = step(kb, (one, one), True)
        carry = lax.fori_loop(kb + 1, nb, functools.partial(step, masked=False), carry)
        (dk0, dv0, dck0), (dk1, dv1, dck1) = carry
        dk_ref[...] = (jnp.where(head1, dk1, dk0) * scale).astype(dk_ref.dtype)
        dv_ref[...] = jnp.where(head1, dv1, dv0).astype(dv_ref.dtype)
        dck_ref[0] = jnp.concatenate([-dck0, -dck1], axis=1)

        @pl.when(kb == nb - 1)
        def _():
            dq_ref[...] = dq_acc[...].astype(dq_ref.dtype)
            dcq_ref[0] = dcq_acc[...]

    full = lambda col: pl.BlockSpec((L, LANES), col)
    row_stat = pl.BlockSpec((1, 2, L), lambda p, i: (p, 0, 0))
    return carrier_call(
        body, name, (npair, nb),
        [full(lambda p, i: (0, p)),
         pl.BlockSpec((T, LANES), lambda p, i: (i, npair + p)),
         pl.BlockSpec((T, LANES), lambda p, i: (i, 2 * npair + p)),
         full(lambda p, i: (0, p)),
         row_stat, row_stat, row_stat,
         pl.BlockSpec((1, T, 2), lambda p, i: (p, i, 0))],
        [full(lambda p, i: (0, p)),
         pl.BlockSpec((T, LANES), lambda p, i: (i, p)),
         pl.BlockSpec((T, LANES), lambda p, i: (i, p)),
         row_stat,
         pl.BlockSpec((1, T, 2), lambda p, i: (p, i, 0))],
        [jax.ShapeDtypeStruct((L, ATTN_W), BF16)] * 3
        + [jax.ShapeDtypeStruct((npair, 2, L), F32), jax.ShapeDtypeStruct((npair, L, 2), F32)],
        [pltpu.VMEM((L, LANES), F32), pltpu.VMEM((2, L), F32)],
        (proj, proj, proj, dcat, lse_row, delta_row, cq_row, ck_col), comm, ("parallel", "arbitrary"))


def _shift_down(x, k):
    rolled = pltpu.roll(x, k, 0)
    return jnp.where(lax.broadcasted_iota(jnp.int32, x.shape, 0) >= k, rolled, 0.0)


def _shift_up(x, k):
    n = x.shape[0]
    rolled = pltpu.roll(x, n - k, 0)
    return jnp.where(lax.broadcasted_iota(jnp.int32, x.shape, 0) < n - k, rolled, 0.0)


def conv_fwd(proj, col0, conv_w, conv_b, name):
    L = proj.shape[0]
    C = conv_w.shape[1]
    nc = C // LANES

    def body(gb_ref, gc_ref, xc_ref, w_ref, b_ref, o_ref):
        z = gc_ref[...] * xc_ref[...]
        conv = (w_ref[0:1, :] * _shift_down(z, 2) + w_ref[1:2, :] * _shift_down(z, 1) + w_ref[2:3, :] * z
                + b_ref[...])
        o_ref[...] = (gb_ref[...] * conv).astype(o_ref.dtype)

    col = lambda off: pl.BlockSpec((L, LANES), lambda j, off=off: (0, col0 + off + j))
    return pl.pallas_call(
        body, name=name, grid=(nc,),
        in_specs=[col(0), col(nc), col(2 * nc), pl.BlockSpec((CONV_K, LANES), lambda j: (0, j)),
                  pl.BlockSpec((1, LANES), lambda j: (0, j))],
        out_specs=pl.BlockSpec((L, LANES), lambda j: (0, j)),
        out_shape=jax.ShapeDtypeStruct((L, C), BF16),
        compiler_params=_params(("parallel",)),
    )(proj, proj, proj, conv_w, conv_b)


def conv_bwd(proj, col0, conv_w, conv_b, dcat, dcol0, name):
    L = proj.shape[0]
    C = conv_w.shape[1]
    nc = C // LANES

    def body(gb_ref, gc_ref, xc_ref, w_ref, b_ref, do_ref, dgb_ref, dgc_ref, dxc_ref, dw_ref, db_ref):
        gc, xc = gc_ref[...], xc_ref[...]
        z = gc * xc
        z1, z2 = _shift_down(z, 1), _shift_down(z, 2)
        w0, w1, w2 = w_ref[0:1, :], w_ref[1:2, :], w_ref[2:3, :]
        conv = w0 * z2 + w1 * z1 + w2 * z + b_ref[...]
        dout = do_ref[...]
        dgb_ref[...] = (dout * conv).astype(dgb_ref.dtype)
        dconv = dout * gb_ref[...]
        dw_ref[...] = jnp.concatenate([jnp.sum(dconv * z2, axis=0, keepdims=True),
                                       jnp.sum(dconv * z1, axis=0, keepdims=True),
                                       jnp.sum(dconv * z, axis=0, keepdims=True)], axis=0)
        db_ref[...] = jnp.sum(dconv, axis=0, keepdims=True)
        dz = w2 * dconv + w1 * _shift_up(dconv, 1) + w0 * _shift_up(dconv, 2)
        dgc_ref[...] = (dz * xc).astype(dgc_ref.dtype)
        dxc_ref[...] = (dz * gc).astype(dxc_ref.dtype)

    col = lambda off: pl.BlockSpec((L, LANES), lambda j, off=off: (0, col0 + off + j))
    out_col = pl.BlockSpec((L, LANES), lambda j: (0, j))
    return pl.pallas_call(
        body, name=name, grid=(nc,),
        in_specs=[col(0), col(nc), col(2 * nc), pl.BlockSpec((CONV_K, LANES), lambda j: (0, j)),
                  pl.BlockSpec((1, LANES), lambda j: (0, j)),
                  pl.BlockSpec((L, LANES), lambda j: (0, dcol0 + j))],
        out_specs=[out_col, out_col, out_col, pl.BlockSpec((CONV_K, LANES), lambda j: (0, j)),
                   pl.BlockSpec((1, LANES), lambda j: (0, j))],
        out_shape=[jax.ShapeDtypeStruct((L, C), BF16)] * 3
        + [jax.ShapeDtypeStruct((CONV_K, C), F32), jax.ShapeDtypeStruct((1, C), F32)],
        compiler_params=_params(("parallel",)),
    )(proj, proj, proj, conv_w, conv_b, dcat)


_GELU_C = math.sqrt(2.0 / math.pi)
_GELU_A = 0.044715


def _gelu(y):
    return 0.5 * y * (1.0 + jnp.tanh(_GELU_C * (y + _GELU_A * y * y * y)))


def _gelu_grad(y):
    t = jnp.tanh(_GELU_C * (y + _GELU_A * y * y * y))
    return 0.5 * (1.0 + t) + 0.5 * y * (1.0 - t * t) * _GELU_C * (1.0 + 3.0 * _GELU_A * y * y)


def _cmul_add(xr, xi, pr, pi, sr, si):
    return xr + pr * sr - pi * si, xi + pr * si + pi * sr


def _scan_tile(br, bi, cr, ci, tab_ref, reverse):
    n = S5_BLOCK_STATES
    xr, xi = br, bi
    for s, k in enumerate((1, 2, 4)):
        shift = SUBLANES - k if reverse else k
        xr, xi = _cmul_add(xr, xi, tab_ref[0, s, :, :n], tab_ref[0, s, :, n:],
                           pltpu.roll(xr, shift, 0), pltpu.roll(xi, shift, 0))
    return _cmul_add(xr, xi, tab_ref[0, 3, :, :n], tab_ref[0, 3, :, n:], cr, ci)


def s5_fwd(u, bmat, cmat, tab, dvec, name, comm=None):
    L, D = u.shape
    nblk = D // LANES
    T = _div_tile(L, (ROW_TILE,))
    ns = 2 * S5_BLOCK_STATES
    n = S5_BLOCK_STATES

    def body(u_ref, b_ref, c_ref, tab_ref, d_ref, y_ref, g_ref, xs_ref, buf, car):
        @pl.when(pl.program_id(1) == 0)
        def _():
            car[...] = jnp.zeros_like(car)

        uv = u_ref[...]
        buf[...] = jnp.dot(uv.astype(BF16), b_ref[0], preferred_element_type=F32)

        def tile(i, carry):
            cr, ci = carry
            r0 = pl.multiple_of(i * SUBLANES, SUBLANES)
            xr, xi = _scan_tile(buf[pl.ds(r0, SUBLANES), :n], buf[pl.ds(r0, SUBLANES), n:], cr, ci, tab_ref, False)
            buf[pl.ds(r0, SUBLANES), :n] = xr
            buf[pl.ds(r0, SUBLANES), n:] = xi
            return xr[SUBLANES - 1:, :], xi[SUBLANES - 1:, :]

        cr, ci = lax.fori_loop(0, T // SUBLANES, tile, (car[:, :n], car[:, n:]))
        car[:, :n] = cr
        car[:, n:] = ci
        xs = buf[...]
        xs_ref[...] = xs
        y = jnp.dot(xs.astype(BF16), c_ref[0], preferred_element_type=F32) + d_ref[...] * uv
        y_ref[...] = y
        g_ref[...] = _gelu(y).astype(g_ref.dtype)

    blk = pl.BlockSpec((T, LANES), lambda j, i: (i, j))
    return carrier_call(
        body, name, (nblk, L // T),
        [blk, pl.BlockSpec((1, LANES, ns), lambda j, i: (j, 0, 0)),
         pl.BlockSpec((1, ns, LANES), lambda j, i: (j, 0, 0)),
         pl.BlockSpec((1, 4, SUBLANES, ns), lambda j, i: (j, 0, 0, 0)),
         pl.BlockSpec((1, LANES), lambda j, i: (0, j))],
        [blk, blk, pl.BlockSpec((T, ns), lambda j, i: (i, j))],
        [jax.ShapeDtypeStruct((L, D), F32), jax.ShapeDtypeStruct((L, D), BF16),
         jax.ShapeDtypeStruct((L, nblk * ns), F32)],
        [pltpu.VMEM((T, ns), F32), pltpu.VMEM((1, ns), F32)],
        (u, bmat, cmat, tab, dvec), comm, ("parallel", "arbitrary"))


def s5_bwd(dg, y, u, xs, cmat_t, bmat_t, rtab, dvec, name, comm=None):
    L, D = u.shape
    nblk = D // LANES
    T = _div_tile(L, (ROW_TILE,))
    nch = L // T
    ns = 2 * S5_BLOCK_STATES
    n = S5_BLOCK_STATES
    ntile = T // SUBLANES

    def body(dg_ref, y_ref, u_ref, xs_ref, xp_ref, ct_ref, bt_ref, tab_ref, d_ref,
             du_ref, dc_ref, db_ref, dlam_ref, dd_ref, buf, xbuf, car):
        step = pl.program_id(1)
        first_chunk = step == nch - 1

        @pl.when(step == 0)
        def _():
            car[...] = jnp.zeros_like(car)
            dc_ref[...] = jnp.zeros_like(dc_ref)
            db_ref[...] = jnp.zeros_like(db_ref)
            dlam_ref[...] = jnp.zeros_like(dlam_ref)
            dd_ref[...] = jnp.zeros_like(dd_ref)

        uv = u_ref[...]
        dy = dg_ref[...].astype(F32) * _gelu_grad(y_ref[...])
        dd_ref[...] += jnp.sum(dy * uv, axis=0, keepdims=True)
        dyb = dy.astype(BF16)
        buf[...] = jnp.dot(dyb, ct_ref[0], preferred_element_type=F32)
        xs = xs_ref[...]
        xbuf[pl.ds(SUBLANES, T), :] = xs
        xbuf[pl.ds(0, SUBLANES), :] = jnp.where(first_chunk, 0.0, xp_ref[...])
        row0 = lax.broadcasted_iota(jnp.int32, (SUBLANES, n), 0) == 0

        def tile(ii, carry):
            cr, ci, ar, ai = carry
            r0 = pl.multiple_of((ntile - 1 - ii) * SUBLANES, SUBLANES)
            xr, xi = _scan_tile(buf[pl.ds(r0, SUBLANES), :n], buf[pl.ds(r0, SUBLANES), n:], cr, ci, tab_ref, True)
            buf[pl.ds(r0, SUBLANES), :n] = xr
            buf[pl.ds(r0, SUBLANES), n:] = xi
            r1 = pl.multiple_of(r0 + SUBLANES, SUBLANES)
            pr = jnp.where(row0, xbuf[pl.ds(r0, SUBLANES), :n][SUBLANES - 1:, :],
                           pltpu.roll(xbuf[pl.ds(r1, SUBLANES), :n], 1, 0))
            pi = jnp.where(row0, xbuf[pl.ds(r0, SUBLANES), n:][SUBLANES - 1:, :],
                           pltpu.roll(xbuf[pl.ds(r1, SUBLANES), n:], 1, 0))
            ar = ar + xr * pr + xi * pi
            ai = ai + xi * pr - xr * pi
            return xr[0:1, :], xi[0:1, :], ar, ai

        zero = jnp.zeros((SUBLANES, n), F32)
        cr, ci, ar, ai = lax.fori_loop(0, ntile, tile, (car[:, :n], car[:, n:], zero, zero))
        car[:, :n] = cr
        car[:, n:] = ci
        dlam_ref[0, :, :n] += ar
        dlam_ref[0, :, n:] += ai
        dxa = buf[...]
        dc_ref[0] += lax.dot_general(dyb, xs.astype(BF16), _DIMS["tn"], preferred_element_type=F32)
        dxb = dxa.astype(BF16)
        db_ref[0] += lax.dot_general(uv.astype(BF16), dxb, _DIMS["tn"], preferred_element_type=F32)
        du_ref[...] = jnp.dot(dxb, bt_ref[0], preferred_element_type=F32) + d_ref[...] * dy

    rev = lambda j, i: (nch - 1 - i, j)
    blk = pl.BlockSpec((T, LANES), rev)
    tpb = T // SUBLANES
    acc = pl.BlockSpec((1, LANES, ns), lambda j, i: (j, 0, 0))
    return carrier_call(
        body, name, (nblk, nch),
        [blk, blk, blk, pl.BlockSpec((T, ns), rev),
         pl.BlockSpec((SUBLANES, ns), lambda j, i: (jnp.maximum((nch - 1 - i) * tpb - 1, 0), j)),
         pl.BlockSpec((1, LANES, ns), lambda j, i: (j, 0, 0)),
         pl.BlockSpec((1, ns, LANES), lambda j, i: (j, 0, 0)),
         pl.BlockSpec((1, 4, SUBLANES, ns), lambda j, i: (j, 0, 0, 0)),
         pl.BlockSpec((1, LANES), lambda j, i: (0, j))],
        [blk, acc, acc, pl.BlockSpec((1, SUBLANES, ns), lambda j, i: (j, 0, 0)),
         pl.BlockSpec((1, LANES), lambda j, i: (0, j))],
        [jax.ShapeDtypeStruct((L, D), F32), jax.ShapeDtypeStruct((nblk, LANES, ns), F32),
         jax.ShapeDtypeStruct((nblk, LANES, ns), F32), jax.ShapeDtypeStruct((nblk, SUBLANES, ns), F32),
         jax.ShapeDtypeStruct((1, D), F32)],
        [pltpu.VMEM((T, ns), F32), pltpu.VMEM((T + SUBLANES, ns), F32), pltpu.VMEM((1, ns), F32)],
        (dg, y, u, xs, xs, cmat_t, bmat_t, rtab, dvec), comm, ("parallel", "arbitrary"))


def _s5_discretize(a_re, a_im, log_step, b_re, b_im):
    lam_re = jnp.minimum(a_re, -S5_MIN_DECAY)
    lam_im = a_im
    delta = jnp.exp(log_step)[:, None]
    mag = jnp.exp(lam_re * delta)
    ang = lam_im * delta
    lb_re = mag * jnp.cos(ang)
    lb_im = mag * jnp.sin(ang)
    den = lam_re * lam_re + lam_im * lam_im
    nr = lb_re - 1.0
    ni = lb_im
    coef_re = (nr * lam_re + ni * lam_im) / den
    coef_im = (ni * lam_re - nr * lam_im) / den
    bb_re = coef_re[..., None] * b_re - coef_im[..., None] * b_im
    bb_im = coef_re[..., None] * b_im + coef_im[..., None] * b_re
    return lb_re, lb_im, bb_re, bb_im


def _s5_tables(lb_re, lb_im):
    nblk = lb_re.shape[0] // S5_BLOCK_GROUPS
    lr = lb_re.reshape(nblk, S5_BLOCK_STATES)
    li = lb_im.reshape(nblk, S5_BLOCK_STATES)
    pows = [(jnp.ones_like(lr), jnp.zeros_like(li))]
    for _ in range(SUBLANES):
        pr, pi = pows[-1]
        pows.append((pr * lr - pi * li, pr * li + pi * lr))
    rows = jnp.arange(SUBLANES)[None, :, None]

    def table(conj, reverse):
        sgn = -1.0 if conj else 1.0
        out = []
        for k in (1, 2, 4):
            mask = (rows <= SUBLANES - 1 - k) if reverse else (rows >= k)
            out.append(jnp.concatenate([jnp.where(mask, pows[k][0][:, None, :], 0.0),
                                        jnp.where(mask, sgn * pows[k][1][:, None, :], 0.0)], axis=-1))
        order = range(SUBLANES, 0, -1) if reverse else range(1, SUBLANES + 1)
        cre = jnp.stack([pows[k][0] for k in order], axis=1)
        cim = jnp.stack([sgn * pows[k][1] for k in order], axis=1)
        out.append(jnp.concatenate([cre, cim], axis=-1))
        return jnp.stack(out, axis=1)

    return table(False, False), table(True, True)


def _s5_block_mats(bb_re, bb_im, c_re, c_im):
    G = bb_re.shape[0]
    nblk = G // S5_BLOCK_GROUPS
    eye = jnp.eye(S5_BLOCK_GROUPS, dtype=F32)
    bb = jnp.stack([bb_re, bb_im]).reshape(2, nblk, S5_BLOCK_GROUPS, S5_STATE, S5_GROUP)
    bmat = jnp.einsum("ab,rjaph->jahrbp", eye, bb).reshape(nblk, LANES, 2 * S5_BLOCK_STATES)
    cc = jnp.stack([c_re, -c_im]).reshape(2, nblk, S5_BLOCK_GROUPS, S5_GROUP, S5_STATE)
    cmat = jnp.einsum("ab,rjahp->jrbpah", eye, cc).reshape(nblk, 2 * S5_BLOCK_STATES, LANES)
    return bmat, cmat


def _s5_unblock(dB, dC, dlam):
    nblk = dB.shape[0]
    G = nblk * S5_BLOCK_GROUPS
    d6 = dB.reshape(nblk, S5_BLOCK_GROUPS, S5_GROUP, 2, S5_BLOCK_GROUPS, S5_STATE)
    dbb = jnp.einsum("jahrap->rjaph", d6).reshape(2, G, S5_STATE, S5_GROUP)
    c6 = dC.reshape(nblk, S5_BLOCK_GROUPS, S5_GROUP, 2, S5_BLOCK_GROUPS, S5_STATE)
    dcc = jnp.einsum("jahrap->rjahp", c6).reshape(2, G, S5_GROUP, S5_STATE)
    dl = jnp.sum(dlam, axis=1).reshape(nblk, 2, S5_BLOCK_GROUPS, S5_STATE)
    dl = jnp.transpose(dl, (1, 0, 2, 3)).reshape(2, G, S5_STATE)
    return dbb[0], dbb[1], dcc[0], -dcc[1], dl[0], dl[1]


def loss_and_grad(y, target, name):
    L, D = y.shape
    tr = _div_tile(L, (512, 256, 128))

    def body(y_ref, t_ref, dy_ref, loss_ref):
        err = y_ref[...] - t_ref[...]
        dy_ref[...] = err * (1.0 / D)
        part = 0.5 * jnp.sum(jnp.mean(err * err, axis=-1, keepdims=True), axis=0, keepdims=True)

        @pl.when(pl.program_id(0) == 0)
        def _():
            loss_ref[...] = part

        @pl.when(pl.program_id(0) > 0)
        def _():
            loss_ref[...] += part

    row = pl.BlockSpec((tr, D), lambda i: (i, 0))
    return pl.pallas_call(
        body, name=name, grid=(L // tr,), in_specs=[row, row],
        out_specs=[row, pl.BlockSpec((1, 1), lambda i: (0, 0))],
        out_shape=[jax.ShapeDtypeStruct((L, D), F32), jax.ShapeDtypeStruct((1, 1), F32)],
        compiler_params=_params(("arbitrary",)),
    )(y, target)


def _adam_math(w, g, m, v):
    m = ADAM_B1 * m + (1.0 - ADAM_B1) * g
    v = ADAM_B2 * v + (1.0 - ADAM_B2) * (g * g)
    m_hat = m / (1.0 - ADAM_B1 ** ADAM_STEP)
    v_hat = v / (1.0 - ADAM_B2 ** ADAM_STEP)
    delta = -ADAM_LR * (m_hat / (jnp.sqrt(v_hat) + ADAM_EPS) + ADAM_WD * w)
    return delta, m, v


def _as3d(a):
    return a.reshape((-1,) + a.shape[-2:])


def adamw(w, g, m, v, name):
    shape = w.shape
    w3, g3, m3, v3 = _as3d(w), _as3d(g), _as3d(m), _as3d(v)
    A, R, C = w3.shape
    tr = _row_tile(R, C)

    def body(w_ref, g_ref, m_ref, v_ref, d_ref, mo_ref, vo_ref):
        d, mn, vn = _adam_math(w_ref[...], g_ref[...], m_ref[...], v_ref[...])
        d_ref[...] = d
        mo_ref[...] = mn
        vo_ref[...] = vn

    spec = pl.BlockSpec((1, tr, C), lambda a, i: (a, i, 0))
    outs = pl.pallas_call(
        body, name=name, grid=(A, R // tr), in_specs=[spec] * 4, out_specs=[spec] * 3,
        out_shape=[jax.ShapeDtypeStruct((A, R, C), F32)] * 3,
        compiler_params=_params(("parallel", "parallel")),
    )(w3, g3, m3, v3)
    return [o.reshape(shape) for o in outs]


def _place():
    x, y, c = lax.axis_index("x"), lax.axis_index("y"), lax.axis_index("c")
    other_chips = [(1 - x, y), (x, 1 - y), (1 - x, 1 - y)]
    return x, y, c, other_chips


def _chip_id(chip):
    return 2 * chip[0] + chip[1]


def _my_chip():
    return 2 * lax.axis_index("x") + lax.axis_index("y")


def _remote(src, dst, send_sem, recv_sem, dev):
    return pltpu.make_async_remote_copy(src_ref=src, dst_ref=dst, send_sem=send_sem, recv_sem=recv_sem,
                                        device_id=dev, device_id_type=MESH)


def allgather_small(arrs):
    T = len(arrs)

    def body(*refs):
        ins, outs = refs[:T], refs[T:2 * T]
        send_sems, recv_sems = refs[2 * T:]
        x, y, c, chips = _place()
        me = _chip_id((x, y))
        sends = []
        for t in range(T):
            outs[t][me] = ins[t][...]
            for j, chip in enumerate(chips):
                cp = _remote(ins[t], outs[t].at[me], send_sems.at[3 * t + j], recv_sems.at[3 * t + j], (*chip, c))
                cp.start()
                sends.append(cp)
        for t in range(T):
            for j, chip in enumerate(chips):
                slot = outs[t].at[_chip_id(chip)]
                _remote(slot, slot, send_sems.at[3 * t + j], recv_sems.at[3 * t + j], (*chip, c)).wait_recv()
        for cp in sends:
            cp.wait_send()

    return pl.pallas_call(
        body, name="allgather_small", in_specs=[VMEM_SPEC] * T, out_specs=[VMEM_SPEC] * T,
        out_shape=[jax.ShapeDtypeStruct((4,) + a.shape, a.dtype) for a in arrs],
        scratch_shapes=[pltpu.SemaphoreType.DMA((3 * T,)), pltpu.SemaphoreType.DMA((3 * T,))],
        compiler_params=pltpu.CompilerParams(vmem_limit_bytes=VMEM_LIMIT, has_side_effects=True),
    )(*arrs)


def allreduce_small(buf):
    R, C = buf.shape

    def body(in_ref, out_ref, pair_ref, all_ref, send_sems, recv_sems):
        x, y, c, chips = _place()
        me, sibling = _chip_id((x, y)), (x, y, 1 - c)
        swap = _remote(in_ref, pair_ref, send_sems.at[0], recv_sems.at[0], sibling)
        swap.start()
        swap.wait()
        all_ref[me] = in_ref[...] + pair_ref[...]
        sends = []
        for j, chip in enumerate(chips):
            cp = _remote(all_ref.at[me], all_ref.at[me], send_sems.at[1 + j], recv_sems.at[1 + j], (*chip, c))
            cp.start()
            sends.append(cp)
        for j, chip in enumerate(chips):
            slot = all_ref.at[_chip_id(chip)]
            _remote(slot, slot, send_sems.at[1 + j], recv_sems.at[1 + j], (*chip, c)).wait_recv()
        for cp in sends:
            cp.wait_send()
        out_ref[...] = ((all_ref[0] + all_ref[1]) + all_ref[2]) + all_ref[3]

    return pl.pallas_call(
        body, name="allreduce_small", in_specs=[VMEM_SPEC], out_specs=VMEM_SPEC,
        out_shape=jax.ShapeDtypeStruct((R, C), F32),
        scratch_shapes=[pltpu.VMEM((R, C), F32), pltpu.VMEM((4, R, C), F32),
                        pltpu.SemaphoreType.DMA((4,)), pltpu.SemaphoreType.DMA((4,))],
        compiler_params=pltpu.CompilerParams(vmem_limit_bytes=VMEM_LIMIT, has_side_effects=True),
    )(buf)


def _half_rows(ref, layout, shard, half):
    if layout == "S":
        hr = ref.shape[1] // 2
        return ref.at[shard, pl.ds(pl.multiple_of(half * hr, 16), hr), :]
    hr, C = ref.shape[0] // 2, ref.shape[1] // 4
    return ref.at[pl.ds(pl.multiple_of(half * hr, 16), hr), pl.ds(pl.multiple_of(shard * C, LANES), C)]


def _half_rows_all(ref, layout, half):
    if layout == "S":
        hr = ref.shape[1] // 2
        return ref.at[:, pl.ds(pl.multiple_of(half * hr, 16), hr), :]
    hr = ref.shape[0] // 2
    return ref.at[pl.ds(pl.multiple_of(half * hr, 16), hr), :]


def _shard_of_half(ref, layout, shard):
    if layout == "S":
        return ref.at[shard]
    C = ref.shape[1] // 4
    return ref.at[:, pl.ds(pl.multiple_of(shard * C, LANES), C)]


def _own_block_spec(layout, tr, C):
    if layout == "S":
        return pl.BlockSpec((None, tr, C), lambda i: (_my_chip(), i, 0))
    return pl.BlockSpec((tr, C), lambda i: (i, _my_chip()))


def cast_into_gathered(shards, layer, layout, name):
    _, R, C = shards.shape
    tr = _row_tile(R, C)

    def body(a_ref, o_ref):
        o_ref[...] = a_ref[...].astype(BF16)

    return pl.pallas_call(
        body, name=name, grid=(R // tr,),
        in_specs=[pl.BlockSpec((None, tr, C), lambda i: (layer, i, 0))],
        out_specs=_own_block_spec(layout, tr, C),
        out_shape=jax.ShapeDtypeStruct((4, R, C) if layout == "S" else (R, 4 * C), BF16),
        compiler_params=_params(("parallel",)),
    )(shards)


def gather_comm(bufs, layouts):
    T = len(bufs)

    def begin(_, outs, send_sems, recv_sems):
        x, y, c, chips = _place()
        for t in range(T):
            mine = _half_rows(outs[t], layouts[t], _chip_id((x, y)), c)
            for j, chip in enumerate(chips):
                _remote(mine, mine, send_sems.at[6 * t + j], recv_sems.at[6 * t + j], (*chip, c)).start()

    def middle(_, outs, send_sems, recv_sems):
        x, y, c, chips = _place()
        for t in range(T):
            for j, chip in enumerate(chips):
                piece = _half_rows(outs[t], layouts[t], _chip_id(chip), c)
                _remote(piece, piece, send_sems.at[6 * t + j], recv_sems.at[6 * t + j], (*chip, c)).wait_recv()
                _remote(piece, piece, send_sems.at[6 * t + 3 + j], recv_sems.at[6 * t + 3 + j], (x, y, 1 - c)).start()

    def finish(_, outs, send_sems, recv_sems):
        x, y, c, chips = _place()
        for t in range(T):
            mine = _half_rows(outs[t], layouts[t], _chip_id((x, y)), c)
            for j, chip in enumerate(chips):
                theirs = _half_rows(outs[t], layouts[t], _chip_id(chip), 1 - c)
                _remote(theirs, theirs, send_sems.at[6 * t + 3 + j], recv_sems.at[6 * t + 3 + j],
                        (x, y, 1 - c)).wait_recv()
                _remote(mine, mine, send_sems.at[6 * t + j], recv_sems.at[6 * t + j], (*chip, c)).wait_send()
                piece = _half_rows(outs[t], layouts[t], _chip_id(chip), c)
                _remote(piece, piece, send_sems.at[6 * t + 3 + j], recv_sems.at[6 * t + 3 + j],
                        (x, y, 1 - c)).wait_send()

    return Comm(bufs, [jax.ShapeDtypeStruct(b.shape, b.dtype) for b in bufs], {t: t for t in range(T)}, 6 * T,
                begin, middle, finish, middle_frac=0.75)


def swap_comm(grads, layouts):
    T = len(grads)

    def out_shape(g, layout):
        return (4, g.shape[1] // 2, g.shape[2]) if layout == "S" else (g.shape[0] // 2, g.shape[1])

    def copies(ins, outs, send_sems, recv_sems):
        x, y, c, _ = _place()
        return [_remote(_half_rows_all(ins[t], layouts[t], 1 - c), outs[t], send_sems.at[t], recv_sems.at[t],
                        (x, y, 1 - c)) for t in range(T)]

    def begin(*refs):
        for cp in copies(*refs):
            cp.start()

    def finish(*refs):
        for cp in copies(*refs):
            cp.wait()

    return Comm(grads, [jax.ShapeDtypeStruct(out_shape(g, k), F32) for g, k in zip(grads, layouts)], {}, T,
                begin, None, finish)


def pair_sum(grad, recv, layout, name):
    if layout == "S":
        _, hr, C = recv.shape
        tr = _row_tile(hr, C)
        nb = hr // tr
        grid = (4, nb)
        g_spec = pl.BlockSpec((None, tr, C), lambda a, i: (a, lax.axis_index("c") * nb + i, 0))
        spec = pl.BlockSpec((None, tr, C), lambda a, i: (a, i, 0))
    else:
        hr, C = recv.shape
        tr = _row_tile(hr, C)
        nb = hr // tr
        grid = (nb,)
        g_spec = pl.BlockSpec((tr, C), lambda i: (lax.axis_index("c") * nb + i, 0))
        spec = pl.BlockSpec((tr, C), lambda i: (i, 0))

    def body(g_ref, r_ref, f_ref, b_ref):
        s = g_ref[...] + r_ref[...]
        f_ref[...] = s
        b_ref[...] = s.astype(BF16)

    return pl.pallas_call(
        body, name=name, grid=grid, in_specs=[g_spec, spec], out_specs=[spec, spec],
        out_shape=[jax.ShapeDtypeStruct(recv.shape, F32), jax.ShapeDtypeStruct(recv.shape, BF16)],
        compiler_params=_params(("parallel",) * len(grid)),
    )(grad, recv)


def exchange_comm(pair_bf16, layouts):
    T = len(pair_bf16)

    def out_shape(p, layout):
        return (3,) + ((p.shape[1], p.shape[2]) if layout == "S" else (p.shape[0], p.shape[1] // 4))

    def copies(ins, outs, send_sems, recv_sems):
        x, y, c, chips = _place()
        return [_remote(_shard_of_half(ins[t], layouts[t], _chip_id(chip)), outs[t].at[j],
                        send_sems.at[3 * t + j], recv_sems.at[3 * t + j], (*chip, c))
                for t in range(T) for j, chip in enumerate(chips)]

    def begin(*refs):
        for cp in copies(*refs):
            cp.start()

    def finish(*refs):
        for cp in copies(*refs):
            cp.wait()

    return Comm(pair_bf16, [jax.ShapeDtypeStruct(out_shape(p, k), BF16) for p, k in zip(pair_bf16, layouts)], {},
                3 * T, begin, None, finish)


def reduce_total(pair_f32, got, layout, layer, n_layers, previous, name):
    _, hr, C = got.shape
    tr = _row_tile(hr, C)
    nb = hr // tr

    def body(*refs):
        p_ref, g_ref, t_ref = refs[0], refs[1], refs[-1]
        t_ref[...] = ((p_ref[...] + g_ref[0].astype(F32)) + g_ref[1].astype(F32)) + g_ref[2].astype(F32)

    args = [pair_f32, got] + ([previous] if previous is not None else [])
    return pl.pallas_call(
        body, name=name, grid=(nb,),
        in_specs=[_own_block_spec(layout, tr, C), pl.BlockSpec((3, tr, C), lambda i: (0, i, 0))]
        + ([ANY] if previous is not None else []),
        out_specs=pl.BlockSpec((None, tr, C), lambda i: (layer, lax.axis_index("c") * nb + i, 0)),
        out_shape=jax.ShapeDtypeStruct((n_layers, 2 * hr, C), F32),
        input_output_aliases={2: 0} if previous is not None else {},
        compiler_params=_params(("parallel",)),
    )(*args)


def share_comm(reduced):
    T = len(reduced)

    def halves(outs, half):
        return [o.at[:, pl.ds(pl.multiple_of(half * (o.shape[1] // 2), 8), o.shape[1] // 2), :] for o in outs]

    def begin(_, outs, send_sems, recv_sems):
        x, y, c, _p = _place()
        for t, mine in enumerate(halves(outs, c)):
            _remote(mine, mine, send_sems.at[t], recv_sems.at[t], (x, y, 1 - c)).start()

    def finish(_, outs, send_sems, recv_sems):
        x, y, c, _p = _place()
        for t, (mine, theirs) in enumerate(zip(halves(outs, c), halves(outs, 1 - c))):
            _remote(mine, mine, send_sems.at[t], recv_sems.at[t], (x, y, 1 - c)).wait_send()
            _remote(theirs, theirs, send_sems.at[t], recv_sems.at[t], (x, y, 1 - c)).wait_recv()

    return Comm(reduced, [jax.ShapeDtypeStruct(r.shape, r.dtype) for r in reduced], {t: t for t in range(T)}, T,
                begin, None, finish)


def _round_up(n, m):
    return (n + m - 1) // m * m


def _heads_col(a16):
    L = a16.shape[0]
    return jnp.transpose(a16.reshape(L, HEADS // 2, 2), (1, 0, 2))


def _heads_row(a16):
    L = a16.shape[0]
    return jnp.transpose(a16.reshape(L, HEADS // 2, 2), (1, 2, 0))


def local_step(x, target, meta, norm_g, b_f, conv_w, conv_b, s5, s5_d, stager):
    S, D = x.shape
    depth = norm_g.shape[0]
    n_even, n_odd = b_f.shape[0], s5_d.shape[0]
    L = N_META + S
    Lp = _round_up(L, ROW_TILE)
    h = jnp.concatenate([meta, x, jnp.zeros((Lp - L, D), F32)], axis=0)
    conv_c = conv_w.shape[2]
    fg_block = 3 * conv_c // LANES
    saved = []

    def riding(tag, fn, *args):
        comm = stager.ride(tag)
        if comm is None and fn is matmul:
            return fn(*args, name=tag)
        outs, arrived = fn(*args, name=tag, comm=comm)
        stager.arrived(tag, arrived)
        return outs[0] if fn is matmul else outs

    for i in range(depth):
        g = norm_g[i]
        j = i // 2
        tag = "l%d_" % i
        w = stager.weights(i)
        st = {"h0": h, "w": w}
        if i % 2 == 0:
            u = rmsnorm_fwd(h, g[0:1], BF16, tag + "norm0")
            qkv = matmul([(u, w["w_qkv"])], "nn", BF16, tag + "qkv")
            rest = matmul([(u, w["w_rest"])], "nn", F32, tag + "rest")
            cgate = gate_fwd(rest, fg_block, b_f[j], tag + "gate")
            c16 = cgate[:, :HEADS]
            attn, lse = riding(tag + "attn", attn_fwd, qkv, _heads_col(c16), _heads_row(c16))
            convo = conv_fwd(rest, 0, conv_w[j], conv_b[j], tag + "conv")
            cat = jnp.concatenate([attn, convo], axis=1)
            m = matmul([(cat, w["w_o"])], "nn", F32, tag + "wo")
            st.update(u=u, qkv=qkv, rest=rest, c16=c16, lse=lse, cat=cat)
        else:
            p = s5[j]
            u = rmsnorm_fwd(h, g[0:1], F32, tag + "norm0")
            y, gact, xs = riding(tag + "s5", s5_fwd, u, p["bmat"], p["cmat"], p["tab"], s5_d[j])
            o1, o2, m = riding(tag + "glu", dual_matmul_act, gact, w["w_glu1"], w["w_glu2"], "glu", F32)
            st.update(u=u, y=y, gact=gact, xs=xs, o1=o1, o2=o2)
        h1 = rmsnorm_fwd(m, g[1:2], F32, tag + "norm1", residual=h)
        u2 = rmsnorm_fwd(h1, g[2:3], BF16, tag + "norm2")
        a, b, hact = riding(tag + "ffn_in", dual_matmul_act, u2, w["w_gate"], w["w_up"], "swiglu", BF16)
        f = riding(tag + "ffn_out", matmul, [(hact, w["w_down"])], "nn", F32)
        h = rmsnorm_fwd(f, g[3:4], F32, tag + "norm3", residual=h1)
        st.update(m=m, h1=h1, u2=u2, a=a, b=b, hact=hact, f=f)
        saved.append(st)

    dy, loss = loss_and_grad(h[N_META:L], target, "loss")
    dh = jnp.concatenate([jnp.zeros((N_META, D), F32), dy, jnp.zeros((Lp - L, D), F32)], axis=0)

    grads = {k: [None] * n_even for k in ("b_f", "conv_w", "conv_b")}
    grads.update({k: [None] * n_odd for k in ("s5_d", "s5_dB", "s5_dC", "s5_dlam")})
    grads["norm_g"] = [None] * depth

    for i in reversed(range(depth)):
        g = norm_g[i]
        j = i // 2
        tag = "l%d_b_" % i
        st = saved[i]
        w = st["w"]
        wg = {}
        df, dg3 = rmsnorm_bwd(st["f"], g[3:4], dh, BF16, tag + "norm3")
        wg["w_down"] = matmul([(st["hact"], df)], "tn", F32, tag + "dw_down")
        da, db = riding(tag + "ffn_act", ffn_bwd_act, df, w["w_down"], st["a"], st["b"])
        wg["w_gate"] = matmul([(st["u2"], da)], "tn", F32, tag + "dw_gate")
        wg["w_up"] = matmul([(st["u2"], db)], "tn", F32, tag + "dw_up")
        stager.put_grads(i, wg)
        wg = {}
        du2 = riding(tag + "du2", matmul, [(da, w["w_gate"]), (db, w["w_up"])], "nt", F32)
        dh1, dg2 = rmsnorm_bwd(st["h1"], g[2:3], du2, F32, tag + "norm2", add=dh)
        if i % 2 == 0:
            dm, dg1 = rmsnorm_bwd(st["m"], g[1:2], dh1, BF16, tag + "norm1")
            wg["w_o"] = matmul([(st["cat"], dm)], "tn", F32, tag + "dw_o")
            dcat = matmul([(dm, w["w_o"])], "nt", F32, tag + "dcat")
            delta = attn_delta(dcat, st["cat"], tag + "delta")
            c16 = st["c16"]
            lse16 = jnp.transpose(st["lse"], (1, 0, 2)).reshape(Lp, HEADS)
            dq, dk, dv, dcq, dck = riding(tag + "attn", attn_bwd, st["qkv"], dcat, _heads_row(lse16),
                                          _heads_row(delta[:, :HEADS]), _heads_row(c16), _heads_col(c16))
            dc16 = (jnp.transpose(dcq, (2, 0, 1)).reshape(Lp, HEADS)
                    + jnp.transpose(dck, (1, 0, 2)).reshape(Lp, HEADS))
            dc = jnp.pad(dc16, ((0, 0), (0, LANES - HEADS)))
            dfg, dbf = gate_bwd(st["rest"], fg_block, b_f[j], dc, tag + "gate")
            dgb, dgc, dxc, dcw, dcb = conv_bwd(st["rest"], 0, conv_w[j], conv_b[j], dcat, ATTN_W // LANES,
                                               tag + "conv")
            dqkv = jnp.concatenate([dq, dk, dv], axis=1)
            drest = jnp.concatenate([dgb, dgc, dxc, dfg.astype(BF16)], axis=1)
            wg["w_qkv"] = matmul([(st["u"], dqkv)], "tn", F32, tag + "dw_qkv")
            wg["w_rest"] = matmul([(st["u"], drest)], "tn", F32, tag + "dw_rest")
            du = matmul([(dqkv, w["w_qkv"])], "nt", F32, tag + "du_qkv")
            du_b = matmul([(drest, w["w_rest"])], "nt", F32, tag + "du_rest")
            grads["b_f"][j], grads["conv_w"][j], grads["conv_b"][j] = dbf, dcw, dcb
        else:
            p = s5[j]
            dmix, dg1 = rmsnorm_bwd(st["m"], g[1:2], dh1, F32, tag + "norm1")
            do1, do2 = glu_bwd_act(dmix, st["o1"], st["o2"], tag + "glu_act")
            wg["w_glu1"] = matmul([(st["gact"], do1)], "tn", F32, tag + "dw_glu1")
            wg["w_glu2"] = matmul([(st["gact"], do2)], "tn", F32, tag + "dw_glu2")
            dgact = matmul([(do1, w["w_glu1"]), (do2, w["w_glu2"])], "nt", F32, tag + "dgact")
            du, dC, dB, dlam, dd = riding(tag + "s5", s5_bwd, dgact, st["y"], st["u"], st["xs"], p["cmat_t"],
                                          p["bmat_t"], p["rtab"], s5_d[j])
            du_b = None
            grads["s5_dB"][j], grads["s5_dC"][j], grads["s5_dlam"][j], grads["s5_d"][j] = dB, dC, dlam, dd
        dh, dg0 = rmsnorm_bwd(st["h0"], g[0:1], du, F32, tag + "norm0", add=dh1, dy2=du_b)
        grads["norm_g"][i] = jnp.concatenate([dg0, dg1, dg2, dg3], axis=0)
        stager.put_grads(i, wg)

    grads["meta"] = dh[:N_META]
    return loss, dh[N_META:L], grads


def _pack(arrs):
    flat = jnp.concatenate([a.reshape(-1).astype(F32) for a in arrs])
    n = flat.shape[0]
    rows = _round_up(_round_up(n, LANES) // LANES, SUBLANES)
    return jnp.pad(flat, (0, rows * LANES - n)).reshape(rows, LANES)


def _unpack(buf, shapes):
    flat = buf.reshape(-1)
    out, off = [], 0
    for s in shapes:
        size = math.prod(s)
        out.append(flat[off:off + size].reshape(s))
        off += size
    return out


class _LayerWeights:
    def __init__(self, stager, layer):
        self.stager, self.layer = stager, layer

    def __getitem__(self, name):
        return self.stager.weight(self.layer, name)


class MeshStager:
    LAYOUT = {"ab_w_in": "S", "ab_w_o": "S", "s5_w_glu1": "S", "s5_w_glu2": "S",
              "ffn_w_gate": "C", "ffn_w_up": "C", "ffn_w_down": "S"}
    EVEN = ("ab_w_in", "ab_w_o", "ffn_w_gate", "ffn_w_up", "ffn_w_down")
    ODD = ("s5_w_glu1", "s5_w_glu2", "ffn_w_gate", "ffn_w_up", "ffn_w_down")

    def __init__(self, shards):
        self.shards = shards
        self.depth = depth = shards["ffn_w_gate"].shape[0]
        self.bufs = {}
        for i in range(depth):
            for k in self.keys(i):
                self.bufs[k, i] = cast_into_gathered(shards[k], self.index(k, i), self.LAYOUT[k],
                                                     "cast_%s_%d" % (k, i))
        self.grads, self.pairs, self.reduced, self.ready = {}, {}, {}, {}
        first = [("ab_w_in", 0)]
        self.gather_plan = {"l0_attn": [it for it in self.stage(0) if it not in first]}
        for i in range(depth - 1):
            if i % 2 == 0:
                self.gather_plan.setdefault("l%d_attn" % i, []).extend(self.stage(i + 1))
            else:
                self.gather_plan["l%d_s5" % i] = self.mix(i + 1)
                self.gather_plan["l%d_ffn_in" % i] = [("ffn_w_gate", i + 1), ("ffn_w_up", i + 1)]
                self.gather_plan["l%d_ffn_out" % i] = [("ffn_w_down", i + 1)]
        self.swap_plan, self.exchange_plan = {}, {}
        for i in range(depth):
            above = self.mix(i + 1) if i + 1 < depth else []
            if above:
                self.swap_plan["l%d_b_ffn_act" % i] = above
            self.swap_plan["l%d_b_du2" % i] = self.ffn(i)
            self.exchange_plan["l%d_b_%s" % (i, "attn" if i % 2 == 0 else "s5")] = above + self.ffn(i)
        self._store(first, comm_call("gather_first", self._gather(first, 0.5)))

    def keys(self, i):
        return self.EVEN if i % 2 == 0 else self.ODD

    def stage(self, i):
        return [(k, i) for k in self.keys(i)]

    def mix(self, i):
        return [(k, i) for k in self.keys(i) if not k.startswith("ffn")]

    @staticmethod
    def ffn(i):
        return [("ffn_w_gate", i), ("ffn_w_up", i), ("ffn_w_down", i)]

    @staticmethod
    def index(key, i):
        return i if key.startswith("ffn") else i // 2

    def _layouts(self, items):
        return [self.LAYOUT[k] for k, _ in items]

    def _gather(self, items, middle_frac):
        comm = gather_comm([self.bufs[it] for it in items], self._layouts(items))
        comm.middle_frac = middle_frac
        return comm

    def _store(self, items, bufs):
        for it, b in zip(items, bufs):
            self.bufs[it] = b

    def ride(self, tag):
        if tag in self.gather_plan:
            return self._gather(self.gather_plan[tag], 0.85 if tag == "l0_attn" else 0.7)
        if tag in self.swap_plan:
            items = self.swap_plan[tag]
            return swap_comm([self.grads[it] for it in items], self._layouts(items))
        if tag in self.exchange_plan:
            items = self.exchange_plan[tag]
            return exchange_comm([self.pairs[it][1] for it in items], self._layouts(items))
        return None

    def arrived(self, tag, outs):
        if tag in self.gather_plan:
            self._store(self.gather_plan[tag], outs)
        elif tag in self.swap_plan:
            self._pair_sums(self.swap_plan[tag], outs)
        elif tag in self.exchange_plan:
            self._totals(self.exchange_plan[tag], outs)

    def _pair_sums(self, items, received):
        for it, r in zip(items, received):
            self.pairs[it] = pair_sum(self.grads[it], r, self.LAYOUT[it[0]], "pair_sum_%s_%d" % it)

    def _totals(self, items, got):
        for it, g in zip(items, got):
            k, i = it
            self.reduced[k] = reduce_total(self.pairs[it][0], g, self.LAYOUT[k], self.index(k, i),
                                           self.shards[k].shape[0], self.reduced.get(k), "reduce_total_%s_%d" % it)

    def weights(self, i):
        return _LayerWeights(self, i)

    def weight(self, i, name):
        if (name, i) not in self.ready:
            if name in ("w_qkv", "w_rest"):
                b = self.bufs["ab_w_in", i]
                w_in = jnp.transpose(b, (1, 0, 2)).reshape(b.shape[1], 4 * b.shape[2])
                self.ready["w_qkv", i], self.ready["w_rest", i] = split_w_in(w_in)
            else:
                k = {"w_o": "ab_w_o", "w_glu1": "s5_w_glu1", "w_glu2": "s5_w_glu2"}.get(name, "ffn_" + name)
                b = self.bufs[k, i]
                self.ready[name, i] = b.reshape(4 * b.shape[1], b.shape[2]) if self.LAYOUT[k] == "S" else b
        return self.ready[name, i]

    def put_grads(self, i, wg):
        for k in self.keys(i):
            _, R, C = self.shards[k].shape
            name = {"ab_w_in": "w_qkv", "ab_w_o": "w_o", "s5_w_glu1": "w_glu1", "s5_w_glu2": "w_glu2"}.get(k, k[4:])
            if name not in wg:
                continue
            if k == "ab_w_in":
                dw = merge_dw_in(wg["w_qkv"], wg["w_rest"])
                self.grads[k, i] = jnp.transpose(dw.reshape(R, 4, C), (1, 0, 2))
            else:
                self.grads[k, i] = wg[name].reshape(4, R, C) if self.LAYOUT[k] == "S" else wg[name]

    def finish(self):
        last = self.mix(0)
        self._pair_sums(last, comm_call("swap_last", swap_comm([self.grads[it] for it in last], self._layouts(last))))
        self._totals(last, comm_call("exchange_last", exchange_comm([self.pairs[it][1] for it in last],
                                                                    self._layouts(last))))
        names = list(self.LAYOUT)
        return dict(zip(names, comm_call("share_reduced", share_comm([self.reduced[k] for k in names]))))


def split_w_in(w_in):
    fg0 = 3 * ATTN_W
    w_rest = jnp.concatenate([w_in[:, fg0 + HEADS:], w_in[:, fg0:fg0 + HEADS],
                              jnp.zeros((w_in.shape[0], LANES - HEADS), w_in.dtype)], axis=1)
    return w_in[:, :fg0], w_rest


def merge_dw_in(dw_qkv, dw_rest):
    nqc = dw_rest.shape[1] - LANES
    return jnp.concatenate([dw_qkv, dw_rest[:, nqc:nqc + HEADS], dw_rest[:, :nqc]], axis=1)


def device_step(x, target, P, stager):
    D = x.shape[-1]
    n_even, n_odd = P["ab_b_f"].shape[0], P["s5_d"].shape[0]
    conv_c = P["ab_conv_b"].shape[1]
    b_f_pad = jnp.pad(P["ab_b_f"], ((0, 0), (0, LANES - HEADS))).reshape(n_even, 1, LANES)

    s5, s5_vjps = [], []
    for j in range(n_odd):
        disc, vjp = jax.vjp(_s5_discretize, P["s5_a_re"][j], P["s5_a_im"][j], P["s5_log_step"][j],
                            P["s5_b_re"][j], P["s5_b_im"][j])
        lb_re, lb_im, bb_re, bb_im = disc
        tab, rtab = _s5_tables(lb_re, lb_im)
        bmat, cmat = _s5_block_mats(bb_re, bb_im, P["s5_c_re"][j], P["s5_c_im"][j])
        s5.append(dict(tab=tab, rtab=rtab, bmat=bmat.astype(BF16), cmat=cmat.astype(BF16),
                       bmat_t=jnp.transpose(bmat, (0, 2, 1)).astype(BF16),
                       cmat_t=jnp.transpose(cmat, (0, 2, 1)).astype(BF16)))
        s5_vjps.append(vjp)

    loss, grad_x, G = local_step(
        x, target, P["meta_tokens"], P["norm_g"], b_f_pad, P["ab_conv_w"],
        P["ab_conv_b"].reshape(n_even, 1, conv_c), s5, P["s5_d"].reshape(n_odd, 1, D), stager)

    out = {
        "meta_tokens": G["meta"],
        "norm_g": jnp.stack(G["norm_g"]),
        "ab_b_f": jnp.stack([b[0, :HEADS] for b in G["b_f"]]),
        "ab_conv_w": jnp.stack(G["conv_w"]),
        "ab_conv_b": jnp.stack([b[0] for b in G["conv_b"]]),
        "s5_d": jnp.stack([d[0] for d in G["s5_d"]]),
    }
    s5g = {k: [] for k in ("s5_a_re", "s5_a_im", "s5_log_step", "s5_b_re", "s5_b_im", "s5_c_re", "s5_c_im")}
    for j in range(n_odd):
        dbb_re, dbb_im, dc_re, dc_im, dl_re, dl_im = _s5_unblock(G["s5_dB"][j], G["s5_dC"][j], G["s5_dlam"][j])
        da_re, da_im, dls, db_re, db_im = s5_vjps[j]((dl_re, dl_im, dbb_re, dbb_im))
        for k, val in zip(s5g, (da_re, da_im, dls, db_re, db_im, dc_re, dc_im)):
            s5g[k].append(val)
    out.update({k: jnp.stack(v) for k, v in s5g.items()})
    return loss, grad_x, out


def kernel(x, meta_tokens, norm_g, ab_w_in, ab_b_f, ab_conv_w, ab_conv_b, ab_w_o, s5_a_re, s5_a_im, s5_log_step, s5_b_re, s5_b_im, s5_c_re, s5_c_im, s5_d, s5_w_glu1, s5_w_glu2, ffn_w_gate, ffn_w_up, ffn_w_down, loss_target, m_meta_tokens, m_norm_g, m_ab_w_in, m_ab_b_f, m_ab_conv_w, m_ab_conv_b, m_ab_w_o, m_s5_a_re, m_s5_a_im, m_s5_log_step, m_s5_b_re, m_s5_b_im, m_s5_c_re, m_s5_c_im, m_s5_d, m_s5_w_glu1, m_s5_w_glu2, m_ffn_w_gate, m_ffn_w_up, m_ffn_w_down, v_meta_tokens, v_norm_g, v_ab_w_in, v_ab_b_f, v_ab_conv_w, v_ab_conv_b, v_ab_w_o, v_s5_a_re, v_s5_a_im, v_s5_log_step, v_s5_b_re, v_s5_b_im, v_s5_c_re, v_s5_c_im, v_s5_d, v_s5_w_glu1, v_s5_w_glu2, v_ffn_w_gate, v_ffn_w_up, v_ffn_w_down):
    names = ["meta_tokens", "norm_g", "ab_w_in", "ab_b_f", "ab_conv_w", "ab_conv_b", "ab_w_o", "s5_a_re", "s5_a_im",
             "s5_log_step", "s5_b_re", "s5_b_im", "s5_c_re", "s5_c_im", "s5_d", "s5_w_glu1", "s5_w_glu2",
             "ffn_w_gate", "ffn_w_up", "ffn_w_down"]
    W = dict(zip(names, [meta_tokens, norm_g, ab_w_in, ab_b_f, ab_conv_w, ab_conv_b, ab_w_o, s5_a_re, s5_a_im,
                         s5_log_step, s5_b_re, s5_b_im, s5_c_re, s5_c_im, s5_d, s5_w_glu1, s5_w_glu2,
                         ffn_w_gate, ffn_w_up, ffn_w_down]))
    Mo = dict(zip(names, [m_meta_tokens, m_norm_g, m_ab_w_in, m_ab_b_f, m_ab_conv_w, m_ab_conv_b, m_ab_w_o, m_s5_a_re,
                          m_s5_a_im, m_s5_log_step, m_s5_b_re, m_s5_b_im, m_s5_c_re, m_s5_c_im, m_s5_d, m_s5_w_glu1,
                          m_s5_w_glu2, m_ffn_w_gate, m_ffn_w_up, m_ffn_w_down]))
    Vo = dict(zip(names, [v_meta_tokens, v_norm_g, v_ab_w_in, v_ab_b_f, v_ab_conv_w, v_ab_conv_b, v_ab_w_o, v_s5_a_re,
                          v_s5_a_im, v_s5_log_step, v_s5_b_re, v_s5_b_im, v_s5_c_re, v_s5_c_im, v_s5_d, v_s5_w_glu1,
                          v_s5_w_glu2, v_ffn_w_gate, v_ffn_w_up, v_ffn_w_down]))
    D = x.shape[-1]
    n_even, n_odd, depth = ab_w_in.shape[0], s5_w_glu1.shape[0], ffn_w_gate.shape[0]
    chip = 2 * lax.axis_index("x") + lax.axis_index("y")

    big = list(MeshStager.LAYOUT)
    stager = MeshStager({k: W[k] for k in big})
    g_meta, g_norm, g_convw, g_s5d = allgather_small([meta_tokens, norm_g, ab_conv_w, s5_d])
    full = {k: W[k] for k in names if k not in big}
    full["meta_tokens"] = jnp.transpose(g_meta, (1, 0, 2)).reshape(N_META, D)
    full["norm_g"] = jnp.transpose(g_norm, (1, 2, 0, 3)).reshape(depth, 4, D)
    full["ab_conv_w"] = jnp.transpose(g_convw, (1, 2, 0, 3)).reshape(n_even, CONV_K, -1)
    full["s5_d"] = jnp.transpose(g_s5d, (1, 0, 2)).reshape(n_odd, D)

    loss, grad_x, G = device_step(x[0], loss_target[0], full, stager)
    reduced = stager.finish()

    small_w = [k for k in names if k not in big]
    small_names = ["loss"] + small_w
    G["loss"] = loss
    summed = dict(zip(small_names, _unpack(allreduce_small(_pack([G[k] for k in small_names])),
                                           [G[k].shape for k in small_names])))
    loss_out = summed["loss"].reshape(())
    for k in ("meta_tokens", "norm_g", "ab_conv_w", "s5_d"):
        n_last = W[k].shape[-1]
        summed[k] = lax.dynamic_slice_in_dim(summed[k], chip * n_last, n_last, axis=summed[k].ndim - 1)
    shapes = [W[k].shape for k in small_w]
    d_s, m_s, v_s = adamw(_pack([W[k] for k in small_w])[None], _pack([summed[k] for k in small_w])[None],
                          _pack([Mo[k] for k in small_w])[None], _pack([Vo[k] for k in small_w])[None], "adamw_small")
    delta = dict(zip(small_w, _unpack(d_s, shapes)))
    new_m = dict(zip(small_w, _unpack(m_s, shapes)))
    new_v = dict(zip(small_w, _unpack(v_s, shapes)))
    grad = {k: summed[k] for k in small_w}
    for k in big:
        grad[k] = reduced[k]
        delta[k], new_m[k], new_v[k] = adamw(W[k], reduced[k], Mo[k], Vo[k], "adamw_" + k)

    return (loss_out, grad_x[None], *[grad[k] for k in names], *[delta[k] for k in names],
            *[new_m[k] for k in names], *[new_v[k] for k in names])
```

```python
import functools
import math

import jax
import jax.numpy as jnp
from jax import lax
from jax.experimental import pallas as pl
from jax.experimental.pallas import tpu as pltpu

F32 = jnp.float32
BF16 = jnp.bfloat16

N_META = 16
HEADS = 16
HEAD_DIM = 64
ATTN_W = HEADS * HEAD_DIM
CONV_K = 3
S5_GROUP = 16
S5_STATE = 64
S5_MIN_DECAY = 1e-4
NORM_EPS = 1e-6
ADAM_LR = 0.001
ADAM_B1 = 0.9
ADAM_B2 = 0.999
ADAM_EPS = 1e-08
ADAM_WD = 0.01
ADAM_STEP = 10

LANES = 128
SUBLANES = 8
VMEM_LIMIT = 56 * 1024 * 1024
VMEM_TILE_BUDGET = 34 * 1024 * 1024
ROW_TILE = 384
ATTN_ROWS = 128
S5_BLOCK_GROUPS = LANES // S5_GROUP
S5_BLOCK_STATES = S5_BLOCK_GROUPS * S5_STATE
NEG_BIG = -1e30

MESH = pl.DeviceIdType.MESH
ANY = pl.BlockSpec(memory_space=pl.ANY)
VMEM_SPEC = pl.BlockSpec(memory_space=pltpu.VMEM)


def _params(sem=None):
    return pltpu.CompilerParams(dimension_semantics=sem, vmem_limit_bytes=VMEM_LIMIT)


def _div_tile(n, prefs):
    for p in prefs:
        if n % p == 0:
            return p
    return n


def _row_tile(rows, cols, itemsize=4, limit=2 * 1024 * 1024):
    for p in (512, 256, 128, 64, 32, 16):
        if rows % p == 0 and p * cols * itemsize <= limit:
            return p
    return 16 if rows % 16 == 0 else rows


def _tile_cands(n):
    c = [d for d in range(LANES, min(n, 2048) + 1, LANES) if n % d == 0]
    if not c or n <= 2048 and n not in c:
        c.append(n)
    return sorted(set(c), reverse=True)


def _mm_tiles(M, N, K, a_bytes, b_bytes, o_bytes, npairs):
    best = None
    for tk in _tile_cands(K):
        for tm in _tile_cands(M):
            for tn in _tile_cands(N):
                mem = npairs * 2 * (tm * tk * a_bytes + tk * tn * b_bytes) + 2 * tm * tn * o_bytes + tm * tn * 4
                mem += npairs * ((tm * tk * 2 if a_bytes == 4 else 0) + (tk * tn * 2 if b_bytes == 4 else 0))
                if mem > VMEM_TILE_BUDGET:
                    continue
                key = (tm * tn * tk, tk, tn)
                if best is None or key > best[0]:
                    best = (key, (tm, tn, tk))
    assert best is not None, (M, N, K)
    return best[1]


class Comm:
    def __init__(self, operands, out_shapes, aliases, n_sems, begin, middle=None, finish=None, middle_frac=0.5):
        self.operands, self.out_shapes, self.aliases, self.n_sems = list(operands), list(out_shapes), aliases, n_sems
        self.begin, self.middle, self.finish, self.middle_frac = begin, middle, finish, middle_frac


def carrier_call(body, name, grid, in_specs, out_specs, out_shape, scratch_shapes, args, comm, semantics):
    n_in, n_out = len(args), len(out_shape)
    if comm is None:
        outs = pl.pallas_call(body, name=name, grid=grid, in_specs=in_specs, out_specs=out_specs, out_shape=out_shape,
                              scratch_shapes=scratch_shapes, compiler_params=_params(semantics))(*args)
        return list(outs), []
    ci, co = len(comm.operands), len(comm.out_shapes)
    total = math.prod(grid)
    middle_at = min(total - 1, max(0, int(total * comm.middle_frac)))

    def carried(*refs):
        ins, cins = refs[:n_in], refs[n_in:n_in + ci]
        outs = refs[n_in + ci:n_in + ci + n_out]
        couts = refs[n_in + ci + n_out:n_in + ci + n_out + co]
        scratch, (send_sems, recv_sems) = refs[n_in + ci + n_out + co:-2], refs[-2:]
        step = 0
        for d, size in enumerate(grid):
            step = step * size + pl.program_id(d)

        @pl.when(step == 0)
        def _():
            comm.begin(cins, couts, send_sems, recv_sems)

        if comm.middle is not None:
            @pl.when(step == middle_at)
            def _():
                comm.middle(cins, couts, send_sems, recv_sems)

        body(*ins, *outs, *scratch)

        @pl.when(step == total - 1)
        def _():
            comm.finish(cins, couts, send_sems, recv_sems)

    outs = pl.pallas_call(
        carried, name=name, grid=grid,
        in_specs=list(in_specs) + [ANY] * ci, out_specs=list(out_specs) + [ANY] * co,
        out_shape=list(out_shape) + comm.out_shapes,
        scratch_shapes=list(scratch_shapes) + [pltpu.SemaphoreType.DMA((comm.n_sems,)),
                                                pltpu.SemaphoreType.DMA((comm.n_sems,))],
        input_output_aliases={n_in + i: n_out + o for i, o in comm.aliases.items()},
        compiler_params=pltpu.CompilerParams(dimension_semantics=("arbitrary",) * len(grid),
                                             vmem_limit_bytes=VMEM_LIMIT, has_side_effects=True),
    )(*args, *comm.operands)
    return list(outs[:n_out]), list(outs[n_out:])


def comm_call(name, comm):
    ci = len(comm.operands)

    def body(*refs):
        cins, couts = refs[:ci], refs[ci:ci + len(comm.out_shapes)]
        send_sems, recv_sems = refs[-2:]
        comm.begin(cins, couts, send_sems, recv_sems)
        if comm.middle is not None:
            comm.middle(cins, couts, send_sems, recv_sems)
        comm.finish(cins, couts, send_sems, recv_sems)

    return pl.pallas_call(
        body, name=name, in_specs=[ANY] * ci, out_specs=[ANY] * len(comm.out_shapes), out_shape=comm.out_shapes,
        input_output_aliases=dict(comm.aliases),
        scratch_shapes=[pltpu.SemaphoreType.DMA((comm.n_sems,)), pltpu.SemaphoreType.DMA((comm.n_sems,))],
        compiler_params=pltpu.CompilerParams(has_side_effects=True),
    )(*comm.operands)


_DIMS ={"nn": (((1,), (0,)), ((), ())), "nt": (((1,), (1,)), ((), ())), "tn": (((0,), (0,)), ((), ()))}


def matmul(pairs, kind, out_dtype, name, comm=None):
    a0, b0 = pairs[0]
    if kind == "nn":
        (M, K), N = a0.shape, b0.shape[1]
    elif kind == "nt":
        (M, K), N = a0.shape, b0.shape[0]
    else:
        (K, M), N = a0.shape, b0.shape[1]
    tm, tn, tk = _mm_tiles(M, N, K, a0.dtype.itemsize, b0.dtype.itemsize, jnp.dtype(out_dtype).itemsize, len(pairs))
    nk = K // tk
    dims = _DIMS[kind]
    npairs = len(pairs)
    n_in = 2 * npairs

    def body(*refs):
        ins, o_ref = refs[:2 * npairs], refs[n_in]
        part = None
        for p in range(npairs):
            d = lax.dot_general(ins[2 * p][...].astype(BF16), ins[2 * p + 1][...].astype(BF16), dims,
                                preferred_element_type=F32)
            part = d if part is None else part + d
        if nk == 1:
            o_ref[...] = part.astype(o_ref.dtype)
        else:
            acc_ref = refs[n_in + 1]
            k = pl.program_id(2)

            @pl.when(k == 0)
            def _():
                acc_ref[...] = part

            @pl.when(k > 0)
            def _():
                acc_ref[...] += part

            @pl.when(k == nk - 1)
            def _():
                o_ref[...] = acc_ref[...].astype(o_ref.dtype)

    if kind == "nn":
        a_blk, a_map = (tm, tk), lambda j, i, k: (i, k)
        b_blk, b_map = (tk, tn), lambda j, i, k: (k, j)
    elif kind == "nt":
        a_blk, a_map = (tm, tk), lambda j, i, k: (i, k)
        b_blk, b_map = (tn, tk), lambda j, i, k: (j, k)
    else:
        a_blk, a_map = (tk, tm), lambda j, i, k: (k, i)
        b_blk, b_map = (tk, tn), lambda j, i, k: (k, j)
    (out,), arrived = carrier_call(
        body, name, (N // tn, M // tm, nk),
        [pl.BlockSpec(a_blk, a_map), pl.BlockSpec(b_blk, b_map)] * npairs,
        [pl.BlockSpec((tm, tn), lambda j, i, k: (i, j))], [jax.ShapeDtypeStruct((M, N), out_dtype)],
        [] if nk == 1 else [pltpu.VMEM((tm, tn), F32)], [t for ab in pairs for t in ab], comm,
        ("parallel", "parallel", "arbitrary"))
    return out if comm is None else ([out], arrived)


def _sigmoid(x):
    return 1.0 / (1.0 + jnp.exp(-x))

def dual_matmul_act(x, w1, w2, act, out_dtype, name, comm=None):
    M, K = x.shape
    N = w1.shape[-1]
    tm = _div_tile(M, (ROW_TILE,))
    tn = _div_tile(N, (1408, 1024, 512, 256, 128))

    def body(x_ref, w1_ref, w2_ref, o1_ref, o2_ref, out_ref):
        xv = x_ref[...]
        o1 = jnp.dot(xv, w1_ref[...], preferred_element_type=F32)
        o2 = jnp.dot(xv, w2_ref[...], preferred_element_type=F32)
        o1_ref[...] = o1.astype(BF16)
        o2_ref[...] = o2.astype(BF16)
        if act == "swiglu":
            out = o1 * _sigmoid(o1) * o2
        else:
            out = o1 * _sigmoid(o2)
        out_ref[...] = out.astype(out_ref.dtype)

    w_spec = pl.BlockSpec((K, tn), lambda j, i: (0, j))
    o_spec = pl.BlockSpec((tm, tn), lambda j, i: (i, j))
    return carrier_call(
        body, name, (N // tn, M // tm), [pl.BlockSpec((tm, K), lambda j, i: (i, 0)), w_spec, w_spec],
        [o_spec, o_spec, o_spec],
        [jax.ShapeDtypeStruct((M, N), BF16), jax.ShapeDtypeStruct((M, N), BF16),
         jax.ShapeDtypeStruct((M, N), out_dtype)], [], (x, w1, w2), comm, ("parallel", "parallel"))


def ffn_bwd_act(df, wd, a, b, name, comm=None):
    M, K = df.shape
    N = wd.shape[0]
    tm = _div_tile(M, (ROW_TILE,))
    tn = _div_tile(N, (1408, 1024, 512, 256, 128))

    def body(df_ref, wd_ref, a_ref, b_ref, da_ref, db_ref):
        dh = lax.dot_general(df_ref[...], wd_ref[...], _DIMS["nt"], preferred_element_type=F32)
        av = a_ref[...].astype(F32)
        bv = b_ref[...].astype(F32)
        sig = _sigmoid(av)
        silu = av * sig
        da_ref[...] = (dh * bv * (sig + silu * (1.0 - sig))).astype(BF16)
        db_ref[...] = (dh * silu).astype(BF16)

    t_spec = pl.BlockSpec((tm, tn), lambda j, i: (i, j))
    return carrier_call(
        body, name, (N // tn, M // tm),
        [pl.BlockSpec((tm, K), lambda j, i: (i, 0)), pl.BlockSpec((tn, K), lambda j, i: (j, 0)), t_spec, t_spec],
        [t_spec, t_spec], [jax.ShapeDtypeStruct((M, N), BF16)] * 2, [], (df, wd, a, b), comm,
        ("parallel", "parallel"))


def glu_bwd_act(dout, o1, o2, name):
    M, N = dout.shape
    tm = _div_tile(M, (ROW_TILE,))

    def body(d_ref, o1_ref, o2_ref, d1_ref, d2_ref):
        d = d_ref[...].astype(F32)
        sig = _sigmoid(o2_ref[...].astype(F32))
        d1_ref[...] = (d * sig).astype(BF16)
        d2_ref[...] = (d * o1_ref[...].astype(F32) * sig * (1.0 - sig)).astype(BF16)

    spec = pl.BlockSpec((tm, N), lambda i: (i, 0))
    return pl.pallas_call(
        body, name=name, grid=(M // tm,), in_specs=[spec] * 3, out_specs=[spec] * 2,
        out_shape=[jax.ShapeDtypeStruct((M, N), BF16)] * 2,
        compiler_params=_params(("parallel",)),
    )(dout, o1, o2)


def rmsnorm_fwd(x, g, out_dtype, name, residual=None):
    L, D = x.shape
    tr = _div_tile(L, (ROW_TILE,))
    has_res = residual is not None

    def body(*refs):
        x_ref, g_ref = refs[0], refs[1]
        o_ref = refs[-1]
        xv = x_ref[...]
        r = lax.rsqrt(jnp.mean(xv * xv, axis=-1, keepdims=True) + NORM_EPS)
        y = xv * r * g_ref[...]
        if has_res:
            y = refs[2][...] + y
        o_ref[...] = y.astype(o_ref.dtype)

    row = pl.BlockSpec((tr, D), lambda i: (i, 0))
    gsp = pl.BlockSpec((1, D), lambda i: (0, 0))
    args = (x, g) + ((residual,) if has_res else ())
    return pl.pallas_call(
        body, name=name, grid=(L // tr,), in_specs=[row, gsp] + ([row] if has_res else []), out_specs=row,
        out_shape=jax.ShapeDtypeStruct((L, D), out_dtype), compiler_params=_params(("parallel",)),
    )(*args)


def rmsnorm_bwd(x, g, dy, out_dtype, name, add=None, dy2=None):
    L, D = x.shape
    tr = _div_tile(L, (ROW_TILE,))
    has_add = add is not None
    has_dy2 = dy2 is not None

    def body(*refs):
        x_ref, g_ref, dy_ref = refs[0], refs[1], refs[2]
        dx_ref, dg_ref = refs[-2], refs[-1]
        xv = x_ref[...]
        dyv = dy_ref[...].astype(F32)
        if has_dy2:
            dyv = dyv + refs[3][...].astype(F32)
        r = lax.rsqrt(jnp.mean(xv * xv, axis=-1, keepdims=True) + NORM_EPS)
        t = dyv * g_ref[...]
        dx = r * t - xv * (r * r * r) * jnp.mean(xv * t, axis=-1, keepdims=True)
        if has_add:
            dx = refs[3 + has_dy2][...] + dx
        dx_ref[...] = dx.astype(dx_ref.dtype)
        dgp = jnp.sum(dyv * xv * r, axis=0, keepdims=True)

        @pl.when(pl.program_id(0) == 0)
        def _():
            dg_ref[...] = dgp

        @pl.when(pl.program_id(0) > 0)
        def _():
            dg_ref[...] += dgp

    row = pl.BlockSpec((tr, D), lambda i: (i, 0))
    gsp = pl.BlockSpec((1, D), lambda i: (0, 0))
    args = (x, g, dy) + ((dy2,) if has_dy2 else ()) + ((add,) if has_add else ())
    return pl.pallas_call(
        body, name=name, grid=(L // tr,), in_specs=[row, gsp] + [row] * (len(args) - 2),
        out_specs=[row, gsp],
        out_shape=[jax.ShapeDtypeStruct((L, D), out_dtype), jax.ShapeDtypeStruct((1, D), F32)],
        compiler_params=_params(("arbitrary",)),
    )(*args)


def _gate_z(fg_ref, b_ref):
    return fg_ref[...] + b_ref[...]


def gate_fwd(fg_src, col_block, b, name):
    L = fg_src.shape[0]
    T = _div_tile(L, (ROW_TILE,))

    def body(fg_ref, b_ref, c_ref, carry):
        @pl.when(pl.program_id(0) == 0)
        def _():
            carry[...] = jnp.zeros_like(carry)

        z = _gate_z(fg_ref, b_ref)
        logf = jnp.minimum(z, 0.0) - jnp.log(1.0 + jnp.exp(-jnp.abs(z)))
        tri = (lax.broadcasted_iota(jnp.int32, (T, T), 1) <= lax.broadcasted_iota(jnp.int32, (T, T), 0)).astype(F32)
        c = jnp.dot(tri, logf, precision=lax.Precision.HIGHEST, preferred_element_type=F32) + carry[...]
        c_ref[...] = c
        carry[...] = c[T - 1:T, :]

    return pl.pallas_call(
        body, name=name, grid=(L // T,),
        in_specs=[pl.BlockSpec((T, LANES), lambda i: (i, col_block)), pl.BlockSpec((1, LANES), lambda i: (0, 0))],
        out_specs=pl.BlockSpec((T, LANES), lambda i: (i, 0)),
        out_shape=jax.ShapeDtypeStruct((L, LANES), F32),
        scratch_shapes=[pltpu.VMEM((1, LANES), F32)],
        compiler_params=_params(("arbitrary",)),
    )(fg_src, b)


def gate_bwd(fg_src, col_block, b, dc, name):
    L = fg_src.shape[0]
    T = _div_tile(L, (ROW_TILE,))
    nb = L // T

    def body(fg_ref, b_ref, dc_ref, dfg_ref, db_ref, carry):
        @pl.when(pl.program_id(0) == 0)
        def _():
            carry[...] = jnp.zeros_like(carry)
            db_ref[...] = jnp.zeros_like(db_ref)

        z = _gate_z(fg_ref, b_ref)
        dcv = dc_ref[...]
        tri = (lax.broadcasted_iota(jnp.int32, (T, T), 1) >= lax.broadcasted_iota(jnp.int32, (T, T), 0)).astype(F32)
        dlogf = jnp.dot(tri, dcv, precision=lax.Precision.HIGHEST, preferred_element_type=F32) + carry[...]
        dfg = dlogf * _sigmoid(-z)
        dfg_ref[...] = dfg
        db_ref[...] += jnp.sum(dfg, axis=0, keepdims=True)
        carry[...] = dlogf[0:1, :]

    return pl.pallas_call(
        body, name=name, grid=(nb,),
        in_specs=[pl.BlockSpec((T, LANES), lambda i: (nb - 1 - i, col_block)),
                  pl.BlockSpec((1, LANES), lambda i: (0, 0)),
                  pl.BlockSpec((T, LANES), lambda i: (nb - 1 - i, 0))],
        out_specs=[pl.BlockSpec((T, LANES), lambda i: (nb - 1 - i, 0)), pl.BlockSpec((1, LANES), lambda i: (0, 0))],
        out_shape=[jax.ShapeDtypeStruct((L, LANES), F32), jax.ShapeDtypeStruct((1, LANES), F32)],
        scratch_shapes=[pltpu.VMEM((1, LANES), F32)],
        compiler_params=_params(("arbitrary",)),
    )(fg_src, b, dc)


def attn_fwd(proj, cq_col, ck_row, name, comm=None):
    L = proj.shape[0]
    T = _div_tile(L, (ROW_TILE,))
    nq = L // T
    npair = HEADS // 2
    scale = HEAD_DIM ** -0.5
    SUB = ATTN_ROWS
    nsub = T // SUB

    def body(q_ref, k_ref, v_ref, cq_ref, ck_ref, o_ref, lse_ref):
        qb = pl.program_id(1)
        rows = [slice(r * SUB, (r + 1) * SUB) for r in range(nsub)]
        head1 = lax.broadcasted_iota(jnp.int32, (SUB, LANES), 1) >= HEAD_DIM
        qs = [[jnp.where(head1 == (h == 1), q_ref[rs, :] * scale, 0.0).astype(BF16) for rs in rows] for h in range(2)]
        cqs = [[cq_ref[0, rs, h:h + 1] for rs in rows] for h in range(2)]

        def logits(kb):
            ks = pl.multiple_of(kb * T, T)
            k = k_ref[pl.ds(ks, T), :]
            return tuple(lax.dot_general(qs[h][r], k, _DIMS["nt"], preferred_element_type=F32) + cqs[h][r]
                         - ck_ref[0, h:h + 1, pl.ds(ks, T)] for h in range(2) for r in range(nsub))

        def softmax_step(kb, s_all, carry, masked):
            ks = pl.multiple_of(kb * T, T)
            v = v_ref[pl.ds(ks, T), :]
            lane = lax.broadcasted_iota(jnp.int32, (T, LANES), 1)
            new = []
            for h in range(2):
                vh = jnp.where(lane == spare[h], 1.0, v).astype(BF16)
                for r in range(nsub):
                    m, acc = carry[h * nsub + r]
                    s = s_all[h * nsub + r]
                    if masked:
                        keep = (lax.broadcasted_iota(jnp.int32, (SUB, T), 1)
                                <= lax.broadcasted_iota(jnp.int32, (SUB, T), 0) + r * SUB)
                        s = jnp.where(keep, s, NEG_BIG)
                    m_new = jnp.maximum(m, jnp.max(s, axis=1, keepdims=True))
                    p = jnp.exp(s - m_new)
                    acc = jnp.exp(m - m_new) * acc + jnp.dot(p.astype(BF16), vh, preferred_element_type=F32)
                    new.append((m_new, acc))
            return tuple(new)

        def step(kb, state):
            s_all, carry = state
            s_next = logits(kb + 1)
            return s_next, softmax_step(kb, s_all, carry, False)

        spare = (HEAD_DIM, 0)
        one = (jnp.full((SUB, 1), NEG_BIG, F32), jnp.zeros((SUB, LANES), F32))
        s_all, carry = lax.fori_loop(0, qb, step, (logits(0), (one,) * (2 * nsub)))
        carry = softmax_step(qb, s_all, carry, True)
        out, lse = [], []
        for h in range(2):
            chains = carry[h * nsub:(h + 1) * nsub]
            ls = [acc[:, spare[h]:spare[h] + 1] for _, acc in chains]
            out.append(jnp.concatenate([acc / l for (_, acc), l in zip(chains, ls)], axis=0))
            lse.append(jnp.concatenate([m + jnp.log(l) for (m, _), l in zip(chains, ls)], axis=0))
        o_ref[...] = jnp.where(lax.broadcasted_iota(jnp.int32, (T, LANES), 1) >= HEAD_DIM, out[1], out[0]
                               ).astype(o_ref.dtype)
        lse_ref[0] = jnp.concatenate(lse, axis=1)

    return carrier_call(
        body, name, (npair, nq),
        [pl.BlockSpec((T, LANES), lambda p, i: (i, p)),
         pl.BlockSpec((L, LANES), lambda p, i: (0, npair + p)),
         pl.BlockSpec((L, LANES), lambda p, i: (0, 2 * npair + p)),
         pl.BlockSpec((1, T, 2), lambda p, i: (p, i, 0)),
         pl.BlockSpec((1, 2, L), lambda p, i: (p, 0, 0))],
        [pl.BlockSpec((T, LANES), lambda p, i: (i, p)), pl.BlockSpec((1, T, 2), lambda p, i: (p, i, 0))],
        [jax.ShapeDtypeStruct((L, ATTN_W), BF16), jax.ShapeDtypeStruct((npair, L, 2), F32)],
        [], (proj, proj, proj, cq_col, ck_row), comm, ("parallel", "parallel"))


def attn_delta(dcat, cat, name):
    L = dcat.shape[0]
    T = _div_tile(L, (ROW_TILE,))

    def body(do_ref, o_ref, d_ref):
        prod = do_ref[...] * o_ref[...].astype(F32)
        sel = (lax.broadcasted_iota(jnp.int32, (ATTN_W, LANES), 0) // HEAD_DIM
               == lax.broadcasted_iota(jnp.int32, (ATTN_W, LANES), 1)).astype(F32)
        d_ref[...] = jnp.dot(prod, sel, precision=lax.Precision.HIGHEST, preferred_element_type=F32)

    return pl.pallas_call(
        body, name=name, grid=(L // T,),
        in_specs=[pl.BlockSpec((T, ATTN_W), lambda i: (i, 0)), pl.BlockSpec((T, ATTN_W), lambda i: (i, 0))],
        out_specs=pl.BlockSpec((T, LANES), lambda i: (i, 0)),
        out_shape=jax.ShapeDtypeStruct((L, LANES), F32),
        compiler_params=_params(("parallel",)),
    )(dcat, cat)


def attn_bwd(proj, dcat, lse_row, delta_row, cq_row, ck_col, name, comm=None):
    L = proj.shape[0]
    T = _div_tile(L, (ROW_TILE,))
    nb = L // T
    npair = HEADS // 2
    scale = HEAD_DIM ** -0.5

    def body(q_ref, k_ref, v_ref, do_ref, lse_ref, dl_ref, cq_ref, ck_ref,
             dq_ref, dk_ref, dv_ref, dcq_ref, dck_ref, dq_acc, dcq_acc):
        kb = pl.program_id(1)

        @pl.when(kb == 0)
        def _():
            dq_acc[...] = jnp.zeros_like(dq_acc)
            dcq_acc[...] = jnp.zeros_like(dcq_acc)

        head1 = lax.broadcasted_iota(jnp.int32, (T, LANES), 1) >= HEAD_DIM
        ks = [jnp.where(head1 == (h == 1), k_ref[...] * scale, 0.0).astype(BF16) for h in range(2)]
        vs = [jnp.where(head1 == (h == 1), v_ref[...], 0.0).astype(BF16) for h in range(2)]
        cks = [ck_ref[0, :, h:h + 1] for h in range(2)]

        def step(qb, carry, masked):
            qs = pl.multiple_of(qb * T, T)
            q = q_ref[pl.ds(qs, T), :]
            do = do_ref[pl.ds(qs, T), :].astype(BF16)
            new, dq = [], None
            for h in range(2):
                dk, dv, dck = carry[h]
                lse = lse_ref[0, h:h + 1, pl.ds(qs, T)]
                dl = dl_ref[0, h:h + 1, pl.ds(qs, T)]
                cq = cq_ref[0, h:h + 1, pl.ds(qs, T)]
                st = lax.dot_general(ks[h], q, _DIMS["nt"], preferred_element_type=F32) + cq - cks[h]
                if masked:
                    keep = lax.broadcasted_iota(jnp.int32, (T, T), 0) <= lax.broadcasted_iota(jnp.int32, (T, T), 1)
                    st = jnp.where(keep, st, NEG_BIG)
                pt = jnp.exp(st - lse)
                dv = dv + jnp.dot(pt.astype(BF16), do, preferred_element_type=F32)
                dpt = lax.dot_general(vs[h], do, _DIMS["nt"], preferred_element_type=F32)
                dst = pt * (dpt - dl)
                dsb = dst.astype(BF16)
                dk = dk + jnp.dot(dsb, q, preferred_element_type=F32)
                part = lax.dot_general(dsb, ks[h], _DIMS["tn"], preferred_element_type=F32)
                dq = part if dq is None else dq + part
                dcq_acc[h:h + 1, pl.ds(qs, T)] += jnp.sum(dst, axis=0, keepdims=True)
                dck = dck + jnp.sum(dst, axis=1, keepdims=True)
                new.append((dk, dv, dck))
            dq_acc[pl.ds(qs, T), :] += dq
            return tuple(new)

        one = (jnp.zeros((T, LANES), F32), jnp.zeros((T, LANES), F32), jnp.zeros((T, 1), F32))
        carry = step(kb, (one, one), True)
        carry = lax.fori_loop(kb + 1, nb, functools.partial(step, masked=False), carry)
        (dk0, dv0, dck0), (dk1, dv1, dck1) = carry
        dk_ref[...] = (jnp.where(head1, dk1, dk0) * scale).astype(dk_ref.dtype)
        dv_ref[...] = jnp.where(head1, dv1, dv0).astype(dv_ref.dtype)
        dck_ref[0] = jnp.concatenate([-dck0, -dck1], axis=1)

        @pl.when(kb == nb - 1)
        def _():
            dq_ref[...] = dq_acc[...].astype(dq_ref.dtype)
            dcq_ref[0] = dcq_acc[...]

    full = lambda col: pl.BlockSpec((L, LANES), col)
    row_stat = pl.BlockSpec((1, 2, L), lambda p, i: (p, 0, 0))
    return carrier_call(
        body, name, (npair, nb),
        [full(lambda p, i: (0, p)),
         pl.BlockSpec((T, LANES), lambda p, i: (i, npair + p)),
         pl.BlockSpec((T, LANES), lambda p, i: (i, 2 * npair + p)),
         full(lambda p, i: (0, p)),
         row_stat, row_stat, row_stat,
         pl.BlockSpec((1, T, 2), lambda p, i: (p, i, 0))],
        [full(lambda p, i: (0, p)),
         pl.BlockSpec((T, LANES), lambda p, i: (i, p)),
         pl.BlockSpec((T, LANES), lambda p, i: (i, p)),
         row_stat,
         pl.BlockSpec((1, T, 2), lambda p, i: (p, i, 0))],
        [jax.ShapeDtypeStruct((L, ATTN_W), BF16)] * 3
        + [jax.ShapeDtypeStruct((npair, 2, L), F32), jax.ShapeDtypeStruct((npair, L, 2), F32)],
        [pltpu.VMEM((L, LANES), F32), pltpu.VMEM((2, L), F32)],
        (proj, proj, proj, dcat, lse_row, delta_row, cq_row, ck_col), comm, ("parallel", "arbitrary"))


def _shift_down(x, k):
    rolled = pltpu.roll(x, k, 0)
    return jnp.where(lax.broadcasted_iota(jnp.int32, x.shape, 0) >= k, rolled, 0.0)


def _shift_up(x, k):
    n = x.shape[0]
    rolled = pltpu.roll(x, n - k, 0)
    return jnp.where(lax.broadcasted_iota(jnp.int32, x.shape, 0) < n - k, rolled, 0.0)


def conv_fwd(proj, col0, conv_w, conv_b, name):
    L = proj.shape[0]
    C = conv_w.shape[1]
    nc = C // LANES

    def body(gb_ref, gc_ref, xc_ref, w_ref, b_ref, o_ref):
        z = gc_ref[...] * xc_ref[...]
        conv = (w_ref[0:1, :] * _shift_down(z, 2) + w_ref[1:2, :] * _shift_down(z, 1) + w_ref[2:3, :] * z
                + b_ref[...])
        o_ref[...] = (gb_ref[...] * conv).astype(o_ref.dtype)

    col = lambda off: pl.BlockSpec((L, LANES), lambda j, off=off: (0, col0 + off + j))
    return pl.pallas_call(
        body, name=name, grid=(nc,),
        in_specs=[col(0), col(nc), col(2 * nc), pl.BlockSpec((CONV_K, LANES), lambda j: (0, j)),
                  pl.BlockSpec((1, LANES), lambda j: (0, j))],
        out_specs=pl.BlockSpec((L, LANES), lambda j: (0, j)),
        out_shape=jax.ShapeDtypeStruct((L, C), BF16),
        compiler_params=_params(("parallel",)),
    )(proj, proj, proj, conv_w, conv_b)


def conv_bwd(proj, col0, conv_w, conv_b, dcat, dcol0, name):
    L = proj.shape[0]
    C = conv_w.shape[1]
    nc = C // LANES

    def body(gb_ref, gc_ref, xc_ref, w_ref, b_ref, do_ref, dgb_ref, dgc_ref, dxc_ref, dw_ref, db_ref):
        gc, xc = gc_ref[...], xc_ref[...]
        z = gc * xc
        z1, z2 = _shift_down(z, 1), _shift_down(z, 2)
        w0, w1, w2 = w_ref[0:1, :], w_ref[1:2, :], w_ref[2:3, :]
        conv = w0 * z2 + w1 * z1 + w2 * z + b_ref[...]
        dout = do_ref[...]
        dgb_ref[...] = (dout * conv).astype(dgb_ref.dtype)
        dconv = dout * gb_ref[...]
        dw_ref[...] = jnp.concatenate([jnp.sum(dconv * z2, axis=0, keepdims=True),
                                       jnp.sum(dconv * z1, axis=0, keepdims=True),
                                       jnp.sum(dconv * z, axis=0, keepdims=True)], axis=0)
        db_ref[...] = jnp.sum(dconv, axis=0, keepdims=True)
        dz = w2 * dconv + w1 * _shift_up(dconv, 1) + w0 * _shift_up(dconv, 2)
        dgc_ref[...] = (dz * xc).astype(dgc_ref.dtype)
        dxc_ref[...] = (dz * gc).astype(dxc_ref.dtype)

    col = lambda off: pl.BlockSpec((L, LANES), lambda j, off=off: (0, col0 + off + j))
    out_col = pl.BlockSpec((L, LANES), lambda j: (0, j))
    return pl.pallas_call(
        body, name=name, grid=(nc,),
        in_specs=[col(0), col(nc), col(2 * nc), pl.BlockSpec((CONV_K, LANES), lambda j: (0, j)),
                  pl.BlockSpec((1, LANES), lambda j: (0, j)),
                  pl.BlockSpec((L, LANES), lambda j: (0, dcol0 + j))],
        out_specs=[out_col, out_col, out_col, pl.BlockSpec((CONV_K, LANES), lambda j: (0, j)),
                   pl.BlockSpec((1, LANES), lambda j: (0, j))],
        out_shape=[jax.ShapeDtypeStruct((L, C), BF16)] * 3
        + [jax.ShapeDtypeStruct((CONV_K, C), F32), jax.ShapeDtypeStruct((1, C), F32)],
        compiler_params=_params(("parallel",)),
    )(proj, proj, proj, conv_w, conv_b, dcat)


_GELU_C = math.sqrt(2.0 / math.pi)
_GELU_A = 0.044715


def _gelu(y):
    return 0.5 * y * (1.0 + jnp.tanh(_GELU_C * (y + _GELU_A * y * y * y)))


def _gelu_grad(y):
    t = jnp.tanh(_GELU_C * (y + _GELU_A * y * y * y))
    return 0.5 * (1.0 + t) + 0.5 * y * (1.0 - t * t) * _GELU_C * (1.0 + 3.0 * _GELU_A * y * y)


def _cmul_add(xr, xi, pr, pi, sr, si):
    return xr + pr * sr - pi * si, xi + pr * si + pi * sr


def _scan_tile(br, bi, cr, ci, tab_ref, reverse):
    n = S5_BLOCK_STATES
    xr, xi = br, bi
    for s, k in enumerate((1, 2, 4)):
        shift = SUBLANES - k if reverse else k
        xr, xi = _cmul_add(xr, xi, tab_ref[0, s, :, :n], tab_ref[0, s, :, n:],
                           pltpu.roll(xr, shift, 0), pltpu.roll(xi, shift, 0))
    return _cmul_add(xr, xi, tab_ref[0, 3, :, :n], tab_ref[0, 3, :, n:], cr, ci)


def s5_fwd(u, bmat, cmat, tab, dvec, name, comm=None):
    L, D = u.shape
    nblk = D // LANES
    T = _div_tile(L, (ROW_TILE,))
    ns = 2 * S5_BLOCK_STATES
    n = S5_BLOCK_STATES

    def body(u_ref, b_ref, c_ref, tab_ref, d_ref, y_ref, g_ref, xs_ref, buf, car):
        @pl.when(pl.program_id(1) == 0)
        def _():
            car[...] = jnp.zeros_like(car)

        uv = u_ref[...]
        buf[...] = jnp.dot(uv.astype(BF16), b_ref[0], preferred_element_type=F32)

        def tile(i, carry):
            cr, ci = carry
            r0 = pl.multiple_of(i * SUBLANES, SUBLANES)
            xr, xi = _scan_tile(buf[pl.ds(r0, SUBLANES), :n], buf[pl.ds(r0, SUBLANES), n:], cr, ci, tab_ref, False)
            buf[pl.ds(r0, SUBLANES), :n] = xr
            buf[pl.ds(r0, SUBLANES), n:] = xi
            return xr[SUBLANES - 1:, :], xi[SUBLANES - 1:, :]

        cr, ci = lax.fori_loop(0, T // SUBLANES, tile, (car[:, :n], car[:, n:]))
        car[:, :n] = cr
        car[:, n:] = ci
        xs = buf[...]
        xs_ref[...] = xs
        y = jnp.dot(xs.astype(BF16), c_ref[0], preferred_element_type=F32) + d_ref[...] * uv
        y_ref[...] = y
        g_ref[...] = _gelu(y).astype(g_ref.dtype)

    blk = pl.BlockSpec((T, LANES), lambda j, i: (i, j))
    return carrier_call(
        body, name, (nblk, L // T),
        [blk, pl.BlockSpec((1, LANES, ns), lambda j, i: (j, 0, 0)),
         pl.BlockSpec((1, ns, LANES), lambda j, i: (j, 0, 0)),
         pl.BlockSpec((1, 4, SUBLANES, ns), lambda j, i: (j, 0, 0, 0)),
         pl.BlockSpec((1, LANES), lambda j, i: (0, j))],
        [blk, blk, pl.BlockSpec((T, ns), lambda j, i: (i, j))],
        [jax.ShapeDtypeStruct((L, D), F32), jax.ShapeDtypeStruct((L, D), BF16),
         jax.ShapeDtypeStruct((L, nblk * ns), F32)],
        [pltpu.VMEM((T, ns), F32), pltpu.VMEM((1, ns), F32)],
        (u, bmat, cmat, tab, dvec), comm, ("parallel", "arbitrary"))


def s5_bwd(dg, y, u, xs, cmat_t, bmat_t, rtab, dvec, name, comm=None):
    L, D = u.shape
    nblk = D // LANES
    T = _div_tile(L, (ROW_TILE,))
    nch = L // T
    ns = 2 * S5_BLOCK_STATES
    n = S5_BLOCK_STATES
    ntile = T // SUBLANES

    def body(dg_ref, y_ref, u_ref, xs_ref, xp_ref, ct_ref, bt_ref, tab_ref, d_ref,
             du_ref, dc_ref, db_ref, dlam_ref, dd_ref, buf, xbuf, car):
        step = pl.program_id(1)
        first_chunk = step == nch - 1

        @pl.when(step == 0)
        def _():
            car[...] = jnp.zeros_like(car)
            dc_ref[...] = jnp.zeros_like(dc_ref)
            db_ref[...] = jnp.zeros_like(db_ref)
            dlam_ref[...] = jnp.zeros_like(dlam_ref)
            dd_ref[...] = jnp.zeros_like(dd_ref)

        uv = u_ref[...]
        dy = dg_ref[...].astype(F32) * _gelu_grad(y_ref[...])
        dd_ref[...] += jnp.sum(dy * uv, axis=0, keepdims=True)
        dyb = dy.astype(BF16)
        buf[...] = jnp.dot(dyb, ct_ref[0], preferred_element_type=F32)
        xs = xs_ref[...]
        xbuf[pl.ds(SUBLANES, T), :] = xs
        xbuf[pl.ds(0, SUBLANES), :] = jnp.where(first_chunk, 0.0, xp_ref[...])
        row0 = lax.broadcasted_iota(jnp.int32, (SUBLANES, n), 0) == 0

        def tile(ii, carry):
            cr, ci, ar, ai = carry
            r0 = pl.multiple_of((ntile - 1 - ii) * SUBLANES, SUBLANES)
            xr, xi = _scan_tile(buf[pl.ds(r0, SUBLANES), :n], buf[pl.ds(r0, SUBLANES), n:], cr, ci, tab_ref, True)
            buf[pl.ds(r0, SUBLANES), :n] = xr
            buf[pl.ds(r0, SUBLANES), n:] = xi
            r1 = pl.multiple_of(r0 + SUBLANES, SUBLANES)
            pr = jnp.where(row0, xbuf[pl.ds(r0, SUBLANES), :n][SUBLANES - 1:, :],
                           pltpu.roll(xbuf[pl.ds(r1, SUBLANES), :n], 1, 0))
            pi = jnp.where(row0, xbuf[pl.ds(r0, SUBLANES), n:][SUBLANES - 1:, :],
                           pltpu.roll(xbuf[pl.ds(r1, SUBLANES), n:], 1, 0))
            ar = ar + xr * pr + xi * pi
            ai = ai + xi * pr - xr * pi
            return xr[0:1, :], xi[0:1, :], ar, ai

        zero = jnp.zeros((SUBLANES, n), F32)
        cr, ci, ar, ai = lax.fori_loop(0, ntile, tile, (car[:, :n], car[:, n:], zero, zero))
        car[:, :n] = cr
        car[:, n:] = ci
        dlam_ref[0, :, :n] += ar
        dlam_ref[0, :, n:] += ai
        dxa = buf[...]
        dc_ref[0] += lax.dot_general(dyb, xs.astype(BF16), _DIMS["tn"], preferred_element_type=F32)
        dxb = dxa.astype(BF16)
        db_ref[0] += lax.dot_general(uv.astype(BF16), dxb, _DIMS["tn"], preferred_element_type=F32)
        du_ref[...] = jnp.dot(dxb, bt_ref[0], preferred_element_type=F32) + d_ref[...] * dy

    rev = lambda j, i: (nch - 1 - i, j)
    blk = pl.BlockSpec((T, LANES), rev)
    tpb = T // SUBLANES
    acc = pl.BlockSpec((1, LANES, ns), lambda j, i: (j, 0, 0))
    return carrier_call(
        body, name, (nblk, nch),
        [blk, blk, blk, pl.BlockSpec((T, ns), rev),
         pl.BlockSpec((SUBLANES, ns), lambda j, i: (jnp.maximum((nch - 1 - i) * tpb - 1, 0), j)),
         pl.BlockSpec((1, LANES, ns), lambda j, i: (j, 0, 0)),
         pl.BlockSpec((1, ns, LANES), lambda j, i: (j, 0, 0)),
         pl.BlockSpec((1, 4, SUBLANES, ns), lambda j, i: (j, 0, 0, 0)),
         pl.BlockSpec((1, LANES), lambda j, i: (0, j))],
        [blk, acc, acc, pl.BlockSpec((1, SUBLANES, ns), lambda j, i: (j, 0, 0)),
         pl.BlockSpec((1, LANES), lambda j, i: (0, j))],
        [jax.ShapeDtypeStruct((L, D), F32), jax.ShapeDtypeStruct((nblk, LANES, ns), F32),
         jax.ShapeDtypeStruct((nblk, LANES, ns), F32), jax.ShapeDtypeStruct((nblk, SUBLANES, ns), F32),
         jax.ShapeDtypeStruct((1, D), F32)],
        [pltpu.VMEM((T, ns), F32), pltpu.VMEM((T + SUBLANES, ns), F32), pltpu.VMEM((1, ns), F32)],
        (dg, y, u, xs, xs, cmat_t, bmat_t, rtab, dvec), comm, ("parallel", "arbitrary"))


def _s5_discretize(a_re, a_im, log_step, b_re, b_im):
    lam_re = jnp.minimum(a_re, -S5_MIN_DECAY)
    lam_im = a_im
    delta = jnp.exp(log_step)[:, None]
    mag = jnp.exp(lam_re * delta)
    ang = lam_im * delta
    lb_re = mag * jnp.cos(ang)
    lb_im = mag * jnp.sin(ang)
    den = lam_re * lam_re + lam_im * lam_im
    nr = lb_re - 1.0
    ni = lb_im
    coef_re = (nr * lam_re + ni * lam_im) / den
    coef_im = (ni * lam_re - nr * lam_im) / den
    bb_re = coef_re[..., None] * b_re - coef_im[..., None] * b_im
    bb_im = coef_re[..., None] * b_im + coef_im[..., None] * b_re
    return lb_re, lb_im, bb_re, bb_im


def _s5_tables(lb_re, lb_im):
    nblk = lb_re.shape[0] // S5_BLOCK_GROUPS
    lr = lb_re.reshape(nblk, S5_BLOCK_STATES)
    li = lb_im.reshape(nblk, S5_BLOCK_STATES)
    pows = [(jnp.ones_like(lr), jnp.zeros_like(li))]
    for _ in range(SUBLANES):
        pr, pi = pows[-1]
        pows.append((pr * lr - pi * li, pr * li + pi * lr))
    rows = jnp.arange(SUBLANES)[None, :, None]

    def table(conj, reverse):
        sgn = -1.0 if conj else 1.0
        out = []
        for k in (1, 2, 4):
            mask = (rows <= SUBLANES - 1 - k) if reverse else (rows >= k)
            out.append(jnp.concatenate([jnp.where(mask, pows[k][0][:, None, :], 0.0),
                                        jnp.where(mask, sgn * pows[k][1][:, None, :], 0.0)], axis=-1))
        order = range(SUBLANES, 0, -1) if reverse else range(1, SUBLANES + 1)
        cre = jnp.stack([pows[k][0] for k in order], axis=1)
        cim = jnp.stack([sgn * pows[k][1] for k in order], axis=1)
        out.append(jnp.concatenate([cre, cim], axis=-1))
        return jnp.stack(out, axis=1)

    return table(False, False), table(True, True)


def _s5_block_mats(bb_re, bb_im, c_re, c_im):
    G = bb_re.shape[0]
    nblk = G // S5_BLOCK_GROUPS
    eye = jnp.eye(S5_BLOCK_GROUPS, dtype=F32)
    bb = jnp.stack([bb_re, bb_im]).reshape(2, nblk, S5_BLOCK_GROUPS, S5_STATE, S5_GROUP)
    bmat = jnp.einsum("ab,rjaph->jahrbp", eye, bb).reshape(nblk, LANES, 2 * S5_BLOCK_STATES)
    cc = jnp.stack([c_re, -c_im]).reshape(2, nblk, S5_BLOCK_GROUPS, S5_GROUP, S5_STATE)
    cmat = jnp.einsum("ab,rjahp->jrbpah", eye, cc).reshape(nblk, 2 * S5_BLOCK_STATES, LANES)
    return bmat, cmat


def _s5_unblock(dB, dC, dlam):
    nblk = dB.shape[0]
    G = nblk * S5_BLOCK_GROUPS
    d6 = dB.reshape(nblk, S5_BLOCK_GROUPS, S5_GROUP, 2, S5_BLOCK_GROUPS, S5_STATE)
    dbb = jnp.einsum("jahrap->rjaph", d6).reshape(2, G, S5_STATE, S5_GROUP)
    c6 = dC.reshape(nblk, S5_BLOCK_GROUPS, S5_GROUP, 2, S5_BLOCK_GROUPS, S5_STATE)
    dcc = jnp.einsum("jahrap->rjahp", c6).reshape(2, G, S5_GROUP, S5_STATE)
    dl = jnp.sum(dlam, axis=1).reshape(nblk, 2, S5_BLOCK_GROUPS, S5_STATE)
    dl = jnp.transpose(dl, (1, 0, 2, 3)).reshape(2, G, S5_STATE)
    return dbb[0], dbb[1], dcc[0], -dcc[1], dl[0], dl[1]


def loss_and_grad(y, target, name):
    L, D = y.shape
    tr = _div_tile(L, (512, 256, 128))

    def body(y_ref, t_ref, dy_ref, loss_ref):
        err = y_ref[...] - t_ref[...]
        dy_ref[...] = err * (1.0 / D)
        part = 0.5 * jnp.sum(jnp.mean(err * err, axis=-1, keepdims=True), axis=0, keepdims=True)

        @pl.when(pl.program_id(0) == 0)
        def _():
            loss_ref[...] = part

        @pl.when(pl.program_id(0) > 0)
        def _():
            loss_ref[...] += part

    row = pl.BlockSpec((tr, D), lambda i: (i, 0))
    return pl.pallas_call(
        body, name=name, grid=(L // tr,), in_specs=[row, row],
        out_specs=[row, pl.BlockSpec((1, 1), lambda i: (0, 0))],
        out_shape=[jax.ShapeDtypeStruct((L, D), F32), jax.ShapeDtypeStruct((1, 1), F32)],
        compiler_params=_params(("arbitrary",)),
    )(y, target)


def _adam_math(w, g, m, v):
    m = ADAM_B1 * m + (1.0 - ADAM_B1) * g
    v = ADAM_B2 * v + (1.0 - ADAM_B2) * (g * g)
    m_hat = m / (1.0 - ADAM_B1 ** ADAM_STEP)
    v_hat = v / (1.0 - ADAM_B2 ** ADAM_STEP)
    delta = -ADAM_LR * (m_hat / (jnp.sqrt(v_hat) + ADAM_EPS) + ADAM_WD * w)
    return delta, m, v


def _as3d(a):
    return a.reshape((-1,) + a.shape[-2:])


def adamw(w, g, m, v, name):
    shape = w.shape
    w3, g3, m3, v3 = _as3d(w), _as3d(g), _as3d(m), _as3d(v)
    A, R, C = w3.shape
    tr = _row_tile(R, C)

    def body(w_ref, g_ref, m_ref, v_ref, d_ref, mo_ref, vo_ref):
        d, mn, vn = _adam_math(w_ref[...], g_ref[...], m_ref[...], v_ref[...])
        d_ref[...] = d
        mo_ref[...] = mn
        vo_ref[...] = vn

    spec = pl.BlockSpec((1, tr, C), lambda a, i: (a, i, 0))
    outs = pl.pallas_call(
        body, name=name, grid=(A, R // tr), in_specs=[spec] * 4, out_specs=[spec] * 3,
        out_shape=[jax.ShapeDtypeStruct((A, R, C), F32)] * 3,
        compiler_params=_params(("parallel", "parallel")),
    )(w3, g3, m3, v3)
    return [o.reshape(shape) for o in outs]


def _place():
    x, y, c = lax.axis_index("x"), lax.axis_index("y"), lax.axis_index("c")
    other_chips = [(1 - x, y), (x, 1 - y), (1 - x, 1 - y)]
    return x, y, c, other_chips


def _chip_id(chip):
    return 2 * chip[0] + chip[1]


def _my_chip():
    return 2 * lax.axis_index("x") + lax.axis_index("y")


def _remote(src, dst, send_sem, recv_sem, dev):
    return pltpu.make_async_remote_copy(src_ref=src, dst_ref=dst, send_sem=send_sem, recv_sem=recv_sem,
                                        device_id=dev, device_id_type=MESH)


def allgather_small(arrs):
    T = len(arrs)

    def body(*refs):
        ins, outs = refs[:T], refs[T:2 * T]
        send_sems, recv_sems = refs[2 * T:]
        x, y, c, chips = _place()
        me = _chip_id((x, y))
        sends = []
        for t in range(T):
            outs[t][me] = ins[t][...]
            for j, chip in enumerate(chips):
                cp = _remote(ins[t], outs[t].at[me], send_sems.at[3 * t + j], recv_sems.at[3 * t + j], (*chip, c))
                cp.start()
                sends.append(cp)
        for t in range(T):
            for j, chip in enumerate(chips):
                slot = outs[t].at[_chip_id(chip)]
                _remote(slot, slot, send_sems.at[3 * t + j], recv_sems.at[3 * t + j], (*chip, c)).wait_recv()
        for cp in sends:
            cp.wait_send()

    return pl.pallas_call(
        body, name="allgather_small", in_specs=[VMEM_SPEC] * T, out_specs=[VMEM_SPEC] * T,
        out_shape=[jax.ShapeDtypeStruct((4,) + a.shape, a.dtype) for a in arrs],
        scratch_shapes=[pltpu.SemaphoreType.DMA((3 * T,)), pltpu.SemaphoreType.DMA((3 * T,))],
        compiler_params=pltpu.CompilerParams(vmem_limit_bytes=VMEM_LIMIT, has_side_effects=True),
    )(*arrs)


def allreduce_small(buf):
    R, C = buf.shape

    def body(in_ref, out_ref, pair_ref, all_ref, send_sems, recv_sems):
        x, y, c, chips = _place()
        me, sibling = _chip_id((x, y)), (x, y, 1 - c)
        swap = _remote(in_ref, pair_ref, send_sems.at[0], recv_sems.at[0], sibling)
        swap.start()
        swap.wait()
        all_ref[me] = in_ref[...] + pair_ref[...]
        sends = []
        for j, chip in enumerate(chips):
            cp = _remote(all_ref.at[me], all_ref.at[me], send_sems.at[1 + j], recv_sems.at[1 + j], (*chip, c))
            cp.start()
            sends.append(cp)
        for j, chip in enumerate(chips):
            slot = all_ref.at[_chip_id(chip)]
            _remote(slot, slot, send_sems.at[1 + j], recv_sems.at[1 + j], (*chip, c)).wait_recv()
        for cp in sends:
            cp.wait_send()
        out_ref[...] = ((all_ref[0] + all_ref[1]) + all_ref[2]) + all_ref[3]

    return pl.pallas_call(
        body, name="allreduce_small", in_specs=[VMEM_SPEC], out_specs=VMEM_SPEC,
        out_shape=jax.ShapeDtypeStruct((R, C), F32),
        scratch_shapes=[pltpu.VMEM((R, C), F32), pltpu.VMEM((4, R, C), F32),
                        pltpu.SemaphoreType.DMA((4,)), pltpu.SemaphoreType.DMA((4,))],
        compiler_params=pltpu.CompilerParams(vmem_limit_bytes=VMEM_LIMIT, has_side_effects=True),
    )(buf)


def _half_rows(ref, layout, shard, half):
    if layout == "S":
        hr = ref.shape[1] // 2
        return ref.at[shard, pl.ds(pl.multiple_of(half * hr, 16), hr), :]
    hr, C = ref.shape[0] // 2, ref.shape[1] // 4
    return ref.at[pl.ds(pl.multiple_of(half * hr, 16), hr), pl.ds(pl.multiple_of(shard * C, LANES), C)]


def _half_rows_all(ref, layout, half):
    if layout == "S":
        hr = ref.shape[1] // 2
        return ref.at[:, pl.ds(pl.multiple_of(half * hr, 16), hr), :]
    hr = ref.shape[0] // 2
    return ref.at[pl.ds(pl.multiple_of(half * hr, 16), hr), :]


def _shard_of_half(ref, layout, shard):
    if layout == "S":
        return ref.at[shard]
    C = ref.shape[1] // 4
    return ref.at[:, pl.ds(pl.multiple_of(shard * C, LANES), C)]


def _own_block_spec(layout, tr, C):
    if layout == "S":
        return pl.BlockSpec((None, tr, C), lambda i: (_my_chip(), i, 0))
    return pl.BlockSpec((tr, C), lambda i: (i, _my_chip()))


def cast_into_gathered(shards, layer, layout, name):
    _, R, C = shards.shape
    tr = _row_tile(R, C)

    def body(a_ref, o_ref):
        o_ref[...] = a_ref[...].astype(BF16)

    return pl.pallas_call(
        body, name=name, grid=(R // tr,),
        in_specs=[pl.BlockSpec((None, tr, C), lambda i: (layer, i, 0))],
        out_specs=_own_block_spec(layout, tr, C),
        out_shape=jax.ShapeDtypeStruct((4, R, C) if layout == "S" else (R, 4 * C), BF16),
        compiler_params=_params(("parallel",)),
    )(shards)


def gather_comm(bufs, layouts):
    T = len(bufs)

    def begin(_, outs, send_sems, recv_sems):
        x, y, c, chips = _place()
        for t in range(T):
            mine = _half_rows(outs[t], layouts[t], _chip_id((x, y)), c)
            for j, chip in enumerate(chips):
                _remote(mine, mine, send_sems.at[6 * t + j], recv_sems.at[6 * t + j], (*chip, c)).start()

    def middle(_, outs, send_sems, recv_sems):
        x, y, c, chips = _place()
        for t in range(T):
            for j, chip in enumerate(chips):
                piece = _half_rows(outs[t], layouts[t], _chip_id(chip), c)
                _remote(piece, piece, send_sems.at[6 * t + j], recv_sems.at[6 * t + j], (*chip, c)).wait_recv()
                _remote(piece, piece, send_sems.at[6 * t + 3 + j], recv_sems.at[6 * t + 3 + j], (x, y, 1 - c)).start()

    def finish(_, outs, send_sems, recv_sems):
        x, y, c, chips = _place()
        for t in range(T):
            mine = _half_rows(outs[t], layouts[t], _chip_id((x, y)), c)
            for j, chip in enumerate(chips):
                theirs = _half_rows(outs[t], layouts[t], _chip_id(chip), 1 - c)
                _remote(theirs, theirs, send_sems.at[6 * t + 3 + j], recv_sems.at[6 * t + 3 + j],
                        (x, y, 1 - c)).wait_recv()
                _remote(mine, mine, send_sems.at[6 * t + j], recv_sems.at[6 * t + j], (*chip, c)).wait_send()
                piece = _half_rows(outs[t], layouts[t], _chip_id(chip), c)
                _remote(piece, piece, send_sems.at[6 * t + 3 + j], recv_sems.at[6 * t + 3 + j],
                        (x, y, 1 - c)).wait_send()

    return Comm(bufs, [jax.ShapeDtypeStruct(b.shape, b.dtype) for b in bufs], {t: t for t in range(T)}, 6 * T,
                begin, middle, finish, middle_frac=0.75)


def swap_comm(grads, layouts):
    T = len(grads)

    def out_shape(g, layout):
        return (4, g.shape[1] // 2, g.shape[2]) if layout == "S" else (g.shape[0] // 2, g.shape[1])

    def copies(ins, outs, send_sems, recv_sems):
        x, y, c, _ = _place()
        return [_remote(_half_rows_all(ins[t], layouts[t], 1 - c), outs[t], send_sems.at[t], recv_sems.at[t],
                        (x, y, 1 - c)) for t in range(T)]

    def begin(*refs):
        for cp in copies(*refs):
            cp.start()

    def finish(*refs):
        for cp in copies(*refs):
            cp.wait()

    return Comm(grads, [jax.ShapeDtypeStruct(out_shape(g, k), F32) for g, k in zip(grads, layouts)], {}, T,
                begin, None, finish)


def pair_sum(grad, recv, layout, name):
    if layout == "S":
        _, hr, C = recv.shape
        tr = _row_tile(hr, C)
        nb = hr // tr
        grid = (4, nb)
        g_spec = pl.BlockSpec((None, tr, C), lambda a, i: (a, lax.axis_index("c") * nb + i, 0))
        spec = pl.BlockSpec((None, tr, C), lambda a, i: (a, i, 0))
    else:
        hr, C = recv.shape
        tr = _row_tile(hr, C)
        nb = hr // tr
        grid = (nb,)
        g_spec = pl.BlockSpec((tr, C), lambda i: (lax.axis_index("c") * nb + i, 0))
        spec = pl.BlockSpec((tr, C), lambda i: (i, 0))

    def body(g_ref, r_ref, f_ref, b_ref):
        s = g_ref[...] + r_ref[...]
        f_ref[...] = s
        b_ref[...] = s.astype(BF16)

    return pl.pallas_call(
        body, name=name, grid=grid, in_specs=[g_spec, spec], out_specs=[spec, spec],
        out_shape=[jax.ShapeDtypeStruct(recv.shape, F32), jax.ShapeDtypeStruct(recv.shape, BF16)],
        compiler_params=_params(("parallel",) * len(grid)),
    )(grad, recv)


def exchange_comm(pair_bf16, layouts):
    T = len(pair_bf16)

    def out_shape(p, layout):
        return (3,) + ((p.shape[1], p.shape[2]) if layout == "S" else (p.shape[0], p.shape[1] // 4))

    def copies(ins, outs, send_sems, recv_sems):
        x, y, c, chips = _place()
        return [_remote(_shard_of_half(ins[t], layouts[t], _chip_id(chip)), outs[t].at[j],
                        send_sems.at[3 * t + j], recv_sems.at[3 * t + j], (*chip, c))
                for t in range(T) for j, chip in enumerate(chips)]

    def begin(*refs):
        for cp in copies(*refs):
            cp.start()

    def finish(*refs):
        for cp in copies(*refs):
            cp.wait()

    return Comm(pair_bf16, [jax.ShapeDtypeStruct(out_shape(p, k), BF16) for p, k in zip(pair_bf16, layouts)], {},
                3 * T, begin, None, finish)


def reduce_total(pair_f32, got, layout, layer, n_layers, previous, name):
    _, hr, C = got.shape
    tr = _row_tile(hr, C)
    nb = hr // tr

    def body(*refs):
        p_ref, g_ref, t_ref = refs[0], refs[1], refs[-1]
        t_ref[...] = ((p_ref[...] + g_ref[0].astype(F32)) + g_ref[1].astype(F32)) + g_ref[2].astype(F32)

    args = [pair_f32, got] + ([previous] if previous is not None else [])
    return pl.pallas_call(
        body, name=name, grid=(nb,),
        in_specs=[_own_block_spec(layout, tr, C), pl.BlockSpec((3, tr, C), lambda i: (0, i, 0))]
        + ([ANY] if previous is not None else []),
        out_specs=pl.BlockSpec((None, tr, C), lambda i: (layer, lax.axis_index("c") * nb + i, 0)),
        out_shape=jax.ShapeDtypeStruct((n_layers, 2 * hr, C), F32),
        input_output_aliases={2: 0} if previous is not None else {},
        compiler_params=_params(("parallel",)),
    )(*args)


def share_comm(reduced):
    T = len(reduced)

    def halves(outs, half):
        return [o.at[:, pl.ds(pl.multiple_of(half * (o.shape[1] // 2), 8), o.shape[1] // 2), :] for o in outs]

    def begin(_, outs, send_sems, recv_sems):
        x, y, c, _p = _place()
        for t, mine in enumerate(halves(outs, c)):
            _remote(mine, mine, send_sems.at[t], recv_sems.at[t], (x, y, 1 - c)).start()

    def finish(_, outs, send_sems, recv_sems):
        x, y, c, _p = _place()
        for t, (mine, theirs) in enumerate(zip(halves(outs, c), halves(outs, 1 - c))):
            _remote(mine, mine, send_sems.at[t], recv_sems.at[t], (x, y, 1 - c)).wait_send()
            _remote(theirs, theirs, send_sems.at[t], recv_sems.at[t], (x, y, 1 - c)).wait_recv()

    return Comm(reduced, [jax.ShapeDtypeStruct(r.shape, r.dtype) for r in reduced], {t: t for t in range(T)}, T,
                begin, None, finish)


def _round_up(n, m):
    return (n + m - 1) // m * m


def _heads_col(a16):
    L = a16.shape[0]
    return jnp.transpose(a16.reshape(L, HEADS // 2, 2), (1, 0, 2))


def _heads_row(a16):
    L = a16.shape[0]
    return jnp.transpose(a16.reshape(L, HEADS // 2, 2), (1, 2, 0))


def local_step(x, target, meta, norm_g, b_f, conv_w, conv_b, s5, s5_d, stager):
    S, D = x.shape
    depth = norm_g.shape[0]
    n_even, n_odd = b_f.shape[0], s5_d.shape[0]
    L = N_META + S
    Lp = _round_up(L, ROW_TILE)
    h = jnp.concatenate([meta, x, jnp.zeros((Lp - L, D), F32)], axis=0)
    conv_c = conv_w.shape[2]
    fg_block = 3 * conv_c // LANES
    saved = []

    def riding(tag, fn, *args):
        comm = stager.ride(tag)
        if comm is None and fn is matmul:
            return fn(*args, name=tag)
        outs, arrived = fn(*args, name=tag, comm=comm)
        stager.arrived(tag, arrived)
        return outs[0] if fn is matmul else outs

    for i in range(depth):
        g = norm_g[i]
        j = i // 2
        tag = "l%d_" % i
        w = stager.weights(i)
        st = {"h0": h, "w": w}
        if i % 2 == 0:
            u = rmsnorm_fwd(h, g[0:1], BF16, tag + "norm0")
            qkv = matmul([(u, w["w_qkv"])], "nn", BF16, tag + "qkv")
            rest = matmul([(u, w["w_rest"])], "nn", F32, tag + "rest")
            cgate = gate_fwd(rest, fg_block, b_f[j], tag + "gate")
            c16 = cgate[:, :HEADS]
            attn, lse = riding(tag + "attn", attn_fwd, qkv, _heads_col(c16), _heads_row(c16))
            convo = conv_fwd(rest, 0, conv_w[j], conv_b[j], tag + "conv")
            cat = jnp.concatenate([attn, convo], axis=1)
            m = matmul([(cat, w["w_o"])], "nn", F32, tag + "wo")
            st.update(u=u, qkv=qkv, rest=rest, c16=c16, lse=lse, cat=cat)
        else:
            p = s5[j]
            u = rmsnorm_fwd(h, g[0:1], F32, tag + "norm0")
            y, gact, xs = riding(tag + "s5", s5_fwd, u, p["bmat"], p["cmat"], p["tab"], s5_d[j])
            o1, o2, m = riding(tag + "glu", dual_matmul_act, gact, w["w_glu1"], w["w_glu2"], "glu", F32)
            st.update(u=u, y=y, gact=gact, xs=xs, o1=o1, o2=o2)
        h1 = rmsnorm_fwd(m, g[1:2], F32, tag + "norm1", residual=h)
        u2 = rmsnorm_fwd(h1, g[2:3], BF16, tag + "norm2")
        a, b, hact = riding(tag + "ffn_in", dual_matmul_act, u2, w["w_gate"], w["w_up"], "swiglu", BF16)
        f = riding(tag + "ffn_out", matmul, [(hact, w["w_down"])], "nn", F32)
        h = rmsnorm_fwd(f, g[3:4], F32, tag + "norm3", residual=h1)
        st.update(m=m, h1=h1, u2=u2, a=a, b=b, hact=hact, f=f)
        saved.append(st)

    dy, loss = loss_and_grad(h[N_META:L], target, "loss")
    dh = jnp.concatenate([jnp.zeros((N_META, D), F32), dy, jnp.zeros((Lp - L, D), F32)], axis=0)

    grads = {k: [None] * n_even for k in ("b_f", "conv_w", "conv_b")}
    grads.update({k: [None] * n_odd for k in ("s5_d", "s5_dB", "s5_dC", "s5_dlam")})
    grads["norm_g"] = [None] * depth

    for i in reversed(range(depth)):
        g = norm_g[i]
        j = i // 2
        tag = "l%d_b_" % i
        st = saved[i]
        w = st["w"]
        wg = {}
        df, dg3 = rmsnorm_bwd(st["f"], g[3:4], dh, BF16, tag + "norm3")
        wg["w_down"] = matmul([(st["hact"], df)], "tn", F32, tag + "dw_down")
        da, db = riding(tag + "ffn_act", ffn_bwd_act, df, w["w_down"], st["a"], st["b"])
        wg["w_gate"] = matmul([(st["u2"], da)], "tn", F32, tag + "dw_gate")
        wg["w_up"] = matmul([(st["u2"], db)], "tn", F32, tag + "dw_up")
        stager.put_grads(i, wg)
        wg = {}
        du2 = riding(tag + "du2", matmul, [(da, w["w_gate"]), (db, w["w_up"])], "nt", F32)
        dh1, dg2 = rmsnorm_bwd(st["h1"], g[2:3], du2, F32, tag + "norm2", add=dh)
        if i % 2 == 0:
            dm, dg1 = rmsnorm_bwd(st["m"], g[1:2], dh1, BF16, tag + "norm1")
            wg["w_o"] = matmul([(st["cat"], dm)], "tn", F32, tag + "dw_o")
            dcat = matmul([(dm, w["w_o"])], "nt", F32, tag + "dcat")
            delta = attn_delta(dcat, st["cat"], tag + "delta")
            c16 = st["c16"]
            lse16 = jnp.transpose(st["lse"], (1, 0, 2)).reshape(Lp, HEADS)
            dq, dk, dv, dcq, dck = riding(tag + "attn", attn_bwd, st["qkv"], dcat, _heads_row(lse16),
                                          _heads_row(delta[:, :HEADS]), _heads_row(c16), _heads_col(c16))
            dc16 = (jnp.transpose(dcq, (2, 0, 1)).reshape(Lp, HEADS)
                    + jnp.transpose(dck, (1, 0, 2)).reshape(Lp, HEADS))
            dc = jnp.pad(dc16, ((0, 0), (0, LANES - HEADS)))
            dfg, dbf = gate_bwd(st["rest"], fg_block, b_f[j], dc, tag + "gate")
            dgb, dgc, dxc, dcw, dcb = conv_bwd(st["rest"], 0, conv_w[j], conv_b[j], dcat, ATTN_W // LANES,
                                               tag + "conv")
            dqkv = jnp.concatenate([dq, dk, dv], axis=1)
            drest = jnp.concatenate([dgb, dgc, dxc, dfg.astype(BF16)], axis=1)
            wg["w_qkv"] = matmul([(st["u"], dqkv)], "tn", F32, tag + "dw_qkv")
            wg["w_rest"] = matmul([(st["u"], drest)], "tn", F32, tag + "dw_rest")
            du = matmul([(dqkv, w["w_qkv"])], "nt", F32, tag + "du_qkv")
            du_b = matmul([(drest, w["w_rest"])], "nt", F32, tag + "du_rest")
            grads["b_f"][j], grads["conv_w"][j], grads["conv_b"][j] = dbf, dcw, dcb
        else:
            p = s5[j]
            dmix, dg1 = rmsnorm_bwd(st["m"], g[1:2], dh1, F32, tag + "norm1")
            do1, do2 = glu_bwd_act(dmix, st["o1"], st["o2"], tag + "glu_act")
            wg["w_glu1"] = matmul([(st["gact"], do1)], "tn", F32, tag + "dw_glu1")
            wg["w_glu2"] = matmul([(st["gact"], do2)], "tn", F32, tag + "dw_glu2")
            dgact = matmul([(do1, w["w_glu1"]), (do2, w["w_glu2"])], "nt", F32, tag + "dgact")
            du, dC, dB, dlam, dd = riding(tag + "s5", s5_bwd, dgact, st["y"], st["u"], st["xs"], p["cmat_t"],
                                          p["bmat_t"], p["rtab"], s5_d[j])
            du_b = None
            grads["s5_dB"][j], grads["s5_dC"][j], grads["s5_dlam"][j], grads["s5_d"][j] = dB, dC, dlam, dd
        dh, dg0 = rmsnorm_bwd(st["h0"], g[0:1], du, F32, tag + "norm0", add=dh1, dy2=du_b)
        grads["norm_g"][i] = jnp.concatenate([dg0, dg1, dg2, dg3], axis=0)
        stager.put_grads(i, wg)

    grads["meta"] = dh[:N_META]
    return loss, dh[N_META:L], grads


def _packed_rows(shape):
    return _round_up(_round_up(math.prod(shape), LANES) // LANES, SUBLANES)


def _pack(arrs):
    rows = []
    for a in arrs:
        flat = a.reshape(-1).astype(F32)
        r = _packed_rows(a.shape)
        rows.append(jnp.pad(flat, (0, r * LANES - flat.shape[0])).reshape(r, LANES))
    return jnp.concatenate(rows, axis=0)


def _unpack(buf, shapes):
    buf = buf.reshape(-1, LANES)
    out, off = [], 0
    for s in shapes:
        r = _packed_rows(s)
        out.append(buf[off:off + r].reshape(-1)[:math.prod(s)].reshape(s))
        off += r
    return out


class _LayerWeights:
    def __init__(self, stager, layer):
        self.stager, self.layer = stager, layer

    def __getitem__(self, name):
        return self.stager.weight(self.layer, name)


class MeshStager:
    LAYOUT = {"ab_w_in": "S", "ab_w_o": "S", "s5_w_glu1": "S", "s5_w_glu2": "S",
              "ffn_w_gate": "C", "ffn_w_up": "C", "ffn_w_down": "S"}
    EVEN = ("ab_w_in", "ab_w_o", "ffn_w_gate", "ffn_w_up", "ffn_w_down")
    ODD = ("s5_w_glu1", "s5_w_glu2", "ffn_w_gate", "ffn_w_up", "ffn_w_down")

    def __init__(self, shards):
        self.shards = shards
        self.depth = depth = shards["ffn_w_gate"].shape[0]
        self.bufs = {}
        for i in range(depth):
            for k in self.keys(i):
                self.bufs[k, i] = cast_into_gathered(shards[k], self.index(k, i), self.LAYOUT[k],
                                                     "cast_%s_%d" % (k, i))
        self.grads, self.pairs, self.reduced, self.ready = {}, {}, {}, {}
        first = [("ab_w_in", 0)]
        plan = self.gather_plan = {"l0_attn": [it for it in self.stage(0) if it not in first]}
        for o in range(1, depth, 2):
            e = o - 1
            plan.setdefault("l%d_attn" % e, []).append(("ffn_w_gate", o))
            plan["l%d_ffn_in" % e] = [("s5_w_glu1", o), ("ffn_w_up", o)]
            plan["l%d_ffn_out" % e] = [("s5_w_glu2", o)]
            plan["l%d_s5" % o] = [("ffn_w_down", o)] + (self.mix(o + 1) if o + 1 < depth else [])
            if o + 1 < depth:
                plan["l%d_ffn_in" % o] = [("ffn_w_gate", o + 1)]
                plan["l%d_attn" % (o + 1)] = [("ffn_w_up", o + 1), ("ffn_w_down", o + 1)]
        self.swap_plan, self.exchange_plan = {}, {}
        for i in range(depth):
            above = self.mix(i + 1) if i + 1 < depth else []
            if above:
                self.swap_plan["l%d_b_ffn_act" % i] = above
            self.swap_plan["l%d_b_du2" % i] = self.ffn(i)
            self.exchange_plan["l%d_b_%s" % (i, "attn" if i % 2 == 0 else "s5")] = above + self.ffn(i)
        self._store(first, comm_call("gather_first", self._gather(first, 0.5)))

    def keys(self, i):
        return self.EVEN if i % 2 == 0 else self.ODD

    def stage(self, i):
        return [(k, i) for k in self.keys(i)]

    def mix(self, i):
        return [(k, i) for k in self.keys(i) if not k.startswith("ffn")]

    @staticmethod
    def ffn(i):
        return [("ffn_w_gate", i), ("ffn_w_up", i), ("ffn_w_down", i)]

    @staticmethod
    def index(key, i):
        return i if key.startswith("ffn") else i // 2

    def _layouts(self, items):
        return [self.LAYOUT[k] for k, _ in items]

    def _gather(self, items, middle_frac):
        comm = gather_comm([self.bufs[it] for it in items], self._layouts(items))
        comm.middle_frac = middle_frac
        return comm

    def _store(self, items, bufs):
        for it, b in zip(items, bufs):
            self.bufs[it] = b

    def ride(self, tag):
        if tag in self.gather_plan:
            return self._gather(self.gather_plan[tag], 0.85 if tag == "l0_attn" else 0.7)
        if tag in self.swap_plan:
            items = self.swap_plan[tag]
            return swap_comm([self.grads[it] for it in items], self._layouts(items))
        if tag in self.exchange_plan:
            items = self.exchange_plan[tag]
            return exchange_comm([self.pairs[it][1] for it in items], self._layouts(items))
        return None

    def arrived(self, tag, outs):
        if tag in self.gather_plan:
            self._store(self.gather_plan[tag], outs)
        elif tag in self.swap_plan:
            self._pair_sums(self.swap_plan[tag], outs)
        elif tag in self.exchange_plan:
            self._totals(self.exchange_plan[tag], outs)

    def _pair_sums(self, items, received):
        for it, r in zip(items, received):
            self.pairs[it] = pair_sum(self.grads[it], r, self.LAYOUT[it[0]], "pair_sum_%s_%d" % it)

    def _totals(self, items, got):
        for it, g in zip(items, got):
            k, i = it
            self.reduced[k] = reduce_total(self.pairs[it][0], g, self.LAYOUT[k], self.index(k, i),
                                           self.shards[k].shape[0], self.reduced.get(k), "reduce_total_%s_%d" % it)

    def weights(self, i):
        return _LayerWeights(self, i)

    def weight(self, i, name):
        if (name, i) not in self.ready:
            if name in ("w_qkv", "w_rest"):
                b = self.bufs["ab_w_in", i]
                w_in = jnp.transpose(b, (1, 0, 2)).reshape(b.shape[1], 4 * b.shape[2])
                self.ready["w_qkv", i], self.ready["w_rest", i] = split_w_in(w_in)
            else:
                k = {"w_o": "ab_w_o", "w_glu1": "s5_w_glu1", "w_glu2": "s5_w_glu2"}.get(name, "ffn_" + name)
                b = self.bufs[k, i]
                self.ready[name, i] = b.reshape(4 * b.shape[1], b.shape[2]) if self.LAYOUT[k] == "S" else b
        return self.ready[name, i]

    def put_grads(self, i, wg):
        for k in self.keys(i):
            _, R, C = self.shards[k].shape
            name = {"ab_w_in": "w_qkv", "ab_w_o": "w_o", "s5_w_glu1": "w_glu1", "s5_w_glu2": "w_glu2"}.get(k, k[4:])
            if name not in wg:
                continue
            if k == "ab_w_in":
                dw = merge_dw_in(wg["w_qkv"], wg["w_rest"])
                self.grads[k, i] = jnp.transpose(dw.reshape(R, 4, C), (1, 0, 2))
            else:
                self.grads[k, i] = wg[name].reshape(4, R, C) if self.LAYOUT[k] == "S" else wg[name]

    def finish(self):
        last = self.mix(0)
        self._pair_sums(last, comm_call("swap_last", swap_comm([self.grads[it] for it in last], self._layouts(last))))
        self._totals(last, comm_call("exchange_last", exchange_comm([self.pairs[it][1] for it in last],
                                                                    self._layouts(last))))
        names = list(self.LAYOUT)
        return dict(zip(names, comm_call("share_reduced", share_comm([self.reduced[k] for k in names]))))


def split_w_in(w_in):
    fg0 = 3 * ATTN_W
    w_rest = jnp.concatenate([w_in[:, fg0 + HEADS:], w_in[:, fg0:fg0 + HEADS],
                              jnp.zeros((w_in.shape[0], LANES - HEADS), w_in.dtype)], axis=1)
    return w_in[:, :fg0], w_rest


def merge_dw_in(dw_qkv, dw_rest):
    nqc = dw_rest.shape[1] - LANES
    return jnp.concatenate([dw_qkv, dw_rest[:, nqc:nqc + HEADS], dw_rest[:, :nqc]], axis=1)


def device_step(x, target, P, stager):
    D = x.shape[-1]
    n_even, n_odd = P["ab_b_f"].shape[0], P["s5_d"].shape[0]
    conv_c = P["ab_conv_b"].shape[1]
    b_f_pad = jnp.pad(P["ab_b_f"], ((0, 0), (0, LANES - HEADS))).reshape(n_even, 1, LANES)

    s5, s5_vjps = [], []
    for j in range(n_odd):
        disc, vjp = jax.vjp(_s5_discretize, P["s5_a_re"][j], P["s5_a_im"][j], P["s5_log_step"][j],
                            P["s5_b_re"][j], P["s5_b_im"][j])
        lb_re, lb_im, bb_re, bb_im = disc
        tab, rtab = _s5_tables(lb_re, lb_im)
        bmat, cmat = _s5_block_mats(bb_re, bb_im, P["s5_c_re"][j], P["s5_c_im"][j])
        s5.append(dict(tab=tab, rtab=rtab, bmat=bmat.astype(BF16), cmat=cmat.astype(BF16),
                       bmat_t=jnp.transpose(bmat, (0, 2, 1)).astype(BF16),
                       cmat_t=jnp.transpose(cmat, (0, 2, 1)).astype(BF16)))
        s5_vjps.append(vjp)

    loss, grad_x, G = local_step(
        x, target, P["meta_tokens"], P["norm_g"], b_f_pad, P["ab_conv_w"],
        P["ab_conv_b"].reshape(n_even, 1, conv_c), s5, P["s5_d"].reshape(n_odd, 1, D), stager)

    out = {
        "meta_tokens": G["meta"],
        "norm_g": jnp.stack(G["norm_g"]),
        "ab_b_f": jnp.stack([b[0, :HEADS] for b in G["b_f"]]),
        "ab_conv_w": jnp.stack(G["conv_w"]),
        "ab_conv_b": jnp.stack([b[0] for b in G["conv_b"]]),
        "s5_d": jnp.stack([d[0] for d in G["s5_d"]]),
    }
    s5g = {k: [] for k in ("s5_a_re", "s5_a_im", "s5_log_step", "s5_b_re", "s5_b_im", "s5_c_re", "s5_c_im")}
    for j in range(n_odd):
        dbb_re, dbb_im, dc_re, dc_im, dl_re, dl_im = _s5_unblock(G["s5_dB"][j], G["s5_dC"][j], G["s5_dlam"][j])
        da_re, da_im, dls, db_re, db_im = s5_vjps[j]((dl_re, dl_im, dbb_re, dbb_im))
        for k, val in zip(s5g, (da_re, da_im, dls, db_re, db_im, dc_re, dc_im)):
            s5g[k].append(val)
    out.update({k: jnp.stack(v) for k, v in s5g.items()})
    return loss, grad_x, out


def kernel(x, meta_tokens, norm_g, ab_w_in, ab_b_f, ab_conv_w, ab_conv_b, ab_w_o, s5_a_re, s5_a_im, s5_log_step, s5_b_re, s5_b_im, s5_c_re, s5_c_im, s5_d, s5_w_glu1, s5_w_glu2, ffn_w_gate, ffn_w_up, ffn_w_down, loss_target, m_meta_tokens, m_norm_g, m_ab_w_in, m_ab_b_f, m_ab_conv_w, m_ab_conv_b, m_ab_w_o, m_s5_a_re, m_s5_a_im, m_s5_log_step, m_s5_b_re, m_s5_b_im, m_s5_c_re, m_s5_c_im, m_s5_d, m_s5_w_glu1, m_s5_w_glu2, m_ffn_w_gate, m_ffn_w_up, m_ffn_w_down, v_meta_tokens, v_norm_g, v_ab_w_in, v_ab_b_f, v_ab_conv_w, v_ab_conv_b, v_ab_w_o, v_s5_a_re, v_s5_a_im, v_s5_log_step, v_s5_b_re, v_s5_b_im, v_s5_c_re, v_s5_c_im, v_s5_d, v_s5_w_glu1, v_s5_w_glu2, v_ffn_w_gate, v_ffn_w_up, v_ffn_w_down):
    names = ["meta_tokens", "norm_g", "ab_w_in", "ab_b_f", "ab_conv_w", "ab_conv_b", "ab_w_o", "s5_a_re", "s5_a_im",
             "s5_log_step", "s5_b_re", "s5_b_im", "s5_c_re", "s5_c_im", "s5_d", "s5_w_glu1", "s5_w_glu2",
             "ffn_w_gate", "ffn_w_up", "ffn_w_down"]
    W = dict(zip(names, [meta_tokens, norm_g, ab_w_in, ab_b_f, ab_conv_w, ab_conv_b, ab_w_o, s5_a_re, s5_a_im,
                         s5_log_step, s5_b_re, s5_b_im, s5_c_re, s5_c_im, s5_d, s5_w_glu1, s5_w_glu2,
                         ffn_w_gate, ffn_w_up, ffn_w_down]))
    Mo = dict(zip(names, [m_meta_tokens, m_norm_g, m_ab_w_in, m_ab_b_f, m_ab_conv_w, m_ab_conv_b, m_ab_w_o, m_s5_a_re,
                          m_s5_a_im, m_s5_log_step, m_s5_b_re, m_s5_b_im, m_s5_c_re, m_s5_c_im, m_s5_d, m_s5_w_glu1,
                          m_s5_w_glu2, m_ffn_w_gate, m_ffn_w_up, m_ffn_w_down]))
    Vo = dict(zip(names, [v_meta_tokens, v_norm_g, v_ab_w_in, v_ab_b_f, v_ab_conv_w, v_ab_conv_b, v_ab_w_o, v_s5_a_re,
                          v_s5_a_im, v_s5_log_step, v_s5_b_re, v_s5_b_im, v_s5_c_re, v_s5_c_im, v_s5_d, v_s5_w_glu1,
                          v_s5_w_glu2, v_ffn_w_gate, v_ffn_w_up, v_ffn_w_down]))
    D = x.shape[-1]
    n_even, n_odd, depth = ab_w_in.shape[0], s5_w_glu1.shape[0], ffn_w_gate.shape[0]
    chip = 2 * lax.axis_index("x") + lax.axis_index("y")

    big = list(MeshStager.LAYOUT)
    stager = MeshStager({k: W[k] for k in big})
    g_meta, g_norm, g_convw, g_s5d = allgather_small([meta_tokens, norm_g, ab_conv_w, s5_d])
    full = {k: W[k] for k in names if k not in big}
    full["meta_tokens"] = jnp.transpose(g_meta, (1, 0, 2)).reshape(N_META, D)
    full["norm_g"] = jnp.transpose(g_norm, (1, 2, 0, 3)).reshape(depth, 4, D)
    full["ab_conv_w"] = jnp.transpose(g_convw, (1, 2, 0, 3)).reshape(n_even, CONV_K, -1)
    full["s5_d"] = jnp.transpose(g_s5d, (1, 0, 2)).reshape(n_odd, D)

    loss, grad_x, G = device_step(x[0], loss_target[0], full, stager)
    reduced = stager.finish()

    small_w = [k for k in names if k not in big]
    small_names = ["loss"] + small_w
    G["loss"] = loss
    summed = dict(zip(small_names, _unpack(allreduce_small(_pack([G[k] for k in small_names])),
                                           [G[k].shape for k in small_names])))
    loss_out = summed["loss"].reshape(())
    for k in ("meta_tokens", "norm_g", "ab_conv_w", "s5_d"):
        n_last = W[k].shape[-1]
        summed[k] = lax.dynamic_slice_in_dim(summed[k], chip * n_last, n_last, axis=summed[k].ndim - 1)
    shapes = [W[k].shape for k in small_w]
    d_s, m_s, v_s = adamw(_pack([W[k] for k in small_w])[None], _pack([summed[k] for k in small_w])[None],
                          _pack([Mo[k] for k in small_w])[None], _pack([Vo[k] for k in small_w])[None], "adamw_small")
    delta = dict(zip(small_w, _unpack(d_s, shapes)))
    new_m = dict(zip(small_w, _unpack(m_s, shapes)))
    new_v = dict(zip(small_w, _unpack(v_s, shapes)))
    grad = {k: summed[k] for k in small_w}
    for k in big:
        grad[k] = reduced[k]
        delta[k], new_m[k], new_v[k] = adamw(W[k], reduced[k], Mo[k], Vo[k], "adamw_" + k)

    return (loss_out, grad_x[None], *[grad[k] for k in names], *[delta[k] for k in names],
            *[new_m[k] for k in names], *[new_v[k] for k in names])
```

```python
import functools
import math

import jax
import jax.numpy as jnp
from jax import lax
from jax.experimental import pallas as pl
from jax.experimental.pallas import tpu as pltpu

F32 = jnp.float32
BF16 = jnp.bfloat16

N_META = 16
HEADS = 16
HEAD_DIM = 64
ATTN_W = HEADS * HEAD_DIM
CONV_K = 3
S5_GROUP = 16
S5_STATE = 64
S5_MIN_DECAY = 1e-4
NORM_EPS = 1e-6
ADAM_LR = 0.001
ADAM_B1 = 0.9
ADAM_B2 = 0.999
ADAM_EPS = 1e-08
ADAM_WD = 0.01
ADAM_STEP = 10

LANES = 128
SUBLANES = 8
VMEM_LIMIT = 56 * 1024 * 1024
VMEM_TILE_BUDGET = 34 * 1024 * 1024
ROW_TILE = 384
ATTN_ROWS = 128
S5_BLOCK_GROUPS = LANES // S5_GROUP
S5_BLOCK_STATES = S5_BLOCK_GROUPS * S5_STATE
NEG_BIG = -1e30

MESH = pl.DeviceIdType.MESH
ANY = pl.BlockSpec(memory_space=pl.ANY)
VMEM_SPEC = pl.BlockSpec(memory_space=pltpu.VMEM)


def _params(sem=None):
    return pltpu.CompilerParams(dimension_semantics=sem, vmem_limit_bytes=VMEM_LIMIT)


def _div_tile(n, prefs):
    for p in prefs:
        if n % p == 0:
            return p
    return n


def _row_tile(rows, cols, itemsize=4, limit=2 * 1024 * 1024):
    for p in (512, 256, 128, 64, 32, 16):
        if rows % p == 0 and p * cols * itemsize <= limit:
            return p
    return 16 if rows % 16 == 0 else rows


def _tile_cands(n):
    c = [d for d in range(LANES, min(n, 2048) + 1, LANES) if n % d == 0]
    if not c or n <= 2048 and n not in c:
        c.append(n)
    return sorted(set(c), reverse=True)


def _mm_tiles(M, N, K, a_bytes, b_bytes, o_bytes, npairs):
    best = None
    for tk in sorted(set(_tile_cands(K) + [K]), reverse=True):
        for tm in _tile_cands(M):
            for tn in _tile_cands(N):
                mem = npairs * 2 * (tm * tk * a_bytes + tk * tn * b_bytes) + 2 * tm * tn * o_bytes + tm * tn * 4
                mem += npairs * ((tm * tk * 2 if a_bytes == 4 else 0) + (tk * tn * 2 if b_bytes == 4 else 0))
                if mem > VMEM_TILE_BUDGET:
                    continue
                key = (tk == K and tm >= 3 * LANES and tn >= 2 * LANES, tm * tn * tk, tk, tn)
                if best is None or key > best[0]:
                    best = (key, (tm, tn, tk))
    assert best is not None, (M, N, K)
    return best[1]


class Comm:
    def __init__(self, operands, out_shapes, aliases, n_sems, begin, middle=None, finish=None, middle_frac=0.5):
        self.operands, self.out_shapes, self.aliases, self.n_sems = list(operands), list(out_shapes), aliases, n_sems
        self.begin, self.middle, self.finish, self.middle_frac = begin, middle, finish, middle_frac


def carrier_call(body, name, grid, in_specs, out_specs, out_shape, scratch_shapes, args, comm, semantics):
    n_in, n_out = len(args), len(out_shape)
    if comm is None:
        outs = pl.pallas_call(body, name=name, grid=grid, in_specs=in_specs, out_specs=out_specs, out_shape=out_shape,
                              scratch_shapes=scratch_shapes, compiler_params=_params(semantics))(*args)
        return list(outs), []
    ci, co = len(comm.operands), len(comm.out_shapes)
    total = math.prod(grid)
    middle_at = min(total - 1, max(0, int(total * comm.middle_frac)))

    def carried(*refs):
        ins, cins = refs[:n_in], refs[n_in:n_in + ci]
        outs = refs[n_in + ci:n_in + ci + n_out]
        couts = refs[n_in + ci + n_out:n_in + ci + n_out + co]
        scratch, (send_sems, recv_sems) = refs[n_in + ci + n_out + co:-2], refs[-2:]
        step = 0
        for d, size in enumerate(grid):
            step = step * size + pl.program_id(d)

        @pl.when(step == 0)
        def _():
            comm.begin(cins, couts, send_sems, recv_sems)

        if comm.middle is not None:
            @pl.when(step == middle_at)
            def _():
                comm.middle(cins, couts, send_sems, recv_sems)

        body(*ins, *outs, *scratch)

        @pl.when(step == total - 1)
        def _():
            comm.finish(cins, couts, send_sems, recv_sems)

    outs = pl.pallas_call(
        carried, name=name, grid=grid,
        in_specs=list(in_specs) + [ANY] * ci, out_specs=list(out_specs) + [ANY] * co,
        out_shape=list(out_shape) + comm.out_shapes,
        scratch_shapes=list(scratch_shapes) + [pltpu.SemaphoreType.DMA((comm.n_sems,)),
                                                pltpu.SemaphoreType.DMA((comm.n_sems,))],
        input_output_aliases={n_in + i: n_out + o for i, o in comm.aliases.items()},
        compiler_params=pltpu.CompilerParams(dimension_semantics=("arbitrary",) * len(grid),
                                             vmem_limit_bytes=VMEM_LIMIT, has_side_effects=True),
    )(*args, *comm.operands)
    return list(outs[:n_out]), list(outs[n_out:])


def comm_call(name, comm):
    ci = len(comm.operands)

    def body(*refs):
        cins, couts = refs[:ci], refs[ci:ci + len(comm.out_shapes)]
        send_sems, recv_sems = refs[-2:]
        comm.begin(cins, couts, send_sems, recv_sems)
        if comm.middle is not None:
            comm.middle(cins, couts, send_sems, recv_sems)
        comm.finish(cins, couts, send_sems, recv_sems)

    return pl.pallas_call(
        body, name=name, in_specs=[ANY] * ci, out_specs=[ANY] * len(comm.out_shapes), out_shape=comm.out_shapes,
        input_output_aliases=dict(comm.aliases),
        scratch_shapes=[pltpu.SemaphoreType.DMA((comm.n_sems,)), pltpu.SemaphoreType.DMA((comm.n_sems,))],
        compiler_params=pltpu.CompilerParams(has_side_effects=True),
    )(*comm.operands)


_DIMS ={"nn": (((1,), (0,)), ((), ())), "nt": (((1,), (1,)), ((), ())), "tn": (((0,), (0,)), ((), ()))}


def matmul(pairs, kind, out_dtype, name, comm=None):
    a0, b0 = pairs[0]
    if kind == "nn":
        (M, K), N = a0.shape, b0.shape[1]
    elif kind == "nt":
        (M, K), N = a0.shape, b0.shape[0]
    else:
        (K, M), N = a0.shape, b0.shape[1]
    tm, tn, tk = _mm_tiles(M, N, K, a0.dtype.itemsize, b0.dtype.itemsize, jnp.dtype(out_dtype).itemsize, len(pairs))
    nk = K // tk
    dims = _DIMS[kind]
    npairs = len(pairs)
    n_in = 2 * npairs

    def body(*refs):
        ins, o_ref = refs[:2 * npairs], refs[n_in]
        part = None
        for p in range(npairs):
            d = lax.dot_general(ins[2 * p][...].astype(BF16), ins[2 * p + 1][...].astype(BF16), dims,
                                preferred_element_type=F32)
            part = d if part is None else part + d
        if nk == 1:
            o_ref[...] = part.astype(o_ref.dtype)
        else:
            acc_ref = refs[n_in + 1]
            k = pl.program_id(2)

            @pl.when(k == 0)
            def _():
                acc_ref[...] = part

            @pl.when(k > 0)
            def _():
                acc_ref[...] += part

            @pl.when(k == nk - 1)
            def _():
                o_ref[...] = acc_ref[...].astype(o_ref.dtype)

    if kind == "nn":
        a_blk, a_map = (tm, tk), lambda j, i, k: (i, k)
        b_blk, b_map = (tk, tn), lambda j, i, k: (k, j)
    elif kind == "nt":
        a_blk, a_map = (tm, tk), lambda j, i, k: (i, k)
        b_blk, b_map = (tn, tk), lambda j, i, k: (j, k)
    else:
        a_blk, a_map = (tk, tm), lambda j, i, k: (k, i)
        b_blk, b_map = (tk, tn), lambda j, i, k: (k, j)
    (out,), arrived = carrier_call(
        body, name, (N // tn, M // tm, nk),
        [pl.BlockSpec(a_blk, a_map), pl.BlockSpec(b_blk, b_map)] * npairs,
        [pl.BlockSpec((tm, tn), lambda j, i, k: (i, j))], [jax.ShapeDtypeStruct((M, N), out_dtype)],
        [] if nk == 1 else [pltpu.VMEM((tm, tn), F32)], [t for ab in pairs for t in ab], comm,
        ("parallel", "parallel", "arbitrary"))
    return out if comm is None else ([out], arrived)


def _sigmoid(x):
    return 1.0 / (1.0 + jnp.exp(-x))

def dual_matmul_act(x, w1, w2, act, out_dtype, name, comm=None):
    M, K = x.shape
    N = w1.shape[-1]
    tm = _div_tile(M, (ROW_TILE,))
    tn = _div_tile(N, (1408, 1024, 512, 256, 128))

    def body(x_ref, w1_ref, w2_ref, o1_ref, o2_ref, out_ref):
        xv = x_ref[...]
        o1 = jnp.dot(xv, w1_ref[...], preferred_element_type=F32)
        o2 = jnp.dot(xv, w2_ref[...], preferred_element_type=F32)
        o1_ref[...] = o1.astype(BF16)
        o2_ref[...] = o2.astype(BF16)
        if act == "swiglu":
            out = o1 * _sigmoid(o1) * o2
        else:
            out = o1 * _sigmoid(o2)
        out_ref[...] = out.astype(out_ref.dtype)

    w_spec = pl.BlockSpec((K, tn), lambda j, i: (0, j))
    o_spec = pl.BlockSpec((tm, tn), lambda j, i: (i, j))
    return carrier_call(
        body, name, (N // tn, M // tm), [pl.BlockSpec((tm, K), lambda j, i: (i, 0)), w_spec, w_spec],
        [o_spec, o_spec, o_spec],
        [jax.ShapeDtypeStruct((M, N), BF16), jax.ShapeDtypeStruct((M, N), BF16),
         jax.ShapeDtypeStruct((M, N), out_dtype)], [], (x, w1, w2), comm, ("parallel", "parallel"))


def ffn_bwd_act(df, wd, a, b, name, comm=None):
    M, K = df.shape
    N = wd.shape[0]
    tm = _div_tile(M, (ROW_TILE,))
    tn = _div_tile(N, (1408, 1024, 512, 256, 128))

    def body(df_ref, wd_ref, a_ref, b_ref, da_ref, db_ref):
        dh = lax.dot_general(df_ref[...], wd_ref[...], _DIMS["nt"], preferred_element_type=F32)
        av = a_ref[...].astype(F32)
        bv = b_ref[...].astype(F32)
        sig = _sigmoid(av)
        silu = av * sig
        da_ref[...] = (dh * bv * (sig + silu * (1.0 - sig))).astype(BF16)
        db_ref[...] = (dh * silu).astype(BF16)

    t_spec = pl.BlockSpec((tm, tn), lambda j, i: (i, j))
    return carrier_call(
        body, name, (N // tn, M // tm),
        [pl.BlockSpec((tm, K), lambda j, i: (i, 0)), pl.BlockSpec((tn, K), lambda j, i: (j, 0)), t_spec, t_spec],
        [t_spec, t_spec], [jax.ShapeDtypeStruct((M, N), BF16)] * 2, [], (df, wd, a, b), comm,
        ("parallel", "parallel"))


def glu_bwd_act(dout, o1, o2, name):
    M, N = dout.shape
    tm = _div_tile(M, (ROW_TILE,))

    def body(d_ref, o1_ref, o2_ref, d1_ref, d2_ref):
        d = d_ref[...].astype(F32)
        sig = _sigmoid(o2_ref[...].astype(F32))
        d1_ref[...] = (d * sig).astype(BF16)
        d2_ref[...] = (d * o1_ref[...].astype(F32) * sig * (1.0 - sig)).astype(BF16)

    spec = pl.BlockSpec((tm, N), lambda i: (i, 0))
    return pl.pallas_call(
        body, name=name, grid=(M // tm,), in_specs=[spec] * 3, out_specs=[spec] * 2,
        out_shape=[jax.ShapeDtypeStruct((M, N), BF16)] * 2,
        compiler_params=_params(("parallel",)),
    )(dout, o1, o2)


def rmsnorm_fwd(x, g, out_dtype, name, residual=None):
    L, D = x.shape
    tr = _div_tile(L, (ROW_TILE,))
    has_res = residual is not None

    def body(*refs):
        x_ref, g_ref = refs[0], refs[1]
        o_ref = refs[-1]
        xv = x_ref[...]
        r = lax.rsqrt(jnp.mean(xv * xv, axis=-1, keepdims=True) + NORM_EPS)
        y = xv * r * g_ref[...]
        if has_res:
            y = refs[2][...] + y
        o_ref[...] = y.astype(o_ref.dtype)

    row = pl.BlockSpec((tr, D), lambda i: (i, 0))
    gsp = pl.BlockSpec((1, D), lambda i: (0, 0))
    args = (x, g) + ((residual,) if has_res else ())
    return pl.pallas_call(
        body, name=name, grid=(L // tr,), in_specs=[row, gsp] + ([row] if has_res else []), out_specs=row,
        out_shape=jax.ShapeDtypeStruct((L, D), out_dtype), compiler_params=_params(("parallel",)),
    )(*args)


def rmsnorm_bwd(x, g, dy, out_dtype, name, add=None, dy2=None):
    L, D = x.shape
    tr = _div_tile(L, (ROW_TILE,))
    has_add = add is not None
    has_dy2 = dy2 is not None

    def body(*refs):
        x_ref, g_ref, dy_ref = refs[0], refs[1], refs[2]
        dx_ref, dg_ref = refs[-2], refs[-1]
        xv = x_ref[...]
        dyv = dy_ref[...].astype(F32)
        if has_dy2:
            dyv = dyv + refs[3][...].astype(F32)
        r = lax.rsqrt(jnp.mean(xv * xv, axis=-1, keepdims=True) + NORM_EPS)
        t = dyv * g_ref[...]
        dx = r * t - xv * (r * r * r) * jnp.mean(xv * t, axis=-1, keepdims=True)
        if has_add:
            dx = refs[3 + has_dy2][...] + dx
        dx_ref[...] = dx.astype(dx_ref.dtype)
        dgp = jnp.sum(dyv * xv * r, axis=0, keepdims=True)

        @pl.when(pl.program_id(0) == 0)
        def _():
            dg_ref[...] = dgp

        @pl.when(pl.program_id(0) > 0)
        def _():
            dg_ref[...] += dgp

    row = pl.BlockSpec((tr, D), lambda i: (i, 0))
    gsp = pl.BlockSpec((1, D), lambda i: (0, 0))
    args = (x, g, dy) + ((dy2,) if has_dy2 else ()) + ((add,) if has_add else ())
    return pl.pallas_call(
        body, name=name, grid=(L // tr,), in_specs=[row, gsp] + [row] * (len(args) - 2),
        out_specs=[row, gsp],
        out_shape=[jax.ShapeDtypeStruct((L, D), out_dtype), jax.ShapeDtypeStruct((1, D), F32)],
        compiler_params=_params(("arbitrary",)),
    )(*args)


def _gate_z(fg_ref, b_ref):
    return fg_ref[...] + b_ref[...]


def gate_fwd(fg_src, col_block, b, name):
    L = fg_src.shape[0]
    T = _div_tile(L, (ROW_TILE,))

    def body(fg_ref, b_ref, c_ref, carry):
        @pl.when(pl.program_id(0) == 0)
        def _():
            carry[...] = jnp.zeros_like(carry)

        z = _gate_z(fg_ref, b_ref)
        logf = jnp.minimum(z, 0.0) - jnp.log(1.0 + jnp.exp(-jnp.abs(z)))
        tri = (lax.broadcasted_iota(jnp.int32, (T, T), 1) <= lax.broadcasted_iota(jnp.int32, (T, T), 0)).astype(F32)
        c = jnp.dot(tri, logf, precision=lax.Precision.HIGHEST, preferred_element_type=F32) + carry[...]
        c_ref[...] = c
        carry[...] = c[T - 1:T, :]

    return pl.pallas_call(
        body, name=name, grid=(L // T,),
        in_specs=[pl.BlockSpec((T, LANES), lambda i: (i, col_block)), pl.BlockSpec((1, LANES), lambda i: (0, 0))],
        out_specs=pl.BlockSpec((T, LANES), lambda i: (i, 0)),
        out_shape=jax.ShapeDtypeStruct((L, LANES), F32),
        scratch_shapes=[pltpu.VMEM((1, LANES), F32)],
        compiler_params=_params(("arbitrary",)),
    )(fg_src, b)


def gate_bwd(fg_src, col_block, b, dc, name):
    L = fg_src.shape[0]
    T = _div_tile(L, (ROW_TILE,))
    nb = L // T

    def body(fg_ref, b_ref, dc_ref, dfg_ref, db_ref, carry):
        @pl.when(pl.program_id(0) == 0)
        def _():
            carry[...] = jnp.zeros_like(carry)
            db_ref[...] = jnp.zeros_like(db_ref)

        z = _gate_z(fg_ref, b_ref)
        dcv = dc_ref[...]
        tri = (lax.broadcasted_iota(jnp.int32, (T, T), 1) >= lax.broadcasted_iota(jnp.int32, (T, T), 0)).astype(F32)
        dlogf = jnp.dot(tri, dcv, precision=lax.Precision.HIGHEST, preferred_element_type=F32) + carry[...]
        dfg = dlogf * _sigmoid(-z)
        dfg_ref[...] = dfg
        db_ref[...] += jnp.sum(dfg, axis=0, keepdims=True)
        carry[...] = dlogf[0:1, :]

    return pl.pallas_call(
        body, name=name, grid=(nb,),
        in_specs=[pl.BlockSpec((T, LANES), lambda i: (nb - 1 - i, col_block)),
                  pl.BlockSpec((1, LANES), lambda i: (0, 0)),
                  pl.BlockSpec((T, LANES), lambda i: (nb - 1 - i, 0))],
        out_specs=[pl.BlockSpec((T, LANES), lambda i: (nb - 1 - i, 0)), pl.BlockSpec((1, LANES), lambda i: (0, 0))],
        out_shape=[jax.ShapeDtypeStruct((L, LANES), F32), jax.ShapeDtypeStruct((1, LANES), F32)],
        scratch_shapes=[pltpu.VMEM((1, LANES), F32)],
        compiler_params=_params(("arbitrary",)),
    )(fg_src, b, dc)


def attn_fwd(proj, cq_col, ck_row, name, comm=None):
    L = proj.shape[0]
    T = _div_tile(L, (ROW_TILE,))
    nq = L // T
    npair = HEADS // 2
    scale = HEAD_DIM ** -0.5
    SUB = ATTN_ROWS
    nsub = T // SUB

    def body(q_ref, k_ref, v_ref, cq_ref, ck_ref, o_ref, lse_ref):
        qb = pl.program_id(1)
        rows = [slice(r * SUB, (r + 1) * SUB) for r in range(nsub)]
        head1 = lax.broadcasted_iota(jnp.int32, (SUB, LANES), 1) >= HEAD_DIM
        qs = [[jnp.where(head1 == (h == 1), q_ref[rs, :] * scale, 0.0).astype(BF16) for rs in rows] for h in range(2)]
        cqs = [[cq_ref[0, rs, h:h + 1] for rs in rows] for h in range(2)]

        def logits(kb):
            ks = pl.multiple_of(kb * T, T)
            k = k_ref[pl.ds(ks, T), :]
            return tuple(lax.dot_general(qs[h][r], k, _DIMS["nt"], preferred_element_type=F32) + cqs[h][r]
                         - ck_ref[0, h:h + 1, pl.ds(ks, T)] for h in range(2) for r in range(nsub))

        def softmax_step(kb, s_all, carry, masked):
            ks = pl.multiple_of(kb * T, T)
            v = v_ref[pl.ds(ks, T), :]
            lane = lax.broadcasted_iota(jnp.int32, (T, LANES), 1)
            new = []
            for h in range(2):
                vh = jnp.where(lane == spare[h], 1.0, v).astype(BF16)
                for r in range(nsub):
                    m, acc = carry[h * nsub + r]
                    s = s_all[h * nsub + r]
                    if masked:
                        keep = (lax.broadcasted_iota(jnp.int32, (SUB, T), 1)
                                <= lax.broadcasted_iota(jnp.int32, (SUB, T), 0) + r * SUB)
                        s = jnp.where(keep, s, NEG_BIG)
                    m_new = jnp.maximum(m, jnp.max(s, axis=1, keepdims=True))
                    p = jnp.exp(s - m_new)
                    acc = jnp.exp(m - m_new) * acc + jnp.dot(p.astype(BF16), vh, preferred_element_type=F32)
                    new.append((m_new, acc))
            return tuple(new)

        def step(kb, state):
            s_all, carry = state
            s_next = logits(kb + 1)
            return s_next, softmax_step(kb, s_all, carry, False)

        spare = (HEAD_DIM, 0)
        one = (jnp.full((SUB, 1), NEG_BIG, F32), jnp.zeros((SUB, LANES), F32))
        s_all, carry = lax.fori_loop(0, qb, step, (logits(0), (one,) * (2 * nsub)))
        carry = softmax_step(qb, s_all, carry, True)
        out, lse = [], []
        for h in range(2):
            chains = carry[h * nsub:(h + 1) * nsub]
            ls = [acc[:, spare[h]:spare[h] + 1] for _, acc in chains]
            out.append(jnp.concatenate([acc / l for (_, acc), l in zip(chains, ls)], axis=0))
            lse.append(jnp.concatenate([m + jnp.log(l) for (m, _), l in zip(chains, ls)], axis=0))
        o_ref[...] = jnp.where(lax.broadcasted_iota(jnp.int32, (T, LANES), 1) >= HEAD_DIM, out[1], out[0]
                               ).astype(o_ref.dtype)
        lse_ref[0] = jnp.concatenate(lse, axis=1)

    return carrier_call(
        body, name, (npair, nq),
        [pl.BlockSpec((T, LANES), lambda p, i: (i, p)),
         pl.BlockSpec((L, LANES), lambda p, i: (0, npair + p)),
         pl.BlockSpec((L, LANES), lambda p, i: (0, 2 * npair + p)),
         pl.BlockSpec((1, T, 2), lambda p, i: (p, i, 0)),
         pl.BlockSpec((1, 2, L), lambda p, i: (p, 0, 0))],
        [pl.BlockSpec((T, LANES), lambda p, i: (i, p)), pl.BlockSpec((1, T, 2), lambda p, i: (p, i, 0))],
        [jax.ShapeDtypeStruct((L, ATTN_W), BF16), jax.ShapeDtypeStruct((npair, L, 2), F32)],
        [], (proj, proj, proj, cq_col, ck_row), comm, ("parallel", "parallel"))


def attn_delta(dcat, cat, name):
    L = dcat.shape[0]
    T = _div_tile(L, (ROW_TILE,))

    def body(do_ref, o_ref, d_ref):
        prod = do_ref[...] * o_ref[...].astype(F32)
        sel = (lax.broadcasted_iota(jnp.int32, (ATTN_W, LANES), 0) // HEAD_DIM
               == lax.broadcasted_iota(jnp.int32, (ATTN_W, LANES), 1)).astype(F32)
        d_ref[...] = jnp.dot(prod, sel, precision=lax.Precision.HIGHEST, preferred_element_type=F32)

    return pl.pallas_call(
        body, name=name, grid=(L // T,),
        in_specs=[pl.BlockSpec((T, ATTN_W), lambda i: (i, 0)), pl.BlockSpec((T, ATTN_W), lambda i: (i, 0))],
        out_specs=pl.BlockSpec((T, LANES), lambda i: (i, 0)),
        out_shape=jax.ShapeDtypeStruct((L, LANES), F32),
        compiler_params=_params(("parallel",)),
    )(dcat, cat)


def attn_bwd(proj, dcat, lse_row, delta_row, cq_row, ck_col, name, comm=None):
    L = proj.shape[0]
    T = _div_tile(L, (ROW_TILE,))
    nb = L // T
    npair = HEADS // 2
    scale = HEAD_DIM ** -0.5

    def body(q_ref, k_ref, v_ref, do_ref, lse_ref, dl_ref, cq_ref, ck_ref,
             dq_ref, dk_ref, dv_ref, dcq_ref, dck_ref, dq_acc, dcq_acc):
        kb = pl.program_id(1)

        @pl.when(kb == 0)
        def _():
            dq_acc[...] = jnp.zeros_like(dq_acc)
            dcq_acc[...] = jnp.zeros_like(dcq_acc)

        head1 = lax.broadcasted_iota(jnp.int32, (T, LANES), 1) >= HEAD_DIM
        ks = [jnp.where(head1 == (h == 1), k_ref[...] * scale, 0.0).astype(BF16) for h in range(2)]
        vs = [jnp.where(head1 == (h == 1), v_ref[...], 0.0).astype(BF16) for h in range(2)]
        cks = [ck_ref[0, :, h:h + 1] for h in range(2)]

        def step(qb, carry, masked):
            qs = pl.multiple_of(qb * T, T)
            q = q_ref[pl.ds(qs, T), :]
            do = do_ref[pl.ds(qs, T), :].astype(BF16)
            new, dq = [], None
            for h in range(2):
                dk, dv, dck = carry[h]
                lse = lse_ref[0, h:h + 1, pl.ds(qs, T)]
                dl = dl_ref[0, h:h + 1, pl.ds(qs, T)]
                cq = cq_ref[0, h:h + 1, pl.ds(qs, T)]
                st = lax.dot_general(ks[h], q, _DIMS["nt"], preferred_element_type=F32) + cq - cks[h]
                if masked:
                    keep = lax.broadcasted_iota(jnp.int32, (T, T), 0) <= lax.broadcasted_iota(jnp.int32, (T, T), 1)
                    st = jnp.where(keep, st, NEG_BIG)
                pt = jnp.exp(st - lse)
                dv = dv + jnp.dot(pt.astype(BF16), do, preferred_element_type=F32)
                dpt = lax.dot_general(vs[h], do, _DIMS["nt"], preferred_element_type=F32)
                dst = pt * (dpt - dl)
                dsb = dst.astype(BF16)
                dk = dk + jnp.dot(dsb, q, preferred_element_type=F32)
                part = lax.dot_general(dsb, ks[h], _DIMS["tn"], preferred_element_type=F32)
                dq = part if dq is None else dq + part
                dcq_acc[h:h + 1, pl.ds(qs, T)] += jnp.sum(dst, axis=0, keepdims=True)
                dck = dck + jnp.sum(dst, axis=1, keepdims=True)
                new.append((dk, dv, dck))
            dq_acc[pl.ds(qs, T), :] += dq
            return tuple(new)

        one = (jnp.zeros((T, LANES), F32), jnp.zeros((T, LANES), F32), jnp.zeros((T, 1), F32))
        carry = step(kb, (one, one), True)
        carry = lax.fori_loop(kb + 1, nb, functools.partial(step, masked=False), carry)
        (dk0, dv0, dck0), (dk1, dv1, dck1) = carry
        dk_ref[...] = (jnp.where(head1, dk1, dk0) * scale).astype(dk_ref.dtype)
        dv_ref[...] = jnp.where(head1, dv1, dv0).astype(dv_ref.dtype)
        dck_ref[0] = jnp.concatenate([-dck0, -dck1], axis=1)

        @pl.when(kb == nb - 1)
        def _():
            dq_ref[...] = dq_acc[...].astype(dq_ref.dtype)
            dcq_ref[0] = dcq_acc[...]

    full = lambda col: pl.BlockSpec((L, LANES), col)
    row_stat = pl.BlockSpec((1, 2, L), lambda p, i: (p, 0, 0))
    return carrier_call(
        body, name, (npair, nb),
        [full(lambda p, i: (0, p)),
         pl.BlockSpec((T, LANES), lambda p, i: (i, npair + p)),
         pl.BlockSpec((T, LANES), lambda p, i: (i, 2 * npair + p)),
         full(lambda p, i: (0, p)),
         row_stat, row_stat, row_stat,
         pl.BlockSpec((1, T, 2), lambda p, i: (p, i, 0))],
        [full(lambda p, i: (0, p)),
         pl.BlockSpec((T, LANES), lambda p, i: (i, p)),
         pl.BlockSpec((T, LANES), lambda p, i: (i, p)),
         row_stat,
         pl.BlockSpec((1, T, 2), lambda p, i: (p, i, 0))],
        [jax.ShapeDtypeStruct((L, ATTN_W), BF16)] * 3
        + [jax.ShapeDtypeStruct((npair, 2, L), F32), jax.ShapeDtypeStruct((npair, L, 2), F32)],
        [pltpu.VMEM((L, LANES), F32), pltpu.VMEM((2, L), F32)],
        (proj, proj, proj, dcat, lse_row, delta_row, cq_row, ck_col), comm, ("parallel", "arbitrary"))


def _shift_down(x, k):
    rolled = pltpu.roll(x, k, 0)
    return jnp.where(lax.broadcasted_iota(jnp.int32, x.shape, 0) >= k, rolled, 0.0)


def _shift_up(x, k):
    n = x.shape[0]
    rolled = pltpu.roll(x, n - k, 0)
    return jnp.where(lax.broadcasted_iota(jnp.int32, x.shape, 0) < n - k, rolled, 0.0)


def conv_fwd(proj, col0, conv_w, conv_b, name):
    L = proj.shape[0]
    C = conv_w.shape[1]
    nc = C // LANES

    def body(gb_ref, gc_ref, xc_ref, w_ref, b_ref, o_ref):
        z = gc_ref[...] * xc_ref[...]
        conv = (w_ref[0:1, :] * _shift_down(z, 2) + w_ref[1:2, :] * _shift_down(z, 1) + w_ref[2:3, :] * z
                + b_ref[...])
        o_ref[...] = (gb_ref[...] * conv).astype(o_ref.dtype)

    col = lambda off: pl.BlockSpec((L, LANES), lambda j, off=off: (0, col0 + off + j))
    return pl.pallas_call(
        body, name=name, grid=(nc,),
        in_specs=[col(0), col(nc), col(2 * nc), pl.BlockSpec((CONV_K, LANES), lambda j: (0, j)),
                  pl.BlockSpec((1, LANES), lambda j: (0, j))],
        out_specs=pl.BlockSpec((L, LANES), lambda j: (0, j)),
        out_shape=jax.ShapeDtypeStruct((L, C), BF16),
        compiler_params=_params(("parallel",)),
    )(proj, proj, proj, conv_w, conv_b)


def conv_bwd(proj, col0, conv_w, conv_b, dcat, dcol0, name):
    L = proj.shape[0]
    C = conv_w.shape[1]
    nc = C // LANES

    def body(gb_ref, gc_ref, xc_ref, w_ref, b_ref, do_ref, dgb_ref, dgc_ref, dxc_ref, dw_ref, db_ref):
        gc, xc = gc_ref[...], xc_ref[...]
        z = gc * xc
        z1, z2 = _shift_down(z, 1), _shift_down(z, 2)
        w0, w1, w2 = w_ref[0:1, :], w_ref[1:2, :], w_ref[2:3, :]
        conv = w0 * z2 + w1 * z1 + w2 * z + b_ref[...]
        dout = do_ref[...]
        dgb_ref[...] = (dout * conv).astype(dgb_ref.dtype)
        dconv = dout * gb_ref[...]
        dw_ref[...] = jnp.concatenate([jnp.sum(dconv * z2, axis=0, keepdims=True),
                                       jnp.sum(dconv * z1, axis=0, keepdims=True),
                                       jnp.sum(dconv * z, axis=0, keepdims=True)], axis=0)
        db_ref[...] = jnp.sum(dconv, axis=0, keepdims=True)
        dz = w2 * dconv + w1 * _shift_up(dconv, 1) + w0 * _shift_up(dconv, 2)
        dgc_ref[...] = (dz * xc).astype(dgc_ref.dtype)
        dxc_ref[...] = (dz * gc).astype(dxc_ref.dtype)

    col = lambda off: pl.BlockSpec((L, LANES), lambda j, off=off: (0, col0 + off + j))
    out_col = pl.BlockSpec((L, LANES), lambda j: (0, j))
    return pl.pallas_call(
        body, name=name, grid=(nc,),
        in_specs=[col(0), col(nc), col(2 * nc), pl.BlockSpec((CONV_K, LANES), lambda j: (0, j)),
                  pl.BlockSpec((1, LANES), lambda j: (0, j)),
                  pl.BlockSpec((L, LANES), lambda j: (0, dcol0 + j))],
        out_specs=[out_col, out_col, out_col, pl.BlockSpec((CONV_K, LANES), lambda j: (0, j)),
                   pl.BlockSpec((1, LANES), lambda j: (0, j))],
        out_shape=[jax.ShapeDtypeStruct((L, C), BF16)] * 3
        + [jax.ShapeDtypeStruct((CONV_K, C), F32), jax.ShapeDtypeStruct((1, C), F32)],
        compiler_params=_params(("parallel",)),
    )(proj, proj, proj, conv_w, conv_b, dcat)


_GELU_C = math.sqrt(2.0 / math.pi)
_GELU_A = 0.044715


def _gelu(y):
    return 0.5 * y * (1.0 + jnp.tanh(_GELU_C * (y + _GELU_A * y * y * y)))


def _gelu_grad(y):
    t = jnp.tanh(_GELU_C * (y + _GELU_A * y * y * y))
    return 0.5 * (1.0 + t) + 0.5 * y * (1.0 - t * t) * _GELU_C * (1.0 + 3.0 * _GELU_A * y * y)


def _cmul_add(xr, xi, pr, pi, sr, si):
    return xr + pr * sr - pi * si, xi + pr * si + pi * sr


def _scan_tile(br, bi, cr, ci, tab_ref, reverse):
    n = S5_BLOCK_STATES
    xr, xi = br, bi
    for s, k in enumerate((1, 2, 4)):
        shift = SUBLANES - k if reverse else k
        xr, xi = _cmul_add(xr, xi, tab_ref[0, s, :, :n], tab_ref[0, s, :, n:],
                           pltpu.roll(xr, shift, 0), pltpu.roll(xi, shift, 0))
    return _cmul_add(xr, xi, tab_ref[0, 3, :, :n], tab_ref[0, 3, :, n:], cr, ci)


def s5_fwd(u, bmat, cmat, tab, dvec, name, comm=None):
    L, D = u.shape
    nblk = D // LANES
    T = _div_tile(L, (ROW_TILE,))
    ns = 2 * S5_BLOCK_STATES
    n = S5_BLOCK_STATES

    def body(u_ref, b_ref, c_ref, tab_ref, d_ref, y_ref, g_ref, xs_ref, buf, car):
        @pl.when(pl.program_id(1) == 0)
        def _():
            car[...] = jnp.zeros_like(car)

        uv = u_ref[...]
        buf[...] = jnp.dot(uv.astype(BF16), b_ref[0], preferred_element_type=F32)

        def tile(i, carry):
            cr, ci = carry
            r0 = pl.multiple_of(i * SUBLANES, SUBLANES)
            xr, xi = _scan_tile(buf[pl.ds(r0, SUBLANES), :n], buf[pl.ds(r0, SUBLANES), n:], cr, ci, tab_ref, False)
            buf[pl.ds(r0, SUBLANES), :n] = xr
            buf[pl.ds(r0, SUBLANES), n:] = xi
            return xr[SUBLANES - 1:, :], xi[SUBLANES - 1:, :]

        cr, ci = lax.fori_loop(0, T // SUBLANES, tile, (car[:, :n], car[:, n:]))
        car[:, :n] = cr
        car[:, n:] = ci
        xs = buf[...]
        xs_ref[...] = xs
        y = jnp.dot(xs.astype(BF16), c_ref[0], preferred_element_type=F32) + d_ref[...] * uv
        y_ref[...] = y
        g_ref[...] = _gelu(y).astype(g_ref.dtype)

    blk = pl.BlockSpec((T, LANES), lambda j, i: (i, j))
    return carrier_call(
        body, name, (nblk, L // T),
        [blk, pl.BlockSpec((1, LANES, ns), lambda j, i: (j, 0, 0)),
         pl.BlockSpec((1, ns, LANES), lambda j, i: (j, 0, 0)),
         pl.BlockSpec((1, 4, SUBLANES, ns), lambda j, i: (j, 0, 0, 0)),
         pl.BlockSpec((1, LANES), lambda j, i: (0, j))],
        [blk, blk, pl.BlockSpec((T, ns), lambda j, i: (i, j))],
        [jax.ShapeDtypeStruct((L, D), F32), jax.ShapeDtypeStruct((L, D), BF16),
         jax.ShapeDtypeStruct((L, nblk * ns), F32)],
        [pltpu.VMEM((T, ns), F32), pltpu.VMEM((1, ns), F32)],
        (u, bmat, cmat, tab, dvec), comm, ("parallel", "arbitrary"))


def s5_bwd(dg, y, u, xs, cmat_t, bmat_t, rtab, dvec, name, comm=None):
    L, D = u.shape
    nblk = D // LANES
    T = _div_tile(L, (ROW_TILE,))
    nch = L // T
    ns = 2 * S5_BLOCK_STATES
    n = S5_BLOCK_STATES
    ntile = T // SUBLANES

    def body(dg_ref, y_ref, u_ref, xs_ref, xp_ref, ct_ref, bt_ref, tab_ref, d_ref,
             du_ref, dc_ref, db_ref, dlam_ref, dd_ref, buf, xbuf, car):
        step = pl.program_id(1)
        first_chunk = step == nch - 1

        @pl.when(step == 0)
        def _():
            car[...] = jnp.zeros_like(car)
            dc_ref[...] = jnp.zeros_like(dc_ref)
            db_ref[...] = jnp.zeros_like(db_ref)
            dlam_ref[...] = jnp.zeros_like(dlam_ref)
            dd_ref[...] = jnp.zeros_like(dd_ref)

        uv = u_ref[...]
        dy = dg_ref[...].astype(F32) * _gelu_grad(y_ref[...])
        dd_ref[...] += jnp.sum(dy * uv, axis=0, keepdims=True)
        dyb = dy.astype(BF16)
        buf[...] = jnp.dot(dyb, ct_ref[0], preferred_element_type=F32)
        xs = xs_ref[...]
        xbuf[pl.ds(SUBLANES, T), :] = xs
        xbuf[pl.ds(0, SUBLANES), :] = jnp.where(first_chunk, 0.0, xp_ref[...])
        row0 = lax.broadcasted_iota(jnp.int32, (SUBLANES, n), 0) == 0

        def tile(ii, carry):
            cr, ci, ar, ai = carry
            r0 = pl.multiple_of((ntile - 1 - ii) * SUBLANES, SUBLANES)
            xr, xi = _scan_tile(buf[pl.ds(r0, SUBLANES), :n], buf[pl.ds(r0, SUBLANES), n:], cr, ci, tab_ref, True)
            buf[pl.ds(r0, SUBLANES), :n] = xr
            buf[pl.ds(r0, SUBLANES), n:] = xi
            r1 = pl.multiple_of(r0 + SUBLANES, SUBLANES)
            pr = jnp.where(row0, xbuf[pl.ds(r0, SUBLANES), :n][SUBLANES - 1:, :],
                           pltpu.roll(xbuf[pl.ds(r1, SUBLANES), :n], 1, 0))
            pi = jnp.where(row0, xbuf[pl.ds(r0, SUBLANES), n:][SUBLANES - 1:, :],
                           pltpu.roll(xbuf[pl.ds(r1, SUBLANES), n:], 1, 0))
            ar = ar + xr * pr + xi * pi
            ai = ai + xi * pr - xr * pi
            return xr[0:1, :], xi[0:1, :], ar, ai

        zero = jnp.zeros((SUBLANES, n), F32)
        cr, ci, ar, ai = lax.fori_loop(0, ntile, tile, (car[:, :n], car[:, n:], zero, zero))
        car[:, :n] = cr
        car[:, n:] = ci
        dlam_ref[0, :, :n] += ar
        dlam_ref[0, :, n:] += ai
        dxa = buf[...]
        dc_ref[0] += lax.dot_general(dyb, xs.astype(BF16), _DIMS["tn"], preferred_element_type=F32)
        dxb = dxa.astype(BF16)
        db_ref[0] += lax.dot_general(uv.astype(BF16), dxb, _DIMS["tn"], preferred_element_type=F32)
        du_ref[...] = jnp.dot(dxb, bt_ref[0], preferred_element_type=F32) + d_ref[...] * dy

    rev = lambda j, i: (nch - 1 - i, j)
    blk = pl.BlockSpec((T, LANES), rev)
    tpb = T // SUBLANES
    acc = pl.BlockSpec((1, LANES, ns), lambda j, i: (j, 0, 0))
    return carrier_call(
        body, name, (nblk, nch),
        [blk, blk, blk, pl.BlockSpec((T, ns), rev),
         pl.BlockSpec((SUBLANES, ns), lambda j, i: (jnp.maximum((nch - 1 - i) * tpb - 1, 0), j)),
         pl.BlockSpec((1, LANES, ns), lambda j, i: (j, 0, 0)),
         pl.BlockSpec((1, ns, LANES), lambda j, i: (j, 0, 0)),
         pl.BlockSpec((1, 4, SUBLANES, ns), lambda j, i: (j, 0, 0, 0)),
         pl.BlockSpec((1, LANES), lambda j, i: (0, j))],
        [blk, acc, acc, pl.BlockSpec((1, SUBLANES, ns), lambda j, i: (j, 0, 0)),
         pl.BlockSpec((1, LANES), lambda j, i: (0, j))],
        [jax.ShapeDtypeStruct((L, D), F32), jax.ShapeDtypeStruct((nblk, LANES, ns), F32),
         jax.ShapeDtypeStruct((nblk, LANES, ns), F32), jax.ShapeDtypeStruct((nblk, SUBLANES, ns), F32),
         jax.ShapeDtypeStruct((1, D), F32)],
        [pltpu.VMEM((T, ns), F32), pltpu.VMEM((T + SUBLANES, ns), F32), pltpu.VMEM((1, ns), F32)],
        (dg, y, u, xs, xs, cmat_t, bmat_t, rtab, dvec), comm, ("parallel", "arbitrary"))


def _s5_discretize(a_re, a_im, log_step, b_re, b_im):
    lam_re = jnp.minimum(a_re, -S5_MIN_DECAY)
    lam_im = a_im
    delta = jnp.exp(log_step)[:, None]
    mag = jnp.exp(lam_re * delta)
    ang = lam_im * delta
    lb_re = mag * jnp.cos(ang)
    lb_im = mag * jnp.sin(ang)
    den = lam_re * lam_re + lam_im * lam_im
    nr = lb_re - 1.0
    ni = lb_im
    coef_re = (nr * lam_re + ni * lam_im) / den
    coef_im = (ni * lam_re - nr * lam_im) / den
    bb_re = coef_re[..., None] * b_re - coef_im[..., None] * b_im
    bb_im = coef_re[..., None] * b_im + coef_im[..., None] * b_re
    return lb_re, lb_im, bb_re, bb_im


def _s5_tables(lb_re, lb_im):
    nblk = lb_re.shape[0] // S5_BLOCK_GROUPS
    lr = lb_re.reshape(nblk, S5_BLOCK_STATES)
    li = lb_im.reshape(nblk, S5_BLOCK_STATES)
    pows = [(jnp.ones_like(lr), jnp.zeros_like(li))]
    for _ in range(SUBLANES):
        pr, pi = pows[-1]
        pows.append((pr * lr - pi * li, pr * li + pi * lr))
    rows = jnp.arange(SUBLANES)[None, :, None]

    def table(conj, reverse):
        sgn = -1.0 if conj else 1.0
        out = []
        for k in (1, 2, 4):
            mask = (rows <= SUBLANES - 1 - k) if reverse else (rows >= k)
            out.append(jnp.concatenate([jnp.where(mask, pows[k][0][:, None, :], 0.0),
                                        jnp.where(mask, sgn * pows[k][1][:, None, :], 0.0)], axis=-1))
        order = range(SUBLANES, 0, -1) if reverse else range(1, SUBLANES + 1)
        cre = jnp.stack([pows[k][0] for k in order], axis=1)
        cim = jnp.stack([sgn * pows[k][1] for k in order], axis=1)
        out.append(jnp.concatenate([cre, cim], axis=-1))
        return jnp.stack(out, axis=1)

    return table(False, False), table(True, True)


def _s5_block_mats(bb_re, bb_im, c_re, c_im):
    G = bb_re.shape[0]
    nblk = G // S5_BLOCK_GROUPS
    eye = jnp.eye(S5_BLOCK_GROUPS, dtype=F32)
    bb = jnp.stack([bb_re, bb_im]).reshape(2, nblk, S5_BLOCK_GROUPS, S5_STATE, S5_GROUP)
    bmat = jnp.einsum("ab,rjaph->jahrbp", eye, bb).reshape(nblk, LANES, 2 * S5_BLOCK_STATES)
    cc = jnp.stack([c_re, -c_im]).reshape(2, nblk, S5_BLOCK_GROUPS, S5_GROUP, S5_STATE)
    cmat = jnp.einsum("ab,rjahp->jrbpah", eye, cc).reshape(nblk, 2 * S5_BLOCK_STATES, LANES)
    return bmat, cmat


def _s5_unblock(dB, dC, dlam):
    nblk = dB.shape[0]
    G = nblk * S5_BLOCK_GROUPS
    d6 = dB.reshape(nblk, S5_BLOCK_GROUPS, S5_GROUP, 2, S5_BLOCK_GROUPS, S5_STATE)
    dbb = jnp.einsum("jahrap->rjaph", d6).reshape(2, G, S5_STATE, S5_GROUP)
    c6 = dC.reshape(nblk, S5_BLOCK_GROUPS, S5_GROUP, 2, S5_BLOCK_GROUPS, S5_STATE)
    dcc = jnp.einsum("jahrap->rjahp", c6).reshape(2, G, S5_GROUP, S5_STATE)
    dl = jnp.sum(dlam, axis=1).reshape(nblk, 2, S5_BLOCK_GROUPS, S5_STATE)
    dl = jnp.transpose(dl, (1, 0, 2, 3)).reshape(2, G, S5_STATE)
    return dbb[0], dbb[1], dcc[0], -dcc[1], dl[0], dl[1]


def loss_and_grad(y, target, name):
    L, D = y.shape
    tr = _div_tile(L, (512, 256, 128))

    def body(y_ref, t_ref, dy_ref, loss_ref):
        err = y_ref[...] - t_ref[...]
        dy_ref[...] = err * (1.0 / D)
        part = 0.5 * jnp.sum(jnp.mean(err * err, axis=-1, keepdims=True), axis=0, keepdims=True)

        @pl.when(pl.program_id(0) == 0)
        def _():
            loss_ref[...] = part

        @pl.when(pl.program_id(0) > 0)
        def _():
            loss_ref[...] += part

    row = pl.BlockSpec((tr, D), lambda i: (i, 0))
    return pl.pallas_call(
        body, name=name, grid=(L // tr,), in_specs=[row, row],
        out_specs=[row, pl.BlockSpec((1, 1), lambda i: (0, 0))],
        out_shape=[jax.ShapeDtypeStruct((L, D), F32), jax.ShapeDtypeStruct((1, 1), F32)],
        compiler_params=_params(("arbitrary",)),
    )(y, target)


def _adam_math(w, g, m, v):
    m = ADAM_B1 * m + (1.0 - ADAM_B1) * g
    v = ADAM_B2 * v + (1.0 - ADAM_B2) * (g * g)
    m_hat = m / (1.0 - ADAM_B1 ** ADAM_STEP)
    v_hat = v / (1.0 - ADAM_B2 ** ADAM_STEP)
    delta = -ADAM_LR * (m_hat / (jnp.sqrt(v_hat) + ADAM_EPS) + ADAM_WD * w)
    return delta, m, v


def _as3d(a):
    return a.reshape((-1,) + a.shape[-2:])


def adamw(w, g, m, v, name):
    shape = w.shape
    w3, g3, m3, v3 = _as3d(w), _as3d(g), _as3d(m), _as3d(v)
    A, R, C = w3.shape
    tr = _row_tile(R, C)

    def body(w_ref, g_ref, m_ref, v_ref, d_ref, mo_ref, vo_ref):
        d, mn, vn = _adam_math(w_ref[...], g_ref[...], m_ref[...], v_ref[...])
        d_ref[...] = d
        mo_ref[...] = mn
        vo_ref[...] = vn

    spec = pl.BlockSpec((1, tr, C), lambda a, i: (a, i, 0))
    outs = pl.pallas_call(
        body, name=name, grid=(A, R // tr), in_specs=[spec] * 4, out_specs=[spec] * 3,
        out_shape=[jax.ShapeDtypeStruct((A, R, C), F32)] * 3,
        compiler_params=_params(("parallel", "parallel")),
    )(w3, g3, m3, v3)
    return [o.reshape(shape) for o in outs]


def _place():
    x, y, c = lax.axis_index("x"), lax.axis_index("y"), lax.axis_index("c")
    other_chips = [(1 - x, y), (x, 1 - y), (1 - x, 1 - y)]
    return x, y, c, other_chips


def _chip_id(chip):
    return 2 * chip[0] + chip[1]


def _my_chip():
    return 2 * lax.axis_index("x") + lax.axis_index("y")


def _remote(src, dst, send_sem, recv_sem, dev):
    return pltpu.make_async_remote_copy(src_ref=src, dst_ref=dst, send_sem=send_sem, recv_sem=recv_sem,
                                        device_id=dev, device_id_type=MESH)


def allgather_small(arrs):
    T = len(arrs)

    def body(*refs):
        ins, outs = refs[:T], refs[T:2 * T]
        send_sems, recv_sems = refs[2 * T:]
        x, y, c, chips = _place()
        me = _chip_id((x, y))
        sends = []
        for t in range(T):
            outs[t][me] = ins[t][...]
            for j, chip in enumerate(chips):
                cp = _remote(ins[t], outs[t].at[me], send_sems.at[3 * t + j], recv_sems.at[3 * t + j], (*chip, c))
                cp.start()
                sends.append(cp)
        for t in range(T):
            for j, chip in enumerate(chips):
                slot = outs[t].at[_chip_id(chip)]
                _remote(slot, slot, send_sems.at[3 * t + j], recv_sems.at[3 * t + j], (*chip, c)).wait_recv()
        for cp in sends:
            cp.wait_send()

    return pl.pallas_call(
        body, name="allgather_small", in_specs=[VMEM_SPEC] * T, out_specs=[VMEM_SPEC] * T,
        out_shape=[jax.ShapeDtypeStruct((4,) + a.shape, a.dtype) for a in arrs],
        scratch_shapes=[pltpu.SemaphoreType.DMA((3 * T,)), pltpu.SemaphoreType.DMA((3 * T,))],
        compiler_params=pltpu.CompilerParams(vmem_limit_bytes=VMEM_LIMIT, has_side_effects=True),
    )(*arrs)


def allreduce_small(buf):
    R, C = buf.shape

    def body(in_ref, out_ref, pair_ref, all_ref, send_sems, recv_sems):
        x, y, c, chips = _place()
        me, sibling = _chip_id((x, y)), (x, y, 1 - c)
        swap = _remote(in_ref, pair_ref, send_sems.at[0], recv_sems.at[0], sibling)
        swap.start()
        swap.wait()
        all_ref[me] = in_ref[...] + pair_ref[...]
        sends = []
        for j, chip in enumerate(chips):
            cp = _remote(all_ref.at[me], all_ref.at[me], send_sems.at[1 + j], recv_sems.at[1 + j], (*chip, c))
            cp.start()
            sends.append(cp)
        for j, chip in enumerate(chips):
            slot = all_ref.at[_chip_id(chip)]
            _remote(slot, slot, send_sems.at[1 + j], recv_sems.at[1 + j], (*chip, c)).wait_recv()
        for cp in sends:
            cp.wait_send()
        out_ref[...] = ((all_ref[0] + all_ref[1]) + all_ref[2]) + all_ref[3]

    return pl.pallas_call(
        body, name="allreduce_small", in_specs=[VMEM_SPEC], out_specs=VMEM_SPEC,
        out_shape=jax.ShapeDtypeStruct((R, C), F32),
        scratch_shapes=[pltpu.VMEM((R, C), F32), pltpu.VMEM((4, R, C), F32),
                        pltpu.SemaphoreType.DMA((4,)), pltpu.SemaphoreType.DMA((4,))],
        compiler_params=pltpu.CompilerParams(vmem_limit_bytes=VMEM_LIMIT, has_side_effects=True),
    )(buf)


def _half_rows(ref, layout, shard, half):
    if layout == "S":
        hr = ref.shape[1] // 2
        return ref.at[shard, pl.ds(pl.multiple_of(half * hr, 16), hr), :]
    hr, C = ref.shape[0] // 2, ref.shape[1] // 4
    return ref.at[pl.ds(pl.multiple_of(half * hr, 16), hr), pl.ds(pl.multiple_of(shard * C, LANES), C)]


def _half_rows_all(ref, layout, half):
    if layout == "S":
        hr = ref.shape[1] // 2
        return ref.at[:, pl.ds(pl.multiple_of(half * hr, 16), hr), :]
    hr = ref.shape[0] // 2
    return ref.at[pl.ds(pl.multiple_of(half * hr, 16), hr), :]


def _shard_of_half(ref, layout, shard):
    if layout == "S":
        return ref.at[shard]
    C = ref.shape[1] // 4
    return ref.at[:, pl.ds(pl.multiple_of(shard * C, LANES), C)]


def _own_block_spec(layout, tr, C):
    if layout == "S":
        return pl.BlockSpec((None, tr, C), lambda i: (_my_chip(), i, 0))
    return pl.BlockSpec((tr, C), lambda i: (i, _my_chip()))


def cast_into_gathered(shards, layer, layout, name):
    _, R, C = shards.shape
    tr = _row_tile(R, C)

    def body(a_ref, o_ref):
        o_ref[...] = a_ref[...].astype(BF16)

    return pl.pallas_call(
        body, name=name, grid=(R // tr,),
        in_specs=[pl.BlockSpec((None, tr, C), lambda i: (layer, i, 0))],
        out_specs=_own_block_spec(layout, tr, C),
        out_shape=jax.ShapeDtypeStruct((4, R, C) if layout == "S" else (R, 4 * C), BF16),
        compiler_params=_params(("parallel",)),
    )(shards)


def gather_comm(bufs, layouts):
    T = len(bufs)

    def begin(_, outs, send_sems, recv_sems):
        x, y, c, chips = _place()
        for t in range(T):
            mine = _half_rows(outs[t], layouts[t], _chip_id((x, y)), c)
            for j, chip in enumerate(chips):
                _remote(mine, mine, send_sems.at[6 * t + j], recv_sems.at[6 * t + j], (*chip, c)).start()

    def middle(_, outs, send_sems, recv_sems):
        x, y, c, chips = _place()
        for t in range(T):
            for j, chip in enumerate(chips):
                piece = _half_rows(outs[t], layouts[t], _chip_id(chip), c)
                _remote(piece, piece, send_sems.at[6 * t + j], recv_sems.at[6 * t + j], (*chip, c)).wait_recv()
                _remote(piece, piece, send_sems.at[6 * t + 3 + j], recv_sems.at[6 * t + 3 + j], (x, y, 1 - c)).start()

    def finish(_, outs, send_sems, recv_sems):
        x, y, c, chips = _place()
        for t in range(T):
            mine = _half_rows(outs[t], layouts[t], _chip_id((x, y)), c)
            for j, chip in enumerate(chips):
                theirs = _half_rows(outs[t], layouts[t], _chip_id(chip), 1 - c)
                _remote(theirs, theirs, send_sems.at[6 * t + 3 + j], recv_sems.at[6 * t + 3 + j],
                        (x, y, 1 - c)).wait_recv()
                _remote(mine, mine, send_sems.at[6 * t + j], recv_sems.at[6 * t + j], (*chip, c)).wait_send()
                piece = _half_rows(outs[t], layouts[t], _chip_id(chip), c)
                _remote(piece, piece, send_sems.at[6 * t + 3 + j], recv_sems.at[6 * t + 3 + j],
                        (x, y, 1 - c)).wait_send()

    return Comm(bufs, [jax.ShapeDtypeStruct(b.shape, b.dtype) for b in bufs], {t: t for t in range(T)}, 6 * T,
                begin, middle, finish, middle_frac=0.75)


def swap_comm(grads, layouts):
    T = len(grads)

    def out_shape(g, layout):
        return (4, g.shape[1] // 2, g.shape[2]) if layout == "S" else (g.shape[0] // 2, g.shape[1])

    def copies(ins, outs, send_sems, recv_sems):
        x, y, c, _ = _place()
        return [_remote(_half_rows_all(ins[t], layouts[t], 1 - c), outs[t], send_sems.at[t], recv_sems.at[t],
                        (x, y, 1 - c)) for t in range(T)]

    def begin(*refs):
        for cp in copies(*refs):
            cp.start()

    def finish(*refs):
        for cp in copies(*refs):
            cp.wait()

    return Comm(grads, [jax.ShapeDtypeStruct(out_shape(g, k), F32) for g, k in zip(grads, layouts)], {}, T,
                begin, None, finish)


def pair_sum(grad, recv, layout, name):
    if layout == "S":
        _, hr, C = recv.shape
        tr = _row_tile(hr, C)
        nb = hr // tr
        grid = (4, nb)
        g_spec = pl.BlockSpec((None, tr, C), lambda a, i: (a, lax.axis_index("c") * nb + i, 0))
        spec = pl.BlockSpec((None, tr, C), lambda a, i: (a, i, 0))
    else:
        hr, C = recv.shape
        tr = _row_tile(hr, C)
        nb = hr // tr
        grid = (nb,)
        g_spec = pl.BlockSpec((tr, C), lambda i: (lax.axis_index("c") * nb + i, 0))
        spec = pl.BlockSpec((tr, C), lambda i: (i, 0))

    def body(g_ref, r_ref, f_ref, b_ref):
        s = g_ref[...] + r_ref[...]
        f_ref[...] = s
        b_ref[...] = s.astype(BF16)

    return pl.pallas_call(
        body, name=name, grid=grid, in_specs=[g_spec, spec], out_specs=[spec, spec],
        out_shape=[jax.ShapeDtypeStruct(recv.shape, F32), jax.ShapeDtypeStruct(recv.shape, BF16)],
        compiler_params=_params(("parallel",) * len(grid)),
    )(grad, recv)


def exchange_comm(pair_bf16, layouts):
    T = len(pair_bf16)

    def out_shape(p, layout):
        return (3,) + ((p.shape[1], p.shape[2]) if layout == "S" else (p.shape[0], p.shape[1] // 4))

    def copies(ins, outs, send_sems, recv_sems):
        x, y, c, chips = _place()
        return [_remote(_shard_of_half(ins[t], layouts[t], _chip_id(chip)), outs[t].at[j],
                        send_sems.at[3 * t + j], recv_sems.at[3 * t + j], (*chip, c))
                for t in range(T) for j, chip in enumerate(chips)]

    def begin(*refs):
        for cp in copies(*refs):
            cp.start()

    def finish(*refs):
        for cp in copies(*refs):
            cp.wait()

    return Comm(pair_bf16, [jax.ShapeDtypeStruct(out_shape(p, k), BF16) for p, k in zip(pair_bf16, layouts)], {},
                3 * T, begin, None, finish)


def reduce_total(pair_f32, got, layout, layer, n_layers, previous, name):
    _, hr, C = got.shape
    tr = _row_tile(hr, C)
    nb = hr // tr

    def body(*refs):
        p_ref, g_ref, t_ref = refs[0], refs[1], refs[-1]
        t_ref[...] = ((p_ref[...] + g_ref[0].astype(F32)) + g_ref[1].astype(F32)) + g_ref[2].astype(F32)

    args = [pair_f32, got] + ([previous] if previous is not None else [])
    return pl.pallas_call(
        body, name=name, grid=(nb,),
        in_specs=[_own_block_spec(layout, tr, C), pl.BlockSpec((3, tr, C), lambda i: (0, i, 0))]
        + ([ANY] if previous is not None else []),
        out_specs=pl.BlockSpec((None, tr, C), lambda i: (layer, lax.axis_index("c") * nb + i, 0)),
        out_shape=jax.ShapeDtypeStruct((n_layers, 2 * hr, C), F32),
        input_output_aliases={2: 0} if previous is not None else {},
        compiler_params=_params(("parallel",)),
    )(*args)


def share_comm(reduced):
    T = len(reduced)

    def halves(outs, half):
        return [o.at[:, pl.ds(pl.multiple_of(half * (o.shape[1] // 2), 8), o.shape[1] // 2), :] for o in outs]

    def begin(_, outs, send_sems, recv_sems):
        x, y, c, _p = _place()
        for t, mine in enumerate(halves(outs, c)):
            _remote(mine, mine, send_sems.at[t], recv_sems.at[t], (x, y, 1 - c)).start()

    def finish(_, outs, send_sems, recv_sems):
        x, y, c, _p = _place()
        for t, (mine, theirs) in enumerate(zip(halves(outs, c), halves(outs, 1 - c))):
            _remote(mine, mine, send_sems.at[t], recv_sems.at[t], (x, y, 1 - c)).wait_send()
            _remote(theirs, theirs, send_sems.at[t], recv_sems.at[t], (x, y, 1 - c)).wait_recv()

    return Comm(reduced, [jax.ShapeDtypeStruct(r.shape, r.dtype) for r in reduced], {t: t for t in range(T)}, T,
                begin, None, finish)


def _round_up(n, m):
    return (n + m - 1) // m * m


def _heads_col(a16):
    L = a16.shape[0]
    return jnp.transpose(a16.reshape(L, HEADS // 2, 2), (1, 0, 2))


def _heads_row(a16):
    L = a16.shape[0]
    return jnp.transpose(a16.reshape(L, HEADS // 2, 2), (1, 2, 0))


def local_step(x, target, meta, norm_g, b_f, conv_w, conv_b, s5, s5_d, stager):
    S, D = x.shape
    depth = norm_g.shape[0]
    n_even, n_odd = b_f.shape[0], s5_d.shape[0]
    L = N_META + S
    Lp = _round_up(L, ROW_TILE)
    h = jnp.concatenate([meta, x, jnp.zeros((Lp - L, D), F32)], axis=0)
    conv_c = conv_w.shape[2]
    fg_block = 3 * conv_c // LANES
    saved = []

    def riding(tag, fn, *args):
        comm = stager.ride(tag)
        if comm is None and fn is matmul:
            return fn(*args, name=tag)
        outs, arrived = fn(*args, name=tag, comm=comm)
        stager.arrived(tag, arrived)
        return outs[0] if fn is matmul else outs

    for i in range(depth):
        g = norm_g[i]
        j = i // 2
        tag = "l%d_" % i
        w = stager.weights(i)
        st = {"h0": h, "w": w}
        if i % 2 == 0:
            u = rmsnorm_fwd(h, g[0:1], BF16, tag + "norm0")
            qkv = matmul([(u, w["w_qkv"])], "nn", BF16, tag + "qkv")
            rest = matmul([(u, w["w_rest"])], "nn", F32, tag + "rest")
            cgate = gate_fwd(rest, fg_block, b_f[j], tag + "gate")
            c16 = cgate[:, :HEADS]
            attn, lse = riding(tag + "attn", attn_fwd, qkv, _heads_col(c16), _heads_row(c16))
            convo = conv_fwd(rest, 0, conv_w[j], conv_b[j], tag + "conv")
            cat = jnp.concatenate([attn, convo], axis=1)
            m = matmul([(cat, w["w_o"])], "nn", F32, tag + "wo")
            st.update(u=u, qkv=qkv, rest=rest, c16=c16, lse=lse, cat=cat)
        else:
            p = s5[j]
            u = rmsnorm_fwd(h, g[0:1], F32, tag + "norm0")
            y, gact, xs = riding(tag + "s5", s5_fwd, u, p["bmat"], p["cmat"], p["tab"], s5_d[j])
            o1, o2, m = riding(tag + "glu", dual_matmul_act, gact, w["w_glu1"], w["w_glu2"], "glu", F32)
            st.update(u=u, y=y, gact=gact, xs=xs, o1=o1, o2=o2)
        h1 = rmsnorm_fwd(m, g[1:2], F32, tag + "norm1", residual=h)
        u2 = rmsnorm_fwd(h1, g[2:3], BF16, tag + "norm2")
        a, b, hact = riding(tag + "ffn_in", dual_matmul_act, u2, w["w_gate"], w["w_up"], "swiglu", BF16)
        f = riding(tag + "ffn_out", matmul, [(hact, w["w_down"])], "nn", F32)
        h = rmsnorm_fwd(f, g[3:4], F32, tag + "norm3", residual=h1)
        st.update(m=m, h1=h1, u2=u2, a=a, b=b, hact=hact, f=f)
        saved.append(st)

    dy, loss = loss_and_grad(h[N_META:L], target, "loss")
    dh = jnp.concatenate([jnp.zeros((N_META, D), F32), dy, jnp.zeros((Lp - L, D), F32)], axis=0)

    grads = {k: [None] * n_even for k in ("b_f", "conv_w", "conv_b")}
    grads.update({k: [None] * n_odd for k in ("s5_d", "s5_dB", "s5_dC", "s5_dlam")})
    grads["norm_g"] = [None] * depth

    for i in reversed(range(depth)):
        g = norm_g[i]
        j = i // 2
        tag = "l%d_b_" % i
        st = saved[i]
        w = st["w"]
        wg = {}
        df, dg3 = rmsnorm_bwd(st["f"], g[3:4], dh, BF16, tag + "norm3")
        wg["w_down"] = matmul([(st["hact"], df)], "tn", F32, tag + "dw_down")
        da, db = riding(tag + "ffn_act", ffn_bwd_act, df, w["w_down"], st["a"], st["b"])
        u2t = st["u2"].T
        wg["w_gate"] = matmul([(u2t, da)], "nn", F32, tag + "dw_gate")
        wg["w_up"] = matmul([(u2t, db)], "nn", F32, tag + "dw_up")
        stager.put_grads(i, wg)
        wg = {}
        du2 = riding(tag + "du2", matmul, [(da, w["w_gate"]), (db, w["w_up"])], "nt", F32)
        dh1, dg2 = rmsnorm_bwd(st["h1"], g[2:3], du2, F32, tag + "norm2", add=dh)
        if i % 2 == 0:
            dm, dg1 = rmsnorm_bwd(st["m"], g[1:2], dh1, BF16, tag + "norm1")
            wg["w_o"] = matmul([(st["cat"], dm)], "tn", F32, tag + "dw_o")
            stager.put_grads(i, wg)
            dcat = riding(tag + "dcat", matmul, [(dm, w["w_o"])], "nt", F32)
            delta = attn_delta(dcat, st["cat"], tag + "delta")
            c16 = st["c16"]
            lse16 = jnp.transpose(st["lse"], (1, 0, 2)).reshape(Lp, HEADS)
            dq, dk, dv, dcq, dck = riding(tag + "attn", attn_bwd, st["qkv"], dcat, _heads_row(lse16),
                                          _heads_row(delta[:, :HEADS]), _heads_row(c16), _heads_col(c16))
            dc16 = (jnp.transpose(dcq, (2, 0, 1)).reshape(Lp, HEADS)
                    + jnp.transpose(dck, (1, 0, 2)).reshape(Lp, HEADS))
            dc = jnp.pad(dc16, ((0, 0), (0, LANES - HEADS)))
            dfg, dbf = gate_bwd(st["rest"], fg_block, b_f[j], dc, tag + "gate")
            dgb, dgc, dxc, dcw, dcb = conv_bwd(st["rest"], 0, conv_w[j], conv_b[j], dcat, ATTN_W // LANES,
                                               tag + "conv")
            dqkv = jnp.concatenate([dq, dk, dv], axis=1)
            drest = jnp.concatenate([dgb, dgc, dxc, dfg.astype(BF16)], axis=1)
            wg["w_qkv"] = matmul([(st["u"], dqkv)], "tn", F32, tag + "dw_qkv")
            wg["w_rest"] = matmul([(st["u"], drest)], "tn", F32, tag + "dw_rest")
            stager.put_grads(i, wg)
            du = riding(tag + "du_qkv", matmul, [(dqkv, w["w_qkv"])], "nt", F32)
            du_b = riding(tag + "du_rest", matmul, [(drest, w["w_rest"])], "nt", F32)
            grads["b_f"][j], grads["conv_w"][j], grads["conv_b"][j] = dbf, dcw, dcb
        else:
            p = s5[j]
            dmix, dg1 = rmsnorm_bwd(st["m"], g[1:2], dh1, F32, tag + "norm1")
            do1, do2 = glu_bwd_act(dmix, st["o1"], st["o2"], tag + "glu_act")
            wg["w_glu1"] = matmul([(st["gact"], do1)], "tn", F32, tag + "dw_glu1")
            wg["w_glu2"] = matmul([(st["gact"], do2)], "tn", F32, tag + "dw_glu2")
            dgact = matmul([(do1, w["w_glu1"]), (do2, w["w_glu2"])], "nt", F32, tag + "dgact")
            du, dC, dB, dlam, dd = riding(tag + "s5", s5_bwd, dgact, st["y"], st["u"], st["xs"], p["cmat_t"],
                                          p["bmat_t"], p["rtab"], s5_d[j])
            du_b = None
            grads["s5_dB"][j], grads["s5_dC"][j], grads["s5_dlam"][j], grads["s5_d"][j] = dB, dC, dlam, dd
        dh, dg0 = rmsnorm_bwd(st["h0"], g[0:1], du, F32, tag + "norm0", add=dh1, dy2=du_b)
        grads["norm_g"][i] = jnp.concatenate([dg0, dg1, dg2, dg3], axis=0)
        stager.put_grads(i, wg)

    grads["meta"] = dh[:N_META]
    return loss, dh[N_META:L], grads


def _packed_rows(shape):
    return _round_up(_round_up(math.prod(shape), LANES) // LANES, SUBLANES)


def _pack(arrs):
    rows = []
    for a in arrs:
        flat = a.reshape(-1).astype(F32)
        r = _packed_rows(a.shape)
        rows.append(jnp.pad(flat, (0, r * LANES - flat.shape[0])).reshape(r, LANES))
    return jnp.concatenate(rows, axis=0)


def _unpack(buf, shapes):
    buf = buf.reshape(-1, LANES)
    out, off = [], 0
    for s in shapes:
        r = _packed_rows(s)
        out.append(buf[off:off + r].reshape(-1)[:math.prod(s)].reshape(s))
        off += r
    return out


class _LayerWeights:
    def __init__(self, stager, layer):
        self.stager, self.layer = stager, layer

    def __getitem__(self, name):
        return self.stager.weight(self.layer, name)


class MeshStager:
    LAYOUT = {"ab_w_in": "S", "ab_w_o": "S", "s5_w_glu1": "S", "s5_w_glu2": "S",
              "ffn_w_gate": "C", "ffn_w_up": "C", "ffn_w_down": "S"}
    EVEN = ("ab_w_in", "ab_w_o", "ffn_w_gate", "ffn_w_up", "ffn_w_down")
    ODD = ("s5_w_glu1", "s5_w_glu2", "ffn_w_gate", "ffn_w_up", "ffn_w_down")

    def __init__(self, shards):
        self.shards = shards
        self.depth = depth = shards["ffn_w_gate"].shape[0]
        self.bufs = {}
        for i in range(depth):
            for k in self.keys(i):
                self.bufs[k, i] = cast_into_gathered(shards[k], self.index(k, i), self.LAYOUT[k],
                                                     "cast_%s_%d" % (k, i))
        self.grads, self.pairs, self.reduced, self.ready = {}, {}, {}, {}
        first = [("ab_w_in", 0)]
        plan = self.gather_plan = {"l0_attn": [it for it in self.stage(0) if it not in first]}
        for o in range(1, depth, 2):
            e = o - 1
            plan.setdefault("l%d_attn" % e, []).append(("ffn_w_gate", o))
            plan["l%d_ffn_in" % e] = [("s5_w_glu1", o), ("ffn_w_up", o)]
            plan["l%d_ffn_out" % e] = [("s5_w_glu2", o)]
            plan["l%d_s5" % o] = [("ffn_w_down", o)] + (self.mix(o + 1) if o + 1 < depth else [])
            if o + 1 < depth:
                plan["l%d_ffn_in" % o] = [("ffn_w_gate", o + 1)]
                plan["l%d_attn" % (o + 1)] = [("ffn_w_up", o + 1), ("ffn_w_down", o + 1)]
        self.swap_plan, self.exchange_plan = {}, {}
        for i in range(depth):
            above = self.mix(i + 1) if i + 1 < depth else []
            if above:
                self.swap_plan["l%d_b_ffn_act" % i] = above
            self.swap_plan["l%d_b_du2" % i] = self.ffn(i)
            self.exchange_plan["l%d_b_%s" % (i, "attn" if i % 2 == 0 else "s5")] = above + self.ffn(i)
        self.swap_plan["l0_b_dcat"] = [("ab_w_o", 0)]
        self.exchange_plan["l0_b_attn"].append(("ab_w_o", 0))
        self.swap_plan["l0_b_du_qkv"] = [("ab_w_in", 0)]
        self.exchange_plan["l0_b_du_rest"] = [("ab_w_in", 0)]
        self._store(first, comm_call("gather_first", self._gather(first, 0.5)))

    def keys(self, i):
        return self.EVEN if i % 2 == 0 else self.ODD

    def stage(self, i):
        return [(k, i) for k in self.keys(i)]

    def mix(self, i):
        return [(k, i) for k in self.keys(i) if not k.startswith("ffn")]

    @staticmethod
    def ffn(i):
        return [("ffn_w_gate", i), ("ffn_w_up", i), ("ffn_w_down", i)]

    @staticmethod
    def index(key, i):
        return i if key.startswith("ffn") else i // 2

    def _layouts(self, items):
        return [self.LAYOUT[k] for k, _ in items]

    def _gather(self, items, middle_frac):
        comm = gather_comm([self.bufs[it] for it in items], self._layouts(items))
        comm.middle_frac = middle_frac
        return comm

    def _store(self, items, bufs):
        for it, b in zip(items, bufs):
            self.bufs[it] = b

    def ride(self, tag):
        if tag in self.gather_plan:
            return self._gather(self.gather_plan[tag], 0.85 if tag == "l0_attn" else 0.7)
        if tag in self.swap_plan:
            items = self.swap_plan[tag]
            return swap_comm([self.grads[it] for it in items], self._layouts(items))
        if tag in self.exchange_plan:
            items = self.exchange_plan[tag]
            return exchange_comm([self.pairs[it][1] for it in items], self._layouts(items))
        return None

    def arrived(self, tag, outs):
        if tag in self.gather_plan:
            self._store(self.gather_plan[tag], outs)
        elif tag in self.swap_plan:
            self._pair_sums(self.swap_plan[tag], outs)
        elif tag in self.exchange_plan:
            self._totals(self.exchange_plan[tag], outs)

    def _pair_sums(self, items, received):
        for it, r in zip(items, received):
            self.pairs[it] = pair_sum(self.grads[it], r, self.LAYOUT[it[0]], "pair_sum_%s_%d" % it)

    def _totals(self, items, got):
        for it, g in zip(items, got):
            k, i = it
            self.reduced[k] = reduce_total(self.pairs[it][0], g, self.LAYOUT[k], self.index(k, i),
                                           self.shards[k].shape[0], self.reduced.get(k), "reduce_total_%s_%d" % it)

    def weights(self, i):
        return _LayerWeights(self, i)

    def weight(self, i, name):
        if (name, i) not in self.ready:
            if name in ("w_qkv", "w_rest"):
                b = self.bufs["ab_w_in", i]
                w_in = jnp.transpose(b, (1, 0, 2)).reshape(b.shape[1], 4 * b.shape[2])
                self.ready["w_qkv", i], self.ready["w_rest", i] = split_w_in(w_in)
            else:
                k = {"w_o": "ab_w_o", "w_glu1": "s5_w_glu1", "w_glu2": "s5_w_glu2"}.get(name, "ffn_" + name)
                b = self.bufs[k, i]
                self.ready[name, i] = b.reshape(4 * b.shape[1], b.shape[2]) if self.LAYOUT[k] == "S" else b
        return self.ready[name, i]

    def put_grads(self, i, wg):
        for k in self.keys(i):
            _, R, C = self.shards[k].shape
            name = {"ab_w_in": "w_qkv", "ab_w_o": "w_o", "s5_w_glu1": "w_glu1", "s5_w_glu2": "w_glu2"}.get(k, k[4:])
            if name not in wg:
                continue
            if k == "ab_w_in":
                dw = merge_dw_in(wg["w_qkv"], wg["w_rest"])
                self.grads[k, i] = jnp.transpose(dw.reshape(R, 4, C), (1, 0, 2))
            else:
                self.grads[k, i] = wg[name].reshape(4, R, C) if self.LAYOUT[k] == "S" else wg[name]

    def finish(self):
        names = list(self.LAYOUT)
        return dict(zip(names, comm_call("share_reduced", share_comm([self.reduced[k] for k in names]))))


def split_w_in(w_in):
    fg0 = 3 * ATTN_W
    w_rest = jnp.concatenate([w_in[:, fg0 + HEADS:], w_in[:, fg0:fg0 + HEADS],
                              jnp.zeros((w_in.shape[0], LANES - HEADS), w_in.dtype)], axis=1)
    return w_in[:, :fg0], w_rest


def merge_dw_in(dw_qkv, dw_rest):
    nqc = dw_rest.shape[1] - LANES
    return jnp.concatenate([dw_qkv, dw_rest[:, nqc:nqc + HEADS], dw_rest[:, :nqc]], axis=1)


def device_step(x, target, P, stager):
    D = x.shape[-1]
    n_even, n_odd = P["ab_b_f"].shape[0], P["s5_d"].shape[0]
    conv_c = P["ab_conv_b"].shape[1]
    b_f_pad = jnp.pad(P["ab_b_f"], ((0, 0), (0, LANES - HEADS))).reshape(n_even, 1, LANES)

    s5, s5_vjps = [], []
    for j in range(n_odd):
        disc, vjp = jax.vjp(_s5_discretize, P["s5_a_re"][j], P["s5_a_im"][j], P["s5_log_step"][j],
                            P["s5_b_re"][j], P["s5_b_im"][j])
        lb_re, lb_im, bb_re, bb_im = disc
        tab, rtab = _s5_tables(lb_re, lb_im)
        bmat, cmat = _s5_block_mats(bb_re, bb_im, P["s5_c_re"][j], P["s5_c_im"][j])
        s5.append(dict(tab=tab, rtab=rtab, bmat=bmat.astype(BF16), cmat=cmat.astype(BF16),
                       bmat_t=jnp.transpose(bmat, (0, 2, 1)).astype(BF16),
                       cmat_t=jnp.transpose(cmat, (0, 2, 1)).astype(BF16)))
        s5_vjps.append(vjp)

    loss, grad_x, G = local_step(
        x, target, P["meta_tokens"], P["norm_g"], b_f_pad, P["ab_conv_w"],
        P["ab_conv_b"].reshape(n_even, 1, conv_c), s5, P["s5_d"].reshape(n_odd, 1, D), stager)

    out = {
        "meta_tokens": G["meta"],
        "norm_g": jnp.stack(G["norm_g"]),
        "ab_b_f": jnp.stack([b[0, :HEADS] for b in G["b_f"]]),
        "ab_conv_w": jnp.stack(G["conv_w"]),
        "ab_conv_b": jnp.stack([b[0] for b in G["conv_b"]]),
        "s5_d": jnp.stack([d[0] for d in G["s5_d"]]),
    }
    s5g = {k: [] for k in ("s5_a_re", "s5_a_im", "s5_log_step", "s5_b_re", "s5_b_im", "s5_c_re", "s5_c_im")}
    for j in range(n_odd):
        dbb_re, dbb_im, dc_re, dc_im, dl_re, dl_im = _s5_unblock(G["s5_dB"][j], G["s5_dC"][j], G["s5_dlam"][j])
        da_re, da_im, dls, db_re, db_im = s5_vjps[j]((dl_re, dl_im, dbb_re, dbb_im))
        for k, val in zip(s5g, (da_re, da_im, dls, db_re, db_im, dc_re, dc_im)):
            s5g[k].append(val)
    out.update({k: jnp.stack(v) for k, v in s5g.items()})
    return loss, grad_x, out


def kernel(x, meta_tokens, norm_g, ab_w_in, ab_b_f, ab_conv_w, ab_conv_b, ab_w_o, s5_a_re, s5_a_im, s5_log_step, s5_b_re, s5_b_im, s5_c_re, s5_c_im, s5_d, s5_w_glu1, s5_w_glu2, ffn_w_gate, ffn_w_up, ffn_w_down, loss_target, m_meta_tokens, m_norm_g, m_ab_w_in, m_ab_b_f, m_ab_conv_w, m_ab_conv_b, m_ab_w_o, m_s5_a_re, m_s5_a_im, m_s5_log_step, m_s5_b_re, m_s5_b_im, m_s5_c_re, m_s5_c_im, m_s5_d, m_s5_w_glu1, m_s5_w_glu2, m_ffn_w_gate, m_ffn_w_up, m_ffn_w_down, v_meta_tokens, v_norm_g, v_ab_w_in, v_ab_b_f, v_ab_conv_w, v_ab_conv_b, v_ab_w_o, v_s5_a_re, v_s5_a_im, v_s5_log_step, v_s5_b_re, v_s5_b_im, v_s5_c_re, v_s5_c_im, v_s5_d, v_s5_w_glu1, v_s5_w_glu2, v_ffn_w_gate, v_ffn_w_up, v_ffn_w_down):
    names = ["meta_tokens", "norm_g", "ab_w_in", "ab_b_f", "ab_conv_w", "ab_conv_b", "ab_w_o", "s5_a_re", "s5_a_im",
             "s5_log_step", "s5_b_re", "s5_b_im", "s5_c_re", "s5_c_im", "s5_d", "s5_w_glu1", "s5_w_glu2",
             "ffn_w_gate", "ffn_w_up", "ffn_w_down"]
    W = dict(zip(names, [meta_tokens, norm_g, ab_w_in, ab_b_f, ab_conv_w, ab_conv_b, ab_w_o, s5_a_re, s5_a_im,
                         s5_log_step, s5_b_re, s5_b_im, s5_c_re, s5_c_im, s5_d, s5_w_glu1, s5_w_glu2,
                         ffn_w_gate, ffn_w_up, ffn_w_down]))
    Mo = dict(zip(names, [m_meta_tokens, m_norm_g, m_ab_w_in, m_ab_b_f, m_ab_conv_w, m_ab_conv_b, m_ab_w_o, m_s5_a_re,
                          m_s5_a_im, m_s5_log_step, m_s5_b_re, m_s5_b_im, m_s5_c_re, m_s5_c_im, m_s5_d, m_s5_w_glu1,
                          m_s5_w_glu2, m_ffn_w_gate, m_ffn_w_up, m_ffn_w_down]))
    Vo = dict(zip(names, [v_meta_tokens, v_norm_g, v_ab_w_in, v_ab_b_f, v_ab_conv_w, v_ab_conv_b, v_ab_w_o, v_s5_a_re,
                          v_s5_a_im, v_s5_log_step, v_s5_b_re, v_s5_b_im, v_s5_c_re, v_s5_c_im, v_s5_d, v_s5_w_glu1,
                          v_s5_w_glu2, v_ffn_w_gate, v_ffn_w_up, v_ffn_w_down]))
    D = x.shape[-1]
    n_even, n_odd, depth = ab_w_in.shape[0], s5_w_glu1.shape[0], ffn_w_gate.shape[0]
    chip = 2 * lax.axis_index("x") + lax.axis_index("y")

    big = list(MeshStager.LAYOUT)
    stager = MeshStager({k: W[k] for k in big})
    g_meta, g_norm, g_convw, g_s5d = allgather_small([meta_tokens, norm_g, ab_conv_w, s5_d])
    full = {k: W[k] for k in names if k not in big}
    full["meta_tokens"] = jnp.transpose(g_meta, (1, 0, 2)).reshape(N_META, D)
    full["norm_g"] = jnp.transpose(g_norm, (1, 2, 0, 3)).reshape(depth, 4, D)
    full["ab_conv_w"] = jnp.transpose(g_convw, (1, 2, 0, 3)).reshape(n_even, CONV_K, -1)
    full["s5_d"] = jnp.transpose(g_s5d, (1, 0, 2)).reshape(n_odd, D)

    loss, grad_x, G = device_step(x[0], loss_target[0], full, stager)
    reduced = stager.finish()

    small_w = [k for k in names if k not in big]
    small_names = ["loss"] + small_w
    G["loss"] = loss
    summed = dict(zip(small_names, _unpack(allreduce_small(_pack([G[k] for k in small_names])),
                                           [G[k].shape for k in small_names])))
    loss_out = summed["loss"].reshape(())
    for k in ("meta_tokens", "norm_g", "ab_conv_w", "s5_d"):
        n_last = W[k].shape[-1]
        summed[k] = lax.dynamic_slice_in_dim(summed[k], chip * n_last, n_last, axis=summed[k].ndim - 1)
    shapes = [W[k].shape for k in small_w]
    d_s, m_s, v_s = adamw(_pack([W[k] for k in small_w])[None], _pack([summed[k] for k in small_w])[None],
                          _pack([Mo[k] for k in small_w])[None], _pack([Vo[k] for k in small_w])[None], "adamw_small")
    delta = dict(zip(small_w, _unpack(d_s, shapes)))
    new_m = dict(zip(small_w, _unpack(m_s, shapes)))
    new_v = dict(zip(small_w, _unpack(v_s, shapes)))
    grad = {k: summed[k] for k in small_w}
    for k in big:
        grad[k] = reduced[k]
        delta[k], new_m[k], new_v[k] = adamw(W[k], reduced[k], Mo[k], Vo[k], "adamw_" + k)

    return (loss_out, grad_x[None], *[grad[k] for k in names], *[delta[k] for k in names],
            *[new_m[k] for k in names], *[new_v[k] for k in names])
```

```python
import functools
import math

import jax
import jax.numpy as jnp
from jax import lax
from jax.experimental import pallas as pl
from jax.experimental.pallas import tpu as pltpu

F32 = jnp.float32
BF16 = jnp.bfloat16

N_META = 16
HEADS = 16
HEAD_DIM = 64
ATTN_W = HEADS * HEAD_DIM
CONV_K = 3
S5_GROUP = 16
S5_STATE = 64
S5_MIN_DECAY = 1e-4
NORM_EPS = 1e-6
ADAM_LR = 0.001
ADAM_B1 = 0.9
ADAM_B2 = 0.999
ADAM_EPS = 1e-08
ADAM_WD = 0.01
ADAM_STEP = 10

LANES = 128
SUBLANES = 8
VMEM_LIMIT = 56 * 1024 * 1024
VMEM_TILE_BUDGET = 34 * 1024 * 1024
ROW_TILE = 384
ATTN_ROWS = 128
S5_BLOCK_GROUPS = LANES // S5_GROUP
S5_BLOCK_STATES = S5_BLOCK_GROUPS * S5_STATE
NEG_BIG = -1e30

MESH = pl.DeviceIdType.MESH
ANY = pl.BlockSpec(memory_space=pl.ANY)
VMEM_SPEC = pl.BlockSpec(memory_space=pltpu.VMEM)


def _params(sem=None):
    return pltpu.CompilerParams(dimension_semantics=sem, vmem_limit_bytes=VMEM_LIMIT)


def _div_tile(n, prefs):
    for p in prefs:
        if n % p == 0:
            return p
    return n


def _row_tile(rows, cols, itemsize=4, limit=2 * 1024 * 1024):
    for p in (512, 256, 128, 64, 32, 16):
        if rows % p == 0 and p * cols * itemsize <= limit:
            return p
    return 16 if rows % 16 == 0 else rows


def _tile_cands(n):
    c = [d for d in range(LANES, min(n, 2048) + 1, LANES) if n % d == 0]
    if not c or n <= 2048 and n not in c:
        c.append(n)
    return sorted(set(c), reverse=True)


def _mm_tiles(M, N, K, a_bytes, b_bytes, o_bytes, npairs):
    best = None
    for tk in sorted(set(_tile_cands(K) + [K]), reverse=True):
        for tm in _tile_cands(M):
            for tn in _tile_cands(N):
                mem = npairs * 2 * (tm * tk * a_bytes + tk * tn * b_bytes) + 2 * tm * tn * o_bytes + tm * tn * 4
                mem += npairs * ((tm * tk * 2 if a_bytes == 4 else 0) + (tk * tn * 2 if b_bytes == 4 else 0))
                if mem > VMEM_TILE_BUDGET:
                    continue
                key = (tk == K and tm >= 3 * LANES and tn >= 4 * LANES, tm * tn * tk, tk, tn)
                if best is None or key > best[0]:
                    best = (key, (tm, tn, tk))
    assert best is not None, (M, N, K)
    return best[1]


class Comm:
    def __init__(self, operands, out_shapes, aliases, n_sems, begin, middle=None, finish=None, middle_frac=0.5):
        self.operands, self.out_shapes, self.aliases, self.n_sems = list(operands), list(out_shapes), aliases, n_sems
        self.begin, self.middle, self.finish, self.middle_frac = begin, middle, finish, middle_frac


def carrier_call(body, name, grid, in_specs, out_specs, out_shape, scratch_shapes, args, comm, semantics):
    n_in, n_out = len(args), len(out_shape)
    if comm is None:
        outs = pl.pallas_call(body, name=name, grid=grid, in_specs=in_specs, out_specs=out_specs, out_shape=out_shape,
                              scratch_shapes=scratch_shapes, compiler_params=_params(semantics))(*args)
        return list(outs), []
    ci, co = len(comm.operands), len(comm.out_shapes)
    total = math.prod(grid)
    middle_at = min(total - 1, max(0, int(total * comm.middle_frac)))

    def carried(*refs):
        ins, cins = refs[:n_in], refs[n_in:n_in + ci]
        outs = refs[n_in + ci:n_in + ci + n_out]
        couts = refs[n_in + ci + n_out:n_in + ci + n_out + co]
        scratch, (send_sems, recv_sems) = refs[n_in + ci + n_out + co:-2], refs[-2:]
        step = 0
        for d, size in enumerate(grid):
            step = step * size + pl.program_id(d)

        @pl.when(step == 0)
        def _():
            comm.begin(cins, couts, send_sems, recv_sems)

        if comm.middle is not None:
            @pl.when(step == middle_at)
            def _():
                comm.middle(cins, couts, send_sems, recv_sems)

        body(*ins, *outs, *scratch)

        @pl.when(step == total - 1)
        def _():
            comm.finish(cins, couts, send_sems, recv_sems)

    outs = pl.pallas_call(
        carried, name=name, grid=grid,
        in_specs=list(in_specs) + [ANY] * ci, out_specs=list(out_specs) + [ANY] * co,
        out_shape=list(out_shape) + comm.out_shapes,
        scratch_shapes=list(scratch_shapes) + [pltpu.SemaphoreType.DMA((comm.n_sems,)),
                                                pltpu.SemaphoreType.DMA((comm.n_sems,))],
        input_output_aliases={n_in + i: n_out + o for i, o in comm.aliases.items()},
        compiler_params=pltpu.CompilerParams(dimension_semantics=("arbitrary",) * len(grid),
                                             vmem_limit_bytes=VMEM_LIMIT, has_side_effects=True),
    )(*args, *comm.operands)
    return list(outs[:n_out]), list(outs[n_out:])


def comm_call(name, comm):
    ci = len(comm.operands)

    def body(*refs):
        cins, couts = refs[:ci], refs[ci:ci + len(comm.out_shapes)]
        send_sems, recv_sems = refs[-2:]
        comm.begin(cins, couts, send_sems, recv_sems)
        if comm.middle is not None:
            comm.middle(cins, couts, send_sems, recv_sems)
        comm.finish(cins, couts, send_sems, recv_sems)

    return pl.pallas_call(
        body, name=name, in_specs=[ANY] * ci, out_specs=[ANY] * len(comm.out_shapes), out_shape=comm.out_shapes,
        input_output_aliases=dict(comm.aliases),
        scratch_shapes=[pltpu.SemaphoreType.DMA((comm.n_sems,)), pltpu.SemaphoreType.DMA((comm.n_sems,))],
        compiler_params=pltpu.CompilerParams(has_side_effects=True),
    )(*comm.operands)


_DIMS ={"nn": (((1,), (0,)), ((), ())), "nt": (((1,), (1,)), ((), ())), "tn": (((0,), (0,)), ((), ()))}


def matmul(pairs, kind, out_dtype, name, comm=None, add=None):
    a0, b0 = pairs[0]
    if kind == "nn":
        (M, K), N = a0.shape, b0.shape[1]
    elif kind == "nt":
        (M, K), N = a0.shape, b0.shape[0]
    else:
        (K, M), N = a0.shape, b0.shape[1]
    tm, tn, tk = _mm_tiles(M, N, K, a0.dtype.itemsize, b0.dtype.itemsize, jnp.dtype(out_dtype).itemsize, len(pairs))
    nk = K // tk
    dims = _DIMS[kind]
    npairs = len(pairs)
    n_in = 2 * npairs + (add is not None)

    def body(*refs):
        ins, o_ref = refs[:2 * npairs], refs[n_in]
        part = None
        for p in range(npairs):
            d = lax.dot_general(ins[2 * p][...].astype(BF16), ins[2 * p + 1][...].astype(BF16), dims,
                                preferred_element_type=F32)
            part = d if part is None else part + d

        def finish(total):
            if add is not None:
                total = total + refs[2 * npairs][...]
            o_ref[...] = total.astype(o_ref.dtype)

        if nk == 1:
            finish(part)
        else:
            acc_ref = refs[n_in + 1]
            k = pl.program_id(2)

            @pl.when(k == 0)
            def _():
                acc_ref[...] = part

            @pl.when(k > 0)
            def _():
                acc_ref[...] += part

            @pl.when(k == nk - 1)
            def _():
                finish(acc_ref[...])

    if kind == "nn":
        a_blk, a_map = (tm, tk), lambda j, i, k: (i, k)
        b_blk, b_map = (tk, tn), lambda j, i, k: (k, j)
    elif kind == "nt":
        a_blk, a_map = (tm, tk), lambda j, i, k: (i, k)
        b_blk, b_map = (tn, tk), lambda j, i, k: (j, k)
    else:
        a_blk, a_map = (tk, tm), lambda j, i, k: (k, i)
        b_blk, b_map = (tk, tn), lambda j, i, k: (k, j)
    o_spec = pl.BlockSpec((tm, tn), lambda j, i, k: (i, j))
    (out,), arrived = carrier_call(
        body, name, (N // tn, M // tm, nk),
        [pl.BlockSpec(a_blk, a_map), pl.BlockSpec(b_blk, b_map)] * npairs + ([o_spec] if add is not None else []),
        [o_spec], [jax.ShapeDtypeStruct((M, N), out_dtype)],
        [] if nk == 1 else [pltpu.VMEM((tm, tn), F32)],
        [t for ab in pairs for t in ab] + ([add] if add is not None else []), comm,
        ("parallel", "parallel", "arbitrary"))
    return out if comm is None else ([out], arrived)


def _sigmoid(x):
    return 1.0 / (1.0 + jnp.exp(-x))

def dual_matmul_act(x, w1, w2, act, out_dtype, name, comm=None):
    M, K = x.shape
    N = w1.shape[-1]
    tm = _div_tile(M, (ROW_TILE,))
    tn = _div_tile(N, (1408, 1024, 512, 256, 128))

    def body(x_ref, w1_ref, w2_ref, o1_ref, o2_ref, out_ref):
        xv = x_ref[...]
        o1 = jnp.dot(xv, w1_ref[...], preferred_element_type=F32)
        o2 = jnp.dot(xv, w2_ref[...], preferred_element_type=F32)
        o1_ref[...] = o1.astype(BF16)
        o2_ref[...] = o2.astype(BF16)
        if act == "swiglu":
            out = o1 * _sigmoid(o1) * o2
        else:
            out = o1 * _sigmoid(o2)
        out_ref[...] = out.astype(out_ref.dtype)

    w_spec = pl.BlockSpec((K, tn), lambda j, i: (0, j))
    o_spec = pl.BlockSpec((tm, tn), lambda j, i: (i, j))
    return carrier_call(
        body, name, (N // tn, M // tm), [pl.BlockSpec((tm, K), lambda j, i: (i, 0)), w_spec, w_spec],
        [o_spec, o_spec, o_spec],
        [jax.ShapeDtypeStruct((M, N), BF16), jax.ShapeDtypeStruct((M, N), BF16),
         jax.ShapeDtypeStruct((M, N), out_dtype)], [], (x, w1, w2), comm, ("parallel", "parallel"))


def ffn_bwd_act(df, wd, a, b, name, comm=None):
    M, K = df.shape
    N = wd.shape[0]
    tm = _div_tile(M, (ROW_TILE,))
    tn = _div_tile(N, (1408, 1024, 512, 256, 128))

    def body(df_ref, wd_ref, a_ref, b_ref, da_ref, db_ref):
        dh = lax.dot_general(df_ref[...], wd_ref[...], _DIMS["nt"], preferred_element_type=F32)
        av = a_ref[...].astype(F32)
        bv = b_ref[...].astype(F32)
        sig = _sigmoid(av)
        silu = av * sig
        da_ref[...] = (dh * bv * (sig + silu * (1.0 - sig))).astype(BF16)
        db_ref[...] = (dh * silu).astype(BF16)

    t_spec = pl.BlockSpec((tm, tn), lambda j, i: (i, j))
    return carrier_call(
        body, name, (N // tn, M // tm),
        [pl.BlockSpec((tm, K), lambda j, i: (i, 0)), pl.BlockSpec((tn, K), lambda j, i: (j, 0)), t_spec, t_spec],
        [t_spec, t_spec], [jax.ShapeDtypeStruct((M, N), BF16)] * 2, [], (df, wd, a, b), comm,
        ("parallel", "parallel"))


def glu_bwd_act(dout, o1, o2, name):
    M, N = dout.shape
    tm = _div_tile(M, (ROW_TILE,))

    def body(d_ref, o1_ref, o2_ref, d1_ref, d2_ref):
        d = d_ref[...].astype(F32)
        sig = _sigmoid(o2_ref[...].astype(F32))
        d1_ref[...] = (d * sig).astype(BF16)
        d2_ref[...] = (d * o1_ref[...].astype(F32) * sig * (1.0 - sig)).astype(BF16)

    spec = pl.BlockSpec((tm, N), lambda i: (i, 0))
    return pl.pallas_call(
        body, name=name, grid=(M // tm,), in_specs=[spec] * 3, out_specs=[spec] * 2,
        out_shape=[jax.ShapeDtypeStruct((M, N), BF16)] * 2,
        compiler_params=_params(("parallel",)),
    )(dout, o1, o2)


def rmsnorm_fwd(x, g, out_dtype, name, residual=None):
    L, D = x.shape
    tr = _div_tile(L, (ROW_TILE,))
    has_res = residual is not None

    def body(*refs):
        x_ref, g_ref = refs[0], refs[1]
        o_ref = refs[-1]
        xv = x_ref[...]
        r = lax.rsqrt(jnp.mean(xv * xv, axis=-1, keepdims=True) + NORM_EPS)
        y = xv * r * g_ref[...]
        if has_res:
            y = refs[2][...] + y
        o_ref[...] = y.astype(o_ref.dtype)

    row = pl.BlockSpec((tr, D), lambda i: (i, 0))
    gsp = pl.BlockSpec((1, D), lambda i: (0, 0))
    args = (x, g) + ((residual,) if has_res else ())
    return pl.pallas_call(
        body, name=name, grid=(L // tr,), in_specs=[row, gsp] + ([row] if has_res else []), out_specs=row,
        out_shape=jax.ShapeDtypeStruct((L, D), out_dtype), compiler_params=_params(("parallel",)),
    )(*args)


def rmsnorm_bwd(x, g, dy, out_dtype, name, add=None, dy2=None):
    L, D = x.shape
    tr = _div_tile(L, (ROW_TILE,))
    has_add = add is not None
    has_dy2 = dy2 is not None

    def body(*refs):
        x_ref, g_ref, dy_ref = refs[0], refs[1], refs[2]
        dx_ref, dg_ref = refs[-2], refs[-1]
        xv = x_ref[...]
        dyv = dy_ref[...].astype(F32)
        if has_dy2:
            dyv = dyv + refs[3][...].astype(F32)
        r = lax.rsqrt(jnp.mean(xv * xv, axis=-1, keepdims=True) + NORM_EPS)
        t = dyv * g_ref[...]
        dx = r * t - xv * (r * r * r) * jnp.mean(xv * t, axis=-1, keepdims=True)
        if has_add:
            dx = refs[3 + has_dy2][...] + dx
        dx_ref[...] = dx.astype(dx_ref.dtype)
        dgp = jnp.sum(dyv * xv * r, axis=0, keepdims=True)

        @pl.when(pl.program_id(0) == 0)
        def _():
            dg_ref[...] = dgp

        @pl.when(pl.program_id(0) > 0)
        def _():
            dg_ref[...] += dgp

    row = pl.BlockSpec((tr, D), lambda i: (i, 0))
    gsp = pl.BlockSpec((1, D), lambda i: (0, 0))
    args = (x, g, dy) + ((dy2,) if has_dy2 else ()) + ((add,) if has_add else ())
    return pl.pallas_call(
        body, name=name, grid=(L // tr,), in_specs=[row, gsp] + [row] * (len(args) - 2),
        out_specs=[row, gsp],
        out_shape=[jax.ShapeDtypeStruct((L, D), out_dtype), jax.ShapeDtypeStruct((1, D), F32)],
        compiler_params=_params(("arbitrary",)),
    )(*args)


def _gate_z(fg_ref, b_ref):
    return fg_ref[...] + b_ref[...]


def gate_fwd(fg_src, col_block, b, name):
    L = fg_src.shape[0]
    T = _div_tile(L, (ROW_TILE,))

    def body(fg_ref, b_ref, c_ref, carry):
        @pl.when(pl.program_id(0) == 0)
        def _():
            carry[...] = jnp.zeros_like(carry)

        z = _gate_z(fg_ref, b_ref)
        logf = jnp.minimum(z, 0.0) - jnp.log(1.0 + jnp.exp(-jnp.abs(z)))
        tri = (lax.broadcasted_iota(jnp.int32, (T, T), 1) <= lax.broadcasted_iota(jnp.int32, (T, T), 0)).astype(F32)
        c = jnp.dot(tri, logf, precision=lax.Precision.HIGHEST, preferred_element_type=F32) + carry[...]
        c_ref[...] = c
        carry[...] = c[T - 1:T, :]

    return pl.pallas_call(
        body, name=name, grid=(L // T,),
        in_specs=[pl.BlockSpec((T, LANES), lambda i: (i, col_block)), pl.BlockSpec((1, LANES), lambda i: (0, 0))],
        out_specs=pl.BlockSpec((T, LANES), lambda i: (i, 0)),
        out_shape=jax.ShapeDtypeStruct((L, LANES), F32),
        scratch_shapes=[pltpu.VMEM((1, LANES), F32)],
        compiler_params=_params(("arbitrary",)),
    )(fg_src, b)


def gate_bwd(fg_src, col_block, b, dc, name):
    L = fg_src.shape[0]
    T = _div_tile(L, (ROW_TILE,))
    nb = L // T

    def body(fg_ref, b_ref, dc_ref, dfg_ref, db_ref, carry):
        @pl.when(pl.program_id(0) == 0)
        def _():
            carry[...] = jnp.zeros_like(carry)
            db_ref[...] = jnp.zeros_like(db_ref)

        z = _gate_z(fg_ref, b_ref)
        dcv = dc_ref[...]
        tri = (lax.broadcasted_iota(jnp.int32, (T, T), 1) >= lax.broadcasted_iota(jnp.int32, (T, T), 0)).astype(F32)
        dlogf = jnp.dot(tri, dcv, precision=lax.Precision.HIGHEST, preferred_element_type=F32) + carry[...]
        dfg = dlogf * _sigmoid(-z)
        dfg_ref[...] = dfg
        db_ref[...] += jnp.sum(dfg, axis=0, keepdims=True)
        carry[...] = dlogf[0:1, :]

    return pl.pallas_call(
        body, name=name, grid=(nb,),
        in_specs=[pl.BlockSpec((T, LANES), lambda i: (nb - 1 - i, col_block)),
                  pl.BlockSpec((1, LANES), lambda i: (0, 0)),
                  pl.BlockSpec((T, LANES), lambda i: (nb - 1 - i, 0))],
        out_specs=[pl.BlockSpec((T, LANES), lambda i: (nb - 1 - i, 0)), pl.BlockSpec((1, LANES), lambda i: (0, 0))],
        out_shape=[jax.ShapeDtypeStruct((L, LANES), F32), jax.ShapeDtypeStruct((1, LANES), F32)],
        scratch_shapes=[pltpu.VMEM((1, LANES), F32)],
        compiler_params=_params(("arbitrary",)),
    )(fg_src, b, dc)


def attn_fwd(proj, cq_col, ck_row, name, comm=None):
    L = proj.shape[0]
    T = _div_tile(L, (ROW_TILE,))
    nq = L // T
    npair = HEADS // 2
    scale = HEAD_DIM ** -0.5
    SUB = ATTN_ROWS
    nsub = T // SUB

    def body(q_ref, k_ref, v_ref, cq_ref, ck_ref, o_ref, lse_ref):
        qb = pl.program_id(1)
        rows = [slice(r * SUB, (r + 1) * SUB) for r in range(nsub)]
        head1 = lax.broadcasted_iota(jnp.int32, (SUB, LANES), 1) >= HEAD_DIM
        qs = [[jnp.where(head1 == (h == 1), q_ref[rs, :] * scale, 0.0).astype(BF16) for rs in rows] for h in range(2)]
        cqs = [[cq_ref[0, rs, h:h + 1] for rs in rows] for h in range(2)]

        def logits(kb):
            ks = pl.multiple_of(kb * T, T)
            k = k_ref[pl.ds(ks, T), :]
            return tuple(lax.dot_general(qs[h][r], k, _DIMS["nt"], preferred_element_type=F32) + cqs[h][r]
                         - ck_ref[0, h:h + 1, pl.ds(ks, T)] for h in range(2) for r in range(nsub))

        def softmax_step(kb, s_all, carry, masked):
            ks = pl.multiple_of(kb * T, T)
            v = v_ref[pl.ds(ks, T), :]
            lane = lax.broadcasted_iota(jnp.int32, (T, LANES), 1)
            new = []
            for h in range(2):
                vh = jnp.where(lane == spare[h], 1.0, v).astype(BF16)
                for r in range(nsub):
                    m, acc = carry[h * nsub + r]
                    s = s_all[h * nsub + r]
                    if masked:
                        keep = (lax.broadcasted_iota(jnp.int32, (SUB, T), 1)
                                <= lax.broadcasted_iota(jnp.int32, (SUB, T), 0) + r * SUB)
                        s = jnp.where(keep, s, NEG_BIG)
                    m_new = jnp.maximum(m, jnp.max(s, axis=1, keepdims=True))
                    p = jnp.exp(s - m_new)
                    acc = jnp.exp(m - m_new) * acc + jnp.dot(p.astype(BF16), vh, preferred_element_type=F32)
                    new.append((m_new, acc))
            return tuple(new)

        def step(kb, state):
            s_all, carry = state
            s_next = logits(kb + 1)
            return s_next, softmax_step(kb, s_all, carry, False)

        spare = (HEAD_DIM, 0)
        one = (jnp.full((SUB, 1), NEG_BIG, F32), jnp.zeros((SUB, LANES), F32))
        s_all, carry = lax.fori_loop(0, qb, step, (logits(0), (one,) * (2 * nsub)))
        carry = softmax_step(qb, s_all, carry, True)
        out, lse = [], []
        for h in range(2):
            chains = carry[h * nsub:(h + 1) * nsub]
            ls = [acc[:, spare[h]:spare[h] + 1] for _, acc in chains]
            out.append(jnp.concatenate([acc / l for (_, acc), l in zip(chains, ls)], axis=0))
            lse.append(jnp.concatenate([m + jnp.log(l) for (m, _), l in zip(chains, ls)], axis=0))
        o_ref[...] = jnp.where(lax.broadcasted_iota(jnp.int32, (T, LANES), 1) >= HEAD_DIM, out[1], out[0]
                               ).astype(o_ref.dtype)
        lse_ref[0] = jnp.concatenate(lse, axis=1)

    return carrier_call(
        body, name, (npair, nq),
        [pl.BlockSpec((T, LANES), lambda p, i: (i, p)),
         pl.BlockSpec((L, LANES), lambda p, i: (0, npair + p)),
         pl.BlockSpec((L, LANES), lambda p, i: (0, 2 * npair + p)),
         pl.BlockSpec((1, T, 2), lambda p, i: (p, i, 0)),
         pl.BlockSpec((1, 2, L), lambda p, i: (p, 0, 0))],
        [pl.BlockSpec((T, LANES), lambda p, i: (i, p)), pl.BlockSpec((1, T, 2), lambda p, i: (p, i, 0))],
        [jax.ShapeDtypeStruct((L, ATTN_W), BF16), jax.ShapeDtypeStruct((npair, L, 2), F32)],
        [], (proj, proj, proj, cq_col, ck_row), comm, ("parallel", "parallel"))


def attn_delta(dcat, cat, name):
    L = dcat.shape[0]
    T = _div_tile(L, (ROW_TILE,))

    def body(do_ref, o_ref, d_ref):
        prod = do_ref[...] * o_ref[...].astype(F32)
        sel = (lax.broadcasted_iota(jnp.int32, (ATTN_W, LANES), 0) // HEAD_DIM
               == lax.broadcasted_iota(jnp.int32, (ATTN_W, LANES), 1)).astype(F32)
        d_ref[...] = jnp.dot(prod, sel, precision=lax.Precision.HIGHEST, preferred_element_type=F32)

    return pl.pallas_call(
        body, name=name, grid=(L // T,),
        in_specs=[pl.BlockSpec((T, ATTN_W), lambda i: (i, 0)), pl.BlockSpec((T, ATTN_W), lambda i: (i, 0))],
        out_specs=pl.BlockSpec((T, LANES), lambda i: (i, 0)),
        out_shape=jax.ShapeDtypeStruct((L, LANES), F32),
        compiler_params=_params(("parallel",)),
    )(dcat, cat)


def attn_bwd(proj, dcat, lse_row, delta_row, cq_row, ck_col, name, comm=None):
    L = proj.shape[0]
    T = _div_tile(L, (ROW_TILE,))
    nb = L // T
    npair = HEADS // 2
    scale = HEAD_DIM ** -0.5

    def body(q_ref, k_ref, v_ref, do_ref, lse_ref, dl_ref, cq_ref, ck_ref,
             dq_ref, dk_ref, dv_ref, dcq_ref, dck_ref, dq_acc, dcq_acc):
        kb = pl.program_id(1)

        @pl.when(kb == 0)
        def _():
            dq_acc[...] = jnp.zeros_like(dq_acc)
            dcq_acc[...] = jnp.zeros_like(dcq_acc)

        head1 = lax.broadcasted_iota(jnp.int32, (T, LANES), 1) >= HEAD_DIM
        ks = [jnp.where(head1 == (h == 1), k_ref[...] * scale, 0.0).astype(BF16) for h in range(2)]
        vs = [jnp.where(head1 == (h == 1), v_ref[...], 0.0).astype(BF16) for h in range(2)]
        cks = [ck_ref[0, :, h:h + 1] for h in range(2)]

        def step(qb, carry, masked):
            qs = pl.multiple_of(qb * T, T)
            q = q_ref[pl.ds(qs, T), :]
            do = do_ref[pl.ds(qs, T), :].astype(BF16)
            new, dq = [], None
            for h in range(2):
                dk, dv, dck = carry[h]
                lse = lse_ref[0, h:h + 1, pl.ds(qs, T)]
                dl = dl_ref[0, h:h + 1, pl.ds(qs, T)]
                cq = cq_ref[0, h:h + 1, pl.ds(qs, T)]
                st = lax.dot_general(ks[h], q, _DIMS["nt"], preferred_element_type=F32) + cq - cks[h]
                if masked:
                    keep = lax.broadcasted_iota(jnp.int32, (T, T), 0) <= lax.broadcasted_iota(jnp.int32, (T, T), 1)
                    st = jnp.where(keep, st, NEG_BIG)
                pt = jnp.exp(st - lse)
                dv = dv + jnp.dot(pt.astype(BF16), do, preferred_element_type=F32)
                dpt = lax.dot_general(vs[h], do, _DIMS["nt"], preferred_element_type=F32)
                dst = pt * (dpt - dl)
                dsb = dst.astype(BF16)
                dk = dk + jnp.dot(dsb, q, preferred_element_type=F32)
                part = lax.dot_general(dsb, ks[h], _DIMS["tn"], preferred_element_type=F32)
                dq = part if dq is None else dq + part
                dcq_acc[h:h + 1, pl.ds(qs, T)] += jnp.sum(dst, axis=0, keepdims=True)
                dck = dck + jnp.sum(dst, axis=1, keepdims=True)
                new.append((dk, dv, dck))
            dq_acc[pl.ds(qs, T), :] += dq
            return tuple(new)

        one = (jnp.zeros((T, LANES), F32), jnp.zeros((T, LANES), F32), jnp.zeros((T, 1), F32))
        carry = step(kb, (one, one), True)
        carry = lax.fori_loop(kb + 1, nb, functools.partial(step, masked=False), carry)
        (dk0, dv0, dck0), (dk1, dv1, dck1) = carry
        dk_ref[...] = (jnp.where(head1, dk1, dk0) * scale).astype(dk_ref.dtype)
        dv_ref[...] = jnp.where(head1, dv1, dv0).astype(dv_ref.dtype)
        dck_ref[0] = jnp.concatenate([-dck0, -dck1], axis=1)

        @pl.when(kb == nb - 1)
        def _():
            dq_ref[...] = dq_acc[...].astype(dq_ref.dtype)
            dcq_ref[0] = dcq_acc[...]

    full = lambda col: pl.BlockSpec((L, LANES), col)
    row_stat = pl.BlockSpec((1, 2, L), lambda p, i: (p, 0, 0))
    return carrier_call(
        body, name, (npair, nb),
        [full(lambda p, i: (0, p)),
         pl.BlockSpec((T, LANES), lambda p, i: (i, npair + p)),
         pl.BlockSpec((T, LANES), lambda p, i: (i, 2 * npair + p)),
         full(lambda p, i: (0, p)),
         row_stat, row_stat, row_stat,
         pl.BlockSpec((1, T, 2), lambda p, i: (p, i, 0))],
        [full(lambda p, i: (0, p)),
         pl.BlockSpec((T, LANES), lambda p, i: (i, p)),
         pl.BlockSpec((T, LANES), lambda p, i: (i, p)),
         row_stat,
         pl.BlockSpec((1, T, 2), lambda p, i: (p, i, 0))],
        [jax.ShapeDtypeStruct((L, ATTN_W), BF16)] * 3
        + [jax.ShapeDtypeStruct((npair, 2, L), F32), jax.ShapeDtypeStruct((npair, L, 2), F32)],
        [pltpu.VMEM((L, LANES), F32), pltpu.VMEM((2, L), F32)],
        (proj, proj, proj, dcat, lse_row, delta_row, cq_row, ck_col), comm, ("parallel", "arbitrary"))


def _shift_down(x, k):
    rolled = pltpu.roll(x, k, 0)
    return jnp.where(lax.broadcasted_iota(jnp.int32, x.shape, 0) >= k, rolled, 0.0)


def _shift_up(x, k):
    n = x.shape[0]
    rolled = pltpu.roll(x, n - k, 0)
    return jnp.where(lax.broadcasted_iota(jnp.int32, x.shape, 0) < n - k, rolled, 0.0)


def conv_fwd(proj, col0, conv_w, conv_b, name):
    L = proj.shape[0]
    C = conv_w.shape[1]
    nc = C // LANES

    def body(gb_ref, gc_ref, xc_ref, w_ref, b_ref, o_ref):
        z = gc_ref[...] * xc_ref[...]
        conv = (w_ref[0:1, :] * _shift_down(z, 2) + w_ref[1:2, :] * _shift_down(z, 1) + w_ref[2:3, :] * z
                + b_ref[...])
        o_ref[...] = (gb_ref[...] * conv).astype(o_ref.dtype)

    col = lambda off: pl.BlockSpec((L, LANES), lambda j, off=off: (0, col0 + off + j))
    return pl.pallas_call(
        body, name=name, grid=(nc,),
        in_specs=[col(0), col(nc), col(2 * nc), pl.BlockSpec((CONV_K, LANES), lambda j: (0, j)),
                  pl.BlockSpec((1, LANES), lambda j: (0, j))],
        out_specs=pl.BlockSpec((L, LANES), lambda j: (0, j)),
        out_shape=jax.ShapeDtypeStruct((L, C), BF16),
        compiler_params=_params(("parallel",)),
    )(proj, proj, proj, conv_w, conv_b)


def conv_bwd(proj, col0, conv_w, conv_b, dcat, dcol0, name):
    L = proj.shape[0]
    C = conv_w.shape[1]
    nc = C // LANES

    def body(gb_ref, gc_ref, xc_ref, w_ref, b_ref, do_ref, dgb_ref, dgc_ref, dxc_ref, dw_ref, db_ref):
        gc, xc = gc_ref[...], xc_ref[...]
        z = gc * xc
        z1, z2 = _shift_down(z, 1), _shift_down(z, 2)
        w0, w1, w2 = w_ref[0:1, :], w_ref[1:2, :], w_ref[2:3, :]
        conv = w0 * z2 + w1 * z1 + w2 * z + b_ref[...]
        dout = do_ref[...]
        dgb_ref[...] = (dout * conv).astype(dgb_ref.dtype)
        dconv = dout * gb_ref[...]
        dw_ref[...] = jnp.concatenate([jnp.sum(dconv * z2, axis=0, keepdims=True),
                                       jnp.sum(dconv * z1, axis=0, keepdims=True),
                                       jnp.sum(dconv * z, axis=0, keepdims=True)], axis=0)
        db_ref[...] = jnp.sum(dconv, axis=0, keepdims=True)
        dz = w2 * dconv + w1 * _shift_up(dconv, 1) + w0 * _shift_up(dconv, 2)
        dgc_ref[...] = (dz * xc).astype(dgc_ref.dtype)
        dxc_ref[...] = (dz * gc).astype(dxc_ref.dtype)

    col = lambda off: pl.BlockSpec((L, LANES), lambda j, off=off: (0, col0 + off + j))
    out_col = pl.BlockSpec((L, LANES), lambda j: (0, j))
    return pl.pallas_call(
        body, name=name, grid=(nc,),
        in_specs=[col(0), col(nc), col(2 * nc), pl.BlockSpec((CONV_K, LANES), lambda j: (0, j)),
                  pl.BlockSpec((1, LANES), lambda j: (0, j)),
                  pl.BlockSpec((L, LANES), lambda j: (0, dcol0 + j))],
        out_specs=[out_col, out_col, out_col, pl.BlockSpec((CONV_K, LANES), lambda j: (0, j)),
                   pl.BlockSpec((1, LANES), lambda j: (0, j))],
        out_shape=[jax.ShapeDtypeStruct((L, C), BF16)] * 3
        + [jax.ShapeDtypeStruct((CONV_K, C), F32), jax.ShapeDtypeStruct((1, C), F32)],
        compiler_params=_params(("parallel",)),
    )(proj, proj, proj, conv_w, conv_b, dcat)


_GELU_C = math.sqrt(2.0 / math.pi)
_GELU_A = 0.044715


def _gelu(y):
    return 0.5 * y * (1.0 + jnp.tanh(_GELU_C * (y + _GELU_A * y * y * y)))


def _gelu_grad(y):
    t = jnp.tanh(_GELU_C * (y + _GELU_A * y * y * y))
    return 0.5 * (1.0 + t) + 0.5 * y * (1.0 - t * t) * _GELU_C * (1.0 + 3.0 * _GELU_A * y * y)


def _cmul_add(xr, xi, pr, pi, sr, si):
    return xr + pr * sr - pi * si, xi + pr * si + pi * sr


def _scan_tile(br, bi, cr, ci, tab_ref, reverse):
    n = S5_BLOCK_STATES
    xr, xi = br, bi
    for s, k in enumerate((1, 2, 4)):
        shift = SUBLANES - k if reverse else k
        xr, xi = _cmul_add(xr, xi, tab_ref[0, s, :, :n], tab_ref[0, s, :, n:],
                           pltpu.roll(xr, shift, 0), pltpu.roll(xi, shift, 0))
    return _cmul_add(xr, xi, tab_ref[0, 3, :, :n], tab_ref[0, 3, :, n:], cr, ci)


def s5_fwd(u, bmat, cmat, tab, dvec, name, comm=None):
    L, D = u.shape
    nblk = D // LANES
    T = _div_tile(L, (ROW_TILE,))
    ns = 2 * S5_BLOCK_STATES
    n = S5_BLOCK_STATES

    def body(u_ref, b_ref, c_ref, tab_ref, d_ref, y_ref, g_ref, xs_ref, buf, car):
        @pl.when(pl.program_id(1) == 0)
        def _():
            car[...] = jnp.zeros_like(car)

        uv = u_ref[...]
        buf[...] = jnp.dot(uv.astype(BF16), b_ref[0], preferred_element_type=F32)

        def tile(i, carry):
            cr, ci = carry
            r0 = pl.multiple_of(i * SUBLANES, SUBLANES)
            xr, xi = _scan_tile(buf[pl.ds(r0, SUBLANES), :n], buf[pl.ds(r0, SUBLANES), n:], cr, ci, tab_ref, False)
            buf[pl.ds(r0, SUBLANES), :n] = xr
            buf[pl.ds(r0, SUBLANES), n:] = xi
            return xr[SUBLANES - 1:, :], xi[SUBLANES - 1:, :]

        cr, ci = lax.fori_loop(0, T // SUBLANES, tile, (car[:, :n], car[:, n:]))
        car[:, :n] = cr
        car[:, n:] = ci
        xs = buf[...]
        xs_ref[...] = xs
        y = jnp.dot(xs.astype(BF16), c_ref[0], preferred_element_type=F32) + d_ref[...] * uv
        y_ref[...] = y
        g_ref[...] = _gelu(y).astype(g_ref.dtype)

    blk = pl.BlockSpec((T, LANES), lambda j, i: (i, j))
    return carrier_call(
        body, name, (nblk, L // T),
        [blk, pl.BlockSpec((1, LANES, ns), lambda j, i: (j, 0, 0)),
         pl.BlockSpec((1, ns, LANES), lambda j, i: (j, 0, 0)),
         pl.BlockSpec((1, 4, SUBLANES, ns), lambda j, i: (j, 0, 0, 0)),
         pl.BlockSpec((1, LANES), lambda j, i: (0, j))],
        [blk, blk, pl.BlockSpec((T, ns), lambda j, i: (i, j))],
        [jax.ShapeDtypeStruct((L, D), F32), jax.ShapeDtypeStruct((L, D), BF16),
         jax.ShapeDtypeStruct((L, nblk * ns), F32)],
        [pltpu.VMEM((T, ns), F32), pltpu.VMEM((1, ns), F32)],
        (u, bmat, cmat, tab, dvec), comm, ("parallel", "arbitrary"))


def s5_bwd(dg, y, u, xs, cmat_t, bmat_t, rtab, dvec, name, comm=None):
    L, D = u.shape
    nblk = D // LANES
    T = _div_tile(L, (ROW_TILE,))
    nch = L // T
    ns = 2 * S5_BLOCK_STATES
    n = S5_BLOCK_STATES
    ntile = T // SUBLANES

    def body(dg_ref, y_ref, u_ref, xs_ref, xp_ref, ct_ref, bt_ref, tab_ref, d_ref,
             du_ref, dc_ref, db_ref, dlam_ref, dd_ref, buf, xbuf, car):
        step = pl.program_id(1)
        first_chunk = step == nch - 1

        @pl.when(step == 0)
        def _():
            car[...] = jnp.zeros_like(car)
            dc_ref[...] = jnp.zeros_like(dc_ref)
            db_ref[...] = jnp.zeros_like(db_ref)
            dlam_ref[...] = jnp.zeros_like(dlam_ref)
            dd_ref[...] = jnp.zeros_like(dd_ref)

        uv = u_ref[...]
        dy = dg_ref[...].astype(F32) * _gelu_grad(y_ref[...])
        dd_ref[...] += jnp.sum(dy * uv, axis=0, keepdims=True)
        dyb = dy.astype(BF16)
        buf[...] = jnp.dot(dyb, ct_ref[0], preferred_element_type=F32)
        xs = xs_ref[...]
        xbuf[pl.ds(SUBLANES, T), :] = xs
        xbuf[pl.ds(0, SUBLANES), :] = jnp.where(first_chunk, 0.0, xp_ref[...])
        row0 = lax.broadcasted_iota(jnp.int32, (SUBLANES, n), 0) == 0

        def tile(ii, carry):
            cr, ci, ar, ai = carry
            r0 = pl.multiple_of((ntile - 1 - ii) * SUBLANES, SUBLANES)
            xr, xi = _scan_tile(buf[pl.ds(r0, SUBLANES), :n], buf[pl.ds(r0, SUBLANES), n:], cr, ci, tab_ref, True)
            buf[pl.ds(r0, SUBLANES), :n] = xr
            buf[pl.ds(r0, SUBLANES), n:] = xi
            r1 = pl.multiple_of(r0 + SUBLANES, SUBLANES)
            pr = jnp.where(row0, xbuf[pl.ds(r0, SUBLANES), :n][SUBLANES - 1:, :],
                           pltpu.roll(xbuf[pl.ds(r1, SUBLANES), :n], 1, 0))
            pi = jnp.where(row0, xbuf[pl.ds(r0, SUBLANES), n:][SUBLANES - 1:, :],
                           pltpu.roll(xbuf[pl.ds(r1, SUBLANES), n:], 1, 0))
            ar = ar + xr * pr + xi * pi
            ai = ai + xi * pr - xr * pi
            return xr[0:1, :], xi[0:1, :], ar, ai

        zero = jnp.zeros((SUBLANES, n), F32)
        cr, ci, ar, ai = lax.fori_loop(0, ntile, tile, (car[:, :n], car[:, n:], zero, zero))
        car[:, :n] = cr
        car[:, n:] = ci
        dlam_ref[0, :, :n] += ar
        dlam_ref[0, :, n:] += ai
        dxa = buf[...]
        dc_ref[0] += lax.dot_general(dyb, xs.astype(BF16), _DIMS["tn"], preferred_element_type=F32)
        dxb = dxa.astype(BF16)
        db_ref[0] += lax.dot_general(uv.astype(BF16), dxb, _DIMS["tn"], preferred_element_type=F32)
        du_ref[...] = jnp.dot(dxb, bt_ref[0], preferred_element_type=F32) + d_ref[...] * dy

    rev = lambda j, i: (nch - 1 - i, j)
    blk = pl.BlockSpec((T, LANES), rev)
    tpb = T // SUBLANES
    acc = pl.BlockSpec((1, LANES, ns), lambda j, i: (j, 0, 0))
    return carrier_call(
        body, name, (nblk, nch),
        [blk, blk, blk, pl.BlockSpec((T, ns), rev),
         pl.BlockSpec((SUBLANES, ns), lambda j, i: (jnp.maximum((nch - 1 - i) * tpb - 1, 0), j)),
         pl.BlockSpec((1, LANES, ns), lambda j, i: (j, 0, 0)),
         pl.BlockSpec((1, ns, LANES), lambda j, i: (j, 0, 0)),
         pl.BlockSpec((1, 4, SUBLANES, ns), lambda j, i: (j, 0, 0, 0)),
         pl.BlockSpec((1, LANES), lambda j, i: (0, j))],
        [blk, acc, acc, pl.BlockSpec((1, SUBLANES, ns), lambda j, i: (j, 0, 0)),
         pl.BlockSpec((1, LANES), lambda j, i: (0, j))],
        [jax.ShapeDtypeStruct((L, D), F32), jax.ShapeDtypeStruct((nblk, LANES, ns), F32),
         jax.ShapeDtypeStruct((nblk, LANES, ns), F32), jax.ShapeDtypeStruct((nblk, SUBLANES, ns), F32),
         jax.ShapeDtypeStruct((1, D), F32)],
        [pltpu.VMEM((T, ns), F32), pltpu.VMEM((T + SUBLANES, ns), F32), pltpu.VMEM((1, ns), F32)],
        (dg, y, u, xs, xs, cmat_t, bmat_t, rtab, dvec), comm, ("parallel", "arbitrary"))


def _s5_discretize(a_re, a_im, log_step, b_re, b_im):
    lam_re = jnp.minimum(a_re, -S5_MIN_DECAY)
    lam_im = a_im
    delta = jnp.exp(log_step)[:, None]
    mag = jnp.exp(lam_re * delta)
    ang = lam_im * delta
    lb_re = mag * jnp.cos(ang)
    lb_im = mag * jnp.sin(ang)
    den = lam_re * lam_re + lam_im * lam_im
    nr = lb_re - 1.0
    ni = lb_im
    coef_re = (nr * lam_re + ni * lam_im) / den
    coef_im = (ni * lam_re - nr * lam_im) / den
    bb_re = coef_re[..., None] * b_re - coef_im[..., None] * b_im
    bb_im = coef_re[..., None] * b_im + coef_im[..., None] * b_re
    return lb_re, lb_im, bb_re, bb_im


def _s5_tables(lb_re, lb_im):
    nblk = lb_re.shape[0] // S5_BLOCK_GROUPS
    lr = lb_re.reshape(nblk, S5_BLOCK_STATES)
    li = lb_im.reshape(nblk, S5_BLOCK_STATES)
    pows = [(jnp.ones_like(lr), jnp.zeros_like(li))]
    for _ in range(SUBLANES):
        pr, pi = pows[-1]
        pows.append((pr * lr - pi * li, pr * li + pi * lr))
    rows = jnp.arange(SUBLANES)[None, :, None]

    def table(conj, reverse):
        sgn = -1.0 if conj else 1.0
        out = []
        for k in (1, 2, 4):
            mask = (rows <= SUBLANES - 1 - k) if reverse else (rows >= k)
            out.append(jnp.concatenate([jnp.where(mask, pows[k][0][:, None, :], 0.0),
                                        jnp.where(mask, sgn * pows[k][1][:, None, :], 0.0)], axis=-1))
        order = range(SUBLANES, 0, -1) if reverse else range(1, SUBLANES + 1)
        cre = jnp.stack([pows[k][0] for k in order], axis=1)
        cim = jnp.stack([sgn * pows[k][1] for k in order], axis=1)
        out.append(jnp.concatenate([cre, cim], axis=-1))
        return jnp.stack(out, axis=1)

    return table(False, False), table(True, True)


def _s5_block_mats(bb_re, bb_im, c_re, c_im):
    G = bb_re.shape[0]
    nblk = G // S5_BLOCK_GROUPS
    eye = jnp.eye(S5_BLOCK_GROUPS, dtype=F32)
    bb = jnp.stack([bb_re, bb_im]).reshape(2, nblk, S5_BLOCK_GROUPS, S5_STATE, S5_GROUP)
    bmat = jnp.einsum("ab,rjaph->jahrbp", eye, bb).reshape(nblk, LANES, 2 * S5_BLOCK_STATES)
    cc = jnp.stack([c_re, -c_im]).reshape(2, nblk, S5_BLOCK_GROUPS, S5_GROUP, S5_STATE)
    cmat = jnp.einsum("ab,rjahp->jrbpah", eye, cc).reshape(nblk, 2 * S5_BLOCK_STATES, LANES)
    return bmat, cmat


def _s5_unblock(dB, dC, dlam):
    nblk = dB.shape[0]
    G = nblk * S5_BLOCK_GROUPS
    d6 = dB.reshape(nblk, S5_BLOCK_GROUPS, S5_GROUP, 2, S5_BLOCK_GROUPS, S5_STATE)
    dbb = jnp.einsum("jahrap->rjaph", d6).reshape(2, G, S5_STATE, S5_GROUP)
    c6 = dC.reshape(nblk, S5_BLOCK_GROUPS, S5_GROUP, 2, S5_BLOCK_GROUPS, S5_STATE)
    dcc = jnp.einsum("jahrap->rjahp", c6).reshape(2, G, S5_GROUP, S5_STATE)
    dl = jnp.sum(dlam, axis=1).reshape(nblk, 2, S5_BLOCK_GROUPS, S5_STATE)
    dl = jnp.transpose(dl, (1, 0, 2, 3)).reshape(2, G, S5_STATE)
    return dbb[0], dbb[1], dcc[0], -dcc[1], dl[0], dl[1]


def loss_and_grad(y, target, name):
    L, D = y.shape
    tr = _div_tile(L, (512, 256, 128))

    def body(y_ref, t_ref, dy_ref, loss_ref):
        err = y_ref[...] - t_ref[...]
        dy_ref[...] = err * (1.0 / D)
        part = 0.5 * jnp.sum(jnp.mean(err * err, axis=-1, keepdims=True), axis=0, keepdims=True)

        @pl.when(pl.program_id(0) == 0)
        def _():
            loss_ref[...] = part

        @pl.when(pl.program_id(0) > 0)
        def _():
            loss_ref[...] += part

    row = pl.BlockSpec((tr, D), lambda i: (i, 0))
    return pl.pallas_call(
        body, name=name, grid=(L // tr,), in_specs=[row, row],
        out_specs=[row, pl.BlockSpec((1, 1), lambda i: (0, 0))],
        out_shape=[jax.ShapeDtypeStruct((L, D), F32), jax.ShapeDtypeStruct((1, 1), F32)],
        compiler_params=_params(("arbitrary",)),
    )(y, target)


def _adam_math(w, g, m, v):
    m = ADAM_B1 * m + (1.0 - ADAM_B1) * g
    v = ADAM_B2 * v + (1.0 - ADAM_B2) * (g * g)
    m_hat = m / (1.0 - ADAM_B1 ** ADAM_STEP)
    v_hat = v / (1.0 - ADAM_B2 ** ADAM_STEP)
    delta = -ADAM_LR * (m_hat / (jnp.sqrt(v_hat) + ADAM_EPS) + ADAM_WD * w)
    return delta, m, v


def _as3d(a):
    return a.reshape((-1,) + a.shape[-2:])


def adamw(w, g, m, v, name, comm=None):
    shape = w.shape
    w3, g3, m3, v3 = _as3d(w), _as3d(g), _as3d(m), _as3d(v)
    A, R, C = w3.shape
    tr = _row_tile(R, C)

    def body(w_ref, g_ref, m_ref, v_ref, d_ref, mo_ref, vo_ref):
        d, mn, vn = _adam_math(w_ref[...], g_ref[...], m_ref[...], v_ref[...])
        d_ref[...] = d
        mo_ref[...] = mn
        vo_ref[...] = vn

    spec = pl.BlockSpec((1, tr, C), lambda a, i: (a, i, 0))
    outs, arrived = carrier_call(body, name, (A, R // tr), [spec] * 4, [spec] * 3,
                                 [jax.ShapeDtypeStruct((A, R, C), F32)] * 3, [], (w3, g3, m3, v3), comm,
                                 ("parallel", "parallel"))
    return [o.reshape(shape) for o in outs], arrived


def _place():
    x, y, c = lax.axis_index("x"), lax.axis_index("y"), lax.axis_index("c")
    other_chips = [(1 - x, y), (x, 1 - y), (1 - x, 1 - y)]
    return x, y, c, other_chips


def _chip_id(chip):
    return 2 * chip[0] + chip[1]


def _my_chip():
    return 2 * lax.axis_index("x") + lax.axis_index("y")


def _remote(src, dst, send_sem, recv_sem, dev):
    return pltpu.make_async_remote_copy(src_ref=src, dst_ref=dst, send_sem=send_sem, recv_sem=recv_sem,
                                        device_id=dev, device_id_type=MESH)


def allgather_small(arrs):
    T = len(arrs)

    def body(*refs):
        ins, outs = refs[:T], refs[T:2 * T]
        send_sems, recv_sems = refs[2 * T:]
        x, y, c, chips = _place()
        me = _chip_id((x, y))
        sends = []
        for t in range(T):
            outs[t][me] = ins[t][...]
            for j, chip in enumerate(chips):
                cp = _remote(ins[t], outs[t].at[me], send_sems.at[3 * t + j], recv_sems.at[3 * t + j], (*chip, c))
                cp.start()
                sends.append(cp)
        for t in range(T):
            for j, chip in enumerate(chips):
                slot = outs[t].at[_chip_id(chip)]
                _remote(slot, slot, send_sems.at[3 * t + j], recv_sems.at[3 * t + j], (*chip, c)).wait_recv()
        for cp in sends:
            cp.wait_send()

    return pl.pallas_call(
        body, name="allgather_small", in_specs=[VMEM_SPEC] * T, out_specs=[VMEM_SPEC] * T,
        out_shape=[jax.ShapeDtypeStruct((4,) + a.shape, a.dtype) for a in arrs],
        scratch_shapes=[pltpu.SemaphoreType.DMA((3 * T,)), pltpu.SemaphoreType.DMA((3 * T,))],
        compiler_params=pltpu.CompilerParams(vmem_limit_bytes=VMEM_LIMIT, has_side_effects=True),
    )(*arrs)


def small_swap_comm(buf):
    def copy(ins, outs, send_sems, recv_sems):
        x, y, c, _ = _place()
        return _remote(ins[0], outs[0], send_sems.at[0], recv_sems.at[0], (x, y, 1 - c))

    return Comm([buf], [jax.ShapeDtypeStruct(buf.shape, F32)], {}, 1,
                lambda *refs: copy(*refs).start(), None, lambda *refs: copy(*refs).wait())


def small_exchange_comm(pair):
    def copies(ins, outs, send_sems, recv_sems):
        x, y, c, chips = _place()
        return [_remote(ins[0], outs[0].at[j], send_sems.at[j], recv_sems.at[j], (*chip, c))
                for j, chip in enumerate(chips)]

    def begin(*refs):
        for cp in copies(*refs):
            cp.start()

    def finish(*refs):
        for cp in copies(*refs):
            cp.wait()

    return Comm([pair], [jax.ShapeDtypeStruct((3,) + pair.shape, F32)], {}, 3, begin, None, finish)


def small_pair_sum(mine, theirs, name):
    R, C = mine.shape
    tr = _row_tile(R, C)

    def body(a_ref, b_ref, o_ref):
        o_ref[...] = a_ref[...] + b_ref[...]

    spec = pl.BlockSpec((tr, C), lambda i: (i, 0))
    return pl.pallas_call(body, name=name, grid=(R // tr,), in_specs=[spec, spec], out_specs=spec,
                          out_shape=jax.ShapeDtypeStruct((R, C), F32), compiler_params=_params(("parallel",)))(mine, theirs)


def small_chip_sum(pair, got, name):
    R, C = pair.shape
    tr = _row_tile(R, C)

    def body(p_ref, g_ref, o_ref):
        me = _my_chip()
        terms = []
        for chip in range(4):
            d = jnp.bitwise_xor(me, chip)
            terms.append(jnp.where(d == 0, p_ref[...],
                                   jnp.where(d == 2, g_ref[0], jnp.where(d == 1, g_ref[1], g_ref[2]))))
        o_ref[...] = ((terms[0] + terms[1]) + terms[2]) + terms[3]

    spec = pl.BlockSpec((tr, C), lambda i: (i, 0))
    return pl.pallas_call(body, name=name, grid=(R // tr,),
                          in_specs=[spec, pl.BlockSpec((3, tr, C), lambda i: (0, i, 0))], out_specs=spec,
                          out_shape=jax.ShapeDtypeStruct((R, C), F32), compiler_params=_params(("parallel",)))(pair, got)


def _half_rows(ref, layout, shard, half):
    if layout == "S":
        hr = ref.shape[1] // 2
        return ref.at[shard, pl.ds(pl.multiple_of(half * hr, 16), hr), :]
    hr, C = ref.shape[0] // 2, ref.shape[1] // 4
    return ref.at[pl.ds(pl.multiple_of(half * hr, 16), hr), pl.ds(pl.multiple_of(shard * C, LANES), C)]


def _half_rows_all(ref, layout, half):
    if layout == "S":
        hr = ref.shape[1] // 2
        return ref.at[:, pl.ds(pl.multiple_of(half * hr, 16), hr), :]
    hr = ref.shape[0] // 2
    return ref.at[pl.ds(pl.multiple_of(half * hr, 16), hr), :]


def _shard_of_half(ref, layout, shard):
    if layout == "S":
        return ref.at[shard]
    C = ref.shape[1] // 4
    return ref.at[:, pl.ds(pl.multiple_of(shard * C, LANES), C)]


def _own_block_spec(layout, tr, C):
    if layout == "S":
        return pl.BlockSpec((None, tr, C), lambda i: (_my_chip(), i, 0))
    return pl.BlockSpec((tr, C), lambda i: (i, _my_chip()))


def cast_into_gathered(shards, layer, layout, name):
    _, R, C = shards.shape
    tr = _row_tile(R, C)

    def body(a_ref, o_ref):
        o_ref[...] = a_ref[...].astype(BF16)

    return pl.pallas_call(
        body, name=name, grid=(R // tr,),
        in_specs=[pl.BlockSpec((None, tr, C), lambda i: (layer, i, 0))],
        out_specs=_own_block_spec(layout, tr, C),
        out_shape=jax.ShapeDtypeStruct((4, R, C) if layout == "S" else (R, 4 * C), BF16),
        compiler_params=_params(("parallel",)),
    )(shards)


def gather_comm(bufs, layouts):
    T = len(bufs)

    def begin(_, outs, send_sems, recv_sems):
        x, y, c, chips = _place()
        for t in range(T):
            mine = _half_rows(outs[t], layouts[t], _chip_id((x, y)), c)
            for j, chip in enumerate(chips):
                _remote(mine, mine, send_sems.at[6 * t + j], recv_sems.at[6 * t + j], (*chip, c)).start()

    def middle(_, outs, send_sems, recv_sems):
        x, y, c, chips = _place()
        for t in range(T):
            for j, chip in enumerate(chips):
                piece = _half_rows(outs[t], layouts[t], _chip_id(chip), c)
                _remote(piece, piece, send_sems.at[6 * t + j], recv_sems.at[6 * t + j], (*chip, c)).wait_recv()
                _remote(piece, piece, send_sems.at[6 * t + 3 + j], recv_sems.at[6 * t + 3 + j], (x, y, 1 - c)).start()

    def finish(_, outs, send_sems, recv_sems):
        x, y, c, chips = _place()
        for t in range(T):
            mine = _half_rows(outs[t], layouts[t], _chip_id((x, y)), c)
            for j, chip in enumerate(chips):
                theirs = _half_rows(outs[t], layouts[t], _chip_id(chip), 1 - c)
                _remote(theirs, theirs, send_sems.at[6 * t + 3 + j], recv_sems.at[6 * t + 3 + j],
                        (x, y, 1 - c)).wait_recv()
                _remote(mine, mine, send_sems.at[6 * t + j], recv_sems.at[6 * t + j], (*chip, c)).wait_send()
                piece = _half_rows(outs[t], layouts[t], _chip_id(chip), c)
                _remote(piece, piece, send_sems.at[6 * t + 3 + j], recv_sems.at[6 * t + 3 + j],
                        (x, y, 1 - c)).wait_send()

    return Comm(bufs, [jax.ShapeDtypeStruct(b.shape, b.dtype) for b in bufs], {t: t for t in range(T)}, 6 * T,
                begin, middle, finish, middle_frac=0.75)


def swap_comm(grads, layouts):
    T = len(grads)

    def out_shape(g, layout):
        return (4, g.shape[1] // 2, g.shape[2]) if layout == "S" else (g.shape[0] // 2, g.shape[1])

    def copies(ins, outs, send_sems, recv_sems):
        x, y, c, _ = _place()
        return [_remote(_half_rows_all(ins[t], layouts[t], 1 - c), outs[t], send_sems.at[t], recv_sems.at[t],
                        (x, y, 1 - c)) for t in range(T)]

    def begin(*refs):
        for cp in copies(*refs):
            cp.start()

    def finish(*refs):
        for cp in copies(*refs):
            cp.wait()

    return Comm(grads, [jax.ShapeDtypeStruct(out_shape(g, k), F32) for g, k in zip(grads, layouts)], {}, T,
                begin, None, finish)


def pair_sum(grad, recv, layout, name):
    if layout == "S":
        _, hr, C = recv.shape
        tr = _row_tile(hr, C)
        nb = hr // tr
        grid = (4, nb)
        g_spec = pl.BlockSpec((None, tr, C), lambda a, i: (a, lax.axis_index("c") * nb + i, 0))
        spec = pl.BlockSpec((None, tr, C), lambda a, i: (a, i, 0))
    else:
        hr, C = recv.shape
        tr = _row_tile(hr, C)
        nb = hr // tr
        grid = (nb,)
        g_spec = pl.BlockSpec((tr, C), lambda i: (lax.axis_index("c") * nb + i, 0))
        spec = pl.BlockSpec((tr, C), lambda i: (i, 0))

    def body(g_ref, r_ref, f_ref, b_ref):
        s = g_ref[...] + r_ref[...]
        f_ref[...] = s
        b_ref[...] = s.astype(BF16)

    return pl.pallas_call(
        body, name=name, grid=grid, in_specs=[g_spec, spec], out_specs=[spec, spec],
        out_shape=[jax.ShapeDtypeStruct(recv.shape, F32), jax.ShapeDtypeStruct(recv.shape, BF16)],
        compiler_params=_params(("parallel",) * len(grid)),
    )(grad, recv)


def exchange_comm(pair_bf16, layouts):
    T = len(pair_bf16)

    def out_shape(p, layout):
        return (3,) + ((p.shape[1], p.shape[2]) if layout == "S" else (p.shape[0], p.shape[1] // 4))

    def copies(ins, outs, send_sems, recv_sems):
        x, y, c, chips = _place()
        return [_remote(_shard_of_half(ins[t], layouts[t], _chip_id(chip)), outs[t].at[j],
                        send_sems.at[3 * t + j], recv_sems.at[3 * t + j], (*chip, c))
                for t in range(T) for j, chip in enumerate(chips)]

    def begin(*refs):
        for cp in copies(*refs):
            cp.start()

    def finish(*refs):
        for cp in copies(*refs):
            cp.wait()

    return Comm(pair_bf16, [jax.ShapeDtypeStruct(out_shape(p, k), BF16) for p, k in zip(pair_bf16, layouts)], {},
                3 * T, begin, None, finish)


def reduce_total(pair_f32, got, layout, layer, n_layers, previous, name):
    _, hr, C = got.shape
    tr = _row_tile(hr, C)
    nb = hr // tr

    def body(*refs):
        p_ref, g_ref, t_ref = refs[0], refs[1], refs[-1]
        t_ref[...] = ((p_ref[...] + g_ref[0].astype(F32)) + g_ref[1].astype(F32)) + g_ref[2].astype(F32)

    args = [pair_f32, got] + ([previous] if previous is not None else [])
    return pl.pallas_call(
        body, name=name, grid=(nb,),
        in_specs=[_own_block_spec(layout, tr, C), pl.BlockSpec((3, tr, C), lambda i: (0, i, 0))]
        + ([ANY] if previous is not None else []),
        out_specs=pl.BlockSpec((None, tr, C), lambda i: (layer, lax.axis_index("c") * nb + i, 0)),
        out_shape=jax.ShapeDtypeStruct((n_layers, 2 * hr, C), F32),
        input_output_aliases={2: 0} if previous is not None else {},
        compiler_params=_params(("parallel",)),
    )(*args)


def share_comm(reduced):
    T = len(reduced)

    def halves(outs, half):
        return [o.at[:, pl.ds(pl.multiple_of(half * (o.shape[1] // 2), 8), o.shape[1] // 2), :] for o in outs]

    def begin(_, outs, send_sems, recv_sems):
        x, y, c, _p = _place()
        for t, mine in enumerate(halves(outs, c)):
            _remote(mine, mine, send_sems.at[t], recv_sems.at[t], (x, y, 1 - c)).start()

    def finish(_, outs, send_sems, recv_sems):
        x, y, c, _p = _place()
        for t, (mine, theirs) in enumerate(zip(halves(outs, c), halves(outs, 1 - c))):
            _remote(mine, mine, send_sems.at[t], recv_sems.at[t], (x, y, 1 - c)).wait_send()
            _remote(theirs, theirs, send_sems.at[t], recv_sems.at[t], (x, y, 1 - c)).wait_recv()

    return Comm(reduced, [jax.ShapeDtypeStruct(r.shape, r.dtype) for r in reduced], {t: t for t in range(T)}, T,
                begin, None, finish)


def _round_up(n, m):
    return (n + m - 1) // m * m


def _heads_col(a16):
    L = a16.shape[0]
    return jnp.transpose(a16.reshape(L, HEADS // 2, 2), (1, 0, 2))


def _heads_row(a16):
    L = a16.shape[0]
    return jnp.transpose(a16.reshape(L, HEADS // 2, 2), (1, 2, 0))


def local_step(x, target, meta, norm_g, b_f, conv_w, conv_b, s5, s5_d, stager):
    S, D = x.shape
    depth = norm_g.shape[0]
    n_even, n_odd = b_f.shape[0], s5_d.shape[0]
    L = N_META + S
    Lp = _round_up(L, ROW_TILE)
    h = jnp.concatenate([meta, x, jnp.zeros((Lp - L, D), F32)], axis=0)
    conv_c = conv_w.shape[2]
    fg_block = 3 * conv_c // LANES
    saved = []

    def riding(tag, fn, *args):
        comm = stager.ride(tag)
        if comm is None and fn is matmul:
            return fn(*args, name=tag)
        outs, arrived = fn(*args, name=tag, comm=comm)
        stager.arrived(tag, arrived)
        return outs[0] if fn is matmul else outs

    for i in range(depth):
        g = norm_g[i]
        j = i // 2
        tag = "l%d_" % i
        w = stager.weights(i)
        st = {"h0": h, "w": w}
        if i % 2 == 0:
            u = rmsnorm_fwd(h, g[0:1], BF16, tag + "norm0")
            qkv = matmul([(u, w["w_qkv"])], "nn", BF16, tag + "qkv")
            rest = matmul([(u, w["w_rest"])], "nn", F32, tag + "rest")
            cgate = gate_fwd(rest, fg_block, b_f[j], tag + "gate")
            c16 = cgate[:, :HEADS]
            attn, lse = riding(tag + "attn", attn_fwd, qkv, _heads_col(c16), _heads_row(c16))
            convo = conv_fwd(rest, 0, conv_w[j], conv_b[j], tag + "conv")
            cat = jnp.concatenate([attn, convo], axis=1)
            m = matmul([(cat, w["w_o"])], "nn", F32, tag + "wo")
            st.update(u=u, qkv=qkv, rest=rest, c16=c16, lse=lse, cat=cat)
        else:
            p = s5[j]
            u = rmsnorm_fwd(h, g[0:1], F32, tag + "norm0")
            y, gact, xs = riding(tag + "s5", s5_fwd, u, p["bmat"], p["cmat"], p["tab"], s5_d[j])
            o1, o2, m = riding(tag + "glu", dual_matmul_act, gact, w["w_glu1"], w["w_glu2"], "glu", F32)
            st.update(u=u, y=y, gact=gact, xs=xs, o1=o1, o2=o2)
        h1 = rmsnorm_fwd(m, g[1:2], F32, tag + "norm1", residual=h)
        u2 = rmsnorm_fwd(h1, g[2:3], BF16, tag + "norm2")
        a, b, hact = riding(tag + "ffn_in", dual_matmul_act, u2, w["w_gate"], w["w_up"], "swiglu", BF16)
        f = riding(tag + "ffn_out", matmul, [(hact, w["w_down"])], "nn", F32)
        h = rmsnorm_fwd(f, g[3:4], F32, tag + "norm3", residual=h1)
        st.update(m=m, h1=h1, u2=u2, a=a, b=b, hact=hact, f=f)
        saved.append(st)

    dy, loss = loss_and_grad(h[N_META:L], target, "loss")
    dh = jnp.concatenate([jnp.zeros((N_META, D), F32), dy, jnp.zeros((Lp - L, D), F32)], axis=0)

    grads = {k: [None] * n_even for k in ("b_f", "conv_w", "conv_b")}
    grads.update({k: [None] * n_odd for k in ("s5_d", "s5_dB", "s5_dC", "s5_dlam")})
    grads["norm_g"] = [None] * depth

    for i in reversed(range(depth)):
        g = norm_g[i]
        j = i // 2
        tag = "l%d_b_" % i
        st = saved[i]
        w = st["w"]
        wg = {}
        df, dg3 = rmsnorm_bwd(st["f"], g[3:4], dh, BF16, tag + "norm3")
        wg["w_down"] = matmul([(st["hact"], df)], "tn", F32, tag + "dw_down")
        da, db = riding(tag + "ffn_act", ffn_bwd_act, df, w["w_down"], st["a"], st["b"])
        u2t = st["u2"].T
        wg["w_gate"] = matmul([(u2t, da)], "nn", F32, tag + "dw_gate")
        wg["w_up"] = matmul([(u2t, db)], "nn", F32, tag + "dw_up")
        stager.put_grads(i, wg)
        wg = {}
        du2 = riding(tag + "du2", matmul, [(da, w["w_gate"])], "nt", F32)
        du2 = matmul([(db, w["w_up"])], "nt", F32, tag + "du2_up", add=du2)
        dh1, dg2 = rmsnorm_bwd(st["h1"], g[2:3], du2, F32, tag + "norm2", add=dh)
        if i % 2 == 0:
            dm, dg1 = rmsnorm_bwd(st["m"], g[1:2], dh1, BF16, tag + "norm1")
            wg["w_o"] = matmul([(st["cat"], dm)], "tn", F32, tag + "dw_o")
            stager.put_grads(i, wg)
            dcat = riding(tag + "dcat", matmul, [(dm, w["w_o"])], "nt", F32)
            delta = attn_delta(dcat, st["cat"], tag + "delta")
            c16 = st["c16"]
            lse16 = jnp.transpose(st["lse"], (1, 0, 2)).reshape(Lp, HEADS)
            dq, dk, dv, dcq, dck = riding(tag + "attn", attn_bwd, st["qkv"], dcat, _heads_row(lse16),
                                          _heads_row(delta[:, :HEADS]), _heads_row(c16), _heads_col(c16))
            dc16 = (jnp.transpose(dcq, (2, 0, 1)).reshape(Lp, HEADS)
                    + jnp.transpose(dck, (1, 0, 2)).reshape(Lp, HEADS))
            dc = jnp.pad(dc16, ((0, 0), (0, LANES - HEADS)))
            dfg, dbf = gate_bwd(st["rest"], fg_block, b_f[j], dc, tag + "gate")
            dgb, dgc, dxc, dcw, dcb = conv_bwd(st["rest"], 0, conv_w[j], conv_b[j], dcat, ATTN_W // LANES,
                                               tag + "conv")
            dqkv = jnp.concatenate([dq, dk, dv], axis=1)
            drest = jnp.concatenate([dgb, dgc, dxc, dfg.astype(BF16)], axis=1)
            wg["w_qkv"] = matmul([(st["u"], dqkv)], "tn", F32, tag + "dw_qkv")
            wg["w_rest"] = matmul([(st["u"], drest)], "tn", F32, tag + "dw_rest")
            stager.put_grads(i, wg)
            du = riding(tag + "du_qkv", matmul, [(dqkv, w["w_qkv"])], "nt", F32)
            du_b = riding(tag + "du_rest", matmul, [(drest, w["w_rest"])], "nt", F32)
            grads["b_f"][j], grads["conv_w"][j], grads["conv_b"][j] = dbf, dcw, dcb
        else:
            p = s5[j]
            dmix, dg1 = rmsnorm_bwd(st["m"], g[1:2], dh1, F32, tag + "norm1")
            do1, do2 = glu_bwd_act(dmix, st["o1"], st["o2"], tag + "glu_act")
            wg["w_glu1"] = matmul([(st["gact"], do1)], "tn", F32, tag + "dw_glu1")
            wg["w_glu2"] = matmul([(st["gact"], do2)], "tn", F32, tag + "dw_glu2")
            dgact = matmul([(do1, w["w_glu1"]), (do2, w["w_glu2"])], "nt", F32, tag + "dgact")
            du, dC, dB, dlam, dd = riding(tag + "s5", s5_bwd, dgact, st["y"], st["u"], st["xs"], p["cmat_t"],
                                          p["bmat_t"], p["rtab"], s5_d[j])
            du_b = None
            grads["s5_dB"][j], grads["s5_dC"][j], grads["s5_dlam"][j], grads["s5_d"][j] = dB, dC, dlam, dd
        dh, dg0 = rmsnorm_bwd(st["h0"], g[0:1], du, F32, tag + "norm0", add=dh1, dy2=du_b)
        grads["norm_g"][i] = jnp.concatenate([dg0, dg1, dg2, dg3], axis=0)
        stager.put_grads(i, wg)

    grads["meta"] = dh[:N_META]
    return loss, dh[N_META:L], grads


def _packed_rows(shape):
    return _round_up(_round_up(math.prod(shape), LANES) // LANES, SUBLANES)


def _pack(arrs):
    rows = []
    for a in arrs:
        flat = a.reshape(-1).astype(F32)
        r = _packed_rows(a.shape)
        rows.append(jnp.pad(flat, (0, r * LANES - flat.shape[0])).reshape(r, LANES))
    return jnp.concatenate(rows, axis=0)


def _unpack(buf, shapes):
    buf = buf.reshape(-1, LANES)
    out, off = [], 0
    for s in shapes:
        r = _packed_rows(s)
        out.append(buf[off:off + r].reshape(-1)[:math.prod(s)].reshape(s))
        off += r
    return out


class _LayerWeights:
    def __init__(self, stager, layer):
        self.stager, self.layer = stager, layer

    def __getitem__(self, name):
        return self.stager.weight(self.layer, name)


class MeshStager:
    LAYOUT = {"ab_w_in": "S", "ab_w_o": "S", "s5_w_glu1": "S", "s5_w_glu2": "S",
              "ffn_w_gate": "C", "ffn_w_up": "C", "ffn_w_down": "S"}
    EVEN = ("ab_w_in", "ab_w_o", "ffn_w_gate", "ffn_w_up", "ffn_w_down")
    ODD = ("s5_w_glu1", "s5_w_glu2", "ffn_w_gate", "ffn_w_up", "ffn_w_down")

    def __init__(self, shards):
        self.shards = shards
        self.depth = depth = shards["ffn_w_gate"].shape[0]
        self.bufs = {}
        for i in range(depth):
            for k in self.keys(i):
                self.bufs[k, i] = cast_into_gathered(shards[k], self.index(k, i), self.LAYOUT[k],
                                                     "cast_%s_%d" % (k, i))
        self.grads, self.pairs, self.reduced, self.ready = {}, {}, {}, {}
        first = [("ab_w_in", 0)]
        plan = self.gather_plan = {"l0_attn": [it for it in self.stage(0) if it not in first]}
        for o in range(1, depth, 2):
            e = o - 1
            plan.setdefault("l%d_attn" % e, []).append(("ffn_w_gate", o))
            plan["l%d_ffn_in" % e] = [("s5_w_glu1", o), ("ffn_w_up", o)]
            plan["l%d_ffn_out" % e] = [("s5_w_glu2", o)]
            plan["l%d_s5" % o] = [("ffn_w_down", o)] + (self.mix(o + 1) if o + 1 < depth else [])
            if o + 1 < depth:
                plan["l%d_ffn_in" % o] = [("ffn_w_gate", o + 1)]
                plan["l%d_attn" % (o + 1)] = [("ffn_w_up", o + 1), ("ffn_w_down", o + 1)]
        self.swap_plan, self.exchange_plan = {}, {}
        for i in range(depth):
            above = self.mix(i + 1) if i + 1 < depth else []
            if above:
                self.swap_plan["l%d_b_ffn_act" % i] = above
            self.swap_plan["l%d_b_du2" % i] = self.ffn(i)
            self.exchange_plan["l%d_b_%s" % (i, "attn" if i % 2 == 0 else "s5")] = above + self.ffn(i)
        self.swap_plan["l0_b_dcat"] = [("ab_w_o", 0)]
        self.exchange_plan["l0_b_attn"].append(("ab_w_o", 0))
        self.swap_plan["l0_b_du_qkv"] = [("ab_w_in", 0)]
        self.exchange_plan["l0_b_du_rest"] = [("ab_w_in", 0)]
        self._store(first, comm_call("gather_first", self._gather(first, 0.5)))

    def keys(self, i):
        return self.EVEN if i % 2 == 0 else self.ODD

    def stage(self, i):
        return [(k, i) for k in self.keys(i)]

    def mix(self, i):
        return [(k, i) for k in self.keys(i) if not k.startswith("ffn")]

    @staticmethod
    def ffn(i):
        return [("ffn_w_gate", i), ("ffn_w_up", i), ("ffn_w_down", i)]

    @staticmethod
    def index(key, i):
        return i if key.startswith("ffn") else i // 2

    def _layouts(self, items):
        return [self.LAYOUT[k] for k, _ in items]

    def _gather(self, items, middle_frac):
        comm = gather_comm([self.bufs[it] for it in items], self._layouts(items))
        comm.middle_frac = middle_frac
        return comm

    def _store(self, items, bufs):
        for it, b in zip(items, bufs):
            self.bufs[it] = b

    def ride(self, tag):
        if tag in self.gather_plan:
            return self._gather(self.gather_plan[tag], 0.85 if tag == "l0_attn" else 0.7)
        if tag in self.swap_plan:
            items = self.swap_plan[tag]
            return swap_comm([self.grads[it] for it in items], self._layouts(items))
        if tag in self.exchange_plan:
            items = self.exchange_plan[tag]
            return exchange_comm([self.pairs[it][1] for it in items], self._layouts(items))
        return None

    def arrived(self, tag, outs):
        if tag in self.gather_plan:
            self._store(self.gather_plan[tag], outs)
        elif tag in self.swap_plan:
            self._pair_sums(self.swap_plan[tag], outs)
        elif tag in self.exchange_plan:
            self._totals(self.exchange_plan[tag], outs)

    def _pair_sums(self, items, received):
        for it, r in zip(items, received):
            self.pairs[it] = pair_sum(self.grads[it], r, self.LAYOUT[it[0]], "pair_sum_%s_%d" % it)

    def _totals(self, items, got):
        for it, g in zip(items, got):
            k, i = it
            self.reduced[k] = reduce_total(self.pairs[it][0], g, self.LAYOUT[k], self.index(k, i),
                                           self.shards[k].shape[0], self.reduced.get(k), "reduce_total_%s_%d" % it)

    def weights(self, i):
        return _LayerWeights(self, i)

    def weight(self, i, name):
        if (name, i) not in self.ready:
            if name in ("w_qkv", "w_rest"):
                b = self.bufs["ab_w_in", i]
                w_in = jnp.transpose(b, (1, 0, 2)).reshape(b.shape[1], 4 * b.shape[2])
                self.ready["w_qkv", i], self.ready["w_rest", i] = split_w_in(w_in)
            else:
                k = {"w_o": "ab_w_o", "w_glu1": "s5_w_glu1", "w_glu2": "s5_w_glu2"}.get(name, "ffn_" + name)
                b = self.bufs[k, i]
                self.ready[name, i] = b.reshape(4 * b.shape[1], b.shape[2]) if self.LAYOUT[k] == "S" else b
        return self.ready[name, i]

    def put_grads(self, i, wg):
        for k in self.keys(i):
            _, R, C = self.shards[k].shape
            name = {"ab_w_in": "w_qkv", "ab_w_o": "w_o", "s5_w_glu1": "w_glu1", "s5_w_glu2": "w_glu2"}.get(k, k[4:])
            if name not in wg:
                continue
            if k == "ab_w_in":
                dw = merge_dw_in(wg["w_qkv"], wg["w_rest"])
                self.grads[k, i] = jnp.transpose(dw.reshape(R, 4, C), (1, 0, 2))
            else:
                self.grads[k, i] = wg[name].reshape(4, R, C) if self.LAYOUT[k] == "S" else wg[name]

    def finish(self):
        names = list(self.LAYOUT)
        return dict(zip(names, comm_call("share_reduced", share_comm([self.reduced[k] for k in names]))))


def split_w_in(w_in):
    fg0 = 3 * ATTN_W
    w_rest = jnp.concatenate([w_in[:, fg0 + HEADS:], w_in[:, fg0:fg0 + HEADS],
                              jnp.zeros((w_in.shape[0], LANES - HEADS), w_in.dtype)], axis=1)
    return w_in[:, :fg0], w_rest


def merge_dw_in(dw_qkv, dw_rest):
    nqc = dw_rest.shape[1] - LANES
    return jnp.concatenate([dw_qkv, dw_rest[:, nqc:nqc + HEADS], dw_rest[:, :nqc]], axis=1)


def device_step(x, target, P, stager):
    D = x.shape[-1]
    n_even, n_odd = P["ab_b_f"].shape[0], P["s5_d"].shape[0]
    conv_c = P["ab_conv_b"].shape[1]
    b_f_pad = jnp.pad(P["ab_b_f"], ((0, 0), (0, LANES - HEADS))).reshape(n_even, 1, LANES)

    s5, s5_vjps = [], []
    for j in range(n_odd):
        disc, vjp = jax.vjp(_s5_discretize, P["s5_a_re"][j], P["s5_a_im"][j], P["s5_log_step"][j],
                            P["s5_b_re"][j], P["s5_b_im"][j])
        lb_re, lb_im, bb_re, bb_im = disc
        tab, rtab = _s5_tables(lb_re, lb_im)
        bmat, cmat = _s5_block_mats(bb_re, bb_im, P["s5_c_re"][j], P["s5_c_im"][j])
        s5.append(dict(tab=tab, rtab=rtab, bmat=bmat.astype(BF16), cmat=cmat.astype(BF16),
                       bmat_t=jnp.transpose(bmat, (0, 2, 1)).astype(BF16),
                       cmat_t=jnp.transpose(cmat, (0, 2, 1)).astype(BF16)))
        s5_vjps.append(vjp)

    loss, grad_x, G = local_step(
        x, target, P["meta_tokens"], P["norm_g"], b_f_pad, P["ab_conv_w"],
        P["ab_conv_b"].reshape(n_even, 1, conv_c), s5, P["s5_d"].reshape(n_odd, 1, D), stager)

    out = {
        "meta_tokens": G["meta"],
        "norm_g": jnp.stack(G["norm_g"]),
        "ab_b_f": jnp.stack([b[0, :HEADS] for b in G["b_f"]]),
        "ab_conv_w": jnp.stack(G["conv_w"]),
        "ab_conv_b": jnp.stack([b[0] for b in G["conv_b"]]),
        "s5_d": jnp.stack([d[0] for d in G["s5_d"]]),
    }
    s5g = {k: [] for k in ("s5_a_re", "s5_a_im", "s5_log_step", "s5_b_re", "s5_b_im", "s5_c_re", "s5_c_im")}
    for j in range(n_odd):
        dbb_re, dbb_im, dc_re, dc_im, dl_re, dl_im = _s5_unblock(G["s5_dB"][j], G["s5_dC"][j], G["s5_dlam"][j])
        da_re, da_im, dls, db_re, db_im = s5_vjps[j]((dl_re, dl_im, dbb_re, dbb_im))
        for k, val in zip(s5g, (da_re, da_im, dls, db_re, db_im, dc_re, dc_im)):
            s5g[k].append(val)
    out.update({k: jnp.stack(v) for k, v in s5g.items()})
    return loss, grad_x, out


def kernel(x, meta_tokens, norm_g, ab_w_in, ab_b_f, ab_conv_w, ab_conv_b, ab_w_o, s5_a_re, s5_a_im, s5_log_step, s5_b_re, s5_b_im, s5_c_re, s5_c_im, s5_d, s5_w_glu1, s5_w_glu2, ffn_w_gate, ffn_w_up, ffn_w_down, loss_target, m_meta_tokens, m_norm_g, m_ab_w_in, m_ab_b_f, m_ab_conv_w, m_ab_conv_b, m_ab_w_o, m_s5_a_re, m_s5_a_im, m_s5_log_step, m_s5_b_re, m_s5_b_im, m_s5_c_re, m_s5_c_im, m_s5_d, m_s5_w_glu1, m_s5_w_glu2, m_ffn_w_gate, m_ffn_w_up, m_ffn_w_down, v_meta_tokens, v_norm_g, v_ab_w_in, v_ab_b_f, v_ab_conv_w, v_ab_conv_b, v_ab_w_o, v_s5_a_re, v_s5_a_im, v_s5_log_step, v_s5_b_re, v_s5_b_im, v_s5_c_re, v_s5_c_im, v_s5_d, v_s5_w_glu1, v_s5_w_glu2, v_ffn_w_gate, v_ffn_w_up, v_ffn_w_down):
    names = ["meta_tokens", "norm_g", "ab_w_in", "ab_b_f", "ab_conv_w", "ab_conv_b", "ab_w_o", "s5_a_re", "s5_a_im",
             "s5_log_step", "s5_b_re", "s5_b_im", "s5_c_re", "s5_c_im", "s5_d", "s5_w_glu1", "s5_w_glu2",
             "ffn_w_gate", "ffn_w_up", "ffn_w_down"]
    W = dict(zip(names, [meta_tokens, norm_g, ab_w_in, ab_b_f, ab_conv_w, ab_conv_b, ab_w_o, s5_a_re, s5_a_im,
                         s5_log_step, s5_b_re, s5_b_im, s5_c_re, s5_c_im, s5_d, s5_w_glu1, s5_w_glu2,
                         ffn_w_gate, ffn_w_up, ffn_w_down]))
    Mo = dict(zip(names, [m_meta_tokens, m_norm_g, m_ab_w_in, m_ab_b_f, m_ab_conv_w, m_ab_conv_b, m_ab_w_o, m_s5_a_re,
                          m_s5_a_im, m_s5_log_step, m_s5_b_re, m_s5_b_im, m_s5_c_re, m_s5_c_im, m_s5_d, m_s5_w_glu1,
                          m_s5_w_glu2, m_ffn_w_gate, m_ffn_w_up, m_ffn_w_down]))
    Vo = dict(zip(names, [v_meta_tokens, v_norm_g, v_ab_w_in, v_ab_b_f, v_ab_conv_w, v_ab_conv_b, v_ab_w_o, v_s5_a_re,
                          v_s5_a_im, v_s5_log_step, v_s5_b_re, v_s5_b_im, v_s5_c_re, v_s5_c_im, v_s5_d, v_s5_w_glu1,
                          v_s5_w_glu2, v_ffn_w_gate, v_ffn_w_up, v_ffn_w_down]))
    D = x.shape[-1]
    n_even, n_odd, depth = ab_w_in.shape[0], s5_w_glu1.shape[0], ffn_w_gate.shape[0]
    chip = 2 * lax.axis_index("x") + lax.axis_index("y")

    big = list(MeshStager.LAYOUT)
    stager = MeshStager({k: W[k] for k in big})
    g_meta, g_norm, g_convw, g_s5d = allgather_small([meta_tokens, norm_g, ab_conv_w, s5_d])
    full = {k: W[k] for k in names if k not in big}
    full["meta_tokens"] = jnp.transpose(g_meta, (1, 0, 2)).reshape(N_META, D)
    full["norm_g"] = jnp.transpose(g_norm, (1, 2, 0, 3)).reshape(depth, 4, D)
    full["ab_conv_w"] = jnp.transpose(g_convw, (1, 2, 0, 3)).reshape(n_even, CONV_K, -1)
    full["s5_d"] = jnp.transpose(g_s5d, (1, 0, 2)).reshape(n_odd, D)

    loss, grad_x, G = device_step(x[0], loss_target[0], full, stager)
    reduced = stager.finish()

    small_w = [k for k in names if k not in big]
    small_names = ["loss"] + small_w
    G["loss"] = loss
    packed = _pack([G[k] for k in small_names])
    grad, delta, new_m, new_v = {}, {}, {}, {}
    pair = total = None
    for k in big:
        comm = small_swap_comm(packed) if k == "ffn_w_gate" else small_exchange_comm(pair) if k == "ffn_w_up" else None
        (delta[k], new_m[k], new_v[k]), arrived = adamw(W[k], reduced[k], Mo[k], Vo[k], "adamw_" + k, comm=comm)
        grad[k] = reduced[k]
        if k == "ffn_w_gate":
            pair = small_pair_sum(packed, arrived[0], "small_pair_sum")
        elif k == "ffn_w_up":
            total = small_chip_sum(pair, arrived[0], "small_chip_sum")
    summed = dict(zip(small_names, _unpack(total, [G[k].shape for k in small_names])))
    loss_out = summed["loss"].reshape(())
    for k in ("meta_tokens", "norm_g", "ab_conv_w", "s5_d"):
        n_last = W[k].shape[-1]
        summed[k] = lax.dynamic_slice_in_dim(summed[k], chip * n_last, n_last, axis=summed[k].ndim - 1)
    shapes = [W[k].shape for k in small_w]
    (d_s, m_s, v_s), _ = adamw(_pack([W[k] for k in small_w])[None], _pack([summed[k] for k in small_w])[None],
                               _pack([Mo[k] for k in small_w])[None], _pack([Vo[k] for k in small_w])[None],
                               "adamw_small")
    delta.update(zip(small_w, _unpack(d_s, shapes)))
    new_m.update(zip(small_w, _unpack(m_s, shapes)))
    new_v.update(zip(small_w, _unpack(v_s, shapes)))
    grad.update({k: summed[k] for k in small_w})

    return (loss_out, grad_x[None], *[grad[k] for k in names], *[delta[k] for k in names],
            *[new_m[k] for k in names], *[new_v[k] for k in names])
```

```python
import functools
import math

import jax
import jax.numpy as jnp
from jax import lax
from jax.experimental import pallas as pl
from jax.experimental.pallas import tpu as pltpu

F32 = jnp.float32
BF16 = jnp.bfloat16

N_META = 16
HEADS = 16
HEAD_DIM = 64
ATTN_W = HEADS * HEAD_DIM
CONV_K = 3
S5_GROUP = 16
S5_STATE = 64
S5_MIN_DECAY = 1e-4
NORM_EPS = 1e-6
ADAM_LR = 0.001
ADAM_B1 = 0.9
ADAM_B2 = 0.999
ADAM_EPS = 1e-08
ADAM_WD = 0.01
ADAM_STEP = 10

LANES = 128
SUBLANES = 8
VMEM_LIMIT = 56 * 1024 * 1024
VMEM_TILE_BUDGET = 34 * 1024 * 1024
ROW_TILE = 384
ATTN_ROWS = 128
S5_BLOCK_GROUPS = LANES // S5_GROUP
S5_BLOCK_STATES = S5_BLOCK_GROUPS * S5_STATE
NEG_BIG = -1e30

MESH = pl.DeviceIdType.MESH
ANY = pl.BlockSpec(memory_space=pl.ANY)
VMEM_SPEC = pl.BlockSpec(memory_space=pltpu.VMEM)


def _params(sem=None):
    return pltpu.CompilerParams(dimension_semantics=sem, vmem_limit_bytes=VMEM_LIMIT)


def _div_tile(n, prefs):
    for p in prefs:
        if n % p == 0:
            return p
    return n


def _row_tile(rows, cols, itemsize=4, limit=2 * 1024 * 1024):
    for p in (512, 256, 128, 64, 32, 16):
        if rows % p == 0 and p * cols * itemsize <= limit:
            return p
    return 16 if rows % 16 == 0 else rows


def _tile_cands(n):
    c = [d for d in range(LANES, min(n, 2048) + 1, LANES) if n % d == 0]
    if not c or n <= 2048 and n not in c:
        c.append(n)
    return sorted(set(c), reverse=True)


def _mm_tiles(M, N, K, a_bytes, b_bytes, o_bytes, npairs):
    best = None
    for tk in sorted(set(_tile_cands(K) + [K]), reverse=True):
        for tm in _tile_cands(M):
            for tn in _tile_cands(N):
                mem = npairs * 2 * (tm * tk * a_bytes + tk * tn * b_bytes) + 2 * tm * tn * o_bytes + tm * tn * 4
                mem += npairs * ((tm * tk * 2 if a_bytes == 4 else 0) + (tk * tn * 2 if b_bytes == 4 else 0))
                if mem > VMEM_TILE_BUDGET:
                    continue
                key = (tk == K and tm >= 3 * LANES and tn >= 4 * LANES, tm * tn * tk, tk, tn)
                if best is None or key > best[0]:
                    best = (key, (tm, tn, tk))
    assert best is not None, (M, N, K)
    return best[1]


class Comm:
    def __init__(self, operands, out_shapes, aliases, n_sems, begin, middle=None, finish=None, middle_frac=0.5):
        self.operands, self.out_shapes, self.aliases, self.n_sems = list(operands), list(out_shapes), aliases, n_sems
        self.begin, self.middle, self.finish, self.middle_frac = begin, middle, finish, middle_frac


class _Shifted:
    def __init__(self, sems, off):
        self.sems, self.off = sems, off

    @property
    def at(self):
        return self

    def __getitem__(self, i):
        return self.sems.at[self.off + i]


def merge_comms(a, b):
    assert a.middle is None and b.middle is None
    na_in, na_out = len(a.operands), len(a.out_shapes)

    def both(stage):
        def run(ins, outs, send_sems, recv_sems):
            getattr(a, stage)(ins[:na_in], outs[:na_out], send_sems, recv_sems)
            getattr(b, stage)(ins[na_in:], outs[na_out:], _Shifted(send_sems, a.n_sems), _Shifted(recv_sems, a.n_sems))
        return run

    aliases = dict(a.aliases)
    aliases.update({na_in + i: na_out + o for i, o in b.aliases.items()})
    return Comm(a.operands + b.operands, a.out_shapes + b.out_shapes, aliases, a.n_sems + b.n_sems,
                both("begin"), None, both("finish"))


def carrier_call(body, name, grid, in_specs, out_specs, out_shape, scratch_shapes, args, comm, semantics):
    n_in, n_out = len(args), len(out_shape)
    if comm is None:
        outs = pl.pallas_call(body, name=name, grid=grid, in_specs=in_specs, out_specs=out_specs, out_shape=out_shape,
                              scratch_shapes=scratch_shapes, compiler_params=_params(semantics))(*args)
        return list(outs), []
    ci, co = len(comm.operands), len(comm.out_shapes)
    total = math.prod(grid)
    middle_at = min(total - 1, max(0, int(total * comm.middle_frac)))

    def carried(*refs):
        ins, cins = refs[:n_in], refs[n_in:n_in + ci]
        outs = refs[n_in + ci:n_in + ci + n_out]
        couts = refs[n_in + ci + n_out:n_in + ci + n_out + co]
        scratch, (send_sems, recv_sems) = refs[n_in + ci + n_out + co:-2], refs[-2:]
        step = 0
        for d, size in enumerate(grid):
            step = step * size + pl.program_id(d)

        @pl.when(step == 0)
        def _():
            comm.begin(cins, couts, send_sems, recv_sems)

        if comm.middle is not None:
            @pl.when(step == middle_at)
            def _():
                comm.middle(cins, couts, send_sems, recv_sems)

        body(*ins, *outs, *scratch)

        @pl.when(step == total - 1)
        def _():
            comm.finish(cins, couts, send_sems, recv_sems)

    outs = pl.pallas_call(
        carried, name=name, grid=grid,
        in_specs=list(in_specs) + [ANY] * ci, out_specs=list(out_specs) + [ANY] * co,
        out_shape=list(out_shape) + comm.out_shapes,
        scratch_shapes=list(scratch_shapes) + [pltpu.SemaphoreType.DMA((comm.n_sems,)),
                                                pltpu.SemaphoreType.DMA((comm.n_sems,))],
        input_output_aliases={n_in + i: n_out + o for i, o in comm.aliases.items()},
        compiler_params=pltpu.CompilerParams(dimension_semantics=("arbitrary",) * len(grid),
                                             vmem_limit_bytes=VMEM_LIMIT, has_side_effects=True),
    )(*args, *comm.operands)
    return list(outs[:n_out]), list(outs[n_out:])


def comm_call(name, comm):
    ci = len(comm.operands)

    def body(*refs):
        cins, couts = refs[:ci], refs[ci:ci + len(comm.out_shapes)]
        send_sems, recv_sems = refs[-2:]
        comm.begin(cins, couts, send_sems, recv_sems)
        if comm.middle is not None:
            comm.middle(cins, couts, send_sems, recv_sems)
        comm.finish(cins, couts, send_sems, recv_sems)

    return pl.pallas_call(
        body, name=name, in_specs=[ANY] * ci, out_specs=[ANY] * len(comm.out_shapes), out_shape=comm.out_shapes,
        input_output_aliases=dict(comm.aliases),
        scratch_shapes=[pltpu.SemaphoreType.DMA((comm.n_sems,)), pltpu.SemaphoreType.DMA((comm.n_sems,))],
        compiler_params=pltpu.CompilerParams(has_side_effects=True),
    )(*comm.operands)


_DIMS ={"nn": (((1,), (0,)), ((), ())), "nt": (((1,), (1,)), ((), ())), "tn": (((0,), (0,)), ((), ()))}


def matmul(pairs, kind, out_dtype, name, comm=None, add=None):
    a0, b0 = pairs[0]
    if kind == "nn":
        (M, K), N = a0.shape, b0.shape[1]
    elif kind == "nt":
        (M, K), N = a0.shape, b0.shape[0]
    else:
        (K, M), N = a0.shape, b0.shape[1]
    tm, tn, tk = _mm_tiles(M, N, K, a0.dtype.itemsize, b0.dtype.itemsize, jnp.dtype(out_dtype).itemsize, len(pairs))
    nk = K // tk
    dims = _DIMS[kind]
    npairs = len(pairs)
    n_in = 2 * npairs + (add is not None)

    def body(*refs):
        ins, o_ref = refs[:2 * npairs], refs[n_in]
        part = None
        for p in range(npairs):
            d = lax.dot_general(ins[2 * p][...].astype(BF16), ins[2 * p + 1][...].astype(BF16), dims,
                                preferred_element_type=F32)
            part = d if part is None else part + d

        def finish(total):
            if add is not None:
                total = total + refs[2 * npairs][...]
            o_ref[...] = total.astype(o_ref.dtype)

        if nk == 1:
            finish(part)
        else:
            acc_ref = refs[n_in + 1]
            k = pl.program_id(2)

            @pl.when(k == 0)
            def _():
                acc_ref[...] = part

            @pl.when(k > 0)
            def _():
                acc_ref[...] += part

            @pl.when(k == nk - 1)
            def _():
                finish(acc_ref[...])

    if kind == "nn":
        a_blk, a_map = (tm, tk), lambda j, i, k: (i, k)
        b_blk, b_map = (tk, tn), lambda j, i, k: (k, j)
    elif kind == "nt":
        a_blk, a_map = (tm, tk), lambda j, i, k: (i, k)
        b_blk, b_map = (tn, tk), lambda j, i, k: (j, k)
    else:
        a_blk, a_map = (tk, tm), lambda j, i, k: (k, i)
        b_blk, b_map = (tk, tn), lambda j, i, k: (k, j)
    o_spec = pl.BlockSpec((tm, tn), lambda j, i, k: (i, j))
    (out,), arrived = carrier_call(
        body, name, (N // tn, M // tm, nk),
        [pl.BlockSpec(a_blk, a_map), pl.BlockSpec(b_blk, b_map)] * npairs + ([o_spec] if add is not None else []),
        [o_spec], [jax.ShapeDtypeStruct((M, N), out_dtype)],
        [] if nk == 1 else [pltpu.VMEM((tm, tn), F32)],
        [t for ab in pairs for t in ab] + ([add] if add is not None else []), comm,
        ("parallel", "parallel", "arbitrary"))
    return out if comm is None else ([out], arrived)


def _sigmoid(x):
    return 1.0 / (1.0 + jnp.exp(-x))

def dual_matmul_act(x, w1, w2, act, out_dtype, name, comm=None):
    M, K = x.shape
    N = w1.shape[-1]
    tm = _div_tile(M, (ROW_TILE,))
    tn = _div_tile(N, (1408, 1024, 512, 256, 128))

    def body(x_ref, w1_ref, w2_ref, o1_ref, o2_ref, out_ref):
        xv = x_ref[...]
        o1 = jnp.dot(xv, w1_ref[...], preferred_element_type=F32)
        o2 = jnp.dot(xv, w2_ref[...], preferred_element_type=F32)
        o1_ref[...] = o1.astype(BF16)
        o2_ref[...] = o2.astype(BF16)
        if act == "swiglu":
            out = o1 * _sigmoid(o1) * o2
        else:
            out = o1 * _sigmoid(o2)
        out_ref[...] = out.astype(out_ref.dtype)

    w_spec = pl.BlockSpec((K, tn), lambda j, i: (0, j))
    o_spec = pl.BlockSpec((tm, tn), lambda j, i: (i, j))
    return carrier_call(
        body, name, (N // tn, M // tm), [pl.BlockSpec((tm, K), lambda j, i: (i, 0)), w_spec, w_spec],
        [o_spec, o_spec, o_spec],
        [jax.ShapeDtypeStruct((M, N), BF16), jax.ShapeDtypeStruct((M, N), BF16),
         jax.ShapeDtypeStruct((M, N), out_dtype)], [], (x, w1, w2), comm, ("parallel", "parallel"))


def ffn_bwd_act(df, wd, a, b, name, comm=None):
    M, K = df.shape
    N = wd.shape[0]
    tm = _div_tile(M, (ROW_TILE,))
    tn = _div_tile(N, (1408, 1024, 512, 256, 128))

    def body(df_ref, wd_ref, a_ref, b_ref, da_ref, db_ref):
        dh = lax.dot_general(df_ref[...], wd_ref[...], _DIMS["nt"], preferred_element_type=F32)
        av = a_ref[...].astype(F32)
        bv = b_ref[...].astype(F32)
        sig = _sigmoid(av)
        silu = av * sig
        da_ref[...] = (dh * bv * (sig + silu * (1.0 - sig))).astype(BF16)
        db_ref[...] = (dh * silu).astype(BF16)

    t_spec = pl.BlockSpec((tm, tn), lambda j, i: (i, j))
    return carrier_call(
        body, name, (N // tn, M // tm),
        [pl.BlockSpec((tm, K), lambda j, i: (i, 0)), pl.BlockSpec((tn, K), lambda j, i: (j, 0)), t_spec, t_spec],
        [t_spec, t_spec], [jax.ShapeDtypeStruct((M, N), BF16)] * 2, [], (df, wd, a, b), comm,
        ("parallel", "parallel"))


def glu_bwd_act(dout, o1, o2, name):
    M, N = dout.shape
    tm = _div_tile(M, (ROW_TILE,))

    def body(d_ref, o1_ref, o2_ref, d1_ref, d2_ref):
        d = d_ref[...].astype(F32)
        sig = _sigmoid(o2_ref[...].astype(F32))
        d1_ref[...] = (d * sig).astype(BF16)
        d2_ref[...] = (d * o1_ref[...].astype(F32) * sig * (1.0 - sig)).astype(BF16)

    spec = pl.BlockSpec((tm, N), lambda i: (i, 0))
    return pl.pallas_call(
        body, name=name, grid=(M // tm,), in_specs=[spec] * 3, out_specs=[spec] * 2,
        out_shape=[jax.ShapeDtypeStruct((M, N), BF16)] * 2,
        compiler_params=_params(("parallel",)),
    )(dout, o1, o2)


def rmsnorm_fwd(x, g, out_dtype, name, residual=None):
    L, D = x.shape
    tr = _div_tile(L, (ROW_TILE,))
    has_res = residual is not None

    def body(*refs):
        x_ref, g_ref = refs[0], refs[1]
        o_ref = refs[-1]
        xv = x_ref[...]
        r = lax.rsqrt(jnp.mean(xv * xv, axis=-1, keepdims=True) + NORM_EPS)
        y = xv * r * g_ref[...]
        if has_res:
            y = refs[2][...] + y
        o_ref[...] = y.astype(o_ref.dtype)

    row = pl.BlockSpec((tr, D), lambda i: (i, 0))
    gsp = pl.BlockSpec((1, D), lambda i: (0, 0))
    args = (x, g) + ((residual,) if has_res else ())
    return pl.pallas_call(
        body, name=name, grid=(L // tr,), in_specs=[row, gsp] + ([row] if has_res else []), out_specs=row,
        out_shape=jax.ShapeDtypeStruct((L, D), out_dtype), compiler_params=_params(("parallel",)),
    )(*args)


def rmsnorm_bwd(x, g, dy, out_dtype, name, add=None, dy2=None):
    L, D = x.shape
    tr = _div_tile(L, (ROW_TILE,))
    has_add = add is not None
    has_dy2 = dy2 is not None

    def body(*refs):
        x_ref, g_ref, dy_ref = refs[0], refs[1], refs[2]
        dx_ref, dg_ref = refs[-2], refs[-1]
        xv = x_ref[...]
        dyv = dy_ref[...].astype(F32)
        if has_dy2:
            dyv = dyv + refs[3][...].astype(F32)
        r = lax.rsqrt(jnp.mean(xv * xv, axis=-1, keepdims=True) + NORM_EPS)
        t = dyv * g_ref[...]
        dx = r * t - xv * (r * r * r) * jnp.mean(xv * t, axis=-1, keepdims=True)
        if has_add:
            dx = refs[3 + has_dy2][...] + dx
        dx_ref[...] = dx.astype(dx_ref.dtype)
        dgp = jnp.sum(dyv * xv * r, axis=0, keepdims=True)

        @pl.when(pl.program_id(0) == 0)
        def _():
            dg_ref[...] = dgp

        @pl.when(pl.program_id(0) > 0)
        def _():
            dg_ref[...] += dgp

    row = pl.BlockSpec((tr, D), lambda i: (i, 0))
    gsp = pl.BlockSpec((1, D), lambda i: (0, 0))
    args = (x, g, dy) + ((dy2,) if has_dy2 else ()) + ((add,) if has_add else ())
    return pl.pallas_call(
        body, name=name, grid=(L // tr,), in_specs=[row, gsp] + [row] * (len(args) - 2),
        out_specs=[row, gsp],
        out_shape=[jax.ShapeDtypeStruct((L, D), out_dtype), jax.ShapeDtypeStruct((1, D), F32)],
        compiler_params=_params(("arbitrary",)),
    )(*args)


def _gate_z(fg_ref, b_ref):
    return fg_ref[...] + b_ref[...]


def gate_fwd(fg_src, col_block, b, name):
    L = fg_src.shape[0]
    T = _div_tile(L, (ROW_TILE,))

    def body(fg_ref, b_ref, c_ref, carry):
        @pl.when(pl.program_id(0) == 0)
        def _():
            carry[...] = jnp.zeros_like(carry)

        z = _gate_z(fg_ref, b_ref)
        logf = jnp.minimum(z, 0.0) - jnp.log(1.0 + jnp.exp(-jnp.abs(z)))
        tri = (lax.broadcasted_iota(jnp.int32, (T, T), 1) <= lax.broadcasted_iota(jnp.int32, (T, T), 0)).astype(F32)
        c = jnp.dot(tri, logf, precision=lax.Precision.HIGHEST, preferred_element_type=F32) + carry[...]
        c_ref[...] = c
        carry[...] = c[T - 1:T, :]

    return pl.pallas_call(
        body, name=name, grid=(L // T,),
        in_specs=[pl.BlockSpec((T, LANES), lambda i: (i, col_block)), pl.BlockSpec((1, LANES), lambda i: (0, 0))],
        out_specs=pl.BlockSpec((T, LANES), lambda i: (i, 0)),
        out_shape=jax.ShapeDtypeStruct((L, LANES), F32),
        scratch_shapes=[pltpu.VMEM((1, LANES), F32)],
        compiler_params=_params(("arbitrary",)),
    )(fg_src, b)


def gate_bwd(fg_src, col_block, b, dc, name):
    L = fg_src.shape[0]
    T = _div_tile(L, (ROW_TILE,))
    nb = L // T

    def body(fg_ref, b_ref, dc_ref, dfg_ref, db_ref, carry):
        @pl.when(pl.program_id(0) == 0)
        def _():
            carry[...] = jnp.zeros_like(carry)
            db_ref[...] = jnp.zeros_like(db_ref)

        z = _gate_z(fg_ref, b_ref)
        dcv = dc_ref[...]
        tri = (lax.broadcasted_iota(jnp.int32, (T, T), 1) >= lax.broadcasted_iota(jnp.int32, (T, T), 0)).astype(F32)
        dlogf = jnp.dot(tri, dcv, precision=lax.Precision.HIGHEST, preferred_element_type=F32) + carry[...]
        dfg = dlogf * _sigmoid(-z)
        dfg_ref[...] = dfg
        db_ref[...] += jnp.sum(dfg, axis=0, keepdims=True)
        carry[...] = dlogf[0:1, :]

    return pl.pallas_call(
        body, name=name, grid=(nb,),
        in_specs=[pl.BlockSpec((T, LANES), lambda i: (nb - 1 - i, col_block)),
                  pl.BlockSpec((1, LANES), lambda i: (0, 0)),
                  pl.BlockSpec((T, LANES), lambda i: (nb - 1 - i, 0))],
        out_specs=[pl.BlockSpec((T, LANES), lambda i: (nb - 1 - i, 0)), pl.BlockSpec((1, LANES), lambda i: (0, 0))],
        out_shape=[jax.ShapeDtypeStruct((L, LANES), F32), jax.ShapeDtypeStruct((1, LANES), F32)],
        scratch_shapes=[pltpu.VMEM((1, LANES), F32)],
        compiler_params=_params(("arbitrary",)),
    )(fg_src, b, dc)


def attn_fwd(proj, cq_col, ck_row, name, comm=None):
    L = proj.shape[0]
    T = _div_tile(L, (ROW_TILE,))
    nq = L // T
    npair = HEADS // 2
    scale = HEAD_DIM ** -0.5
    SUB = ATTN_ROWS
    nsub = T // SUB

    def body(q_ref, k_ref, v_ref, cq_ref, ck_ref, o_ref, lse_ref):
        qb = pl.program_id(1)
        rows = [slice(r * SUB, (r + 1) * SUB) for r in range(nsub)]
        head1 = lax.broadcasted_iota(jnp.int32, (SUB, LANES), 1) >= HEAD_DIM
        qs = [[jnp.where(head1 == (h == 1), q_ref[rs, :] * scale, 0.0).astype(BF16) for rs in rows] for h in range(2)]
        cqs = [[cq_ref[0, rs, h:h + 1] for rs in rows] for h in range(2)]

        def logits(kb):
            ks = pl.multiple_of(kb * T, T)
            k = k_ref[pl.ds(ks, T), :]
            return tuple(lax.dot_general(qs[h][r], k, _DIMS["nt"], preferred_element_type=F32) + cqs[h][r]
                         - ck_ref[0, h:h + 1, pl.ds(ks, T)] for h in range(2) for r in range(nsub))

        def softmax_step(kb, s_all, carry, masked):
            ks = pl.multiple_of(kb * T, T)
            v = v_ref[pl.ds(ks, T), :]
            lane = lax.broadcasted_iota(jnp.int32, (T, LANES), 1)
            new = []
            for h in range(2):
                vh = jnp.where(lane == spare[h], 1.0, v).astype(BF16)
                for r in range(nsub):
                    m, acc = carry[h * nsub + r]
                    s = s_all[h * nsub + r]
                    if masked:
                        keep = (lax.broadcasted_iota(jnp.int32, (SUB, T), 1)
                                <= lax.broadcasted_iota(jnp.int32, (SUB, T), 0) + r * SUB)
                        s = jnp.where(keep, s, NEG_BIG)
                    m_new = jnp.maximum(m, jnp.max(s, axis=1, keepdims=True))
                    p = jnp.exp(s - m_new)
                    acc = jnp.exp(m - m_new) * acc + jnp.dot(p.astype(BF16), vh, preferred_element_type=F32)
                    new.append((m_new, acc))
            return tuple(new)

        def step(kb, state):
            s_all, carry = state
            s_next = logits(kb + 1)
            return s_next, softmax_step(kb, s_all, carry, False)

        spare = (HEAD_DIM, 0)
        one = (jnp.full((SUB, 1), NEG_BIG, F32), jnp.zeros((SUB, LANES), F32))
        s_all, carry = lax.fori_loop(0, qb, step, (logits(0), (one,) * (2 * nsub)))
        carry = softmax_step(qb, s_all, carry, True)
        out, lse = [], []
        for h in range(2):
            chains = carry[h * nsub:(h + 1) * nsub]
            ls = [acc[:, spare[h]:spare[h] + 1] for _, acc in chains]
            out.append(jnp.concatenate([acc / l for (_, acc), l in zip(chains, ls)], axis=0))
            lse.append(jnp.concatenate([m + jnp.log(l) for (m, _), l in zip(chains, ls)], axis=0))
        o_ref[...] = jnp.where(lax.broadcasted_iota(jnp.int32, (T, LANES), 1) >= HEAD_DIM, out[1], out[0]
                               ).astype(o_ref.dtype)
        lse_ref[0] = jnp.concatenate(lse, axis=1)

    return carrier_call(
        body, name, (npair, nq),
        [pl.BlockSpec((T, LANES), lambda p, i: (i, p)),
         pl.BlockSpec((L, LANES), lambda p, i: (0, npair + p)),
         pl.BlockSpec((L, LANES), lambda p, i: (0, 2 * npair + p)),
         pl.BlockSpec((1, T, 2), lambda p, i: (p, i, 0)),
         pl.BlockSpec((1, 2, L), lambda p, i: (p, 0, 0))],
        [pl.BlockSpec((T, LANES), lambda p, i: (i, p)), pl.BlockSpec((1, T, 2), lambda p, i: (p, i, 0))],
        [jax.ShapeDtypeStruct((L, ATTN_W), BF16), jax.ShapeDtypeStruct((npair, L, 2), F32)],
        [], (proj, proj, proj, cq_col, ck_row), comm, ("parallel", "parallel"))


def attn_delta(dcat, cat, name):
    L = dcat.shape[0]
    T = _div_tile(L, (ROW_TILE,))

    def body(do_ref, o_ref, d_ref):
        prod = do_ref[...] * o_ref[...].astype(F32)
        sel = (lax.broadcasted_iota(jnp.int32, (ATTN_W, LANES), 0) // HEAD_DIM
               == lax.broadcasted_iota(jnp.int32, (ATTN_W, LANES), 1)).astype(F32)
        d_ref[...] = jnp.dot(prod, sel, precision=lax.Precision.HIGHEST, preferred_element_type=F32)

    return pl.pallas_call(
        body, name=name, grid=(L // T,),
        in_specs=[pl.BlockSpec((T, ATTN_W), lambda i: (i, 0)), pl.BlockSpec((T, ATTN_W), lambda i: (i, 0))],
        out_specs=pl.BlockSpec((T, LANES), lambda i: (i, 0)),
        out_shape=jax.ShapeDtypeStruct((L, LANES), F32),
        compiler_params=_params(("parallel",)),
    )(dcat, cat)


def attn_bwd(proj, dcat, lse_row, delta_row, cq_row, ck_col, name, comm=None):
    L = proj.shape[0]
    T = _div_tile(L, (ROW_TILE,))
    nb = L // T
    npair = HEADS // 2
    scale = HEAD_DIM ** -0.5

    def body(q_ref, k_ref, v_ref, do_ref, lse_ref, dl_ref, cq_ref, ck_ref,
             dq_ref, dk_ref, dv_ref, dcq_ref, dck_ref, dq_acc, dcq_acc):
        kb = pl.program_id(1)

        @pl.when(kb == 0)
        def _():
            dq_acc[...] = jnp.zeros_like(dq_acc)
            dcq_acc[...] = jnp.zeros_like(dcq_acc)

        head1 = lax.broadcasted_iota(jnp.int32, (T, LANES), 1) >= HEAD_DIM
        ks = [jnp.where(head1 == (h == 1), k_ref[...] * scale, 0.0).astype(BF16) for h in range(2)]
        vs = [jnp.where(head1 == (h == 1), v_ref[...], 0.0).astype(BF16) for h in range(2)]
        cks = [ck_ref[0, :, h:h + 1] for h in range(2)]
        kts = [k.T for k in ks]

        def step(qb, carry, masked):
            qs = pl.multiple_of(qb * T, T)
            q = q_ref[pl.ds(qs, T), :]
            do = do_ref[pl.ds(qs, T), :].astype(BF16)
            new, dq = [], None
            for h in range(2):
                dk, dv, dck = carry[h]
                lse = lse_ref[0, h:h + 1, pl.ds(qs, T)]
                dl = dl_ref[0, h:h + 1, pl.ds(qs, T)]
                cq = cq_ref[0, h:h + 1, pl.ds(qs, T)]
                st = lax.dot_general(ks[h], q, _DIMS["nt"], preferred_element_type=F32) + cq - cks[h]
                if masked:
                    keep = lax.broadcasted_iota(jnp.int32, (T, T), 0) <= lax.broadcasted_iota(jnp.int32, (T, T), 1)
                    st = jnp.where(keep, st, NEG_BIG)
                pt = jnp.exp(st - lse)
                dv = dv + jnp.dot(pt.astype(BF16), do, preferred_element_type=F32)
                dpt = lax.dot_general(vs[h], do, _DIMS["nt"], preferred_element_type=F32)
                dst = pt * (dpt - dl)
                dsb = dst.astype(BF16)
                dk = dk + jnp.dot(dsb, q, preferred_element_type=F32)
                part = jnp.dot(kts[h], dsb, preferred_element_type=F32)
                dq = part if dq is None else dq + part
                dcq_acc[h:h + 1, pl.ds(qs, T)] += jnp.sum(dst, axis=0, keepdims=True)
                dck = dck + jnp.sum(dst, axis=1, keepdims=True)
                new.append((dk, dv, dck))
            dq_acc[:, pl.ds(qs, T)] += dq
            return tuple(new)

        one = (jnp.zeros((T, LANES), F32), jnp.zeros((T, LANES), F32), jnp.zeros((T, 1), F32))
        carry = step(kb, (one, one), True)
        carry = lax.fori_loop(kb + 1, nb, functools.partial(step, masked=False), carry)
        (dk0, dv0, dck0), (dk1, dv1, dck1) = carry
        dk_ref[...] = (jnp.where(head1, dk1, dk0) * scale).astype(dk_ref.dtype)
        dv_ref[...] = jnp.where(head1, dv1, dv0).astype(dv_ref.dtype)
        dck_ref[0] = jnp.concatenate([-dck0, -dck1], axis=1)

        @pl.when(kb == nb - 1)
        def _():
            dq_ref[...] = dq_acc[...].T.astype(dq_ref.dtype)
            dcq_ref[0] = dcq_acc[...]

    full = lambda col: pl.BlockSpec((L, LANES), col)
    row_stat = pl.BlockSpec((1, 2, L), lambda p, i: (p, 0, 0))
    return carrier_call(
        body, name, (npair, nb),
        [full(lambda p, i: (0, p)),
         pl.BlockSpec((T, LANES), lambda p, i: (i, npair + p)),
         pl.BlockSpec((T, LANES), lambda p, i: (i, 2 * npair + p)),
         full(lambda p, i: (0, p)),
         row_stat, row_stat, row_stat,
         pl.BlockSpec((1, T, 2), lambda p, i: (p, i, 0))],
        [full(lambda p, i: (0, p)),
         pl.BlockSpec((T, LANES), lambda p, i: (i, p)),
         pl.BlockSpec((T, LANES), lambda p, i: (i, p)),
         row_stat,
         pl.BlockSpec((1, T, 2), lambda p, i: (p, i, 0))],
        [jax.ShapeDtypeStruct((L, ATTN_W), BF16)] * 3
        + [jax.ShapeDtypeStruct((npair, 2, L), F32), jax.ShapeDtypeStruct((npair, L, 2), F32)],
        [pltpu.VMEM((LANES, L), F32), pltpu.VMEM((2, L), F32)],
        (proj, proj, proj, dcat, lse_row, delta_row, cq_row, ck_col), comm, ("parallel", "arbitrary"))


def _shift_down(x, k):
    rolled = pltpu.roll(x, k, 0)
    return jnp.where(lax.broadcasted_iota(jnp.int32, x.shape, 0) >= k, rolled, 0.0)


def _shift_up(x, k):
    n = x.shape[0]
    rolled = pltpu.roll(x, n - k, 0)
    return jnp.where(lax.broadcasted_iota(jnp.int32, x.shape, 0) < n - k, rolled, 0.0)


def conv_fwd(proj, col0, conv_w, conv_b, name):
    L = proj.shape[0]
    C = conv_w.shape[1]
    nc = C // LANES

    def body(gb_ref, gc_ref, xc_ref, w_ref, b_ref, o_ref):
        z = gc_ref[...] * xc_ref[...]
        conv = (w_ref[0:1, :] * _shift_down(z, 2) + w_ref[1:2, :] * _shift_down(z, 1) + w_ref[2:3, :] * z
                + b_ref[...])
        o_ref[...] = (gb_ref[...] * conv).astype(o_ref.dtype)

    col = lambda off: pl.BlockSpec((L, LANES), lambda j, off=off: (0, col0 + off + j))
    return pl.pallas_call(
        body, name=name, grid=(nc,),
        in_specs=[col(0), col(nc), col(2 * nc), pl.BlockSpec((CONV_K, LANES), lambda j: (0, j)),
                  pl.BlockSpec((1, LANES), lambda j: (0, j))],
        out_specs=pl.BlockSpec((L, LANES), lambda j: (0, j)),
        out_shape=jax.ShapeDtypeStruct((L, C), BF16),
        compiler_params=_params(("parallel",)),
    )(proj, proj, proj, conv_w, conv_b)


def conv_bwd(proj, col0, conv_w, conv_b, dcat, dcol0, name):
    L = proj.shape[0]
    C = conv_w.shape[1]
    nc = C // LANES

    def body(gb_ref, gc_ref, xc_ref, w_ref, b_ref, do_ref, dgb_ref, dgc_ref, dxc_ref, dw_ref, db_ref):
        gc, xc = gc_ref[...], xc_ref[...]
        z = gc * xc
        z1, z2 = _shift_down(z, 1), _shift_down(z, 2)
        w0, w1, w2 = w_ref[0:1, :], w_ref[1:2, :], w_ref[2:3, :]
        conv = w0 * z2 + w1 * z1 + w2 * z + b_ref[...]
        dout = do_ref[...]
        dgb_ref[...] = (dout * conv).astype(dgb_ref.dtype)
        dconv = dout * gb_ref[...]
        dw_ref[...] = jnp.concatenate([jnp.sum(dconv * z2, axis=0, keepdims=True),
                                       jnp.sum(dconv * z1, axis=0, keepdims=True),
                                       jnp.sum(dconv * z, axis=0, keepdims=True)], axis=0)
        db_ref[...] = jnp.sum(dconv, axis=0, keepdims=True)
        dz = w2 * dconv + w1 * _shift_up(dconv, 1) + w0 * _shift_up(dconv, 2)
        dgc_ref[...] = (dz * xc).astype(dgc_ref.dtype)
        dxc_ref[...] = (dz * gc).astype(dxc_ref.dtype)

    col = lambda off: pl.BlockSpec((L, LANES), lambda j, off=off: (0, col0 + off + j))
    out_col = pl.BlockSpec((L, LANES), lambda j: (0, j))
    return pl.pallas_call(
        body, name=name, grid=(nc,),
        in_specs=[col(0), col(nc), col(2 * nc), pl.BlockSpec((CONV_K, LANES), lambda j: (0, j)),
                  pl.BlockSpec((1, LANES), lambda j: (0, j)),
                  pl.BlockSpec((L, LANES), lambda j: (0, dcol0 + j))],
        out_specs=[out_col, out_col, out_col, pl.BlockSpec((CONV_K, LANES), lambda j: (0, j)),
                   pl.BlockSpec((1, LANES), lambda j: (0, j))],
        out_shape=[jax.ShapeDtypeStruct((L, C), BF16)] * 3
        + [jax.ShapeDtypeStruct((CONV_K, C), F32), jax.ShapeDtypeStruct((1, C), F32)],
        compiler_params=_params(("parallel",)),
    )(proj, proj, proj, conv_w, conv_b, dcat)


_GELU_C = math.sqrt(2.0 / math.pi)
_GELU_A = 0.044715


def _gelu(y):
    return 0.5 * y * (1.0 + jnp.tanh(_GELU_C * (y + _GELU_A * y * y * y)))


def _gelu_grad(y):
    t = jnp.tanh(_GELU_C * (y + _GELU_A * y * y * y))
    return 0.5 * (1.0 + t) + 0.5 * y * (1.0 - t * t) * _GELU_C * (1.0 + 3.0 * _GELU_A * y * y)


def _cmul_add(xr, xi, pr, pi, sr, si):
    return xr + pr * sr - pi * si, xi + pr * si + pi * sr


def _scan_tile(br, bi, cr, ci, tab_ref, reverse):
    n = S5_BLOCK_STATES
    xr, xi = br, bi
    for s, k in enumerate((1, 2, 4)):
        shift = SUBLANES - k if reverse else k
        xr, xi = _cmul_add(xr, xi, tab_ref[0, s, :, :n], tab_ref[0, s, :, n:],
                           pltpu.roll(xr, shift, 0), pltpu.roll(xi, shift, 0))
    return _cmul_add(xr, xi, tab_ref[0, 3, :, :n], tab_ref[0, 3, :, n:], cr, ci)


def s5_fwd(u, bmat, cmat, tab, dvec, name, comm=None):
    L, D = u.shape
    nblk = D // LANES
    T = _div_tile(L, (ROW_TILE,))
    ns = 2 * S5_BLOCK_STATES
    n = S5_BLOCK_STATES

    def body(u_ref, b_ref, c_ref, tab_ref, d_ref, y_ref, g_ref, xs_ref, buf, car):
        @pl.when(pl.program_id(1) == 0)
        def _():
            car[...] = jnp.zeros_like(car)

        uv = u_ref[...]
        buf[...] = jnp.dot(uv.astype(BF16), b_ref[0], preferred_element_type=F32)

        def tile(i, carry):
            cr, ci = carry
            r0 = pl.multiple_of(i * SUBLANES, SUBLANES)
            xr, xi = _scan_tile(buf[pl.ds(r0, SUBLANES), :n], buf[pl.ds(r0, SUBLANES), n:], cr, ci, tab_ref, False)
            buf[pl.ds(r0, SUBLANES), :n] = xr
            buf[pl.ds(r0, SUBLANES), n:] = xi
            return xr[SUBLANES - 1:, :], xi[SUBLANES - 1:, :]

        cr, ci = lax.fori_loop(0, T // SUBLANES, tile, (car[:, :n], car[:, n:]))
        car[:, :n] = cr
        car[:, n:] = ci
        xs = buf[...]
        xs_ref[...] = xs
        y = jnp.dot(xs.astype(BF16), c_ref[0], preferred_element_type=F32) + d_ref[...] * uv
        y_ref[...] = y
        g_ref[...] = _gelu(y).astype(g_ref.dtype)

    blk = pl.BlockSpec((T, LANES), lambda j, i: (i, j))
    return carrier_call(
        body, name, (nblk, L // T),
        [blk, pl.BlockSpec((1, LANES, ns), lambda j, i: (j, 0, 0)),
         pl.BlockSpec((1, ns, LANES), lambda j, i: (j, 0, 0)),
         pl.BlockSpec((1, 4, SUBLANES, ns), lambda j, i: (j, 0, 0, 0)),
         pl.BlockSpec((1, LANES), lambda j, i: (0, j))],
        [blk, blk, pl.BlockSpec((T, ns), lambda j, i: (i, j))],
        [jax.ShapeDtypeStruct((L, D), F32), jax.ShapeDtypeStruct((L, D), BF16),
         jax.ShapeDtypeStruct((L, nblk * ns), F32)],
        [pltpu.VMEM((T, ns), F32), pltpu.VMEM((1, ns), F32)],
        (u, bmat, cmat, tab, dvec), comm, ("parallel", "arbitrary"))


def s5_bwd(dg, y, u, xs, cmat_t, bmat_t, rtab, dvec, name, comm=None):
    L, D = u.shape
    nblk = D // LANES
    T = _div_tile(L, (ROW_TILE,))
    nch = L // T
    ns = 2 * S5_BLOCK_STATES
    n = S5_BLOCK_STATES
    ntile = T // SUBLANES

    def body(dg_ref, y_ref, u_ref, xs_ref, xp_ref, ct_ref, bt_ref, tab_ref, d_ref,
             du_ref, dc_ref, db_ref, dlam_ref, dd_ref, buf, xbuf, car):
        step = pl.program_id(1)
        first_chunk = step == nch - 1

        @pl.when(step == 0)
        def _():
            car[...] = jnp.zeros_like(car)
            dc_ref[...] = jnp.zeros_like(dc_ref)
            db_ref[...] = jnp.zeros_like(db_ref)
            dlam_ref[...] = jnp.zeros_like(dlam_ref)
            dd_ref[...] = jnp.zeros_like(dd_ref)

        uv = u_ref[...]
        dy = dg_ref[...].astype(F32) * _gelu_grad(y_ref[...])
        dd_ref[...] += jnp.sum(dy * uv, axis=0, keepdims=True)
        dyb = dy.astype(BF16)
        buf[...] = jnp.dot(dyb, ct_ref[0], preferred_element_type=F32)
        xs = xs_ref[...]
        xbuf[pl.ds(SUBLANES, T), :] = xs
        xbuf[pl.ds(0, SUBLANES), :] = jnp.where(first_chunk, 0.0, xp_ref[...])
        row0 = lax.broadcasted_iota(jnp.int32, (SUBLANES, n), 0) == 0

        def tile(ii, carry):
            cr, ci, ar, ai = carry
            r0 = pl.multiple_of((ntile - 1 - ii) * SUBLANES, SUBLANES)
            xr, xi = _scan_tile(buf[pl.ds(r0, SUBLANES), :n], buf[pl.ds(r0, SUBLANES), n:], cr, ci, tab_ref, True)
            buf[pl.ds(r0, SUBLANES), :n] = xr
            buf[pl.ds(r0, SUBLANES), n:] = xi
            r1 = pl.multiple_of(r0 + SUBLANES, SUBLANES)
            pr = jnp.where(row0, xbuf[pl.ds(r0, SUBLANES), :n][SUBLANES - 1:, :],
                           pltpu.roll(xbuf[pl.ds(r1, SUBLANES), :n], 1, 0))
            pi = jnp.where(row0, xbuf[pl.ds(r0, SUBLANES), n:][SUBLANES - 1:, :],
                           pltpu.roll(xbuf[pl.ds(r1, SUBLANES), n:], 1, 0))
            ar = ar + xr * pr + xi * pi
            ai = ai + xi * pr - xr * pi
            return xr[0:1, :], xi[0:1, :], ar, ai

        zero = jnp.zeros((SUBLANES, n), F32)
        cr, ci, ar, ai = lax.fori_loop(0, ntile, tile, (car[:, :n], car[:, n:], zero, zero))
        car[:, :n] = cr
        car[:, n:] = ci
        dlam_ref[0, :, :n] += ar
        dlam_ref[0, :, n:] += ai
        dxa = buf[...]
        dc_ref[0] += lax.dot_general(dyb, xs.astype(BF16), _DIMS["tn"], preferred_element_type=F32)
        dxb = dxa.astype(BF16)
        db_ref[0] += lax.dot_general(uv.astype(BF16), dxb, _DIMS["tn"], preferred_element_type=F32)
        du_ref[...] = jnp.dot(dxb, bt_ref[0], preferred_element_type=F32) + d_ref[...] * dy

    rev = lambda j, i: (nch - 1 - i, j)
    blk = pl.BlockSpec((T, LANES), rev)
    tpb = T // SUBLANES
    acc = pl.BlockSpec((1, LANES, ns), lambda j, i: (j, 0, 0))
    return carrier_call(
        body, name, (nblk, nch),
        [blk, blk, blk, pl.BlockSpec((T, ns), rev),
         pl.BlockSpec((SUBLANES, ns), lambda j, i: (jnp.maximum((nch - 1 - i) * tpb - 1, 0), j)),
         pl.BlockSpec((1, LANES, ns), lambda j, i: (j, 0, 0)),
         pl.BlockSpec((1, ns, LANES), lambda j, i: (j, 0, 0)),
         pl.BlockSpec((1, 4, SUBLANES, ns), lambda j, i: (j, 0, 0, 0)),
         pl.BlockSpec((1, LANES), lambda j, i: (0, j))],
        [blk, acc, acc, pl.BlockSpec((1, SUBLANES, ns), lambda j, i: (j, 0, 0)),
         pl.BlockSpec((1, LANES), lambda j, i: (0, j))],
        [jax.ShapeDtypeStruct((L, D), F32), jax.ShapeDtypeStruct((nblk, LANES, ns), F32),
         jax.ShapeDtypeStruct((nblk, LANES, ns), F32), jax.ShapeDtypeStruct((nblk, SUBLANES, ns), F32),
         jax.ShapeDtypeStruct((1, D), F32)],
        [pltpu.VMEM((T, ns), F32), pltpu.VMEM((T + SUBLANES, ns), F32), pltpu.VMEM((1, ns), F32)],
        (dg, y, u, xs, xs, cmat_t, bmat_t, rtab, dvec), comm, ("parallel", "arbitrary"))


def _s5_discretize(a_re, a_im, log_step, b_re, b_im):
    lam_re = jnp.minimum(a_re, -S5_MIN_DECAY)
    lam_im = a_im
    delta = jnp.exp(log_step)[:, None]
    mag = jnp.exp(lam_re * delta)
    ang = lam_im * delta
    lb_re = mag * jnp.cos(ang)
    lb_im = mag * jnp.sin(ang)
    den = lam_re * lam_re + lam_im * lam_im
    nr = lb_re - 1.0
    ni = lb_im
    coef_re = (nr * lam_re + ni * lam_im) / den
    coef_im = (ni * lam_re - nr * lam_im) / den
    bb_re = coef_re[..., None] * b_re - coef_im[..., None] * b_im
    bb_im = coef_re[..., None] * b_im + coef_im[..., None] * b_re
    return lb_re, lb_im, bb_re, bb_im


def _s5_tables(lb_re, lb_im):
    nblk = lb_re.shape[0] // S5_BLOCK_GROUPS
    lr = lb_re.reshape(nblk, S5_BLOCK_STATES)
    li = lb_im.reshape(nblk, S5_BLOCK_STATES)
    pows = [(jnp.ones_like(lr), jnp.zeros_like(li))]
    for _ in range(SUBLANES):
        pr, pi = pows[-1]
        pows.append((pr * lr - pi * li, pr * li + pi * lr))
    rows = jnp.arange(SUBLANES)[None, :, None]

    def table(conj, reverse):
        sgn = -1.0 if conj else 1.0
        out = []
        for k in (1, 2, 4):
            mask = (rows <= SUBLANES - 1 - k) if reverse else (rows >= k)
            out.append(jnp.concatenate([jnp.where(mask, pows[k][0][:, None, :], 0.0),
                                        jnp.where(mask, sgn * pows[k][1][:, None, :], 0.0)], axis=-1))
        order = range(SUBLANES, 0, -1) if reverse else range(1, SUBLANES + 1)
        cre = jnp.stack([pows[k][0] for k in order], axis=1)
        cim = jnp.stack([sgn * pows[k][1] for k in order], axis=1)
        out.append(jnp.concatenate([cre, cim], axis=-1))
        return jnp.stack(out, axis=1)

    return table(False, False), table(True, True)


def _s5_block_mats(bb_re, bb_im, c_re, c_im):
    G = bb_re.shape[0]
    nblk = G // S5_BLOCK_GROUPS
    eye = jnp.eye(S5_BLOCK_GROUPS, dtype=F32)
    bb = jnp.stack([bb_re, bb_im]).reshape(2, nblk, S5_BLOCK_GROUPS, S5_STATE, S5_GROUP)
    bmat = jnp.einsum("ab,rjaph->jahrbp", eye, bb).reshape(nblk, LANES, 2 * S5_BLOCK_STATES)
    cc = jnp.stack([c_re, -c_im]).reshape(2, nblk, S5_BLOCK_GROUPS, S5_GROUP, S5_STATE)
    cmat = jnp.einsum("ab,rjahp->jrbpah", eye, cc).reshape(nblk, 2 * S5_BLOCK_STATES, LANES)
    return bmat, cmat


def _s5_unblock(dB, dC, dlam):
    nblk = dB.shape[0]
    G = nblk * S5_BLOCK_GROUPS
    d6 = dB.reshape(nblk, S5_BLOCK_GROUPS, S5_GROUP, 2, S5_BLOCK_GROUPS, S5_STATE)
    dbb = jnp.einsum("jahrap->rjaph", d6).reshape(2, G, S5_STATE, S5_GROUP)
    c6 = dC.reshape(nblk, S5_BLOCK_GROUPS, S5_GROUP, 2, S5_BLOCK_GROUPS, S5_STATE)
    dcc = jnp.einsum("jahrap->rjahp", c6).reshape(2, G, S5_GROUP, S5_STATE)
    dl = jnp.sum(dlam, axis=1).reshape(nblk, 2, S5_BLOCK_GROUPS, S5_STATE)
    dl = jnp.transpose(dl, (1, 0, 2, 3)).reshape(2, G, S5_STATE)
    return dbb[0], dbb[1], dcc[0], -dcc[1], dl[0], dl[1]


def loss_and_grad(y, target, name):
    L, D = y.shape
    tr = _div_tile(L, (512, 256, 128))

    def body(y_ref, t_ref, dy_ref, loss_ref):
        err = y_ref[...] - t_ref[...]
        dy_ref[...] = err * (1.0 / D)
        part = 0.5 * jnp.sum(jnp.mean(err * err, axis=-1, keepdims=True), axis=0, keepdims=True)

        @pl.when(pl.program_id(0) == 0)
        def _():
            loss_ref[...] = part

        @pl.when(pl.program_id(0) > 0)
        def _():
            loss_ref[...] += part

    row = pl.BlockSpec((tr, D), lambda i: (i, 0))
    return pl.pallas_call(
        body, name=name, grid=(L // tr,), in_specs=[row, row],
        out_specs=[row, pl.BlockSpec((1, 1), lambda i: (0, 0))],
        out_shape=[jax.ShapeDtypeStruct((L, D), F32), jax.ShapeDtypeStruct((1, 1), F32)],
        compiler_params=_params(("arbitrary",)),
    )(y, target)


def _adam_math(w, g, m, v):
    m = ADAM_B1 * m + (1.0 - ADAM_B1) * g
    v = ADAM_B2 * v + (1.0 - ADAM_B2) * (g * g)
    m_hat = m / (1.0 - ADAM_B1 ** ADAM_STEP)
    v_hat = v / (1.0 - ADAM_B2 ** ADAM_STEP)
    delta = -ADAM_LR * (m_hat / (jnp.sqrt(v_hat) + ADAM_EPS) + ADAM_WD * w)
    return delta, m, v


def _as3d(a):
    return a.reshape((-1,) + a.shape[-2:])


def adamw(w, g, m, v, name, comm=None):
    shape = w.shape
    w3, g3, m3, v3 = _as3d(w), _as3d(g), _as3d(m), _as3d(v)
    A, R, C = w3.shape
    tr = _row_tile(R, C)

    def body(w_ref, g_ref, m_ref, v_ref, d_ref, mo_ref, vo_ref):
        d, mn, vn = _adam_math(w_ref[...], g_ref[...], m_ref[...], v_ref[...])
        d_ref[...] = d
        mo_ref[...] = mn
        vo_ref[...] = vn

    spec = pl.BlockSpec((1, tr, C), lambda a, i: (a, i, 0))
    outs, arrived = carrier_call(body, name, (A, R // tr), [spec] * 4, [spec] * 3,
                                 [jax.ShapeDtypeStruct((A, R, C), F32)] * 3, [], (w3, g3, m3, v3), comm,
                                 ("parallel", "parallel"))
    return [o.reshape(shape) for o in outs], arrived


def _place():
    x, y, c = lax.axis_index("x"), lax.axis_index("y"), lax.axis_index("c")
    other_chips = [(1 - x, y), (x, 1 - y), (1 - x, 1 - y)]
    return x, y, c, other_chips


def _chip_id(chip):
    return 2 * chip[0] + chip[1]


def _my_chip():
    return 2 * lax.axis_index("x") + lax.axis_index("y")


def _remote(src, dst, send_sem, recv_sem, dev):
    return pltpu.make_async_remote_copy(src_ref=src, dst_ref=dst, send_sem=send_sem, recv_sem=recv_sem,
                                        device_id=dev, device_id_type=MESH)


def allgather_small(arrs):
    T = len(arrs)

    def body(*refs):
        ins, outs = refs[:T], refs[T:2 * T]
        send_sems, recv_sems = refs[2 * T:]
        x, y, c, chips = _place()
        me = _chip_id((x, y))
        sends = []
        for t in range(T):
            outs[t][me] = ins[t][...]
            for j, chip in enumerate(chips):
                cp = _remote(ins[t], outs[t].at[me], send_sems.at[3 * t + j], recv_sems.at[3 * t + j], (*chip, c))
                cp.start()
                sends.append(cp)
        for t in range(T):
            for j, chip in enumerate(chips):
                slot = outs[t].at[_chip_id(chip)]
                _remote(slot, slot, send_sems.at[3 * t + j], recv_sems.at[3 * t + j], (*chip, c)).wait_recv()
        for cp in sends:
            cp.wait_send()

    return pl.pallas_call(
        body, name="allgather_small", in_specs=[VMEM_SPEC] * T, out_specs=[VMEM_SPEC] * T,
        out_shape=[jax.ShapeDtypeStruct((4,) + a.shape, a.dtype) for a in arrs],
        scratch_shapes=[pltpu.SemaphoreType.DMA((3 * T,)), pltpu.SemaphoreType.DMA((3 * T,))],
        compiler_params=pltpu.CompilerParams(vmem_limit_bytes=VMEM_LIMIT, has_side_effects=True),
    )(*arrs)


def small_swap_comm(buf):
    def copy(ins, outs, send_sems, recv_sems):
        x, y, c, _ = _place()
        return _remote(ins[0], outs[0], send_sems.at[0], recv_sems.at[0], (x, y, 1 - c))

    return Comm([buf], [jax.ShapeDtypeStruct(buf.shape, F32)], {}, 1,
                lambda *refs: copy(*refs).start(), None, lambda *refs: copy(*refs).wait())


def small_exchange_comm(pair):
    def copies(ins, outs, send_sems, recv_sems):
        x, y, c, chips = _place()
        return [_remote(ins[0], outs[0].at[j], send_sems.at[j], recv_sems.at[j], (*chip, c))
                for j, chip in enumerate(chips)]

    def begin(*refs):
        for cp in copies(*refs):
            cp.start()

    def finish(*refs):
        for cp in copies(*refs):
            cp.wait()

    return Comm([pair], [jax.ShapeDtypeStruct((3,) + pair.shape, F32)], {}, 3, begin, None, finish)


def small_pair_sum(mine, theirs, name):
    R, C = mine.shape
    tr = _row_tile(R, C)

    def body(a_ref, b_ref, o_ref):
        o_ref[...] = a_ref[...] + b_ref[...]

    spec = pl.BlockSpec((tr, C), lambda i: (i, 0))
    return pl.pallas_call(body, name=name, grid=(R // tr,), in_specs=[spec, spec], out_specs=spec,
                          out_shape=jax.ShapeDtypeStruct((R, C), F32), compiler_params=_params(("parallel",)))(mine, theirs)


def small_chip_sum(pair, got, name):
    R, C = pair.shape
    tr = _row_tile(R, C)

    def body(p_ref, g_ref, o_ref):
        me = _my_chip()
        terms = []
        for chip in range(4):
            d = jnp.bitwise_xor(me, chip)
            terms.append(jnp.where(d == 0, p_ref[...],
                                   jnp.where(d == 2, g_ref[0], jnp.where(d == 1, g_ref[1], g_ref[2]))))
        o_ref[...] = ((terms[0] + terms[1]) + terms[2]) + terms[3]

    spec = pl.BlockSpec((tr, C), lambda i: (i, 0))
    return pl.pallas_call(body, name=name, grid=(R // tr,),
                          in_specs=[spec, pl.BlockSpec((3, tr, C), lambda i: (0, i, 0))], out_specs=spec,
                          out_shape=jax.ShapeDtypeStruct((R, C), F32), compiler_params=_params(("parallel",)))(pair, got)


def _half_rows(ref, layout, shard, half):
    if layout == "S":
        hr = ref.shape[1] // 2
        return ref.at[shard, pl.ds(pl.multiple_of(half * hr, 16), hr), :]
    hr, C = ref.shape[0] // 2, ref.shape[1] // 4
    return ref.at[pl.ds(pl.multiple_of(half * hr, 16), hr), pl.ds(pl.multiple_of(shard * C, LANES), C)]


def _half_rows_all(ref, layout, half):
    if layout == "S":
        hr = ref.shape[1] // 2
        return ref.at[:, pl.ds(pl.multiple_of(half * hr, 16), hr), :]
    hr = ref.shape[0] // 2
    return ref.at[pl.ds(pl.multiple_of(half * hr, 16), hr), :]


def _shard_of_half(ref, layout, shard):
    if layout == "S":
        return ref.at[shard]
    C = ref.shape[1] // 4
    return ref.at[:, pl.ds(pl.multiple_of(shard * C, LANES), C)]


def _own_block_spec(layout, tr, C):
    if layout == "S":
        return pl.BlockSpec((None, tr, C), lambda i: (_my_chip(), i, 0))
    return pl.BlockSpec((tr, C), lambda i: (i, _my_chip()))


def cast_into_gathered(shards, layer, layout, name):
    _, R, C = shards.shape
    tr = _row_tile(R, C)

    def body(a_ref, o_ref):
        o_ref[...] = a_ref[...].astype(BF16)

    return pl.pallas_call(
        body, name=name, grid=(R // tr,),
        in_specs=[pl.BlockSpec((None, tr, C), lambda i: (layer, i, 0))],
        out_specs=_own_block_spec(layout, tr, C),
        out_shape=jax.ShapeDtypeStruct((4, R, C) if layout == "S" else (R, 4 * C), BF16),
        compiler_params=_params(("parallel",)),
    )(shards)


def gather_comm(bufs, layouts):
    T = len(bufs)

    def begin(_, outs, send_sems, recv_sems):
        x, y, c, chips = _place()
        for t in range(T):
            mine = _half_rows(outs[t], layouts[t], _chip_id((x, y)), c)
            for j, chip in enumerate(chips):
                _remote(mine, mine, send_sems.at[6 * t + j], recv_sems.at[6 * t + j], (*chip, c)).start()

    def middle(_, outs, send_sems, recv_sems):
        x, y, c, chips = _place()
        for t in range(T):
            for j, chip in enumerate(chips):
                piece = _half_rows(outs[t], layouts[t], _chip_id(chip), c)
                _remote(piece, piece, send_sems.at[6 * t + j], recv_sems.at[6 * t + j], (*chip, c)).wait_recv()
                _remote(piece, piece, send_sems.at[6 * t + 3 + j], recv_sems.at[6 * t + 3 + j], (x, y, 1 - c)).start()

    def finish(_, outs, send_sems, recv_sems):
        x, y, c, chips = _place()
        for t in range(T):
            mine = _half_rows(outs[t], layouts[t], _chip_id((x, y)), c)
            for j, chip in enumerate(chips):
                theirs = _half_rows(outs[t], layouts[t], _chip_id(chip), 1 - c)
                _remote(theirs, theirs, send_sems.at[6 * t + 3 + j], recv_sems.at[6 * t + 3 + j],
                        (x, y, 1 - c)).wait_recv()
                _remote(mine, mine, send_sems.at[6 * t + j], recv_sems.at[6 * t + j], (*chip, c)).wait_send()
                piece = _half_rows(outs[t], layouts[t], _chip_id(chip), c)
                _remote(piece, piece, send_sems.at[6 * t + 3 + j], recv_sems.at[6 * t + 3 + j],
                        (x, y, 1 - c)).wait_send()

    return Comm(bufs, [jax.ShapeDtypeStruct(b.shape, b.dtype) for b in bufs], {t: t for t in range(T)}, 6 * T,
                begin, middle, finish, middle_frac=0.75)


def swap_comm(grads, layouts):
    T = len(grads)

    def out_shape(g, layout):
        return (4, g.shape[1] // 2, g.shape[2]) if layout == "S" else (g.shape[0] // 2, g.shape[1])

    def copies(ins, outs, send_sems, recv_sems):
        x, y, c, _ = _place()
        return [_remote(_half_rows_all(ins[t], layouts[t], 1 - c), outs[t], send_sems.at[t], recv_sems.at[t],
                        (x, y, 1 - c)) for t in range(T)]

    def begin(*refs):
        for cp in copies(*refs):
            cp.start()

    def finish(*refs):
        for cp in copies(*refs):
            cp.wait()

    return Comm(grads, [jax.ShapeDtypeStruct(out_shape(g, k), F32) for g, k in zip(grads, layouts)], {}, T,
                begin, None, finish)


def pair_sum(grad, recv, layout, name):
    if layout == "S":
        _, hr, C = recv.shape
        tr = _row_tile(hr, C)
        nb = hr // tr
        grid = (4, nb)
        g_spec = pl.BlockSpec((None, tr, C), lambda a, i: (a, lax.axis_index("c") * nb + i, 0))
        spec = pl.BlockSpec((None, tr, C), lambda a, i: (a, i, 0))
    else:
        hr, C = recv.shape
        tr = _row_tile(hr, C)
        nb = hr // tr
        grid = (nb,)
        g_spec = pl.BlockSpec((tr, C), lambda i: (lax.axis_index("c") * nb + i, 0))
        spec = pl.BlockSpec((tr, C), lambda i: (i, 0))

    def body(g_ref, r_ref, f_ref, b_ref):
        s = g_ref[...] + r_ref[...]
        f_ref[...] = s
        b_ref[...] = s.astype(BF16)

    return pl.pallas_call(
        body, name=name, grid=grid, in_specs=[g_spec, spec], out_specs=[spec, spec],
        out_shape=[jax.ShapeDtypeStruct(recv.shape, F32), jax.ShapeDtypeStruct(recv.shape, BF16)],
        compiler_params=_params(("parallel",) * len(grid)),
    )(grad, recv)


def exchange_comm(pair_bf16, layouts):
    T = len(pair_bf16)

    def out_shape(p, layout):
        return (3,) + ((p.shape[1], p.shape[2]) if layout == "S" else (p.shape[0], p.shape[1] // 4))

    def copies(ins, outs, send_sems, recv_sems):
        x, y, c, chips = _place()
        return [_remote(_shard_of_half(ins[t], layouts[t], _chip_id(chip)), outs[t].at[j],
                        send_sems.at[3 * t + j], recv_sems.at[3 * t + j], (*chip, c))
                for t in range(T) for j, chip in enumerate(chips)]

    def begin(*refs):
        for cp in copies(*refs):
            cp.start()

    def finish(*refs):
        for cp in copies(*refs):
            cp.wait()

    return Comm(pair_bf16, [jax.ShapeDtypeStruct(out_shape(p, k), BF16) for p, k in zip(pair_bf16, layouts)], {},
                3 * T, begin, None, finish)


def reduce_total(pair_f32, got, layout, layer, n_layers, previous, name):
    _, hr, C = got.shape
    tr = _row_tile(hr, C)
    nb = hr // tr

    def body(*refs):
        p_ref, g_ref, t_ref = refs[0], refs[1], refs[-1]
        t_ref[...] = ((p_ref[...] + g_ref[0].astype(F32)) + g_ref[1].astype(F32)) + g_ref[2].astype(F32)

    args = [pair_f32, got] + ([previous] if previous is not None else [])
    return pl.pallas_call(
        body, name=name, grid=(nb,),
        in_specs=[_own_block_spec(layout, tr, C), pl.BlockSpec((3, tr, C), lambda i: (0, i, 0))]
        + ([ANY] if previous is not None else []),
        out_specs=pl.BlockSpec((None, tr, C), lambda i: (layer, lax.axis_index("c") * nb + i, 0)),
        out_shape=jax.ShapeDtypeStruct((n_layers, 2 * hr, C), F32),
        input_output_aliases={2: 0} if previous is not None else {},
        compiler_params=_params(("parallel",)),
    )(*args)


def share_comm(reduced):
    T = len(reduced)

    def halves(outs, half):
        return [o.at[:, pl.ds(pl.multiple_of(half * (o.shape[1] // 2), 8), o.shape[1] // 2), :] for o in outs]

    def begin(_, outs, send_sems, recv_sems):
        x, y, c, _p = _place()
        for t, mine in enumerate(halves(outs, c)):
            _remote(mine, mine, send_sems.at[t], recv_sems.at[t], (x, y, 1 - c)).start()

    def finish(_, outs, send_sems, recv_sems):
        x, y, c, _p = _place()
        for t, (mine, theirs) in enumerate(zip(halves(outs, c), halves(outs, 1 - c))):
            _remote(mine, mine, send_sems.at[t], recv_sems.at[t], (x, y, 1 - c)).wait_send()
            _remote(theirs, theirs, send_sems.at[t], recv_sems.at[t], (x, y, 1 - c)).wait_recv()

    return Comm(reduced, [jax.ShapeDtypeStruct(r.shape, r.dtype) for r in reduced], {t: t for t in range(T)}, T,
                begin, None, finish)


def _round_up(n, m):
    return (n + m - 1) // m * m


def _heads_col(a16):
    L = a16.shape[0]
    return jnp.transpose(a16.reshape(L, HEADS // 2, 2), (1, 0, 2))


def _heads_row(a16):
    L = a16.shape[0]
    return jnp.transpose(a16.reshape(L, HEADS // 2, 2), (1, 2, 0))


def local_step(x, target, meta, norm_g, b_f, conv_w, conv_b, s5, s5_d, stager):
    S, D = x.shape
    depth = norm_g.shape[0]
    n_even, n_odd = b_f.shape[0], s5_d.shape[0]
    L = N_META + S
    Lp = _round_up(L, ROW_TILE)
    h = jnp.concatenate([meta, x, jnp.zeros((Lp - L, D), F32)], axis=0)
    conv_c = conv_w.shape[2]
    fg_block = 3 * conv_c // LANES
    saved = []

    def riding(tag, fn, *args):
        comm = stager.ride(tag)
        if comm is None and fn is matmul:
            return fn(*args, name=tag)
        outs, arrived = fn(*args, name=tag, comm=comm)
        stager.arrived(tag, arrived)
        return outs[0] if fn is matmul else outs

    for i in range(depth):
        g = norm_g[i]
        j = i // 2
        tag = "l%d_" % i
        w = stager.weights(i)
        st = {"h0": h, "w": w}
        if i % 2 == 0:
            u = rmsnorm_fwd(h, g[0:1], BF16, tag + "norm0")
            qkv = matmul([(u, w["w_qkv"])], "nn", BF16, tag + "qkv")
            rest = matmul([(u, w["w_rest"])], "nn", F32, tag + "rest")
            cgate = gate_fwd(rest, fg_block, b_f[j], tag + "gate")
            c16 = cgate[:, :HEADS]
            attn, lse = riding(tag + "attn", attn_fwd, qkv, _heads_col(c16), _heads_row(c16))
            convo = conv_fwd(rest, 0, conv_w[j], conv_b[j], tag + "conv")
            cat = jnp.concatenate([attn, convo], axis=1)
            m = matmul([(cat, w["w_o"])], "nn", F32, tag + "wo")
            st.update(u=u, qkv=qkv, rest=rest, c16=c16, lse=lse, cat=cat)
        else:
            p = s5[j]
            u = rmsnorm_fwd(h, g[0:1], F32, tag + "norm0")
            y, gact, xs = riding(tag + "s5", s5_fwd, u, p["bmat"], p["cmat"], p["tab"], s5_d[j])
            o1, o2, m = riding(tag + "glu", dual_matmul_act, gact, w["w_glu1"], w["w_glu2"], "glu", F32)
            st.update(u=u, y=y, gact=gact, xs=xs, o1=o1, o2=o2)
        h1 = rmsnorm_fwd(m, g[1:2], F32, tag + "norm1", residual=h)
        u2 = rmsnorm_fwd(h1, g[2:3], BF16, tag + "norm2")
        a, b, hact = riding(tag + "ffn_in", dual_matmul_act, u2, w["w_gate"], w["w_up"], "swiglu", BF16)
        f = riding(tag + "ffn_out", matmul, [(hact, w["w_down"])], "nn", F32)
        h = rmsnorm_fwd(f, g[3:4], F32, tag + "norm3", residual=h1)
        st.update(m=m, h1=h1, u2=u2, a=a, b=b, hact=hact, f=f)
        saved.append(st)

    dy, loss = loss_and_grad(h[N_META:L], target, "loss")
    dh = jnp.concatenate([jnp.zeros((N_META, D), F32), dy, jnp.zeros((Lp - L, D), F32)], axis=0)

    grads = {k: [None] * n_even for k in ("b_f", "conv_w", "conv_b")}
    grads.update({k: [None] * n_odd for k in ("s5_d", "s5_dB", "s5_dC", "s5_dlam")})
    grads["norm_g"] = [None] * depth

    for i in reversed(range(depth)):
        g = norm_g[i]
        j = i // 2
        tag = "l%d_b_" % i
        st = saved[i]
        w = st["w"]
        wg = {}
        df, dg3 = rmsnorm_bwd(st["f"], g[3:4], dh, BF16, tag + "norm3")
        wg["w_down"] = matmul([(st["hact"], df)], "tn", F32, tag + "dw_down")
        da, db = riding(tag + "ffn_act", ffn_bwd_act, df, w["w_down"], st["a"], st["b"])
        u2t = st["u2"].T
        wg["w_gate"] = matmul([(u2t, da)], "nn", F32, tag + "dw_gate")
        wg["w_up"] = matmul([(u2t, db)], "nn", F32, tag + "dw_up")
        stager.put_grads(i, wg)
        wg = {}
        du2 = riding(tag + "du2", matmul, [(da, w["w_gate"])], "nt", F32)
        du2 = matmul([(db, w["w_up"])], "nt", F32, tag + "du2_up", add=du2)
        dh1, dg2 = rmsnorm_bwd(st["h1"], g[2:3], du2, F32, tag + "norm2", add=dh)
        if i % 2 == 0:
            dm, dg1 = rmsnorm_bwd(st["m"], g[1:2], dh1, BF16, tag + "norm1")
            wg["w_o"] = matmul([(st["cat"], dm)], "tn", F32, tag + "dw_o")
            stager.put_grads(i, wg)
            dcat = riding(tag + "dcat", matmul, [(dm, w["w_o"])], "nt", F32)
            delta = attn_delta(dcat, st["cat"], tag + "delta")
            c16 = st["c16"]
            lse16 = jnp.transpose(st["lse"], (1, 0, 2)).reshape(Lp, HEADS)
            dq, dk, dv, dcq, dck = riding(tag + "attn", attn_bwd, st["qkv"], dcat, _heads_row(lse16),
                                          _heads_row(delta[:, :HEADS]), _heads_row(c16), _heads_col(c16))
            dc16 = (jnp.transpose(dcq, (2, 0, 1)).reshape(Lp, HEADS)
                    + jnp.transpose(dck, (1, 0, 2)).reshape(Lp, HEADS))
            dc = jnp.pad(dc16, ((0, 0), (0, LANES - HEADS)))
            dfg, dbf = gate_bwd(st["rest"], fg_block, b_f[j], dc, tag + "gate")
            dgb, dgc, dxc, dcw, dcb = conv_bwd(st["rest"], 0, conv_w[j], conv_b[j], dcat, ATTN_W // LANES,
                                               tag + "conv")
            dqkv = jnp.concatenate([dq, dk, dv], axis=1)
            drest = jnp.concatenate([dgb, dgc, dxc, dfg.astype(BF16)], axis=1)
            wg["w_qkv"] = matmul([(st["u"], dqkv)], "tn", F32, tag + "dw_qkv")
            wg["w_rest"] = matmul([(st["u"], drest)], "tn", F32, tag + "dw_rest")
            stager.put_grads(i, wg)
            du = riding(tag + "du_qkv", matmul, [(dqkv, w["w_qkv"])], "nt", F32)
            du_b = riding(tag + "du_rest", matmul, [(drest, w["w_rest"])], "nt", F32)
            grads["b_f"][j], grads["conv_w"][j], grads["conv_b"][j] = dbf, dcw, dcb
        else:
            p = s5[j]
            dmix, dg1 = rmsnorm_bwd(st["m"], g[1:2], dh1, F32, tag + "norm1")
            do1, do2 = glu_bwd_act(dmix, st["o1"], st["o2"], tag + "glu_act")
            wg["w_glu1"] = matmul([(st["gact"], do1)], "tn", F32, tag + "dw_glu1")
            wg["w_glu2"] = matmul([(st["gact"], do2)], "tn", F32, tag + "dw_glu2")
            dgact = matmul([(do1, w["w_glu1"]), (do2, w["w_glu2"])], "nt", F32, tag + "dgact")
            du, dC, dB, dlam, dd = riding(tag + "s5", s5_bwd, dgact, st["y"], st["u"], st["xs"], p["cmat_t"],
                                          p["bmat_t"], p["rtab"], s5_d[j])
            du_b = None
            grads["s5_dB"][j], grads["s5_dC"][j], grads["s5_dlam"][j], grads["s5_d"][j] = dB, dC, dlam, dd
        dh, dg0 = rmsnorm_bwd(st["h0"], g[0:1], du, F32, tag + "norm0", add=dh1, dy2=du_b)
        grads["norm_g"][i] = jnp.concatenate([dg0, dg1, dg2, dg3], axis=0)
        stager.put_grads(i, wg)

    grads["meta"] = dh[:N_META]
    return loss, dh[N_META:L], grads


def _packed_rows(shape):
    return _round_up(_round_up(math.prod(shape), LANES) // LANES, SUBLANES)


def _pack(arrs):
    rows = []
    for a in arrs:
        flat = a.reshape(-1).astype(F32)
        r = _packed_rows(a.shape)
        rows.append(jnp.pad(flat, (0, r * LANES - flat.shape[0])).reshape(r, LANES))
    return jnp.concatenate(rows, axis=0)


def _unpack(buf, shapes):
    buf = buf.reshape(-1, LANES)
    out, off = [], 0
    for s in shapes:
        r = _packed_rows(s)
        out.append(buf[off:off + r].reshape(-1)[:math.prod(s)].reshape(s))
        off += r
    return out


class _LayerWeights:
    def __init__(self, stager, layer):
        self.stager, self.layer = stager, layer

    def __getitem__(self, name):
        return self.stager.weight(self.layer, name)


class MeshStager:
    LAYOUT = {"ab_w_in": "S", "ab_w_o": "S", "s5_w_glu1": "S", "s5_w_glu2": "S",
              "ffn_w_gate": "C", "ffn_w_up": "C", "ffn_w_down": "S"}
    EVEN = ("ab_w_in", "ab_w_o", "ffn_w_gate", "ffn_w_up", "ffn_w_down")
    ODD = ("s5_w_glu1", "s5_w_glu2", "ffn_w_gate", "ffn_w_up", "ffn_w_down")

    def __init__(self, shards):
        self.shards = shards
        self.depth = depth = shards["ffn_w_gate"].shape[0]
        self.bufs = {}
        for i in range(depth):
            for k in self.keys(i):
                self.bufs[k, i] = cast_into_gathered(shards[k], self.index(k, i), self.LAYOUT[k],
                                                     "cast_%s_%d" % (k, i))
        self.grads, self.pairs, self.reduced, self.ready = {}, {}, {}, {}
        first = [("ab_w_in", 0)]
        plan = self.gather_plan = {"l0_attn": [it for it in self.stage(0) if it not in first]}
        for o in range(1, depth, 2):
            e = o - 1
            plan.setdefault("l%d_attn" % e, []).append(("ffn_w_gate", o))
            plan["l%d_ffn_in" % e] = [("s5_w_glu1", o), ("ffn_w_up", o)]
            plan["l%d_ffn_out" % e] = [("s5_w_glu2", o)]
            plan["l%d_s5" % o] = [("ffn_w_down", o)] + (self.mix(o + 1) if o + 1 < depth else [])
            if o + 1 < depth:
                plan["l%d_ffn_in" % o] = [("ffn_w_gate", o + 1)]
                plan["l%d_attn" % (o + 1)] = [("ffn_w_up", o + 1), ("ffn_w_down", o + 1)]
        self.swap_plan, self.exchange_plan = {}, {}
        for i in range(depth):
            above = self.mix(i + 1) if i + 1 < depth else []
            if above:
                self.swap_plan["l%d_b_ffn_act" % i] = above
            self.swap_plan["l%d_b_du2" % i] = self.ffn(i)
            self.exchange_plan["l%d_b_%s" % (i, "attn" if i % 2 == 0 else "s5")] = above + self.ffn(i)
        self.swap_plan["l0_b_dcat"] = [("ab_w_o", 0)]
        self.exchange_plan["l0_b_attn"].append(("ab_w_o", 0))
        self.swap_plan["l0_b_du_qkv"] = [("ab_w_in", 0)]
        self.exchange_plan["l0_b_du_rest"] = [("ab_w_in", 0)]
        self._store(first, comm_call("gather_first", self._gather(first, 0.5)))

    def keys(self, i):
        return self.EVEN if i % 2 == 0 else self.ODD

    def stage(self, i):
        return [(k, i) for k in self.keys(i)]

    def mix(self, i):
        return [(k, i) for k in self.keys(i) if not k.startswith("ffn")]

    @staticmethod
    def ffn(i):
        return [("ffn_w_gate", i), ("ffn_w_up", i), ("ffn_w_down", i)]

    @staticmethod
    def index(key, i):
        return i if key.startswith("ffn") else i // 2

    def _layouts(self, items):
        return [self.LAYOUT[k] for k, _ in items]

    def _gather(self, items, middle_frac):
        comm = gather_comm([self.bufs[it] for it in items], self._layouts(items))
        comm.middle_frac = middle_frac
        return comm

    def _store(self, items, bufs):
        for it, b in zip(items, bufs):
            self.bufs[it] = b

    def ride(self, tag):
        if tag in self.gather_plan:
            return self._gather(self.gather_plan[tag], 0.85 if tag == "l0_attn" else 0.7)
        if tag in self.swap_plan:
            items = self.swap_plan[tag]
            return swap_comm([self.grads[it] for it in items], self._layouts(items))
        if tag in self.exchange_plan:
            items = self.exchange_plan[tag]
            return exchange_comm([self.pairs[it][1] for it in items], self._layouts(items))
        return None

    def arrived(self, tag, outs):
        if tag in self.gather_plan:
            self._store(self.gather_plan[tag], outs)
        elif tag in self.swap_plan:
            self._pair_sums(self.swap_plan[tag], outs)
        elif tag in self.exchange_plan:
            self._totals(self.exchange_plan[tag], outs)

    def _pair_sums(self, items, received):
        for it, r in zip(items, received):
            self.pairs[it] = pair_sum(self.grads[it], r, self.LAYOUT[it[0]], "pair_sum_%s_%d" % it)

    def _totals(self, items, got):
        for it, g in zip(items, got):
            k, i = it
            self.reduced[k] = reduce_total(self.pairs[it][0], g, self.LAYOUT[k], self.index(k, i),
                                           self.shards[k].shape[0], self.reduced.get(k), "reduce_total_%s_%d" % it)

    def weights(self, i):
        return _LayerWeights(self, i)

    def weight(self, i, name):
        if (name, i) not in self.ready:
            if name in ("w_qkv", "w_rest"):
                b = self.bufs["ab_w_in", i]
                w_in = jnp.transpose(b, (1, 0, 2)).reshape(b.shape[1], 4 * b.shape[2])
                self.ready["w_qkv", i], self.ready["w_rest", i] = split_w_in(w_in)
            else:
                k = {"w_o": "ab_w_o", "w_glu1": "s5_w_glu1", "w_glu2": "s5_w_glu2"}.get(name, "ffn_" + name)
                b = self.bufs[k, i]
                self.ready[name, i] = b.reshape(4 * b.shape[1], b.shape[2]) if self.LAYOUT[k] == "S" else b
        return self.ready[name, i]

    def put_grads(self, i, wg):
        for k in self.keys(i):
            _, R, C = self.shards[k].shape
            name = {"ab_w_in": "w_qkv", "ab_w_o": "w_o", "s5_w_glu1": "w_glu1", "s5_w_glu2": "w_glu2"}.get(k, k[4:])
            if name not in wg:
                continue
            if k == "ab_w_in":
                dw = merge_dw_in(wg["w_qkv"], wg["w_rest"])
                self.grads[k, i] = jnp.transpose(dw.reshape(R, 4, C), (1, 0, 2))
            else:
                self.grads[k, i] = wg[name].reshape(4, R, C) if self.LAYOUT[k] == "S" else wg[name]

    def finish(self, beside):
        names = list(self.LAYOUT)
        outs = comm_call("share_reduced", merge_comms(share_comm([self.reduced[k] for k in names]), beside))
        return dict(zip(names, outs)), outs[len(names):]


def split_w_in(w_in):
    fg0 = 3 * ATTN_W
    w_rest = jnp.concatenate([w_in[:, fg0 + HEADS:], w_in[:, fg0:fg0 + HEADS],
                              jnp.zeros((w_in.shape[0], LANES - HEADS), w_in.dtype)], axis=1)
    return w_in[:, :fg0], w_rest


def merge_dw_in(dw_qkv, dw_rest):
    nqc = dw_rest.shape[1] - LANES
    return jnp.concatenate([dw_qkv, dw_rest[:, nqc:nqc + HEADS], dw_rest[:, :nqc]], axis=1)


def device_step(x, target, P, stager):
    D = x.shape[-1]
    n_even, n_odd = P["ab_b_f"].shape[0], P["s5_d"].shape[0]
    conv_c = P["ab_conv_b"].shape[1]
    b_f_pad = jnp.pad(P["ab_b_f"], ((0, 0), (0, LANES - HEADS))).reshape(n_even, 1, LANES)

    s5, s5_vjps = [], []
    for j in range(n_odd):
        disc, vjp = jax.vjp(_s5_discretize, P["s5_a_re"][j], P["s5_a_im"][j], P["s5_log_step"][j],
                            P["s5_b_re"][j], P["s5_b_im"][j])
        lb_re, lb_im, bb_re, bb_im = disc
        tab, rtab = _s5_tables(lb_re, lb_im)
        bmat, cmat = _s5_block_mats(bb_re, bb_im, P["s5_c_re"][j], P["s5_c_im"][j])
        s5.append(dict(tab=tab, rtab=rtab, bmat=bmat.astype(BF16), cmat=cmat.astype(BF16),
                       bmat_t=jnp.transpose(bmat, (0, 2, 1)).astype(BF16),
                       cmat_t=jnp.transpose(cmat, (0, 2, 1)).astype(BF16)))
        s5_vjps.append(vjp)

    loss, grad_x, G = local_step(
        x, target, P["meta_tokens"], P["norm_g"], b_f_pad, P["ab_conv_w"],
        P["ab_conv_b"].reshape(n_even, 1, conv_c), s5, P["s5_d"].reshape(n_odd, 1, D), stager)

    out = {
        "meta_tokens": G["meta"],
        "norm_g": jnp.stack(G["norm_g"]),
        "ab_b_f": jnp.stack([b[0, :HEADS] for b in G["b_f"]]),
        "ab_conv_w": jnp.stack(G["conv_w"]),
        "ab_conv_b": jnp.stack([b[0] for b in G["conv_b"]]),
        "s5_d": jnp.stack([d[0] for d in G["s5_d"]]),
    }
    s5g = {k: [] for k in ("s5_a_re", "s5_a_im", "s5_log_step", "s5_b_re", "s5_b_im", "s5_c_re", "s5_c_im")}
    for j in range(n_odd):
        dbb_re, dbb_im, dc_re, dc_im, dl_re, dl_im = _s5_unblock(G["s5_dB"][j], G["s5_dC"][j], G["s5_dlam"][j])
        da_re, da_im, dls, db_re, db_im = s5_vjps[j]((dl_re, dl_im, dbb_re, dbb_im))
        for k, val in zip(s5g, (da_re, da_im, dls, db_re, db_im, dc_re, dc_im)):
            s5g[k].append(val)
    out.update({k: jnp.stack(v) for k, v in s5g.items()})
    return loss, grad_x, out


def kernel(x, meta_tokens, norm_g, ab_w_in, ab_b_f, ab_conv_w, ab_conv_b, ab_w_o, s5_a_re, s5_a_im, s5_log_step, s5_b_re, s5_b_im, s5_c_re, s5_c_im, s5_d, s5_w_glu1, s5_w_glu2, ffn_w_gate, ffn_w_up, ffn_w_down, loss_target, m_meta_tokens, m_norm_g, m_ab_w_in, m_ab_b_f, m_ab_conv_w, m_ab_conv_b, m_ab_w_o, m_s5_a_re, m_s5_a_im, m_s5_log_step, m_s5_b_re, m_s5_b_im, m_s5_c_re, m_s5_c_im, m_s5_d, m_s5_w_glu1, m_s5_w_glu2, m_ffn_w_gate, m_ffn_w_up, m_ffn_w_down, v_meta_tokens, v_norm_g, v_ab_w_in, v_ab_b_f, v_ab_conv_w, v_ab_conv_b, v_ab_w_o, v_s5_a_re, v_s5_a_im, v_s5_log_step, v_s5_b_re, v_s5_b_im, v_s5_c_re, v_s5_c_im, v_s5_d, v_s5_w_glu1, v_s5_w_glu2, v_ffn_w_gate, v_ffn_w_up, v_ffn_w_down):
    names = ["meta_tokens", "norm_g", "ab_w_in", "ab_b_f", "ab_conv_w", "ab_conv_b", "ab_w_o", "s5_a_re", "s5_a_im",
             "s5_log_step", "s5_b_re", "s5_b_im", "s5_c_re", "s5_c_im", "s5_d", "s5_w_glu1", "s5_w_glu2",
             "ffn_w_gate", "ffn_w_up", "ffn_w_down"]
    W = dict(zip(names, [meta_tokens, norm_g, ab_w_in, ab_b_f, ab_conv_w, ab_conv_b, ab_w_o, s5_a_re, s5_a_im,
                         s5_log_step, s5_b_re, s5_b_im, s5_c_re, s5_c_im, s5_d, s5_w_glu1, s5_w_glu2,
                         ffn_w_gate, ffn_w_up, ffn_w_down]))
    Mo = dict(zip(names, [m_meta_tokens, m_norm_g, m_ab_w_in, m_ab_b_f, m_ab_conv_w, m_ab_conv_b, m_ab_w_o, m_s5_a_re,
                          m_s5_a_im, m_s5_log_step, m_s5_b_re, m_s5_b_im, m_s5_c_re, m_s5_c_im, m_s5_d, m_s5_w_glu1,
                          m_s5_w_glu2, m_ffn_w_gate, m_ffn_w_up, m_ffn_w_down]))
    Vo = dict(zip(names, [v_meta_tokens, v_norm_g, v_ab_w_in, v_ab_b_f, v_ab_conv_w, v_ab_conv_b, v_ab_w_o, v_s5_a_re,
                          v_s5_a_im, v_s5_log_step, v_s5_b_re, v_s5_b_im, v_s5_c_re, v_s5_c_im, v_s5_d, v_s5_w_glu1,
                          v_s5_w_glu2, v_ffn_w_gate, v_ffn_w_up, v_ffn_w_down]))
    D = x.shape[-1]
    n_even, n_odd, depth = ab_w_in.shape[0], s5_w_glu1.shape[0], ffn_w_gate.shape[0]
    chip = 2 * lax.axis_index("x") + lax.axis_index("y")

    big = list(MeshStager.LAYOUT)
    stager = MeshStager({k: W[k] for k in big})
    g_meta, g_norm, g_convw, g_s5d = allgather_small([meta_tokens, norm_g, ab_conv_w, s5_d])
    full = {k: W[k] for k in names if k not in big}
    full["meta_tokens"] = jnp.transpose(g_meta, (1, 0, 2)).reshape(N_META, D)
    full["norm_g"] = jnp.transpose(g_norm, (1, 2, 0, 3)).reshape(depth, 4, D)
    full["ab_conv_w"] = jnp.transpose(g_convw, (1, 2, 0, 3)).reshape(n_even, CONV_K, -1)
    full["s5_d"] = jnp.transpose(g_s5d, (1, 0, 2)).reshape(n_odd, D)

    loss, grad_x, G = device_step(x[0], loss_target[0], full, stager)
    small_w = [k for k in names if k not in big]
    small_names = ["loss"] + small_w
    G["loss"] = loss
    packed = _pack([G[k] for k in small_names])
    pair = small_pair_sum(packed, comm_call("small_swap", small_swap_comm(packed))[0], "small_pair_sum")
    reduced, (got,) = stager.finish(small_exchange_comm(pair))
    total = small_chip_sum(pair, got, "small_chip_sum")

    grad, delta, new_m, new_v = {}, {}, {}, {}
    for k in big:
        (delta[k], new_m[k], new_v[k]), _ = adamw(W[k], reduced[k], Mo[k], Vo[k], "adamw_" + k)
        grad[k] = reduced[k]
    summed = dict(zip(small_names, _unpack(total, [G[k].shape for k in small_names])))
    loss_out = summed["loss"].reshape(())
    for k in ("meta_tokens", "norm_g", "ab_conv_w", "s5_d"):
        n_last = W[k].shape[-1]
        summed[k] = lax.dynamic_slice_in_dim(summed[k], chip * n_last, n_last, axis=summed[k].ndim - 1)
    shapes = [W[k].shape for k in small_w]
    (d_s, m_s, v_s), _ = adamw(_pack([W[k] for k in small_w])[None], _pack([summed[k] for k in small_w])[None],
                               _pack([Mo[k] for k in small_w])[None], _pack([Vo[k] for k in small_w])[None],
                               "adamw_small")
    delta.update(zip(small_w, _unpack(d_s, shapes)))
    new_m.update(zip(small_w, _unpack(m_s, shapes)))
    new_v.update(zip(small_w, _unpack(v_s, shapes)))
    grad.update({k: summed[k] for k in small_w})

    return (loss_out, grad_x[None], *[grad[k] for k in names], *[delta[k] for k in names],
            *[new_m[k] for k in names], *[new_v[k] for k in names])
```

```python
import functools
import math

import jax
import jax.numpy as jnp
from jax import lax
from jax.experimental import pallas as pl
from jax.experimental.pallas import tpu as pltpu

F32 = jnp.float32
BF16 = jnp.bfloat16

N_META = 16
HEADS = 16
HEAD_DIM = 64
ATTN_W = HEADS * HEAD_DIM
CONV_K = 3
S5_GROUP = 16
S5_STATE = 64
S5_MIN_DECAY = 1e-4
NORM_EPS = 1e-6
ADAM_LR = 0.001
ADAM_B1 = 0.9
ADAM_B2 = 0.999
ADAM_EPS = 1e-08
ADAM_WD = 0.01
ADAM_STEP = 10

LANES = 128
SUBLANES = 8
VMEM_LIMIT = 56 * 1024 * 1024
VMEM_TILE_BUDGET = 34 * 1024 * 1024
ROW_TILE = 384
ATTN_ROWS = 128
S5_BLOCK_GROUPS = LANES // S5_GROUP
S5_BLOCK_STATES = S5_BLOCK_GROUPS * S5_STATE
NEG_BIG = -1e30

MESH = pl.DeviceIdType.MESH
ANY = pl.BlockSpec(memory_space=pl.ANY)
VMEM_SPEC = pl.BlockSpec(memory_space=pltpu.VMEM)


def _params(sem=None):
    return pltpu.CompilerParams(dimension_semantics=sem, vmem_limit_bytes=VMEM_LIMIT)


def _div_tile(n, prefs):
    for p in prefs:
        if n % p == 0:
            return p
    return n


def _row_tile(rows, cols, itemsize=4, limit=2 * 1024 * 1024):
    for p in (512, 256, 128, 64, 32, 16):
        if rows % p == 0 and p * cols * itemsize <= limit:
            return p
    return 16 if rows % 16 == 0 else rows


def _tile_cands(n):
    c = [d for d in range(LANES, min(n, 2048) + 1, LANES) if n % d == 0]
    if not c or n <= 2048 and n not in c:
        c.append(n)
    return sorted(set(c), reverse=True)


def _mm_tiles(M, N, K, a_bytes, b_bytes, o_bytes, npairs):
    best = None
    for tk in sorted(set(_tile_cands(K) + [K]), reverse=True):
        for tm in _tile_cands(M):
            for tn in _tile_cands(N):
                mem = npairs * 2 * (tm * tk * a_bytes + tk * tn * b_bytes) + 2 * tm * tn * o_bytes + tm * tn * 4
                mem += npairs * ((tm * tk * 2 if a_bytes == 4 else 0) + (tk * tn * 2 if b_bytes == 4 else 0))
                if mem > VMEM_TILE_BUDGET:
                    continue
                key = (tk == K and tm >= 3 * LANES and tn >= 4 * LANES, tm * tn * tk, tk, tn)
                if best is None or key > best[0]:
                    best = (key, (tm, tn, tk))
    assert best is not None, (M, N, K)
    return best[1]


class Comm:
    def __init__(self, operands, out_shapes, aliases, n_sems, begin, middle=None, finish=None, middle_frac=0.5):
        self.operands, self.out_shapes, self.aliases, self.n_sems = list(operands), list(out_shapes), aliases, n_sems
        self.begin, self.middle, self.finish, self.middle_frac = begin, middle, finish, middle_frac


class _Shifted:
    def __init__(self, sems, off):
        self.sems, self.off = sems, off

    @property
    def at(self):
        return self

    def __getitem__(self, i):
        return self.sems.at[self.off + i]


def merge_comms(a, b):
    assert a.middle is None and b.middle is None
    na_in, na_out = len(a.operands), len(a.out_shapes)

    def both(stage):
        def run(ins, outs, send_sems, recv_sems):
            getattr(a, stage)(ins[:na_in], outs[:na_out], send_sems, recv_sems)
            getattr(b, stage)(ins[na_in:], outs[na_out:], _Shifted(send_sems, a.n_sems), _Shifted(recv_sems, a.n_sems))
        return run

    aliases = dict(a.aliases)
    aliases.update({na_in + i: na_out + o for i, o in b.aliases.items()})
    return Comm(a.operands + b.operands, a.out_shapes + b.out_shapes, aliases, a.n_sems + b.n_sems,
                both("begin"), None, both("finish"))


def carrier_call(body, name, grid, in_specs, out_specs, out_shape, scratch_shapes, args, comm, semantics):
    n_in, n_out = len(args), len(out_shape)
    if comm is None:
        outs = pl.pallas_call(body, name=name, grid=grid, in_specs=in_specs, out_specs=out_specs, out_shape=out_shape,
                              scratch_shapes=scratch_shapes, compiler_params=_params(semantics))(*args)
        return list(outs), []
    ci, co = len(comm.operands), len(comm.out_shapes)
    total = math.prod(grid)
    middle_at = min(total - 1, max(0, int(total * comm.middle_frac)))

    def carried(*refs):
        ins, cins = refs[:n_in], refs[n_in:n_in + ci]
        outs = refs[n_in + ci:n_in + ci + n_out]
        couts = refs[n_in + ci + n_out:n_in + ci + n_out + co]
        scratch, (send_sems, recv_sems) = refs[n_in + ci + n_out + co:-2], refs[-2:]
        step = 0
        for d, size in enumerate(grid):
            step = step * size + pl.program_id(d)

        @pl.when(step == 0)
        def _():
            comm.begin(cins, couts, send_sems, recv_sems)

        if comm.middle is not None:
            @pl.when(step == middle_at)
            def _():
                comm.middle(cins, couts, send_sems, recv_sems)

        body(*ins, *outs, *scratch)

        @pl.when(step == total - 1)
        def _():
            comm.finish(cins, couts, send_sems, recv_sems)

    outs = pl.pallas_call(
        carried, name=name, grid=grid,
        in_specs=list(in_specs) + [ANY] * ci, out_specs=list(out_specs) + [ANY] * co,
        out_shape=list(out_shape) + comm.out_shapes,
        scratch_shapes=list(scratch_shapes) + [pltpu.SemaphoreType.DMA((comm.n_sems,)),
                                                pltpu.SemaphoreType.DMA((comm.n_sems,))],
        input_output_aliases={n_in + i: n_out + o for i, o in comm.aliases.items()},
        compiler_params=pltpu.CompilerParams(dimension_semantics=("arbitrary",) * len(grid),
                                             vmem_limit_bytes=VMEM_LIMIT, has_side_effects=True),
    )(*args, *comm.operands)
    return list(outs[:n_out]), list(outs[n_out:])


def comm_call(name, comm):
    ci = len(comm.operands)

    def body(*refs):
        cins, couts = refs[:ci], refs[ci:ci + len(comm.out_shapes)]
        send_sems, recv_sems = refs[-2:]
        comm.begin(cins, couts, send_sems, recv_sems)
        if comm.middle is not None:
            comm.middle(cins, couts, send_sems, recv_sems)
        comm.finish(cins, couts, send_sems, recv_sems)

    return pl.pallas_call(
        body, name=name, in_specs=[ANY] * ci, out_specs=[ANY] * len(comm.out_shapes), out_shape=comm.out_shapes,
        input_output_aliases=dict(comm.aliases),
        scratch_shapes=[pltpu.SemaphoreType.DMA((comm.n_sems,)), pltpu.SemaphoreType.DMA((comm.n_sems,))],
        compiler_params=pltpu.CompilerParams(has_side_effects=True),
    )(*comm.operands)


_DIMS ={"nn": (((1,), (0,)), ((), ())), "nt": (((1,), (1,)), ((), ())), "tn": (((0,), (0,)), ((), ()))}


def matmul(pairs, kind, out_dtype, name, comm=None, add=None):
    a0, b0 = pairs[0]
    if kind == "nn":
        (M, K), N = a0.shape, b0.shape[1]
    elif kind == "nt":
        (M, K), N = a0.shape, b0.shape[0]
    else:
        (K, M), N = a0.shape, b0.shape[1]
    tm, tn, tk = _mm_tiles(M, N, K, a0.dtype.itemsize, b0.dtype.itemsize, jnp.dtype(out_dtype).itemsize, len(pairs))
    nk = K // tk
    dims = _DIMS[kind]
    npairs = len(pairs)
    n_in = 2 * npairs + (add is not None)

    def body(*refs):
        ins, o_ref = refs[:2 * npairs], refs[n_in]
        part = None
        for p in range(npairs):
            d = lax.dot_general(ins[2 * p][...].astype(BF16), ins[2 * p + 1][...].astype(BF16), dims,
                                preferred_element_type=F32)
            part = d if part is None else part + d

        def finish(total):
            if add is not None:
                total = total + refs[2 * npairs][...]
            o_ref[...] = total.astype(o_ref.dtype)

        if nk == 1:
            finish(part)
        else:
            acc_ref = refs[n_in + 1]
            k = pl.program_id(2)

            @pl.when(k == 0)
            def _():
                acc_ref[...] = part

            @pl.when(k > 0)
            def _():
                acc_ref[...] += part

            @pl.when(k == nk - 1)
            def _():
                finish(acc_ref[...])

    if kind == "nn":
        a_blk, a_map = (tm, tk), lambda j, i, k: (i, k)
        b_blk, b_map = (tk, tn), lambda j, i, k: (k, j)
    elif kind == "nt":
        a_blk, a_map = (tm, tk), lambda j, i, k: (i, k)
        b_blk, b_map = (tn, tk), lambda j, i, k: (j, k)
    else:
        a_blk, a_map = (tk, tm), lambda j, i, k: (k, i)
        b_blk, b_map = (tk, tn), lambda j, i, k: (k, j)
    o_spec = pl.BlockSpec((tm, tn), lambda j, i, k: (i, j))
    (out,), arrived = carrier_call(
        body, name, (N // tn, M // tm, nk),
        [pl.BlockSpec(a_blk, a_map), pl.BlockSpec(b_blk, b_map)] * npairs + ([o_spec] if add is not None else []),
        [o_spec], [jax.ShapeDtypeStruct((M, N), out_dtype)],
        [] if nk == 1 else [pltpu.VMEM((tm, tn), F32)],
        [t for ab in pairs for t in ab] + ([add] if add is not None else []), comm,
        ("parallel", "parallel", "arbitrary"))
    return out if comm is None else ([out], arrived)


def _sigmoid(x):
    return 1.0 / (1.0 + jnp.exp(-x))

def dual_matmul_act(x, w1, w2, act, out_dtype, name, comm=None):
    M, K = x.shape
    N = w1.shape[-1]
    tm = _div_tile(M, (ROW_TILE,))
    tn = _div_tile(N, (1408, 1024, 512, 256, 128))

    def body(x_ref, w1_ref, w2_ref, o1_ref, o2_ref, out_ref):
        xv = x_ref[...]
        o1 = jnp.dot(xv, w1_ref[...], preferred_element_type=F32)
        o2 = jnp.dot(xv, w2_ref[...], preferred_element_type=F32)
        o1_ref[...] = o1.astype(BF16)
        o2_ref[...] = o2.astype(BF16)
        if act == "swiglu":
            out = o1 * _sigmoid(o1) * o2
        else:
            out = o1 * _sigmoid(o2)
        out_ref[...] = out.astype(out_ref.dtype)

    w_spec = pl.BlockSpec((K, tn), lambda j, i: (0, j))
    o_spec = pl.BlockSpec((tm, tn), lambda j, i: (i, j))
    return carrier_call(
        body, name, (N // tn, M // tm), [pl.BlockSpec((tm, K), lambda j, i: (i, 0)), w_spec, w_spec],
        [o_spec, o_spec, o_spec],
        [jax.ShapeDtypeStruct((M, N), BF16), jax.ShapeDtypeStruct((M, N), BF16),
         jax.ShapeDtypeStruct((M, N), out_dtype)], [], (x, w1, w2), comm, ("parallel", "parallel"))


def ffn_bwd_act(df, wd, a, b, name, comm=None):
    M, K = df.shape
    N = wd.shape[0]
    tm = _div_tile(M, (ROW_TILE,))
    tn = _div_tile(N, (1408, 1024, 512, 256, 128))

    def body(df_ref, wd_ref, a_ref, b_ref, da_ref, db_ref):
        dh = lax.dot_general(df_ref[...], wd_ref[...], _DIMS["nt"], preferred_element_type=F32)
        av = a_ref[...].astype(F32)
        bv = b_ref[...].astype(F32)
        sig = _sigmoid(av)
        silu = av * sig
        da_ref[...] = (dh * bv * (sig + silu * (1.0 - sig))).astype(BF16)
        db_ref[...] = (dh * silu).astype(BF16)

    t_spec = pl.BlockSpec((tm, tn), lambda j, i: (i, j))
    return carrier_call(
        body, name, (N // tn, M // tm),
        [pl.BlockSpec((tm, K), lambda j, i: (i, 0)), pl.BlockSpec((tn, K), lambda j, i: (j, 0)), t_spec, t_spec],
        [t_spec, t_spec], [jax.ShapeDtypeStruct((M, N), BF16)] * 2, [], (df, wd, a, b), comm,
        ("parallel", "parallel"))


def glu_bwd_act(dout, o1, o2, name):
    M, N = dout.shape
    tm = _div_tile(M, (ROW_TILE,))

    def body(d_ref, o1_ref, o2_ref, d1_ref, d2_ref):
        d = d_ref[...].astype(F32)
        sig = _sigmoid(o2_ref[...].astype(F32))
        d1_ref[...] = (d * sig).astype(BF16)
        d2_ref[...] = (d * o1_ref[...].astype(F32) * sig * (1.0 - sig)).astype(BF16)

    spec = pl.BlockSpec((tm, N), lambda i: (i, 0))
    return pl.pallas_call(
        body, name=name, grid=(M // tm,), in_specs=[spec] * 3, out_specs=[spec] * 2,
        out_shape=[jax.ShapeDtypeStruct((M, N), BF16)] * 2,
        compiler_params=_params(("parallel",)),
    )(dout, o1, o2)


def rmsnorm_fwd(x, g, out_dtype, name, residual=None):
    L, D = x.shape
    tr = _div_tile(L, (ROW_TILE,))
    has_res = residual is not None

    def body(*refs):
        x_ref, g_ref = refs[0], refs[1]
        o_ref = refs[-1]
        xv = x_ref[...]
        r = lax.rsqrt(jnp.mean(xv * xv, axis=-1, keepdims=True) + NORM_EPS)
        y = xv * r * g_ref[...]
        if has_res:
            y = refs[2][...] + y
        o_ref[...] = y.astype(o_ref.dtype)

    row = pl.BlockSpec((tr, D), lambda i: (i, 0))
    gsp = pl.BlockSpec((1, D), lambda i: (0, 0))
    args = (x, g) + ((residual,) if has_res else ())
    return pl.pallas_call(
        body, name=name, grid=(L // tr,), in_specs=[row, gsp] + ([row] if has_res else []), out_specs=row,
        out_shape=jax.ShapeDtypeStruct((L, D), out_dtype), compiler_params=_params(("parallel",)),
    )(*args)


def rmsnorm_bwd(x, g, dy, out_dtype, name, add=None, dy2=None):
    L, D = x.shape
    tr = _div_tile(L, (ROW_TILE,))
    has_add = add is not None
    has_dy2 = dy2 is not None

    def body(*refs):
        x_ref, g_ref, dy_ref = refs[0], refs[1], refs[2]
        dx_ref, dg_ref = refs[-2], refs[-1]
        xv = x_ref[...]
        dyv = dy_ref[...].astype(F32)
        if has_dy2:
            dyv = dyv + refs[3][...].astype(F32)
        r = lax.rsqrt(jnp.mean(xv * xv, axis=-1, keepdims=True) + NORM_EPS)
        t = dyv * g_ref[...]
        dx = r * t - xv * (r * r * r) * jnp.mean(xv * t, axis=-1, keepdims=True)
        if has_add:
            dx = refs[3 + has_dy2][...] + dx
        dx_ref[...] = dx.astype(dx_ref.dtype)
        dgp = jnp.sum(dyv * xv * r, axis=0, keepdims=True)

        @pl.when(pl.program_id(0) == 0)
        def _():
            dg_ref[...] = dgp

        @pl.when(pl.program_id(0) > 0)
        def _():
            dg_ref[...] += dgp

    row = pl.BlockSpec((tr, D), lambda i: (i, 0))
    gsp = pl.BlockSpec((1, D), lambda i: (0, 0))
    args = (x, g, dy) + ((dy2,) if has_dy2 else ()) + ((add,) if has_add else ())
    return pl.pallas_call(
        body, name=name, grid=(L // tr,), in_specs=[row, gsp] + [row] * (len(args) - 2),
        out_specs=[row, gsp],
        out_shape=[jax.ShapeDtypeStruct((L, D), out_dtype), jax.ShapeDtypeStruct((1, D), F32)],
        compiler_params=_params(("arbitrary",)),
    )(*args)


def _gate_z(fg_ref, b_ref):
    return fg_ref[...] + b_ref[...]


def gate_fwd(fg_src, col_block, b, name):
    L = fg_src.shape[0]
    T = _div_tile(L, (ROW_TILE,))

    def body(fg_ref, b_ref, c_ref, carry):
        @pl.when(pl.program_id(0) == 0)
        def _():
            carry[...] = jnp.zeros_like(carry)

        z = _gate_z(fg_ref, b_ref)
        logf = jnp.minimum(z, 0.0) - jnp.log(1.0 + jnp.exp(-jnp.abs(z)))
        tri = (lax.broadcasted_iota(jnp.int32, (T, T), 1) <= lax.broadcasted_iota(jnp.int32, (T, T), 0)).astype(F32)
        c = jnp.dot(tri, logf, precision=lax.Precision.HIGHEST, preferred_element_type=F32) + carry[...]
        c_ref[...] = c
        carry[...] = c[T - 1:T, :]

    return pl.pallas_call(
        body, name=name, grid=(L // T,),
        in_specs=[pl.BlockSpec((T, LANES), lambda i: (i, col_block)), pl.BlockSpec((1, LANES), lambda i: (0, 0))],
        out_specs=pl.BlockSpec((T, LANES), lambda i: (i, 0)),
        out_shape=jax.ShapeDtypeStruct((L, LANES), F32),
        scratch_shapes=[pltpu.VMEM((1, LANES), F32)],
        compiler_params=_params(("arbitrary",)),
    )(fg_src, b)


def gate_bwd(fg_src, col_block, b, dc, name):
    L = fg_src.shape[0]
    T = _div_tile(L, (ROW_TILE,))
    nb = L // T

    def body(fg_ref, b_ref, dc_ref, dfg_ref, db_ref, carry):
        @pl.when(pl.program_id(0) == 0)
        def _():
            carry[...] = jnp.zeros_like(carry)
            db_ref[...] = jnp.zeros_like(db_ref)

        z = _gate_z(fg_ref, b_ref)
        dcv = dc_ref[...]
        tri = (lax.broadcasted_iota(jnp.int32, (T, T), 1) >= lax.broadcasted_iota(jnp.int32, (T, T), 0)).astype(F32)
        dlogf = jnp.dot(tri, dcv, precision=lax.Precision.HIGHEST, preferred_element_type=F32) + carry[...]
        dfg = dlogf * _sigmoid(-z)
        dfg_ref[...] = dfg
        db_ref[...] += jnp.sum(dfg, axis=0, keepdims=True)
        carry[...] = dlogf[0:1, :]

    return pl.pallas_call(
        body, name=name, grid=(nb,),
        in_specs=[pl.BlockSpec((T, LANES), lambda i: (nb - 1 - i, col_block)),
                  pl.BlockSpec((1, LANES), lambda i: (0, 0)),
                  pl.BlockSpec((T, LANES), lambda i: (nb - 1 - i, 0))],
        out_specs=[pl.BlockSpec((T, LANES), lambda i: (nb - 1 - i, 0)), pl.BlockSpec((1, LANES), lambda i: (0, 0))],
        out_shape=[jax.ShapeDtypeStruct((L, LANES), F32), jax.ShapeDtypeStruct((1, LANES), F32)],
        scratch_shapes=[pltpu.VMEM((1, LANES), F32)],
        compiler_params=_params(("arbitrary",)),
    )(fg_src, b, dc)


def attn_fwd(proj, cq_col, ck_row, name, comm=None):
    L = proj.shape[0]
    T = _div_tile(L, (ROW_TILE,))
    nq = L // T
    npair = HEADS // 2
    scale = HEAD_DIM ** -0.5
    SUB = ATTN_ROWS
    nsub = T // SUB

    def body(q_ref, k_ref, v_ref, cq_ref, ck_ref, o_ref, lse_ref):
        qb = pl.program_id(1)
        rows = [slice(r * SUB, (r + 1) * SUB) for r in range(nsub)]
        head1 = lax.broadcasted_iota(jnp.int32, (SUB, LANES), 1) >= HEAD_DIM
        qs = [[jnp.where(head1 == (h == 1), q_ref[rs, :] * scale, 0.0).astype(BF16) for rs in rows] for h in range(2)]
        cqs = [[cq_ref[0, rs, h:h + 1] for rs in rows] for h in range(2)]

        def logits(kb):
            ks = pl.multiple_of(kb * T, T)
            k = k_ref[pl.ds(ks, T), :]
            return tuple(lax.dot_general(qs[h][r], k, _DIMS["nt"], preferred_element_type=F32) + cqs[h][r]
                         - ck_ref[0, h:h + 1, pl.ds(ks, T)] for h in range(2) for r in range(nsub))

        def softmax_step(kb, s_all, carry, masked):
            ks = pl.multiple_of(kb * T, T)
            v = v_ref[pl.ds(ks, T), :]
            lane = lax.broadcasted_iota(jnp.int32, (T, LANES), 1)
            new = []
            for h in range(2):
                vh = jnp.where(lane == spare[h], 1.0, v).astype(BF16)
                for r in range(nsub):
                    m, acc = carry[h * nsub + r]
                    s, vr = s_all[h * nsub + r], vh
                    if masked:
                        n = (r + 1) * SUB
                        s, vr = s[:, :n], vh[:n]
                        keep = (lax.broadcasted_iota(jnp.int32, (SUB, n), 1)
                                <= lax.broadcasted_iota(jnp.int32, (SUB, n), 0) + r * SUB)
                        s = jnp.where(keep, s, NEG_BIG)
                    m_new = jnp.maximum(m, jnp.max(s, axis=1, keepdims=True))
                    p = jnp.exp(s - m_new)
                    acc = jnp.exp(m - m_new) * acc + jnp.dot(p.astype(BF16), vr, preferred_element_type=F32)
                    new.append((m_new, acc))
            return tuple(new)

        def step(kb, state):
            s_all, carry = state
            s_next = logits(kb + 1)
            return s_next, softmax_step(kb, s_all, carry, False)

        spare = (HEAD_DIM, 0)
        one = (jnp.full((SUB, 1), NEG_BIG, F32), jnp.zeros((SUB, LANES), F32))
        s_all, carry = lax.fori_loop(0, qb, step, (logits(0), (one,) * (2 * nsub)))
        carry = softmax_step(qb, s_all, carry, True)
        out, lse = [], []
        for h in range(2):
            chains = carry[h * nsub:(h + 1) * nsub]
            ls = [acc[:, spare[h]:spare[h] + 1] for _, acc in chains]
            out.append(jnp.concatenate([acc / l for (_, acc), l in zip(chains, ls)], axis=0))
            lse.append(jnp.concatenate([m + jnp.log(l) for (m, _), l in zip(chains, ls)], axis=0))
        o_ref[...] = jnp.where(lax.broadcasted_iota(jnp.int32, (T, LANES), 1) >= HEAD_DIM, out[1], out[0]
                               ).astype(o_ref.dtype)
        lse_ref[0] = jnp.concatenate(lse, axis=1)

    return carrier_call(
        body, name, (npair, nq),
        [pl.BlockSpec((T, LANES), lambda p, i: (i, p)),
         pl.BlockSpec((L, LANES), lambda p, i: (0, npair + p)),
         pl.BlockSpec((L, LANES), lambda p, i: (0, 2 * npair + p)),
         pl.BlockSpec((1, T, 2), lambda p, i: (p, i, 0)),
         pl.BlockSpec((1, 2, L), lambda p, i: (p, 0, 0))],
        [pl.BlockSpec((T, LANES), lambda p, i: (i, p)), pl.BlockSpec((1, T, 2), lambda p, i: (p, i, 0))],
        [jax.ShapeDtypeStruct((L, ATTN_W), BF16), jax.ShapeDtypeStruct((npair, L, 2), F32)],
        [], (proj, proj, proj, cq_col, ck_row), comm, ("parallel", "parallel"))


def attn_delta(dcat, cat, name):
    L = dcat.shape[0]
    T = _div_tile(L, (ROW_TILE,))

    def body(do_ref, o_ref, d_ref):
        prod = do_ref[...] * o_ref[...].astype(F32)
        sel = (lax.broadcasted_iota(jnp.int32, (ATTN_W, LANES), 0) // HEAD_DIM
               == lax.broadcasted_iota(jnp.int32, (ATTN_W, LANES), 1)).astype(F32)
        d_ref[...] = jnp.dot(prod, sel, precision=lax.Precision.HIGHEST, preferred_element_type=F32)

    return pl.pallas_call(
        body, name=name, grid=(L // T,),
        in_specs=[pl.BlockSpec((T, ATTN_W), lambda i: (i, 0)), pl.BlockSpec((T, ATTN_W), lambda i: (i, 0))],
        out_specs=pl.BlockSpec((T, LANES), lambda i: (i, 0)),
        out_shape=jax.ShapeDtypeStruct((L, LANES), F32),
        compiler_params=_params(("parallel",)),
    )(dcat, cat)


def attn_bwd(proj, dcat, lse_row, delta_row, cq_row, ck_col, name, comm=None):
    L = proj.shape[0]
    T = _div_tile(L, (ROW_TILE,))
    nb = L // T
    npair = HEADS // 2
    scale = HEAD_DIM ** -0.5

    def body(q_ref, k_ref, v_ref, do_ref, lse_ref, dl_ref, cq_ref, ck_ref,
             dq_ref, dk_ref, dv_ref, dcq_ref, dck_ref, dq_acc, dcq_acc):
        kb = pl.program_id(1)

        @pl.when(kb == 0)
        def _():
            dq_acc[...] = jnp.zeros_like(dq_acc)
            dcq_acc[...] = jnp.zeros_like(dcq_acc)

        head1 = lax.broadcasted_iota(jnp.int32, (T, LANES), 1) >= HEAD_DIM
        ks = [jnp.where(head1 == (h == 1), k_ref[...] * scale, 0.0).astype(BF16) for h in range(2)]
        vs = [jnp.where(head1 == (h == 1), v_ref[...], 0.0).astype(BF16) for h in range(2)]
        cks = [ck_ref[0, :, h:h + 1] for h in range(2)]
        kts = [k.T for k in ks]

        def step(qb, carry, masked):
            qs = pl.multiple_of(qb * T, T)
            q = q_ref[pl.ds(qs, T), :]
            do = do_ref[pl.ds(qs, T), :].astype(BF16)
            new, dq = [], None
            for h in range(2):
                dk, dv, dck = carry[h]
                lse = lse_ref[0, h:h + 1, pl.ds(qs, T)]
                dl = dl_ref[0, h:h + 1, pl.ds(qs, T)]
                cq = cq_ref[0, h:h + 1, pl.ds(qs, T)]
                st = lax.dot_general(ks[h], q, _DIMS["nt"], preferred_element_type=F32) + cq - cks[h]
                if masked:
                    keep = lax.broadcasted_iota(jnp.int32, (T, T), 0) <= lax.broadcasted_iota(jnp.int32, (T, T), 1)
                    st = jnp.where(keep, st, NEG_BIG)
                pt = jnp.exp(st - lse)
                dv = dv + jnp.dot(pt.astype(BF16), do, preferred_element_type=F32)
                dpt = lax.dot_general(vs[h], do, _DIMS["nt"], preferred_element_type=F32)
                dst = pt * (dpt - dl)
                dsb = dst.astype(BF16)
                dk = dk + jnp.dot(dsb, q, preferred_element_type=F32)
                part = jnp.dot(kts[h], dsb, preferred_element_type=F32)
                dq = part if dq is None else dq + part
                dcq_acc[h:h + 1, pl.ds(qs, T)] += jnp.sum(dst, axis=0, keepdims=True)
                dck = dck + jnp.sum(dst, axis=1, keepdims=True)
                new.append((dk, dv, dck))
            dq_acc[:, pl.ds(qs, T)] += dq
            return tuple(new)

        one = (jnp.zeros((T, LANES), F32), jnp.zeros((T, LANES), F32), jnp.zeros((T, 1), F32))
        carry = step(kb, (one, one), True)
        carry = lax.fori_loop(kb + 1, nb, functools.partial(step, masked=False), carry)
        (dk0, dv0, dck0), (dk1, dv1, dck1) = carry
        dk_ref[...] = (jnp.where(head1, dk1, dk0) * scale).astype(dk_ref.dtype)
        dv_ref[...] = jnp.where(head1, dv1, dv0).astype(dv_ref.dtype)
        dck_ref[0] = jnp.concatenate([-dck0, -dck1], axis=1)

        @pl.when(kb == nb - 1)
        def _():
            dq_ref[...] = dq_acc[...].T.astype(dq_ref.dtype)
            dcq_ref[0] = dcq_acc[...]

    full = lambda col: pl.BlockSpec((L, LANES), col)
    row_stat = pl.BlockSpec((1, 2, L), lambda p, i: (p, 0, 0))
    return carrier_call(
        body, name, (npair, nb),
        [full(lambda p, i: (0, p)),
         pl.BlockSpec((T, LANES), lambda p, i: (i, npair + p)),
         pl.BlockSpec((T, LANES), lambda p, i: (i, 2 * npair + p)),
         full(lambda p, i: (0, p)),
         row_stat, row_stat, row_stat,
         pl.BlockSpec((1, T, 2), lambda p, i: (p, i, 0))],
        [full(lambda p, i: (0, p)),
         pl.BlockSpec((T, LANES), lambda p, i: (i, p)),
         pl.BlockSpec((T, LANES), lambda p, i: (i, p)),
         row_stat,
         pl.BlockSpec((1, T, 2), lambda p, i: (p, i, 0))],
        [jax.ShapeDtypeStruct((L, ATTN_W), BF16)] * 3
        + [jax.ShapeDtypeStruct((npair, 2, L), F32), jax.ShapeDtypeStruct((npair, L, 2), F32)],
        [pltpu.VMEM((LANES, L), F32), pltpu.VMEM((2, L), F32)],
        (proj, proj, proj, dcat, lse_row, delta_row, cq_row, ck_col), comm, ("parallel", "arbitrary"))


def _shift_down(x, k):
    rolled = pltpu.roll(x, k, 0)
    return jnp.where(lax.broadcasted_iota(jnp.int32, x.shape, 0) >= k, rolled, 0.0)


def _shift_up(x, k):
    n = x.shape[0]
    rolled = pltpu.roll(x, n - k, 0)
    return jnp.where(lax.broadcasted_iota(jnp.int32, x.shape, 0) < n - k, rolled, 0.0)


def conv_fwd(proj, col0, conv_w, conv_b, name):
    L = proj.shape[0]
    C = conv_w.shape[1]
    nc = C // LANES

    def body(gb_ref, gc_ref, xc_ref, w_ref, b_ref, o_ref):
        z = gc_ref[...] * xc_ref[...]
        conv = (w_ref[0:1, :] * _shift_down(z, 2) + w_ref[1:2, :] * _shift_down(z, 1) + w_ref[2:3, :] * z
                + b_ref[...])
        o_ref[...] = (gb_ref[...] * conv).astype(o_ref.dtype)

    col = lambda off: pl.BlockSpec((L, LANES), lambda j, off=off: (0, col0 + off + j))
    return pl.pallas_call(
        body, name=name, grid=(nc,),
        in_specs=[col(0), col(nc), col(2 * nc), pl.BlockSpec((CONV_K, LANES), lambda j: (0, j)),
                  pl.BlockSpec((1, LANES), lambda j: (0, j))],
        out_specs=pl.BlockSpec((L, LANES), lambda j: (0, j)),
        out_shape=jax.ShapeDtypeStruct((L, C), BF16),
        compiler_params=_params(("parallel",)),
    )(proj, proj, proj, conv_w, conv_b)


def conv_bwd(proj, col0, conv_w, conv_b, dcat, dcol0, name):
    L = proj.shape[0]
    C = conv_w.shape[1]
    nc = C // LANES

    def body(gb_ref, gc_ref, xc_ref, w_ref, b_ref, do_ref, dgb_ref, dgc_ref, dxc_ref, dw_ref, db_ref):
        gc, xc = gc_ref[...], xc_ref[...]
        z = gc * xc
        z1, z2 = _shift_down(z, 1), _shift_down(z, 2)
        w0, w1, w2 = w_ref[0:1, :], w_ref[1:2, :], w_ref[2:3, :]
        conv = w0 * z2 + w1 * z1 + w2 * z + b_ref[...]
        dout = do_ref[...]
        dgb_ref[...] = (dout * conv).astype(dgb_ref.dtype)
        dconv = dout * gb_ref[...]
        dw_ref[...] = jnp.concatenate([jnp.sum(dconv * z2, axis=0, keepdims=True),
                                       jnp.sum(dconv * z1, axis=0, keepdims=True),
                                       jnp.sum(dconv * z, axis=0, keepdims=True)], axis=0)
        db_ref[...] = jnp.sum(dconv, axis=0, keepdims=True)
        dz = w2 * dconv + w1 * _shift_up(dconv, 1) + w0 * _shift_up(dconv, 2)
        dgc_ref[...] = (dz * xc).astype(dgc_ref.dtype)
        dxc_ref[...] = (dz * gc).astype(dxc_ref.dtype)

    col = lambda off: pl.BlockSpec((L, LANES), lambda j, off=off: (0, col0 + off + j))
    out_col = pl.BlockSpec((L, LANES), lambda j: (0, j))
    return pl.pallas_call(
        body, name=name, grid=(nc,),
        in_specs=[col(0), col(nc), col(2 * nc), pl.BlockSpec((CONV_K, LANES), lambda j: (0, j)),
                  pl.BlockSpec((1, LANES), lambda j: (0, j)),
                  pl.BlockSpec((L, LANES), lambda j: (0, dcol0 + j))],
        out_specs=[out_col, out_col, out_col, pl.BlockSpec((CONV_K, LANES), lambda j: (0, j)),
                   pl.BlockSpec((1, LANES), lambda j: (0, j))],
        out_shape=[jax.ShapeDtypeStruct((L, C), BF16)] * 3
        + [jax.ShapeDtypeStruct((CONV_K, C), F32), jax.ShapeDtypeStruct((1, C), F32)],
        compiler_params=_params(("parallel",)),
    )(proj, proj, proj, conv_w, conv_b, dcat)


_GELU_C = math.sqrt(2.0 / math.pi)
_GELU_A = 0.044715


def _gelu(y):
    return 0.5 * y * (1.0 + jnp.tanh(_GELU_C * (y + _GELU_A * y * y * y)))


def _gelu_grad(y):
    t = jnp.tanh(_GELU_C * (y + _GELU_A * y * y * y))
    return 0.5 * (1.0 + t) + 0.5 * y * (1.0 - t * t) * _GELU_C * (1.0 + 3.0 * _GELU_A * y * y)


def _cmul_add(xr, xi, pr, pi, sr, si):
    return xr + pr * sr - pi * si, xi + pr * si + pi * sr


def _scan_tile(br, bi, cr, ci, tab_ref, reverse):
    n = S5_BLOCK_STATES
    xr, xi = br, bi
    for s, k in enumerate((1, 2, 4)):
        shift = SUBLANES - k if reverse else k
        xr, xi = _cmul_add(xr, xi, tab_ref[0, s, :, :n], tab_ref[0, s, :, n:],
                           pltpu.roll(xr, shift, 0), pltpu.roll(xi, shift, 0))
    return _cmul_add(xr, xi, tab_ref[0, 3, :, :n], tab_ref[0, 3, :, n:], cr, ci)


def s5_fwd(u, bmat, cmat, tab, dvec, name, comm=None):
    L, D = u.shape
    nblk = D // LANES
    T = _div_tile(L, (ROW_TILE,))
    ns = 2 * S5_BLOCK_STATES
    n = S5_BLOCK_STATES

    def body(u_ref, b_ref, c_ref, tab_ref, d_ref, y_ref, g_ref, xs_ref, buf, car):
        @pl.when(pl.program_id(1) == 0)
        def _():
            car[...] = jnp.zeros_like(car)

        uv = u_ref[...]
        buf[...] = jnp.dot(uv.astype(BF16), b_ref[0], preferred_element_type=F32)

        def tile(i, carry):
            cr, ci = carry
            r0 = pl.multiple_of(i * SUBLANES, SUBLANES)
            xr, xi = _scan_tile(buf[pl.ds(r0, SUBLANES), :n], buf[pl.ds(r0, SUBLANES), n:], cr, ci, tab_ref, False)
            buf[pl.ds(r0, SUBLANES), :n] = xr
            buf[pl.ds(r0, SUBLANES), n:] = xi
            return xr[SUBLANES - 1:, :], xi[SUBLANES - 1:, :]

        cr, ci = lax.fori_loop(0, T // SUBLANES, tile, (car[:, :n], car[:, n:]))
        car[:, :n] = cr
        car[:, n:] = ci
        xs = buf[...]
        xs_ref[...] = xs
        y = jnp.dot(xs.astype(BF16), c_ref[0], preferred_element_type=F32) + d_ref[...] * uv
        y_ref[...] = y
        g_ref[...] = _gelu(y).astype(g_ref.dtype)

    blk = pl.BlockSpec((T, LANES), lambda j, i: (i, j))
    return carrier_call(
        body, name, (nblk, L // T),
        [blk, pl.BlockSpec((1, LANES, ns), lambda j, i: (j, 0, 0)),
         pl.BlockSpec((1, ns, LANES), lambda j, i: (j, 0, 0)),
         pl.BlockSpec((1, 4, SUBLANES, ns), lambda j, i: (j, 0, 0, 0)),
         pl.BlockSpec((1, LANES), lambda j, i: (0, j))],
        [blk, blk, pl.BlockSpec((T, ns), lambda j, i: (i, j))],
        [jax.ShapeDtypeStruct((L, D), F32), jax.ShapeDtypeStruct((L, D), BF16),
         jax.ShapeDtypeStruct((L, nblk * ns), F32)],
        [pltpu.VMEM((T, ns), F32), pltpu.VMEM((1, ns), F32)],
        (u, bmat, cmat, tab, dvec), comm, ("parallel", "arbitrary"))


def s5_bwd(dg, y, u, xs, cmat_t, bmat_t, rtab, dvec, name, comm=None):
    L, D = u.shape
    nblk = D // LANES
    T = _div_tile(L, (ROW_TILE,))
    nch = L // T
    ns = 2 * S5_BLOCK_STATES
    n = S5_BLOCK_STATES
    ntile = T // SUBLANES

    def body(dg_ref, y_ref, u_ref, xs_ref, xp_ref, ct_ref, bt_ref, tab_ref, d_ref,
             du_ref, dc_ref, db_ref, dlam_ref, dd_ref, buf, xbuf, car):
        step = pl.program_id(1)
        first_chunk = step == nch - 1

        @pl.when(step == 0)
        def _():
            car[...] = jnp.zeros_like(car)
            dc_ref[...] = jnp.zeros_like(dc_ref)
            db_ref[...] = jnp.zeros_like(db_ref)
            dlam_ref[...] = jnp.zeros_like(dlam_ref)
            dd_ref[...] = jnp.zeros_like(dd_ref)

        uv = u_ref[...]
        dy = dg_ref[...].astype(F32) * _gelu_grad(y_ref[...])
        dd_ref[...] += jnp.sum(dy * uv, axis=0, keepdims=True)
        dyb = dy.astype(BF16)
        buf[...] = jnp.dot(dyb, ct_ref[0], preferred_element_type=F32)
        xs = xs_ref[...]
        xbuf[pl.ds(SUBLANES, T), :] = xs
        xbuf[pl.ds(0, SUBLANES), :] = jnp.where(first_chunk, 0.0, xp_ref[...])
        row0 = lax.broadcasted_iota(jnp.int32, (SUBLANES, n), 0) == 0

        def tile(ii, carry):
            cr, ci, ar, ai = carry
            r0 = pl.multiple_of((ntile - 1 - ii) * SUBLANES, SUBLANES)
            xr, xi = _scan_tile(buf[pl.ds(r0, SUBLANES), :n], buf[pl.ds(r0, SUBLANES), n:], cr, ci, tab_ref, True)
            buf[pl.ds(r0, SUBLANES), :n] = xr
            buf[pl.ds(r0, SUBLANES), n:] = xi
            r1 = pl.multiple_of(r0 + SUBLANES, SUBLANES)
            pr = jnp.where(row0, xbuf[pl.ds(r0, SUBLANES), :n][SUBLANES - 1:, :],
                           pltpu.roll(xbuf[pl.ds(r1, SUBLANES), :n], 1, 0))
            pi = jnp.where(row0, xbuf[pl.ds(r0, SUBLANES), n:][SUBLANES - 1:, :],
                           pltpu.roll(xbuf[pl.ds(r1, SUBLANES), n:], 1, 0))
            ar = ar + xr * pr + xi * pi
            ai = ai + xi * pr - xr * pi
            return xr[0:1, :], xi[0:1, :], ar, ai

        zero = jnp.zeros((SUBLANES, n), F32)
        cr, ci, ar, ai = lax.fori_loop(0, ntile, tile, (car[:, :n], car[:, n:], zero, zero))
        car[:, :n] = cr
        car[:, n:] = ci
        dlam_ref[0, :, :n] += ar
        dlam_ref[0, :, n:] += ai
        dxa = buf[...]
        dc_ref[0] += lax.dot_general(dyb, xs.astype(BF16), _DIMS["tn"], preferred_element_type=F32)
        dxb = dxa.astype(BF16)
        db_ref[0] += lax.dot_general(uv.astype(BF16), dxb, _DIMS["tn"], preferred_element_type=F32)
        du_ref[...] = jnp.dot(dxb, bt_ref[0], preferred_element_type=F32) + d_ref[...] * dy

    rev = lambda j, i: (nch - 1 - i, j)
    blk = pl.BlockSpec((T, LANES), rev)
    tpb = T // SUBLANES
    acc = pl.BlockSpec((1, LANES, ns), lambda j, i: (j, 0, 0))
    return carrier_call(
        body, name, (nblk, nch),
        [blk, blk, blk, pl.BlockSpec((T, ns), rev),
         pl.BlockSpec((SUBLANES, ns), lambda j, i: (jnp.maximum((nch - 1 - i) * tpb - 1, 0), j)),
         pl.BlockSpec((1, LANES, ns), lambda j, i: (j, 0, 0)),
         pl.BlockSpec((1, ns, LANES), lambda j, i: (j, 0, 0)),
         pl.BlockSpec((1, 4, SUBLANES, ns), lambda j, i: (j, 0, 0, 0)),
         pl.BlockSpec((1, LANES), lambda j, i: (0, j))],
        [blk, acc, acc, pl.BlockSpec((1, SUBLANES, ns), lambda j, i: (j, 0, 0)),
         pl.BlockSpec((1, LANES), lambda j, i: (0, j))],
        [jax.ShapeDtypeStruct((L, D), F32), jax.ShapeDtypeStruct((nblk, LANES, ns), F32),
         jax.ShapeDtypeStruct((nblk, LANES, ns), F32), jax.ShapeDtypeStruct((nblk, SUBLANES, ns), F32),
         jax.ShapeDtypeStruct((1, D), F32)],
        [pltpu.VMEM((T, ns), F32), pltpu.VMEM((T + SUBLANES, ns), F32), pltpu.VMEM((1, ns), F32)],
        (dg, y, u, xs, xs, cmat_t, bmat_t, rtab, dvec), comm, ("parallel", "arbitrary"))


def _s5_discretize(a_re, a_im, log_step, b_re, b_im):
    lam_re = jnp.minimum(a_re, -S5_MIN_DECAY)
    lam_im = a_im
    delta = jnp.exp(log_step)[:, None]
    mag = jnp.exp(lam_re * delta)
    ang = lam_im * delta
    lb_re = mag * jnp.cos(ang)
    lb_im = mag * jnp.sin(ang)
    den = lam_re * lam_re + lam_im * lam_im
    nr = lb_re - 1.0
    ni = lb_im
    coef_re = (nr * lam_re + ni * lam_im) / den
    coef_im = (ni * lam_re - nr * lam_im) / den
    bb_re = coef_re[..., None] * b_re - coef_im[..., None] * b_im
    bb_im = coef_re[..., None] * b_im + coef_im[..., None] * b_re
    return lb_re, lb_im, bb_re, bb_im


def _s5_tables(lb_re, lb_im):
    nblk = lb_re.shape[0] // S5_BLOCK_GROUPS
    lr = lb_re.reshape(nblk, S5_BLOCK_STATES)
    li = lb_im.reshape(nblk, S5_BLOCK_STATES)
    pows = [(jnp.ones_like(lr), jnp.zeros_like(li))]
    for _ in range(SUBLANES):
        pr, pi = pows[-1]
        pows.append((pr * lr - pi * li, pr * li + pi * lr))
    rows = jnp.arange(SUBLANES)[None, :, None]

    def table(conj, reverse):
        sgn = -1.0 if conj else 1.0
        out = []
        for k in (1, 2, 4):
            mask = (rows <= SUBLANES - 1 - k) if reverse else (rows >= k)
            out.append(jnp.concatenate([jnp.where(mask, pows[k][0][:, None, :], 0.0),
                                        jnp.where(mask, sgn * pows[k][1][:, None, :], 0.0)], axis=-1))
        order = range(SUBLANES, 0, -1) if reverse else range(1, SUBLANES + 1)
        cre = jnp.stack([pows[k][0] for k in order], axis=1)
        cim = jnp.stack([sgn * pows[k][1] for k in order], axis=1)
        out.append(jnp.concatenate([cre, cim], axis=-1))
        return jnp.stack(out, axis=1)

    return table(False, False), table(True, True)


def _s5_block_mats(bb_re, bb_im, c_re, c_im):
    G = bb_re.shape[0]
    nblk = G // S5_BLOCK_GROUPS
    eye = jnp.eye(S5_BLOCK_GROUPS, dtype=F32)
    bb = jnp.stack([bb_re, bb_im]).reshape(2, nblk, S5_BLOCK_GROUPS, S5_STATE, S5_GROUP)
    bmat = jnp.einsum("ab,rjaph->jahrbp", eye, bb).reshape(nblk, LANES, 2 * S5_BLOCK_STATES)
    cc = jnp.stack([c_re, -c_im]).reshape(2, nblk, S5_BLOCK_GROUPS, S5_GROUP, S5_STATE)
    cmat = jnp.einsum("ab,rjahp->jrbpah", eye, cc).reshape(nblk, 2 * S5_BLOCK_STATES, LANES)
    return bmat, cmat


def _s5_unblock(dB, dC, dlam):
    nblk = dB.shape[0]
    G = nblk * S5_BLOCK_GROUPS
    d6 = dB.reshape(nblk, S5_BLOCK_GROUPS, S5_GROUP, 2, S5_BLOCK_GROUPS, S5_STATE)
    dbb = jnp.einsum("jahrap->rjaph", d6).reshape(2, G, S5_STATE, S5_GROUP)
    c6 = dC.reshape(nblk, S5_BLOCK_GROUPS, S5_GROUP, 2, S5_BLOCK_GROUPS, S5_STATE)
    dcc = jnp.einsum("jahrap->rjahp", c6).reshape(2, G, S5_GROUP, S5_STATE)
    dl = jnp.sum(dlam, axis=1).reshape(nblk, 2, S5_BLOCK_GROUPS, S5_STATE)
    dl = jnp.transpose(dl, (1, 0, 2, 3)).reshape(2, G, S5_STATE)
    return dbb[0], dbb[1], dcc[0], -dcc[1], dl[0], dl[1]


def loss_and_grad(y, target, name):
    L, D = y.shape
    tr = _div_tile(L, (512, 256, 128))

    def body(y_ref, t_ref, dy_ref, loss_ref):
        err = y_ref[...] - t_ref[...]
        dy_ref[...] = err * (1.0 / D)
        part = 0.5 * jnp.sum(jnp.mean(err * err, axis=-1, keepdims=True), axis=0, keepdims=True)

        @pl.when(pl.program_id(0) == 0)
        def _():
            loss_ref[...] = part

        @pl.when(pl.program_id(0) > 0)
        def _():
            loss_ref[...] += part

    row = pl.BlockSpec((tr, D), lambda i: (i, 0))
    return pl.pallas_call(
        body, name=name, grid=(L // tr,), in_specs=[row, row],
        out_specs=[row, pl.BlockSpec((1, 1), lambda i: (0, 0))],
        out_shape=[jax.ShapeDtypeStruct((L, D), F32), jax.ShapeDtypeStruct((1, 1), F32)],
        compiler_params=_params(("arbitrary",)),
    )(y, target)


def _adam_math(w, g, m, v):
    m = ADAM_B1 * m + (1.0 - ADAM_B1) * g
    v = ADAM_B2 * v + (1.0 - ADAM_B2) * (g * g)
    m_hat = m / (1.0 - ADAM_B1 ** ADAM_STEP)
    v_hat = v / (1.0 - ADAM_B2 ** ADAM_STEP)
    delta = -ADAM_LR * (m_hat / (jnp.sqrt(v_hat) + ADAM_EPS) + ADAM_WD * w)
    return delta, m, v


def _as3d(a):
    return a.reshape((-1,) + a.shape[-2:])


def adamw(w, g, m, v, name):
    shape = w.shape
    w3, g3, m3, v3 = _as3d(w), _as3d(g), _as3d(m), _as3d(v)
    A, R, C = w3.shape
    tr = _row_tile(R, C)

    def body(w_ref, g_ref, m_ref, v_ref, d_ref, mo_ref, vo_ref):
        d, mn, vn = _adam_math(w_ref[...], g_ref[...], m_ref[...], v_ref[...])
        d_ref[...] = d
        mo_ref[...] = mn
        vo_ref[...] = vn

    spec = pl.BlockSpec((1, tr, C), lambda a, i: (a, i, 0))
    outs = pl.pallas_call(
        body, name=name, grid=(A, R // tr), in_specs=[spec] * 4, out_specs=[spec] * 3,
        out_shape=[jax.ShapeDtypeStruct((A, R, C), F32)] * 3,
        compiler_params=_params(("parallel", "parallel")),
    )(w3, g3, m3, v3)
    return [o.reshape(shape) for o in outs]


def _place():
    x, y, c = lax.axis_index("x"), lax.axis_index("y"), lax.axis_index("c")
    other_chips = [(1 - x, y), (x, 1 - y), (1 - x, 1 - y)]
    return x, y, c, other_chips


def _chip_id(chip):
    return 2 * chip[0] + chip[1]


def _my_chip():
    return 2 * lax.axis_index("x") + lax.axis_index("y")


def _remote(src, dst, send_sem, recv_sem, dev):
    return pltpu.make_async_remote_copy(src_ref=src, dst_ref=dst, send_sem=send_sem, recv_sem=recv_sem,
                                        device_id=dev, device_id_type=MESH)


def allgather_small(arrs):
    T = len(arrs)

    def body(*refs):
        ins, outs = refs[:T], refs[T:2 * T]
        send_sems, recv_sems = refs[2 * T:]
        x, y, c, chips = _place()
        me = _chip_id((x, y))
        sends = []
        for t in range(T):
            outs[t][me] = ins[t][...]
            for j, chip in enumerate(chips):
                cp = _remote(ins[t], outs[t].at[me], send_sems.at[3 * t + j], recv_sems.at[3 * t + j], (*chip, c))
                cp.start()
                sends.append(cp)
        for t in range(T):
            for j, chip in enumerate(chips):
                slot = outs[t].at[_chip_id(chip)]
                _remote(slot, slot, send_sems.at[3 * t + j], recv_sems.at[3 * t + j], (*chip, c)).wait_recv()
        for cp in sends:
            cp.wait_send()

    return pl.pallas_call(
        body, name="allgather_small", in_specs=[VMEM_SPEC] * T, out_specs=[VMEM_SPEC] * T,
        out_shape=[jax.ShapeDtypeStruct((4,) + a.shape, a.dtype) for a in arrs],
        scratch_shapes=[pltpu.SemaphoreType.DMA((3 * T,)), pltpu.SemaphoreType.DMA((3 * T,))],
        compiler_params=pltpu.CompilerParams(vmem_limit_bytes=VMEM_LIMIT, has_side_effects=True),
    )(*arrs)


def small_swap_comm(buf):
    def copy(ins, outs, send_sems, recv_sems):
        x, y, c, _ = _place()
        return _remote(ins[0], outs[0], send_sems.at[0], recv_sems.at[0], (x, y, 1 - c))

    return Comm([buf], [jax.ShapeDtypeStruct(buf.shape, F32)], {}, 1,
                lambda *refs: copy(*refs).start(), None, lambda *refs: copy(*refs).wait())


def small_exchange_comm(pair):
    def copies(ins, outs, send_sems, recv_sems):
        x, y, c, chips = _place()
        return [_remote(ins[0], outs[0].at[j], send_sems.at[j], recv_sems.at[j], (*chip, c))
                for j, chip in enumerate(chips)]

    def begin(*refs):
        for cp in copies(*refs):
            cp.start()

    def finish(*refs):
        for cp in copies(*refs):
            cp.wait()

    return Comm([pair], [jax.ShapeDtypeStruct((3,) + pair.shape, F32)], {}, 3, begin, None, finish)


def small_pair_sum(mine, theirs, name):
    R, C = mine.shape
    tr = _row_tile(R, C)

    def body(a_ref, b_ref, o_ref):
        o_ref[...] = a_ref[...] + b_ref[...]

    spec = pl.BlockSpec((tr, C), lambda i: (i, 0))
    return pl.pallas_call(body, name=name, grid=(R // tr,), in_specs=[spec, spec], out_specs=spec,
                          out_shape=jax.ShapeDtypeStruct((R, C), F32), compiler_params=_params(("parallel",)))(mine, theirs)


def small_chip_sum(pair, got, name):
    R, C = pair.shape
    tr = _row_tile(R, C)

    def body(p_ref, g_ref, o_ref):
        me = _my_chip()
        terms = []
        for chip in range(4):
            d = jnp.bitwise_xor(me, chip)
            terms.append(jnp.where(d == 0, p_ref[...],
                                   jnp.where(d == 2, g_ref[0], jnp.where(d == 1, g_ref[1], g_ref[2]))))
        o_ref[...] = ((terms[0] + terms[1]) + terms[2]) + terms[3]

    spec = pl.BlockSpec((tr, C), lambda i: (i, 0))
    return pl.pallas_call(body, name=name, grid=(R // tr,),
                          in_specs=[spec, pl.BlockSpec((3, tr, C), lambda i: (0, i, 0))], out_specs=spec,
                          out_shape=jax.ShapeDtypeStruct((R, C), F32), compiler_params=_params(("parallel",)))(pair, got)


def _half_rows(ref, layout, shard, half):
    if layout == "S":
        hr = ref.shape[1] // 2
        return ref.at[shard, pl.ds(pl.multiple_of(half * hr, 16), hr), :]
    hr, C = ref.shape[0] // 2, ref.shape[1] // 4
    return ref.at[pl.ds(pl.multiple_of(half * hr, 16), hr), pl.ds(pl.multiple_of(shard * C, LANES), C)]


def _half_rows_all(ref, layout, half):
    if layout == "S":
        hr = ref.shape[1] // 2
        return ref.at[:, pl.ds(pl.multiple_of(half * hr, 16), hr), :]
    hr = ref.shape[0] // 2
    return ref.at[pl.ds(pl.multiple_of(half * hr, 16), hr), :]


def _shard_of_half(ref, layout, shard):
    if layout == "S":
        return ref.at[shard]
    C = ref.shape[1] // 4
    return ref.at[:, pl.ds(pl.multiple_of(shard * C, LANES), C)]


def _own_block_spec(layout, tr, C):
    if layout == "S":
        return pl.BlockSpec((None, tr, C), lambda i: (_my_chip(), i, 0))
    return pl.BlockSpec((tr, C), lambda i: (i, _my_chip()))


def cast_into_gathered(shards, layer, layout, name):
    _, R, C = shards.shape
    tr = _row_tile(R, C)

    def body(a_ref, o_ref):
        o_ref[...] = a_ref[...].astype(BF16)

    return pl.pallas_call(
        body, name=name, grid=(R // tr,),
        in_specs=[pl.BlockSpec((None, tr, C), lambda i: (layer, i, 0))],
        out_specs=_own_block_spec(layout, tr, C),
        out_shape=jax.ShapeDtypeStruct((4, R, C) if layout == "S" else (R, 4 * C), BF16),
        compiler_params=_params(("parallel",)),
    )(shards)


def gather_comm(bufs, layouts):
    T = len(bufs)

    def begin(_, outs, send_sems, recv_sems):
        x, y, c, chips = _place()
        for t in range(T):
            mine = _half_rows(outs[t], layouts[t], _chip_id((x, y)), c)
            for j, chip in enumerate(chips):
                _remote(mine, mine, send_sems.at[6 * t + j], recv_sems.at[6 * t + j], (*chip, c)).start()

    def middle(_, outs, send_sems, recv_sems):
        x, y, c, chips = _place()
        for t in range(T):
            for j, chip in enumerate(chips):
                piece = _half_rows(outs[t], layouts[t], _chip_id(chip), c)
                _remote(piece, piece, send_sems.at[6 * t + j], recv_sems.at[6 * t + j], (*chip, c)).wait_recv()
                _remote(piece, piece, send_sems.at[6 * t + 3 + j], recv_sems.at[6 * t + 3 + j], (x, y, 1 - c)).start()

    def finish(_, outs, send_sems, recv_sems):
        x, y, c, chips = _place()
        for t in range(T):
            mine = _half_rows(outs[t], layouts[t], _chip_id((x, y)), c)
            for j, chip in enumerate(chips):
                theirs = _half_rows(outs[t], layouts[t], _chip_id(chip), 1 - c)
                _remote(theirs, theirs, send_sems.at[6 * t + 3 + j], recv_sems.at[6 * t + 3 + j],
                        (x, y, 1 - c)).wait_recv()
                _remote(mine, mine, send_sems.at[6 * t + j], recv_sems.at[6 * t + j], (*chip, c)).wait_send()
                piece = _half_rows(outs[t], layouts[t], _chip_id(chip), c)
                _remote(piece, piece, send_sems.at[6 * t + 3 + j], recv_sems.at[6 * t + 3 + j],
                        (x, y, 1 - c)).wait_send()

    return Comm(bufs, [jax.ShapeDtypeStruct(b.shape, b.dtype) for b in bufs], {t: t for t in range(T)}, 6 * T,
                begin, middle, finish, middle_frac=0.75)


def swap_comm(grads, layouts):
    T = len(grads)

    def out_shape(g, layout):
        return (4, g.shape[1] // 2, g.shape[2]) if layout == "S" else (g.shape[0] // 2, g.shape[1])

    def copies(ins, outs, send_sems, recv_sems):
        x, y, c, _ = _place()
        return [_remote(_half_rows_all(ins[t], layouts[t], 1 - c), outs[t], send_sems.at[t], recv_sems.at[t],
                        (x, y, 1 - c)) for t in range(T)]

    def begin(*refs):
        for cp in copies(*refs):
            cp.start()

    def finish(*refs):
        for cp in copies(*refs):
            cp.wait()

    return Comm(grads, [jax.ShapeDtypeStruct(out_shape(g, k), F32) for g, k in zip(grads, layouts)], {}, T,
                begin, None, finish)


def pair_sum(grad, recv, layout, name):
    if layout == "S":
        _, hr, C = recv.shape
        tr = _row_tile(hr, C)
        nb = hr // tr
        grid = (4, nb)
        g_spec = pl.BlockSpec((None, tr, C), lambda a, i: (a, lax.axis_index("c") * nb + i, 0))
        spec = pl.BlockSpec((None, tr, C), lambda a, i: (a, i, 0))
    else:
        hr, C = recv.shape
        tr = _row_tile(hr, C)
        nb = hr // tr
        grid = (nb,)
        g_spec = pl.BlockSpec((tr, C), lambda i: (lax.axis_index("c") * nb + i, 0))
        spec = pl.BlockSpec((tr, C), lambda i: (i, 0))

    def body(g_ref, r_ref, f_ref, b_ref):
        s = g_ref[...] + r_ref[...]
        f_ref[...] = s
        b_ref[...] = s.astype(BF16)

    return pl.pallas_call(
        body, name=name, grid=grid, in_specs=[g_spec, spec], out_specs=[spec, spec],
        out_shape=[jax.ShapeDtypeStruct(recv.shape, F32), jax.ShapeDtypeStruct(recv.shape, BF16)],
        compiler_params=_params(("parallel",) * len(grid)),
    )(grad, recv)


def exchange_comm(pair_bf16, layouts):
    T = len(pair_bf16)

    def out_shape(p, layout):
        return (3,) + ((p.shape[1], p.shape[2]) if layout == "S" else (p.shape[0], p.shape[1] // 4))

    def copies(ins, outs, send_sems, recv_sems):
        x, y, c, chips = _place()
        return [_remote(_shard_of_half(ins[t], layouts[t], _chip_id(chip)), outs[t].at[j],
                        send_sems.at[3 * t + j], recv_sems.at[3 * t + j], (*chip, c))
                for t in range(T) for j, chip in enumerate(chips)]

    def begin(*refs):
        for cp in copies(*refs):
            cp.start()

    def finish(*refs):
        for cp in copies(*refs):
            cp.wait()

    return Comm(pair_bf16, [jax.ShapeDtypeStruct(out_shape(p, k), BF16) for p, k in zip(pair_bf16, layouts)], {},
                3 * T, begin, None, finish)


def reduce_total(pair_f32, got, layout, layer, n_layers, previous, name):
    _, hr, C = got.shape
    tr = _row_tile(hr, C)
    nb = hr // tr

    def body(*refs):
        p_ref, g_ref, t_ref = refs[0], refs[1], refs[-1]
        t_ref[...] = ((p_ref[...] + g_ref[0].astype(F32)) + g_ref[1].astype(F32)) + g_ref[2].astype(F32)

    args = [pair_f32, got] + ([previous] if previous is not None else [])
    return pl.pallas_call(
        body, name=name, grid=(nb,),
        in_specs=[_own_block_spec(layout, tr, C), pl.BlockSpec((3, tr, C), lambda i: (0, i, 0))]
        + ([ANY] if previous is not None else []),
        out_specs=pl.BlockSpec((None, tr, C), lambda i: (layer, lax.axis_index("c") * nb + i, 0)),
        out_shape=jax.ShapeDtypeStruct((n_layers, 2 * hr, C), F32),
        input_output_aliases={2: 0} if previous is not None else {},
        compiler_params=_params(("parallel",)),
    )(*args)


def share_comm(reduced):
    T = len(reduced)

    def halves(outs, half):
        return [o.at[:, pl.ds(pl.multiple_of(half * (o.shape[1] // 2), 8), o.shape[1] // 2), :] for o in outs]

    def begin(_, outs, send_sems, recv_sems):
        x, y, c, _p = _place()
        for t, mine in enumerate(halves(outs, c)):
            _remote(mine, mine, send_sems.at[t], recv_sems.at[t], (x, y, 1 - c)).start()

    def finish(_, outs, send_sems, recv_sems):
        x, y, c, _p = _place()
        for t, (mine, theirs) in enumerate(zip(halves(outs, c), halves(outs, 1 - c))):
            _remote(mine, mine, send_sems.at[t], recv_sems.at[t], (x, y, 1 - c)).wait_send()
            _remote(theirs, theirs, send_sems.at[t], recv_sems.at[t], (x, y, 1 - c)).wait_recv()

    return Comm(reduced, [jax.ShapeDtypeStruct(r.shape, r.dtype) for r in reduced], {t: t for t in range(T)}, T,
                begin, None, finish)


def _round_up(n, m):
    return (n + m - 1) // m * m


def _heads_col(a16):
    L = a16.shape[0]
    return jnp.transpose(a16.reshape(L, HEADS // 2, 2), (1, 0, 2))


def _heads_row(a16):
    L = a16.shape[0]
    return jnp.transpose(a16.reshape(L, HEADS // 2, 2), (1, 2, 0))


def local_step(x, target, meta, norm_g, b_f, conv_w, conv_b, s5, s5_d, stager):
    S, D = x.shape
    depth = norm_g.shape[0]
    n_even, n_odd = b_f.shape[0], s5_d.shape[0]
    L = N_META + S
    Lp = _round_up(L, ROW_TILE)
    h = jnp.concatenate([meta, x, jnp.zeros((Lp - L, D), F32)], axis=0)
    conv_c = conv_w.shape[2]
    fg_block = 3 * conv_c // LANES
    saved = []

    def riding(tag, fn, *args):
        comm = stager.ride(tag)
        if comm is None and fn is matmul:
            return fn(*args, name=tag)
        outs, arrived = fn(*args, name=tag, comm=comm)
        stager.arrived(tag, arrived)
        return outs[0] if fn is matmul else outs

    for i in range(depth):
        g = norm_g[i]
        j = i // 2
        tag = "l%d_" % i
        w = stager.weights(i)
        st = {"h0": h, "w": w}
        if i % 2 == 0:
            u = rmsnorm_fwd(h, g[0:1], BF16, tag + "norm0")
            qkv = matmul([(u, w["w_qkv"])], "nn", BF16, tag + "qkv")
            rest = matmul([(u, w["w_rest"])], "nn", F32, tag + "rest")
            cgate = gate_fwd(rest, fg_block, b_f[j], tag + "gate")
            c16 = cgate[:, :HEADS]
            attn, lse = riding(tag + "attn", attn_fwd, qkv, _heads_col(c16), _heads_row(c16))
            convo = conv_fwd(rest, 0, conv_w[j], conv_b[j], tag + "conv")
            cat = jnp.concatenate([attn, convo], axis=1)
            m = matmul([(cat, w["w_o"])], "nn", F32, tag + "wo")
            st.update(u=u, qkv=qkv, rest=rest, c16=c16, lse=lse, cat=cat)
        else:
            p = s5[j]
            u = rmsnorm_fwd(h, g[0:1], F32, tag + "norm0")
            y, gact, xs = riding(tag + "s5", s5_fwd, u, p["bmat"], p["cmat"], p["tab"], s5_d[j])
            o1, o2, m = riding(tag + "glu", dual_matmul_act, gact, w["w_glu1"], w["w_glu2"], "glu", F32)
            st.update(u=u, y=y, gact=gact, xs=xs, o1=o1, o2=o2)
        h1 = rmsnorm_fwd(m, g[1:2], F32, tag + "norm1", residual=h)
        u2 = rmsnorm_fwd(h1, g[2:3], BF16, tag + "norm2")
        a, b, hact = riding(tag + "ffn_in", dual_matmul_act, u2, w["w_gate"], w["w_up"], "swiglu", BF16)
        f = riding(tag + "ffn_out", matmul, [(hact, w["w_down"])], "nn", F32)
        h = rmsnorm_fwd(f, g[3:4], F32, tag + "norm3", residual=h1)
        st.update(m=m, h1=h1, u2=u2, a=a, b=b, hact=hact, f=f)
        saved.append(st)

    dy, loss = loss_and_grad(h[N_META:L], target, "loss")
    dh = jnp.concatenate([jnp.zeros((N_META, D), F32), dy, jnp.zeros((Lp - L, D), F32)], axis=0)

    grads = {k: [None] * n_even for k in ("b_f", "conv_w", "conv_b")}
    grads.update({k: [None] * n_odd for k in ("s5_d", "s5_dB", "s5_dC", "s5_dlam")})
    grads["norm_g"] = [None] * depth

    for i in reversed(range(depth)):
        g = norm_g[i]
        j = i // 2
        tag = "l%d_b_" % i
        st = saved[i]
        w = st["w"]
        wg = {}
        df, dg3 = rmsnorm_bwd(st["f"], g[3:4], dh, BF16, tag + "norm3")
        wg["w_down"] = matmul([(st["hact"], df)], "tn", F32, tag + "dw_down")
        da, db = riding(tag + "ffn_act", ffn_bwd_act, df, w["w_down"], st["a"], st["b"])
        u2t = st["u2"].T
        wg["w_gate"] = matmul([(u2t, da)], "nn", F32, tag + "dw_gate")
        wg["w_up"] = matmul([(u2t, db)], "nn", F32, tag + "dw_up")
        stager.put_grads(i, wg)
        wg = {}
        du2 = riding(tag + "du2", matmul, [(da, w["w_gate"])], "nt", F32)
        du2 = matmul([(db, w["w_up"])], "nt", F32, tag + "du2_up", add=du2)
        dh1, dg2 = rmsnorm_bwd(st["h1"], g[2:3], du2, F32, tag + "norm2", add=dh)
        if i % 2 == 0:
            dm, dg1 = rmsnorm_bwd(st["m"], g[1:2], dh1, BF16, tag + "norm1")
            wg["w_o"] = matmul([(st["cat"], dm)], "tn", F32, tag + "dw_o")
            stager.put_grads(i, wg)
            dcat = riding(tag + "dcat", matmul, [(dm, w["w_o"])], "nt", F32)
            delta = attn_delta(dcat, st["cat"], tag + "delta")
            c16 = st["c16"]
            lse16 = jnp.transpose(st["lse"], (1, 0, 2)).reshape(Lp, HEADS)
            dq, dk, dv, dcq, dck = riding(tag + "attn", attn_bwd, st["qkv"], dcat, _heads_row(lse16),
                                          _heads_row(delta[:, :HEADS]), _heads_row(c16), _heads_col(c16))
            dc16 = (jnp.transpose(dcq, (2, 0, 1)).reshape(Lp, HEADS)
                    + jnp.transpose(dck, (1, 0, 2)).reshape(Lp, HEADS))
            dc = jnp.pad(dc16, ((0, 0), (0, LANES - HEADS)))
            dfg, dbf = gate_bwd(st["rest"], fg_block, b_f[j], dc, tag + "gate")
            dgb, dgc, dxc, dcw, dcb = conv_bwd(st["rest"], 0, conv_w[j], conv_b[j], dcat, ATTN_W // LANES,
                                               tag + "conv")
            dqkv = jnp.concatenate([dq, dk, dv], axis=1)
            drest = jnp.concatenate([dgb, dgc, dxc, dfg.astype(BF16)], axis=1)
            wg["w_qkv"] = matmul([(st["u"], dqkv)], "tn", F32, tag + "dw_qkv")
            wg["w_rest"] = matmul([(st["u"], drest)], "tn", F32, tag + "dw_rest")
            stager.put_grads(i, wg)
            du = riding(tag + "du_qkv", matmul, [(dqkv, w["w_qkv"])], "nt", F32)
            du_b = riding(tag + "du_rest", matmul, [(drest, w["w_rest"])], "nt", F32)
            grads["b_f"][j], grads["conv_w"][j], grads["conv_b"][j] = dbf, dcw, dcb
        else:
            p = s5[j]
            dmix, dg1 = rmsnorm_bwd(st["m"], g[1:2], dh1, F32, tag + "norm1")
            do1, do2 = glu_bwd_act(dmix, st["o1"], st["o2"], tag + "glu_act")
            wg["w_glu1"] = matmul([(st["gact"], do1)], "tn", F32, tag + "dw_glu1")
            wg["w_glu2"] = matmul([(st["gact"], do2)], "tn", F32, tag + "dw_glu2")
            dgact = matmul([(do1, w["w_glu1"]), (do2, w["w_glu2"])], "nt", F32, tag + "dgact")
            du, dC, dB, dlam, dd = riding(tag + "s5", s5_bwd, dgact, st["y"], st["u"], st["xs"], p["cmat_t"],
                                          p["bmat_t"], p["rtab"], s5_d[j])
            du_b = None
            grads["s5_dB"][j], grads["s5_dC"][j], grads["s5_dlam"][j], grads["s5_d"][j] = dB, dC, dlam, dd
        dh, dg0 = rmsnorm_bwd(st["h0"], g[0:1], du, F32, tag + "norm0", add=dh1, dy2=du_b)
        grads["norm_g"][i] = jnp.concatenate([dg0, dg1, dg2, dg3], axis=0)
        stager.put_grads(i, wg)

    grads["meta"] = dh[:N_META]
    return loss, dh[N_META:L], grads


def _packed_rows(shape):
    return _round_up(_round_up(math.prod(shape), LANES) // LANES, SUBLANES)


def _pack(arrs):
    rows = []
    for a in arrs:
        flat = a.reshape(-1).astype(F32)
        r = _packed_rows(a.shape)
        rows.append(jnp.pad(flat, (0, r * LANES - flat.shape[0])).reshape(r, LANES))
    return jnp.concatenate(rows, axis=0)


def _unpack(buf, shapes):
    buf = buf.reshape(-1, LANES)
    out, off = [], 0
    for s in shapes:
        r = _packed_rows(s)
        out.append(buf[off:off + r].reshape(-1)[:math.prod(s)].reshape(s))
        off += r
    return out


class _LayerWeights:
    def __init__(self, stager, layer):
        self.stager, self.layer = stager, layer

    def __getitem__(self, name):
        return self.stager.weight(self.layer, name)


class MeshStager:
    LAYOUT = {"ab_w_in": "S", "ab_w_o": "S", "s5_w_glu1": "S", "s5_w_glu2": "S",
              "ffn_w_gate": "C", "ffn_w_up": "C", "ffn_w_down": "S"}
    EVEN = ("ab_w_in", "ab_w_o", "ffn_w_gate", "ffn_w_up", "ffn_w_down")
    ODD = ("s5_w_glu1", "s5_w_glu2", "ffn_w_gate", "ffn_w_up", "ffn_w_down")

    def __init__(self, shards):
        self.shards = shards
        self.depth = depth = shards["ffn_w_gate"].shape[0]
        self.bufs = {}
        for i in range(depth):
            for k in self.keys(i):
                self.bufs[k, i] = cast_into_gathered(shards[k], self.index(k, i), self.LAYOUT[k],
                                                     "cast_%s_%d" % (k, i))
        self.grads, self.pairs, self.reduced, self.ready = {}, {}, {}, {}
        first = [("ab_w_in", 0)]
        plan = self.gather_plan = {"l0_attn": [it for it in self.stage(0) if it not in first]}
        for o in range(1, depth, 2):
            e = o - 1
            plan.setdefault("l%d_attn" % e, [])
            plan["l%d_ffn_in" % e] = [("s5_w_glu1", o), ("ffn_w_up", o)]
            plan["l%d_ffn_out" % e] = [("ffn_w_gate", o)]
            plan["l%d_s5" % o] = ([("ffn_w_down", o), ("s5_w_glu2", o)]
                                  + (self.mix(o + 1) if o + 1 < depth else []))
            if o + 1 < depth:
                plan["l%d_ffn_in" % o] = [("ffn_w_gate", o + 1)]
                plan["l%d_attn" % (o + 1)] = [("ffn_w_up", o + 1), ("ffn_w_down", o + 1)]
        self.swap_plan, self.exchange_plan = {}, {}
        for i in range(depth):
            above = self.mix(i + 1) if i + 1 < depth else []
            if above:
                self.swap_plan["l%d_b_ffn_act" % i] = above
            self.swap_plan["l%d_b_du2" % i] = self.ffn(i)
            self.exchange_plan["l%d_b_%s" % (i, "attn" if i % 2 == 0 else "s5")] = above + self.ffn(i)
        self.swap_plan["l0_b_dcat"] = [("ab_w_o", 0)]
        self.exchange_plan["l0_b_attn"].append(("ab_w_o", 0))
        self.swap_plan["l0_b_du_qkv"] = [("ab_w_in", 0)]
        self.exchange_plan["l0_b_du_rest"] = [("ab_w_in", 0)]
        self._store(first, comm_call("gather_first", self._gather(first, 0.5)))

    def keys(self, i):
        return self.EVEN if i % 2 == 0 else self.ODD

    def stage(self, i):
        return [(k, i) for k in self.keys(i)]

    def mix(self, i):
        return [(k, i) for k in self.keys(i) if not k.startswith("ffn")]

    @staticmethod
    def ffn(i):
        return [("ffn_w_gate", i), ("ffn_w_up", i), ("ffn_w_down", i)]

    @staticmethod
    def index(key, i):
        return i if key.startswith("ffn") else i // 2

    def _layouts(self, items):
        return [self.LAYOUT[k] for k, _ in items]

    def _gather(self, items, middle_frac):
        comm = gather_comm([self.bufs[it] for it in items], self._layouts(items))
        comm.middle_frac = middle_frac
        return comm

    def _store(self, items, bufs):
        for it, b in zip(items, bufs):
            self.bufs[it] = b

    def ride(self, tag):
        if tag in self.gather_plan:
            return self._gather(self.gather_plan[tag], 0.85 if tag == "l0_attn" else 0.7)
        if tag in self.swap_plan:
            items = self.swap_plan[tag]
            return swap_comm([self.grads[it] for it in items], self._layouts(items))
        if tag in self.exchange_plan:
            items = self.exchange_plan[tag]
            return exchange_comm([self.pairs[it][1] for it in items], self._layouts(items))
        return None

    def arrived(self, tag, outs):
        if tag in self.gather_plan:
            self._store(self.gather_plan[tag], outs)
        elif tag in self.swap_plan:
            self._pair_sums(self.swap_plan[tag], outs)
        elif tag in self.exchange_plan:
            self._totals(self.exchange_plan[tag], outs)

    def _pair_sums(self, items, received):
        for it, r in zip(items, received):
            self.pairs[it] = pair_sum(self.grads[it], r, self.LAYOUT[it[0]], "pair_sum_%s_%d" % it)

    def _totals(self, items, got):
        for it, g in zip(items, got):
            k, i = it
            self.reduced[k] = reduce_total(self.pairs[it][0], g, self.LAYOUT[k], self.index(k, i),
                                           self.shards[k].shape[0], self.reduced.get(k), "reduce_total_%s_%d" % it)

    def weights(self, i):
        return _LayerWeights(self, i)

    def weight(self, i, name):
        if (name, i) not in self.ready:
            if name in ("w_qkv", "w_rest"):
                b = self.bufs["ab_w_in", i]
                w_in = jnp.transpose(b, (1, 0, 2)).reshape(b.shape[1], 4 * b.shape[2])
                self.ready["w_qkv", i], self.ready["w_rest", i] = split_w_in(w_in)
            else:
                k = {"w_o": "ab_w_o", "w_glu1": "s5_w_glu1", "w_glu2": "s5_w_glu2"}.get(name, "ffn_" + name)
                b = self.bufs[k, i]
                self.ready[name, i] = b.reshape(4 * b.shape[1], b.shape[2]) if self.LAYOUT[k] == "S" else b
        return self.ready[name, i]

    def put_grads(self, i, wg):
        for k in self.keys(i):
            _, R, C = self.shards[k].shape
            name = {"ab_w_in": "w_qkv", "ab_w_o": "w_o", "s5_w_glu1": "w_glu1", "s5_w_glu2": "w_glu2"}.get(k, k[4:])
            if name not in wg:
                continue
            if k == "ab_w_in":
                dw = merge_dw_in(wg["w_qkv"], wg["w_rest"])
                self.grads[k, i] = jnp.transpose(dw.reshape(R, 4, C), (1, 0, 2))
            else:
                self.grads[k, i] = wg[name].reshape(4, R, C) if self.LAYOUT[k] == "S" else wg[name]

    def finish(self, beside):
        names = list(self.LAYOUT)
        outs = comm_call("share_reduced", merge_comms(share_comm([self.reduced[k] for k in names]), beside))
        return dict(zip(names, outs)), outs[len(names):]


def split_w_in(w_in):
    fg0 = 3 * ATTN_W
    w_rest = jnp.concatenate([w_in[:, fg0 + HEADS:], w_in[:, fg0:fg0 + HEADS],
                              jnp.zeros((w_in.shape[0], LANES - HEADS), w_in.dtype)], axis=1)
    return w_in[:, :fg0], w_rest


def merge_dw_in(dw_qkv, dw_rest):
    nqc = dw_rest.shape[1] - LANES
    return jnp.concatenate([dw_qkv, dw_rest[:, nqc:nqc + HEADS], dw_rest[:, :nqc]], axis=1)


def device_step(x, target, P, stager):
    D = x.shape[-1]
    n_even, n_odd = P["ab_b_f"].shape[0], P["s5_d"].shape[0]
    conv_c = P["ab_conv_b"].shape[1]
    b_f_pad = jnp.pad(P["ab_b_f"], ((0, 0), (0, LANES - HEADS))).reshape(n_even, 1, LANES)

    s5, s5_vjps = [], []
    for j in range(n_odd):
        disc, vjp = jax.vjp(_s5_discretize, P["s5_a_re"][j], P["s5_a_im"][j], P["s5_log_step"][j],
                            P["s5_b_re"][j], P["s5_b_im"][j])
        lb_re, lb_im, bb_re, bb_im = disc
        tab, rtab = _s5_tables(lb_re, lb_im)
        bmat, cmat = _s5_block_mats(bb_re, bb_im, P["s5_c_re"][j], P["s5_c_im"][j])
        s5.append(dict(tab=tab, rtab=rtab, bmat=bmat.astype(BF16), cmat=cmat.astype(BF16),
                       bmat_t=jnp.transpose(bmat, (0, 2, 1)).astype(BF16),
                       cmat_t=jnp.transpose(cmat, (0, 2, 1)).astype(BF16)))
        s5_vjps.append(vjp)

    loss, grad_x, G = local_step(
        x, target, P["meta_tokens"], P["norm_g"], b_f_pad, P["ab_conv_w"],
        P["ab_conv_b"].reshape(n_even, 1, conv_c), s5, P["s5_d"].reshape(n_odd, 1, D), stager)

    out = {
        "meta_tokens": G["meta"],
        "norm_g": jnp.stack(G["norm_g"]),
        "ab_b_f": jnp.stack([b[0, :HEADS] for b in G["b_f"]]),
        "ab_conv_w": jnp.stack(G["conv_w"]),
        "ab_conv_b": jnp.stack([b[0] for b in G["conv_b"]]),
        "s5_d": jnp.stack([d[0] for d in G["s5_d"]]),
    }
    s5g = {k: [] for k in ("s5_a_re", "s5_a_im", "s5_log_step", "s5_b_re", "s5_b_im", "s5_c_re", "s5_c_im")}
    for j in range(n_odd):
        dbb_re, dbb_im, dc_re, dc_im, dl_re, dl_im = _s5_unblock(G["s5_dB"][j], G["s5_dC"][j], G["s5_dlam"][j])
        da_re, da_im, dls, db_re, db_im = s5_vjps[j]((dl_re, dl_im, dbb_re, dbb_im))
        for k, val in zip(s5g, (da_re, da_im, dls, db_re, db_im, dc_re, dc_im)):
            s5g[k].append(val)
    out.update({k: jnp.stack(v) for k, v in s5g.items()})
    return loss, grad_x, out


def kernel(x, meta_tokens, norm_g, ab_w_in, ab_b_f, ab_conv_w, ab_conv_b, ab_w_o, s5_a_re, s5_a_im, s5_log_step, s5_b_re, s5_b_im, s5_c_re, s5_c_im, s5_d, s5_w_glu1, s5_w_glu2, ffn_w_gate, ffn_w_up, ffn_w_down, loss_target, m_meta_tokens, m_norm_g, m_ab_w_in, m_ab_b_f, m_ab_conv_w, m_ab_conv_b, m_ab_w_o, m_s5_a_re, m_s5_a_im, m_s5_log_step, m_s5_b_re, m_s5_b_im, m_s5_c_re, m_s5_c_im, m_s5_d, m_s5_w_glu1, m_s5_w_glu2, m_ffn_w_gate, m_ffn_w_up, m_ffn_w_down, v_meta_tokens, v_norm_g, v_ab_w_in, v_ab_b_f, v_ab_conv_w, v_ab_conv_b, v_ab_w_o, v_s5_a_re, v_s5_a_im, v_s5_log_step, v_s5_b_re, v_s5_b_im, v_s5_c_re, v_s5_c_im, v_s5_d, v_s5_w_glu1, v_s5_w_glu2, v_ffn_w_gate, v_ffn_w_up, v_ffn_w_down):
    names = ["meta_tokens", "norm_g", "ab_w_in", "ab_b_f", "ab_conv_w", "ab_conv_b", "ab_w_o", "s5_a_re", "s5_a_im",
             "s5_log_step", "s5_b_re", "s5_b_im", "s5_c_re", "s5_c_im", "s5_d", "s5_w_glu1", "s5_w_glu2",
             "ffn_w_gate", "ffn_w_up", "ffn_w_down"]
    W = dict(zip(names, [meta_tokens, norm_g, ab_w_in, ab_b_f, ab_conv_w, ab_conv_b, ab_w_o, s5_a_re, s5_a_im,
                         s5_log_step, s5_b_re, s5_b_im, s5_c_re, s5_c_im, s5_d, s5_w_glu1, s5_w_glu2,
                         ffn_w_gate, ffn_w_up, ffn_w_down]))
    Mo = dict(zip(names, [m_meta_tokens, m_norm_g, m_ab_w_in, m_ab_b_f, m_ab_conv_w, m_ab_conv_b, m_ab_w_o, m_s5_a_re,
                          m_s5_a_im, m_s5_log_step, m_s5_b_re, m_s5_b_im, m_s5_c_re, m_s5_c_im, m_s5_d, m_s5_w_glu1,
                          m_s5_w_glu2, m_ffn_w_gate, m_ffn_w_up, m_ffn_w_down]))
    Vo = dict(zip(names, [v_meta_tokens, v_norm_g, v_ab_w_in, v_ab_b_f, v_ab_conv_w, v_ab_conv_b, v_ab_w_o, v_s5_a_re,
                          v_s5_a_im, v_s5_log_step, v_s5_b_re, v_s5_b_im, v_s5_c_re, v_s5_c_im, v_s5_d, v_s5_w_glu1,
                          v_s5_w_glu2, v_ffn_w_gate, v_ffn_w_up, v_ffn_w_down]))
    D = x.shape[-1]
    n_even, n_odd, depth = ab_w_in.shape[0], s5_w_glu1.shape[0], ffn_w_gate.shape[0]
    chip = 2 * lax.axis_index("x") + lax.axis_index("y")

    big = list(MeshStager.LAYOUT)
    stager = MeshStager({k: W[k] for k in big})
    g_meta, g_norm, g_convw, g_s5d = allgather_small([meta_tokens, norm_g, ab_conv_w, s5_d])
    full = {k: W[k] for k in names if k not in big}
    full["meta_tokens"] = jnp.transpose(g_meta, (1, 0, 2)).reshape(N_META, D)
    full["norm_g"] = jnp.transpose(g_norm, (1, 2, 0, 3)).reshape(depth, 4, D)
    full["ab_conv_w"] = jnp.transpose(g_convw, (1, 2, 0, 3)).reshape(n_even, CONV_K, -1)
    full["s5_d"] = jnp.transpose(g_s5d, (1, 0, 2)).reshape(n_odd, D)

    loss, grad_x, G = device_step(x[0], loss_target[0], full, stager)
    small_w = [k for k in names if k not in big]
    small_names = ["loss"] + small_w
    G["loss"] = loss
    packed = _pack([G[k] for k in small_names])
    pair = small_pair_sum(packed, comm_call("small_swap", small_swap_comm(packed))[0], "small_pair_sum")
    reduced, (got,) = stager.finish(small_exchange_comm(pair))
    total = small_chip_sum(pair, got, "small_chip_sum")

    grad, delta, new_m, new_v = {}, {}, {}, {}
    for k in big:
        delta[k], new_m[k], new_v[k] = adamw(W[k], reduced[k], Mo[k], Vo[k], "adamw_" + k)
        grad[k] = reduced[k]
    summed = dict(zip(small_names, _unpack(total, [G[k].shape for k in small_names])))
    loss_out = summed["loss"].reshape(())
    for k in ("meta_tokens", "norm_g", "ab_conv_w", "s5_d"):
        n_last = W[k].shape[-1]
        summed[k] = lax.dynamic_slice_in_dim(summed[k], chip * n_last, n_last, axis=summed[k].ndim - 1)
    shapes = [W[k].shape for k in small_w]
    d_s, m_s, v_s = adamw(_pack([W[k] for k in small_w])[None], _pack([summed[k] for k in small_w])[None],
                          _pack([Mo[k] for k in small_w])[None], _pack([Vo[k] for k in small_w])[None], "adamw_small")
    delta.update(zip(small_w, _unpack(d_s, shapes)))
    new_m.update(zip(small_w, _unpack(m_s, shapes)))
    new_v.update(zip(small_w, _unpack(v_s, shapes)))
    grad.update({k: summed[k] for k in small_w})

    return (loss_out, grad_x[None], *[grad[k] for k in names], *[delta[k] for k in names],
            *[new_m[k] for k in names], *[new_v[k] for k in names])
```

```python
import functools
import math

import jax
import jax.numpy as jnp
from jax import lax
from jax.experimental import pallas as pl
from jax.experimental.pallas import tpu as pltpu

F32 = jnp.float32
BF16 = jnp.bfloat16

N_META = 16
HEADS = 16
HEAD_DIM = 64
ATTN_W = HEADS * HEAD_DIM
CONV_K = 3
S5_GROUP = 16
S5_STATE = 64
S5_MIN_DECAY = 1e-4
NORM_EPS = 1e-6
ADAM_LR = 0.001
ADAM_B1 = 0.9
ADAM_B2 = 0.999
ADAM_EPS = 1e-08
ADAM_WD = 0.01
ADAM_STEP = 10

LANES = 128
SUBLANES = 8
VMEM_LIMIT = 56 * 1024 * 1024
VMEM_TILE_BUDGET = 34 * 1024 * 1024
ROW_TILE = 384
ATTN_ROWS = 384
S5_BLOCK_GROUPS = LANES // S5_GROUP
S5_BLOCK_STATES = S5_BLOCK_GROUPS * S5_STATE
NEG_BIG = -1e30

MESH = pl.DeviceIdType.MESH
ANY = pl.BlockSpec(memory_space=pl.ANY)
VMEM_SPEC = pl.BlockSpec(memory_space=pltpu.VMEM)


def _params(sem=None):
    return pltpu.CompilerParams(dimension_semantics=sem, vmem_limit_bytes=VMEM_LIMIT)


def _div_tile(n, prefs):
    for p in prefs:
        if n % p == 0:
            return p
    return n


def _row_tile(rows, cols, itemsize=4, limit=2 * 1024 * 1024):
    for p in (512, 256, 128, 64, 32, 16):
        if rows % p == 0 and p * cols * itemsize <= limit:
            return p
    return 16 if rows % 16 == 0 else rows


def _tile_cands(n):
    c = [d for d in range(LANES, min(n, 2048) + 1, LANES) if n % d == 0]
    if not c or n <= 2048 and n not in c:
        c.append(n)
    return sorted(set(c), reverse=True)


def _mm_tiles(M, N, K, a_bytes, b_bytes, o_bytes, npairs):
    best = None
    for tk in sorted(set(_tile_cands(K) + [K]), reverse=True):
        for tm in _tile_cands(M):
            for tn in _tile_cands(N):
                mem = npairs * 2 * (tm * tk * a_bytes + tk * tn * b_bytes) + 2 * tm * tn * o_bytes + tm * tn * 4
                mem += npairs * ((tm * tk * 2 if a_bytes == 4 else 0) + (tk * tn * 2 if b_bytes == 4 else 0))
                if mem > VMEM_TILE_BUDGET:
                    continue
                key = (tk == K and tm >= 3 * LANES and tn >= 4 * LANES, tm * tn * tk, tk, tn)
                if best is None or key > best[0]:
                    best = (key, (tm, tn, tk))
    assert best is not None, (M, N, K)
    return best[1]


class Comm:
    def __init__(self, operands, out_shapes, aliases, n_sems, begin, middle=None, finish=None, middle_frac=0.5):
        self.operands, self.out_shapes, self.aliases, self.n_sems = list(operands), list(out_shapes), aliases, n_sems
        self.begin, self.middle, self.finish, self.middle_frac = begin, middle, finish, middle_frac


class _Shifted:
    def __init__(self, sems, off):
        self.sems, self.off = sems, off

    @property
    def at(self):
        return self

    def __getitem__(self, i):
        return self.sems.at[self.off + i]


def merge_comms(a, b):
    assert a.middle is None and b.middle is None
    na_in, na_out = len(a.operands), len(a.out_shapes)

    def both(stage):
        def run(ins, outs, send_sems, recv_sems):
            getattr(a, stage)(ins[:na_in], outs[:na_out], send_sems, recv_sems)
            getattr(b, stage)(ins[na_in:], outs[na_out:], _Shifted(send_sems, a.n_sems), _Shifted(recv_sems, a.n_sems))
        return run

    aliases = dict(a.aliases)
    aliases.update({na_in + i: na_out + o for i, o in b.aliases.items()})
    return Comm(a.operands + b.operands, a.out_shapes + b.out_shapes, aliases, a.n_sems + b.n_sems,
                both("begin"), None, both("finish"))


def carrier_call(body, name, grid, in_specs, out_specs, out_shape, scratch_shapes, args, comm, semantics):
    n_in, n_out = len(args), len(out_shape)
    if comm is None:
        outs = pl.pallas_call(body, name=name, grid=grid, in_specs=in_specs, out_specs=out_specs, out_shape=out_shape,
                              scratch_shapes=scratch_shapes, compiler_params=_params(semantics))(*args)
        return list(outs), []
    ci, co = len(comm.operands), len(comm.out_shapes)
    total = math.prod(grid)
    middle_at = min(total - 1, max(0, int(total * comm.middle_frac)))

    def carried(*refs):
        ins, cins = refs[:n_in], refs[n_in:n_in + ci]
        outs = refs[n_in + ci:n_in + ci + n_out]
        couts = refs[n_in + ci + n_out:n_in + ci + n_out + co]
        scratch, (send_sems, recv_sems) = refs[n_in + ci + n_out + co:-2], refs[-2:]
        step = 0
        for d, size in enumerate(grid):
            step = step * size + pl.program_id(d)

        @pl.when(step == 0)
        def _():
            comm.begin(cins, couts, send_sems, recv_sems)

        if comm.middle is not None:
            @pl.when(step == middle_at)
            def _():
                comm.middle(cins, couts, send_sems, recv_sems)

        body(*ins, *outs, *scratch)

        @pl.when(step == total - 1)
        def _():
            comm.finish(cins, couts, send_sems, recv_sems)

    outs = pl.pallas_call(
        carried, name=name, grid=grid,
        in_specs=list(in_specs) + [ANY] * ci, out_specs=list(out_specs) + [ANY] * co,
        out_shape=list(out_shape) + comm.out_shapes,
        scratch_shapes=list(scratch_shapes) + [pltpu.SemaphoreType.DMA((comm.n_sems,)),
                                                pltpu.SemaphoreType.DMA((comm.n_sems,))],
        input_output_aliases={n_in + i: n_out + o for i, o in comm.aliases.items()},
        compiler_params=pltpu.CompilerParams(dimension_semantics=("arbitrary",) * len(grid),
                                             vmem_limit_bytes=VMEM_LIMIT, has_side_effects=True),
    )(*args, *comm.operands)
    return list(outs[:n_out]), list(outs[n_out:])


def comm_call(name, comm):
    ci = len(comm.operands)

    def body(*refs):
        cins, couts = refs[:ci], refs[ci:ci + len(comm.out_shapes)]
        send_sems, recv_sems = refs[-2:]
        comm.begin(cins, couts, send_sems, recv_sems)
        if comm.middle is not None:
            comm.middle(cins, couts, send_sems, recv_sems)
        comm.finish(cins, couts, send_sems, recv_sems)

    return pl.pallas_call(
        body, name=name, in_specs=[ANY] * ci, out_specs=[ANY] * len(comm.out_shapes), out_shape=comm.out_shapes,
        input_output_aliases=dict(comm.aliases),
        scratch_shapes=[pltpu.SemaphoreType.DMA((comm.n_sems,)), pltpu.SemaphoreType.DMA((comm.n_sems,))],
        compiler_params=pltpu.CompilerParams(has_side_effects=True),
    )(*comm.operands)


_DIMS ={"nn": (((1,), (0,)), ((), ())), "nt": (((1,), (1,)), ((), ())), "tn": (((0,), (0,)), ((), ()))}


def matmul(pairs, kind, out_dtype, name, comm=None, add=None):
    a0, b0 = pairs[0]
    if kind == "nn":
        (M, K), N = a0.shape, b0.shape[1]
    elif kind == "nt":
        (M, K), N = a0.shape, b0.shape[0]
    else:
        (K, M), N = a0.shape, b0.shape[1]
    tm, tn, tk = _mm_tiles(M, N, K, a0.dtype.itemsize, b0.dtype.itemsize, jnp.dtype(out_dtype).itemsize, len(pairs))
    nk = K // tk
    dims = _DIMS[kind]
    npairs = len(pairs)
    n_in = 2 * npairs + (add is not None)

    def body(*refs):
        ins, o_ref = refs[:2 * npairs], refs[n_in]
        part = None
        for p in range(npairs):
            d = lax.dot_general(ins[2 * p][...].astype(BF16), ins[2 * p + 1][...].astype(BF16), dims,
                                preferred_element_type=F32)
            part = d if part is None else part + d

        def finish(total):
            if add is not None:
                total = total + refs[2 * npairs][...]
            o_ref[...] = total.astype(o_ref.dtype)

        if nk == 1:
            finish(part)
        else:
            acc_ref = refs[n_in + 1]
            k = pl.program_id(2)

            @pl.when(k == 0)
            def _():
                acc_ref[...] = part

            @pl.when(k > 0)
            def _():
                acc_ref[...] += part

            @pl.when(k == nk - 1)
            def _():
                finish(acc_ref[...])

    if kind == "nn":
        a_blk, a_map = (tm, tk), lambda j, i, k: (i, k)
        b_blk, b_map = (tk, tn), lambda j, i, k: (k, j)
    elif kind == "nt":
        a_blk, a_map = (tm, tk), lambda j, i, k: (i, k)
        b_blk, b_map = (tn, tk), lambda j, i, k: (j, k)
    else:
        a_blk, a_map = (tk, tm), lambda j, i, k: (k, i)
        b_blk, b_map = (tk, tn), lambda j, i, k: (k, j)
    o_spec = pl.BlockSpec((tm, tn), lambda j, i, k: (i, j))
    (out,), arrived = carrier_call(
        body, name, (N // tn, M // tm, nk),
        [pl.BlockSpec(a_blk, a_map), pl.BlockSpec(b_blk, b_map)] * npairs + ([o_spec] if add is not None else []),
        [o_spec], [jax.ShapeDtypeStruct((M, N), out_dtype)],
        [] if nk == 1 else [pltpu.VMEM((tm, tn), F32)],
        [t for ab in pairs for t in ab] + ([add] if add is not None else []), comm,
        ("parallel", "parallel", "arbitrary"))
    return out if comm is None else ([out], arrived)


def _sigmoid(x):
    return 1.0 / (1.0 + jnp.exp(-x))

def dual_matmul_act(x, w1, w2, act, out_dtype, name, comm=None):
    M, K = x.shape
    N = w1.shape[-1]
    tm = _div_tile(M, (ROW_TILE,))
    tn = _div_tile(N, (1408, 1024, 512, 256, 128))

    def body(x_ref, w1_ref, w2_ref, o1_ref, o2_ref, out_ref):
        xv = x_ref[...]
        o1 = jnp.dot(xv, w1_ref[...], preferred_element_type=F32)
        o2 = jnp.dot(xv, w2_ref[...], preferred_element_type=F32)
        o1_ref[...] = o1.astype(BF16)
        o2_ref[...] = o2.astype(BF16)
        if act == "swiglu":
            out = o1 * _sigmoid(o1) * o2
        else:
            out = o1 * _sigmoid(o2)
        out_ref[...] = out.astype(out_ref.dtype)

    w_spec = pl.BlockSpec((K, tn), lambda j, i: (0, j))
    o_spec = pl.BlockSpec((tm, tn), lambda j, i: (i, j))
    return carrier_call(
        body, name, (N // tn, M // tm), [pl.BlockSpec((tm, K), lambda j, i: (i, 0)), w_spec, w_spec],
        [o_spec, o_spec, o_spec],
        [jax.ShapeDtypeStruct((M, N), BF16), jax.ShapeDtypeStruct((M, N), BF16),
         jax.ShapeDtypeStruct((M, N), out_dtype)], [], (x, w1, w2), comm, ("parallel", "parallel"))


def ffn_bwd_act(df, wd, a, b, name, comm=None):
    M, K = df.shape
    N = wd.shape[0]
    tm = _div_tile(M, (ROW_TILE,))
    tn = _div_tile(N, (1408, 1024, 512, 256, 128))

    def body(df_ref, wd_ref, a_ref, b_ref, da_ref, db_ref):
        dh = lax.dot_general(df_ref[...], wd_ref[...], _DIMS["nt"], preferred_element_type=F32)
        av = a_ref[...].astype(F32)
        bv = b_ref[...].astype(F32)
        sig = _sigmoid(av)
        silu = av * sig
        da_ref[...] = (dh * bv * (sig + silu * (1.0 - sig))).astype(BF16)
        db_ref[...] = (dh * silu).astype(BF16)

    t_spec = pl.BlockSpec((tm, tn), lambda j, i: (i, j))
    return carrier_call(
        body, name, (N // tn, M // tm),
        [pl.BlockSpec((tm, K), lambda j, i: (i, 0)), pl.BlockSpec((tn, K), lambda j, i: (j, 0)), t_spec, t_spec],
        [t_spec, t_spec], [jax.ShapeDtypeStruct((M, N), BF16)] * 2, [], (df, wd, a, b), comm,
        ("parallel", "parallel"))


def glu_bwd_act(dout, o1, o2, name):
    M, N = dout.shape
    tm = _div_tile(M, (ROW_TILE,))

    def body(d_ref, o1_ref, o2_ref, d1_ref, d2_ref):
        d = d_ref[...].astype(F32)
        sig = _sigmoid(o2_ref[...].astype(F32))
        d1_ref[...] = (d * sig).astype(BF16)
        d2_ref[...] = (d * o1_ref[...].astype(F32) * sig * (1.0 - sig)).astype(BF16)

    spec = pl.BlockSpec((tm, N), lambda i: (i, 0))
    return pl.pallas_call(
        body, name=name, grid=(M // tm,), in_specs=[spec] * 3, out_specs=[spec] * 2,
        out_shape=[jax.ShapeDtypeStruct((M, N), BF16)] * 2,
        compiler_params=_params(("parallel",)),
    )(dout, o1, o2)


def rmsnorm_fwd(x, g, out_dtype, name, residual=None):
    L, D = x.shape
    tr = _div_tile(L, (ROW_TILE,))
    has_res = residual is not None

    def body(*refs):
        x_ref, g_ref = refs[0], refs[1]
        o_ref = refs[-1]
        xv = x_ref[...]
        r = lax.rsqrt(jnp.mean(xv * xv, axis=-1, keepdims=True) + NORM_EPS)
        y = xv * r * g_ref[...]
        if has_res:
            y = refs[2][...] + y
        o_ref[...] = y.astype(o_ref.dtype)

    row = pl.BlockSpec((tr, D), lambda i: (i, 0))
    gsp = pl.BlockSpec((1, D), lambda i: (0, 0))
    args = (x, g) + ((residual,) if has_res else ())
    return pl.pallas_call(
        body, name=name, grid=(L // tr,), in_specs=[row, gsp] + ([row] if has_res else []), out_specs=row,
        out_shape=jax.ShapeDtypeStruct((L, D), out_dtype), compiler_params=_params(("parallel",)),
    )(*args)


def rmsnorm_bwd(x, g, dy, out_dtype, name, add=None, dy2=None):
    L, D = x.shape
    tr = _div_tile(L, (ROW_TILE,))
    has_add = add is not None
    has_dy2 = dy2 is not None

    def body(*refs):
        x_ref, g_ref, dy_ref = refs[0], refs[1], refs[2]
        dx_ref, dg_ref = refs[-2], refs[-1]
        xv = x_ref[...]
        dyv = dy_ref[...].astype(F32)
        if has_dy2:
            dyv = dyv + refs[3][...].astype(F32)
        r = lax.rsqrt(jnp.mean(xv * xv, axis=-1, keepdims=True) + NORM_EPS)
        t = dyv * g_ref[...]
        dx = r * t - xv * (r * r * r) * jnp.mean(xv * t, axis=-1, keepdims=True)
        if has_add:
            dx = refs[3 + has_dy2][...] + dx
        dx_ref[...] = dx.astype(dx_ref.dtype)
        dgp = jnp.sum(dyv * xv * r, axis=0, keepdims=True)

        @pl.when(pl.program_id(0) == 0)
        def _():
            dg_ref[...] = dgp

        @pl.when(pl.program_id(0) > 0)
        def _():
            dg_ref[...] += dgp

    row = pl.BlockSpec((tr, D), lambda i: (i, 0))
    gsp = pl.BlockSpec((1, D), lambda i: (0, 0))
    args = (x, g, dy) + ((dy2,) if has_dy2 else ()) + ((add,) if has_add else ())
    return pl.pallas_call(
        body, name=name, grid=(L // tr,), in_specs=[row, gsp] + [row] * (len(args) - 2),
        out_specs=[row, gsp],
        out_shape=[jax.ShapeDtypeStruct((L, D), out_dtype), jax.ShapeDtypeStruct((1, D), F32)],
        compiler_params=_params(("arbitrary",)),
    )(*args)


def _gate_z(fg_ref, b_ref):
    return fg_ref[...] + b_ref[...]


def gate_fwd(fg_src, col_block, b, name):
    L = fg_src.shape[0]
    T = _div_tile(L, (ROW_TILE,))

    def body(fg_ref, b_ref, c_ref, carry):
        @pl.when(pl.program_id(0) == 0)
        def _():
            carry[...] = jnp.zeros_like(carry)

        z = _gate_z(fg_ref, b_ref)
        logf = jnp.minimum(z, 0.0) - jnp.log(1.0 + jnp.exp(-jnp.abs(z)))
        tri = (lax.broadcasted_iota(jnp.int32, (T, T), 1) <= lax.broadcasted_iota(jnp.int32, (T, T), 0)).astype(F32)
        c = jnp.dot(tri, logf, precision=lax.Precision.HIGHEST, preferred_element_type=F32) + carry[...]
        c_ref[...] = c
        carry[...] = c[T - 1:T, :]

    return pl.pallas_call(
        body, name=name, grid=(L // T,),
        in_specs=[pl.BlockSpec((T, LANES), lambda i: (i, col_block)), pl.BlockSpec((1, LANES), lambda i: (0, 0))],
        out_specs=pl.BlockSpec((T, LANES), lambda i: (i, 0)),
        out_shape=jax.ShapeDtypeStruct((L, LANES), F32),
        scratch_shapes=[pltpu.VMEM((1, LANES), F32)],
        compiler_params=_params(("arbitrary",)),
    )(fg_src, b)


def gate_bwd(fg_src, col_block, b, dc, name):
    L = fg_src.shape[0]
    T = _div_tile(L, (ROW_TILE,))
    nb = L // T

    def body(fg_ref, b_ref, dc_ref, dfg_ref, db_ref, carry):
        @pl.when(pl.program_id(0) == 0)
        def _():
            carry[...] = jnp.zeros_like(carry)
            db_ref[...] = jnp.zeros_like(db_ref)

        z = _gate_z(fg_ref, b_ref)
        dcv = dc_ref[...]
        tri = (lax.broadcasted_iota(jnp.int32, (T, T), 1) >= lax.broadcasted_iota(jnp.int32, (T, T), 0)).astype(F32)
        dlogf = jnp.dot(tri, dcv, precision=lax.Precision.HIGHEST, preferred_element_type=F32) + carry[...]
        dfg = dlogf * _sigmoid(-z)
        dfg_ref[...] = dfg
        db_ref[...] += jnp.sum(dfg, axis=0, keepdims=True)
        carry[...] = dlogf[0:1, :]

    return pl.pallas_call(
        body, name=name, grid=(nb,),
        in_specs=[pl.BlockSpec((T, LANES), lambda i: (nb - 1 - i, col_block)),
                  pl.BlockSpec((1, LANES), lambda i: (0, 0)),
                  pl.BlockSpec((T, LANES), lambda i: (nb - 1 - i, 0))],
        out_specs=[pl.BlockSpec((T, LANES), lambda i: (nb - 1 - i, 0)), pl.BlockSpec((1, LANES), lambda i: (0, 0))],
        out_shape=[jax.ShapeDtypeStruct((L, LANES), F32), jax.ShapeDtypeStruct((1, LANES), F32)],
        scratch_shapes=[pltpu.VMEM((1, LANES), F32)],
        compiler_params=_params(("arbitrary",)),
    )(fg_src, b, dc)


def attn_fwd(proj, cq_col, ck_row, name, comm=None):
    L = proj.shape[0]
    T = _div_tile(L, (ROW_TILE,))
    nq = L // T
    npair = HEADS // 2
    scale = HEAD_DIM ** -0.5
    SUB = ATTN_ROWS
    nsub = T // SUB

    def body(q_ref, k_ref, v_ref, cq_ref, ck_ref, o_ref, lse_ref):
        qb = pl.program_id(1)
        rows = [slice(r * SUB, (r + 1) * SUB) for r in range(nsub)]
        head1 = lax.broadcasted_iota(jnp.int32, (SUB, LANES), 1) >= HEAD_DIM
        qs = [[jnp.where(head1 == (h == 1), q_ref[rs, :] * scale, 0.0).astype(BF16) for rs in rows] for h in range(2)]
        cqs = [[cq_ref[0, rs, h:h + 1] for rs in rows] for h in range(2)]

        def logits(kb):
            ks = pl.multiple_of(kb * T, T)
            k = k_ref[pl.ds(ks, T), :]
            return tuple(lax.dot_general(qs[h][r], k, _DIMS["nt"], preferred_element_type=F32) + cqs[h][r]
                         - ck_ref[0, h:h + 1, pl.ds(ks, T)] for h in range(2) for r in range(nsub))

        def softmax_step(kb, s_all, carry, masked):
            ks = pl.multiple_of(kb * T, T)
            v = v_ref[pl.ds(ks, T), :]
            lane = lax.broadcasted_iota(jnp.int32, (T, LANES), 1)
            new = []
            for h in range(2):
                vh = jnp.where(lane == spare[h], 1.0, v).astype(BF16)
                for r in range(nsub):
                    m, acc = carry[h * nsub + r]
                    s, vr = s_all[h * nsub + r], vh
                    if masked:
                        n = (r + 1) * SUB
                        s, vr = s[:, :n], vh[:n]
                        keep = (lax.broadcasted_iota(jnp.int32, (SUB, n), 1)
                                <= lax.broadcasted_iota(jnp.int32, (SUB, n), 0) + r * SUB)
                        s = jnp.where(keep, s, NEG_BIG)
                    m_new = jnp.maximum(m, jnp.max(s, axis=1, keepdims=True))
                    p = jnp.exp(s - m_new)
                    acc = jnp.exp(m - m_new) * acc + jnp.dot(p.astype(BF16), vr, preferred_element_type=F32)
                    new.append((m_new, acc))
            return tuple(new)

        def step(kb, state):
            s_all, carry = state
            s_next = logits(kb + 1)
            return s_next, softmax_step(kb, s_all, carry, False)

        spare = (HEAD_DIM, 0)
        one = (jnp.full((SUB, 1), NEG_BIG, F32), jnp.zeros((SUB, LANES), F32))
        s_all, carry = lax.fori_loop(0, qb, step, (logits(0), (one,) * (2 * nsub)))
        carry = softmax_step(qb, s_all, carry, True)
        out, lse = [], []
        for h in range(2):
            chains = carry[h * nsub:(h + 1) * nsub]
            ls = [acc[:, spare[h]:spare[h] + 1] for _, acc in chains]
            out.append(jnp.concatenate([acc / l for (_, acc), l in zip(chains, ls)], axis=0))
            lse.append(jnp.concatenate([m + jnp.log(l) for (m, _), l in zip(chains, ls)], axis=0))
        o_ref[...] = jnp.where(lax.broadcasted_iota(jnp.int32, (T, LANES), 1) >= HEAD_DIM, out[1], out[0]
                               ).astype(o_ref.dtype)
        lse_ref[0] = jnp.concatenate(lse, axis=1)

    return carrier_call(
        body, name, (npair, nq),
        [pl.BlockSpec((T, LANES), lambda p, i: (i, p)),
         pl.BlockSpec((L, LANES), lambda p, i: (0, npair + p)),
         pl.BlockSpec((L, LANES), lambda p, i: (0, 2 * npair + p)),
         pl.BlockSpec((1, T, 2), lambda p, i: (p, i, 0)),
         pl.BlockSpec((1, 2, L), lambda p, i: (p, 0, 0))],
        [pl.BlockSpec((T, LANES), lambda p, i: (i, p)), pl.BlockSpec((1, T, 2), lambda p, i: (p, i, 0))],
        [jax.ShapeDtypeStruct((L, ATTN_W), BF16), jax.ShapeDtypeStruct((npair, L, 2), F32)],
        [], (proj, proj, proj, cq_col, ck_row), comm, ("parallel", "parallel"))


def attn_delta(dcat, cat, name):
    L = dcat.shape[0]
    T = _div_tile(L, (ROW_TILE,))

    def body(do_ref, o_ref, d_ref):
        prod = do_ref[...] * o_ref[...].astype(F32)
        sel = (lax.broadcasted_iota(jnp.int32, (ATTN_W, LANES), 0) // HEAD_DIM
               == lax.broadcasted_iota(jnp.int32, (ATTN_W, LANES), 1)).astype(F32)
        d_ref[...] = jnp.dot(prod, sel, precision=lax.Precision.HIGHEST, preferred_element_type=F32)

    return pl.pallas_call(
        body, name=name, grid=(L // T,),
        in_specs=[pl.BlockSpec((T, ATTN_W), lambda i: (i, 0)), pl.BlockSpec((T, ATTN_W), lambda i: (i, 0))],
        out_specs=pl.BlockSpec((T, LANES), lambda i: (i, 0)),
        out_shape=jax.ShapeDtypeStruct((L, LANES), F32),
        compiler_params=_params(("parallel",)),
    )(dcat, cat)


def attn_bwd(proj, dcat, lse_row, delta_row, cq_row, ck_col, name, comm=None):
    L = proj.shape[0]
    T = _div_tile(L, (ROW_TILE,))
    nb = L // T
    npair = HEADS // 2
    scale = HEAD_DIM ** -0.5

    def body(q_ref, k_ref, v_ref, do_ref, lse_ref, dl_ref, cq_ref, ck_ref,
             dq_ref, dk_ref, dv_ref, dcq_ref, dck_ref, dq_acc, dcq_acc):
        kb = pl.program_id(1)

        @pl.when(kb == 0)
        def _():
            dq_acc[...] = jnp.zeros_like(dq_acc)
            dcq_acc[...] = jnp.zeros_like(dcq_acc)

        head1 = lax.broadcasted_iota(jnp.int32, (T, LANES), 1) >= HEAD_DIM
        ks = [jnp.where(head1 == (h == 1), k_ref[...] * scale, 0.0).astype(BF16) for h in range(2)]
        vs = [jnp.where(head1 == (h == 1), v_ref[...], 0.0).astype(BF16) for h in range(2)]
        cks = [ck_ref[0, :, h:h + 1] for h in range(2)]
        kts = [k.T for k in ks]

        def step(qb, carry, masked):
            qs = pl.multiple_of(qb * T, T)
            q = q_ref[pl.ds(qs, T), :]
            do = do_ref[pl.ds(qs, T), :].astype(BF16)
            new, dq = [], None
            for h in range(2):
                dk, dv, dck = carry[h]
                lse = lse_ref[0, h:h + 1, pl.ds(qs, T)]
                dl = dl_ref[0, h:h + 1, pl.ds(qs, T)]
                cq = cq_ref[0, h:h + 1, pl.ds(qs, T)]
                st = lax.dot_general(ks[h], q, _DIMS["nt"], preferred_element_type=F32) + cq - cks[h]
                if masked:
                    keep = lax.broadcasted_iota(jnp.int32, (T, T), 0) <= lax.broadcasted_iota(jnp.int32, (T, T), 1)
                    st = jnp.where(keep, st, NEG_BIG)
                pt = jnp.exp(st - lse)
                dv = dv + jnp.dot(pt.astype(BF16), do, preferred_element_type=F32)
                dpt = lax.dot_general(vs[h], do, _DIMS["nt"], preferred_element_type=F32)
                dst = pt * (dpt - dl)
                dsb = dst.astype(BF16)
                dk = dk + jnp.dot(dsb, q, preferred_element_type=F32)
                part = jnp.dot(kts[h], dsb, preferred_element_type=F32)
                dq = part if dq is None else dq + part
                dcq_acc[h:h + 1, pl.ds(qs, T)] += jnp.sum(dst, axis=0, keepdims=True)
                dck = dck + jnp.sum(dst, axis=1, keepdims=True)
                new.append((dk, dv, dck))
            dq_acc[:, pl.ds(qs, T)] += dq
            return tuple(new)

        one = (jnp.zeros((T, LANES), F32), jnp.zeros((T, LANES), F32), jnp.zeros((T, 1), F32))
        carry = step(kb, (one, one), True)
        carry = lax.fori_loop(kb + 1, nb, functools.partial(step, masked=False), carry)
        (dk0, dv0, dck0), (dk1, dv1, dck1) = carry
        dk_ref[...] = (jnp.where(head1, dk1, dk0) * scale).astype(dk_ref.dtype)
        dv_ref[...] = jnp.where(head1, dv1, dv0).astype(dv_ref.dtype)
        dck_ref[0] = jnp.concatenate([-dck0, -dck1], axis=1)

        @pl.when(kb == nb - 1)
        def _():
            dq_ref[...] = dq_acc[...].T.astype(dq_ref.dtype)
            dcq_ref[0] = dcq_acc[...]

    full = lambda col: pl.BlockSpec((L, LANES), col)
    row_stat = pl.BlockSpec((1, 2, L), lambda p, i: (p, 0, 0))
    return carrier_call(
        body, name, (npair, nb),
        [full(lambda p, i: (0, p)),
         pl.BlockSpec((T, LANES), lambda p, i: (i, npair + p)),
         pl.BlockSpec((T, LANES), lambda p, i: (i, 2 * npair + p)),
         full(lambda p, i: (0, p)),
         row_stat, row_stat, row_stat,
         pl.BlockSpec((1, T, 2), lambda p, i: (p, i, 0))],
        [full(lambda p, i: (0, p)),
         pl.BlockSpec((T, LANES), lambda p, i: (i, p)),
         pl.BlockSpec((T, LANES), lambda p, i: (i, p)),
         row_stat,
         pl.BlockSpec((1, T, 2), lambda p, i: (p, i, 0))],
        [jax.ShapeDtypeStruct((L, ATTN_W), BF16)] * 3
        + [jax.ShapeDtypeStruct((npair, 2, L), F32), jax.ShapeDtypeStruct((npair, L, 2), F32)],
        [pltpu.VMEM((LANES, L), F32), pltpu.VMEM((2, L), F32)],
        (proj, proj, proj, dcat, lse_row, delta_row, cq_row, ck_col), comm, ("parallel", "arbitrary"))


def _shift_down(x, k):
    rolled = pltpu.roll(x, k, 0)
    return jnp.where(lax.broadcasted_iota(jnp.int32, x.shape, 0) >= k, rolled, 0.0)


def _shift_up(x, k):
    n = x.shape[0]
    rolled = pltpu.roll(x, n - k, 0)
    return jnp.where(lax.broadcasted_iota(jnp.int32, x.shape, 0) < n - k, rolled, 0.0)


def conv_fwd(proj, col0, conv_w, conv_b, name):
    L = proj.shape[0]
    C = conv_w.shape[1]
    nc = C // LANES

    def body(gb_ref, gc_ref, xc_ref, w_ref, b_ref, o_ref):
        z = gc_ref[...] * xc_ref[...]
        conv = (w_ref[0:1, :] * _shift_down(z, 2) + w_ref[1:2, :] * _shift_down(z, 1) + w_ref[2:3, :] * z
                + b_ref[...])
        o_ref[...] = (gb_ref[...] * conv).astype(o_ref.dtype)

    col = lambda off: pl.BlockSpec((L, LANES), lambda j, off=off: (0, col0 + off + j))
    return pl.pallas_call(
        body, name=name, grid=(nc,),
        in_specs=[col(0), col(nc), col(2 * nc), pl.BlockSpec((CONV_K, LANES), lambda j: (0, j)),
                  pl.BlockSpec((1, LANES), lambda j: (0, j))],
        out_specs=pl.BlockSpec((L, LANES), lambda j: (0, j)),
        out_shape=jax.ShapeDtypeStruct((L, C), BF16),
        compiler_params=_params(("parallel",)),
    )(proj, proj, proj, conv_w, conv_b)


def conv_bwd(proj, col0, conv_w, conv_b, dcat, dcol0, name):
    L = proj.shape[0]
    C = conv_w.shape[1]
    nc = C // LANES

    def body(gb_ref, gc_ref, xc_ref, w_ref, b_ref, do_ref, dgb_ref, dgc_ref, dxc_ref, dw_ref, db_ref):
        gc, xc = gc_ref[...], xc_ref[...]
        z = gc * xc
        z1, z2 = _shift_down(z, 1), _shift_down(z, 2)
        w0, w1, w2 = w_ref[0:1, :], w_ref[1:2, :], w_ref[2:3, :]
        conv = w0 * z2 + w1 * z1 + w2 * z + b_ref[...]
        dout = do_ref[...]
        dgb_ref[...] = (dout * conv).astype(dgb_ref.dtype)
        dconv = dout * gb_ref[...]
        dw_ref[...] = jnp.concatenate([jnp.sum(dconv * z2, axis=0, keepdims=True),
                                       jnp.sum(dconv * z1, axis=0, keepdims=True),
                                       jnp.sum(dconv * z, axis=0, keepdims=True)], axis=0)
        db_ref[...] = jnp.sum(dconv, axis=0, keepdims=True)
        dz = w2 * dconv + w1 * _shift_up(dconv, 1) + w0 * _shift_up(dconv, 2)
        dgc_ref[...] = (dz * xc).astype(dgc_ref.dtype)
        dxc_ref[...] = (dz * gc).astype(dxc_ref.dtype)

    col = lambda off: pl.BlockSpec((L, LANES), lambda j, off=off: (0, col0 + off + j))
    out_col = pl.BlockSpec((L, LANES), lambda j: (0, j))
    return pl.pallas_call(
        body, name=name, grid=(nc,),
        in_specs=[col(0), col(nc), col(2 * nc), pl.BlockSpec((CONV_K, LANES), lambda j: (0, j)),
                  pl.BlockSpec((1, LANES), lambda j: (0, j)),
                  pl.BlockSpec((L, LANES), lambda j: (0, dcol0 + j))],
        out_specs=[out_col, out_col, out_col, pl.BlockSpec((CONV_K, LANES), lambda j: (0, j)),
                   pl.BlockSpec((1, LANES), lambda j: (0, j))],
        out_shape=[jax.ShapeDtypeStruct((L, C), BF16)] * 3
        + [jax.ShapeDtypeStruct((CONV_K, C), F32), jax.ShapeDtypeStruct((1, C), F32)],
        compiler_params=_params(("parallel",)),
    )(proj, proj, proj, conv_w, conv_b, dcat)


_GELU_C = math.sqrt(2.0 / math.pi)
_GELU_A = 0.044715


def _gelu(y):
    return 0.5 * y * (1.0 + jnp.tanh(_GELU_C * (y + _GELU_A * y * y * y)))


def _gelu_grad(y):
    t = jnp.tanh(_GELU_C * (y + _GELU_A * y * y * y))
    return 0.5 * (1.0 + t) + 0.5 * y * (1.0 - t * t) * _GELU_C * (1.0 + 3.0 * _GELU_A * y * y)


def _cmul_add(xr, xi, pr, pi, sr, si):
    return xr + pr * sr - pi * si, xi + pr * si + pi * sr


def _scan_tile(br, bi, cr, ci, tab_ref, reverse):
    n = S5_BLOCK_STATES
    xr, xi = br, bi
    for s, k in enumerate((1, 2, 4)):
        shift = SUBLANES - k if reverse else k
        xr, xi = _cmul_add(xr, xi, tab_ref[0, s, :, :n], tab_ref[0, s, :, n:],
                           pltpu.roll(xr, shift, 0), pltpu.roll(xi, shift, 0))
    return _cmul_add(xr, xi, tab_ref[0, 3, :, :n], tab_ref[0, 3, :, n:], cr, ci)


def s5_fwd(u, bmat, cmat, tab, dvec, name, comm=None):
    L, D = u.shape
    nblk = D // LANES
    T = _div_tile(L, (ROW_TILE,))
    ns = 2 * S5_BLOCK_STATES
    n = S5_BLOCK_STATES

    def body(u_ref, b_ref, c_ref, tab_ref, d_ref, y_ref, g_ref, xs_ref, buf, car):
        @pl.when(pl.program_id(1) == 0)
        def _():
            car[...] = jnp.zeros_like(car)

        uv = u_ref[...]
        buf[...] = jnp.dot(uv.astype(BF16), b_ref[0], preferred_element_type=F32)

        def tile(i, carry):
            cr, ci = carry
            r0 = pl.multiple_of(i * SUBLANES, SUBLANES)
            xr, xi = _scan_tile(buf[pl.ds(r0, SUBLANES), :n], buf[pl.ds(r0, SUBLANES), n:], cr, ci, tab_ref, False)
            buf[pl.ds(r0, SUBLANES), :n] = xr
            buf[pl.ds(r0, SUBLANES), n:] = xi
            return xr[SUBLANES - 1:, :], xi[SUBLANES - 1:, :]

        cr, ci = lax.fori_loop(0, T // SUBLANES, tile, (car[:, :n], car[:, n:]))
        car[:, :n] = cr
        car[:, n:] = ci
        xs = buf[...]
        xs_ref[...] = xs
        y = jnp.dot(xs.astype(BF16), c_ref[0], preferred_element_type=F32) + d_ref[...] * uv
        y_ref[...] = y
        g_ref[...] = _gelu(y).astype(g_ref.dtype)

    blk = pl.BlockSpec((T, LANES), lambda j, i: (i, j))
    return carrier_call(
        body, name, (nblk, L // T),
        [blk, pl.BlockSpec((1, LANES, ns), lambda j, i: (j, 0, 0)),
         pl.BlockSpec((1, ns, LANES), lambda j, i: (j, 0, 0)),
         pl.BlockSpec((1, 4, SUBLANES, ns), lambda j, i: (j, 0, 0, 0)),
         pl.BlockSpec((1, LANES), lambda j, i: (0, j))],
        [blk, blk, pl.BlockSpec((T, ns), lambda j, i: (i, j))],
        [jax.ShapeDtypeStruct((L, D), F32), jax.ShapeDtypeStruct((L, D), BF16),
         jax.ShapeDtypeStruct((L, nblk * ns), F32)],
        [pltpu.VMEM((T, ns), F32), pltpu.VMEM((1, ns), F32)],
        (u, bmat, cmat, tab, dvec), comm, ("parallel", "arbitrary"))


def s5_bwd(dg, y, u, xs, cmat_t, bmat_t, rtab, dvec, name, comm=None):
    L, D = u.shape
    nblk = D // LANES
    T = _div_tile(L, (ROW_TILE,))
    nch = L // T
    ns = 2 * S5_BLOCK_STATES
    n = S5_BLOCK_STATES
    ntile = T // SUBLANES

    def body(dg_ref, y_ref, u_ref, xs_ref, xp_ref, ct_ref, bt_ref, tab_ref, d_ref,
             du_ref, dc_ref, db_ref, dlam_ref, dd_ref, buf, xbuf, car):
        step = pl.program_id(1)
        first_chunk = step == nch - 1

        @pl.when(step == 0)
        def _():
            car[...] = jnp.zeros_like(car)
            dc_ref[...] = jnp.zeros_like(dc_ref)
            db_ref[...] = jnp.zeros_like(db_ref)
            dlam_ref[...] = jnp.zeros_like(dlam_ref)
            dd_ref[...] = jnp.zeros_like(dd_ref)

        uv = u_ref[...]
        dy = dg_ref[...].astype(F32) * _gelu_grad(y_ref[...])
        dd_ref[...] += jnp.sum(dy * uv, axis=0, keepdims=True)
        dyb = dy.astype(BF16)
        buf[...] = jnp.dot(dyb, ct_ref[0], preferred_element_type=F32)
        xs = xs_ref[...]
        xbuf[pl.ds(SUBLANES, T), :] = xs
        xbuf[pl.ds(0, SUBLANES), :] = jnp.where(first_chunk, 0.0, xp_ref[...])
        row0 = lax.broadcasted_iota(jnp.int32, (SUBLANES, n), 0) == 0

        def tile(ii, carry):
            cr, ci, ar, ai = carry
            r0 = pl.multiple_of((ntile - 1 - ii) * SUBLANES, SUBLANES)
            xr, xi = _scan_tile(buf[pl.ds(r0, SUBLANES), :n], buf[pl.ds(r0, SUBLANES), n:], cr, ci, tab_ref, True)
            buf[pl.ds(r0, SUBLANES), :n] = xr
            buf[pl.ds(r0, SUBLANES), n:] = xi
            r1 = pl.multiple_of(r0 + SUBLANES, SUBLANES)
            pr = jnp.where(row0, xbuf[pl.ds(r0, SUBLANES), :n][SUBLANES - 1:, :],
                           pltpu.roll(xbuf[pl.ds(r1, SUBLANES), :n], 1, 0))
            pi = jnp.where(row0, xbuf[pl.ds(r0, SUBLANES), n:][SUBLANES - 1:, :],
                           pltpu.roll(xbuf[pl.ds(r1, SUBLANES), n:], 1, 0))
            ar = ar + xr * pr + xi * pi
            ai = ai + xi * pr - xr * pi
            return xr[0:1, :], xi[0:1, :], ar, ai

        zero = jnp.zeros((SUBLANES, n), F32)
        cr, ci, ar, ai = lax.fori_loop(0, ntile, tile, (car[:, :n], car[:, n:], zero, zero))
        car[:, :n] = cr
        car[:, n:] = ci
        dlam_ref[0, :, :n] += ar
        dlam_ref[0, :, n:] += ai
        dxa = buf[...]
        dc_ref[0] += lax.dot_general(dyb, xs.astype(BF16), _DIMS["tn"], preferred_element_type=F32)
        dxb = dxa.astype(BF16)
        db_ref[0] += lax.dot_general(uv.astype(BF16), dxb, _DIMS["tn"], preferred_element_type=F32)
        du_ref[...] = jnp.dot(dxb, bt_ref[0], preferred_element_type=F32) + d_ref[...] * dy

    rev = lambda j, i: (nch - 1 - i, j)
    blk = pl.BlockSpec((T, LANES), rev)
    tpb = T // SUBLANES
    acc = pl.BlockSpec((1, LANES, ns), lambda j, i: (j, 0, 0))
    return carrier_call(
        body, name, (nblk, nch),
        [blk, blk, blk, pl.BlockSpec((T, ns), rev),
         pl.BlockSpec((SUBLANES, ns), lambda j, i: (jnp.maximum((nch - 1 - i) * tpb - 1, 0), j)),
         pl.BlockSpec((1, LANES, ns), lambda j, i: (j, 0, 0)),
         pl.BlockSpec((1, ns, LANES), lambda j, i: (j, 0, 0)),
         pl.BlockSpec((1, 4, SUBLANES, ns), lambda j, i: (j, 0, 0, 0)),
         pl.BlockSpec((1, LANES), lambda j, i: (0, j))],
        [blk, acc, acc, pl.BlockSpec((1, SUBLANES, ns), lambda j, i: (j, 0, 0)),
         pl.BlockSpec((1, LANES), lambda j, i: (0, j))],
        [jax.ShapeDtypeStruct((L, D), F32), jax.ShapeDtypeStruct((nblk, LANES, ns), F32),
         jax.ShapeDtypeStruct((nblk, LANES, ns), F32), jax.ShapeDtypeStruct((nblk, SUBLANES, ns), F32),
         jax.ShapeDtypeStruct((1, D), F32)],
        [pltpu.VMEM((T, ns), F32), pltpu.VMEM((T + SUBLANES, ns), F32), pltpu.VMEM((1, ns), F32)],
        (dg, y, u, xs, xs, cmat_t, bmat_t, rtab, dvec), comm, ("parallel", "arbitrary"))


def _s5_discretize(a_re, a_im, log_step, b_re, b_im):
    lam_re = jnp.minimum(a_re, -S5_MIN_DECAY)
    lam_im = a_im
    delta = jnp.exp(log_step)[:, None]
    mag = jnp.exp(lam_re * delta)
    ang = lam_im * delta
    lb_re = mag * jnp.cos(ang)
    lb_im = mag * jnp.sin(ang)
    den = lam_re * lam_re + lam_im * lam_im
    nr = lb_re - 1.0
    ni = lb_im
    coef_re = (nr * lam_re + ni * lam_im) / den
    coef_im = (ni * lam_re - nr * lam_im) / den
    bb_re = coef_re[..., None] * b_re - coef_im[..., None] * b_im
    bb_im = coef_re[..., None] * b_im + coef_im[..., None] * b_re
    return lb_re, lb_im, bb_re, bb_im


def _s5_tables(lb_re, lb_im):
    nblk = lb_re.shape[0] // S5_BLOCK_GROUPS
    lr = lb_re.reshape(nblk, S5_BLOCK_STATES)
    li = lb_im.reshape(nblk, S5_BLOCK_STATES)
    pows = [(jnp.ones_like(lr), jnp.zeros_like(li))]
    for _ in range(SUBLANES):
        pr, pi = pows[-1]
        pows.append((pr * lr - pi * li, pr * li + pi * lr))
    rows = jnp.arange(SUBLANES)[None, :, None]

    def table(conj, reverse):
        sgn = -1.0 if conj else 1.0
        out = []
        for k in (1, 2, 4):
            mask = (rows <= SUBLANES - 1 - k) if reverse else (rows >= k)
            out.append(jnp.concatenate([jnp.where(mask, pows[k][0][:, None, :], 0.0),
                                        jnp.where(mask, sgn * pows[k][1][:, None, :], 0.0)], axis=-1))
        order = range(SUBLANES, 0, -1) if reverse else range(1, SUBLANES + 1)
        cre = jnp.stack([pows[k][0] for k in order], axis=1)
        cim = jnp.stack([sgn * pows[k][1] for k in order], axis=1)
        out.append(jnp.concatenate([cre, cim], axis=-1))
        return jnp.stack(out, axis=1)

    return table(False, False), table(True, True)


def _s5_block_mats(bb_re, bb_im, c_re, c_im):
    G = bb_re.shape[0]
    nblk = G // S5_BLOCK_GROUPS
    eye = jnp.eye(S5_BLOCK_GROUPS, dtype=F32)
    bb = jnp.stack([bb_re, bb_im]).reshape(2, nblk, S5_BLOCK_GROUPS, S5_STATE, S5_GROUP)
    bmat = jnp.einsum("ab,rjaph->jahrbp", eye, bb).reshape(nblk, LANES, 2 * S5_BLOCK_STATES)
    cc = jnp.stack([c_re, -c_im]).reshape(2, nblk, S5_BLOCK_GROUPS, S5_GROUP, S5_STATE)
    cmat = jnp.einsum("ab,rjahp->jrbpah", eye, cc).reshape(nblk, 2 * S5_BLOCK_STATES, LANES)
    return bmat, cmat


def _s5_unblock(dB, dC, dlam):
    nblk = dB.shape[0]
    G = nblk * S5_BLOCK_GROUPS
    d6 = dB.reshape(nblk, S5_BLOCK_GROUPS, S5_GROUP, 2, S5_BLOCK_GROUPS, S5_STATE)
    dbb = jnp.einsum("jahrap->rjaph", d6).reshape(2, G, S5_STATE, S5_GROUP)
    c6 = dC.reshape(nblk, S5_BLOCK_GROUPS, S5_GROUP, 2, S5_BLOCK_GROUPS, S5_STATE)
    dcc = jnp.einsum("jahrap->rjahp", c6).reshape(2, G, S5_GROUP, S5_STATE)
    dl = jnp.sum(dlam, axis=1).reshape(nblk, 2, S5_BLOCK_GROUPS, S5_STATE)
    dl = jnp.transpose(dl, (1, 0, 2, 3)).reshape(2, G, S5_STATE)
    return dbb[0], dbb[1], dcc[0], -dcc[1], dl[0], dl[1]


def loss_and_grad(y, target, name):
    L, D = y.shape
    tr = _div_tile(L, (512, 256, 128))

    def body(y_ref, t_ref, dy_ref, loss_ref):
        err = y_ref[...] - t_ref[...]
        dy_ref[...] = err * (1.0 / D)
        part = 0.5 * jnp.sum(jnp.mean(err * err, axis=-1, keepdims=True), axis=0, keepdims=True)

        @pl.when(pl.program_id(0) == 0)
        def _():
            loss_ref[...] = part

        @pl.when(pl.program_id(0) > 0)
        def _():
            loss_ref[...] += part

    row = pl.BlockSpec((tr, D), lambda i: (i, 0))
    return pl.pallas_call(
        body, name=name, grid=(L // tr,), in_specs=[row, row],
        out_specs=[row, pl.BlockSpec((1, 1), lambda i: (0, 0))],
        out_shape=[jax.ShapeDtypeStruct((L, D), F32), jax.ShapeDtypeStruct((1, 1), F32)],
        compiler_params=_params(("arbitrary",)),
    )(y, target)


def _adam_math(w, g, m, v):
    m = ADAM_B1 * m + (1.0 - ADAM_B1) * g
    v = ADAM_B2 * v + (1.0 - ADAM_B2) * (g * g)
    m_hat = m / (1.0 - ADAM_B1 ** ADAM_STEP)
    v_hat = v / (1.0 - ADAM_B2 ** ADAM_STEP)
    delta = -ADAM_LR * (m_hat / (jnp.sqrt(v_hat) + ADAM_EPS) + ADAM_WD * w)
    return delta, m, v


def _as3d(a):
    return a.reshape((-1,) + a.shape[-2:])


def adamw(w, g, m, v, name):
    shape = w.shape
    w3, g3, m3, v3 = _as3d(w), _as3d(g), _as3d(m), _as3d(v)
    A, R, C = w3.shape
    tr = _row_tile(R, C)

    def body(w_ref, g_ref, m_ref, v_ref, d_ref, mo_ref, vo_ref):
        d, mn, vn = _adam_math(w_ref[...], g_ref[...], m_ref[...], v_ref[...])
        d_ref[...] = d
        mo_ref[...] = mn
        vo_ref[...] = vn

    spec = pl.BlockSpec((1, tr, C), lambda a, i: (a, i, 0))
    outs = pl.pallas_call(
        body, name=name, grid=(A, R // tr), in_specs=[spec] * 4, out_specs=[spec] * 3,
        out_shape=[jax.ShapeDtypeStruct((A, R, C), F32)] * 3,
        compiler_params=_params(("parallel", "parallel")),
    )(w3, g3, m3, v3)
    return [o.reshape(shape) for o in outs]


def _place():
    x, y, c = lax.axis_index("x"), lax.axis_index("y"), lax.axis_index("c")
    other_chips = [(1 - x, y), (x, 1 - y), (1 - x, 1 - y)]
    return x, y, c, other_chips


def _chip_id(chip):
    return 2 * chip[0] + chip[1]


def _my_chip():
    return 2 * lax.axis_index("x") + lax.axis_index("y")


def _remote(src, dst, send_sem, recv_sem, dev):
    return pltpu.make_async_remote_copy(src_ref=src, dst_ref=dst, send_sem=send_sem, recv_sem=recv_sem,
                                        device_id=dev, device_id_type=MESH)


def allgather_small(arrs):
    T = len(arrs)

    def body(*refs):
        ins, outs = refs[:T], refs[T:2 * T]
        send_sems, recv_sems = refs[2 * T:]
        x, y, c, chips = _place()
        me = _chip_id((x, y))
        sends = []
        for t in range(T):
            outs[t][me] = ins[t][...]
            for j, chip in enumerate(chips):
                cp = _remote(ins[t], outs[t].at[me], send_sems.at[3 * t + j], recv_sems.at[3 * t + j], (*chip, c))
                cp.start()
                sends.append(cp)
        for t in range(T):
            for j, chip in enumerate(chips):
                slot = outs[t].at[_chip_id(chip)]
                _remote(slot, slot, send_sems.at[3 * t + j], recv_sems.at[3 * t + j], (*chip, c)).wait_recv()
        for cp in sends:
            cp.wait_send()

    return pl.pallas_call(
        body, name="allgather_small", in_specs=[VMEM_SPEC] * T, out_specs=[VMEM_SPEC] * T,
        out_shape=[jax.ShapeDtypeStruct((4,) + a.shape, a.dtype) for a in arrs],
        scratch_shapes=[pltpu.SemaphoreType.DMA((3 * T,)), pltpu.SemaphoreType.DMA((3 * T,))],
        compiler_params=pltpu.CompilerParams(vmem_limit_bytes=VMEM_LIMIT, has_side_effects=True),
    )(*arrs)


def small_swap_comm(buf):
    def copy(ins, outs, send_sems, recv_sems):
        x, y, c, _ = _place()
        return _remote(ins[0], outs[0], send_sems.at[0], recv_sems.at[0], (x, y, 1 - c))

    return Comm([buf], [jax.ShapeDtypeStruct(buf.shape, F32)], {}, 1,
                lambda *refs: copy(*refs).start(), None, lambda *refs: copy(*refs).wait())


def small_exchange_comm(pair):
    def copies(ins, outs, send_sems, recv_sems):
        x, y, c, chips = _place()
        return [_remote(ins[0], outs[0].at[j], send_sems.at[j], recv_sems.at[j], (*chip, c))
                for j, chip in enumerate(chips)]

    def begin(*refs):
        for cp in copies(*refs):
            cp.start()

    def finish(*refs):
        for cp in copies(*refs):
            cp.wait()

    return Comm([pair], [jax.ShapeDtypeStruct((3,) + pair.shape, F32)], {}, 3, begin, None, finish)


def small_pair_sum(mine, theirs, name):
    R, C = mine.shape
    tr = _row_tile(R, C)

    def body(a_ref, b_ref, o_ref):
        o_ref[...] = a_ref[...] + b_ref[...]

    spec = pl.BlockSpec((tr, C), lambda i: (i, 0))
    return pl.pallas_call(body, name=name, grid=(R // tr,), in_specs=[spec, spec], out_specs=spec,
                          out_shape=jax.ShapeDtypeStruct((R, C), F32), compiler_params=_params(("parallel",)))(mine, theirs)


def small_chip_sum(pair, got, name):
    R, C = pair.shape
    tr = _row_tile(R, C)

    def body(p_ref, g_ref, o_ref):
        me = _my_chip()
        terms = []
        for chip in range(4):
            d = jnp.bitwise_xor(me, chip)
            terms.append(jnp.where(d == 0, p_ref[...],
                                   jnp.where(d == 2, g_ref[0], jnp.where(d == 1, g_ref[1], g_ref[2]))))
        o_ref[...] = ((terms[0] + terms[1]) + terms[2]) + terms[3]

    spec = pl.BlockSpec((tr, C), lambda i: (i, 0))
    return pl.pallas_call(body, name=name, grid=(R // tr,),
                          in_specs=[spec, pl.BlockSpec((3, tr, C), lambda i: (0, i, 0))], out_specs=spec,
                          out_shape=jax.ShapeDtypeStruct((R, C), F32), compiler_params=_params(("parallel",)))(pair, got)


def _half_rows(ref, layout, shard, half):
    if layout == "S":
        hr = ref.shape[1] // 2
        return ref.at[shard, pl.ds(pl.multiple_of(half * hr, 16), hr), :]
    hr, C = ref.shape[0] // 2, ref.shape[1] // 4
    return ref.at[pl.ds(pl.multiple_of(half * hr, 16), hr), pl.ds(pl.multiple_of(shard * C, LANES), C)]


def _half_rows_all(ref, layout, half):
    if layout == "S":
        hr = ref.shape[1] // 2
        return ref.at[:, pl.ds(pl.multiple_of(half * hr, 16), hr), :]
    hr = ref.shape[0] // 2
    return ref.at[pl.ds(pl.multiple_of(half * hr, 16), hr), :]


def _shard_of_half(ref, layout, shard):
    if layout == "S":
        return ref.at[shard]
    C = ref.shape[1] // 4
    return ref.at[:, pl.ds(pl.multiple_of(shard * C, LANES), C)]


def _own_block_spec(layout, tr, C):
    if layout == "S":
        return pl.BlockSpec((None, tr, C), lambda i: (_my_chip(), i, 0))
    return pl.BlockSpec((tr, C), lambda i: (i, _my_chip()))


def cast_into_gathered(shards, layer, layout, name):
    _, R, C = shards.shape
    tr = _row_tile(R, C)

    def body(a_ref, o_ref):
        o_ref[...] = a_ref[...].astype(BF16)

    return pl.pallas_call(
        body, name=name, grid=(R // tr,),
        in_specs=[pl.BlockSpec((None, tr, C), lambda i: (layer, i, 0))],
        out_specs=_own_block_spec(layout, tr, C),
        out_shape=jax.ShapeDtypeStruct((4, R, C) if layout == "S" else (R, 4 * C), BF16),
        compiler_params=_params(("parallel",)),
    )(shards)


def gather_comm(bufs, layouts):
    T = len(bufs)

    def begin(_, outs, send_sems, recv_sems):
        x, y, c, chips = _place()
        for t in range(T):
            mine = _half_rows(outs[t], layouts[t], _chip_id((x, y)), c)
            for j, chip in enumerate(chips):
                _remote(mine, mine, send_sems.at[6 * t + j], recv_sems.at[6 * t + j], (*chip, c)).start()

    def middle(_, outs, send_sems, recv_sems):
        x, y, c, chips = _place()
        for t in range(T):
            for j, chip in enumerate(chips):
                piece = _half_rows(outs[t], layouts[t], _chip_id(chip), c)
                _remote(piece, piece, send_sems.at[6 * t + j], recv_sems.at[6 * t + j], (*chip, c)).wait_recv()
                _remote(piece, piece, send_sems.at[6 * t + 3 + j], recv_sems.at[6 * t + 3 + j], (x, y, 1 - c)).start()

    def finish(_, outs, send_sems, recv_sems):
        x, y, c, chips = _place()
        for t in range(T):
            mine = _half_rows(outs[t], layouts[t], _chip_id((x, y)), c)
            for j, chip in enumerate(chips):
                theirs = _half_rows(outs[t], layouts[t], _chip_id(chip), 1 - c)
                _remote(theirs, theirs, send_sems.at[6 * t + 3 + j], recv_sems.at[6 * t + 3 + j],
                        (x, y, 1 - c)).wait_recv()
                _remote(mine, mine, send_sems.at[6 * t + j], recv_sems.at[6 * t + j], (*chip, c)).wait_send()
                piece = _half_rows(outs[t], layouts[t], _chip_id(chip), c)
                _remote(piece, piece, send_sems.at[6 * t + 3 + j], recv_sems.at[6 * t + 3 + j],
                        (x, y, 1 - c)).wait_send()

    return Comm(bufs, [jax.ShapeDtypeStruct(b.shape, b.dtype) for b in bufs], {t: t for t in range(T)}, 6 * T,
                begin, middle, finish, middle_frac=0.75)


def swap_comm(grads, layouts):
    T = len(grads)

    def out_shape(g, layout):
        return (4, g.shape[1] // 2, g.shape[2]) if layout == "S" else (g.shape[0] // 2, g.shape[1])

    def copies(ins, outs, send_sems, recv_sems):
        x, y, c, _ = _place()
        return [_remote(_half_rows_all(ins[t], layouts[t], 1 - c), outs[t], send_sems.at[t], recv_sems.at[t],
                        (x, y, 1 - c)) for t in range(T)]

    def begin(*refs):
        for cp in copies(*refs):
            cp.start()

    def finish(*refs):
        for cp in copies(*refs):
            cp.wait()

    return Comm(grads, [jax.ShapeDtypeStruct(out_shape(g, k), F32) for g, k in zip(grads, layouts)], {}, T,
                begin, None, finish)


def pair_sum(grad, recv, layout, name):
    if layout == "S":
        _, hr, C = recv.shape
        tr = _row_tile(hr, C)
        nb = hr // tr
        grid = (4, nb)
        g_spec = pl.BlockSpec((None, tr, C), lambda a, i: (a, lax.axis_index("c") * nb + i, 0))
        spec = pl.BlockSpec((None, tr, C), lambda a, i: (a, i, 0))
    else:
        hr, C = recv.shape
        tr = _row_tile(hr, C)
        nb = hr // tr
        grid = (nb,)
        g_spec = pl.BlockSpec((tr, C), lambda i: (lax.axis_index("c") * nb + i, 0))
        spec = pl.BlockSpec((tr, C), lambda i: (i, 0))

    def body(g_ref, r_ref, f_ref, b_ref):
        s = g_ref[...] + r_ref[...]
        f_ref[...] = s
        b_ref[...] = s.astype(BF16)

    return pl.pallas_call(
        body, name=name, grid=grid, in_specs=[g_spec, spec], out_specs=[spec, spec],
        out_shape=[jax.ShapeDtypeStruct(recv.shape, F32), jax.ShapeDtypeStruct(recv.shape, BF16)],
        compiler_params=_params(("parallel",) * len(grid)),
    )(grad, recv)


def exchange_comm(pair_bf16, layouts):
    T = len(pair_bf16)

    def out_shape(p, layout):
        return (3,) + ((p.shape[1], p.shape[2]) if layout == "S" else (p.shape[0], p.shape[1] // 4))

    def copies(ins, outs, send_sems, recv_sems):
        x, y, c, chips = _place()
        return [_remote(_shard_of_half(ins[t], layouts[t], _chip_id(chip)), outs[t].at[j],
                        send_sems.at[3 * t + j], recv_sems.at[3 * t + j], (*chip, c))
                for t in range(T) for j, chip in enumerate(chips)]

    def begin(*refs):
        for cp in copies(*refs):
            cp.start()

    def finish(*refs):
        for cp in copies(*refs):
            cp.wait()

    return Comm(pair_bf16, [jax.ShapeDtypeStruct(out_shape(p, k), BF16) for p, k in zip(pair_bf16, layouts)], {},
                3 * T, begin, None, finish)


def reduce_total(pair_f32, got, layout, layer, n_layers, previous, name):
    _, hr, C = got.shape
    tr = _row_tile(hr, C)
    nb = hr // tr

    def body(*refs):
        p_ref, g_ref, t_ref = refs[0], refs[1], refs[-1]
        t_ref[...] = ((p_ref[...] + g_ref[0].astype(F32)) + g_ref[1].astype(F32)) + g_ref[2].astype(F32)

    args = [pair_f32, got] + ([previous] if previous is not None else [])
    return pl.pallas_call(
        body, name=name, grid=(nb,),
        in_specs=[_own_block_spec(layout, tr, C), pl.BlockSpec((3, tr, C), lambda i: (0, i, 0))]
        + ([ANY] if previous is not None else []),
        out_specs=pl.BlockSpec((None, tr, C), lambda i: (layer, lax.axis_index("c") * nb + i, 0)),
        out_shape=jax.ShapeDtypeStruct((n_layers, 2 * hr, C), F32),
        input_output_aliases={2: 0} if previous is not None else {},
        compiler_params=_params(("parallel",)),
    )(*args)


def share_comm(reduced):
    T = len(reduced)

    def halves(outs, half):
        return [o.at[:, pl.ds(pl.multiple_of(half * (o.shape[1] // 2), 8), o.shape[1] // 2), :] for o in outs]

    def begin(_, outs, send_sems, recv_sems):
        x, y, c, _p = _place()
        for t, mine in enumerate(halves(outs, c)):
            _remote(mine, mine, send_sems.at[t], recv_sems.at[t], (x, y, 1 - c)).start()

    def finish(_, outs, send_sems, recv_sems):
        x, y, c, _p = _place()
        for t, (mine, theirs) in enumerate(zip(halves(outs, c), halves(outs, 1 - c))):
            _remote(mine, mine, send_sems.at[t], recv_sems.at[t], (x, y, 1 - c)).wait_send()
            _remote(theirs, theirs, send_sems.at[t], recv_sems.at[t], (x, y, 1 - c)).wait_recv()

    return Comm(reduced, [jax.ShapeDtypeStruct(r.shape, r.dtype) for r in reduced], {t: t for t in range(T)}, T,
                begin, None, finish)


def _round_up(n, m):
    return (n + m - 1) // m * m


def _heads_col(a16):
    L = a16.shape[0]
    return jnp.transpose(a16.reshape(L, HEADS // 2, 2), (1, 0, 2))


def _heads_row(a16):
    L = a16.shape[0]
    return jnp.transpose(a16.reshape(L, HEADS // 2, 2), (1, 2, 0))


def local_step(x, target, meta, norm_g, b_f, conv_w, conv_b, s5, s5_d, stager):
    S, D = x.shape
    depth = norm_g.shape[0]
    n_even, n_odd = b_f.shape[0], s5_d.shape[0]
    L = N_META + S
    Lp = _round_up(L, ROW_TILE)
    h = jnp.concatenate([meta, x, jnp.zeros((Lp - L, D), F32)], axis=0)
    conv_c = conv_w.shape[2]
    fg_block = 3 * conv_c // LANES
    saved = []

    def riding(tag, fn, *args):
        comm = stager.ride(tag)
        if comm is None and fn is matmul:
            return fn(*args, name=tag)
        outs, arrived = fn(*args, name=tag, comm=comm)
        stager.arrived(tag, arrived)
        return outs[0] if fn is matmul else outs

    for i in range(depth):
        g = norm_g[i]
        j = i // 2
        tag = "l%d_" % i
        w = stager.weights(i)
        st = {"h0": h, "w": w}
        if i % 2 == 0:
            u = rmsnorm_fwd(h, g[0:1], BF16, tag + "norm0")
            qkv = matmul([(u, w["w_qkv"])], "nn", BF16, tag + "qkv")
            rest = matmul([(u, w["w_rest"])], "nn", F32, tag + "rest")
            cgate = gate_fwd(rest, fg_block, b_f[j], tag + "gate")
            c16 = cgate[:, :HEADS]
            attn, lse = riding(tag + "attn", attn_fwd, qkv, _heads_col(c16), _heads_row(c16))
            convo = conv_fwd(rest, 0, conv_w[j], conv_b[j], tag + "conv")
            cat = jnp.concatenate([attn, convo], axis=1)
            m = matmul([(cat, w["w_o"])], "nn", F32, tag + "wo")
            st.update(u=u, qkv=qkv, rest=rest, c16=c16, lse=lse, cat=cat)
        else:
            p = s5[j]
            u = rmsnorm_fwd(h, g[0:1], F32, tag + "norm0")
            y, gact, xs = riding(tag + "s5", s5_fwd, u, p["bmat"], p["cmat"], p["tab"], s5_d[j])
            o1, o2, m = riding(tag + "glu", dual_matmul_act, gact, w["w_glu1"], w["w_glu2"], "glu", F32)
            st.update(u=u, y=y, gact=gact, xs=xs, o1=o1, o2=o2)
        h1 = rmsnorm_fwd(m, g[1:2], F32, tag + "norm1", residual=h)
        u2 = rmsnorm_fwd(h1, g[2:3], BF16, tag + "norm2")
        a, b, hact = riding(tag + "ffn_in", dual_matmul_act, u2, w["w_gate"], w["w_up"], "swiglu", BF16)
        f = riding(tag + "ffn_out", matmul, [(hact, w["w_down"])], "nn", F32)
        h = rmsnorm_fwd(f, g[3:4], F32, tag + "norm3", residual=h1)
        st.update(m=m, h1=h1, u2=u2, a=a, b=b, hact=hact, f=f)
        saved.append(st)

    dy, loss = loss_and_grad(h[N_META:L], target, "loss")
    dh = jnp.concatenate([jnp.zeros((N_META, D), F32), dy, jnp.zeros((Lp - L, D), F32)], axis=0)

    grads = {k: [None] * n_even for k in ("b_f", "conv_w", "conv_b")}
    grads.update({k: [None] * n_odd for k in ("s5_d", "s5_dB", "s5_dC", "s5_dlam")})
    grads["norm_g"] = [None] * depth

    for i in reversed(range(depth)):
        g = norm_g[i]
        j = i // 2
        tag = "l%d_b_" % i
        st = saved[i]
        w = st["w"]
        wg = {}
        df, dg3 = rmsnorm_bwd(st["f"], g[3:4], dh, BF16, tag + "norm3")
        wg["w_down"] = matmul([(st["hact"], df)], "tn", F32, tag + "dw_down")
        da, db = riding(tag + "ffn_act", ffn_bwd_act, df, w["w_down"], st["a"], st["b"])
        u2t = st["u2"].T
        wg["w_gate"] = matmul([(u2t, da)], "nn", F32, tag + "dw_gate")
        wg["w_up"] = matmul([(u2t, db)], "nn", F32, tag + "dw_up")
        stager.put_grads(i, wg)
        wg = {}
        du2 = riding(tag + "du2", matmul, [(da, w["w_gate"])], "nt", F32)
        du2 = matmul([(db, w["w_up"])], "nt", F32, tag + "du2_up", add=du2)
        dh1, dg2 = rmsnorm_bwd(st["h1"], g[2:3], du2, F32, tag + "norm2", add=dh)
        if i % 2 == 0:
            dm, dg1 = rmsnorm_bwd(st["m"], g[1:2], dh1, BF16, tag + "norm1")
            wg["w_o"] = matmul([(st["cat"], dm)], "tn", F32, tag + "dw_o")
            stager.put_grads(i, wg)
            dcat = riding(tag + "dcat", matmul, [(dm, w["w_o"])], "nt", F32)
            delta = attn_delta(dcat, st["cat"], tag + "delta")
            c16 = st["c16"]
            lse16 = jnp.transpose(st["lse"], (1, 0, 2)).reshape(Lp, HEADS)
            dq, dk, dv, dcq, dck = riding(tag + "attn", attn_bwd, st["qkv"], dcat, _heads_row(lse16),
                                          _heads_row(delta[:, :HEADS]), _heads_row(c16), _heads_col(c16))
            dc16 = (jnp.transpose(dcq, (2, 0, 1)).reshape(Lp, HEADS)
                    + jnp.transpose(dck, (1, 0, 2)).reshape(Lp, HEADS))
            dc = jnp.pad(dc16, ((0, 0), (0, LANES - HEADS)))
            dfg, dbf = gate_bwd(st["rest"], fg_block, b_f[j], dc, tag + "gate")
            dgb, dgc, dxc, dcw, dcb = conv_bwd(st["rest"], 0, conv_w[j], conv_b[j], dcat, ATTN_W // LANES,
                                               tag + "conv")
            dqkv = jnp.concatenate([dq, dk, dv], axis=1)
            drest = jnp.concatenate([dgb, dgc, dxc, dfg.astype(BF16)], axis=1)
            wg["w_qkv"] = matmul([(st["u"], dqkv)], "tn", F32, tag + "dw_qkv")
            wg["w_rest"] = matmul([(st["u"], drest)], "tn", F32, tag + "dw_rest")
            stager.put_grads(i, wg)
            du = riding(tag + "du_qkv", matmul, [(dqkv, w["w_qkv"])], "nt", F32)
            du_b = riding(tag + "du_rest", matmul, [(drest, w["w_rest"])], "nt", F32)
            grads["b_f"][j], grads["conv_w"][j], grads["conv_b"][j] = dbf, dcw, dcb
        else:
            p = s5[j]
            dmix, dg1 = rmsnorm_bwd(st["m"], g[1:2], dh1, F32, tag + "norm1")
            do1, do2 = glu_bwd_act(dmix, st["o1"], st["o2"], tag + "glu_act")
            wg["w_glu1"] = matmul([(st["gact"], do1)], "tn", F32, tag + "dw_glu1")
            wg["w_glu2"] = matmul([(st["gact"], do2)], "tn", F32, tag + "dw_glu2")
            dgact = matmul([(do1, w["w_glu1"]), (do2, w["w_glu2"])], "nt", F32, tag + "dgact")
            du, dC, dB, dlam, dd = riding(tag + "s5", s5_bwd, dgact, st["y"], st["u"], st["xs"], p["cmat_t"],
                                          p["bmat_t"], p["rtab"], s5_d[j])
            du_b = None
            grads["s5_dB"][j], grads["s5_dC"][j], grads["s5_dlam"][j], grads["s5_d"][j] = dB, dC, dlam, dd
        dh, dg0 = rmsnorm_bwd(st["h0"], g[0:1], du, F32, tag + "norm0", add=dh1, dy2=du_b)
        grads["norm_g"][i] = jnp.concatenate([dg0, dg1, dg2, dg3], axis=0)
        stager.put_grads(i, wg)

    grads["meta"] = dh[:N_META]
    return loss, dh[N_META:L], grads


def _packed_rows(shape):
    return _round_up(_round_up(math.prod(shape), LANES) // LANES, SUBLANES)


def _pack(arrs):
    rows = []
    for a in arrs:
        flat = a.reshape(-1).astype(F32)
        r = _packed_rows(a.shape)
        rows.append(jnp.pad(flat, (0, r * LANES - flat.shape[0])).reshape(r, LANES))
    return jnp.concatenate(rows, axis=0)


def _unpack(buf, shapes):
    buf = buf.reshape(-1, LANES)
    out, off = [], 0
    for s in shapes:
        r = _packed_rows(s)
        out.append(buf[off:off + r].reshape(-1)[:math.prod(s)].reshape(s))
        off += r
    return out


class _LayerWeights:
    def __init__(self, stager, layer):
        self.stager, self.layer = stager, layer

    def __getitem__(self, name):
        return self.stager.weight(self.layer, name)


class MeshStager:
    LAYOUT = {"ab_w_in": "S", "ab_w_o": "S", "s5_w_glu1": "S", "s5_w_glu2": "S",
              "ffn_w_gate": "C", "ffn_w_up": "C", "ffn_w_down": "S"}
    EVEN = ("ab_w_in", "ab_w_o", "ffn_w_gate", "ffn_w_up", "ffn_w_down")
    ODD = ("s5_w_glu1", "s5_w_glu2", "ffn_w_gate", "ffn_w_up", "ffn_w_down")

    def __init__(self, shards):
        self.shards = shards
        self.depth = depth = shards["ffn_w_gate"].shape[0]
        self.bufs = {}
        for i in range(depth):
            for k in self.keys(i):
                self.bufs[k, i] = cast_into_gathered(shards[k], self.index(k, i), self.LAYOUT[k],
                                                     "cast_%s_%d" % (k, i))
        self.grads, self.pairs, self.reduced, self.ready = {}, {}, {}, {}
        first = [("ab_w_in", 0)]
        plan = self.gather_plan = {"l0_attn": [it for it in self.stage(0) if it not in first]}
        for o in range(1, depth, 2):
            e = o - 1
            plan.setdefault("l%d_attn" % e, [])
            plan["l%d_ffn_in" % e] = [("s5_w_glu1", o), ("ffn_w_up", o)]
            plan["l%d_ffn_out" % e] = [("ffn_w_gate", o)]
            plan["l%d_s5" % o] = ([("ffn_w_down", o), ("s5_w_glu2", o)]
                                  + (self.mix(o + 1) if o + 1 < depth else []))
            if o + 1 < depth:
                plan["l%d_ffn_in" % o] = [("ffn_w_gate", o + 1)]
                plan["l%d_attn" % (o + 1)] = [("ffn_w_up", o + 1), ("ffn_w_down", o + 1)]
        self.swap_plan, self.exchange_plan = {}, {}
        for i in range(depth):
            above = self.mix(i + 1) if i + 1 < depth else []
            if above:
                self.swap_plan["l%d_b_ffn_act" % i] = above
            self.swap_plan["l%d_b_du2" % i] = self.ffn(i)
            self.exchange_plan["l%d_b_%s" % (i, "attn" if i % 2 == 0 else "s5")] = above + self.ffn(i)
        self.swap_plan["l0_b_dcat"] = [("ab_w_o", 0)]
        self.exchange_plan["l0_b_attn"].append(("ab_w_o", 0))
        self.swap_plan["l0_b_du_qkv"] = [("ab_w_in", 0)]
        self.exchange_plan["l0_b_du_rest"] = [("ab_w_in", 0)]
        self._store(first, comm_call("gather_first", self._gather(first, 0.5)))

    def keys(self, i):
        return self.EVEN if i % 2 == 0 else self.ODD

    def stage(self, i):
        return [(k, i) for k in self.keys(i)]

    def mix(self, i):
        return [(k, i) for k in self.keys(i) if not k.startswith("ffn")]

    @staticmethod
    def ffn(i):
        return [("ffn_w_gate", i), ("ffn_w_up", i), ("ffn_w_down", i)]

    @staticmethod
    def index(key, i):
        return i if key.startswith("ffn") else i // 2

    def _layouts(self, items):
        return [self.LAYOUT[k] for k, _ in items]

    def _gather(self, items, middle_frac):
        comm = gather_comm([self.bufs[it] for it in items], self._layouts(items))
        comm.middle_frac = middle_frac
        return comm

    def _store(self, items, bufs):
        for it, b in zip(items, bufs):
            self.bufs[it] = b

    def ride(self, tag):
        if tag in self.gather_plan:
            return self._gather(self.gather_plan[tag], 0.85 if tag == "l0_attn" else 0.7)
        if tag in self.swap_plan:
            items = self.swap_plan[tag]
            return swap_comm([self.grads[it] for it in items], self._layouts(items))
        if tag in self.exchange_plan:
            items = self.exchange_plan[tag]
            return exchange_comm([self.pairs[it][1] for it in items], self._layouts(items))
        return None

    def arrived(self, tag, outs):
        if tag in self.gather_plan:
            self._store(self.gather_plan[tag], outs)
        elif tag in self.swap_plan:
            self._pair_sums(self.swap_plan[tag], outs)
        elif tag in self.exchange_plan:
            self._totals(self.exchange_plan[tag], outs)

    def _pair_sums(self, items, received):
        for it, r in zip(items, received):
            self.pairs[it] = pair_sum(self.grads[it], r, self.LAYOUT[it[0]], "pair_sum_%s_%d" % it)

    def _totals(self, items, got):
        for it, g in zip(items, got):
            k, i = it
            self.reduced[k] = reduce_total(self.pairs[it][0], g, self.LAYOUT[k], self.index(k, i),
                                           self.shards[k].shape[0], self.reduced.get(k), "reduce_total_%s_%d" % it)

    def weights(self, i):
        return _LayerWeights(self, i)

    def weight(self, i, name):
        if (name, i) not in self.ready:
            if name in ("w_qkv", "w_rest"):
                b = self.bufs["ab_w_in", i]
                w_in = jnp.transpose(b, (1, 0, 2)).reshape(b.shape[1], 4 * b.shape[2])
                self.ready["w_qkv", i], self.ready["w_rest", i] = split_w_in(w_in)
            else:
                k = {"w_o": "ab_w_o", "w_glu1": "s5_w_glu1", "w_glu2": "s5_w_glu2"}.get(name, "ffn_" + name)
                b = self.bufs[k, i]
                self.ready[name, i] = b.reshape(4 * b.shape[1], b.shape[2]) if self.LAYOUT[k] == "S" else b
        return self.ready[name, i]

    def put_grads(self, i, wg):
        for k in self.keys(i):
            _, R, C = self.shards[k].shape
            name = {"ab_w_in": "w_qkv", "ab_w_o": "w_o", "s5_w_glu1": "w_glu1", "s5_w_glu2": "w_glu2"}.get(k, k[4:])
            if name not in wg:
                continue
            if k == "ab_w_in":
                dw = merge_dw_in(wg["w_qkv"], wg["w_rest"])
                self.grads[k, i] = jnp.transpose(dw.reshape(R, 4, C), (1, 0, 2))
            else:
                self.grads[k, i] = wg[name].reshape(4, R, C) if self.LAYOUT[k] == "S" else wg[name]

    def finish(self, beside):
        names = list(self.LAYOUT)
        outs = comm_call("share_reduced", merge_comms(share_comm([self.reduced[k] for k in names]), beside))
        return dict(zip(names, outs)), outs[len(names):]


def split_w_in(w_in):
    fg0 = 3 * ATTN_W
    w_rest = jnp.concatenate([w_in[:, fg0 + HEADS:], w_in[:, fg0:fg0 + HEADS],
                              jnp.zeros((w_in.shape[0], LANES - HEADS), w_in.dtype)], axis=1)
    return w_in[:, :fg0], w_rest


def merge_dw_in(dw_qkv, dw_rest):
    nqc = dw_rest.shape[1] - LANES
    return jnp.concatenate([dw_qkv, dw_rest[:, nqc:nqc + HEADS], dw_rest[:, :nqc]], axis=1)


def device_step(x, target, P, stager):
    D = x.shape[-1]
    n_even, n_odd = P["ab_b_f"].shape[0], P["s5_d"].shape[0]
    conv_c = P["ab_conv_b"].shape[1]
    b_f_pad = jnp.pad(P["ab_b_f"], ((0, 0), (0, LANES - HEADS))).reshape(n_even, 1, LANES)

    s5, s5_vjps = [], []
    for j in range(n_odd):
        disc, vjp = jax.vjp(_s5_discretize, P["s5_a_re"][j], P["s5_a_im"][j], P["s5_log_step"][j],
                            P["s5_b_re"][j], P["s5_b_im"][j])
        lb_re, lb_im, bb_re, bb_im = disc
        tab, rtab = _s5_tables(lb_re, lb_im)
        bmat, cmat = _s5_block_mats(bb_re, bb_im, P["s5_c_re"][j], P["s5_c_im"][j])
        s5.append(dict(tab=tab, rtab=rtab, bmat=bmat.astype(BF16), cmat=cmat.astype(BF16),
                       bmat_t=jnp.transpose(bmat, (0, 2, 1)).astype(BF16),
                       cmat_t=jnp.transpose(cmat, (0, 2, 1)).astype(BF16)))
        s5_vjps.append(vjp)

    loss, grad_x, G = local_step(
        x, target, P["meta_tokens"], P["norm_g"], b_f_pad, P["ab_conv_w"],
        P["ab_conv_b"].reshape(n_even, 1, conv_c), s5, P["s5_d"].reshape(n_odd, 1, D), stager)

    out = {
        "meta_tokens": G["meta"],
        "norm_g": jnp.stack(G["norm_g"]),
        "ab_b_f": jnp.stack([b[0, :HEADS] for b in G["b_f"]]),
        "ab_conv_w": jnp.stack(G["conv_w"]),
        "ab_conv_b": jnp.stack([b[0] for b in G["conv_b"]]),
        "s5_d": jnp.stack([d[0] for d in G["s5_d"]]),
    }
    s5g = {k: [] for k in ("s5_a_re", "s5_a_im", "s5_log_step", "s5_b_re", "s5_b_im", "s5_c_re", "s5_c_im")}
    for j in range(n_odd):
        dbb_re, dbb_im, dc_re, dc_im, dl_re, dl_im = _s5_unblock(G["s5_dB"][j], G["s5_dC"][j], G["s5_dlam"][j])
        da_re, da_im, dls, db_re, db_im = s5_vjps[j]((dl_re, dl_im, dbb_re, dbb_im))
        for k, val in zip(s5g, (da_re, da_im, dls, db_re, db_im, dc_re, dc_im)):
            s5g[k].append(val)
    out.update({k: jnp.stack(v) for k, v in s5g.items()})
    return loss, grad_x, out


def kernel(x, meta_tokens, norm_g, ab_w_in, ab_b_f, ab_conv_w, ab_conv_b, ab_w_o, s5_a_re, s5_a_im, s5_log_step, s5_b_re, s5_b_im, s5_c_re, s5_c_im, s5_d, s5_w_glu1, s5_w_glu2, ffn_w_gate, ffn_w_up, ffn_w_down, loss_target, m_meta_tokens, m_norm_g, m_ab_w_in, m_ab_b_f, m_ab_conv_w, m_ab_conv_b, m_ab_w_o, m_s5_a_re, m_s5_a_im, m_s5_log_step, m_s5_b_re, m_s5_b_im, m_s5_c_re, m_s5_c_im, m_s5_d, m_s5_w_glu1, m_s5_w_glu2, m_ffn_w_gate, m_ffn_w_up, m_ffn_w_down, v_meta_tokens, v_norm_g, v_ab_w_in, v_ab_b_f, v_ab_conv_w, v_ab_conv_b, v_ab_w_o, v_s5_a_re, v_s5_a_im, v_s5_log_step, v_s5_b_re, v_s5_b_im, v_s5_c_re, v_s5_c_im, v_s5_d, v_s5_w_glu1, v_s5_w_glu2, v_ffn_w_gate, v_ffn_w_up, v_ffn_w_down):
    names = ["meta_tokens", "norm_g", "ab_w_in", "ab_b_f", "ab_conv_w", "ab_conv_b", "ab_w_o", "s5_a_re", "s5_a_im",
             "s5_log_step", "s5_b_re", "s5_b_im", "s5_c_re", "s5_c_im", "s5_d", "s5_w_glu1", "s5_w_glu2",
             "ffn_w_gate", "ffn_w_up", "ffn_w_down"]
    W = dict(zip(names, [meta_tokens, norm_g, ab_w_in, ab_b_f, ab_conv_w, ab_conv_b, ab_w_o, s5_a_re, s5_a_im,
                         s5_log_step, s5_b_re, s5_b_im, s5_c_re, s5_c_im, s5_d, s5_w_glu1, s5_w_glu2,
                         ffn_w_gate, ffn_w_up, ffn_w_down]))
    Mo = dict(zip(names, [m_meta_tokens, m_norm_g, m_ab_w_in, m_ab_b_f, m_ab_conv_w, m_ab_conv_b, m_ab_w_o, m_s5_a_re,
                          m_s5_a_im, m_s5_log_step, m_s5_b_re, m_s5_b_im, m_s5_c_re, m_s5_c_im, m_s5_d, m_s5_w_glu1,
                          m_s5_w_glu2, m_ffn_w_gate, m_ffn_w_up, m_ffn_w_down]))
    Vo = dict(zip(names, [v_meta_tokens, v_norm_g, v_ab_w_in, v_ab_b_f, v_ab_conv_w, v_ab_conv_b, v_ab_w_o, v_s5_a_re,
                          v_s5_a_im, v_s5_log_step, v_s5_b_re, v_s5_b_im, v_s5_c_re, v_s5_c_im, v_s5_d, v_s5_w_glu1,
                          v_s5_w_glu2, v_ffn_w_gate, v_ffn_w_up, v_ffn_w_down]))
    D = x.shape[-1]
    n_even, n_odd, depth = ab_w_in.shape[0], s5_w_glu1.shape[0], ffn_w_gate.shape[0]
    chip = 2 * lax.axis_index("x") + lax.axis_index("y")

    big = list(MeshStager.LAYOUT)
    stager = MeshStager({k: W[k] for k in big})
    g_meta, g_norm, g_convw, g_s5d = allgather_small([meta_tokens, norm_g, ab_conv_w, s5_d])
    full = {k: W[k] for k in names if k not in big}
    full["meta_tokens"] = jnp.transpose(g_meta, (1, 0, 2)).reshape(N_META, D)
    full["norm_g"] = jnp.transpose(g_norm, (1, 2, 0, 3)).reshape(depth, 4, D)
    full["ab_conv_w"] = jnp.transpose(g_convw, (1, 2, 0, 3)).reshape(n_even, CONV_K, -1)
    full["s5_d"] = jnp.transpose(g_s5d, (1, 0, 2)).reshape(n_odd, D)

    loss, grad_x, G = device_step(x[0], loss_target[0], full, stager)
    small_w = [k for k in names if k not in big]
    small_names = ["loss"] + small_w
    G["loss"] = loss
    packed = _pack([G[k] for k in small_names])
    pair = small_pair_sum(packed, comm_call("small_swap", small_swap_comm(packed))[0], "small_pair_sum")
    reduced, (got,) = stager.finish(small_exchange_comm(pair))
    total = small_chip_sum(pair, got, "small_chip_sum")

    grad, delta, new_m, new_v = {}, {}, {}, {}
    for k in big:
        delta[k], new_m[k], new_v[k] = adamw(W[k], reduced[k], Mo[k], Vo[k], "adamw_" + k)
        grad[k] = reduced[k]
    summed = dict(zip(small_names, _unpack(total, [G[k].shape for k in small_names])))
    loss_out = summed["loss"].reshape(())
    for k in ("meta_tokens", "norm_g", "ab_conv_w", "s5_d"):
        n_last = W[k].shape[-1]
        summed[k] = lax.dynamic_slice_in_dim(summed[k], chip * n_last, n_last, axis=summed[k].ndim - 1)
    shapes = [W[k].shape for k in small_w]
    d_s, m_s, v_s = adamw(_pack([W[k] for k in small_w])[None], _pack([summed[k] for k in small_w])[None],
                          _pack([Mo[k] for k in small_w])[None], _pack([Vo[k] for k in small_w])[None], "adamw_small")
    delta.update(zip(small_w, _unpack(d_s, shapes)))
    new_m.update(zip(small_w, _unpack(m_s, shapes)))
    new_v.update(zip(small_w, _unpack(v_s, shapes)))
    grad.update({k: summed[k] for k in small_w})

    return (loss_out, grad_x[None], *[grad[k] for k in names], *[delta[k] for k in names],
            *[new_m[k] for k in names], *[new_v[k] for k in names])
```

```python
import functools
import math

import jax
import jax.numpy as jnp
from jax import lax
from jax.experimental import pallas as pl
from jax.experimental.pallas import tpu as pltpu

F32 = jnp.float32
BF16 = jnp.bfloat16

N_META = 16
HEADS = 16
HEAD_DIM = 64
ATTN_W = HEADS * HEAD_DIM
CONV_K = 3
S5_GROUP = 16
S5_STATE = 64
S5_MIN_DECAY = 1e-4
NORM_EPS = 1e-6
ADAM_LR = 0.001
ADAM_B1 = 0.9
ADAM_B2 = 0.999
ADAM_EPS = 1e-08
ADAM_WD = 0.01
ADAM_STEP = 10

LANES = 128
SUBLANES = 8
VMEM_LIMIT = 56 * 1024 * 1024
VMEM_TILE_BUDGET = 34 * 1024 * 1024
ROW_TILE = 384
ATTN_ROWS = 384
S5_BLOCK_GROUPS = LANES // S5_GROUP
S5_BLOCK_STATES = S5_BLOCK_GROUPS * S5_STATE
NEG_BIG = -1e30

MESH = pl.DeviceIdType.MESH
ANY = pl.BlockSpec(memory_space=pl.ANY)
VMEM_SPEC = pl.BlockSpec(memory_space=pltpu.VMEM)


def _params(sem=None):
    return pltpu.CompilerParams(dimension_semantics=sem, vmem_limit_bytes=VMEM_LIMIT)


def _div_tile(n, prefs):
    for p in prefs:
        if n % p == 0:
            return p
    return n


def _row_tile(rows, cols, itemsize=4, limit=2 * 1024 * 1024):
    for p in (512, 256, 128, 64, 32, 16):
        if rows % p == 0 and p * cols * itemsize <= limit:
            return p
    return 16 if rows % 16 == 0 else rows


def _tile_cands(n):
    c = [d for d in range(LANES, min(n, 2048) + 1, LANES) if n % d == 0]
    if not c or n <= 2048 and n not in c:
        c.append(n)
    return sorted(set(c), reverse=True)


def _mm_tiles(M, N, K, a_bytes, b_bytes, o_bytes, npairs):
    best = None
    for tk in sorted(set(_tile_cands(K) + [K]), reverse=True):
        for tm in _tile_cands(M):
            for tn in _tile_cands(N):
                mem = npairs * 2 * (tm * tk * a_bytes + tk * tn * b_bytes) + 2 * tm * tn * o_bytes + tm * tn * 4
                mem += npairs * ((tm * tk * 2 if a_bytes == 4 else 0) + (tk * tn * 2 if b_bytes == 4 else 0))
                if mem > VMEM_TILE_BUDGET:
                    continue
                key = (tk == K and tm >= 3 * LANES and tn >= 4 * LANES, tm * tn * tk, tk, tn)
                if best is None or key > best[0]:
                    best = (key, (tm, tn, tk))
    assert best is not None, (M, N, K)
    return best[1]


class Comm:
    def __init__(self, operands, out_shapes, aliases, n_sems, begin, middle=None, finish=None, middle_frac=0.5):
        self.operands, self.out_shapes, self.aliases, self.n_sems = list(operands), list(out_shapes), aliases, n_sems
        self.begin, self.middle, self.finish, self.middle_frac = begin, middle, finish, middle_frac


class _Shifted:
    def __init__(self, sems, off):
        self.sems, self.off = sems, off

    @property
    def at(self):
        return self

    def __getitem__(self, i):
        return self.sems.at[self.off + i]


def merge_comms(a, b):
    assert a.middle is None and b.middle is None
    na_in, na_out = len(a.operands), len(a.out_shapes)

    def both(stage):
        def run(ins, outs, send_sems, recv_sems):
            getattr(a, stage)(ins[:na_in], outs[:na_out], send_sems, recv_sems)
            getattr(b, stage)(ins[na_in:], outs[na_out:], _Shifted(send_sems, a.n_sems), _Shifted(recv_sems, a.n_sems))
        return run

    aliases = dict(a.aliases)
    aliases.update({na_in + i: na_out + o for i, o in b.aliases.items()})
    return Comm(a.operands + b.operands, a.out_shapes + b.out_shapes, aliases, a.n_sems + b.n_sems,
                both("begin"), None, both("finish"))


def carrier_call(body, name, grid, in_specs, out_specs, out_shape, scratch_shapes, args, comm, semantics):
    n_in, n_out = len(args), len(out_shape)
    if comm is None:
        outs = pl.pallas_call(body, name=name, grid=grid, in_specs=in_specs, out_specs=out_specs, out_shape=out_shape,
                              scratch_shapes=scratch_shapes, compiler_params=_params(semantics))(*args)
        return list(outs), []
    ci, co = len(comm.operands), len(comm.out_shapes)
    total = math.prod(grid)
    middle_at = min(total - 1, max(0, int(total * comm.middle_frac)))

    def carried(*refs):
        ins, cins = refs[:n_in], refs[n_in:n_in + ci]
        outs = refs[n_in + ci:n_in + ci + n_out]
        couts = refs[n_in + ci + n_out:n_in + ci + n_out + co]
        scratch, (send_sems, recv_sems) = refs[n_in + ci + n_out + co:-2], refs[-2:]
        step = 0
        for d, size in enumerate(grid):
            step = step * size + pl.program_id(d)

        @pl.when(step == 0)
        def _():
            comm.begin(cins, couts, send_sems, recv_sems)

        if comm.middle is not None:
            @pl.when(step == middle_at)
            def _():
                comm.middle(cins, couts, send_sems, recv_sems)

        body(*ins, *outs, *scratch)

        @pl.when(step == total - 1)
        def _():
            comm.finish(cins, couts, send_sems, recv_sems)

    outs = pl.pallas_call(
        carried, name=name, grid=grid,
        in_specs=list(in_specs) + [ANY] * ci, out_specs=list(out_specs) + [ANY] * co,
        out_shape=list(out_shape) + comm.out_shapes,
        scratch_shapes=list(scratch_shapes) + [pltpu.SemaphoreType.DMA((comm.n_sems,)),
                                                pltpu.SemaphoreType.DMA((comm.n_sems,))],
        input_output_aliases={n_in + i: n_out + o for i, o in comm.aliases.items()},
        compiler_params=pltpu.CompilerParams(dimension_semantics=("arbitrary",) * len(grid),
                                             vmem_limit_bytes=VMEM_LIMIT, has_side_effects=True),
    )(*args, *comm.operands)
    return list(outs[:n_out]), list(outs[n_out:])


def comm_call(name, comm):
    ci = len(comm.operands)

    def body(*refs):
        cins, couts = refs[:ci], refs[ci:ci + len(comm.out_shapes)]
        send_sems, recv_sems = refs[-2:]
        comm.begin(cins, couts, send_sems, recv_sems)
        if comm.middle is not None:
            comm.middle(cins, couts, send_sems, recv_sems)
        comm.finish(cins, couts, send_sems, recv_sems)

    return pl.pallas_call(
        body, name=name, in_specs=[ANY] * ci, out_specs=[ANY] * len(comm.out_shapes), out_shape=comm.out_shapes,
        input_output_aliases=dict(comm.aliases),
        scratch_shapes=[pltpu.SemaphoreType.DMA((comm.n_sems,)), pltpu.SemaphoreType.DMA((comm.n_sems,))],
        compiler_params=pltpu.CompilerParams(has_side_effects=True),
    )(*comm.operands)


_DIMS ={"nn": (((1,), (0,)), ((), ())), "nt": (((1,), (1,)), ((), ())), "tn": (((0,), (0,)), ((), ()))}


def matmul(pairs, kind, out_dtype, name, comm=None, add=None):
    a0, b0 = pairs[0]
    if kind == "nn":
        (M, K), N = a0.shape, b0.shape[1]
    elif kind == "nt":
        (M, K), N = a0.shape, b0.shape[0]
    else:
        (K, M), N = a0.shape, b0.shape[1]
    tm, tn, tk = _mm_tiles(M, N, K, a0.dtype.itemsize, b0.dtype.itemsize, jnp.dtype(out_dtype).itemsize, len(pairs))
    nk = K // tk
    dims = _DIMS[kind]
    npairs = len(pairs)
    n_in = 2 * npairs + (add is not None)

    def body(*refs):
        ins, o_ref = refs[:2 * npairs], refs[n_in]
        part = None
        for p in range(npairs):
            d = lax.dot_general(ins[2 * p][...].astype(BF16), ins[2 * p + 1][...].astype(BF16), dims,
                                preferred_element_type=F32)
            part = d if part is None else part + d

        def finish(total):
            if add is not None:
                total = total + refs[2 * npairs][...]
            o_ref[...] = total.astype(o_ref.dtype)

        if nk == 1:
            finish(part)
        else:
            acc_ref = refs[n_in + 1]
            k = pl.program_id(2)

            @pl.when(k == 0)
            def _():
                acc_ref[...] = part

            @pl.when(k > 0)
            def _():
                acc_ref[...] += part

            @pl.when(k == nk - 1)
            def _():
                finish(acc_ref[...])

    if kind == "nn":
        a_blk, a_map = (tm, tk), lambda j, i, k: (i, k)
        b_blk, b_map = (tk, tn), lambda j, i, k: (k, j)
    elif kind == "nt":
        a_blk, a_map = (tm, tk), lambda j, i, k: (i, k)
        b_blk, b_map = (tn, tk), lambda j, i, k: (j, k)
    else:
        a_blk, a_map = (tk, tm), lambda j, i, k: (k, i)
        b_blk, b_map = (tk, tn), lambda j, i, k: (k, j)
    o_spec = pl.BlockSpec((tm, tn), lambda j, i, k: (i, j))
    (out,), arrived = carrier_call(
        body, name, (N // tn, M // tm, nk),
        [pl.BlockSpec(a_blk, a_map), pl.BlockSpec(b_blk, b_map)] * npairs + ([o_spec] if add is not None else []),
        [o_spec], [jax.ShapeDtypeStruct((M, N), out_dtype)],
        [] if nk == 1 else [pltpu.VMEM((tm, tn), F32)],
        [t for ab in pairs for t in ab] + ([add] if add is not None else []), comm,
        ("parallel", "parallel", "arbitrary"))
    return out if comm is None else ([out], arrived)


def _sigmoid(x):
    return 1.0 / (1.0 + jnp.exp(-x))

def dual_matmul_act(x, w1, w2, act, out_dtype, name, comm=None):
    M, K = x.shape
    N = w1.shape[-1]
    tm = _div_tile(M, (ROW_TILE,))
    tn = _div_tile(N, (1408, 1024, 512, 256, 128))

    def body(x_ref, w1_ref, w2_ref, o1_ref, o2_ref, out_ref):
        xv = x_ref[...]
        o1 = jnp.dot(xv, w1_ref[...], preferred_element_type=F32)
        o2 = jnp.dot(xv, w2_ref[...], preferred_element_type=F32)
        o1_ref[...] = o1.astype(BF16)
        o2_ref[...] = o2.astype(BF16)
        if act == "swiglu":
            out = o1 * _sigmoid(o1) * o2
        else:
            out = o1 * _sigmoid(o2)
        out_ref[...] = out.astype(out_ref.dtype)

    w_spec = pl.BlockSpec((K, tn), lambda j, i: (0, j))
    o_spec = pl.BlockSpec((tm, tn), lambda j, i: (i, j))
    return carrier_call(
        body, name, (N // tn, M // tm), [pl.BlockSpec((tm, K), lambda j, i: (i, 0)), w_spec, w_spec],
        [o_spec, o_spec, o_spec],
        [jax.ShapeDtypeStruct((M, N), BF16), jax.ShapeDtypeStruct((M, N), BF16),
         jax.ShapeDtypeStruct((M, N), out_dtype)], [], (x, w1, w2), comm, ("parallel", "parallel"))


def ffn_bwd_act(df, wd, a, b, name, comm=None):
    M, K = df.shape
    N = wd.shape[0]
    tm = _div_tile(M, (ROW_TILE,))
    tn = _div_tile(N, (1408, 1024, 512, 256, 128))

    def body(df_ref, wd_ref, a_ref, b_ref, da_ref, db_ref):
        dh = lax.dot_general(df_ref[...], wd_ref[...], _DIMS["nt"], preferred_element_type=F32)
        av = a_ref[...].astype(F32)
        bv = b_ref[...].astype(F32)
        sig = _sigmoid(av)
        silu = av * sig
        da_ref[...] = (dh * bv * (sig + silu * (1.0 - sig))).astype(BF16)
        db_ref[...] = (dh * silu).astype(BF16)

    t_spec = pl.BlockSpec((tm, tn), lambda j, i: (i, j))
    return carrier_call(
        body, name, (N // tn, M // tm),
        [pl.BlockSpec((tm, K), lambda j, i: (i, 0)), pl.BlockSpec((tn, K), lambda j, i: (j, 0)), t_spec, t_spec],
        [t_spec, t_spec], [jax.ShapeDtypeStruct((M, N), BF16)] * 2, [], (df, wd, a, b), comm,
        ("parallel", "parallel"))


def glu_bwd_act(dout, o1, o2, name):
    M, N = dout.shape
    tm = _div_tile(M, (ROW_TILE,))

    def body(d_ref, o1_ref, o2_ref, d1_ref, d2_ref):
        d = d_ref[...].astype(F32)
        sig = _sigmoid(o2_ref[...].astype(F32))
        d1_ref[...] = (d * sig).astype(BF16)
        d2_ref[...] = (d * o1_ref[...].astype(F32) * sig * (1.0 - sig)).astype(BF16)

    spec = pl.BlockSpec((tm, N), lambda i: (i, 0))
    return pl.pallas_call(
        body, name=name, grid=(M // tm,), in_specs=[spec] * 3, out_specs=[spec] * 2,
        out_shape=[jax.ShapeDtypeStruct((M, N), BF16)] * 2,
        compiler_params=_params(("parallel",)),
    )(dout, o1, o2)


def rmsnorm_fwd(x, g, out_dtype, name, residual=None):
    L, D = x.shape
    tr = _div_tile(L, (ROW_TILE,))
    has_res = residual is not None

    def body(*refs):
        x_ref, g_ref = refs[0], refs[1]
        o_ref = refs[-1]
        xv = x_ref[...]
        r = lax.rsqrt(jnp.mean(xv * xv, axis=-1, keepdims=True) + NORM_EPS)
        y = xv * r * g_ref[...]
        if has_res:
            y = refs[2][...] + y
        o_ref[...] = y.astype(o_ref.dtype)

    row = pl.BlockSpec((tr, D), lambda i: (i, 0))
    gsp = pl.BlockSpec((1, D), lambda i: (0, 0))
    args = (x, g) + ((residual,) if has_res else ())
    return pl.pallas_call(
        body, name=name, grid=(L // tr,), in_specs=[row, gsp] + ([row] if has_res else []), out_specs=row,
        out_shape=jax.ShapeDtypeStruct((L, D), out_dtype), compiler_params=_params(("parallel",)),
    )(*args)


def rmsnorm_bwd(x, g, dy, out_dtype, name, add=None, dy2=None):
    L, D = x.shape
    tr = _div_tile(L, (ROW_TILE,))
    has_add = add is not None
    has_dy2 = dy2 is not None

    def body(*refs):
        x_ref, g_ref, dy_ref = refs[0], refs[1], refs[2]
        dx_ref, dg_ref = refs[-2], refs[-1]
        xv = x_ref[...]
        dyv = dy_ref[...].astype(F32)
        if has_dy2:
            dyv = dyv + refs[3][...].astype(F32)
        r = lax.rsqrt(jnp.mean(xv * xv, axis=-1, keepdims=True) + NORM_EPS)
        t = dyv * g_ref[...]
        dx = r * t - xv * (r * r * r) * jnp.mean(xv * t, axis=-1, keepdims=True)
        if has_add:
            dx = refs[3 + has_dy2][...] + dx
        dx_ref[...] = dx.astype(dx_ref.dtype)
        dgp = jnp.sum(dyv * xv * r, axis=0, keepdims=True)

        @pl.when(pl.program_id(0) == 0)
        def _():
            dg_ref[...] = dgp

        @pl.when(pl.program_id(0) > 0)
        def _():
            dg_ref[...] += dgp

    row = pl.BlockSpec((tr, D), lambda i: (i, 0))
    gsp = pl.BlockSpec((1, D), lambda i: (0, 0))
    args = (x, g, dy) + ((dy2,) if has_dy2 else ()) + ((add,) if has_add else ())
    return pl.pallas_call(
        body, name=name, grid=(L // tr,), in_specs=[row, gsp] + [row] * (len(args) - 2),
        out_specs=[row, gsp],
        out_shape=[jax.ShapeDtypeStruct((L, D), out_dtype), jax.ShapeDtypeStruct((1, D), F32)],
        compiler_params=_params(("arbitrary",)),
    )(*args)


def _gate_z(fg_ref, b_ref):
    return fg_ref[...] + b_ref[...]


def gate_fwd(fg_src, col_block, b, name):
    L = fg_src.shape[0]
    T = _div_tile(L, (ROW_TILE,))

    def body(fg_ref, b_ref, c_ref, carry):
        @pl.when(pl.program_id(0) == 0)
        def _():
            carry[...] = jnp.zeros_like(carry)

        z = _gate_z(fg_ref, b_ref)
        logf = jnp.minimum(z, 0.0) - jnp.log(1.0 + jnp.exp(-jnp.abs(z)))
        tri = (lax.broadcasted_iota(jnp.int32, (T, T), 1) <= lax.broadcasted_iota(jnp.int32, (T, T), 0)).astype(F32)
        c = jnp.dot(tri, logf, precision=lax.Precision.HIGHEST, preferred_element_type=F32) + carry[...]
        c_ref[...] = c
        carry[...] = c[T - 1:T, :]

    return pl.pallas_call(
        body, name=name, grid=(L // T,),
        in_specs=[pl.BlockSpec((T, LANES), lambda i: (i, col_block)), pl.BlockSpec((1, LANES), lambda i: (0, 0))],
        out_specs=pl.BlockSpec((T, LANES), lambda i: (i, 0)),
        out_shape=jax.ShapeDtypeStruct((L, LANES), F32),
        scratch_shapes=[pltpu.VMEM((1, LANES), F32)],
        compiler_params=_params(("arbitrary",)),
    )(fg_src, b)


def gate_bwd(fg_src, col_block, b, dc, name):
    L = fg_src.shape[0]
    T = _div_tile(L, (ROW_TILE,))
    nb = L // T

    def body(fg_ref, b_ref, dc_ref, dfg_ref, db_ref, carry):
        @pl.when(pl.program_id(0) == 0)
        def _():
            carry[...] = jnp.zeros_like(carry)
            db_ref[...] = jnp.zeros_like(db_ref)

        z = _gate_z(fg_ref, b_ref)
        dcv = dc_ref[...]
        tri = (lax.broadcasted_iota(jnp.int32, (T, T), 1) >= lax.broadcasted_iota(jnp.int32, (T, T), 0)).astype(F32)
        dlogf = jnp.dot(tri, dcv, precision=lax.Precision.HIGHEST, preferred_element_type=F32) + carry[...]
        dfg = dlogf * _sigmoid(-z)
        dfg_ref[...] = dfg
        db_ref[...] += jnp.sum(dfg, axis=0, keepdims=True)
        carry[...] = dlogf[0:1, :]

    return pl.pallas_call(
        body, name=name, grid=(nb,),
        in_specs=[pl.BlockSpec((T, LANES), lambda i: (nb - 1 - i, col_block)),
                  pl.BlockSpec((1, LANES), lambda i: (0, 0)),
                  pl.BlockSpec((T, LANES), lambda i: (nb - 1 - i, 0))],
        out_specs=[pl.BlockSpec((T, LANES), lambda i: (nb - 1 - i, 0)), pl.BlockSpec((1, LANES), lambda i: (0, 0))],
        out_shape=[jax.ShapeDtypeStruct((L, LANES), F32), jax.ShapeDtypeStruct((1, LANES), F32)],
        scratch_shapes=[pltpu.VMEM((1, LANES), F32)],
        compiler_params=_params(("arbitrary",)),
    )(fg_src, b, dc)


def attn_fwd(proj, cq_col, ck_row, name, comm=None):
    L = proj.shape[0]
    T = _div_tile(L, (ROW_TILE,))
    nq = L // T
    npair = HEADS // 2
    scale = HEAD_DIM ** -0.5
    SUB = ATTN_ROWS
    nsub = T // SUB

    def body(q_ref, k_ref, v_ref, cq_ref, ck_ref, o_ref, lse_ref):
        qb = pl.program_id(1)
        rows = [slice(r * SUB, (r + 1) * SUB) for r in range(nsub)]
        head1 = lax.broadcasted_iota(jnp.int32, (SUB, LANES), 1) >= HEAD_DIM
        qs = [[jnp.where(head1 == (h == 1), q_ref[rs, :] * scale, 0.0).astype(BF16) for rs in rows] for h in range(2)]
        cqs = [[cq_ref[0, rs, h:h + 1] for rs in rows] for h in range(2)]

        def logits(kb):
            ks = pl.multiple_of(kb * T, T)
            k = k_ref[pl.ds(ks, T), :]
            return tuple(lax.dot_general(qs[h][r], k, _DIMS["nt"], preferred_element_type=F32) + cqs[h][r]
                         - ck_ref[0, h:h + 1, pl.ds(ks, T)] for h in range(2) for r in range(nsub))

        def softmax_step(kb, s_all, carry, masked):
            ks = pl.multiple_of(kb * T, T)
            v = v_ref[pl.ds(ks, T), :]
            lane = lax.broadcasted_iota(jnp.int32, (T, LANES), 1)
            new = []
            for h in range(2):
                vh = jnp.where(lane == spare[h], 1.0, v).astype(BF16)
                for r in range(nsub):
                    m, acc = carry[h * nsub + r]
                    s, vr = s_all[h * nsub + r], vh
                    if masked:
                        n = (r + 1) * SUB
                        s, vr = s[:, :n], vh[:n]
                        keep = (lax.broadcasted_iota(jnp.int32, (SUB, n), 1)
                                <= lax.broadcasted_iota(jnp.int32, (SUB, n), 0) + r * SUB)
                        s = jnp.where(keep, s, NEG_BIG)
                    m_new = jnp.maximum(m, jnp.max(s, axis=1, keepdims=True))
                    p = jnp.exp(s - m_new)
                    acc = jnp.exp(m - m_new) * acc + jnp.dot(p.astype(BF16), vr, preferred_element_type=F32)
                    new.append((m_new, acc))
            return tuple(new)

        def step(kb, state):
            s_all, carry = state
            s_next = logits(kb + 1)
            return s_next, softmax_step(kb, s_all, carry, False)

        spare = (HEAD_DIM, 0)
        one = (jnp.full((SUB, 1), NEG_BIG, F32), jnp.zeros((SUB, LANES), F32))
        s_all, carry = lax.fori_loop(0, qb, step, (logits(0), (one,) * (2 * nsub)))
        carry = softmax_step(qb, s_all, carry, True)
        out, lse = [], []
        for h in range(2):
            chains = carry[h * nsub:(h + 1) * nsub]
            ls = [acc[:, spare[h]:spare[h] + 1] for _, acc in chains]
            out.append(jnp.concatenate([acc / l for (_, acc), l in zip(chains, ls)], axis=0))
            lse.append(jnp.concatenate([m + jnp.log(l) for (m, _), l in zip(chains, ls)], axis=0))
        o_ref[...] = jnp.where(lax.broadcasted_iota(jnp.int32, (T, LANES), 1) >= HEAD_DIM, out[1], out[0]
                               ).astype(o_ref.dtype)
        lse_ref[0] = jnp.concatenate(lse, axis=1)

    return carrier_call(
        body, name, (npair, nq),
        [pl.BlockSpec((T, LANES), lambda p, i: (i, p)),
         pl.BlockSpec((L, LANES), lambda p, i: (0, npair + p)),
         pl.BlockSpec((L, LANES), lambda p, i: (0, 2 * npair + p)),
         pl.BlockSpec((1, T, 2), lambda p, i: (p, i, 0)),
         pl.BlockSpec((1, 2, L), lambda p, i: (p, 0, 0))],
        [pl.BlockSpec((T, LANES), lambda p, i: (i, p)), pl.BlockSpec((1, T, 2), lambda p, i: (p, i, 0))],
        [jax.ShapeDtypeStruct((L, ATTN_W), BF16), jax.ShapeDtypeStruct((npair, L, 2), F32)],
        [], (proj, proj, proj, cq_col, ck_row), comm, ("parallel", "parallel"))


def attn_delta(dcat, cat, name):
    L = dcat.shape[0]
    T = _div_tile(L, (ROW_TILE,))

    def body(do_ref, o_ref, d_ref):
        prod = do_ref[...] * o_ref[...].astype(F32)
        sel = (lax.broadcasted_iota(jnp.int32, (ATTN_W, LANES), 0) // HEAD_DIM
               == lax.broadcasted_iota(jnp.int32, (ATTN_W, LANES), 1)).astype(F32)
        d_ref[...] = jnp.dot(prod, sel, precision=lax.Precision.HIGHEST, preferred_element_type=F32)

    return pl.pallas_call(
        body, name=name, grid=(L // T,),
        in_specs=[pl.BlockSpec((T, ATTN_W), lambda i: (i, 0)), pl.BlockSpec((T, ATTN_W), lambda i: (i, 0))],
        out_specs=pl.BlockSpec((T, LANES), lambda i: (i, 0)),
        out_shape=jax.ShapeDtypeStruct((L, LANES), F32),
        compiler_params=_params(("parallel",)),
    )(dcat, cat)


def attn_bwd(proj, dcat, lse_row, delta_row, cq_row, ck_col, name, comm=None):
    L = proj.shape[0]
    T = _div_tile(L, (ROW_TILE,))
    nb = L // T
    npair = HEADS // 2
    scale = HEAD_DIM ** -0.5

    def body(q_ref, k_ref, v_ref, do_ref, lse_ref, dl_ref, cq_ref, ck_ref,
             dq_ref, dk_ref, dv_ref, dcq_ref, dck_ref, dq_acc, dcq_acc):
        kb = pl.program_id(1)

        @pl.when(kb == 0)
        def _():
            dq_acc[...] = jnp.zeros_like(dq_acc)
            dcq_acc[...] = jnp.zeros_like(dcq_acc)

        head1 = lax.broadcasted_iota(jnp.int32, (T, LANES), 1) >= HEAD_DIM
        ks = [jnp.where(head1 == (h == 1), k_ref[...] * scale, 0.0).astype(BF16) for h in range(2)]
        vs = [jnp.where(head1 == (h == 1), v_ref[...], 0.0).astype(BF16) for h in range(2)]
        cks = [ck_ref[0, :, h:h + 1] for h in range(2)]
        kts = [k.T for k in ks]

        def step(qb, carry, masked):
            qs = pl.multiple_of(qb * T, T)
            q = q_ref[pl.ds(qs, T), :]
            do = do_ref[pl.ds(qs, T), :].astype(BF16)
            new, dq = [], None
            for h in range(2):
                dk, dv, dck = carry[h]
                lse = lse_ref[0, h:h + 1, pl.ds(qs, T)]
                dl = dl_ref[0, h:h + 1, pl.ds(qs, T)]
                cq = cq_ref[0, h:h + 1, pl.ds(qs, T)]
                st = lax.dot_general(ks[h], q, _DIMS["nt"], preferred_element_type=F32) + cq - cks[h]
                if masked:
                    keep = lax.broadcasted_iota(jnp.int32, (T, T), 0) <= lax.broadcasted_iota(jnp.int32, (T, T), 1)
                    st = jnp.where(keep, st, NEG_BIG)
                pt = jnp.exp(st - lse)
                dv = dv + jnp.dot(pt.astype(BF16), do, preferred_element_type=F32)
                dpt = lax.dot_general(vs[h], do, _DIMS["nt"], preferred_element_type=F32)
                dst = pt * (dpt - dl)
                dsb = dst.astype(BF16)
                dk = dk + jnp.dot(dsb, q, preferred_element_type=F32)
                part = jnp.dot(kts[h], dsb, preferred_element_type=F32)
                dq = part if dq is None else dq + part
                dcq_acc[h:h + 1, pl.ds(qs, T)] += jnp.sum(dst, axis=0, keepdims=True)
                dck = dck + jnp.sum(dst, axis=1, keepdims=True)
                new.append((dk, dv, dck))
            dq_acc[:, pl.ds(qs, T)] += dq
            return tuple(new)

        one = (jnp.zeros((T, LANES), F32), jnp.zeros((T, LANES), F32), jnp.zeros((T, 1), F32))
        carry = step(kb, (one, one), True)
        carry = lax.fori_loop(kb + 1, nb, functools.partial(step, masked=False), carry)
        (dk0, dv0, dck0), (dk1, dv1, dck1) = carry
        dk_ref[...] = (jnp.where(head1, dk1, dk0) * scale).astype(dk_ref.dtype)
        dv_ref[...] = jnp.where(head1, dv1, dv0).astype(dv_ref.dtype)
        dck_ref[0] = jnp.concatenate([-dck0, -dck1], axis=1)

        @pl.when(kb == nb - 1)
        def _():
            dq_ref[...] = dq_acc[...].T.astype(dq_ref.dtype)
            dcq_ref[0] = dcq_acc[...]

    full = lambda col: pl.BlockSpec((L, LANES), col)
    row_stat = pl.BlockSpec((1, 2, L), lambda p, i: (p, 0, 0))
    return carrier_call(
        body, name, (npair, nb),
        [full(lambda p, i: (0, p)),
         pl.BlockSpec((T, LANES), lambda p, i: (i, npair + p)),
         pl.BlockSpec((T, LANES), lambda p, i: (i, 2 * npair + p)),
         full(lambda p, i: (0, p)),
         row_stat, row_stat, row_stat,
         pl.BlockSpec((1, T, 2), lambda p, i: (p, i, 0))],
        [full(lambda p, i: (0, p)),
         pl.BlockSpec((T, LANES), lambda p, i: (i, p)),
         pl.BlockSpec((T, LANES), lambda p, i: (i, p)),
         row_stat,
         pl.BlockSpec((1, T, 2), lambda p, i: (p, i, 0))],
        [jax.ShapeDtypeStruct((L, ATTN_W), BF16)] * 3
        + [jax.ShapeDtypeStruct((npair, 2, L), F32), jax.ShapeDtypeStruct((npair, L, 2), F32)],
        [pltpu.VMEM((LANES, L), F32), pltpu.VMEM((2, L), F32)],
        (proj, proj, proj, dcat, lse_row, delta_row, cq_row, ck_col), comm, ("parallel", "arbitrary"))


def _shift_down(x, k):
    rolled = pltpu.roll(x, k, 0)
    return jnp.where(lax.broadcasted_iota(jnp.int32, x.shape, 0) >= k, rolled, 0.0)


def _shift_up(x, k):
    n = x.shape[0]
    rolled = pltpu.roll(x, n - k, 0)
    return jnp.where(lax.broadcasted_iota(jnp.int32, x.shape, 0) < n - k, rolled, 0.0)


def conv_fwd(proj, col0, conv_w, conv_b, name):
    L = proj.shape[0]
    C = conv_w.shape[1]
    nc = C // LANES

    def body(gb_ref, gc_ref, xc_ref, w_ref, b_ref, o_ref):
        z = gc_ref[...] * xc_ref[...]
        conv = (w_ref[0:1, :] * _shift_down(z, 2) + w_ref[1:2, :] * _shift_down(z, 1) + w_ref[2:3, :] * z
                + b_ref[...])
        o_ref[...] = (gb_ref[...] * conv).astype(o_ref.dtype)

    col = lambda off: pl.BlockSpec((L, LANES), lambda j, off=off: (0, col0 + off + j))
    return pl.pallas_call(
        body, name=name, grid=(nc,),
        in_specs=[col(0), col(nc), col(2 * nc), pl.BlockSpec((CONV_K, LANES), lambda j: (0, j)),
                  pl.BlockSpec((1, LANES), lambda j: (0, j))],
        out_specs=pl.BlockSpec((L, LANES), lambda j: (0, j)),
        out_shape=jax.ShapeDtypeStruct((L, C), BF16),
        compiler_params=_params(("parallel",)),
    )(proj, proj, proj, conv_w, conv_b)


def conv_bwd(proj, col0, conv_w, conv_b, dcat, dcol0, name):
    L = proj.shape[0]
    C = conv_w.shape[1]
    nc = C // LANES

    def body(gb_ref, gc_ref, xc_ref, w_ref, b_ref, do_ref, dgb_ref, dgc_ref, dxc_ref, dw_ref, db_ref):
        gc, xc = gc_ref[...], xc_ref[...]
        z = gc * xc
        z1, z2 = _shift_down(z, 1), _shift_down(z, 2)
        w0, w1, w2 = w_ref[0:1, :], w_ref[1:2, :], w_ref[2:3, :]
        conv = w0 * z2 + w1 * z1 + w2 * z + b_ref[...]
        dout = do_ref[...]
        dgb_ref[...] = (dout * conv).astype(dgb_ref.dtype)
        dconv = dout * gb_ref[...]
        dw_ref[...] = jnp.concatenate([jnp.sum(dconv * z2, axis=0, keepdims=True),
                                       jnp.sum(dconv * z1, axis=0, keepdims=True),
                                       jnp.sum(dconv * z, axis=0, keepdims=True)], axis=0)
        db_ref[...] = jnp.sum(dconv, axis=0, keepdims=True)
        dz = w2 * dconv + w1 * _shift_up(dconv, 1) + w0 * _shift_up(dconv, 2)
        dgc_ref[...] = (dz * xc).astype(dgc_ref.dtype)
        dxc_ref[...] = (dz * gc).astype(dxc_ref.dtype)

    col = lambda off: pl.BlockSpec((L, LANES), lambda j, off=off: (0, col0 + off + j))
    out_col = pl.BlockSpec((L, LANES), lambda j: (0, j))
    return pl.pallas_call(
        body, name=name, grid=(nc,),
        in_specs=[col(0), col(nc), col(2 * nc), pl.BlockSpec((CONV_K, LANES), lambda j: (0, j)),
                  pl.BlockSpec((1, LANES), lambda j: (0, j)),
                  pl.BlockSpec((L, LANES), lambda j: (0, dcol0 + j))],
        out_specs=[out_col, out_col, out_col, pl.BlockSpec((CONV_K, LANES), lambda j: (0, j)),
                   pl.BlockSpec((1, LANES), lambda j: (0, j))],
        out_shape=[jax.ShapeDtypeStruct((L, C), BF16)] * 3
        + [jax.ShapeDtypeStruct((CONV_K, C), F32), jax.ShapeDtypeStruct((1, C), F32)],
        compiler_params=_params(("parallel",)),
    )(proj, proj, proj, conv_w, conv_b, dcat)


_GELU_C = math.sqrt(2.0 / math.pi)
_GELU_A = 0.044715


def _gelu(y):
    return 0.5 * y * (1.0 + jnp.tanh(_GELU_C * (y + _GELU_A * y * y * y)))


def _gelu_grad(y):
    t = jnp.tanh(_GELU_C * (y + _GELU_A * y * y * y))
    return 0.5 * (1.0 + t) + 0.5 * y * (1.0 - t * t) * _GELU_C * (1.0 + 3.0 * _GELU_A * y * y)


def _cmul_add(xr, xi, pr, pi, sr, si):
    return xr + pr * sr - pi * si, xi + pr * si + pi * sr


def _scan_tile(br, bi, cr, ci, tab_ref, reverse):
    n = S5_BLOCK_STATES
    xr, xi = br, bi
    for s, k in enumerate((1, 2, 4)):
        shift = SUBLANES - k if reverse else k
        xr, xi = _cmul_add(xr, xi, tab_ref[0, s, :, :n], tab_ref[0, s, :, n:],
                           pltpu.roll(xr, shift, 0), pltpu.roll(xi, shift, 0))
    return _cmul_add(xr, xi, tab_ref[0, 3, :, :n], tab_ref[0, 3, :, n:], cr, ci)


def s5_fwd(u, bmat, cmat, tab, dvec, name, comm=None):
    L, D = u.shape
    nblk = D // LANES
    T = _div_tile(L, (ROW_TILE,))
    ns = 2 * S5_BLOCK_STATES
    n = S5_BLOCK_STATES

    def body(u_ref, b_ref, c_ref, tab_ref, d_ref, y_ref, g_ref, xs_ref, buf, car):
        @pl.when(pl.program_id(1) == 0)
        def _():
            car[...] = jnp.zeros_like(car)

        uv = u_ref[...]
        buf[...] = jnp.dot(uv.astype(BF16), b_ref[0], preferred_element_type=F32)

        def tile(i, carry):
            cr, ci = carry
            r0 = pl.multiple_of(i * SUBLANES, SUBLANES)
            xr, xi = _scan_tile(buf[pl.ds(r0, SUBLANES), :n], buf[pl.ds(r0, SUBLANES), n:], cr, ci, tab_ref, False)
            buf[pl.ds(r0, SUBLANES), :n] = xr
            buf[pl.ds(r0, SUBLANES), n:] = xi
            return xr[SUBLANES - 1:, :], xi[SUBLANES - 1:, :]

        cr, ci = lax.fori_loop(0, T // SUBLANES, tile, (car[:, :n], car[:, n:]))
        car[:, :n] = cr
        car[:, n:] = ci
        xs = buf[...]
        xs_ref[...] = xs
        y = jnp.dot(xs.astype(BF16), c_ref[0], preferred_element_type=F32) + d_ref[...] * uv
        y_ref[...] = y
        g_ref[...] = _gelu(y).astype(g_ref.dtype)

    blk = pl.BlockSpec((T, LANES), lambda j, i: (i, j))
    return carrier_call(
        body, name, (nblk, L // T),
        [blk, pl.BlockSpec((1, LANES, ns), lambda j, i: (j, 0, 0)),
         pl.BlockSpec((1, ns, LANES), lambda j, i: (j, 0, 0)),
         pl.BlockSpec((1, 4, SUBLANES, ns), lambda j, i: (j, 0, 0, 0)),
         pl.BlockSpec((1, LANES), lambda j, i: (0, j))],
        [blk, blk, pl.BlockSpec((T, ns), lambda j, i: (i, j))],
        [jax.ShapeDtypeStruct((L, D), F32), jax.ShapeDtypeStruct((L, D), BF16),
         jax.ShapeDtypeStruct((L, nblk * ns), F32)],
        [pltpu.VMEM((T, ns), F32), pltpu.VMEM((1, ns), F32)],
        (u, bmat, cmat, tab, dvec), comm, ("parallel", "arbitrary"))


def s5_bwd(dg, y, u, xs, cmat_t, bmat_t, rtab, dvec, name, comm=None):
    L, D = u.shape
    nblk = D // LANES
    T = _div_tile(L, (ROW_TILE,))
    nch = L // T
    ns = 2 * S5_BLOCK_STATES
    n = S5_BLOCK_STATES
    ntile = T // SUBLANES

    def body(dg_ref, y_ref, u_ref, xs_ref, xp_ref, ct_ref, bt_ref, tab_ref, d_ref,
             du_ref, dc_ref, db_ref, dlam_ref, dd_ref, buf, xbuf, car):
        step = pl.program_id(1)
        first_chunk = step == nch - 1

        @pl.when(step == 0)
        def _():
            car[...] = jnp.zeros_like(car)
            dc_ref[...] = jnp.zeros_like(dc_ref)
            db_ref[...] = jnp.zeros_like(db_ref)
            dlam_ref[...] = jnp.zeros_like(dlam_ref)
            dd_ref[...] = jnp.zeros_like(dd_ref)

        uv = u_ref[...]
        dy = dg_ref[...].astype(F32) * _gelu_grad(y_ref[...])
        dd_ref[...] += jnp.sum(dy * uv, axis=0, keepdims=True)
        dyb = dy.astype(BF16)
        buf[...] = jnp.dot(dyb, ct_ref[0], preferred_element_type=F32)
        xs = xs_ref[...]
        xbuf[pl.ds(SUBLANES, T), :] = xs
        xbuf[pl.ds(0, SUBLANES), :] = jnp.where(first_chunk, 0.0, xp_ref[...])
        row0 = lax.broadcasted_iota(jnp.int32, (SUBLANES, n), 0) == 0

        def tile(ii, carry):
            cr, ci, ar, ai = carry
            r0 = pl.multiple_of((ntile - 1 - ii) * SUBLANES, SUBLANES)
            xr, xi = _scan_tile(buf[pl.ds(r0, SUBLANES), :n], buf[pl.ds(r0, SUBLANES), n:], cr, ci, tab_ref, True)
            buf[pl.ds(r0, SUBLANES), :n] = xr
            buf[pl.ds(r0, SUBLANES), n:] = xi
            r1 = pl.multiple_of(r0 + SUBLANES, SUBLANES)
            pr = jnp.where(row0, xbuf[pl.ds(r0, SUBLANES), :n][SUBLANES - 1:, :],
                           pltpu.roll(xbuf[pl.ds(r1, SUBLANES), :n], 1, 0))
            pi = jnp.where(row0, xbuf[pl.ds(r0, SUBLANES), n:][SUBLANES - 1:, :],
                           pltpu.roll(xbuf[pl.ds(r1, SUBLANES), n:], 1, 0))
            ar = ar + xr * pr + xi * pi
            ai = ai + xi * pr - xr * pi
            return xr[0:1, :], xi[0:1, :], ar, ai

        zero = jnp.zeros((SUBLANES, n), F32)
        cr, ci, ar, ai = lax.fori_loop(0, ntile, tile, (car[:, :n], car[:, n:], zero, zero))
        car[:, :n] = cr
        car[:, n:] = ci
        dlam_ref[0, :, :n] += ar
        dlam_ref[0, :, n:] += ai
        dxa = buf[...]
        dc_ref[0] += lax.dot_general(dyb, xs.astype(BF16), _DIMS["tn"], preferred_element_type=F32)
        dxb = dxa.astype(BF16)
        db_ref[0] += lax.dot_general(uv.astype(BF16), dxb, _DIMS["tn"], preferred_element_type=F32)
        du_ref[...] = jnp.dot(dxb, bt_ref[0], preferred_element_type=F32) + d_ref[...] * dy

    rev = lambda j, i: (nch - 1 - i, j)
    blk = pl.BlockSpec((T, LANES), rev)
    tpb = T // SUBLANES
    acc = pl.BlockSpec((1, LANES, ns), lambda j, i: (j, 0, 0))
    return carrier_call(
        body, name, (nblk, nch),
        [blk, blk, blk, pl.BlockSpec((T, ns), rev),
         pl.BlockSpec((SUBLANES, ns), lambda j, i: (jnp.maximum((nch - 1 - i) * tpb - 1, 0), j)),
         pl.BlockSpec((1, LANES, ns), lambda j, i: (j, 0, 0)),
         pl.BlockSpec((1, ns, LANES), lambda j, i: (j, 0, 0)),
         pl.BlockSpec((1, 4, SUBLANES, ns), lambda j, i: (j, 0, 0, 0)),
         pl.BlockSpec((1, LANES), lambda j, i: (0, j))],
        [blk, acc, acc, pl.BlockSpec((1, SUBLANES, ns), lambda j, i: (j, 0, 0)),
         pl.BlockSpec((1, LANES), lambda j, i: (0, j))],
        [jax.ShapeDtypeStruct((L, D), F32), jax.ShapeDtypeStruct((nblk, LANES, ns), F32),
         jax.ShapeDtypeStruct((nblk, LANES, ns), F32), jax.ShapeDtypeStruct((nblk, SUBLANES, ns), F32),
         jax.ShapeDtypeStruct((1, D), F32)],
        [pltpu.VMEM((T, ns), F32), pltpu.VMEM((T + SUBLANES, ns), F32), pltpu.VMEM((1, ns), F32)],
        (dg, y, u, xs, xs, cmat_t, bmat_t, rtab, dvec), comm, ("parallel", "arbitrary"))


def _s5_discretize(a_re, a_im, log_step, b_re, b_im):
    lam_re = jnp.minimum(a_re, -S5_MIN_DECAY)
    lam_im = a_im
    delta = jnp.exp(log_step)[:, None]
    mag = jnp.exp(lam_re * delta)
    ang = lam_im * delta
    lb_re = mag * jnp.cos(ang)
    lb_im = mag * jnp.sin(ang)
    den = lam_re * lam_re + lam_im * lam_im
    nr = lb_re - 1.0
    ni = lb_im
    coef_re = (nr * lam_re + ni * lam_im) / den
    coef_im = (ni * lam_re - nr * lam_im) / den
    bb_re = coef_re[..., None] * b_re - coef_im[..., None] * b_im
    bb_im = coef_re[..., None] * b_im + coef_im[..., None] * b_re
    return lb_re, lb_im, bb_re, bb_im


def _s5_tables(lb_re, lb_im):
    nblk = lb_re.shape[0] // S5_BLOCK_GROUPS
    lr = lb_re.reshape(nblk, S5_BLOCK_STATES)
    li = lb_im.reshape(nblk, S5_BLOCK_STATES)
    pows = [(jnp.ones_like(lr), jnp.zeros_like(li))]
    for _ in range(SUBLANES):
        pr, pi = pows[-1]
        pows.append((pr * lr - pi * li, pr * li + pi * lr))
    rows = jnp.arange(SUBLANES)[None, :, None]

    def table(conj, reverse):
        sgn = -1.0 if conj else 1.0
        out = []
        for k in (1, 2, 4):
            mask = (rows <= SUBLANES - 1 - k) if reverse else (rows >= k)
            out.append(jnp.concatenate([jnp.where(mask, pows[k][0][:, None, :], 0.0),
                                        jnp.where(mask, sgn * pows[k][1][:, None, :], 0.0)], axis=-1))
        order = range(SUBLANES, 0, -1) if reverse else range(1, SUBLANES + 1)
        cre = jnp.stack([pows[k][0] for k in order], axis=1)
        cim = jnp.stack([sgn * pows[k][1] for k in order], axis=1)
        out.append(jnp.concatenate([cre, cim], axis=-1))
        return jnp.stack(out, axis=1)

    return table(False, False), table(True, True)


def _s5_block_mats(bb_re, bb_im, c_re, c_im):
    G = bb_re.shape[0]
    nblk = G // S5_BLOCK_GROUPS
    eye = jnp.eye(S5_BLOCK_GROUPS, dtype=F32)
    bb = jnp.stack([bb_re, bb_im]).reshape(2, nblk, S5_BLOCK_GROUPS, S5_STATE, S5_GROUP)
    bmat = jnp.einsum("ab,rjaph->jahrbp", eye, bb).reshape(nblk, LANES, 2 * S5_BLOCK_STATES)
    cc = jnp.stack([c_re, -c_im]).reshape(2, nblk, S5_BLOCK_GROUPS, S5_GROUP, S5_STATE)
    cmat = jnp.einsum("ab,rjahp->jrbpah", eye, cc).reshape(nblk, 2 * S5_BLOCK_STATES, LANES)
    return bmat, cmat


def _s5_unblock(dB, dC, dlam):
    nblk = dB.shape[0]
    G = nblk * S5_BLOCK_GROUPS
    d6 = dB.reshape(nblk, S5_BLOCK_GROUPS, S5_GROUP, 2, S5_BLOCK_GROUPS, S5_STATE)
    dbb = jnp.einsum("jahrap->rjaph", d6).reshape(2, G, S5_STATE, S5_GROUP)
    c6 = dC.reshape(nblk, S5_BLOCK_GROUPS, S5_GROUP, 2, S5_BLOCK_GROUPS, S5_STATE)
    dcc = jnp.einsum("jahrap->rjahp", c6).reshape(2, G, S5_GROUP, S5_STATE)
    dl = jnp.sum(dlam, axis=1).reshape(nblk, 2, S5_BLOCK_GROUPS, S5_STATE)
    dl = jnp.transpose(dl, (1, 0, 2, 3)).reshape(2, G, S5_STATE)
    return dbb[0], dbb[1], dcc[0], -dcc[1], dl[0], dl[1]


def loss_and_grad(y, target, name):
    L, D = y.shape
    tr = _div_tile(L, (512, 256, 128))

    def body(y_ref, t_ref, dy_ref, loss_ref):
        err = y_ref[...] - t_ref[...]
        dy_ref[...] = err * (1.0 / D)
        part = 0.5 * jnp.sum(jnp.mean(err * err, axis=-1, keepdims=True), axis=0, keepdims=True)

        @pl.when(pl.program_id(0) == 0)
        def _():
            loss_ref[...] = part

        @pl.when(pl.program_id(0) > 0)
        def _():
            loss_ref[...] += part

    row = pl.BlockSpec((tr, D), lambda i: (i, 0))
    return pl.pallas_call(
        body, name=name, grid=(L // tr,), in_specs=[row, row],
        out_specs=[row, pl.BlockSpec((1, 1), lambda i: (0, 0))],
        out_shape=[jax.ShapeDtypeStruct((L, D), F32), jax.ShapeDtypeStruct((1, 1), F32)],
        compiler_params=_params(("arbitrary",)),
    )(y, target)


def _adam_math(w, g, m, v):
    m = ADAM_B1 * m + (1.0 - ADAM_B1) * g
    v = ADAM_B2 * v + (1.0 - ADAM_B2) * (g * g)
    m_hat = m / (1.0 - ADAM_B1 ** ADAM_STEP)
    v_hat = v / (1.0 - ADAM_B2 ** ADAM_STEP)
    delta = -ADAM_LR * (m_hat / (jnp.sqrt(v_hat) + ADAM_EPS) + ADAM_WD * w)
    return delta, m, v


def _as3d(a):
    return a.reshape((-1,) + a.shape[-2:])


def adamw(w, g, m, v, name):
    shape = w.shape
    w3, g3, m3, v3 = _as3d(w), _as3d(g), _as3d(m), _as3d(v)
    A, R, C = w3.shape
    tr = _row_tile(R, C)

    def body(w_ref, g_ref, m_ref, v_ref, d_ref, mo_ref, vo_ref):
        d, mn, vn = _adam_math(w_ref[...], g_ref[...], m_ref[...], v_ref[...])
        d_ref[...] = d
        mo_ref[...] = mn
        vo_ref[...] = vn

    spec = pl.BlockSpec((1, tr, C), lambda a, i: (a, i, 0))
    outs = pl.pallas_call(
        body, name=name, grid=(A, R // tr), in_specs=[spec] * 4, out_specs=[spec] * 3,
        out_shape=[jax.ShapeDtypeStruct((A, R, C), F32)] * 3,
        compiler_params=_params(("parallel", "parallel")),
    )(w3, g3, m3, v3)
    return [o.reshape(shape) for o in outs]


def _place():
    x, y, c = lax.axis_index("x"), lax.axis_index("y"), lax.axis_index("c")
    other_chips = [(1 - x, y), (x, 1 - y), (1 - x, 1 - y)]
    return x, y, c, other_chips


def _chip_id(chip):
    return 2 * chip[0] + chip[1]


def _my_chip():
    return 2 * lax.axis_index("x") + lax.axis_index("y")


def _remote(src, dst, send_sem, recv_sem, dev):
    return pltpu.make_async_remote_copy(src_ref=src, dst_ref=dst, send_sem=send_sem, recv_sem=recv_sem,
                                        device_id=dev, device_id_type=MESH)


def allgather_small(arrs):
    T = len(arrs)

    def body(*refs):
        ins, outs = refs[:T], refs[T:2 * T]
        send_sems, recv_sems = refs[2 * T:]
        x, y, c, chips = _place()
        me = _chip_id((x, y))
        sends = []
        for t in range(T):
            outs[t][me] = ins[t][...]
            for j, chip in enumerate(chips):
                cp = _remote(ins[t], outs[t].at[me], send_sems.at[3 * t + j], recv_sems.at[3 * t + j], (*chip, c))
                cp.start()
                sends.append(cp)
        for t in range(T):
            for j, chip in enumerate(chips):
                slot = outs[t].at[_chip_id(chip)]
                _remote(slot, slot, send_sems.at[3 * t + j], recv_sems.at[3 * t + j], (*chip, c)).wait_recv()
        for cp in sends:
            cp.wait_send()

    return pl.pallas_call(
        body, name="allgather_small", in_specs=[VMEM_SPEC] * T, out_specs=[VMEM_SPEC] * T,
        out_shape=[jax.ShapeDtypeStruct((4,) + a.shape, a.dtype) for a in arrs],
        scratch_shapes=[pltpu.SemaphoreType.DMA((3 * T,)), pltpu.SemaphoreType.DMA((3 * T,))],
        compiler_params=pltpu.CompilerParams(vmem_limit_bytes=VMEM_LIMIT, has_side_effects=True),
    )(*arrs)


def small_swap_comm(buf):
    def copy(ins, outs, send_sems, recv_sems):
        x, y, c, _ = _place()
        return _remote(ins[0], outs[0], send_sems.at[0], recv_sems.at[0], (x, y, 1 - c))

    return Comm([buf], [jax.ShapeDtypeStruct(buf.shape, F32)], {}, 1,
                lambda *refs: copy(*refs).start(), None, lambda *refs: copy(*refs).wait())


def small_exchange_comm(pair):
    def copies(ins, outs, send_sems, recv_sems):
        x, y, c, chips = _place()
        return [_remote(ins[0], outs[0].at[j], send_sems.at[j], recv_sems.at[j], (*chip, c))
                for j, chip in enumerate(chips)]

    def begin(*refs):
        for cp in copies(*refs):
            cp.start()

    def finish(*refs):
        for cp in copies(*refs):
            cp.wait()

    return Comm([pair], [jax.ShapeDtypeStruct((3,) + pair.shape, F32)], {}, 3, begin, None, finish)


def small_pair_sum(mine, theirs, name):
    R, C = mine.shape
    tr = _row_tile(R, C)

    def body(a_ref, b_ref, o_ref):
        o_ref[...] = a_ref[...] + b_ref[...]

    spec = pl.BlockSpec((tr, C), lambda i: (i, 0))
    return pl.pallas_call(body, name=name, grid=(R // tr,), in_specs=[spec, spec], out_specs=spec,
                          out_shape=jax.ShapeDtypeStruct((R, C), F32), compiler_params=_params(("parallel",)))(mine, theirs)


def small_chip_sum(pair, got, name):
    R, C = pair.shape
    tr = _row_tile(R, C)

    def body(p_ref, g_ref, o_ref):
        me = _my_chip()
        terms = []
        for chip in range(4):
            d = jnp.bitwise_xor(me, chip)
            terms.append(jnp.where(d == 0, p_ref[...],
                                   jnp.where(d == 2, g_ref[0], jnp.where(d == 1, g_ref[1], g_ref[2]))))
        o_ref[...] = ((terms[0] + terms[1]) + terms[2]) + terms[3]

    spec = pl.BlockSpec((tr, C), lambda i: (i, 0))
    return pl.pallas_call(body, name=name, grid=(R // tr,),
                          in_specs=[spec, pl.BlockSpec((3, tr, C), lambda i: (0, i, 0))], out_specs=spec,
                          out_shape=jax.ShapeDtypeStruct((R, C), F32), compiler_params=_params(("parallel",)))(pair, got)


def _half_rows(ref, layout, shard, half):
    if layout == "S":
        hr = ref.shape[1] // 2
        return ref.at[shard, pl.ds(pl.multiple_of(half * hr, 16), hr), :]
    hr, C = ref.shape[0] // 2, ref.shape[1] // 4
    return ref.at[pl.ds(pl.multiple_of(half * hr, 16), hr), pl.ds(pl.multiple_of(shard * C, LANES), C)]


def _half_rows_all(ref, layout, half):
    if layout == "S":
        hr = ref.shape[1] // 2
        return ref.at[:, pl.ds(pl.multiple_of(half * hr, 16), hr), :]
    hr = ref.shape[0] // 2
    return ref.at[pl.ds(pl.multiple_of(half * hr, 16), hr), :]


def _shard_of_half(ref, layout, shard):
    if layout == "S":
        return ref.at[shard]
    C = ref.shape[1] // 4
    return ref.at[:, pl.ds(pl.multiple_of(shard * C, LANES), C)]


def _own_block_spec(layout, tr, C):
    if layout == "S":
        return pl.BlockSpec((None, tr, C), lambda i: (_my_chip(), i, 0))
    return pl.BlockSpec((tr, C), lambda i: (i, _my_chip()))


def cast_into_gathered(shards, layer, layout, name):
    _, R, C = shards.shape
    tr = _row_tile(R, C)

    def body(a_ref, o_ref):
        o_ref[...] = a_ref[...].astype(BF16)

    return pl.pallas_call(
        body, name=name, grid=(R // tr,),
        in_specs=[pl.BlockSpec((None, tr, C), lambda i: (layer, i, 0))],
        out_specs=_own_block_spec(layout, tr, C),
        out_shape=jax.ShapeDtypeStruct((4, R, C) if layout == "S" else (R, 4 * C), BF16),
        compiler_params=_params(("parallel",)),
    )(shards)


def gather_comm(bufs, layouts):
    T = len(bufs)

    def begin(_, outs, send_sems, recv_sems):
        x, y, c, chips = _place()
        for t in range(T):
            mine = _half_rows(outs[t], layouts[t], _chip_id((x, y)), c)
            for j, chip in enumerate(chips):
                _remote(mine, mine, send_sems.at[6 * t + j], recv_sems.at[6 * t + j], (*chip, c)).start()

    def middle(_, outs, send_sems, recv_sems):
        x, y, c, chips = _place()
        for t in range(T):
            for j, chip in enumerate(chips):
                piece = _half_rows(outs[t], layouts[t], _chip_id(chip), c)
                _remote(piece, piece, send_sems.at[6 * t + j], recv_sems.at[6 * t + j], (*chip, c)).wait_recv()
                _remote(piece, piece, send_sems.at[6 * t + 3 + j], recv_sems.at[6 * t + 3 + j], (x, y, 1 - c)).start()

    def finish(_, outs, send_sems, recv_sems):
        x, y, c, chips = _place()
        for t in range(T):
            mine = _half_rows(outs[t], layouts[t], _chip_id((x, y)), c)
            for j, chip in enumerate(chips):
                theirs = _half_rows(outs[t], layouts[t], _chip_id(chip), 1 - c)
                _remote(theirs, theirs, send_sems.at[6 * t + 3 + j], recv_sems.at[6 * t + 3 + j],
                        (x, y, 1 - c)).wait_recv()
                _remote(mine, mine, send_sems.at[6 * t + j], recv_sems.at[6 * t + j], (*chip, c)).wait_send()
                piece = _half_rows(outs[t], layouts[t], _chip_id(chip), c)
                _remote(piece, piece, send_sems.at[6 * t + 3 + j], recv_sems.at[6 * t + 3 + j],
                        (x, y, 1 - c)).wait_send()

    return Comm(bufs, [jax.ShapeDtypeStruct(b.shape, b.dtype) for b in bufs], {t: t for t in range(T)}, 6 * T,
                begin, middle, finish, middle_frac=0.75)


def swap_comm(grads, layouts):
    T = len(grads)

    def out_shape(g, layout):
        return (4, g.shape[1] // 2, g.shape[2]) if layout == "S" else (g.shape[0] // 2, g.shape[1])

    def copies(ins, outs, send_sems, recv_sems):
        x, y, c, _ = _place()
        return [_remote(_half_rows_all(ins[t], layouts[t], 1 - c), outs[t], send_sems.at[t], recv_sems.at[t],
                        (x, y, 1 - c)) for t in range(T)]

    def begin(*refs):
        for cp in copies(*refs):
            cp.start()

    def finish(*refs):
        for cp in copies(*refs):
            cp.wait()

    return Comm(grads, [jax.ShapeDtypeStruct(out_shape(g, k), F32) for g, k in zip(grads, layouts)], {}, T,
                begin, None, finish)


def pair_sum(grad, recv, layout, name):
    if layout == "S":
        _, hr, C = recv.shape
        tr = _row_tile(hr, C)
        nb = hr // tr
        grid = (4, nb)
        g_spec = pl.BlockSpec((None, tr, C), lambda a, i: (a, lax.axis_index("c") * nb + i, 0))
        spec = pl.BlockSpec((None, tr, C), lambda a, i: (a, i, 0))
    else:
        hr, C = recv.shape
        tr = _row_tile(hr, C)
        nb = hr // tr
        grid = (nb,)
        g_spec = pl.BlockSpec((tr, C), lambda i: (lax.axis_index("c") * nb + i, 0))
        spec = pl.BlockSpec((tr, C), lambda i: (i, 0))

    def body(g_ref, r_ref, f_ref, b_ref):
        s = g_ref[...] + r_ref[...]
        f_ref[...] = s
        b_ref[...] = s.astype(BF16)

    return pl.pallas_call(
        body, name=name, grid=grid, in_specs=[g_spec, spec], out_specs=[spec, spec],
        out_shape=[jax.ShapeDtypeStruct(recv.shape, F32), jax.ShapeDtypeStruct(recv.shape, BF16)],
        compiler_params=_params(("parallel",) * len(grid)),
    )(grad, recv)


def exchange_comm(pair_bf16, layouts):
    T = len(pair_bf16)

    def out_shape(p, layout):
        return (3,) + ((p.shape[1], p.shape[2]) if layout == "S" else (p.shape[0], p.shape[1] // 4))

    def copies(ins, outs, send_sems, recv_sems):
        x, y, c, chips = _place()
        return [_remote(_shard_of_half(ins[t], layouts[t], _chip_id(chip)), outs[t].at[j],
                        send_sems.at[3 * t + j], recv_sems.at[3 * t + j], (*chip, c))
                for t in range(T) for j, chip in enumerate(chips)]

    def begin(*refs):
        for cp in copies(*refs):
            cp.start()

    def finish(*refs):
        for cp in copies(*refs):
            cp.wait()

    return Comm(pair_bf16, [jax.ShapeDtypeStruct(out_shape(p, k), BF16) for p, k in zip(pair_bf16, layouts)], {},
                3 * T, begin, None, finish)


def reduce_total(pair_f32, got, layout, layer, n_layers, previous, name):
    _, hr, C = got.shape
    tr = _row_tile(hr, C)
    nb = hr // tr

    def body(*refs):
        p_ref, g_ref, t_ref = refs[0], refs[1], refs[-1]
        t_ref[...] = ((p_ref[...] + g_ref[0].astype(F32)) + g_ref[1].astype(F32)) + g_ref[2].astype(F32)

    args = [pair_f32, got] + ([previous] if previous is not None else [])
    return pl.pallas_call(
        body, name=name, grid=(nb,),
        in_specs=[_own_block_spec(layout, tr, C), pl.BlockSpec((3, tr, C), lambda i: (0, i, 0))]
        + ([ANY] if previous is not None else []),
        out_specs=pl.BlockSpec((None, tr, C), lambda i: (layer, lax.axis_index("c") * nb + i, 0)),
        out_shape=jax.ShapeDtypeStruct((n_layers, 2 * hr, C), F32),
        input_output_aliases={2: 0} if previous is not None else {},
        compiler_params=_params(("parallel",)),
    )(*args)


def share_comm(reduced):
    T = len(reduced)

    def halves(outs, half):
        return [o.at[:, pl.ds(pl.multiple_of(half * (o.shape[1] // 2), 8), o.shape[1] // 2), :] for o in outs]

    def begin(_, outs, send_sems, recv_sems):
        x, y, c, _p = _place()
        for t, mine in enumerate(halves(outs, c)):
            _remote(mine, mine, send_sems.at[t], recv_sems.at[t], (x, y, 1 - c)).start()

    def finish(_, outs, send_sems, recv_sems):
        x, y, c, _p = _place()
        for t, (mine, theirs) in enumerate(zip(halves(outs, c), halves(outs, 1 - c))):
            _remote(mine, mine, send_sems.at[t], recv_sems.at[t], (x, y, 1 - c)).wait_send()
            _remote(theirs, theirs, send_sems.at[t], recv_sems.at[t], (x, y, 1 - c)).wait_recv()

    return Comm(reduced, [jax.ShapeDtypeStruct(r.shape, r.dtype) for r in reduced], {t: t for t in range(T)}, T,
                begin, None, finish)


def _round_up(n, m):
    return (n + m - 1) // m * m


def _heads_col(a16):
    L = a16.shape[0]
    return jnp.transpose(a16.reshape(L, HEADS // 2, 2), (1, 0, 2))


def _heads_row(a16):
    L = a16.shape[0]
    return jnp.transpose(a16.reshape(L, HEADS // 2, 2), (1, 2, 0))


def local_step(x, target, meta, norm_g, b_f, conv_w, conv_b, s5, s5_d, stager):
    S, D = x.shape
    depth = norm_g.shape[0]
    n_even, n_odd = b_f.shape[0], s5_d.shape[0]
    L = N_META + S
    Lp = _round_up(L, ROW_TILE)
    h = jnp.concatenate([meta, x, jnp.zeros((Lp - L, D), F32)], axis=0)
    conv_c = conv_w.shape[2]
    fg_block = 3 * conv_c // LANES
    saved = []

    def riding(tag, fn, *args):
        comm = stager.ride(tag)
        if comm is None and fn is matmul:
            return fn(*args, name=tag)
        outs, arrived = fn(*args, name=tag, comm=comm)
        stager.arrived(tag, arrived)
        return outs[0] if fn is matmul else outs

    for i in range(depth):
        g = norm_g[i]
        j = i // 2
        tag = "l%d_" % i
        w = stager.weights(i)
        st = {"h0": h, "w": w}
        if i % 2 == 0:
            u = rmsnorm_fwd(h, g[0:1], BF16, tag + "norm0")
            qkv = matmul([(u, w["w_qkv"])], "nn", BF16, tag + "qkv")
            rest = matmul([(u, w["w_rest"])], "nn", F32, tag + "rest")
            cgate = gate_fwd(rest, fg_block, b_f[j], tag + "gate")
            c16 = cgate[:, :HEADS]
            attn, lse = riding(tag + "attn", attn_fwd, qkv, _heads_col(c16), _heads_row(c16))
            convo = conv_fwd(rest, 0, conv_w[j], conv_b[j], tag + "conv")
            cat = jnp.concatenate([attn, convo], axis=1)
            m = matmul([(cat, w["w_o"])], "nn", F32, tag + "wo")
            st.update(u=u, qkv=qkv, rest=rest, c16=c16, lse=lse, cat=cat)
        else:
            p = s5[j]
            u = rmsnorm_fwd(h, g[0:1], F32, tag + "norm0")
            y, gact, xs = riding(tag + "s5", s5_fwd, u, p["bmat"], p["cmat"], p["tab"], s5_d[j])
            o1, o2, m = riding(tag + "glu", dual_matmul_act, gact, w["w_glu1"], w["w_glu2"], "glu", F32)
            st.update(u=u, y=y, gact=gact, xs=xs, o1=o1, o2=o2)
        h1 = rmsnorm_fwd(m, g[1:2], F32, tag + "norm1", residual=h)
        u2 = rmsnorm_fwd(h1, g[2:3], BF16, tag + "norm2")
        a, b, hact = riding(tag + "ffn_in", dual_matmul_act, u2, w["w_gate"], w["w_up"], "swiglu", BF16)
        f = riding(tag + "ffn_out", matmul, [(hact, w["w_down"])], "nn", F32)
        h = rmsnorm_fwd(f, g[3:4], F32, tag + "norm3", residual=h1)
        st.update(m=m, h1=h1, u2=u2, a=a, b=b, hact=hact, f=f)
        saved.append(st)

    dy, loss = loss_and_grad(h[N_META:L], target, "loss")
    dh = jnp.concatenate([jnp.zeros((N_META, D), F32), dy, jnp.zeros((Lp - L, D), F32)], axis=0)

    grads = {k: [None] * n_even for k in ("b_f", "conv_w", "conv_b")}
    grads.update({k: [None] * n_odd for k in ("s5_d", "s5_dB", "s5_dC", "s5_dlam")})
    grads["norm_g"] = [None] * depth

    for i in reversed(range(depth)):
        g = norm_g[i]
        j = i // 2
        tag = "l%d_b_" % i
        st = saved[i]
        w = st["w"]
        wg = {}
        df, dg3 = rmsnorm_bwd(st["f"], g[3:4], dh, BF16, tag + "norm3")
        wg["w_down"] = matmul([(st["hact"], df)], "tn", F32, tag + "dw_down")
        da, db = riding(tag + "ffn_act", ffn_bwd_act, df, w["w_down"], st["a"], st["b"])
        u2t = st["u2"].T
        wg["w_gate"] = matmul([(u2t, da)], "nn", F32, tag + "dw_gate")
        wg["w_up"] = matmul([(u2t, db)], "nn", F32, tag + "dw_up")
        stager.put_grads(i, wg)
        wg = {}
        du2 = riding(tag + "du2", matmul, [(da, w["w_gate"])], "nt", F32)
        du2 = matmul([(db, w["w_up"])], "nt", F32, tag + "du2_up", add=du2)
        dh1, dg2 = rmsnorm_bwd(st["h1"], g[2:3], du2, F32, tag + "norm2", add=dh)
        if i % 2 == 0:
            dm, dg1 = rmsnorm_bwd(st["m"], g[1:2], dh1, BF16, tag + "norm1")
            wg["w_o"] = matmul([(st["cat"], dm)], "tn", F32, tag + "dw_o")
            stager.put_grads(i, wg)
            dcat = riding(tag + "dcat", matmul, [(dm, w["w_o"])], "nt", F32)
            delta = attn_delta(dcat, st["cat"], tag + "delta")
            c16 = st["c16"]
            lse16 = jnp.transpose(st["lse"], (1, 0, 2)).reshape(Lp, HEADS)
            dq, dk, dv, dcq, dck = riding(tag + "attn", attn_bwd, st["qkv"], dcat, _heads_row(lse16),
                                          _heads_row(delta[:, :HEADS]), _heads_row(c16), _heads_col(c16))
            dc16 = (jnp.transpose(dcq, (2, 0, 1)).reshape(Lp, HEADS)
                    + jnp.transpose(dck, (1, 0, 2)).reshape(Lp, HEADS))
            dc = jnp.pad(dc16, ((0, 0), (0, LANES - HEADS)))
            dfg, dbf = gate_bwd(st["rest"], fg_block, b_f[j], dc, tag + "gate")
            dgb, dgc, dxc, dcw, dcb = conv_bwd(st["rest"], 0, conv_w[j], conv_b[j], dcat, ATTN_W // LANES,
                                               tag + "conv")
            dqkv = jnp.concatenate([dq, dk, dv], axis=1)
            drest = jnp.concatenate([dgb, dgc, dxc, dfg.astype(BF16)], axis=1)
            wg["w_qkv"] = matmul([(st["u"], dqkv)], "tn", F32, tag + "dw_qkv")
            wg["w_rest"] = matmul([(st["u"], drest)], "tn", F32, tag + "dw_rest")
            stager.put_grads(i, wg)
            du = riding(tag + "du_qkv", matmul, [(dqkv, w["w_qkv"])], "nt", F32)
            du_b = riding(tag + "du_rest", matmul, [(drest, w["w_rest"])], "nt", F32)
            grads["b_f"][j], grads["conv_w"][j], grads["conv_b"][j] = dbf, dcw, dcb
        else:
            p = s5[j]
            dmix, dg1 = rmsnorm_bwd(st["m"], g[1:2], dh1, F32, tag + "norm1")
            do1, do2 = glu_bwd_act(dmix, st["o1"], st["o2"], tag + "glu_act")
            wg["w_glu1"] = matmul([(st["gact"], do1)], "tn", F32, tag + "dw_glu1")
            wg["w_glu2"] = matmul([(st["gact"], do2)], "tn", F32, tag + "dw_glu2")
            dgact = matmul([(do1, w["w_glu1"]), (do2, w["w_glu2"])], "nt", F32, tag + "dgact")
            du, dC, dB, dlam, dd = riding(tag + "s5", s5_bwd, dgact, st["y"], st["u"], st["xs"], p["cmat_t"],
                                          p["bmat_t"], p["rtab"], s5_d[j])
            du_b = None
            grads["s5_dB"][j], grads["s5_dC"][j], grads["s5_dlam"][j], grads["s5_d"][j] = dB, dC, dlam, dd
        dh, dg0 = rmsnorm_bwd(st["h0"], g[0:1], du, F32, tag + "norm0", add=dh1, dy2=du_b)
        grads["norm_g"][i] = jnp.concatenate([dg0, dg1, dg2, dg3], axis=0)
        stager.put_grads(i, wg)

    grads["meta"] = dh[:N_META]
    return loss, dh[N_META:L], grads


def _packed_rows(shape):
    return _round_up(_round_up(math.prod(shape), LANES) // LANES, SUBLANES)


def _pack(arrs):
    rows = []
    for a in arrs:
        flat = a.reshape(-1).astype(F32)
        r = _packed_rows(a.shape)
        rows.append(jnp.pad(flat, (0, r * LANES - flat.shape[0])).reshape(r, LANES))
    return jnp.concatenate(rows, axis=0)


def _unpack(buf, shapes):
    buf = buf.reshape(-1, LANES)
    out, off = [], 0
    for s in shapes:
        r = _packed_rows(s)
        out.append(buf[off:off + r].reshape(-1)[:math.prod(s)].reshape(s))
        off += r
    return out


class _LayerWeights:
    def __init__(self, stager, layer):
        self.stager, self.layer = stager, layer

    def __getitem__(self, name):
        return self.stager.weight(self.layer, name)


class MeshStager:
    LAYOUT = {"ab_w_in": "S", "ab_w_o": "S", "s5_w_glu1": "S", "s5_w_glu2": "S",
              "ffn_w_gate": "C", "ffn_w_up": "C", "ffn_w_down": "S"}
    EVEN = ("ab_w_in", "ab_w_o", "ffn_w_gate", "ffn_w_up", "ffn_w_down")
    ODD = ("s5_w_glu1", "s5_w_glu2", "ffn_w_gate", "ffn_w_up", "ffn_w_down")

    def __init__(self, shards):
        self.shards = shards
        self.depth = depth = shards["ffn_w_gate"].shape[0]
        self.bufs = {}
        for i in range(depth):
            for k in self.keys(i):
                self.bufs[k, i] = cast_into_gathered(shards[k], self.index(k, i), self.LAYOUT[k],
                                                     "cast_%s_%d" % (k, i))
        self.grads, self.pairs, self.reduced, self.ready = {}, {}, {}, {}
        first = [("ab_w_in", 0)]
        plan = self.gather_plan = {"l0_attn": [it for it in self.stage(0) if it not in first]}
        for o in range(1, depth, 2):
            e = o - 1
            plan.setdefault("l%d_attn" % e, []).extend([("s5_w_glu2", o)] + ([("ffn_w_gate", o)] if e else []))
            plan["l%d_ffn_in" % e] = [("s5_w_glu1", o), ("ffn_w_up", o)]
            if not e:
                plan["l%d_ffn_out" % e] = [("ffn_w_gate", o)]
            plan["l%d_s5" % o] = [("ffn_w_down", o)] + (self.mix(o + 1) if o + 1 < depth else [])
            if o + 1 < depth:
                plan["l%d_ffn_in" % o] = [("ffn_w_gate", o + 1)]
                plan["l%d_attn" % (o + 1)] = [("ffn_w_up", o + 1), ("ffn_w_down", o + 1)]
        self.swap_plan, self.exchange_plan = {}, {}
        for i in range(depth):
            above = self.mix(i + 1) if i + 1 < depth else []
            if above:
                self.swap_plan["l%d_b_ffn_act" % i] = above
            self.swap_plan["l%d_b_du2" % i] = self.ffn(i)
            self.exchange_plan["l%d_b_%s" % (i, "attn" if i % 2 == 0 else "s5")] = above + self.ffn(i)
        self.swap_plan["l0_b_dcat"] = [("ab_w_o", 0)]
        self.exchange_plan["l0_b_attn"].append(("ab_w_o", 0))
        self.swap_plan["l0_b_du_qkv"] = [("ab_w_in", 0)]
        self.exchange_plan["l0_b_du_rest"] = [("ab_w_in", 0)]
        self._store(first, comm_call("gather_first", self._gather(first, 0.5)))

    def keys(self, i):
        return self.EVEN if i % 2 == 0 else self.ODD

    def stage(self, i):
        return [(k, i) for k in self.keys(i)]

    def mix(self, i):
        return [(k, i) for k in self.keys(i) if not k.startswith("ffn")]

    @staticmethod
    def ffn(i):
        return [("ffn_w_gate", i), ("ffn_w_up", i), ("ffn_w_down", i)]

    @staticmethod
    def index(key, i):
        return i if key.startswith("ffn") else i // 2

    def _layouts(self, items):
        return [self.LAYOUT[k] for k, _ in items]

    def _gather(self, items, middle_frac):
        comm = gather_comm([self.bufs[it] for it in items], self._layouts(items))
        comm.middle_frac = middle_frac
        return comm

    def _store(self, items, bufs):
        for it, b in zip(items, bufs):
            self.bufs[it] = b

    def ride(self, tag):
        if tag in self.gather_plan:
            return self._gather(self.gather_plan[tag], 0.85 if tag == "l0_attn" else 0.7)
        if tag in self.swap_plan:
            items = self.swap_plan[tag]
            return swap_comm([self.grads[it] for it in items], self._layouts(items))
        if tag in self.exchange_plan:
            items = self.exchange_plan[tag]
            return exchange_comm([self.pairs[it][1] for it in items], self._layouts(items))
        return None

    def arrived(self, tag, outs):
        if tag in self.gather_plan:
            self._store(self.gather_plan[tag], outs)
        elif tag in self.swap_plan:
            self._pair_sums(self.swap_plan[tag], outs)
        elif tag in self.exchange_plan:
            self._totals(self.exchange_plan[tag], outs)

    def _pair_sums(self, items, received):
        for it, r in zip(items, received):
            self.pairs[it] = pair_sum(self.grads[it], r, self.LAYOUT[it[0]], "pair_sum_%s_%d" % it)

    def _totals(self, items, got):
        for it, g in zip(items, got):
            k, i = it
            self.reduced[k] = reduce_total(self.pairs[it][0], g, self.LAYOUT[k], self.index(k, i),
                                           self.shards[k].shape[0], self.reduced.get(k), "reduce_total_%s_%d" % it)

    def weights(self, i):
        return _LayerWeights(self, i)

    def weight(self, i, name):
        if (name, i) not in self.ready:
            if name in ("w_qkv", "w_rest"):
                b = self.bufs["ab_w_in", i]
                w_in = jnp.transpose(b, (1, 0, 2)).reshape(b.shape[1], 4 * b.shape[2])
                self.ready["w_qkv", i], self.ready["w_rest", i] = split_w_in(w_in)
            else:
                k = {"w_o": "ab_w_o", "w_glu1": "s5_w_glu1", "w_glu2": "s5_w_glu2"}.get(name, "ffn_" + name)
                b = self.bufs[k, i]
                self.ready[name, i] = b.reshape(4 * b.shape[1], b.shape[2]) if self.LAYOUT[k] == "S" else b
        return self.ready[name, i]

    def put_grads(self, i, wg):
        for k in self.keys(i):
            _, R, C = self.shards[k].shape
            name = {"ab_w_in": "w_qkv", "ab_w_o": "w_o", "s5_w_glu1": "w_glu1", "s5_w_glu2": "w_glu2"}.get(k, k[4:])
            if name not in wg:
                continue
            if k == "ab_w_in":
                dw = merge_dw_in(wg["w_qkv"], wg["w_rest"])
                self.grads[k, i] = jnp.transpose(dw.reshape(R, 4, C), (1, 0, 2))
            else:
                self.grads[k, i] = wg[name].reshape(4, R, C) if self.LAYOUT[k] == "S" else wg[name]

    def finish(self, beside):
        names = list(self.LAYOUT)
        outs = comm_call("share_reduced", merge_comms(share_comm([self.reduced[k] for k in names]), beside))
        return dict(zip(names, outs)), outs[len(names):]


def split_w_in(w_in):
    fg0 = 3 * ATTN_W
    w_rest = jnp.concatenate([w_in[:, fg0 + HEADS:], w_in[:, fg0:fg0 + HEADS],
                              jnp.zeros((w_in.shape[0], LANES - HEADS), w_in.dtype)], axis=1)
    return w_in[:, :fg0], w_rest


def merge_dw_in(dw_qkv, dw_rest):
    nqc = dw_rest.shape[1] - LANES
    return jnp.concatenate([dw_qkv, dw_rest[:, nqc:nqc + HEADS], dw_rest[:, :nqc]], axis=1)


def device_step(x, target, P, stager):
    D = x.shape[-1]
    n_even, n_odd = P["ab_b_f"].shape[0], P["s5_d"].shape[0]
    conv_c = P["ab_conv_b"].shape[1]
    b_f_pad = jnp.pad(P["ab_b_f"], ((0, 0), (0, LANES - HEADS))).reshape(n_even, 1, LANES)

    s5, s5_vjps = [], []
    for j in range(n_odd):
        disc, vjp = jax.vjp(_s5_discretize, P["s5_a_re"][j], P["s5_a_im"][j], P["s5_log_step"][j],
                            P["s5_b_re"][j], P["s5_b_im"][j])
        lb_re, lb_im, bb_re, bb_im = disc
        tab, rtab = _s5_tables(lb_re, lb_im)
        bmat, cmat = _s5_block_mats(bb_re, bb_im, P["s5_c_re"][j], P["s5_c_im"][j])
        s5.append(dict(tab=tab, rtab=rtab, bmat=bmat.astype(BF16), cmat=cmat.astype(BF16),
                       bmat_t=jnp.transpose(bmat, (0, 2, 1)).astype(BF16),
                       cmat_t=jnp.transpose(cmat, (0, 2, 1)).astype(BF16)))
        s5_vjps.append(vjp)

    loss, grad_x, G = local_step(
        x, target, P["meta_tokens"], P["norm_g"], b_f_pad, P["ab_conv_w"],
        P["ab_conv_b"].reshape(n_even, 1, conv_c), s5, P["s5_d"].reshape(n_odd, 1, D), stager)

    out = {
        "meta_tokens": G["meta"],
        "norm_g": jnp.stack(G["norm_g"]),
        "ab_b_f": jnp.stack([b[0, :HEADS] for b in G["b_f"]]),
        "ab_conv_w": jnp.stack(G["conv_w"]),
        "ab_conv_b": jnp.stack([b[0] for b in G["conv_b"]]),
        "s5_d": jnp.stack([d[0] for d in G["s5_d"]]),
    }
    s5g = {k: [] for k in ("s5_a_re", "s5_a_im", "s5_log_step", "s5_b_re", "s5_b_im", "s5_c_re", "s5_c_im")}
    for j in range(n_odd):
        dbb_re, dbb_im, dc_re, dc_im, dl_re, dl_im = _s5_unblock(G["s5_dB"][j], G["s5_dC"][j], G["s5_dlam"][j])
        da_re, da_im, dls, db_re, db_im = s5_vjps[j]((dl_re, dl_im, dbb_re, dbb_im))
        for k, val in zip(s5g, (da_re, da_im, dls, db_re, db_im, dc_re, dc_im)):
            s5g[k].append(val)
    out.update({k: jnp.stack(v) for k, v in s5g.items()})
    return loss, grad_x, out


def kernel(x, meta_tokens, norm_g, ab_w_in, ab_b_f, ab_conv_w, ab_conv_b, ab_w_o, s5_a_re, s5_a_im, s5_log_step, s5_b_re, s5_b_im, s5_c_re, s5_c_im, s5_d, s5_w_glu1, s5_w_glu2, ffn_w_gate, ffn_w_up, ffn_w_down, loss_target, m_meta_tokens, m_norm_g, m_ab_w_in, m_ab_b_f, m_ab_conv_w, m_ab_conv_b, m_ab_w_o, m_s5_a_re, m_s5_a_im, m_s5_log_step, m_s5_b_re, m_s5_b_im, m_s5_c_re, m_s5_c_im, m_s5_d, m_s5_w_glu1, m_s5_w_glu2, m_ffn_w_gate, m_ffn_w_up, m_ffn_w_down, v_meta_tokens, v_norm_g, v_ab_w_in, v_ab_b_f, v_ab_conv_w, v_ab_conv_b, v_ab_w_o, v_s5_a_re, v_s5_a_im, v_s5_log_step, v_s5_b_re, v_s5_b_im, v_s5_c_re, v_s5_c_im, v_s5_d, v_s5_w_glu1, v_s5_w_glu2, v_ffn_w_gate, v_ffn_w_up, v_ffn_w_down):
    names = ["meta_tokens", "norm_g", "ab_w_in", "ab_b_f", "ab_conv_w", "ab_conv_b", "ab_w_o", "s5_a_re", "s5_a_im",
             "s5_log_step", "s5_b_re", "s5_b_im", "s5_c_re", "s5_c_im", "s5_d", "s5_w_glu1", "s5_w_glu2",
             "ffn_w_gate", "ffn_w_up", "ffn_w_down"]
    W = dict(zip(names, [meta_tokens, norm_g, ab_w_in, ab_b_f, ab_conv_w, ab_conv_b, ab_w_o, s5_a_re, s5_a_im,
                         s5_log_step, s5_b_re, s5_b_im, s5_c_re, s5_c_im, s5_d, s5_w_glu1, s5_w_glu2,
                         ffn_w_gate, ffn_w_up, ffn_w_down]))
    Mo = dict(zip(names, [m_meta_tokens, m_norm_g, m_ab_w_in, m_ab_b_f, m_ab_conv_w, m_ab_conv_b, m_ab_w_o, m_s5_a_re,
                          m_s5_a_im, m_s5_log_step, m_s5_b_re, m_s5_b_im, m_s5_c_re, m_s5_c_im, m_s5_d, m_s5_w_glu1,
                          m_s5_w_glu2, m_ffn_w_gate, m_ffn_w_up, m_ffn_w_down]))
    Vo = dict(zip(names, [v_meta_tokens, v_norm_g, v_ab_w_in, v_ab_b_f, v_ab_conv_w, v_ab_conv_b, v_ab_w_o, v_s5_a_re,
                          v_s5_a_im, v_s5_log_step, v_s5_b_re, v_s5_b_im, v_s5_c_re, v_s5_c_im, v_s5_d, v_s5_w_glu1,
                          v_s5_w_glu2, v_ffn_w_gate, v_ffn_w_up, v_ffn_w_down]))
    D = x.shape[-1]
    n_even, n_odd, depth = ab_w_in.shape[0], s5_w_glu1.shape[0], ffn_w_gate.shape[0]
    chip = 2 * lax.axis_index("x") + lax.axis_index("y")

    big = list(MeshStager.LAYOUT)
    stager = MeshStager({k: W[k] for k in big})
    g_meta, g_norm, g_convw, g_s5d = allgather_small([meta_tokens, norm_g, ab_conv_w, s5_d])
    full = {k: W[k] for k in names if k not in big}
    full["meta_tokens"] = jnp.transpose(g_meta, (1, 0, 2)).reshape(N_META, D)
    full["norm_g"] = jnp.transpose(g_norm, (1, 2, 0, 3)).reshape(depth, 4, D)
    full["ab_conv_w"] = jnp.transpose(g_convw, (1, 2, 0, 3)).reshape(n_even, CONV_K, -1)
    full["s5_d"] = jnp.transpose(g_s5d, (1, 0, 2)).reshape(n_odd, D)

    loss, grad_x, G = device_step(x[0], loss_target[0], full, stager)
    small_w = [k for k in names if k not in big]
    small_names = ["loss"] + small_w
    G["loss"] = loss
    packed = _pack([G[k] for k in small_names])
    pair = small_pair_sum(packed, comm_call("small_swap", small_swap_comm(packed))[0], "small_pair_sum")
    reduced, (got,) = stager.finish(small_exchange_comm(pair))
    total = small_chip_sum(pair, got, "small_chip_sum")

    grad, delta, new_m, new_v = {}, {}, {}, {}
    for k in big:
        delta[k], new_m[k], new_v[k] = adamw(W[k], reduced[k], Mo[k], Vo[k], "adamw_" + k)
        grad[k] = reduced[k]
    summed = dict(zip(small_names, _unpack(total, [G[k].shape for k in small_names])))
    loss_out = summed["loss"].reshape(())
    for k in ("meta_tokens", "norm_g", "ab_conv_w", "s5_d"):
        n_last = W[k].shape[-1]
        summed[k] = lax.dynamic_slice_in_dim(summed[k], chip * n_last, n_last, axis=summed[k].ndim - 1)
    shapes = [W[k].shape for k in small_w]
    d_s, m_s, v_s = adamw(_pack([W[k] for k in small_w])[None], _pack([summed[k] for k in small_w])[None],
                          _pack([Mo[k] for k in small_w])[None], _pack([Vo[k] for k in small_w])[None], "adamw_small")
    delta.update(zip(small_w, _unpack(d_s, shapes)))
    new_m.update(zip(small_w, _unpack(m_s, shapes)))
    new_v.update(zip(small_w, _unpack(v_s, shapes)))
    grad.update({k: summed[k] for k in small_w})

    return (loss_out, grad_x[None], *[grad[k] for k in names], *[delta[k] for k in names],
            *[new_m[k] for k in names], *[new_v[k] for k in names])
```

```python
import functools
import math

import jax
import jax.numpy as jnp
from jax import lax
from jax.experimental import pallas as pl
from jax.experimental.pallas import tpu as pltpu

F32 = jnp.float32
BF16 = jnp.bfloat16

N_META = 16
HEADS = 16
HEAD_DIM = 64
ATTN_W = HEADS * HEAD_DIM
CONV_K = 3
S5_GROUP = 16
S5_STATE = 64
S5_MIN_DECAY = 1e-4
NORM_EPS = 1e-6
ADAM_LR = 0.001
ADAM_B1 = 0.9
ADAM_B2 = 0.999
ADAM_EPS = 1e-08
ADAM_WD = 0.01
ADAM_STEP = 10

LANES = 128
SUBLANES = 8
VMEM_LIMIT = 56 * 1024 * 1024
VMEM_TILE_BUDGET = 34 * 1024 * 1024
ROW_TILE = 384
ATTN_ROWS = 384
S5_BLOCK_GROUPS = LANES // S5_GROUP
S5_BLOCK_STATES = S5_BLOCK_GROUPS * S5_STATE
NEG_BIG = -1e30

MESH = pl.DeviceIdType.MESH
ANY = pl.BlockSpec(memory_space=pl.ANY)
VMEM_SPEC = pl.BlockSpec(memory_space=pltpu.VMEM)


def _params(sem=None):
    return pltpu.CompilerParams(dimension_semantics=sem, vmem_limit_bytes=VMEM_LIMIT)


def _div_tile(n, prefs):
    for p in prefs:
        if n % p == 0:
            return p
    return n


def _row_tile(rows, cols, itemsize=4, limit=2 * 1024 * 1024):
    for p in (512, 256, 128, 64, 32, 16):
        if rows % p == 0 and p * cols * itemsize <= limit:
            return p
    return 16 if rows % 16 == 0 else rows


def _tile_cands(n):
    c = [d for d in range(LANES, min(n, 2048) + 1, LANES) if n % d == 0]
    if not c or n <= 2048 and n not in c:
        c.append(n)
    return sorted(set(c), reverse=True)


def _mm_tiles(M, N, K, a_bytes, b_bytes, o_bytes, npairs):
    best = None
    for tk in sorted(set(_tile_cands(K) + [K]), reverse=True):
        for tm in _tile_cands(M):
            for tn in _tile_cands(N):
                mem = npairs * 2 * (tm * tk * a_bytes + tk * tn * b_bytes) + 2 * tm * tn * o_bytes + tm * tn * 4
                mem += npairs * ((tm * tk * 2 if a_bytes == 4 else 0) + (tk * tn * 2 if b_bytes == 4 else 0))
                if mem > VMEM_TILE_BUDGET:
                    continue
                key = (tk == K and tm >= 3 * LANES and tn >= 4 * LANES, tm * tn * tk, tk, tn)
                if best is None or key > best[0]:
                    best = (key, (tm, tn, tk))
    assert best is not None, (M, N, K)
    return best[1]


class Comm:
    def __init__(self, operands, out_shapes, aliases, n_sems, begin, middle=None, finish=None, middle_frac=0.5):
        self.operands, self.out_shapes, self.aliases, self.n_sems = list(operands), list(out_shapes), aliases, n_sems
        self.begin, self.middle, self.finish, self.middle_frac = begin, middle, finish, middle_frac


class _Shifted:
    def __init__(self, sems, off):
        self.sems, self.off = sems, off

    @property
    def at(self):
        return self

    def __getitem__(self, i):
        return self.sems.at[self.off + i]


def merge_comms(a, b):
    assert a.middle is None and b.middle is None
    na_in, na_out = len(a.operands), len(a.out_shapes)

    def both(stage):
        def run(ins, outs, send_sems, recv_sems):
            getattr(a, stage)(ins[:na_in], outs[:na_out], send_sems, recv_sems)
            getattr(b, stage)(ins[na_in:], outs[na_out:], _Shifted(send_sems, a.n_sems), _Shifted(recv_sems, a.n_sems))
        return run

    aliases = dict(a.aliases)
    aliases.update({na_in + i: na_out + o for i, o in b.aliases.items()})
    return Comm(a.operands + b.operands, a.out_shapes + b.out_shapes, aliases, a.n_sems + b.n_sems,
                both("begin"), None, both("finish"))


def carrier_call(body, name, grid, in_specs, out_specs, out_shape, scratch_shapes, args, comm, semantics):
    n_in, n_out = len(args), len(out_shape)
    if comm is None:
        outs = pl.pallas_call(body, name=name, grid=grid, in_specs=in_specs, out_specs=out_specs, out_shape=out_shape,
                              scratch_shapes=scratch_shapes, compiler_params=_params(semantics))(*args)
        return list(outs), []
    ci, co = len(comm.operands), len(comm.out_shapes)
    total = math.prod(grid)
    middle_at = min(total - 1, max(0, int(total * comm.middle_frac)))

    def carried(*refs):
        ins, cins = refs[:n_in], refs[n_in:n_in + ci]
        outs = refs[n_in + ci:n_in + ci + n_out]
        couts = refs[n_in + ci + n_out:n_in + ci + n_out + co]
        scratch, (send_sems, recv_sems) = refs[n_in + ci + n_out + co:-2], refs[-2:]
        step = 0
        for d, size in enumerate(grid):
            step = step * size + pl.program_id(d)

        @pl.when(step == 0)
        def _():
            comm.begin(cins, couts, send_sems, recv_sems)

        if comm.middle is not None:
            @pl.when(step == middle_at)
            def _():
                comm.middle(cins, couts, send_sems, recv_sems)

        body(*ins, *outs, *scratch)

        @pl.when(step == total - 1)
        def _():
            comm.finish(cins, couts, send_sems, recv_sems)

    outs = pl.pallas_call(
        carried, name=name, grid=grid,
        in_specs=list(in_specs) + [ANY] * ci, out_specs=list(out_specs) + [ANY] * co,
        out_shape=list(out_shape) + comm.out_shapes,
        scratch_shapes=list(scratch_shapes) + [pltpu.SemaphoreType.DMA((comm.n_sems,)),
                                                pltpu.SemaphoreType.DMA((comm.n_sems,))],
        input_output_aliases={n_in + i: n_out + o for i, o in comm.aliases.items()},
        compiler_params=pltpu.CompilerParams(dimension_semantics=("arbitrary",) * len(grid),
                                             vmem_limit_bytes=VMEM_LIMIT, has_side_effects=True),
    )(*args, *comm.operands)
    return list(outs[:n_out]), list(outs[n_out:])


def comm_call(name, comm):
    ci = len(comm.operands)

    def body(*refs):
        cins, couts = refs[:ci], refs[ci:ci + len(comm.out_shapes)]
        send_sems, recv_sems = refs[-2:]
        comm.begin(cins, couts, send_sems, recv_sems)
        if comm.middle is not None:
            comm.middle(cins, couts, send_sems, recv_sems)
        comm.finish(cins, couts, send_sems, recv_sems)

    return pl.pallas_call(
        body, name=name, in_specs=[ANY] * ci, out_specs=[ANY] * len(comm.out_shapes), out_shape=comm.out_shapes,
        input_output_aliases=dict(comm.aliases),
        scratch_shapes=[pltpu.SemaphoreType.DMA((comm.n_sems,)), pltpu.SemaphoreType.DMA((comm.n_sems,))],
        compiler_params=pltpu.CompilerParams(has_side_effects=True),
    )(*comm.operands)


_DIMS ={"nn": (((1,), (0,)), ((), ())), "nt": (((1,), (1,)), ((), ())), "tn": (((0,), (0,)), ((), ()))}


def matmul(pairs, kind, out_dtype, name, comm=None, add=None):
    a0, b0 = pairs[0]
    if kind == "nn":
        (M, K), N = a0.shape, b0.shape[1]
    elif kind == "nt":
        (M, K), N = a0.shape, b0.shape[0]
    else:
        (K, M), N = a0.shape, b0.shape[1]
    tm, tn, tk = _mm_tiles(M, N, K, a0.dtype.itemsize, b0.dtype.itemsize, jnp.dtype(out_dtype).itemsize, len(pairs))
    nk = K // tk
    dims = _DIMS[kind]
    npairs = len(pairs)
    n_in = 2 * npairs + (add is not None)

    def body(*refs):
        ins, o_ref = refs[:2 * npairs], refs[n_in]
        part = None
        for p in range(npairs):
            d = lax.dot_general(ins[2 * p][...].astype(BF16), ins[2 * p + 1][...].astype(BF16), dims,
                                preferred_element_type=F32)
            part = d if part is None else part + d

        def finish(total):
            if add is not None:
                total = total + refs[2 * npairs][...]
            o_ref[...] = total.astype(o_ref.dtype)

        if nk == 1:
            finish(part)
        else:
            acc_ref = refs[n_in + 1]
            k = pl.program_id(2)

            @pl.when(k == 0)
            def _():
                acc_ref[...] = part

            @pl.when(k > 0)
            def _():
                acc_ref[...] += part

            @pl.when(k == nk - 1)
            def _():
                finish(acc_ref[...])

    if kind == "nn":
        a_blk, a_map = (tm, tk), lambda j, i, k: (i, k)
        b_blk, b_map = (tk, tn), lambda j, i, k: (k, j)
    elif kind == "nt":
        a_blk, a_map = (tm, tk), lambda j, i, k: (i, k)
        b_blk, b_map = (tn, tk), lambda j, i, k: (j, k)
    else:
        a_blk, a_map = (tk, tm), lambda j, i, k: (k, i)
        b_blk, b_map = (tk, tn), lambda j, i, k: (k, j)
    o_spec = pl.BlockSpec((tm, tn), lambda j, i, k: (i, j))
    (out,), arrived = carrier_call(
        body, name, (N // tn, M // tm, nk),
        [pl.BlockSpec(a_blk, a_map), pl.BlockSpec(b_blk, b_map)] * npairs + ([o_spec] if add is not None else []),
        [o_spec], [jax.ShapeDtypeStruct((M, N), out_dtype)],
        [] if nk == 1 else [pltpu.VMEM((tm, tn), F32)],
        [t for ab in pairs for t in ab] + ([add] if add is not None else []), comm,
        ("parallel", "parallel", "arbitrary"))
    return out if comm is None else ([out], arrived)


def _sigmoid(x):
    return 1.0 / (1.0 + jnp.exp(-x))

def dual_matmul_act(x, w1, w2, act, out_dtype, name, comm=None):
    M, K = x.shape
    N = w1.shape[-1]
    tm = _div_tile(M, (ROW_TILE,))
    tn = _div_tile(N, (1408, 1024, 512, 256, 128))

    def body(x_ref, w1_ref, w2_ref, o1_ref, o2_ref, out_ref):
        xv = x_ref[...]
        o1 = jnp.dot(xv, w1_ref[...], preferred_element_type=F32)
        o2 = jnp.dot(xv, w2_ref[...], preferred_element_type=F32)
        o1_ref[...] = o1.astype(BF16)
        o2_ref[...] = o2.astype(BF16)
        if act == "swiglu":
            out = o1 * _sigmoid(o1) * o2
        else:
            out = o1 * _sigmoid(o2)
        out_ref[...] = out.astype(out_ref.dtype)

    w_spec = pl.BlockSpec((K, tn), lambda j, i: (0, j))
    o_spec = pl.BlockSpec((tm, tn), lambda j, i: (i, j))
    return carrier_call(
        body, name, (N // tn, M // tm), [pl.BlockSpec((tm, K), lambda j, i: (i, 0)), w_spec, w_spec],
        [o_spec, o_spec, o_spec],
        [jax.ShapeDtypeStruct((M, N), BF16), jax.ShapeDtypeStruct((M, N), BF16),
         jax.ShapeDtypeStruct((M, N), out_dtype)], [], (x, w1, w2), comm, ("parallel", "parallel"))


def ffn_bwd_act(df, wd, a, b, name, comm=None):
    M, K = df.shape
    N = wd.shape[0]
    tm = _div_tile(M, (ROW_TILE,))
    tn = _div_tile(N, (1408, 1024, 512, 256, 128))

    def body(df_ref, wd_ref, a_ref, b_ref, da_ref, db_ref):
        dh = lax.dot_general(df_ref[...], wd_ref[...], _DIMS["nt"], preferred_element_type=F32)
        av = a_ref[...].astype(F32)
        bv = b_ref[...].astype(F32)
        sig = _sigmoid(av)
        silu = av * sig
        da_ref[...] = (dh * bv * (sig + silu * (1.0 - sig))).astype(BF16)
        db_ref[...] = (dh * silu).astype(BF16)

    t_spec = pl.BlockSpec((tm, tn), lambda j, i: (i, j))
    return carrier_call(
        body, name, (N // tn, M // tm),
        [pl.BlockSpec((tm, K), lambda j, i: (i, 0)), pl.BlockSpec((tn, K), lambda j, i: (j, 0)), t_spec, t_spec],
        [t_spec, t_spec], [jax.ShapeDtypeStruct((M, N), BF16)] * 2, [], (df, wd, a, b), comm,
        ("parallel", "parallel"))


def glu_bwd_act(dout, o1, o2, name):
    M, N = dout.shape
    tm = _div_tile(M, (ROW_TILE,))

    def body(d_ref, o1_ref, o2_ref, d1_ref, d2_ref):
        d = d_ref[...].astype(F32)
        sig = _sigmoid(o2_ref[...].astype(F32))
        d1_ref[...] = (d * sig).astype(BF16)
        d2_ref[...] = (d * o1_ref[...].astype(F32) * sig * (1.0 - sig)).astype(BF16)

    spec = pl.BlockSpec((tm, N), lambda i: (i, 0))
    return pl.pallas_call(
        body, name=name, grid=(M // tm,), in_specs=[spec] * 3, out_specs=[spec] * 2,
        out_shape=[jax.ShapeDtypeStruct((M, N), BF16)] * 2,
        compiler_params=_params(("parallel",)),
    )(dout, o1, o2)


def rmsnorm_fwd(x, g, out_dtype, name, residual=None):
    L, D = x.shape
    tr = _div_tile(L, (ROW_TILE,))
    has_res = residual is not None

    def body(*refs):
        x_ref, g_ref = refs[0], refs[1]
        o_ref = refs[-1]
        xv = x_ref[...]
        r = lax.rsqrt(jnp.mean(xv * xv, axis=-1, keepdims=True) + NORM_EPS)
        y = xv * r * g_ref[...]
        if has_res:
            y = refs[2][...] + y
        o_ref[...] = y.astype(o_ref.dtype)

    row = pl.BlockSpec((tr, D), lambda i: (i, 0))
    gsp = pl.BlockSpec((1, D), lambda i: (0, 0))
    args = (x, g) + ((residual,) if has_res else ())
    return pl.pallas_call(
        body, name=name, grid=(L // tr,), in_specs=[row, gsp] + ([row] if has_res else []), out_specs=row,
        out_shape=jax.ShapeDtypeStruct((L, D), out_dtype), compiler_params=_params(("parallel",)),
    )(*args)


def rmsnorm_residual_norm(x, g, residual, g_next, next_dtype, name):
    L, D = x.shape
    tr = _div_tile(L, (ROW_TILE,))

    def body(x_ref, g_ref, r_ref, gn_ref, h_ref, u_ref):
        xv = x_ref[...]
        r = lax.rsqrt(jnp.mean(xv * xv, axis=-1, keepdims=True) + NORM_EPS)
        h = r_ref[...] + xv * r * g_ref[...]
        h_ref[...] = h
        rn = lax.rsqrt(jnp.mean(h * h, axis=-1, keepdims=True) + NORM_EPS)
        u_ref[...] = (h * rn * gn_ref[...]).astype(u_ref.dtype)

    row = pl.BlockSpec((tr, D), lambda i: (i, 0))
    gsp = pl.BlockSpec((1, D), lambda i: (0, 0))
    return pl.pallas_call(
        body, name=name, grid=(L // tr,), in_specs=[row, gsp, row, gsp], out_specs=[row, row],
        out_shape=[jax.ShapeDtypeStruct((L, D), F32), jax.ShapeDtypeStruct((L, D), next_dtype)],
        compiler_params=_params(("parallel",)),
    )(x, g, residual, g_next)


def rmsnorm_bwd(x, g, dy, out_dtype, name, add=None, dy2=None):
    L, D = x.shape
    tr = _div_tile(L, (ROW_TILE,))
    has_add = add is not None
    has_dy2 = dy2 is not None

    def body(*refs):
        x_ref, g_ref, dy_ref = refs[0], refs[1], refs[2]
        dx_ref, dg_ref = refs[-2], refs[-1]
        xv = x_ref[...]
        dyv = dy_ref[...].astype(F32)
        if has_dy2:
            dyv = dyv + refs[3][...].astype(F32)
        r = lax.rsqrt(jnp.mean(xv * xv, axis=-1, keepdims=True) + NORM_EPS)
        t = dyv * g_ref[...]
        dx = r * t - xv * (r * r * r) * jnp.mean(xv * t, axis=-1, keepdims=True)
        if has_add:
            dx = refs[3 + has_dy2][...] + dx
        dx_ref[...] = dx.astype(dx_ref.dtype)
        dgp = jnp.sum(dyv * xv * r, axis=0, keepdims=True)

        @pl.when(pl.program_id(0) == 0)
        def _():
            dg_ref[...] = dgp

        @pl.when(pl.program_id(0) > 0)
        def _():
            dg_ref[...] += dgp

    row = pl.BlockSpec((tr, D), lambda i: (i, 0))
    gsp = pl.BlockSpec((1, D), lambda i: (0, 0))
    args = (x, g, dy) + ((dy2,) if has_dy2 else ()) + ((add,) if has_add else ())
    return pl.pallas_call(
        body, name=name, grid=(L // tr,), in_specs=[row, gsp] + [row] * (len(args) - 2),
        out_specs=[row, gsp],
        out_shape=[jax.ShapeDtypeStruct((L, D), out_dtype), jax.ShapeDtypeStruct((1, D), F32)],
        compiler_params=_params(("arbitrary",)),
    )(*args)


def _gate_z(fg_ref, b_ref):
    return fg_ref[...] + b_ref[...]


def gate_fwd(fg_src, col_block, b, name):
    L = fg_src.shape[0]
    T = _div_tile(L, (ROW_TILE,))

    def body(fg_ref, b_ref, c_ref, carry):
        @pl.when(pl.program_id(0) == 0)
        def _():
            carry[...] = jnp.zeros_like(carry)

        z = _gate_z(fg_ref, b_ref)
        logf = jnp.minimum(z, 0.0) - jnp.log(1.0 + jnp.exp(-jnp.abs(z)))
        tri = (lax.broadcasted_iota(jnp.int32, (T, T), 1) <= lax.broadcasted_iota(jnp.int32, (T, T), 0)).astype(F32)
        c = jnp.dot(tri, logf, precision=lax.Precision.HIGHEST, preferred_element_type=F32) + carry[...]
        c_ref[...] = c
        carry[...] = c[T - 1:T, :]

    return pl.pallas_call(
        body, name=name, grid=(L // T,),
        in_specs=[pl.BlockSpec((T, LANES), lambda i: (i, col_block)), pl.BlockSpec((1, LANES), lambda i: (0, 0))],
        out_specs=pl.BlockSpec((T, LANES), lambda i: (i, 0)),
        out_shape=jax.ShapeDtypeStruct((L, LANES), F32),
        scratch_shapes=[pltpu.VMEM((1, LANES), F32)],
        compiler_params=_params(("arbitrary",)),
    )(fg_src, b)


def gate_bwd(fg_src, col_block, b, dc, name):
    L = fg_src.shape[0]
    T = _div_tile(L, (ROW_TILE,))
    nb = L // T

    def body(fg_ref, b_ref, dc_ref, dfg_ref, db_ref, carry):
        @pl.when(pl.program_id(0) == 0)
        def _():
            carry[...] = jnp.zeros_like(carry)
            db_ref[...] = jnp.zeros_like(db_ref)

        z = _gate_z(fg_ref, b_ref)
        dcv = dc_ref[...]
        tri = (lax.broadcasted_iota(jnp.int32, (T, T), 1) >= lax.broadcasted_iota(jnp.int32, (T, T), 0)).astype(F32)
        dlogf = jnp.dot(tri, dcv, precision=lax.Precision.HIGHEST, preferred_element_type=F32) + carry[...]
        dfg = dlogf * _sigmoid(-z)
        dfg_ref[...] = dfg
        db_ref[...] += jnp.sum(dfg, axis=0, keepdims=True)
        carry[...] = dlogf[0:1, :]

    return pl.pallas_call(
        body, name=name, grid=(nb,),
        in_specs=[pl.BlockSpec((T, LANES), lambda i: (nb - 1 - i, col_block)),
                  pl.BlockSpec((1, LANES), lambda i: (0, 0)),
                  pl.BlockSpec((T, LANES), lambda i: (nb - 1 - i, 0))],
        out_specs=[pl.BlockSpec((T, LANES), lambda i: (nb - 1 - i, 0)), pl.BlockSpec((1, LANES), lambda i: (0, 0))],
        out_shape=[jax.ShapeDtypeStruct((L, LANES), F32), jax.ShapeDtypeStruct((1, LANES), F32)],
        scratch_shapes=[pltpu.VMEM((1, LANES), F32)],
        compiler_params=_params(("arbitrary",)),
    )(fg_src, b, dc)


def attn_fwd(proj, cq_col, ck_row, name, comm=None):
    L = proj.shape[0]
    T = _div_tile(L, (ROW_TILE,))
    nq = L // T
    npair = HEADS // 2
    scale = HEAD_DIM ** -0.5
    SUB = ATTN_ROWS
    nsub = T // SUB

    def body(q_ref, k_ref, v_ref, cq_ref, ck_ref, o_ref, lse_ref):
        qb = pl.program_id(1)
        rows = [slice(r * SUB, (r + 1) * SUB) for r in range(nsub)]
        head1 = lax.broadcasted_iota(jnp.int32, (SUB, LANES), 1) >= HEAD_DIM
        qs = [[jnp.where(head1 == (h == 1), q_ref[rs, :] * scale, 0.0).astype(BF16) for rs in rows] for h in range(2)]
        cqs = [[cq_ref[0, rs, h:h + 1] for rs in rows] for h in range(2)]

        def logits(kb):
            ks = pl.multiple_of(kb * T, T)
            k = k_ref[pl.ds(ks, T), :]
            return tuple(lax.dot_general(qs[h][r], k, _DIMS["nt"], preferred_element_type=F32) + cqs[h][r]
                         - ck_ref[0, h:h + 1, pl.ds(ks, T)] for h in range(2) for r in range(nsub))

        def softmax_step(kb, s_all, carry, masked):
            ks = pl.multiple_of(kb * T, T)
            v = v_ref[pl.ds(ks, T), :]
            lane = lax.broadcasted_iota(jnp.int32, (T, LANES), 1)
            new = []
            for h in range(2):
                vh = jnp.where(lane == spare[h], 1.0, v).astype(BF16)
                for r in range(nsub):
                    m, acc = carry[h * nsub + r]
                    s, vr = s_all[h * nsub + r], vh
                    if masked:
                        n = (r + 1) * SUB
                        s, vr = s[:, :n], vh[:n]
                        keep = (lax.broadcasted_iota(jnp.int32, (SUB, n), 1)
                                <= lax.broadcasted_iota(jnp.int32, (SUB, n), 0) + r * SUB)
                        s = jnp.where(keep, s, NEG_BIG)
                    m_new = jnp.maximum(m, jnp.max(s, axis=1, keepdims=True))
                    p = jnp.exp(s - m_new)
                    acc = jnp.exp(m - m_new) * acc + jnp.dot(p.astype(BF16), vr, preferred_element_type=F32)
                    new.append((m_new, acc))
            return tuple(new)

        def step(kb, state):
            s_all, carry = state
            s_next = logits(kb + 1)
            return s_next, softmax_step(kb, s_all, carry, False)

        spare = (HEAD_DIM, 0)
        one = (jnp.full((SUB, 1), NEG_BIG, F32), jnp.zeros((SUB, LANES), F32))
        s_all, carry = lax.fori_loop(0, qb, step, (logits(0), (one,) * (2 * nsub)))
        carry = softmax_step(qb, s_all, carry, True)
        out, lse = [], []
        for h in range(2):
            chains = carry[h * nsub:(h + 1) * nsub]
            ls = [acc[:, spare[h]:spare[h] + 1] for _, acc in chains]
            out.append(jnp.concatenate([acc / l for (_, acc), l in zip(chains, ls)], axis=0))
            lse.append(jnp.concatenate([m + jnp.log(l) for (m, _), l in zip(chains, ls)], axis=0))
        o_ref[...] = jnp.where(lax.broadcasted_iota(jnp.int32, (T, LANES), 1) >= HEAD_DIM, out[1], out[0]
                               ).astype(o_ref.dtype)
        lse_ref[0] = jnp.concatenate(lse, axis=1)

    return carrier_call(
        body, name, (npair, nq),
        [pl.BlockSpec((T, LANES), lambda p, i: (i, p)),
         pl.BlockSpec((L, LANES), lambda p, i: (0, npair + p)),
         pl.BlockSpec((L, LANES), lambda p, i: (0, 2 * npair + p)),
         pl.BlockSpec((1, T, 2), lambda p, i: (p, i, 0)),
         pl.BlockSpec((1, 2, L), lambda p, i: (p, 0, 0))],
        [pl.BlockSpec((T, LANES), lambda p, i: (i, p)), pl.BlockSpec((1, T, 2), lambda p, i: (p, i, 0))],
        [jax.ShapeDtypeStruct((L, ATTN_W), BF16), jax.ShapeDtypeStruct((npair, L, 2), F32)],
        [], (proj, proj, proj, cq_col, ck_row), comm, ("parallel", "parallel"))


def attn_delta(dcat, cat, name):
    L = dcat.shape[0]
    T = _div_tile(L, (ROW_TILE,))

    def body(do_ref, o_ref, d_ref):
        prod = do_ref[...] * o_ref[...].astype(F32)
        sel = (lax.broadcasted_iota(jnp.int32, (ATTN_W, LANES), 0) // HEAD_DIM
               == lax.broadcasted_iota(jnp.int32, (ATTN_W, LANES), 1)).astype(F32)
        d_ref[...] = jnp.dot(prod, sel, precision=lax.Precision.HIGHEST, preferred_element_type=F32)

    return pl.pallas_call(
        body, name=name, grid=(L // T,),
        in_specs=[pl.BlockSpec((T, ATTN_W), lambda i: (i, 0)), pl.BlockSpec((T, ATTN_W), lambda i: (i, 0))],
        out_specs=pl.BlockSpec((T, LANES), lambda i: (i, 0)),
        out_shape=jax.ShapeDtypeStruct((L, LANES), F32),
        compiler_params=_params(("parallel",)),
    )(dcat, cat)


def attn_bwd(proj, dcat, lse_row, delta_row, cq_row, ck_col, name, comm=None):
    L = proj.shape[0]
    T = _div_tile(L, (ROW_TILE,))
    nb = L // T
    npair = HEADS // 2
    scale = HEAD_DIM ** -0.5

    def body(q_ref, k_ref, v_ref, do_ref, lse_ref, dl_ref, cq_ref, ck_ref,
             dq_ref, dk_ref, dv_ref, dcq_ref, dck_ref, dq_acc, dcq_acc):
        kb = pl.program_id(1)

        @pl.when(kb == 0)
        def _():
            dq_acc[...] = jnp.zeros_like(dq_acc)
            dcq_acc[...] = jnp.zeros_like(dcq_acc)

        head1 = lax.broadcasted_iota(jnp.int32, (T, LANES), 1) >= HEAD_DIM
        ks = [jnp.where(head1 == (h == 1), k_ref[...] * scale, 0.0).astype(BF16) for h in range(2)]
        vs = [jnp.where(head1 == (h == 1), v_ref[...], 0.0).astype(BF16) for h in range(2)]
        cks = [ck_ref[0, :, h:h + 1] for h in range(2)]
        kts = [k.T for k in ks]

        def step(qb, carry, masked):
            qs = pl.multiple_of(qb * T, T)
            q = q_ref[pl.ds(qs, T), :]
            do = do_ref[pl.ds(qs, T), :].astype(BF16)
            new, dq = [], None
            for h in range(2):
                dk, dv, dck = carry[h]
                lse = lse_ref[0, h:h + 1, pl.ds(qs, T)]
                dl = dl_ref[0, h:h + 1, pl.ds(qs, T)]
                cq = cq_ref[0, h:h + 1, pl.ds(qs, T)]
                st = lax.dot_general(ks[h], q, _DIMS["nt"], preferred_element_type=F32) + cq - cks[h]
                if masked:
                    keep = lax.broadcasted_iota(jnp.int32, (T, T), 0) <= lax.broadcasted_iota(jnp.int32, (T, T), 1)
                    st = jnp.where(keep, st, NEG_BIG)
                pt = jnp.exp(st - lse)
                dv = dv + jnp.dot(pt.astype(BF16), do, preferred_element_type=F32)
                dpt = lax.dot_general(vs[h], do, _DIMS["nt"], preferred_element_type=F32)
                dst = pt * (dpt - dl)
                dsb = dst.astype(BF16)
                dk = dk + jnp.dot(dsb, q, preferred_element_type=F32)
                part = jnp.dot(kts[h], dsb, preferred_element_type=F32)
                dq = part if dq is None else dq + part
                dcq_acc[h:h + 1, pl.ds(qs, T)] += jnp.sum(dst, axis=0, keepdims=True)
                dck = dck + jnp.sum(dst, axis=1, keepdims=True)
                new.append((dk, dv, dck))
            dq_acc[:, pl.ds(qs, T)] += dq
            return tuple(new)

        one = (jnp.zeros((T, LANES), F32), jnp.zeros((T, LANES), F32), jnp.zeros((T, 1), F32))
        carry = step(kb, (one, one), True)
        carry = lax.fori_loop(kb + 1, nb, functools.partial(step, masked=False), carry)
        (dk0, dv0, dck0), (dk1, dv1, dck1) = carry
        dk_ref[...] = (jnp.where(head1, dk1, dk0) * scale).astype(dk_ref.dtype)
        dv_ref[...] = jnp.where(head1, dv1, dv0).astype(dv_ref.dtype)
        dck_ref[0] = jnp.concatenate([-dck0, -dck1], axis=1)

        @pl.when(kb == nb - 1)
        def _():
            dq_ref[...] = dq_acc[...].T.astype(dq_ref.dtype)
            dcq_ref[0] = dcq_acc[...]

    full = lambda col: pl.BlockSpec((L, LANES), col)
    row_stat = pl.BlockSpec((1, 2, L), lambda p, i: (p, 0, 0))
    return carrier_call(
        body, name, (npair, nb),
        [full(lambda p, i: (0, p)),
         pl.BlockSpec((T, LANES), lambda p, i: (i, npair + p)),
         pl.BlockSpec((T, LANES), lambda p, i: (i, 2 * npair + p)),
         full(lambda p, i: (0, p)),
         row_stat, row_stat, row_stat,
         pl.BlockSpec((1, T, 2), lambda p, i: (p, i, 0))],
        [full(lambda p, i: (0, p)),
         pl.BlockSpec((T, LANES), lambda p, i: (i, p)),
         pl.BlockSpec((T, LANES), lambda p, i: (i, p)),
         row_stat,
         pl.BlockSpec((1, T, 2), lambda p, i: (p, i, 0))],
        [jax.ShapeDtypeStruct((L, ATTN_W), BF16)] * 3
        + [jax.ShapeDtypeStruct((npair, 2, L), F32), jax.ShapeDtypeStruct((npair, L, 2), F32)],
        [pltpu.VMEM((LANES, L), F32), pltpu.VMEM((2, L), F32)],
        (proj, proj, proj, dcat, lse_row, delta_row, cq_row, ck_col), comm, ("parallel", "arbitrary"))


def _shift_down(x, k):
    rolled = pltpu.roll(x, k, 0)
    return jnp.where(lax.broadcasted_iota(jnp.int32, x.shape, 0) >= k, rolled, 0.0)


def _shift_up(x, k):
    n = x.shape[0]
    rolled = pltpu.roll(x, n - k, 0)
    return jnp.where(lax.broadcasted_iota(jnp.int32, x.shape, 0) < n - k, rolled, 0.0)


def conv_fwd(proj, col0, conv_w, conv_b, name):
    L = proj.shape[0]
    C = conv_w.shape[1]
    nc = C // LANES

    def body(gb_ref, gc_ref, xc_ref, w_ref, b_ref, o_ref):
        z = gc_ref[...] * xc_ref[...]
        conv = (w_ref[0:1, :] * _shift_down(z, 2) + w_ref[1:2, :] * _shift_down(z, 1) + w_ref[2:3, :] * z
                + b_ref[...])
        o_ref[...] = (gb_ref[...] * conv).astype(o_ref.dtype)

    col = lambda off: pl.BlockSpec((L, LANES), lambda j, off=off: (0, col0 + off + j))
    return pl.pallas_call(
        body, name=name, grid=(nc,),
        in_specs=[col(0), col(nc), col(2 * nc), pl.BlockSpec((CONV_K, LANES), lambda j: (0, j)),
                  pl.BlockSpec((1, LANES), lambda j: (0, j))],
        out_specs=pl.BlockSpec((L, LANES), lambda j: (0, j)),
        out_shape=jax.ShapeDtypeStruct((L, C), BF16),
        compiler_params=_params(("parallel",)),
    )(proj, proj, proj, conv_w, conv_b)


def conv_bwd(proj, col0, conv_w, conv_b, dcat, dcol0, name):
    L = proj.shape[0]
    C = conv_w.shape[1]
    nc = C // LANES

    def body(gb_ref, gc_ref, xc_ref, w_ref, b_ref, do_ref, dgb_ref, dgc_ref, dxc_ref, dw_ref, db_ref):
        gc, xc = gc_ref[...], xc_ref[...]
        z = gc * xc
        z1, z2 = _shift_down(z, 1), _shift_down(z, 2)
        w0, w1, w2 = w_ref[0:1, :], w_ref[1:2, :], w_ref[2:3, :]
        conv = w0 * z2 + w1 * z1 + w2 * z + b_ref[...]
        dout = do_ref[...]
        dgb_ref[...] = (dout * conv).astype(dgb_ref.dtype)
        dconv = dout * gb_ref[...]
        dw_ref[...] = jnp.concatenate([jnp.sum(dconv * z2, axis=0, keepdims=True),
                                       jnp.sum(dconv * z1, axis=0, keepdims=True),
                                       jnp.sum(dconv * z, axis=0, keepdims=True)], axis=0)
        db_ref[...] = jnp.sum(dconv, axis=0, keepdims=True)
        dz = w2 * dconv + w1 * _shift_up(dconv, 1) + w0 * _shift_up(dconv, 2)
        dgc_ref[...] = (dz * xc).astype(dgc_ref.dtype)
        dxc_ref[...] = (dz * gc).astype(dxc_ref.dtype)

    col = lambda off: pl.BlockSpec((L, LANES), lambda j, off=off: (0, col0 + off + j))
    out_col = pl.BlockSpec((L, LANES), lambda j: (0, j))
    return pl.pallas_call(
        body, name=name, grid=(nc,),
        in_specs=[col(0), col(nc), col(2 * nc), pl.BlockSpec((CONV_K, LANES), lambda j: (0, j)),
                  pl.BlockSpec((1, LANES), lambda j: (0, j)),
                  pl.BlockSpec((L, LANES), lambda j: (0, dcol0 + j))],
        out_specs=[out_col, out_col, out_col, pl.BlockSpec((CONV_K, LANES), lambda j: (0, j)),
                   pl.BlockSpec((1, LANES), lambda j: (0, j))],
        out_shape=[jax.ShapeDtypeStruct((L, C), BF16)] * 3
        + [jax.ShapeDtypeStruct((CONV_K, C), F32), jax.ShapeDtypeStruct((1, C), F32)],
        compiler_params=_params(("parallel",)),
    )(proj, proj, proj, conv_w, conv_b, dcat)


_GELU_C = math.sqrt(2.0 / math.pi)
_GELU_A = 0.044715


def _gelu(y):
    return 0.5 * y * (1.0 + jnp.tanh(_GELU_C * (y + _GELU_A * y * y * y)))


def _gelu_grad(y):
    t = jnp.tanh(_GELU_C * (y + _GELU_A * y * y * y))
    return 0.5 * (1.0 + t) + 0.5 * y * (1.0 - t * t) * _GELU_C * (1.0 + 3.0 * _GELU_A * y * y)


def _cmul_add(xr, xi, pr, pi, sr, si):
    return xr + pr * sr - pi * si, xi + pr * si + pi * sr


def _scan_tile(br, bi, cr, ci, tab_ref, reverse):
    n = S5_BLOCK_STATES
    xr, xi = br, bi
    for s, k in enumerate((1, 2, 4)):
        shift = SUBLANES - k if reverse else k
        xr, xi = _cmul_add(xr, xi, tab_ref[0, s, :, :n], tab_ref[0, s, :, n:],
                           pltpu.roll(xr, shift, 0), pltpu.roll(xi, shift, 0))
    return _cmul_add(xr, xi, tab_ref[0, 3, :, :n], tab_ref[0, 3, :, n:], cr, ci)


def s5_fwd(u, bmat, cmat, tab, dvec, name, comm=None):
    L, D = u.shape
    nblk = D // LANES
    T = _div_tile(L, (ROW_TILE,))
    ns = 2 * S5_BLOCK_STATES
    n = S5_BLOCK_STATES

    def body(u_ref, b_ref, c_ref, tab_ref, d_ref, y_ref, g_ref, xs_ref, buf, car):
        @pl.when(pl.program_id(1) == 0)
        def _():
            car[...] = jnp.zeros_like(car)

        uv = u_ref[...]
        buf[...] = jnp.dot(uv.astype(BF16), b_ref[0], preferred_element_type=F32)

        def tile(i, carry):
            cr, ci = carry
            r0 = pl.multiple_of(i * SUBLANES, SUBLANES)
            xr, xi = _scan_tile(buf[pl.ds(r0, SUBLANES), :n], buf[pl.ds(r0, SUBLANES), n:], cr, ci, tab_ref, False)
            buf[pl.ds(r0, SUBLANES), :n] = xr
            buf[pl.ds(r0, SUBLANES), n:] = xi
            return xr[SUBLANES - 1:, :], xi[SUBLANES - 1:, :]

        cr, ci = lax.fori_loop(0, T // SUBLANES, tile, (car[:, :n], car[:, n:]))
        car[:, :n] = cr
        car[:, n:] = ci
        xs = buf[...]
        xs_ref[...] = xs
        y = jnp.dot(xs.astype(BF16), c_ref[0], preferred_element_type=F32) + d_ref[...] * uv
        y_ref[...] = y
        g_ref[...] = _gelu(y).astype(g_ref.dtype)

    blk = pl.BlockSpec((T, LANES), lambda j, i: (i, j))
    return carrier_call(
        body, name, (nblk, L // T),
        [blk, pl.BlockSpec((1, LANES, ns), lambda j, i: (j, 0, 0)),
         pl.BlockSpec((1, ns, LANES), lambda j, i: (j, 0, 0)),
         pl.BlockSpec((1, 4, SUBLANES, ns), lambda j, i: (j, 0, 0, 0)),
         pl.BlockSpec((1, LANES), lambda j, i: (0, j))],
        [blk, blk, pl.BlockSpec((T, ns), lambda j, i: (i, j))],
        [jax.ShapeDtypeStruct((L, D), F32), jax.ShapeDtypeStruct((L, D), BF16),
         jax.ShapeDtypeStruct((L, nblk * ns), F32)],
        [pltpu.VMEM((T, ns), F32), pltpu.VMEM((1, ns), F32)],
        (u, bmat, cmat, tab, dvec), comm, ("parallel", "arbitrary"))


def s5_bwd(dg, y, u, xs, cmat_t, bmat_t, rtab, dvec, name, comm=None):
    L, D = u.shape
    nblk = D // LANES
    T = _div_tile(L, (ROW_TILE,))
    nch = L // T
    ns = 2 * S5_BLOCK_STATES
    n = S5_BLOCK_STATES
    ntile = T // SUBLANES

    def body(dg_ref, y_ref, u_ref, xs_ref, xp_ref, ct_ref, bt_ref, tab_ref, d_ref,
             du_ref, dc_ref, db_ref, dlam_ref, dd_ref, buf, xbuf, car):
        step = pl.program_id(1)
        first_chunk = step == nch - 1

        @pl.when(step == 0)
        def _():
            car[...] = jnp.zeros_like(car)
            dc_ref[...] = jnp.zeros_like(dc_ref)
            db_ref[...] = jnp.zeros_like(db_ref)
            dlam_ref[...] = jnp.zeros_like(dlam_ref)
            dd_ref[...] = jnp.zeros_like(dd_ref)

        uv = u_ref[...]
        dy = dg_ref[...].astype(F32) * _gelu_grad(y_ref[...])
        dd_ref[...] += jnp.sum(dy * uv, axis=0, keepdims=True)
        dyb = dy.astype(BF16)
        buf[...] = jnp.dot(dyb, ct_ref[0], preferred_element_type=F32)
        xs = xs_ref[...]
        xbuf[pl.ds(SUBLANES, T), :] = xs
        xbuf[pl.ds(0, SUBLANES), :] = jnp.where(first_chunk, 0.0, xp_ref[...])
        row0 = lax.broadcasted_iota(jnp.int32, (SUBLANES, n), 0) == 0

        def tile(ii, carry):
            cr, ci, ar, ai = carry
            r0 = pl.multiple_of((ntile - 1 - ii) * SUBLANES, SUBLANES)
            xr, xi = _scan_tile(buf[pl.ds(r0, SUBLANES), :n], buf[pl.ds(r0, SUBLANES), n:], cr, ci, tab_ref, True)
            buf[pl.ds(r0, SUBLANES), :n] = xr
            buf[pl.ds(r0, SUBLANES), n:] = xi
            r1 = pl.multiple_of(r0 + SUBLANES, SUBLANES)
            pr = jnp.where(row0, xbuf[pl.ds(r0, SUBLANES), :n][SUBLANES - 1:, :],
                           pltpu.roll(xbuf[pl.ds(r1, SUBLANES), :n], 1, 0))
            pi = jnp.where(row0, xbuf[pl.ds(r0, SUBLANES), n:][SUBLANES - 1:, :],
                           pltpu.roll(xbuf[pl.ds(r1, SUBLANES), n:], 1, 0))
            ar = ar + xr * pr + xi * pi
            ai = ai + xi * pr - xr * pi
            return xr[0:1, :], xi[0:1, :], ar, ai

        zero = jnp.zeros((SUBLANES, n), F32)
        cr, ci, ar, ai = lax.fori_loop(0, ntile, tile, (car[:, :n], car[:, n:], zero, zero))
        car[:, :n] = cr
        car[:, n:] = ci
        dlam_ref[0, :, :n] += ar
        dlam_ref[0, :, n:] += ai
        dxa = buf[...]
        dc_ref[0] += lax.dot_general(dyb, xs.astype(BF16), _DIMS["tn"], preferred_element_type=F32)
        dxb = dxa.astype(BF16)
        db_ref[0] += lax.dot_general(uv.astype(BF16), dxb, _DIMS["tn"], preferred_element_type=F32)
        du_ref[...] = jnp.dot(dxb, bt_ref[0], preferred_element_type=F32) + d_ref[...] * dy

    rev = lambda j, i: (nch - 1 - i, j)
    blk = pl.BlockSpec((T, LANES), rev)
    tpb = T // SUBLANES
    acc = pl.BlockSpec((1, LANES, ns), lambda j, i: (j, 0, 0))
    return carrier_call(
        body, name, (nblk, nch),
        [blk, blk, blk, pl.BlockSpec((T, ns), rev),
         pl.BlockSpec((SUBLANES, ns), lambda j, i: (jnp.maximum((nch - 1 - i) * tpb - 1, 0), j)),
         pl.BlockSpec((1, LANES, ns), lambda j, i: (j, 0, 0)),
         pl.BlockSpec((1, ns, LANES), lambda j, i: (j, 0, 0)),
         pl.BlockSpec((1, 4, SUBLANES, ns), lambda j, i: (j, 0, 0, 0)),
         pl.BlockSpec((1, LANES), lambda j, i: (0, j))],
        [blk, acc, acc, pl.BlockSpec((1, SUBLANES, ns), lambda j, i: (j, 0, 0)),
         pl.BlockSpec((1, LANES), lambda j, i: (0, j))],
        [jax.ShapeDtypeStruct((L, D), F32), jax.ShapeDtypeStruct((nblk, LANES, ns), F32),
         jax.ShapeDtypeStruct((nblk, LANES, ns), F32), jax.ShapeDtypeStruct((nblk, SUBLANES, ns), F32),
         jax.ShapeDtypeStruct((1, D), F32)],
        [pltpu.VMEM((T, ns), F32), pltpu.VMEM((T + SUBLANES, ns), F32), pltpu.VMEM((1, ns), F32)],
        (dg, y, u, xs, xs, cmat_t, bmat_t, rtab, dvec), comm, ("parallel", "arbitrary"))


def _s5_discretize(a_re, a_im, log_step, b_re, b_im):
    lam_re = jnp.minimum(a_re, -S5_MIN_DECAY)
    lam_im = a_im
    delta = jnp.exp(log_step)[:, None]
    mag = jnp.exp(lam_re * delta)
    ang = lam_im * delta
    lb_re = mag * jnp.cos(ang)
    lb_im = mag * jnp.sin(ang)
    den = lam_re * lam_re + lam_im * lam_im
    nr = lb_re - 1.0
    ni = lb_im
    coef_re = (nr * lam_re + ni * lam_im) / den
    coef_im = (ni * lam_re - nr * lam_im) / den
    bb_re = coef_re[..., None] * b_re - coef_im[..., None] * b_im
    bb_im = coef_re[..., None] * b_im + coef_im[..., None] * b_re
    return lb_re, lb_im, bb_re, bb_im


def _s5_tables(lb_re, lb_im):
    nblk = lb_re.shape[0] // S5_BLOCK_GROUPS
    lr = lb_re.reshape(nblk, S5_BLOCK_STATES)
    li = lb_im.reshape(nblk, S5_BLOCK_STATES)
    pows = [(jnp.ones_like(lr), jnp.zeros_like(li))]
    for _ in range(SUBLANES):
        pr, pi = pows[-1]
        pows.append((pr * lr - pi * li, pr * li + pi * lr))
    rows = jnp.arange(SUBLANES)[None, :, None]

    def table(conj, reverse):
        sgn = -1.0 if conj else 1.0
        out = []
        for k in (1, 2, 4):
            mask = (rows <= SUBLANES - 1 - k) if reverse else (rows >= k)
            out.append(jnp.concatenate([jnp.where(mask, pows[k][0][:, None, :], 0.0),
                                        jnp.where(mask, sgn * pows[k][1][:, None, :], 0.0)], axis=-1))
        order = range(SUBLANES, 0, -1) if reverse else range(1, SUBLANES + 1)
        cre = jnp.stack([pows[k][0] for k in order], axis=1)
        cim = jnp.stack([sgn * pows[k][1] for k in order], axis=1)
        out.append(jnp.concatenate([cre, cim], axis=-1))
        return jnp.stack(out, axis=1)

    return table(False, False), table(True, True)


def _s5_block_mats(bb_re, bb_im, c_re, c_im):
    G = bb_re.shape[0]
    nblk = G // S5_BLOCK_GROUPS
    eye = jnp.eye(S5_BLOCK_GROUPS, dtype=F32)
    bb = jnp.stack([bb_re, bb_im]).reshape(2, nblk, S5_BLOCK_GROUPS, S5_STATE, S5_GROUP)
    bmat = jnp.einsum("ab,rjaph->jahrbp", eye, bb).reshape(nblk, LANES, 2 * S5_BLOCK_STATES)
    cc = jnp.stack([c_re, -c_im]).reshape(2, nblk, S5_BLOCK_GROUPS, S5_GROUP, S5_STATE)
    cmat = jnp.einsum("ab,rjahp->jrbpah", eye, cc).reshape(nblk, 2 * S5_BLOCK_STATES, LANES)
    return bmat, cmat


def _s5_unblock(dB, dC, dlam):
    nblk = dB.shape[0]
    G = nblk * S5_BLOCK_GROUPS
    d6 = dB.reshape(nblk, S5_BLOCK_GROUPS, S5_GROUP, 2, S5_BLOCK_GROUPS, S5_STATE)
    dbb = jnp.einsum("jahrap->rjaph", d6).reshape(2, G, S5_STATE, S5_GROUP)
    c6 = dC.reshape(nblk, S5_BLOCK_GROUPS, S5_GROUP, 2, S5_BLOCK_GROUPS, S5_STATE)
    dcc = jnp.einsum("jahrap->rjahp", c6).reshape(2, G, S5_GROUP, S5_STATE)
    dl = jnp.sum(dlam, axis=1).reshape(nblk, 2, S5_BLOCK_GROUPS, S5_STATE)
    dl = jnp.transpose(dl, (1, 0, 2, 3)).reshape(2, G, S5_STATE)
    return dbb[0], dbb[1], dcc[0], -dcc[1], dl[0], dl[1]


def loss_and_grad(y, target, name):
    L, D = y.shape
    tr = _div_tile(L, (512, 256, 128))

    def body(y_ref, t_ref, dy_ref, loss_ref):
        err = y_ref[...] - t_ref[...]
        dy_ref[...] = err * (1.0 / D)
        part = 0.5 * jnp.sum(jnp.mean(err * err, axis=-1, keepdims=True), axis=0, keepdims=True)

        @pl.when(pl.program_id(0) == 0)
        def _():
            loss_ref[...] = part

        @pl.when(pl.program_id(0) > 0)
        def _():
            loss_ref[...] += part

    row = pl.BlockSpec((tr, D), lambda i: (i, 0))
    return pl.pallas_call(
        body, name=name, grid=(L // tr,), in_specs=[row, row],
        out_specs=[row, pl.BlockSpec((1, 1), lambda i: (0, 0))],
        out_shape=[jax.ShapeDtypeStruct((L, D), F32), jax.ShapeDtypeStruct((1, 1), F32)],
        compiler_params=_params(("arbitrary",)),
    )(y, target)


def _adam_math(w, g, m, v):
    m = ADAM_B1 * m + (1.0 - ADAM_B1) * g
    v = ADAM_B2 * v + (1.0 - ADAM_B2) * (g * g)
    m_hat = m / (1.0 - ADAM_B1 ** ADAM_STEP)
    v_hat = v / (1.0 - ADAM_B2 ** ADAM_STEP)
    delta = -ADAM_LR * (m_hat / (jnp.sqrt(v_hat) + ADAM_EPS) + ADAM_WD * w)
    return delta, m, v


def _as3d(a):
    return a.reshape((-1,) + a.shape[-2:])


def adamw(w, g, m, v, name):
    shape = w.shape
    w3, g3, m3, v3 = _as3d(w), _as3d(g), _as3d(m), _as3d(v)
    A, R, C = w3.shape
    tr = _row_tile(R, C)

    def body(w_ref, g_ref, m_ref, v_ref, d_ref, mo_ref, vo_ref):
        d, mn, vn = _adam_math(w_ref[...], g_ref[...], m_ref[...], v_ref[...])
        d_ref[...] = d
        mo_ref[...] = mn
        vo_ref[...] = vn

    spec = pl.BlockSpec((1, tr, C), lambda a, i: (a, i, 0))
    outs = pl.pallas_call(
        body, name=name, grid=(A, R // tr), in_specs=[spec] * 4, out_specs=[spec] * 3,
        out_shape=[jax.ShapeDtypeStruct((A, R, C), F32)] * 3,
        compiler_params=_params(("parallel", "parallel")),
    )(w3, g3, m3, v3)
    return [o.reshape(shape) for o in outs]


def _place():
    x, y, c = lax.axis_index("x"), lax.axis_index("y"), lax.axis_index("c")
    other_chips = [(1 - x, y), (x, 1 - y), (1 - x, 1 - y)]
    return x, y, c, other_chips


def _chip_id(chip):
    return 2 * chip[0] + chip[1]


def _my_chip():
    return 2 * lax.axis_index("x") + lax.axis_index("y")


def _remote(src, dst, send_sem, recv_sem, dev):
    return pltpu.make_async_remote_copy(src_ref=src, dst_ref=dst, send_sem=send_sem, recv_sem=recv_sem,
                                        device_id=dev, device_id_type=MESH)


def allgather_small(arrs):
    T = len(arrs)

    def body(*refs):
        ins, outs = refs[:T], refs[T:2 * T]
        send_sems, recv_sems = refs[2 * T:]
        x, y, c, chips = _place()
        me = _chip_id((x, y))
        sends = []
        for t in range(T):
            outs[t][me] = ins[t][...]
            for j, chip in enumerate(chips):
                cp = _remote(ins[t], outs[t].at[me], send_sems.at[3 * t + j], recv_sems.at[3 * t + j], (*chip, c))
                cp.start()
                sends.append(cp)
        for t in range(T):
            for j, chip in enumerate(chips):
                slot = outs[t].at[_chip_id(chip)]
                _remote(slot, slot, send_sems.at[3 * t + j], recv_sems.at[3 * t + j], (*chip, c)).wait_recv()
        for cp in sends:
            cp.wait_send()

    return pl.pallas_call(
        body, name="allgather_small", in_specs=[VMEM_SPEC] * T, out_specs=[VMEM_SPEC] * T,
        out_shape=[jax.ShapeDtypeStruct((4,) + a.shape, a.dtype) for a in arrs],
        scratch_shapes=[pltpu.SemaphoreType.DMA((3 * T,)), pltpu.SemaphoreType.DMA((3 * T,))],
        compiler_params=pltpu.CompilerParams(vmem_limit_bytes=VMEM_LIMIT, has_side_effects=True),
    )(*arrs)


def small_swap_comm(buf):
    def copy(ins, outs, send_sems, recv_sems):
        x, y, c, _ = _place()
        return _remote(ins[0], outs[0], send_sems.at[0], recv_sems.at[0], (x, y, 1 - c))

    return Comm([buf], [jax.ShapeDtypeStruct(buf.shape, F32)], {}, 1,
                lambda *refs: copy(*refs).start(), None, lambda *refs: copy(*refs).wait())


def small_exchange_comm(pair):
    def copies(ins, outs, send_sems, recv_sems):
        x, y, c, chips = _place()
        return [_remote(ins[0], outs[0].at[j], send_sems.at[j], recv_sems.at[j], (*chip, c))
                for j, chip in enumerate(chips)]

    def begin(*refs):
        for cp in copies(*refs):
            cp.start()

    def finish(*refs):
        for cp in copies(*refs):
            cp.wait()

    return Comm([pair], [jax.ShapeDtypeStruct((3,) + pair.shape, F32)], {}, 3, begin, None, finish)


def small_pair_sum(mine, theirs, name):
    R, C = mine.shape
    tr = _row_tile(R, C)

    def body(a_ref, b_ref, o_ref):
        o_ref[...] = a_ref[...] + b_ref[...]

    spec = pl.BlockSpec((tr, C), lambda i: (i, 0))
    return pl.pallas_call(body, name=name, grid=(R // tr,), in_specs=[spec, spec], out_specs=spec,
                          out_shape=jax.ShapeDtypeStruct((R, C), F32), compiler_params=_params(("parallel",)))(mine, theirs)


def small_chip_sum(pair, got, name):
    R, C = pair.shape
    tr = _row_tile(R, C)

    def body(p_ref, g_ref, o_ref):
        me = _my_chip()
        terms = []
        for chip in range(4):
            d = jnp.bitwise_xor(me, chip)
            terms.append(jnp.where(d == 0, p_ref[...],
                                   jnp.where(d == 2, g_ref[0], jnp.where(d == 1, g_ref[1], g_ref[2]))))
        o_ref[...] = ((terms[0] + terms[1]) + terms[2]) + terms[3]

    spec = pl.BlockSpec((tr, C), lambda i: (i, 0))
    return pl.pallas_call(body, name=name, grid=(R // tr,),
                          in_specs=[spec, pl.BlockSpec((3, tr, C), lambda i: (0, i, 0))], out_specs=spec,
                          out_shape=jax.ShapeDtypeStruct((R, C), F32), compiler_params=_params(("parallel",)))(pair, got)


def _half_rows(ref, layout, shard, half):
    if layout == "S":
        hr = ref.shape[1] // 2
        return ref.at[shard, pl.ds(pl.multiple_of(half * hr, 16), hr), :]
    hr, C = ref.shape[0] // 2, ref.shape[1] // 4
    return ref.at[pl.ds(pl.multiple_of(half * hr, 16), hr), pl.ds(pl.multiple_of(shard * C, LANES), C)]


def _half_rows_all(ref, layout, half):
    if layout == "S":
        hr = ref.shape[1] // 2
        return ref.at[:, pl.ds(pl.multiple_of(half * hr, 16), hr), :]
    hr = ref.shape[0] // 2
    return ref.at[pl.ds(pl.multiple_of(half * hr, 16), hr), :]


def _shard_of_half(ref, layout, shard):
    if layout == "S":
        return ref.at[shard]
    C = ref.shape[1] // 4
    return ref.at[:, pl.ds(pl.multiple_of(shard * C, LANES), C)]


def _own_block_spec(layout, tr, C):
    if layout == "S":
        return pl.BlockSpec((None, tr, C), lambda i: (_my_chip(), i, 0))
    return pl.BlockSpec((tr, C), lambda i: (i, _my_chip()))


def cast_into_gathered(shards, layer, layout, name):
    _, R, C = shards.shape
    tr = _row_tile(R, C)

    def body(a_ref, o_ref):
        o_ref[...] = a_ref[...].astype(BF16)

    return pl.pallas_call(
        body, name=name, grid=(R // tr,),
        in_specs=[pl.BlockSpec((None, tr, C), lambda i: (layer, i, 0))],
        out_specs=_own_block_spec(layout, tr, C),
        out_shape=jax.ShapeDtypeStruct((4, R, C) if layout == "S" else (R, 4 * C), BF16),
        compiler_params=_params(("parallel",)),
    )(shards)


def gather_comm(bufs, layouts):
    T = len(bufs)

    def begin(_, outs, send_sems, recv_sems):
        x, y, c, chips = _place()
        for t in range(T):
            mine = _half_rows(outs[t], layouts[t], _chip_id((x, y)), c)
            for j, chip in enumerate(chips):
                _remote(mine, mine, send_sems.at[6 * t + j], recv_sems.at[6 * t + j], (*chip, c)).start()

    def middle(_, outs, send_sems, recv_sems):
        x, y, c, chips = _place()
        for t in range(T):
            for j, chip in enumerate(chips):
                piece = _half_rows(outs[t], layouts[t], _chip_id(chip), c)
                _remote(piece, piece, send_sems.at[6 * t + j], recv_sems.at[6 * t + j], (*chip, c)).wait_recv()
                _remote(piece, piece, send_sems.at[6 * t + 3 + j], recv_sems.at[6 * t + 3 + j], (x, y, 1 - c)).start()

    def finish(_, outs, send_sems, recv_sems):
        x, y, c, chips = _place()
        for t in range(T):
            mine = _half_rows(outs[t], layouts[t], _chip_id((x, y)), c)
            for j, chip in enumerate(chips):
                theirs = _half_rows(outs[t], layouts[t], _chip_id(chip), 1 - c)
                _remote(theirs, theirs, send_sems.at[6 * t + 3 + j], recv_sems.at[6 * t + 3 + j],
                        (x, y, 1 - c)).wait_recv()
                _remote(mine, mine, send_sems.at[6 * t + j], recv_sems.at[6 * t + j], (*chip, c)).wait_send()
                piece = _half_rows(outs[t], layouts[t], _chip_id(chip), c)
                _remote(piece, piece, send_sems.at[6 * t + 3 + j], recv_sems.at[6 * t + 3 + j],
                        (x, y, 1 - c)).wait_send()

    return Comm(bufs, [jax.ShapeDtypeStruct(b.shape, b.dtype) for b in bufs], {t: t for t in range(T)}, 6 * T,
                begin, middle, finish, middle_frac=0.75)


def swap_comm(grads, layouts):
    T = len(grads)

    def out_shape(g, layout):
        return (4, g.shape[1] // 2, g.shape[2]) if layout == "S" else (g.shape[0] // 2, g.shape[1])

    def copies(ins, outs, send_sems, recv_sems):
        x, y, c, _ = _place()
        return [_remote(_half_rows_all(ins[t], layouts[t], 1 - c), outs[t], send_sems.at[t], recv_sems.at[t],
                        (x, y, 1 - c)) for t in range(T)]

    def begin(*refs):
        for cp in copies(*refs):
            cp.start()

    def finish(*refs):
        for cp in copies(*refs):
            cp.wait()

    return Comm(grads, [jax.ShapeDtypeStruct(out_shape(g, k), F32) for g, k in zip(grads, layouts)], {}, T,
                begin, None, finish)


def pair_sum(grad, recv, layout, name):
    if layout == "S":
        _, hr, C = recv.shape
        tr = _row_tile(hr, C)
        nb = hr // tr
        grid = (4, nb)
        g_spec = pl.BlockSpec((None, tr, C), lambda a, i: (a, lax.axis_index("c") * nb + i, 0))
        spec = pl.BlockSpec((None, tr, C), lambda a, i: (a, i, 0))
    else:
        hr, C = recv.shape
        tr = _row_tile(hr, C)
        nb = hr // tr
        grid = (nb,)
        g_spec = pl.BlockSpec((tr, C), lambda i: (lax.axis_index("c") * nb + i, 0))
        spec = pl.BlockSpec((tr, C), lambda i: (i, 0))

    def body(g_ref, r_ref, f_ref, b_ref):
        s = g_ref[...] + r_ref[...]
        f_ref[...] = s
        b_ref[...] = s.astype(BF16)

    return pl.pallas_call(
        body, name=name, grid=grid, in_specs=[g_spec, spec], out_specs=[spec, spec],
        out_shape=[jax.ShapeDtypeStruct(recv.shape, F32), jax.ShapeDtypeStruct(recv.shape, BF16)],
        compiler_params=_params(("parallel",) * len(grid)),
    )(grad, recv)


def exchange_comm(pair_bf16, layouts):
    T = len(pair_bf16)

    def out_shape(p, layout):
        return (3,) + ((p.shape[1], p.shape[2]) if layout == "S" else (p.shape[0], p.shape[1] // 4))

    def copies(ins, outs, send_sems, recv_sems):
        x, y, c, chips = _place()
        return [_remote(_shard_of_half(ins[t], layouts[t], _chip_id(chip)), outs[t].at[j],
                        send_sems.at[3 * t + j], recv_sems.at[3 * t + j], (*chip, c))
                for t in range(T) for j, chip in enumerate(chips)]

    def begin(*refs):
        for cp in copies(*refs):
            cp.start()

    def finish(*refs):
        for cp in copies(*refs):
            cp.wait()

    return Comm(pair_bf16, [jax.ShapeDtypeStruct(out_shape(p, k), BF16) for p, k in zip(pair_bf16, layouts)], {},
                3 * T, begin, None, finish)


def reduce_total(pair_f32, got, layout, layer, n_layers, previous, name):
    _, hr, C = got.shape
    tr = _row_tile(hr, C)
    nb = hr // tr

    def body(*refs):
        p_ref, g_ref, t_ref = refs[0], refs[1], refs[-1]
        t_ref[...] = ((p_ref[...] + g_ref[0].astype(F32)) + g_ref[1].astype(F32)) + g_ref[2].astype(F32)

    args = [pair_f32, got] + ([previous] if previous is not None else [])
    return pl.pallas_call(
        body, name=name, grid=(nb,),
        in_specs=[_own_block_spec(layout, tr, C), pl.BlockSpec((3, tr, C), lambda i: (0, i, 0))]
        + ([ANY] if previous is not None else []),
        out_specs=pl.BlockSpec((None, tr, C), lambda i: (layer, lax.axis_index("c") * nb + i, 0)),
        out_shape=jax.ShapeDtypeStruct((n_layers, 2 * hr, C), F32),
        input_output_aliases={2: 0} if previous is not None else {},
        compiler_params=_params(("parallel",)),
    )(*args)


def share_comm(reduced):
    T = len(reduced)

    def halves(outs, half):
        return [o.at[:, pl.ds(pl.multiple_of(half * (o.shape[1] // 2), 8), o.shape[1] // 2), :] for o in outs]

    def begin(_, outs, send_sems, recv_sems):
        x, y, c, _p = _place()
        for t, mine in enumerate(halves(outs, c)):
            _remote(mine, mine, send_sems.at[t], recv_sems.at[t], (x, y, 1 - c)).start()

    def finish(_, outs, send_sems, recv_sems):
        x, y, c, _p = _place()
        for t, (mine, theirs) in enumerate(zip(halves(outs, c), halves(outs, 1 - c))):
            _remote(mine, mine, send_sems.at[t], recv_sems.at[t], (x, y, 1 - c)).wait_send()
            _remote(theirs, theirs, send_sems.at[t], recv_sems.at[t], (x, y, 1 - c)).wait_recv()

    return Comm(reduced, [jax.ShapeDtypeStruct(r.shape, r.dtype) for r in reduced], {t: t for t in range(T)}, T,
                begin, None, finish)


def _round_up(n, m):
    return (n + m - 1) // m * m


def _heads_col(a16):
    L = a16.shape[0]
    return jnp.transpose(a16.reshape(L, HEADS // 2, 2), (1, 0, 2))


def _heads_row(a16):
    L = a16.shape[0]
    return jnp.transpose(a16.reshape(L, HEADS // 2, 2), (1, 2, 0))


def local_step(x, target, meta, norm_g, b_f, conv_w, conv_b, s5, s5_d, stager):
    S, D = x.shape
    depth = norm_g.shape[0]
    n_even, n_odd = b_f.shape[0], s5_d.shape[0]
    L = N_META + S
    Lp = _round_up(L, ROW_TILE)
    h = jnp.concatenate([meta, x, jnp.zeros((Lp - L, D), F32)], axis=0)
    conv_c = conv_w.shape[2]
    fg_block = 3 * conv_c // LANES
    saved = []

    def riding(tag, fn, *args):
        comm = stager.ride(tag)
        if comm is None and fn is matmul:
            return fn(*args, name=tag)
        outs, arrived = fn(*args, name=tag, comm=comm)
        stager.arrived(tag, arrived)
        return outs[0] if fn is matmul else outs

    for i in range(depth):
        g = norm_g[i]
        j = i // 2
        tag = "l%d_" % i
        w = stager.weights(i)
        st = {"h0": h, "w": w}
        if i % 2 == 0:
            u = u_next if i else rmsnorm_fwd(h, g[0:1], BF16, tag + "norm0")
            qkv = matmul([(u, w["w_qkv"])], "nn", BF16, tag + "qkv")
            rest = matmul([(u, w["w_rest"])], "nn", F32, tag + "rest")
            cgate = gate_fwd(rest, fg_block, b_f[j], tag + "gate")
            c16 = cgate[:, :HEADS]
            attn, lse = riding(tag + "attn", attn_fwd, qkv, _heads_col(c16), _heads_row(c16))
            convo = conv_fwd(rest, 0, conv_w[j], conv_b[j], tag + "conv")
            cat = jnp.concatenate([attn, convo], axis=1)
            m = matmul([(cat, w["w_o"])], "nn", F32, tag + "wo")
            st.update(u=u, qkv=qkv, rest=rest, c16=c16, lse=lse, cat=cat)
        else:
            p = s5[j]
            u = u_next if i else rmsnorm_fwd(h, g[0:1], F32, tag + "norm0")
            y, gact, xs = riding(tag + "s5", s5_fwd, u, p["bmat"], p["cmat"], p["tab"], s5_d[j])
            o1, o2, m = riding(tag + "glu", dual_matmul_act, gact, w["w_glu1"], w["w_glu2"], "glu", F32)
            st.update(u=u, y=y, gact=gact, xs=xs, o1=o1, o2=o2)
        h1, u2 = rmsnorm_residual_norm(m, g[1:2], h, g[2:3], BF16, tag + "norm1_2")
        a, b, hact = riding(tag + "ffn_in", dual_matmul_act, u2, w["w_gate"], w["w_up"], "swiglu", BF16)
        f = riding(tag + "ffn_out", matmul, [(hact, w["w_down"])], "nn", F32)
        if i + 1 < depth:
            h, u_next = rmsnorm_residual_norm(f, g[3:4], h1, norm_g[i + 1][0:1], F32 if i % 2 == 0 else BF16,
                                              tag + "norm3_0")
        else:
            h = rmsnorm_fwd(f, g[3:4], F32, tag + "norm3", residual=h1)
        st.update(m=m, h1=h1, u2=u2, a=a, b=b, hact=hact, f=f)
        saved.append(st)

    dy, loss = loss_and_grad(h[N_META:L], target, "loss")
    dh = jnp.concatenate([jnp.zeros((N_META, D), F32), dy, jnp.zeros((Lp - L, D), F32)], axis=0)

    grads = {k: [None] * n_even for k in ("b_f", "conv_w", "conv_b")}
    grads.update({k: [None] * n_odd for k in ("s5_d", "s5_dB", "s5_dC", "s5_dlam")})
    grads["norm_g"] = [None] * depth

    for i in reversed(range(depth)):
        g = norm_g[i]
        j = i // 2
        tag = "l%d_b_" % i
        st = saved[i]
        w = st["w"]
        wg = {}
        df, dg3 = rmsnorm_bwd(st["f"], g[3:4], dh, BF16, tag + "norm3")
        wg["w_down"] = matmul([(st["hact"], df)], "tn", F32, tag + "dw_down")
        da, db = riding(tag + "ffn_act", ffn_bwd_act, df, w["w_down"], st["a"], st["b"])
        u2t = st["u2"].T
        wg["w_gate"] = matmul([(u2t, da)], "nn", F32, tag + "dw_gate")
        wg["w_up"] = matmul([(u2t, db)], "nn", F32, tag + "dw_up")
        stager.put_grads(i, wg)
        wg = {}
        du2 = riding(tag + "du2", matmul, [(da, w["w_gate"])], "nt", F32)
        du2 = matmul([(db, w["w_up"])], "nt", F32, tag + "du2_up", add=du2)
        dh1, dg2 = rmsnorm_bwd(st["h1"], g[2:3], du2, F32, tag + "norm2", add=dh)
        if i % 2 == 0:
            dm, dg1 = rmsnorm_bwd(st["m"], g[1:2], dh1, BF16, tag + "norm1")
            wg["w_o"] = matmul([(st["cat"], dm)], "tn", F32, tag + "dw_o")
            stager.put_grads(i, wg)
            dcat = riding(tag + "dcat", matmul, [(dm, w["w_o"])], "nt", F32)
            delta = attn_delta(dcat, st["cat"], tag + "delta")
            c16 = st["c16"]
            lse16 = jnp.transpose(st["lse"], (1, 0, 2)).reshape(Lp, HEADS)
            dq, dk, dv, dcq, dck = riding(tag + "attn", attn_bwd, st["qkv"], dcat, _heads_row(lse16),
                                          _heads_row(delta[:, :HEADS]), _heads_row(c16), _heads_col(c16))
            dc16 = (jnp.transpose(dcq, (2, 0, 1)).reshape(Lp, HEADS)
                    + jnp.transpose(dck, (1, 0, 2)).reshape(Lp, HEADS))
            dc = jnp.pad(dc16, ((0, 0), (0, LANES - HEADS)))
            dfg, dbf = gate_bwd(st["rest"], fg_block, b_f[j], dc, tag + "gate")
            dgb, dgc, dxc, dcw, dcb = conv_bwd(st["rest"], 0, conv_w[j], conv_b[j], dcat, ATTN_W // LANES,
                                               tag + "conv")
            dqkv = jnp.concatenate([dq, dk, dv], axis=1)
            drest = jnp.concatenate([dgb, dgc, dxc, dfg.astype(BF16)], axis=1)
            wg["w_qkv"] = matmul([(st["u"], dqkv)], "tn", F32, tag + "dw_qkv")
            wg["w_rest"] = matmul([(st["u"], drest)], "tn", F32, tag + "dw_rest")
            stager.put_grads(i, wg)
            du = riding(tag + "du_qkv", matmul, [(dqkv, w["w_qkv"])], "nt", F32)
            du_b = riding(tag + "du_rest", matmul, [(drest, w["w_rest"])], "nt", F32)
            grads["b_f"][j], grads["conv_w"][j], grads["conv_b"][j] = dbf, dcw, dcb
        else:
            p = s5[j]
            dmix, dg1 = rmsnorm_bwd(st["m"], g[1:2], dh1, F32, tag + "norm1")
            do1, do2 = glu_bwd_act(dmix, st["o1"], st["o2"], tag + "glu_act")
            wg["w_glu1"] = matmul([(st["gact"], do1)], "tn", F32, tag + "dw_glu1")
            wg["w_glu2"] = matmul([(st["gact"], do2)], "tn", F32, tag + "dw_glu2")
            dgact = matmul([(do1, w["w_glu1"]), (do2, w["w_glu2"])], "nt", F32, tag + "dgact")
            du, dC, dB, dlam, dd = riding(tag + "s5", s5_bwd, dgact, st["y"], st["u"], st["xs"], p["cmat_t"],
                                          p["bmat_t"], p["rtab"], s5_d[j])
            du_b = None
            grads["s5_dB"][j], grads["s5_dC"][j], grads["s5_dlam"][j], grads["s5_d"][j] = dB, dC, dlam, dd
        dh, dg0 = rmsnorm_bwd(st["h0"], g[0:1], du, F32, tag + "norm0", add=dh1, dy2=du_b)
        grads["norm_g"][i] = jnp.concatenate([dg0, dg1, dg2, dg3], axis=0)
        stager.put_grads(i, wg)

    grads["meta"] = dh[:N_META]
    return loss, dh[N_META:L], grads


def _packed_rows(shape):
    return _round_up(_round_up(math.prod(shape), LANES) // LANES, SUBLANES)


def _pack(arrs):
    rows = []
    for a in arrs:
        flat = a.reshape(-1).astype(F32)
        r = _packed_rows(a.shape)
        rows.append(jnp.pad(flat, (0, r * LANES - flat.shape[0])).reshape(r, LANES))
    return jnp.concatenate(rows, axis=0)


def _unpack(buf, shapes):
    buf = buf.reshape(-1, LANES)
    out, off = [], 0
    for s in shapes:
        r = _packed_rows(s)
        out.append(buf[off:off + r].reshape(-1)[:math.prod(s)].reshape(s))
        off += r
    return out


class _LayerWeights:
    def __init__(self, stager, layer):
        self.stager, self.layer = stager, layer

    def __getitem__(self, name):
        return self.stager.weight(self.layer, name)


class MeshStager:
    LAYOUT = {"ab_w_in": "S", "ab_w_o": "S", "s5_w_glu1": "S", "s5_w_glu2": "S",
              "ffn_w_gate": "C", "ffn_w_up": "C", "ffn_w_down": "S"}
    EVEN = ("ab_w_in", "ab_w_o", "ffn_w_gate", "ffn_w_up", "ffn_w_down")
    ODD = ("s5_w_glu1", "s5_w_glu2", "ffn_w_gate", "ffn_w_up", "ffn_w_down")

    def __init__(self, shards):
        self.shards = shards
        self.depth = depth = shards["ffn_w_gate"].shape[0]
        self.bufs = {}
        for i in range(depth):
            for k in self.keys(i):
                self.bufs[k, i] = cast_into_gathered(shards[k], self.index(k, i), self.LAYOUT[k],
                                                     "cast_%s_%d" % (k, i))
        self.grads, self.pairs, self.reduced, self.ready = {}, {}, {}, {}
        first = [("ab_w_in", 0)]
        plan = self.gather_plan = {"l0_attn": [it for it in self.stage(0) if it not in first]}
        for o in range(1, depth, 2):
            e = o - 1
            plan.setdefault("l%d_attn" % e, [])
            plan["l%d_ffn_in" % e] = [("s5_w_glu1", o), ("ffn_w_up", o)]
            plan["l%d_ffn_out" % e] = [("ffn_w_gate", o)]
            plan["l%d_s5" % o] = ([("ffn_w_down", o), ("s5_w_glu2", o)]
                                  + (self.mix(o + 1) if o + 1 < depth else []))
            if o + 1 < depth:
                plan["l%d_ffn_in" % o] = [("ffn_w_gate", o + 1)]
                plan["l%d_attn" % (o + 1)] = [("ffn_w_up", o + 1), ("ffn_w_down", o + 1)]
        self.swap_plan, self.exchange_plan = {}, {}
        for i in range(depth):
            above = self.mix(i + 1) if i + 1 < depth else []
            if above:
                self.swap_plan["l%d_b_ffn_act" % i] = above
            self.swap_plan["l%d_b_du2" % i] = self.ffn(i)
            self.exchange_plan["l%d_b_%s" % (i, "attn" if i % 2 == 0 else "s5")] = above + self.ffn(i)
        self.swap_plan["l0_b_dcat"] = [("ab_w_o", 0)]
        self.exchange_plan["l0_b_attn"].append(("ab_w_o", 0))
        self.swap_plan["l0_b_du_qkv"] = [("ab_w_in", 0)]
        self.exchange_plan["l0_b_du_rest"] = [("ab_w_in", 0)]
        self._store(first, comm_call("gather_first", self._gather(first, 0.5)))

    def keys(self, i):
        return self.EVEN if i % 2 == 0 else self.ODD

    def stage(self, i):
        return [(k, i) for k in self.keys(i)]

    def mix(self, i):
        return [(k, i) for k in self.keys(i) if not k.startswith("ffn")]

    @staticmethod
    def ffn(i):
        return [("ffn_w_gate", i), ("ffn_w_up", i), ("ffn_w_down", i)]

    @staticmethod
    def index(key, i):
        return i if key.startswith("ffn") else i // 2

    def _layouts(self, items):
        return [self.LAYOUT[k] for k, _ in items]

    def _gather(self, items, middle_frac):
        comm = gather_comm([self.bufs[it] for it in items], self._layouts(items))
        comm.middle_frac = middle_frac
        return comm

    def _store(self, items, bufs):
        for it, b in zip(items, bufs):
            self.bufs[it] = b

    def ride(self, tag):
        if tag in self.gather_plan:
            return self._gather(self.gather_plan[tag], 0.85 if tag == "l0_attn" else 0.7)
        if tag in self.swap_plan:
            items = self.swap_plan[tag]
            return swap_comm([self.grads[it] for it in items], self._layouts(items))
        if tag in self.exchange_plan:
            items = self.exchange_plan[tag]
            return exchange_comm([self.pairs[it][1] for it in items], self._layouts(items))
        return None

    def arrived(self, tag, outs):
        if tag in self.gather_plan:
            self._store(self.gather_plan[tag], outs)
        elif tag in self.swap_plan:
            self._pair_sums(self.swap_plan[tag], outs)
        elif tag in self.exchange_plan:
            self._totals(self.exchange_plan[tag], outs)

    def _pair_sums(self, items, received):
        for it, r in zip(items, received):
            self.pairs[it] = pair_sum(self.grads[it], r, self.LAYOUT[it[0]], "pair_sum_%s_%d" % it)

    def _totals(self, items, got):
        for it, g in zip(items, got):
            k, i = it
            self.reduced[k] = reduce_total(self.pairs[it][0], g, self.LAYOUT[k], self.index(k, i),
                                           self.shards[k].shape[0], self.reduced.get(k), "reduce_total_%s_%d" % it)

    def weights(self, i):
        return _LayerWeights(self, i)

    def weight(self, i, name):
        if (name, i) not in self.ready:
            if name in ("w_qkv", "w_rest"):
                b = self.bufs["ab_w_in", i]
                w_in = jnp.transpose(b, (1, 0, 2)).reshape(b.shape[1], 4 * b.shape[2])
                self.ready["w_qkv", i], self.ready["w_rest", i] = split_w_in(w_in)
            else:
                k = {"w_o": "ab_w_o", "w_glu1": "s5_w_glu1", "w_glu2": "s5_w_glu2"}.get(name, "ffn_" + name)
                b = self.bufs[k, i]
                self.ready[name, i] = b.reshape(4 * b.shape[1], b.shape[2]) if self.LAYOUT[k] == "S" else b
        return self.ready[name, i]

    def put_grads(self, i, wg):
        for k in self.keys(i):
            _, R, C = self.shards[k].shape
            name = {"ab_w_in": "w_qkv", "ab_w_o": "w_o", "s5_w_glu1": "w_glu1", "s5_w_glu2": "w_glu2"}.get(k, k[4:])
            if name not in wg:
                continue
            if k == "ab_w_in":
                dw = merge_dw_in(wg["w_qkv"], wg["w_rest"])
                self.grads[k, i] = jnp.transpose(dw.reshape(R, 4, C), (1, 0, 2))
            else:
                self.grads[k, i] = wg[name].reshape(4, R, C) if self.LAYOUT[k] == "S" else wg[name]

    def finish(self, beside):
        names = list(self.LAYOUT)
        outs = comm_call("share_reduced", merge_comms(share_comm([self.reduced[k] for k in names]), beside))
        return dict(zip(names, outs)), outs[len(names):]


def split_w_in(w_in):
    fg0 = 3 * ATTN_W
    w_rest = jnp.concatenate([w_in[:, fg0 + HEADS:], w_in[:, fg0:fg0 + HEADS],
                              jnp.zeros((w_in.shape[0], LANES - HEADS), w_in.dtype)], axis=1)
    return w_in[:, :fg0], w_rest


def merge_dw_in(dw_qkv, dw_rest):
    nqc = dw_rest.shape[1] - LANES
    return jnp.concatenate([dw_qkv, dw_rest[:, nqc:nqc + HEADS], dw_rest[:, :nqc]], axis=1)


def device_step(x, target, P, stager):
    D = x.shape[-1]
    n_even, n_odd = P["ab_b_f"].shape[0], P["s5_d"].shape[0]
    conv_c = P["ab_conv_b"].shape[1]
    b_f_pad = jnp.pad(P["ab_b_f"], ((0, 0), (0, LANES - HEADS))).reshape(n_even, 1, LANES)

    s5, s5_vjps = [], []
    for j in range(n_odd):
        disc, vjp = jax.vjp(_s5_discretize, P["s5_a_re"][j], P["s5_a_im"][j], P["s5_log_step"][j],
                            P["s5_b_re"][j], P["s5_b_im"][j])
        lb_re, lb_im, bb_re, bb_im = disc
        tab, rtab = _s5_tables(lb_re, lb_im)
        bmat, cmat = _s5_block_mats(bb_re, bb_im, P["s5_c_re"][j], P["s5_c_im"][j])
        s5.append(dict(tab=tab, rtab=rtab, bmat=bmat.astype(BF16), cmat=cmat.astype(BF16),
                       bmat_t=jnp.transpose(bmat, (0, 2, 1)).astype(BF16),
                       cmat_t=jnp.transpose(cmat, (0, 2, 1)).astype(BF16)))
        s5_vjps.append(vjp)

    loss, grad_x, G = local_step(
        x, target, P["meta_tokens"], P["norm_g"], b_f_pad, P["ab_conv_w"],
        P["ab_conv_b"].reshape(n_even, 1, conv_c), s5, P["s5_d"].reshape(n_odd, 1, D), stager)

    out = {
        "meta_tokens": G["meta"],
        "norm_g": jnp.stack(G["norm_g"]),
        "ab_b_f": jnp.stack([b[0, :HEADS] for b in G["b_f"]]),
        "ab_conv_w": jnp.stack(G["conv_w"]),
        "ab_conv_b": jnp.stack([b[0] for b in G["conv_b"]]),
        "s5_d": jnp.stack([d[0] for d in G["s5_d"]]),
    }
    s5g = {k: [] for k in ("s5_a_re", "s5_a_im", "s5_log_step", "s5_b_re", "s5_b_im", "s5_c_re", "s5_c_im")}
    for j in range(n_odd):
        dbb_re, dbb_im, dc_re, dc_im, dl_re, dl_im = _s5_unblock(G["s5_dB"][j], G["s5_dC"][j], G["s5_dlam"][j])
        da_re, da_im, dls, db_re, db_im = s5_vjps[j]((dl_re, dl_im, dbb_re, dbb_im))
        for k, val in zip(s5g, (da_re, da_im, dls, db_re, db_im, dc_re, dc_im)):
            s5g[k].append(val)
    out.update({k: jnp.stack(v) for k, v in s5g.items()})
    return loss, grad_x, out


def kernel(x, meta_tokens, norm_g, ab_w_in, ab_b_f, ab_conv_w, ab_conv_b, ab_w_o, s5_a_re, s5_a_im, s5_log_step, s5_b_re, s5_b_im, s5_c_re, s5_c_im, s5_d, s5_w_glu1, s5_w_glu2, ffn_w_gate, ffn_w_up, ffn_w_down, loss_target, m_meta_tokens, m_norm_g, m_ab_w_in, m_ab_b_f, m_ab_conv_w, m_ab_conv_b, m_ab_w_o, m_s5_a_re, m_s5_a_im, m_s5_log_step, m_s5_b_re, m_s5_b_im, m_s5_c_re, m_s5_c_im, m_s5_d, m_s5_w_glu1, m_s5_w_glu2, m_ffn_w_gate, m_ffn_w_up, m_ffn_w_down, v_meta_tokens, v_norm_g, v_ab_w_in, v_ab_b_f, v_ab_conv_w, v_ab_conv_b, v_ab_w_o, v_s5_a_re, v_s5_a_im, v_s5_log_step, v_s5_b_re, v_s5_b_im, v_s5_c_re, v_s5_c_im, v_s5_d, v_s5_w_glu1, v_s5_w_glu2, v_ffn_w_gate, v_ffn_w_up, v_ffn_w_down):
    names = ["meta_tokens", "norm_g", "ab_w_in", "ab_b_f", "ab_conv_w", "ab_conv_b", "ab_w_o", "s5_a_re", "s5_a_im",
             "s5_log_step", "s5_b_re", "s5_b_im", "s5_c_re", "s5_c_im", "s5_d", "s5_w_glu1", "s5_w_glu2",
             "ffn_w_gate", "ffn_w_up", "ffn_w_down"]
    W = dict(zip(names, [meta_tokens, norm_g, ab_w_in, ab_b_f, ab_conv_w, ab_conv_b, ab_w_o, s5_a_re, s5_a_im,
                         s5_log_step, s5_b_re, s5_b_im, s5_c_re, s5_c_im, s5_d, s5_w_glu1, s5_w_glu2,
                         ffn_w_gate, ffn_w_up, ffn_w_down]))
    Mo = dict(zip(names, [m_meta_tokens, m_norm_g, m_ab_w_in, m_ab_b_f, m_ab_conv_w, m_ab_conv_b, m_ab_w_o, m_s5_a_re,
                          m_s5_a_im, m_s5_log_step, m_s5_b_re, m_s5_b_im, m_s5_c_re, m_s5_c_im, m_s5_d, m_s5_w_glu1,
                          m_s5_w_glu2, m_ffn_w_gate, m_ffn_w_up, m_ffn_w_down]))
    Vo = dict(zip(names, [v_meta_tokens, v_norm_g, v_ab_w_in, v_ab_b_f, v_ab_conv_w, v_ab_conv_b, v_ab_w_o, v_s5_a_re,
                          v_s5_a_im, v_s5_log_step, v_s5_b_re, v_s5_b_im, v_s5_c_re, v_s5_c_im, v_s5_d, v_s5_w_glu1,
                          v_s5_w_glu2, v_ffn_w_gate, v_ffn_w_up, v_ffn_w_down]))
    D = x.shape[-1]
    n_even, n_odd, depth = ab_w_in.shape[0], s5_w_glu1.shape[0], ffn_w_gate.shape[0]
    chip = 2 * lax.axis_index("x") + lax.axis_index("y")

    big = list(MeshStager.LAYOUT)
    stager = MeshStager({k: W[k] for k in big})
    g_meta, g_norm, g_convw, g_s5d = allgather_small([meta_tokens, norm_g, ab_conv_w, s5_d])
    full = {k: W[k] for k in names if k not in big}
    full["meta_tokens"] = jnp.transpose(g_meta, (1, 0, 2)).reshape(N_META, D)
    full["norm_g"] = jnp.transpose(g_norm, (1, 2, 0, 3)).reshape(depth, 4, D)
    full["ab_conv_w"] = jnp.transpose(g_convw, (1, 2, 0, 3)).reshape(n_even, CONV_K, -1)
    full["s5_d"] = jnp.transpose(g_s5d, (1, 0, 2)).reshape(n_odd, D)

    loss, grad_x, G = device_step(x[0], loss_target[0], full, stager)
    small_w = [k for k in names if k not in big]
    small_names = ["loss"] + small_w
    G["loss"] = loss
    packed = _pack([G[k] for k in small_names])
    pair = small_pair_sum(packed, comm_call("small_swap", small_swap_comm(packed))[0], "small_pair_sum")
    reduced, (got,) = stager.finish(small_exchange_comm(pair))
    total = small_chip_sum(pair, got, "small_chip_sum")

    grad, delta, new_m, new_v = {}, {}, {}, {}
    for k in big:
        delta[k], new_m[k], new_v[k] = adamw(W[k], reduced[k], Mo[k], Vo[k], "adamw_" + k)
        grad[k] = reduced[k]
    summed = dict(zip(small_names, _unpack(total, [G[k].shape for k in small_names])))
    loss_out = summed["loss"].reshape(())
    for k in ("meta_tokens", "norm_g", "ab_conv_w", "s5_d"):
        n_last = W[k].shape[-1]
        summed[k] = lax.dynamic_slice_in_dim(summed[k], chip * n_last, n_last, axis=summed[k].ndim - 1)
    shapes = [W[k].shape for k in small_w]
    d_s, m_s, v_s = adamw(_pack([W[k] for k in small_w])[None], _pack([summed[k] for k in small_w])[None],
                          _pack([Mo[k] for k in small_w])[None], _pack([Vo[k] for k in small_w])[None], "adamw_small")
    delta.update(zip(small_w, _unpack(d_s, shapes)))
    new_m.update(zip(small_w, _unpack(m_s, shapes)))
    new_v.update(zip(small_w, _unpack(v_s, shapes)))
    grad.update({k: summed[k] for k in small_w})

    return (loss_out, grad_x[None], *[grad[k] for k in names], *[delta[k] for k in names],
            *[new_m[k] for k in names], *[new_v[k] for k in names])
```
